```python
import jax, jax.numpy as jnp
from jax import lax
import numpy as np

D_MODEL = 2048
BATCH = 8
SEQ = 2048
DEPTH = 2

D_MIX = D_MODEL
GLA_HEADS = 6
GLA_DK = 64
GLA_DV = 128
GLA_GATE_RANK = 16
GLA_GATE_TEMP = 16.0
GLA_CHUNK = 64
GLA_WIDTH = GLA_HEADS * GLA_DV
MLA_HEADS = 6
MLA_Q_LORA = 384
MLA_KV_LORA = 256
MLA_NOPE = 128
MLA_ROPE = 64
MLA_DV = 128
MLA_WIDTH = MLA_HEADS * MLA_DV
ROPE_THETA = 10000.0
Q_BLOCK = 128
CONV_CH = D_MIX - GLA_WIDTH - MLA_WIDTH
CONV_K = 3
EPS = 1e-6

IN_SPLIT_SIZES = (
    GLA_HEADS * GLA_DK,
    GLA_HEADS * GLA_DK,
    GLA_WIDTH,
    2 * GLA_GATE_RANK,
    MLA_Q_LORA,
    MLA_KV_LORA,
    MLA_ROPE,
    CONV_CH,
    CONV_CH,
    CONV_CH,
    D_MIX,
)
IN_DIM = sum(IN_SPLIT_SIZES)

kernel_name = "bidir_hybrid_gla_mla_shortconv_adaln"


def rms_norm(x, g):
    xf = x.astype(jnp.float32)
    y = xf * lax.rsqrt(jnp.mean(xf * xf, axis=-1, keepdims=True) + EPS)
    return (y * g.astype(jnp.float32)).astype(x.dtype)


def rope(x, cos, sin):
    x1, x2 = jnp.split(x.astype(jnp.float32), 2, axis=-1)
    out = jnp.concatenate([x1 * cos - x2 * sin, x2 * cos + x1 * sin], axis=-1)
    return out.astype(x.dtype)


def gla_chunked(q, k, v, log_a, strict):
    bsz, s, h, dk = q.shape
    dv = v.shape[-1]
    n = s // GLA_CHUNK

    def to_chunks(t):
        return t.reshape(bsz, n, GLA_CHUNK, h, t.shape[-1]).transpose(0, 3, 1, 2, 4).astype(jnp.float32)

    q, k, v, g = to_chunks(q), to_chunks(k), to_chunks(v), to_chunks(log_a)
    b = jnp.cumsum(g, axis=3)
    b_last = b[:, :, :, -1:, :]
    q_dec = q * jnp.exp(b)
    k_inv = k * jnp.exp(-b)
    k_to_end = k * jnp.exp(b_last - b)
    scores = jnp.einsum('bhncd,bhnjd->bhncj', q_dec, k_inv)
    mask = jnp.tril(jnp.ones((GLA_CHUNK, GLA_CHUNK), dtype=bool), k=-1 if strict else 0)
    o_intra = jnp.einsum('bhncj,bhnje->bhnce', jnp.where(mask, scores, 0.0), v)
    chunk_kv = jnp.einsum('bhncd,bhnce->bhnde', k_to_end, v)
    chunk_decay = jnp.exp(b_last[:, :, :, 0, :])

    def step(state, inp):
        decay, kv = inp
        return decay[..., None] * state + kv, state

    init = jnp.zeros((bsz, h, dk, dv), jnp.float32)
    _, s_before = lax.scan(step, init, (jnp.moveaxis(chunk_decay, 2, 0), jnp.moveaxis(chunk_kv, 2, 0)))
    s_before = jnp.moveaxis(s_before, 0, 2)
    o = o_intra + jnp.einsum('bhncd,bhnde->bhnce', q_dec, s_before)
    return o.transpose(0, 2, 3, 1, 4).reshape(bsz, s, h, dv)


def mla_attention(q_nope, q_rope, k_nope, k_rope, v):
    bsz, s, h, _ = q_nope.shape
    nb = s // Q_BLOCK
    scale = (MLA_NOPE + MLA_ROPE) ** -0.5

    def blocks(t):
        return t.reshape(bsz, nb, Q_BLOCK, *t.shape[2:]).swapaxes(0, 1)

    def attend(qb):
        qn, qr = qb
        sc = jnp.einsum('bqhd,bkhd->bhqk', qn, k_nope) + jnp.einsum('bqhr,bkr->bhqk', qr, k_rope)
        p = jax.nn.softmax(sc.astype(jnp.float32) * scale, axis=-1).astype(v.dtype)
        return jnp.einsum('bhqk,bkhd->bqhd', p, v)

    o = lax.map(attend, (blocks(q_nope), blocks(q_rope)))
    return o.swapaxes(0, 1).reshape(bsz, s, h * v.shape[-1])


def hybrid_layer(x, c_act, cos, sin, ada_w, ada_b, norm_g, w_in,
                 gla_wg_f, gla_bg_f, gla_wg_b, gla_bg_b, gla_norm_g,
                 mla_q_norm_g, mla_kv_norm_g, mla_w_uq, mla_w_ukv, mla_out_g,
                 conv_w, conv_out_g, w_out):
    bsz, s, _ = x.shape
    shift, scale, gate = jnp.split(c_act @ ada_w + ada_b, 3, axis=-1)
    h = rms_norm(x, norm_g) * (1.0 + scale[:, None, :]) + shift[:, None, :]
    proj = h @ w_in
    split_idx = [int(i) for i in np.cumsum(IN_SPLIT_SIZES)[:-1]]
    (gq, gk, gv, g_lr, mq, mkv, mkr, cb, cc, cx, z) = jnp.split(proj, split_idx, axis=-1)

    q = gq.reshape(bsz, s, GLA_HEADS, GLA_DK) * (GLA_DK ** -0.5)
    k = gk.reshape(bsz, s, GLA_HEADS, GLA_DK)
    v = gv.reshape(bsz, s, GLA_HEADS, GLA_DV)
    lr_f, lr_b = jnp.split(g_lr, 2, axis=-1)
    la_f = jax.nn.log_sigmoid((lr_f @ gla_wg_f + gla_bg_f).astype(jnp.float32)) / GLA_GATE_TEMP
    la_b = jax.nn.log_sigmoid((lr_b @ gla_wg_b + gla_bg_b).astype(jnp.float32)) / GLA_GATE_TEMP
    la_f = la_f.reshape(bsz, s, GLA_HEADS, GLA_DK)
    la_b = la_b.reshape(bsz, s, GLA_HEADS, GLA_DK)
    o_fwd = gla_chunked(q, k, v, la_f, strict=False)
    o_bwd = jnp.flip(gla_chunked(jnp.flip(q, 1), jnp.flip(k, 1), jnp.flip(v, 1), jnp.flip(la_b, 1), strict=True), 1)
    o_gla = rms_norm((o_fwd + o_bwd).astype(x.dtype), gla_norm_g).reshape(bsz, s, GLA_WIDTH)

    cq = rms_norm(mq, mla_q_norm_g)
    qm = (cq @ mla_w_uq).reshape(bsz, s, MLA_HEADS, MLA_NOPE + MLA_ROPE)
    q_nope, q_rope = qm[..., :MLA_NOPE], rope(qm[..., MLA_NOPE:], cos[:, :, None, :], sin[:, :, None, :])
    ckv = rms_norm(mkv, mla_kv_norm_g)
    kv = (ckv @ mla_w_ukv).reshape(bsz, s, MLA_HEADS, MLA_NOPE + MLA_DV)
    k_nope, v_m = kv[..., :MLA_NOPE], kv[..., MLA_NOPE:]
    k_rope = rope(mkr, cos, sin)
    o_mla = rms_norm(mla_attention(q_nope, q_rope, k_nope, k_rope, v_m), mla_out_g)

    u = cc * cx
    up = jnp.pad(u, ((0, 0), (1, 1), (0, 0)))
    conv = up[:, :-2] * conv_w[0] + up[:, 1:-1] * conv_w[1] + up[:, 2:] * conv_w[2]
    o_conv = rms_norm(cb * conv, conv_out_g)

    y = jnp.concatenate([o_gla, o_mla, o_conv], axis=-1) * jax.nn.silu(z)
    return x + gate[:, None, :] * (y @ w_out)


def _fwd_setup_inputs(seed: int = 0) -> dict:
    key = jax.random.key(seed)
    ks = jax.random.split(key, 24)
    f32 = jnp.float32

    def nrm(k, shape, std):
        return std * jax.random.normal(k, shape, f32)

    def gain(k, shape):
        return 1.0 + 0.1 * jax.random.normal(k, shape, f32)

    L = DEPTH
    offsets = jax.random.randint(ks[2], (BATCH, 1), 0, 1024, dtype=jnp.int32)
    positions = offsets + jnp.arange(SEQ, dtype=jnp.int32)[None, :]
    return {
        "x": nrm(ks[0], (BATCH, SEQ, D_MODEL), 1.0),
        "c": nrm(ks[1], (BATCH, D_MODEL), 1.0),
        "positions": positions,
        "ada_w": nrm(ks[3], (L, D_MODEL, 3 * D_MODEL), 0.5 * D_MODEL ** -0.5),
        "ada_b": nrm(ks[4], (L, 3 * D_MODEL), 0.02),
        "norm_g": gain(ks[5], (L, D_MODEL)),
        "w_in": nrm(ks[6], (L, D_MODEL, IN_DIM), D_MODEL ** -0.5),
        "gla_wg_f": nrm(ks[7], (L, GLA_GATE_RANK, GLA_HEADS * GLA_DK), GLA_GATE_RANK ** -0.5),
        "gla_bg_f": nrm(ks[8], (L, GLA_HEADS * GLA_DK), 0.1),
        "gla_wg_b": nrm(ks[9], (L, GLA_GATE_RANK, GLA_HEADS * GLA_DK), GLA_GATE_RANK ** -0.5),
        "gla_bg_b": nrm(ks[10], (L, GLA_HEADS * GLA_DK), 0.1),
        "gla_norm_g": gain(ks[11], (L, GLA_DV)),
        "mla_q_norm_g": gain(ks[12], (L, MLA_Q_LORA)),
        "mla_kv_norm_g": gain(ks[13], (L, MLA_KV_LORA)),
        "mla_w_uq": nrm(ks[14], (L, MLA_Q_LORA, MLA_HEADS * (MLA_NOPE + MLA_ROPE)), MLA_Q_LORA ** -0.5),
        "mla_w_ukv": nrm(ks[15], (L, MLA_KV_LORA, MLA_HEADS * (MLA_NOPE + MLA_DV)), MLA_KV_LORA ** -0.5),
        "mla_out_g": gain(ks[16], (L, MLA_WIDTH)),
        "conv_w": nrm(ks[17], (L, CONV_K, CONV_CH), CONV_K ** -0.5),
        "conv_out_g": gain(ks[18], (L, CONV_CH)),
        "w_out": nrm(ks[19], (L, D_MIX, D_MODEL), D_MIX ** -0.5),
        "final_g": gain(ks[20], (D_MODEL,)),
    }


def _fwd_reference(x, c, positions, ada_w, ada_b, norm_g, w_in,
              gla_wg_f, gla_bg_f, gla_wg_b, gla_bg_b, gla_norm_g,
              mla_q_norm_g, mla_kv_norm_g, mla_w_uq, mla_w_ukv, mla_out_g,
              conv_w, conv_out_g, w_out, final_g):
    inv_freq = ROPE_THETA ** (-jnp.arange(0, MLA_ROPE, 2, dtype=jnp.float32) / MLA_ROPE)
    ang = positions.astype(jnp.float32)[..., None] * inv_freq
    cos, sin = jnp.cos(ang), jnp.sin(ang)
    c_act = jax.nn.silu(c)
    h = x
    for l in range(DEPTH):
        h = hybrid_layer(h, c_act, cos, sin, ada_w[l], ada_b[l], norm_g[l], w_in[l],
                         gla_wg_f[l], gla_bg_f[l], gla_wg_b[l], gla_bg_b[l], gla_norm_g[l],
                         mla_q_norm_g[l], mla_kv_norm_g[l], mla_w_uq[l], mla_w_ukv[l], mla_out_g[l],
                         conv_w[l], conv_out_g[l], w_out[l])
    return rms_norm(h, final_g)


import jax as _jax
import jax.numpy as _jnp

TWIN_FORMAT = 'train_step'
FWD_PARAMS = ['x', 'c', 'positions', 'ada_w', 'ada_b', 'norm_g', 'w_in', 'gla_wg_f', 'gla_bg_f', 'gla_wg_b', 'gla_bg_b', 'gla_norm_g', 'mla_q_norm_g', 'mla_kv_norm_g', 'mla_w_uq', 'mla_w_ukv', 'mla_out_g', 'conv_w', 'conv_out_g', 'w_out', 'final_g']
TWIN_WEIGHTS = ['ada_w', 'ada_b', 'norm_g', 'w_in', 'gla_wg_f', 'gla_bg_f', 'gla_wg_b', 'gla_bg_b', 'gla_norm_g', 'mla_q_norm_g', 'mla_kv_norm_g', 'mla_w_uq', 'mla_w_ukv', 'mla_out_g', 'conv_w', 'conv_out_g', 'w_out', 'final_g']
TWIN_DIFF_INPUT = 'x'
TWIN_INPUTS = ['x', 'c', 'positions', 'ada_w', 'ada_b', 'norm_g', 'w_in', 'gla_wg_f', 'gla_bg_f', 'gla_wg_b', 'gla_bg_b', 'gla_norm_g', 'mla_q_norm_g', 'mla_kv_norm_g', 'mla_w_uq', 'mla_w_ukv', 'mla_out_g', 'conv_w', 'conv_out_g', 'w_out', 'final_g', 'loss_target', 'm_ada_w', 'm_ada_b', 'm_norm_g', 'm_w_in', 'm_gla_wg_f', 'm_gla_bg_f', 'm_gla_wg_b', 'm_gla_bg_b', 'm_gla_norm_g', 'm_mla_q_norm_g', 'm_mla_kv_norm_g', 'm_mla_w_uq', 'm_mla_w_ukv', 'm_mla_out_g', 'm_conv_w', 'm_conv_out_g', 'm_w_out', 'm_final_g', 'v_ada_w', 'v_ada_b', 'v_norm_g', 'v_w_in', 'v_gla_wg_f', 'v_gla_bg_f', 'v_gla_wg_b', 'v_gla_bg_b', 'v_gla_norm_g', 'v_mla_q_norm_g', 'v_mla_kv_norm_g', 'v_mla_w_uq', 'v_mla_w_ukv', 'v_mla_out_g', 'v_conv_w', 'v_conv_out_g', 'v_w_out', 'v_final_g']
TWIN_OUTPUTS = ['loss', 'grad_x', 'grad_ada_w', 'grad_ada_b', 'grad_norm_g', 'grad_w_in', 'grad_gla_wg_f', 'grad_gla_bg_f', 'grad_gla_wg_b', 'grad_gla_bg_b', 'grad_gla_norm_g', 'grad_mla_q_norm_g', 'grad_mla_kv_norm_g', 'grad_mla_w_uq', 'grad_mla_w_ukv', 'grad_mla_out_g', 'grad_conv_w', 'grad_conv_out_g', 'grad_w_out', 'grad_final_g', 'delta_ada_w', 'delta_ada_b', 'delta_norm_g', 'delta_w_in', 'delta_gla_wg_f', 'delta_gla_bg_f', 'delta_gla_wg_b', 'delta_gla_bg_b', 'delta_gla_norm_g', 'delta_mla_q_norm_g', 'delta_mla_kv_norm_g', 'delta_mla_w_uq', 'delta_mla_w_ukv', 'delta_mla_out_g', 'delta_conv_w', 'delta_conv_out_g', 'delta_w_out', 'delta_final_g', 'new_m_ada_w', 'new_m_ada_b', 'new_m_norm_g', 'new_m_w_in', 'new_m_gla_wg_f', 'new_m_gla_bg_f', 'new_m_gla_wg_b', 'new_m_gla_bg_b', 'new_m_gla_norm_g', 'new_m_mla_q_norm_g', 'new_m_mla_kv_norm_g', 'new_m_mla_w_uq', 'new_m_mla_w_ukv', 'new_m_mla_out_g', 'new_m_conv_w', 'new_m_conv_out_g', 'new_m_w_out', 'new_m_final_g', 'new_v_ada_w', 'new_v_ada_b', 'new_v_norm_g', 'new_v_w_in', 'new_v_gla_wg_f', 'new_v_gla_bg_f', 'new_v_gla_wg_b', 'new_v_gla_bg_b', 'new_v_gla_norm_g', 'new_v_mla_q_norm_g', 'new_v_mla_kv_norm_g', 'new_v_mla_w_uq', 'new_v_mla_w_ukv', 'new_v_mla_out_g', 'new_v_conv_w', 'new_v_conv_out_g', 'new_v_w_out', 'new_v_final_g']
TWIN_LEAF_KINDS = {'loss': 'loss', 'grad_x': 'grad_x', 'grad_ada_w': 'grad_w', 'grad_ada_b': 'grad_w', 'grad_norm_g': 'grad_w', 'grad_w_in': 'grad_w', 'grad_gla_wg_f': 'grad_w', 'grad_gla_bg_f': 'grad_w', 'grad_gla_wg_b': 'grad_w', 'grad_gla_bg_b': 'grad_w', 'grad_gla_norm_g': 'grad_w', 'grad_mla_q_norm_g': 'grad_w', 'grad_mla_kv_norm_g': 'grad_w', 'grad_mla_w_uq': 'grad_w', 'grad_mla_w_ukv': 'grad_w', 'grad_mla_out_g': 'grad_w', 'grad_conv_w': 'grad_w', 'grad_conv_out_g': 'grad_w', 'grad_w_out': 'grad_w', 'grad_final_g': 'grad_w', 'delta_ada_w': 'delta_w', 'delta_ada_b': 'delta_w', 'delta_norm_g': 'delta_w', 'delta_w_in': 'delta_w', 'delta_gla_wg_f': 'delta_w', 'delta_gla_bg_f': 'delta_w', 'delta_gla_wg_b': 'delta_w', 'delta_gla_bg_b': 'delta_w', 'delta_gla_norm_g': 'delta_w', 'delta_mla_q_norm_g': 'delta_w', 'delta_mla_kv_norm_g': 'delta_w', 'delta_mla_w_uq': 'delta_w', 'delta_mla_w_ukv': 'delta_w', 'delta_mla_out_g': 'delta_w', 'delta_conv_w': 'delta_w', 'delta_conv_out_g': 'delta_w', 'delta_w_out': 'delta_w', 'delta_final_g': 'delta_w', 'new_m_ada_w': 'new_m', 'new_m_ada_b': 'new_m', 'new_m_norm_g': 'new_m', 'new_m_w_in': 'new_m', 'new_m_gla_wg_f': 'new_m', 'new_m_gla_bg_f': 'new_m', 'new_m_gla_wg_b': 'new_m', 'new_m_gla_bg_b': 'new_m', 'new_m_gla_norm_g': 'new_m', 'new_m_mla_q_norm_g': 'new_m', 'new_m_mla_kv_norm_g': 'new_m', 'new_m_mla_w_uq': 'new_m', 'new_m_mla_w_ukv': 'new_m', 'new_m_mla_out_g': 'new_m', 'new_m_conv_w': 'new_m', 'new_m_conv_out_g': 'new_m', 'new_m_w_out': 'new_m', 'new_m_final_g': 'new_m', 'new_v_ada_w': 'new_v', 'new_v_ada_b': 'new_v', 'new_v_norm_g': 'new_v', 'new_v_w_in': 'new_v', 'new_v_gla_wg_f': 'new_v', 'new_v_gla_bg_f': 'new_v', 'new_v_gla_wg_b': 'new_v', 'new_v_gla_bg_b': 'new_v', 'new_v_gla_norm_g': 'new_v', 'new_v_mla_q_norm_g': 'new_v', 'new_v_mla_kv_norm_g': 'new_v', 'new_v_mla_w_uq': 'new_v', 'new_v_mla_w_ukv': 'new_v', 'new_v_mla_out_g': 'new_v', 'new_v_conv_w': 'new_v', 'new_v_conv_out_g': 'new_v', 'new_v_w_out': 'new_v', 'new_v_final_g': 'new_v'}


def _forward(args):
    return _fwd_reference(*[args[k] for k in FWD_PARAMS])


def _output_shape():
    out = _jax.eval_shape(lambda: _forward(_fwd_setup_inputs(0)))
    return out.shape, out.dtype

N_MICROBATCH = 1
ADAM_LR = 0.001
ADAM_B1 = 0.9
ADAM_B2 = 0.999
ADAM_EPS = 1e-08
ADAM_WD = 0.01
ADAM_STEP = 10
PER_EXAMPLE_BATCH_AXIS = {'x': 0, 'c': 0, 'positions': 0, 'loss_target': 0}
SHARED_INPUTS = []
_WEIGHT_DTYPES = {'ada_w': _jnp.float32, 'ada_b': _jnp.float32, 'norm_g': _jnp.float32, 'w_in': _jnp.float32, 'gla_wg_f': _jnp.float32, 'gla_bg_f': _jnp.float32, 'gla_wg_b': _jnp.float32, 'gla_bg_b': _jnp.float32, 'gla_norm_g': _jnp.float32, 'mla_q_norm_g': _jnp.float32, 'mla_kv_norm_g': _jnp.float32, 'mla_w_uq': _jnp.float32, 'mla_w_ukv': _jnp.float32, 'mla_out_g': _jnp.float32, 'conv_w': _jnp.float32, 'conv_out_g': _jnp.float32, 'w_out': _jnp.float32, 'final_g': _jnp.float32}
MOMENT_SCALE = {'ada_w': 3.361722e-02, 'ada_b': 6.233595e-02, 'norm_g': 2.209763e-02, 'w_in': 1.448315e-02, 'gla_wg_f': 2.622546e-03, 'gla_bg_f': 6.333921e-03, 'gla_wg_b': 2.501815e-03, 'gla_bg_b': 5.920201e-03, 'gla_norm_g': 3.122282e-02, 'mla_q_norm_g': 7.178550e-03, 'mla_kv_norm_g': 2.534326e-02, 'mla_w_uq': 3.976659e-03, 'mla_w_ukv': 9.797466e-03, 'mla_out_g': 1.373787e-02, 'conv_w': 1.385566e-02, 'conv_out_g': 1.470160e-02, 'w_out': 1.327713e-02, 'final_g': 8.081636e+00}


def _to_microbatches(a, axis):
    t = _jnp.moveaxis(a, axis, 0)
    t = t.reshape((N_MICROBATCH, t.shape[0] // N_MICROBATCH) + t.shape[1:])
    return _jnp.moveaxis(t, 1, axis + 1)


def setup_inputs(seed: int = 0) -> dict:
    inp = _fwd_setup_inputs(seed)
    key = _jax.random.fold_in(_jax.random.key(seed), 7919)
    shape, _ = _output_shape()
    out = dict(inp)
    out["loss_target"] = _jax.random.normal(_jax.random.fold_in(key, 0), shape, _jnp.float32)
    for i, name in enumerate(TWIN_WEIGHTS):
        w = inp[name].astype(_jnp.float32)
        if MOMENT_SCALE is None:
            s = _jnp.sqrt(_jnp.mean(_jnp.square(w)) + 1e-30)
        else:
            s = MOMENT_SCALE[name]
        km, kv = _jax.random.split(_jax.random.fold_in(key, i + 1))
        out[name] = w
        out["m_" + name] = s * _jax.random.normal(km, w.shape, _jnp.float32)
        out["v_" + name] = (s * s) * _jax.random.uniform(kv, w.shape, _jnp.float32, 0.5, 1.5)
    if N_MICROBATCH > 1:
        for name, axis in PER_EXAMPLE_BATCH_AXIS.items():
            out[name] = _to_microbatches(out[name], axis)
    return {'x': out['x'], 'c': out['c'], 'positions': out['positions'], 'ada_w': out['ada_w'], 'ada_b': out['ada_b'], 'norm_g': out['norm_g'], 'w_in': out['w_in'], 'gla_wg_f': out['gla_wg_f'], 'gla_bg_f': out['gla_bg_f'], 'gla_wg_b': out['gla_wg_b'], 'gla_bg_b': out['gla_bg_b'], 'gla_norm_g': out['gla_norm_g'], 'mla_q_norm_g': out['mla_q_norm_g'], 'mla_kv_norm_g': out['mla_kv_norm_g'], 'mla_w_uq': out['mla_w_uq'], 'mla_w_ukv': out['mla_w_ukv'], 'mla_out_g': out['mla_out_g'], 'conv_w': out['conv_w'], 'conv_out_g': out['conv_out_g'], 'w_out': out['w_out'], 'final_g': out['final_g'], 'loss_target': out['loss_target'], 'm_ada_w': out['m_ada_w'], 'm_ada_b': out['m_ada_b'], 'm_norm_g': out['m_norm_g'], 'm_w_in': out['m_w_in'], 'm_gla_wg_f': out['m_gla_wg_f'], 'm_gla_bg_f': out['m_gla_bg_f'], 'm_gla_wg_b': out['m_gla_wg_b'], 'm_gla_bg_b': out['m_gla_bg_b'], 'm_gla_norm_g': out['m_gla_norm_g'], 'm_mla_q_norm_g': out['m_mla_q_norm_g'], 'm_mla_kv_norm_g': out['m_mla_kv_norm_g'], 'm_mla_w_uq': out['m_mla_w_uq'], 'm_mla_w_ukv': out['m_mla_w_ukv'], 'm_mla_out_g': out['m_mla_out_g'], 'm_conv_w': out['m_conv_w'], 'm_conv_out_g': out['m_conv_out_g'], 'm_w_out': out['m_w_out'], 'm_final_g': out['m_final_g'], 'v_ada_w': out['v_ada_w'], 'v_ada_b': out['v_ada_b'], 'v_norm_g': out['v_norm_g'], 'v_w_in': out['v_w_in'], 'v_gla_wg_f': out['v_gla_wg_f'], 'v_gla_bg_f': out['v_gla_bg_f'], 'v_gla_wg_b': out['v_gla_wg_b'], 'v_gla_bg_b': out['v_gla_bg_b'], 'v_gla_norm_g': out['v_gla_norm_g'], 'v_mla_q_norm_g': out['v_mla_q_norm_g'], 'v_mla_kv_norm_g': out['v_mla_kv_norm_g'], 'v_mla_w_uq': out['v_mla_w_uq'], 'v_mla_w_ukv': out['v_mla_w_ukv'], 'v_mla_out_g': out['v_mla_out_g'], 'v_conv_w': out['v_conv_w'], 'v_conv_out_g': out['v_conv_out_g'], 'v_w_out': out['v_w_out'], 'v_final_g': out['v_final_g']}


def _loss(weights, diff, rest, loss_target):
    with _jax.named_scope("forward"):
        args = {**rest, TWIN_DIFF_INPUT: diff, **{k: w.astype(_WEIGHT_DTYPES[k]) for k, w in weights.items()}}
        y = _forward(args)
    with _jax.named_scope("loss_head"):
        err = _jnp.square(y.astype(_jnp.float32) - loss_target)
        return 0.5 * _jnp.sum(_jnp.mean(err, axis=-1)) if err.ndim else 0.5 * err


def _adamw(w, g, m, v):
    m = ADAM_B1 * m + (1.0 - ADAM_B1) * g
    v = ADAM_B2 * v + (1.0 - ADAM_B2) * _jnp.square(g)
    m_hat = m / (1.0 - ADAM_B1 ** ADAM_STEP)
    v_hat = v / (1.0 - ADAM_B2 ** ADAM_STEP)
    delta = -ADAM_LR * (m_hat / (_jnp.sqrt(v_hat) + ADAM_EPS) + ADAM_WD * w)
    return delta, m, v


def reference(x, c, positions, ada_w, ada_b, norm_g, w_in, gla_wg_f, gla_bg_f, gla_wg_b, gla_bg_b, gla_norm_g, mla_q_norm_g, mla_kv_norm_g, mla_w_uq, mla_w_ukv, mla_out_g, conv_w, conv_out_g, w_out, final_g, loss_target, m_ada_w, m_ada_b, m_norm_g, m_w_in, m_gla_wg_f, m_gla_bg_f, m_gla_wg_b, m_gla_bg_b, m_gla_norm_g, m_mla_q_norm_g, m_mla_kv_norm_g, m_mla_w_uq, m_mla_w_ukv, m_mla_out_g, m_conv_w, m_conv_out_g, m_w_out, m_final_g, v_ada_w, v_ada_b, v_norm_g, v_w_in, v_gla_wg_f, v_gla_bg_f, v_gla_wg_b, v_gla_bg_b, v_gla_norm_g, v_mla_q_norm_g, v_mla_kv_norm_g, v_mla_w_uq, v_mla_w_ukv, v_mla_out_g, v_conv_w, v_conv_out_g, v_w_out, v_final_g):
    given = dict(x=x, c=c, positions=positions, ada_w=ada_w, ada_b=ada_b, norm_g=norm_g, w_in=w_in, gla_wg_f=gla_wg_f, gla_bg_f=gla_bg_f, gla_wg_b=gla_wg_b, gla_bg_b=gla_bg_b, gla_norm_g=gla_norm_g, mla_q_norm_g=mla_q_norm_g, mla_kv_norm_g=mla_kv_norm_g, mla_w_uq=mla_w_uq, mla_w_ukv=mla_w_ukv, mla_out_g=mla_out_g, conv_w=conv_w, conv_out_g=conv_out_g, w_out=w_out, final_g=final_g, loss_target=loss_target, m_ada_w=m_ada_w, m_ada_b=m_ada_b, m_norm_g=m_norm_g, m_w_in=m_w_in, m_gla_wg_f=m_gla_wg_f, m_gla_bg_f=m_gla_bg_f, m_gla_wg_b=m_gla_wg_b, m_gla_bg_b=m_gla_bg_b, m_gla_norm_g=m_gla_norm_g, m_mla_q_norm_g=m_mla_q_norm_g, m_mla_kv_norm_g=m_mla_kv_norm_g, m_mla_w_uq=m_mla_w_uq, m_mla_w_ukv=m_mla_w_ukv, m_mla_out_g=m_mla_out_g, m_conv_w=m_conv_w, m_conv_out_g=m_conv_out_g, m_w_out=m_w_out, m_final_g=m_final_g, v_ada_w=v_ada_w, v_ada_b=v_ada_b, v_norm_g=v_norm_g, v_w_in=v_w_in, v_gla_wg_f=v_gla_wg_f, v_gla_bg_f=v_gla_bg_f, v_gla_wg_b=v_gla_wg_b, v_gla_bg_b=v_gla_bg_b, v_gla_norm_g=v_gla_norm_g, v_mla_q_norm_g=v_mla_q_norm_g, v_mla_kv_norm_g=v_mla_kv_norm_g, v_mla_w_uq=v_mla_w_uq, v_mla_w_ukv=v_mla_w_ukv, v_mla_out_g=v_mla_out_g, v_conv_w=v_conv_w, v_conv_out_g=v_conv_out_g, v_w_out=v_w_out, v_final_g=v_final_g)
    weights = {n: given[n] for n in TWIN_WEIGHTS}
    shared = {n: given[n] for n in SHARED_INPUTS}
    per_example = {n: given[n] for n in ['x', 'c', 'positions']}
    grad_fn = _jax.value_and_grad(_loss, argnums=(0, 1))

    def one_microbatch(ex, loss_target):
        ex = dict(ex)
        diff = ex.pop(TWIN_DIFF_INPUT)
        return grad_fn(weights, diff, {**shared, **ex}, loss_target)

    if N_MICROBATCH == 1:
        loss, (grad_w, grad_x) = one_microbatch(per_example, given["loss_target"])
    else:
        def body(carry, xs):
            loss_sum, grad_sum = carry
            l_k, (gw_k, gx_k) = one_microbatch(xs[0], xs[1])
            with _jax.named_scope("update"):
                return (loss_sum + l_k, _jax.tree.map(_jnp.add, grad_sum, gw_k)), gx_k

        init = (_jnp.zeros((), _jnp.float32), _jax.tree.map(_jnp.zeros_like, weights))
        (loss, grad_w), grad_x = _jax.lax.scan(body, init, (per_example, given["loss_target"]))
    with _jax.named_scope("update"):
        delta_w, new_m, new_v = {}, {}, {}
        for n in TWIN_WEIGHTS:
            delta_w[n], new_m[n], new_v[n] = _adamw(weights[n], grad_w[n], given["m_" + n], given["v_" + n])
    return (loss, grad_x, *[grad_w[n] for n in TWIN_WEIGHTS], *[delta_w[n] for n in TWIN_WEIGHTS],
            *[new_m[n] for n in TWIN_WEIGHTS], *[new_v[n] for n in TWIN_WEIGHTS])
```

```python
import functools

import jax
import jax.numpy as jnp
from jax import lax
from jax.experimental import pallas as pl
from jax.experimental.pallas import tpu as pltpu

F32 = jnp.float32
BF16 = jnp.bfloat16
MESH = pl.DeviceIdType.MESH
HIGHEST = lax.Precision.HIGHEST

DEPTH = 2
D_MODEL = 2048
GLA_HEADS = 6
GLA_DK = 64
GLA_DV = 128
GLA_RANK = 16
GLA_TEMP = 16.0
GLA_CHUNK = 64
GLA_W = GLA_HEADS * GLA_DV
MLA_HEADS = 6
MLA_QL = 384
MLA_KVL = 256
MLA_NOPE = 128
MLA_ROPE = 64
MLA_DV = 128
MLA_W = MLA_HEADS * MLA_DV
CONV_CH = D_MODEL - GLA_W - MLA_W
ROPE_THETA = 10000.0
EPS = 1e-6
IN_DIM = 5856
N_CHIPS = 4
N_DEV = 8

ADAM_LR = 0.001
ADAM_B1 = 0.9
ADAM_B2 = 0.999
ADAM_EPS = 1e-08
ADAM_WD = 0.01
ADAM_STEP = 10

PROJ_SEGS = (
    ("gq", 0, 384, 384), ("gk", 384, 384, 384), ("gv", 768, 768, 768), ("glr", 1536, 32, 128),
    ("mq", 1568, 384, 384), ("mkv", 1952, 256, 256), ("mkr", 2208, 64, 128),
    ("cb", 2272, 512, 512), ("cc", 2784, 512, 512), ("cx", 3296, 512, 512),
    ("pad", 3808, 0, 128), ("z", 3808, 2048, 2048),
)
PROJ_AL = sum(s[3] for s in PROJ_SEGS)

VMEM_LIMIT = 48 * 1024 * 1024
BLOCK_BYTES = 2 * 1024 * 1024


def _params(sem=None):
    return pltpu.CompilerParams(dimension_semantics=sem, vmem_limit_bytes=VMEM_LIMIT)


def _dot(a, b, ca, cb, precision=None):
    return lax.dot_general(a, b, (((ca,), (cb,)), ((), ())), preferred_element_type=F32, precision=precision)


def _tile(dim, prefs):
    for t in prefs:
        if dim % t == 0:
            return t
    return dim


def _pick_rows(rows, width, itemsize=4):
    for t in (2048, 1024, 512, 256, 128, 64, 32, 16, 8):
        if rows % t == 0 and t * width * itemsize <= BLOCK_BYTES:
            return t
    return rows


def _mm(a, b, *, ta=False, tb=False, bias=None, out_dtype=F32, name="mm"):
    if ta:
        K, M = a.shape
    else:
        M, K = a.shape
    if tb:
        N, Kb = b.shape
    else:
        Kb, N = b.shape
    assert K == Kb, (a.shape, b.shape, ta, tb)
    tm = _tile(M, (512, 256, 128))
    tn = _tile(N, (1024, 512, 384, 256, 128))
    tk = _tile(K, (512, 256, 128))
    nk = K // tk
    has_bias = bias is not None

    def body(*refs):
        if has_bias:
            a_ref, b_ref, bias_ref, o_ref, acc_ref = refs
        else:
            a_ref, b_ref, o_ref, acc_ref = refs
        k = pl.program_id(2)

        @pl.when(k == 0)
        def _():
            acc_ref[...] = jnp.zeros_like(acc_ref)

        acc_ref[...] += _dot(a_ref[...].astype(BF16), b_ref[...].astype(BF16), 0 if ta else 1, 1 if tb else 0)

        @pl.when(k == nk - 1)
        def _():
            r = acc_ref[...]
            if has_bias:
                r = r + bias_ref[...]
            o_ref[...] = r.astype(out_dtype)

    a_spec = pl.BlockSpec((tk, tm), lambda i, j, k: (k, i)) if ta else pl.BlockSpec((tm, tk), lambda i, j, k: (i, k))
    b_spec = pl.BlockSpec((tn, tk), lambda i, j, k: (j, k)) if tb else pl.BlockSpec((tk, tn), lambda i, j, k: (k, j))
    in_specs = [a_spec, b_spec]
    args = [a, b]
    if has_bias:
        in_specs.append(pl.BlockSpec((1, tn), lambda i, j, k: (0, j)))
        args.append(bias)
    return pl.pallas_call(
        body, name=name, grid=(M // tm, N // tn, nk),
        in_specs=in_specs, out_specs=pl.BlockSpec((tm, tn), lambda i, j, k: (i, j)),
        out_shape=jax.ShapeDtypeStruct((M, N), out_dtype),
        scratch_shapes=[pltpu.VMEM((tm, tn), F32)],
        compiler_params=_params(("parallel", "parallel", "arbitrary")),
    )(*args)


@jax.custom_vjp
def mm(a, b):
    return _mm(a, b, name="mm_fwd")


def _mm_f(a, b):
    return _mm(a, b, name="mm_fwd"), (a, b)


def _mm_b(res, g):
    a, b = res
    return _mm(g, b, tb=True, out_dtype=a.dtype, name="mm_da"), _mm(a, g, ta=True, out_dtype=b.dtype, name="mm_db")


mm.defvjp(_mm_f, _mm_b)


def _rows(body, name, tiled, full, tiled_out, acc_out, tr=None):
    rows = tiled[0].shape[0]
    if tr is None:
        width = max([a.shape[1] for a in tiled] + [s.shape[1] for s in tiled_out])
        tr = _pick_rows(rows, width)
    in_specs = [pl.BlockSpec((tr, a.shape[1]), lambda i: (i, 0)) for a in tiled]
    in_specs += [pl.BlockSpec(a.shape, lambda i: (0, 0)) for a in full]
    out_specs = [pl.BlockSpec((tr, s.shape[1]), lambda i: (i, 0)) for s in tiled_out]
    out_specs += [pl.BlockSpec(s.shape, lambda i: (0, 0)) for s in acc_out]

    def wrapped(*refs):
        body(pl.program_id(0), *refs)

    outs = pl.pallas_call(
        wrapped, name=name, grid=(rows // tr,), in_specs=in_specs, out_specs=out_specs,
        out_shape=list(tiled_out) + list(acc_out),
        compiler_params=_params(("arbitrary",)),
    )(*tiled, *full)
    return outs


def _sds(shape, dtype=F32):
    return jax.ShapeDtypeStruct(tuple(shape), dtype)


def _acc(step, ref, val):
    @pl.when(step == 0)
    def _():
        ref[...] = val

    @pl.when(step != 0)
    def _():
        ref[...] += val


def _colsum(v):
    return jnp.sum(v, axis=0, keepdims=True)


def _rstd(x):
    return lax.rsqrt(jnp.mean(x * x, axis=-1, keepdims=True) + EPS)


@jax.custom_vjp
def rmsnorm(x, g):
    def body(i, x_ref, g_ref, o_ref):
        x = x_ref[...]
        o_ref[...] = x * _rstd(x) * g_ref[...]
    return _rows(body, "rmsnorm_fwd", [x], [g], [_sds(x.shape)], [])[0]


def _rmsnorm_f(x, g):
    return rmsnorm(x, g), (x, g)


def _rmsnorm_b(res, dy):
    x, g = res

    def body(i, x_ref, dy_ref, g_ref, dx_ref, dg_ref):
        x = x_ref[...]
        dy = dy_ref[...]
        r = _rstd(x)
        xh = x * r
        dxh = dy * g_ref[...]
        dx_ref[...] = r * (dxh - xh * jnp.mean(dxh * xh, axis=-1, keepdims=True))
        _acc(i, dg_ref, _colsum(dy * xh))

    dx, dg = _rows(body, "rmsnorm_bwd", [x, dy], [g], [_sds(x.shape)], [_sds(g.shape)])
    return dx, dg


rmsnorm.defvjp(_rmsnorm_f, _rmsnorm_b)


def _modulate(x, g, scale, shift):
    def body(i, x_ref, g_ref, sc_ref, sh_ref, o_ref):
        x = x_ref[...]
        xn = x * _rstd(x) * g_ref[...]
        o_ref[...] = (xn * (1.0 + sc_ref[...]) + sh_ref[...]).astype(BF16)
    return _rows(body, "modulate_fwd", [x], [g, scale, shift], [_sds(x.shape, BF16)], [])[0]


def _modulate_bwd(x, g, scale, shift, dh):
    def body(i, x_ref, dh_ref, g_ref, sc_ref, dx_ref, dg_ref, dsc_ref, dsh_ref):
        x = x_ref[...]
        dh = dh_ref[...]
        gv = g_ref[...]
        r = _rstd(x)
        xh = x * r
        dxn = dh * (1.0 + sc_ref[...])
        dxh = dxn * gv
        dx_ref[...] = r * (dxh - xh * jnp.mean(dxh * xh, axis=-1, keepdims=True))
        _acc(i, dg_ref, _colsum(dxn * xh))
        _acc(i, dsc_ref, _colsum(dh * (xh * gv)))
        _acc(i, dsh_ref, _colsum(dh))

    v = _sds(g.shape)
    return _rows(body, "modulate_bwd", [x, dh], [g, scale], [_sds(x.shape)], [v, v, v])


@jax.custom_vjp
def mod_mm(x, g, scale, shift, w):
    return _mm(_modulate(x, g, scale, shift), w, name="mm_in")


def _mod_mm_f(x, g, scale, shift, w):
    h = _modulate(x, g, scale, shift)
    return _mm(h, w, name="mm_in"), (x, g, scale, shift, w, h)


def _mod_mm_b(res, dproj):
    x, g, scale, shift, w, h = res
    dh = _mm(dproj, w, tb=True, name="mm_in_dh")
    dw = _mm(h, dproj, ta=True, out_dtype=w.dtype, name="mm_in_dw")
    dx, dg, dsc, dsh = _modulate_bwd(x, g, scale, shift, dh)
    return dx, dg, dsc, dsh, dw


mod_mm.defvjp(_mod_mm_f, _mod_mm_b)


def _sigmoid(z):
    return 1.0 / (1.0 + jnp.exp(-z))


def _gate_mul(o, z):
    def body(i, o_ref, z_ref, y_ref):
        z = z_ref[...]
        y_ref[...] = (o_ref[...] * (z * _sigmoid(z))).astype(BF16)
    return _rows(body, "gate_mul_fwd", [o, z], [], [_sds(o.shape, BF16)], [])[0]


def _gate_mul_bwd(o, z, dy):
    def body(i, o_ref, z_ref, dy_ref, do_ref, dz_ref):
        z = z_ref[...]
        dy = dy_ref[...]
        s = _sigmoid(z)
        do_ref[...] = dy * (z * s)
        dz_ref[...] = dy * o_ref[...] * (s * (1.0 + z * (1.0 - s)))
    return _rows(body, "gate_mul_bwd", [o, z, dy], [], [_sds(o.shape), _sds(o.shape)], [])


@jax.custom_vjp
def gated_out(o, z, w):
    return _mm(_gate_mul(o, z), w, name="mm_out")


def _gated_out_f(o, z, w):
    y = _gate_mul(o, z)
    return _mm(y, w, name="mm_out"), (o, z, w, y)


def _gated_out_b(res, du):
    o, z, w, y = res
    dy = _mm(du, w, tb=True, name="mm_out_dy")
    dw = _mm(y, du, ta=True, out_dtype=w.dtype, name="mm_out_dw")
    do, dz = _gate_mul_bwd(o, z, dy)
    return do, dz, dw


gated_out.defvjp(_gated_out_f, _gated_out_b)


@jax.custom_vjp
def residual(x, u, gate):
    def body(i, x_ref, u_ref, g_ref, o_ref):
        o_ref[...] = x_ref[...] + g_ref[...] * u_ref[...]
    return _rows(body, "residual_fwd", [x, u], [gate], [_sds(x.shape)], [])[0]


def _residual_f(x, u, gate):
    return residual(x, u, gate), (u, gate)


def _residual_b(res, d):
    u, gate = res

    def body(i, d_ref, u_ref, g_ref, du_ref, dg_ref):
        d = d_ref[...]
        du_ref[...] = g_ref[...] * d
        _acc(i, dg_ref, _colsum(d * u_ref[...]))

    du, dg = _rows(body, "residual_bwd", [d, u], [gate], [_sds(u.shape)], [_sds(gate.shape)])
    return d, du, dg


residual.defvjp(_residual_f, _residual_b)


@jax.custom_vjp
def gate_act(u, b):
    def body(i, u_ref, b_ref, o_ref):
        t = u_ref[...] + b_ref[...]
        o_ref[...] = (jnp.minimum(t, 0.0) - jnp.log(1.0 + jnp.exp(-jnp.abs(t)))) / GLA_TEMP
    return _rows(body, "gate_act_fwd", [u], [b], [_sds(u.shape)], [])[0]


def _gate_act_f(u, b):
    return gate_act(u, b), (u, b)


def _gate_act_b(res, d):
    u, b = res

    def body(i, u_ref, d_ref, b_ref, du_ref, db_ref):
        t = u_ref[...] + b_ref[...]
        du = d_ref[...] * _sigmoid(-t) / GLA_TEMP
        du_ref[...] = du
        _acc(i, db_ref, _colsum(du))

    du, db = _rows(body, "gate_act_bwd", [u, d], [b], [_sds(u.shape)], [_sds(b.shape)])
    return du, db


gate_act.defvjp(_gate_act_f, _gate_act_b)


@jax.custom_vjp
def fma(a, b, c, d):
    def body(i, a_ref, b_ref, c_ref, d_ref, o_ref):
        o_ref[...] = a_ref[...] * b_ref[...] + c_ref[...] * d_ref[...]
    return _rows(body, "fma_fwd", [a, b, c, d], [], [_sds(a.shape)], [])[0]


def _fma_f(a, b, c, d):
    return fma(a, b, c, d), (b, d)


def _fma_b(res, g):
    b, d = res

    def body(i, g_ref, b_ref, d_ref, da_ref, dc_ref):
        g = g_ref[...]
        da_ref[...] = g * b_ref[...]
        dc_ref[...] = g * d_ref[...]

    da, dc = _rows(body, "fma_bwd", [g, b, d], [], [_sds(g.shape), _sds(g.shape)], [])
    return da, jnp.zeros_like(b), dc, jnp.zeros_like(d)


fma.defvjp(_fma_f, _fma_b)


def _silu_rows(c):
    def body(i, c_ref, o_ref):
        v = c_ref[...]
        o_ref[...] = v * _sigmoid(v)
    return _rows(body, "silu", [c], [], [_sds(c.shape)], [])[0]


@jax.custom_vjp
def loss_op(y, t):
    return _loss_fwd(y, t)[0]


def _loss_fwd(y, t):
    inv = 1.0 / y.shape[1]

    def body(i, y_ref, t_ref, d_ref, l_ref):
        e = y_ref[...] - t_ref[...]
        d_ref[...] = e * inv
        _acc(i, l_ref, jnp.sum(_colsum(e * e), axis=1, keepdims=True) * (0.5 * inv))

    d, l = _rows(body, "loss_fwd", [y, t], [], [_sds(y.shape)], [_sds((1, 1))])
    return l, d


def _loss_f(y, t):
    l, d = _loss_fwd(y, t)
    return l, d


def _loss_b(d, g):
    return d * g, jnp.zeros_like(d)


loss_op.defvjp(_loss_f, _loss_b)


def _conv_terms(cc, cx, rows, n):
    u = cc * cx
    up = jnp.where(rows == 0, 0.0, pltpu.roll(u, 1, 0))
    un = jnp.where(rows == n - 1, 0.0, pltpu.roll(u, n - 1, 0))
    return u, up, un


CONV_COLS = 128


def _conv_specs(s, n_in):
    blk = pl.BlockSpec((s, CONV_COLS), lambda j: (0, j))
    wblk = pl.BlockSpec((8, CONV_COLS), lambda j: (0, j))
    return [blk] * n_in + [wblk], blk, wblk


@jax.custom_vjp
def conv_op(cb, cc, cx, w):
    s, ch = cb.shape

    def body(cb_ref, cc_ref, cx_ref, w_ref, o_ref):
        rows = lax.broadcasted_iota(jnp.int32, (s, CONV_COLS), 0)
        u, up, un = _conv_terms(cc_ref[...], cx_ref[...], rows, s)
        conv = up * w_ref[0:1, :] + u * w_ref[1:2, :] + un * w_ref[2:3, :]
        o_ref[...] = cb_ref[...] * conv

    in_specs, blk, _ = _conv_specs(s, 3)
    return pl.pallas_call(
        body, name="conv_fwd", grid=(ch // CONV_COLS,), in_specs=in_specs, out_specs=blk,
        out_shape=_sds(cb.shape), compiler_params=_params(("parallel",)),
    )(cb, cc, cx, w)


def _conv_f(cb, cc, cx, w):
    return conv_op(cb, cc, cx, w), (cb, cc, cx, w)


def _conv_b(res, d):
    cb, cc, cx, w = res
    s, ch = cb.shape

    def body(cb_ref, cc_ref, cx_ref, d_ref, w_ref, dcb_ref, dcc_ref, dcx_ref, dw_ref):
        rows = lax.broadcasted_iota(jnp.int32, (s, CONV_COLS), 0)
        cc_v = cc_ref[...]
        cx_v = cx_ref[...]
        u, up, un = _conv_terms(cc_v, cx_v, rows, s)
        w0, w1, w2 = w_ref[0:1, :], w_ref[1:2, :], w_ref[2:3, :]
        dv = d_ref[...]
        dcb_ref[...] = dv * (up * w0 + u * w1 + un * w2)
        dconv = dv * cb_ref[...]
        d_next = jnp.where(rows == s - 1, 0.0, pltpu.roll(dconv, s - 1, 0))
        d_prev = jnp.where(rows == 0, 0.0, pltpu.roll(dconv, 1, 0))
        du = w0 * d_next + w1 * dconv + w2 * d_prev
        dcc_ref[...] = du * cx_v
        dcx_ref[...] = du * cc_v
        dw_ref[...] = jnp.zeros_like(dw_ref)
        dw_ref[0:1, :] = _colsum(dconv * up)
        dw_ref[1:2, :] = _colsum(dconv * u)
        dw_ref[2:3, :] = _colsum(dconv * un)

    in_specs, blk, wblk = _conv_specs(s, 4)
    v = _sds(cb.shape)
    return tuple(pl.pallas_call(
        body, name="conv_bwd", grid=(ch // CONV_COLS,), in_specs=in_specs, out_specs=[blk, blk, blk, wblk],
        out_shape=[v, v, v, _sds(w.shape)], compiler_params=_params(("parallel",)),
    )(cb, cc, cx, d, w))


conv_op.defvjp(_conv_f, _conv_b)


def _gla_masks(rev):
    c = GLA_CHUNK
    row = lax.broadcasted_iota(jnp.int32, (c, c), 0)
    col = lax.broadcasted_iota(jnp.int32, (c, c), 1)
    tri = jnp.where((row <= col) if rev else (row >= col), 1.0, 0.0).astype(F32)
    mask = (row < col) if rev else (row >= col)
    return tri, mask


def _gla_chunk(q_ref, k_ref, v_ref, g_ref, n, rev, tri):
    c = GLA_CHUNK
    rows = pl.ds(pl.multiple_of(n * c, c), c)
    q = q_ref[0, rows, :] * (GLA_DK ** -0.5)
    k = k_ref[0, rows, :]
    v = v_ref[0, rows, :]
    g = g_ref[0, rows, :]
    b = _dot(tri, g, 1, 0, HIGHEST)
    bt = _colsum(g)
    return rows, q, k, v, b, bt


def _gla_fwd(q, k, v, gf, gb):
    h, s, dk = q.shape
    dv = v.shape[2]
    c = GLA_CHUNK
    n_chunks = s // c

    def body(q_ref, k_ref, v_ref, gf_ref, gb_ref, o_ref, sf_ref, sb_ref):
        for rev, g_ref, s_ref in ((False, gf_ref, sf_ref), (True, gb_ref, sb_ref)):
            tri, mask = _gla_masks(rev)

            def step(i, st, rev=rev, g_ref=g_ref, s_ref=s_ref, tri=tri, mask=mask):
                n = (n_chunks - 1 - i) if rev else i
                rows, qv, kv, vv, b, bt = _gla_chunk(q_ref, k_ref, v_ref, g_ref, n, rev, tri)
                vb = vv.astype(BF16)
                qd = (qv * jnp.exp(b)).astype(BF16)
                ki = (kv * jnp.exp(-b)).astype(BF16)
                ke = (kv * jnp.exp(bt - b)).astype(BF16)
                a = jnp.where(mask, _dot(qd, ki, 1, 1), 0.0).astype(BF16)
                o = _dot(a, vb, 1, 0) + _dot(qd, st.astype(BF16), 1, 1)
                if rev:
                    o_ref[0, rows, :] += o
                else:
                    o_ref[0, rows, :] = o
                s_ref[0, n] = st
                return st * jnp.exp(bt) + _dot(vb, ke, 0, 0)

            lax.fori_loop(0, n_chunks, step, jnp.zeros((dv, dk), F32))

    blk_k = pl.BlockSpec((1, s, dk), lambda i: (i, 0, 0))
    blk_v = pl.BlockSpec((1, s, dv), lambda i: (i, 0, 0))
    blk_s = pl.BlockSpec((1, n_chunks, dv, dk), lambda i: (i, 0, 0, 0))
    st = _sds((h, n_chunks, dv, dk))
    return pl.pallas_call(
        body, name="gla_fwd", grid=(h,), in_specs=[blk_k, blk_k, blk_v, blk_k, blk_k],
        out_specs=[blk_v, blk_s, blk_s], out_shape=[_sds(v.shape), st, st],
        compiler_params=_params(("parallel",)),
    )(q, k, v, gf, gb)


def _gla_bwd(q, k, v, gf, gb, sf, sb, do):
    h, s, dk = q.shape
    dv = v.shape[2]
    c = GLA_CHUNK
    n_chunks = s // c
    scale = GLA_DK ** -0.5

    def body(q_ref, k_ref, v_ref, gf_ref, gb_ref, sf_ref, sb_ref, do_ref, dq_ref, dk_ref, dv_ref, dgf_ref, dgb_ref,
             dst_ref):
        rowc = lax.broadcasted_iota(jnp.int32, (c, dk), 0)
        for rev, g_ref, s_ref, dg_ref in ((False, gf_ref, sf_ref, dgf_ref), (True, gb_ref, sb_ref, dgb_ref)):
            tri, mask = _gla_masks(rev)
            dst_ref[...] = jnp.zeros_like(dst_ref)

            def step(i, carry, rev=rev, g_ref=g_ref, s_ref=s_ref, dg_ref=dg_ref, tri=tri, mask=mask):
                dst = dst_ref[...]
                n = i if rev else (n_chunks - 1 - i)
                rows, qv, kv, vv, b, bt = _gla_chunk(q_ref, k_ref, v_ref, g_ref, n, rev, tri)
                eb = jnp.exp(b)
                enb = jnp.exp(-b)
                etb = jnp.exp(bt - b)
                ebt = jnp.exp(bt)
                qd = qv * eb
                ki = kv * enb
                ke = kv * etb
                qd_b, ki_b, ke_b = qd.astype(BF16), ki.astype(BF16), ke.astype(BF16)
                st = s_ref[0, n]
                vb = vv.astype(BF16)
                do_b = do_ref[0, rows, :].astype(BF16)
                dst_b = dst.astype(BF16)
                a = jnp.where(mask, _dot(qd_b, ki_b, 1, 1), 0.0).astype(BF16)
                da = jnp.where(mask, _dot(do_b, vb, 1, 1), 0.0).astype(BF16)
                dvv = _dot(a, do_b, 0, 0) + _dot(ke_b, dst_b, 1, 1)
                dqd = _dot(da, ki_b, 1, 0) + _dot(do_b, st.astype(BF16), 1, 0)
                dki = _dot(da, qd_b, 0, 0)
                dke = _dot(vb, dst_b, 1, 0)
                dbt = _colsum(st * dst) * ebt + _colsum(dke * ke)
                db = dqd * qd - dki * ki - dke * ke
                db = db + jnp.where(rowc == (0 if rev else c - 1), dbt, 0.0)
                dg_ref[0, rows, :] = _dot(tri, db, 0, 0, HIGHEST)
                dqv = dqd * eb * scale
                dkv = dki * enb + dke * etb
                if rev:
                    dq_ref[0, rows, :] += dqv
                    dk_ref[0, rows, :] += dkv
                    dv_ref[0, rows, :] += dvv
                else:
                    dq_ref[0, rows, :] = dqv
                    dk_ref[0, rows, :] = dkv
                    dv_ref[0, rows, :] = dvv
                dst_ref[...] = _dot(do_b, qd_b, 0, 0) + dst * ebt
                return carry

            lax.fori_loop(0, n_chunks, step, 0)

    blk_k = pl.BlockSpec((1, s, dk), lambda i: (i, 0, 0))
    blk_v = pl.BlockSpec((1, s, dv), lambda i: (i, 0, 0))
    blk_s = pl.BlockSpec((1, n_chunks, dv, dk), lambda i: (i, 0, 0, 0))
    vk, vv = _sds(q.shape), _sds(v.shape)
    return pl.pallas_call(
        body, name="gla_bwd", grid=(h,), in_specs=[blk_k, blk_k, blk_v, blk_k, blk_k, blk_s, blk_s, blk_v],
        out_specs=[blk_k, blk_k, blk_v, blk_k, blk_k], out_shape=[vk, vk, vv, vk, vk],
        scratch_shapes=[pltpu.VMEM((dv, dk), F32)], compiler_params=_params(("parallel",)),
    )(q, k, v, gf, gb, sf, sb, do)


@jax.custom_vjp
def gla(q, k, v, gf, gb):
    return _gla_fwd(q, k, v, gf, gb)[0]


def _gla_f(q, k, v, gf, gb):
    o, sf, sb = _gla_fwd(q, k, v, gf, gb)
    return o, (q, k, v, gf, gb, sf, sb)


def _gla_b(res, do):
    return tuple(_gla_bwd(*res, do))


gla.defvjp(_gla_f, _gla_b)


ATTN_TQ = 256


def _attn_fwd(q, k, v):
    h, s, dq = q.shape
    dv = v.shape[2]
    tq = min(ATTN_TQ, s)
    scale = (MLA_NOPE + MLA_ROPE) ** -0.5

    def body(q_ref, k_ref, v_ref, o_ref, lse_ref):
        sc = _dot(q_ref[0].astype(BF16), k_ref[0].astype(BF16), 1, 1) * scale
        m = jnp.max(sc, axis=-1, keepdims=True)
        p = jnp.exp(sc - m)
        l = jnp.sum(p, axis=-1, keepdims=True)
        p = p * (1.0 / l)
        o_ref[0] = _dot(p.astype(BF16), v_ref[0].astype(BF16), 1, 0)
        lse_ref[0] = m + jnp.log(l)

    return pl.pallas_call(
        body, name="attn_fwd", grid=(h, s // tq),
        in_specs=[pl.BlockSpec((1, tq, dq), lambda i, j: (i, j, 0)), pl.BlockSpec((1, s, dq), lambda i, j: (i, 0, 0)),
                  pl.BlockSpec((1, s, dv), lambda i, j: (i, 0, 0))],
        out_specs=[pl.BlockSpec((1, tq, dv), lambda i, j: (i, j, 0)), pl.BlockSpec((1, tq, 1), lambda i, j: (i, j, 0))],
        out_shape=[_sds((h, s, dv)), _sds((h, s, 1))],
        compiler_params=_params(("parallel", "parallel")),
    )(q, k, v)


def _attn_bwd(q, k, v, o, lse, do):
    h, s, dq = q.shape
    dv = v.shape[2]
    tq = min(ATTN_TQ, s)
    scale = (MLA_NOPE + MLA_ROPE) ** -0.5

    def body(q_ref, k_ref, v_ref, o_ref, lse_ref, do_ref, dq_ref, dk_ref, dv_ref):
        j = pl.program_id(1)
        qb = q_ref[0].astype(BF16)
        kb = k_ref[0].astype(BF16)
        do = do_ref[0]
        do_b = do.astype(BF16)
        p = jnp.exp(_dot(qb, kb, 1, 1) * scale - lse_ref[0])
        dp = _dot(do_b, v_ref[0].astype(BF16), 1, 1)
        delta = jnp.sum(do * o_ref[0], axis=-1, keepdims=True)
        ds = (p * (dp - delta) * scale).astype(BF16)
        dq_ref[0] = _dot(ds, kb, 1, 0)
        dk_c = _dot(ds, qb, 0, 0)
        dv_c = _dot(p.astype(BF16), do_b, 0, 0)

        @pl.when(j == 0)
        def _():
            dk_ref[0] = dk_c
            dv_ref[0] = dv_c

        @pl.when(j != 0)
        def _():
            dk_ref[0] += dk_c
            dv_ref[0] += dv_c

    qblk = pl.BlockSpec((1, tq, dq), lambda i, j: (i, j, 0))
    kblk = pl.BlockSpec((1, s, dq), lambda i, j: (i, 0, 0))
    vblk = pl.BlockSpec((1, s, dv), lambda i, j: (i, 0, 0))
    oblk = pl.BlockSpec((1, tq, dv), lambda i, j: (i, j, 0))
    lblk = pl.BlockSpec((1, tq, 1), lambda i, j: (i, j, 0))
    return pl.pallas_call(
        body, name="attn_bwd", grid=(h, s // tq),
        in_specs=[qblk, kblk, vblk, oblk, lblk, oblk], out_specs=[qblk, kblk, vblk],
        out_shape=[_sds(q.shape), _sds(k.shape), _sds(v.shape)],
        compiler_params=_params(("parallel", "arbitrary")),
    )(q, k, v, o, lse, do)


@jax.custom_vjp
def attn(q, k, v):
    return _attn_fwd(q, k, v)[0]


def _attn_f(q, k, v):
    o, lse = _attn_fwd(q, k, v)
    return o, (q, k, v, o, lse)


def _attn_b(res, do):
    return tuple(_attn_bwd(*res, do))


attn.defvjp(_attn_f, _attn_b)


@jax.custom_vjp
def split_proj(proj):
    out, at = [], 0
    for _, _, _, wp in PROJ_SEGS:
        out.append(proj[:, at:at + wp])
        at += wp
    return tuple(out)


def _split_f(proj):
    return split_proj(proj), None


def _split_b(_, gs):
    return (jnp.concatenate(gs, axis=1),)


split_proj.defvjp(_split_f, _split_b)


def _add2(a, b):
    shp = a.shape
    a2, b2 = a.reshape(-1, shp[-1]), b.reshape(-1, shp[-1])

    def body(i, a_ref, b_ref, o_ref):
        o_ref[...] = (a_ref[...].astype(F32) + b_ref[...].astype(F32)).astype(BF16)

    return _rows(body, "add_pair", [a2, b2], [], [_sds(a2.shape, BF16)], [])[0].reshape(shp)


def _add_chips(q):
    n, r, w = q.shape
    tr = _pick_rows(r, w)

    def body(*refs):
        o_ref = refs[n]
        t = refs[0][0].astype(F32)
        for j in range(1, n):
            t = t + refs[j][0].astype(F32)
        o_ref[...] = t

    specs = [pl.BlockSpec((1, tr, w), functools.partial(lambda i, j: (j, i, 0), j=j)) for j in range(n)]
    return pl.pallas_call(
        body, name="add_chips", grid=(r // tr,), in_specs=specs, out_specs=pl.BlockSpec((tr, w), lambda i: (i, 0)),
        out_shape=_sds((r, w)), compiler_params=_params(("parallel",)),
    )(*([q] * n))


def _sum_devices(g):
    n = g.shape[2]

    def body(g_ref, o_ref):
        t = g_ref[0]
        for j in range(1, N_DEV):
            t = t + g_ref[j]
        o_ref[...] = t

    return pl.pallas_call(body, name="sum_devices", out_shape=_sds((1, n)), compiler_params=_params())(g)


def _adamw(w, g, m, v):
    shp = w.shape
    w2, g2, m2, v2 = (t.reshape(-1, shp[-1]) for t in (w, g, m, v))
    c1 = 1.0 - ADAM_B1 ** ADAM_STEP
    c2 = 1.0 - ADAM_B2 ** ADAM_STEP

    def body(i, w_ref, g_ref, m_ref, v_ref, d_ref, mo_ref, vo_ref):
        gv = g_ref[...]
        mn = ADAM_B1 * m_ref[...] + (1.0 - ADAM_B1) * gv
        vn = ADAM_B2 * v_ref[...] + (1.0 - ADAM_B2) * (gv * gv)
        d_ref[...] = -ADAM_LR * ((mn / c1) / (jnp.sqrt(vn / c2) + ADAM_EPS) + ADAM_WD * w_ref[...])
        mo_ref[...] = mn
        vo_ref[...] = vn

    s2 = _sds(w2.shape)
    tr = _pick_rows(w2.shape[0], w2.shape[1], 8)
    d, mn, vn = _rows(body, "adamw", [w2, g2, m2, v2], [], [s2, s2, s2], [], tr=tr)
    return d.reshape(shp), mn.reshape(shp), vn.reshape(shp)


ANY = pl.BlockSpec(memory_space=pl.ANY)


def _place():
    return lax.axis_index("x"), lax.axis_index("y"), lax.axis_index("c")


def _comm_call(body, name, arrs, out_shapes, n_remote, n_local):
    return pl.pallas_call(
        body, name=name, in_specs=[ANY] * len(arrs), out_specs=[ANY] * len(out_shapes), out_shape=out_shapes,
        scratch_shapes=[pltpu.SemaphoreType.DMA((n_remote,)), pltpu.SemaphoreType.DMA((n_remote,)),
                        pltpu.SemaphoreType.DMA((n_local,))],
    )(*arrs)


def all_gather8(arrs, name):
    n = len(arrs)

    def body(*refs):
        ins, outs = refs[:n], refs[n:2 * n]
        send, recv, lsem = refs[2 * n:]
        x, y, c = _place()
        me, sib = (x, y, c), (x, y, 1 - c)
        chips = [(1 - x, y), (x, 1 - y), (1 - x, 1 - y)]

        def slot(p):
            return 4 * p[0] + 2 * p[1] + p[2]

        def copy(t, k, block, to, src=None):
            dst = outs[t].at[slot(block)]
            return pltpu.make_async_remote_copy(
                src_ref=dst if src is None else src, dst_ref=dst, send_sem=send.at[7 * t + k],
                recv_sem=recv.at[7 * t + k], device_id=to, device_id_type=MESH)

        mine = [pltpu.make_async_copy(ins[t], outs[t].at[slot(me)], lsem.at[t]) for t in range(n)]
        for cp in mine:
            cp.start()
        first = []
        for t in range(n):
            first.append(copy(t, 0, me, sib, src=ins[t]))
            first += [copy(t, 1 + j, me, (*chip, c), src=ins[t]) for j, chip in enumerate(chips)]
        for cp in first:
            cp.start()
        passed = []
        for j, chip in enumerate(chips):
            for t in range(n):
                copy(t, 1 + j, (*chip, c), me).wait_recv()
                fw = copy(t, 4 + j, (*chip, c), sib)
                fw.start()
                passed.append(fw)
        for t in range(n):
            copy(t, 0, sib, me).wait_recv()
            for j, chip in enumerate(chips):
                copy(t, 4 + j, (*chip, 1 - c), me).wait_recv()
        for cp in first + passed:
            cp.wait_send()
        for cp in mine:
            cp.wait()

    outs = [_sds((N_DEV,) + a.shape, a.dtype) for a in arrs]
    return _comm_call(body, name, arrs, outs, 7 * n, n)


def sibling_select_swap(arrs, name):
    n = len(arrs)

    def body(*refs):
        ins, mine, theirs = refs[:n], refs[n:2 * n], refs[2 * n:3 * n]
        send, recv, lsem = refs[3 * n:]
        x, y, c = _place()
        loc = [pltpu.make_async_copy(ins[t].at[c], mine[t], lsem.at[t]) for t in range(n)]
        rem = [pltpu.make_async_remote_copy(src_ref=ins[t].at[1 - c], dst_ref=theirs[t], send_sem=send.at[t],
                                            recv_sem=recv.at[t], device_id=(x, y, 1 - c), device_id_type=MESH)
               for t in range(n)]
        for cp in loc + rem:
            cp.start()
        for cp in rem:
            cp.wait_recv()
        for cp in rem:
            cp.wait_send()
        for cp in loc:
            cp.wait()

    outs = [_sds(a.shape[1:], a.dtype) for a in arrs]
    res = _comm_call(body, name, arrs, outs + outs, n, n)
    return res[:n], res[n:]


def exchange_chips(arrs, name):
    n = len(arrs)

    def body(*refs):
        ins, outs = refs[:n], refs[n:2 * n]
        send, recv, lsem = refs[2 * n:]
        x, y, c = _place()
        my_chip = 2 * x + y
        peers = [(1 - x, y), (x, 1 - y), (1 - x, 1 - y)]
        loc = [pltpu.make_async_copy(ins[t].at[my_chip], outs[t].at[my_chip], lsem.at[t]) for t in range(n)]
        rem = []
        for t in range(n):
            for j, (px, py) in enumerate(peers):
                rem.append(pltpu.make_async_remote_copy(
                    src_ref=ins[t].at[2 * px + py], dst_ref=outs[t].at[my_chip], send_sem=send.at[3 * t + j],
                    recv_sem=recv.at[3 * t + j], device_id=(px, py, c), device_id_type=MESH))
        for cp in loc + rem:
            cp.start()
        for t in range(n):
            for j, (px, py) in enumerate(peers):
                pltpu.make_async_remote_copy(
                    src_ref=ins[t].at[my_chip], dst_ref=outs[t].at[2 * px + py], send_sem=send.at[3 * t + j],
                    recv_sem=recv.at[3 * t + j], device_id=(px, py, c), device_id_type=MESH).wait_recv()
        for cp in rem:
            cp.wait_send()
        for cp in loc:
            cp.wait()

    outs = [_sds(a.shape, a.dtype) for a in arrs]
    return _comm_call(body, name, arrs, outs, 3 * n, n)


def sibling_all_gather(arrs, name):
    n = len(arrs)

    def body(*refs):
        ins, outs = refs[:n], refs[n:2 * n]
        send, recv, lsem = refs[2 * n:]
        x, y, c = _place()
        loc = [pltpu.make_async_copy(ins[t], outs[t].at[c], lsem.at[t]) for t in range(n)]
        rem = [pltpu.make_async_remote_copy(src_ref=ins[t], dst_ref=outs[t].at[c], send_sem=send.at[t],
                                            recv_sem=recv.at[t], device_id=(x, y, 1 - c), device_id_type=MESH)
               for t in range(n)]
        for cp in loc + rem:
            cp.start()
        for t in range(n):
            pltpu.make_async_remote_copy(src_ref=ins[t], dst_ref=outs[t].at[1 - c], send_sem=send.at[t],
                                         recv_sem=recv.at[t], device_id=(x, y, 1 - c), device_id_type=MESH).wait_recv()
        for cp in rem:
            cp.wait_send()
        for cp in loc:
            cp.wait()

    outs = [_sds((2,) + a.shape, a.dtype) for a in arrs]
    return _comm_call(body, name, arrs, outs, n, n)


def _build_w_in(w4):
    full = jnp.concatenate([w4[j] for j in range(N_CHIPS)], axis=1)
    parts = []
    for _, start, width, wp in PROJ_SEGS:
        if width:
            parts.append(full[:, start:start + width])
        if wp > width:
            parts.append(jnp.zeros((full.shape[0], wp - width), full.dtype))
    return jnp.concatenate(parts, axis=1)


def _permute_w_uq(w):
    w3 = w.reshape(w.shape[0], MLA_HEADS, MLA_NOPE + MLA_ROPE)
    hr = MLA_ROPE // 2
    return jnp.concatenate([
        w3[:, :, :MLA_NOPE].reshape(w.shape[0], -1),
        w3[:, :, MLA_NOPE:MLA_NOPE + hr].reshape(w.shape[0], -1),
        w3[:, :, MLA_NOPE + hr:].reshape(w.shape[0], -1)], axis=1)


def _heads(t, d):
    return t.reshape(t.shape[0], -1, d).transpose(1, 0, 2)


def _unheads(t):
    return t.transpose(1, 0, 2).reshape(t.shape[1], -1)


def _layer(xh, mod, big, small, rope_q, rope_k):
    s = xh.shape[0]
    d = D_MODEL
    shift, scale, gate = mod[None, 0:d], mod[None, d:2 * d], mod[None, 2 * d:3 * d]
    w_al = _build_w_in(big["w_in"])
    proj = mod_mm(xh, small["norm_g"][None], scale, shift, w_al)
    gq, gk, gv, glr, mq, mkv, mkr, cb, cc, cx, _, z = split_proj(proj)

    rk = GLA_RANK
    hk = GLA_HEADS * GLA_DK
    wg = jnp.zeros((128, 2 * hk), F32)
    wg = wg.at[0:rk, 0:hk].set(small["gla_wg_f"]).at[rk:2 * rk, hk:].set(small["gla_wg_b"])
    bg = jnp.concatenate([small["gla_bg_f"], small["gla_bg_b"]])[None]
    la = gate_act(mm(glr, wg), bg)
    o_gla = gla(_heads(gq, GLA_DK), _heads(gk, GLA_DK), _heads(gv, GLA_DV),
                _heads(la[:, :hk], GLA_DK), _heads(la[:, hk:], GLA_DK))
    o_gla = rmsnorm(o_gla.reshape(GLA_HEADS * s, GLA_DV), small["gla_norm_g"][None])
    o_gla = _unheads(o_gla.reshape(GLA_HEADS, s, GLA_DV))

    nh = MLA_HEADS * MLA_NOPE
    hr = MLA_HEADS * MLA_ROPE // 2
    cq = rmsnorm(mq, small["mla_q_norm_g"][None])
    qm = mm(cq, _permute_w_uq(jnp.concatenate([big["w_uq"][j] for j in range(N_CHIPS)], axis=1)))
    qr = qm[:, nh:]
    qr = fma(qr, rope_q[0], jnp.concatenate([qr[:, hr:], qr[:, :hr]], axis=1), rope_q[1])
    ckv = rmsnorm(mkv, small["mla_kv_norm_g"][None])
    kv = mm(ckv, jnp.concatenate([big["w_ukv"][j] for j in range(N_CHIPS)], axis=1))
    kr = mkr[:, :MLA_ROPE]
    kr = fma(kr, rope_k[0], jnp.concatenate([kr[:, MLA_ROPE // 2:], kr[:, :MLA_ROPE // 2]], axis=1), rope_k[1])
    q3 = jnp.concatenate([qm[:, :nh].reshape(s, MLA_HEADS, MLA_NOPE), qr[:, :hr].reshape(s, MLA_HEADS, -1),
                          qr[:, hr:].reshape(s, MLA_HEADS, -1)], axis=-1).transpose(1, 0, 2)
    kv3 = kv.reshape(s, MLA_HEADS, MLA_NOPE + MLA_DV)
    k3 = jnp.concatenate([kv3[:, :, :MLA_NOPE], jnp.broadcast_to(kr[:, None, :], (s, MLA_HEADS, MLA_ROPE))],
                         axis=-1).transpose(1, 0, 2)
    v3 = kv3[:, :, MLA_NOPE:].transpose(1, 0, 2)
    o_mla = rmsnorm(_unheads(attn(q3, k3, v3)), small["mla_out_g"][None])

    cw = jnp.concatenate([small["conv_w"], jnp.zeros((5, CONV_CH), F32)], axis=0)
    o_conv = rmsnorm(conv_op(cb, cc, cx, cw), small["conv_out_g"][None])

    o = jnp.concatenate([o_gla, o_mla, o_conv], axis=1)
    w_out = big["w_out"].reshape(d, d)
    return residual(xh, gated_out(o, z, w_out), gate)


def _loss_fn(xh, mods, bigs, smalls, final_g, target, rope_q, rope_k):
    h = xh
    for l in range(DEPTH):
        h = _layer(h, mods[l], bigs[l], smalls[l], rope_q, rope_k)
    return loss_op(rmsnorm(h, final_g[None]), target)[0, 0]


SMALL_REPL = ("norm_g", "gla_bg_f", "gla_bg_b", "gla_norm_g", "mla_q_norm_g", "mla_kv_norm_g", "mla_out_g",
              "conv_out_g")
SMALL_SHARDED = ("gla_wg_f", "gla_wg_b", "conv_w")
BIG = ("w_in", "w_out", "w_uq", "w_ukv")


def kernel(x, c, positions, ada_w, ada_b, norm_g, w_in, gla_wg_f, gla_bg_f, gla_wg_b, gla_bg_b, gla_norm_g, mla_q_norm_g, mla_kv_norm_g, mla_w_uq, mla_w_ukv, mla_out_g, conv_w, conv_out_g, w_out, final_g, loss_target, m_ada_w, m_ada_b, m_norm_g, m_w_in, m_gla_wg_f, m_gla_bg_f, m_gla_wg_b, m_gla_bg_b, m_gla_norm_g, m_mla_q_norm_g, m_mla_kv_norm_g, m_mla_w_uq, m_mla_w_ukv, m_mla_out_g, m_conv_w, m_conv_out_g, m_w_out, m_final_g, v_ada_w, v_ada_b, v_norm_g, v_w_in, v_gla_wg_f, v_gla_bg_f, v_gla_wg_b, v_gla_bg_b, v_gla_norm_g, v_mla_q_norm_g, v_mla_kv_norm_g, v_mla_w_uq, v_mla_w_ukv, v_mla_out_g, v_conv_w, v_conv_out_g, v_w_out, v_final_g):
    xi, yi, ci = _place()
    chip = 2 * xi + yi
    dev = 2 * chip + ci
    s = x.shape[1]
    d = D_MODEL
    weights = dict(ada_w=ada_w, ada_b=ada_b, norm_g=norm_g, w_in=w_in, gla_wg_f=gla_wg_f, gla_bg_f=gla_bg_f,
                   gla_wg_b=gla_wg_b, gla_bg_b=gla_bg_b, gla_norm_g=gla_norm_g, mla_q_norm_g=mla_q_norm_g,
                   mla_kv_norm_g=mla_kv_norm_g, mla_w_uq=mla_w_uq, mla_w_ukv=mla_w_ukv, mla_out_g=mla_out_g,
                   conv_w=conv_w, conv_out_g=conv_out_g, w_out=w_out, final_g=final_g)
    m_in = dict(ada_w=m_ada_w, ada_b=m_ada_b, norm_g=m_norm_g, w_in=m_w_in, gla_wg_f=m_gla_wg_f, gla_bg_f=m_gla_bg_f,
                gla_wg_b=m_gla_wg_b, gla_bg_b=m_gla_bg_b, gla_norm_g=m_gla_norm_g, mla_q_norm_g=m_mla_q_norm_g,
                mla_kv_norm_g=m_mla_kv_norm_g, mla_w_uq=m_mla_w_uq, mla_w_ukv=m_mla_w_ukv, mla_out_g=m_mla_out_g,
                conv_w=m_conv_w, conv_out_g=m_conv_out_g, w_out=m_w_out, final_g=m_final_g)
    v_in = dict(ada_w=v_ada_w, ada_b=v_ada_b, norm_g=v_norm_g, w_in=v_w_in, gla_wg_f=v_gla_wg_f, gla_bg_f=v_gla_bg_f,
                gla_wg_b=v_gla_wg_b, gla_bg_b=v_gla_bg_b, gla_norm_g=v_gla_norm_g, mla_q_norm_g=v_mla_q_norm_g,
                mla_kv_norm_g=v_mla_kv_norm_g, mla_w_uq=v_mla_w_uq, mla_w_ukv=v_mla_w_ukv, mla_out_g=v_mla_out_g,
                conv_w=v_conv_w, conv_out_g=v_conv_out_g, w_out=v_w_out, final_g=v_final_g)

    def mine_bf16(w):
        return lax.dynamic_index_in_dim(w, ci, 0, keepdims=False).astype(BF16)

    g_c, g_in, g_out, g_uq, g_ukv, g_wgf, g_wgb, g_cw = all_gather8(
        [c, mine_bf16(w_in), mine_bf16(w_out), mine_bf16(mla_w_uq), mine_bf16(mla_w_ukv), gla_wg_f, gla_wg_b, conv_w],
        "gather_weights")

    def by_layer(g):
        g4 = g.reshape((N_CHIPS, 2) + g.shape[1:])
        return [g4[:, l] for l in range(DEPTH)]

    bigs = [dict(w_in=a, w_out=b, w_uq=u, w_ukv=k)
            for a, b, u, k in zip(by_layer(g_in), by_layer(g_out), by_layer(g_uq), by_layer(g_ukv))]

    def unshard_cols(g):
        g4 = g[0::2]
        return g4.transpose(1, 2, 0, 3).reshape(g4.shape[1], g4.shape[2], -1)

    small_full = dict(gla_wg_f=unshard_cols(g_wgf), gla_wg_b=unshard_cols(g_wgb), conv_w=unshard_cols(g_cw))
    for nme in SMALL_REPL:
        small_full[nme] = weights[nme]
    smalls = [{nme: small_full[nme][l] for nme in SMALL_REPL + SMALL_SHARDED} for l in range(DEPTH)]

    c_act = _silu_rows(g_c[:, 0, :])
    c_act16 = jnp.concatenate([c_act, jnp.zeros_like(c_act)], axis=0)
    n_ada = ada_w.shape[2]
    parts = []
    for l in range(DEPTH):
        bias = lax.dynamic_slice_in_dim(ada_b[l], chip * n_ada, n_ada)[None]
        parts.append(_mm(c_act16, ada_w[l], bias=bias, name="ada_fwd"))
    g_mod, = all_gather8([jnp.stack(parts)], "gather_mod")
    mod_mine = lax.dynamic_index_in_dim(g_mod[0::2], dev, 2, keepdims=False)
    mods = mod_mine.transpose(1, 0, 2).reshape(DEPTH, 3 * d)

    inv_freq = ROPE_THETA ** (-jnp.arange(0, MLA_ROPE, 2, dtype=F32) / MLA_ROPE)
    ang = positions[0].astype(F32)[:, None] * inv_freq
    cos, sin = jnp.cos(ang), jnp.sin(ang)
    cos_h, sin_h = jnp.tile(cos, (1, MLA_HEADS)), jnp.tile(sin, (1, MLA_HEADS))
    rope_q = (jnp.concatenate([cos_h, cos_h], axis=1), jnp.concatenate([-sin_h, sin_h], axis=1))
    rope_k = (jnp.concatenate([cos, cos], axis=1), jnp.concatenate([-sin, sin], axis=1))

    loss_dev, (dx, dmods, dbigs, dsmalls, dfinal) = jax.value_and_grad(_loss_fn, argnums=(0, 1, 2, 3, 4))(
        x[0], mods, bigs, smalls, final_g, loss_target[0], rope_q, rope_k)
    loss = lax.psum(loss_dev, ("x", "y", "c"))

    pieces = [dmods.reshape(-1), dfinal]
    for nme in SMALL_REPL + SMALL_SHARDED:
        pieces.append(jnp.stack([dsmalls[l][nme] for l in range(DEPTH)]).reshape(-1))
    sizes = [p.shape[0] for p in pieces]
    flat = jnp.concatenate(pieces)
    padn = (-flat.shape[0]) % 128
    flat = jnp.concatenate([flat, jnp.zeros((padn,), F32)])[None]
    g_small, = all_gather8([flat], "gather_small_grads")
    total = _sum_devices(g_small)[0]
    offs, at = [], 0
    for n_el in sizes:
        offs.append(at)
        at += n_el

    def piece(i, shape):
        return total[offs[i]:offs[i] + sizes[i]].reshape(shape)

    grads = {"ada_b": piece(0, (DEPTH, 3 * d)), "final_g": piece(1, (d,))}
    for i, nme in enumerate(SMALL_REPL + SMALL_SHARDED):
        full = piece(2 + i, small_full[nme].shape)
        if nme in SMALL_SHARDED:
            ncol = weights[nme].shape[2]
            full = lax.dynamic_slice_in_dim(full, chip * ncol, ncol, axis=2)
        grads[nme] = full

    dmod_all = g_small[:, 0, :DEPTH * 3 * d].reshape(N_DEV, DEPTH, 3 * d)
    dmod_cols = lax.dynamic_slice_in_dim(dmod_all, chip * n_ada, n_ada, axis=2)
    g_ada = []
    for l in range(DEPTH):
        dm16 = jnp.concatenate([dmod_cols[:, l], jnp.zeros((N_DEV, n_ada), F32)], axis=0)
        g_ada.append(_mm(c_act16, dm16, ta=True, name="ada_bwd"))
    grads["ada_w"] = jnp.stack(g_ada)

    stacked = [jnp.stack([dbigs[l][nme] for l in range(DEPTH)]) for nme in BIG]
    mine, theirs = sibling_select_swap(stacked, "reduce_sibling")
    pair = [_add2(a, b) for a, b in zip(mine, theirs)]
    landed = exchange_chips(pair, "reduce_chips")
    reduced = [_add_chips(q.reshape(N_CHIPS, -1, q.shape[-1])).reshape(q.shape[1:]) for q in landed]
    both = sibling_all_gather(reduced, "share_sibling")
    for nme, g in zip(("w_in", "w_out", "mla_w_uq", "mla_w_ukv"), both):
        grads[nme] = g

    order = list(weights)
    delta, new_m, new_v = {}, {}, {}
    for nme in order:
        delta[nme], new_m[nme], new_v[nme] = _adamw(weights[nme], grads[nme], m_in[nme], v_in[nme])
    return (loss, dx[None], *[grads[n_] for n_ in order], *[delta[n_] for n_ in order],
            *[new_m[n_] for n_ in order], *[new_v[n_] for n_ in order])
```

```python
import functools

import jax
import jax.numpy as jnp
from jax import lax
from jax.experimental import pallas as pl
from jax.experimental.pallas import tpu as pltpu

F32 = jnp.float32
BF16 = jnp.bfloat16
MESH = pl.DeviceIdType.MESH
HIGHEST = lax.Precision.HIGHEST

DEPTH = 2
D_MODEL = 2048
GLA_HEADS = 6
GLA_DK = 64
GLA_DV = 128
GLA_RANK = 16
GLA_TEMP = 16.0
GLA_CHUNK = 64
GLA_W = GLA_HEADS * GLA_DV
MLA_HEADS = 6
MLA_QL = 384
MLA_KVL = 256
MLA_NOPE = 128
MLA_ROPE = 64
MLA_DV = 128
MLA_W = MLA_HEADS * MLA_DV
CONV_CH = D_MODEL - GLA_W - MLA_W
ROPE_THETA = 10000.0
EPS = 1e-6
IN_DIM = 5856
N_CHIPS = 4
N_DEV = 8

ADAM_LR = 0.001
ADAM_B1 = 0.9
ADAM_B2 = 0.999
ADAM_EPS = 1e-08
ADAM_WD = 0.01
ADAM_STEP = 10

PROJ_SEGS = (
    ("gq", 0, 384, 384), ("gk", 384, 384, 384), ("gv", 768, 768, 768), ("glr", 1536, 32, 128),
    ("mq", 1568, 384, 384), ("mkv", 1952, 256, 256), ("mkr", 2208, 64, 128),
    ("cb", 2272, 512, 512), ("cc", 2784, 512, 512), ("cx", 3296, 512, 512),
    ("pad", 3808, 0, 128), ("z", 3808, 2048, 2048),
)
PROJ_AL = sum(s[3] for s in PROJ_SEGS)

VMEM_LIMIT = 48 * 1024 * 1024
BLOCK_BYTES = 2 * 1024 * 1024


def _params(sem=None):
    return pltpu.CompilerParams(dimension_semantics=sem, vmem_limit_bytes=VMEM_LIMIT)


def _dot(a, b, ca, cb, precision=None):
    return lax.dot_general(a, b, (((ca,), (cb,)), ((), ())), preferred_element_type=F32, precision=precision)


def _tile(dim, prefs):
    for t in prefs:
        if dim % t == 0:
            return t
    return dim


def _pick_rows(rows, width, itemsize=4):
    for t in (2048, 1024, 512, 256, 128, 64, 32, 16, 8):
        if rows % t == 0 and t * width * itemsize <= BLOCK_BYTES:
            return t
    return rows


def _mm(a, b, *, ta=False, tb=False, bias=None, out_dtype=F32, name="mm"):
    if ta:
        K, M = a.shape
    else:
        M, K = a.shape
    if tb:
        N, Kb = b.shape
    else:
        Kb, N = b.shape
    assert K == Kb, (a.shape, b.shape, ta, tb)
    tm = _tile(M, (512, 256, 128))
    tn = _tile(N, (1024, 512, 384, 256, 128))
    tk = _tile(K, (512, 256, 128))
    nk = K // tk
    has_bias = bias is not None

    def body(*refs):
        if has_bias:
            a_ref, b_ref, bias_ref, o_ref, acc_ref = refs
        else:
            a_ref, b_ref, o_ref, acc_ref = refs
        k = pl.program_id(2)

        @pl.when(k == 0)
        def _():
            acc_ref[...] = jnp.zeros_like(acc_ref)

        acc_ref[...] += _dot(a_ref[...].astype(BF16), b_ref[...].astype(BF16), 0 if ta else 1, 1 if tb else 0)

        @pl.when(k == nk - 1)
        def _():
            r = acc_ref[...]
            if has_bias:
                r = r + bias_ref[...]
            o_ref[...] = r.astype(out_dtype)

    a_spec = pl.BlockSpec((tk, tm), lambda i, j, k: (k, i)) if ta else pl.BlockSpec((tm, tk), lambda i, j, k: (i, k))
    b_spec = pl.BlockSpec((tn, tk), lambda i, j, k: (j, k)) if tb else pl.BlockSpec((tk, tn), lambda i, j, k: (k, j))
    in_specs = [a_spec, b_spec]
    args = [a, b]
    if has_bias:
        in_specs.append(pl.BlockSpec((1, tn), lambda i, j, k: (0, j)))
        args.append(bias)
    return pl.pallas_call(
        body, name=name, grid=(M // tm, N // tn, nk),
        in_specs=in_specs, out_specs=pl.BlockSpec((tm, tn), lambda i, j, k: (i, j)),
        out_shape=jax.ShapeDtypeStruct((M, N), out_dtype),
        scratch_shapes=[pltpu.VMEM((tm, tn), F32)],
        compiler_params=_params(("parallel", "parallel", "arbitrary")),
    )(*args)


@jax.custom_vjp
def mm(a, b):
    return _mm(a, b, name="mm_fwd")


def _mm_f(a, b):
    return _mm(a, b, name="mm_fwd"), (a, b)


def _mm_b(res, g):
    a, b = res
    return _mm(g, b, tb=True, out_dtype=a.dtype, name="mm_da"), _mm(a, g, ta=True, out_dtype=b.dtype, name="mm_db")


mm.defvjp(_mm_f, _mm_b)


def _rows(body, name, tiled, full, tiled_out, acc_out, tr=None):
    rows = tiled[0].shape[0]
    if tr is None:
        width = max([a.shape[1] for a in tiled] + [s.shape[1] for s in tiled_out])
        tr = _pick_rows(rows, width)
    in_specs = [pl.BlockSpec((tr, a.shape[1]), lambda i: (i, 0)) for a in tiled]
    in_specs += [pl.BlockSpec(a.shape, lambda i: (0, 0)) for a in full]
    out_specs = [pl.BlockSpec((tr, s.shape[1]), lambda i: (i, 0)) for s in tiled_out]
    out_specs += [pl.BlockSpec(s.shape, lambda i: (0, 0)) for s in acc_out]

    def wrapped(*refs):
        body(pl.program_id(0), *refs)

    outs = pl.pallas_call(
        wrapped, name=name, grid=(rows // tr,), in_specs=in_specs, out_specs=out_specs,
        out_shape=list(tiled_out) + list(acc_out),
        compiler_params=_params(("arbitrary",)),
    )(*tiled, *full)
    return outs


def _sds(shape, dtype=F32):
    return jax.ShapeDtypeStruct(tuple(shape), dtype)


def _acc(step, ref, val):
    @pl.when(step == 0)
    def _():
        ref[...] = val

    @pl.when(step != 0)
    def _():
        ref[...] += val


def _colsum(v):
    return jnp.sum(v, axis=0, keepdims=True)


def _rstd(x):
    return lax.rsqrt(jnp.mean(x * x, axis=-1, keepdims=True) + EPS)


@jax.custom_vjp
def rmsnorm(x, g):
    def body(i, x_ref, g_ref, o_ref):
        x = x_ref[...]
        o_ref[...] = x * _rstd(x) * g_ref[...]
    return _rows(body, "rmsnorm_fwd", [x], [g], [_sds(x.shape)], [])[0]


def _rmsnorm_f(x, g):
    return rmsnorm(x, g), (x, g)


def _rmsnorm_b(res, dy):
    x, g = res

    def body(i, x_ref, dy_ref, g_ref, dx_ref, dg_ref):
        x = x_ref[...]
        dy = dy_ref[...]
        r = _rstd(x)
        xh = x * r
        dxh = dy * g_ref[...]
        dx_ref[...] = r * (dxh - xh * jnp.mean(dxh * xh, axis=-1, keepdims=True))
        _acc(i, dg_ref, _colsum(dy * xh))

    dx, dg = _rows(body, "rmsnorm_bwd", [x, dy], [g], [_sds(x.shape)], [_sds(g.shape)])
    return dx, dg


rmsnorm.defvjp(_rmsnorm_f, _rmsnorm_b)


def _modulate(x, g, scale, shift):
    def body(i, x_ref, g_ref, sc_ref, sh_ref, o_ref):
        x = x_ref[...]
        xn = x * _rstd(x) * g_ref[...]
        o_ref[...] = (xn * (1.0 + sc_ref[...]) + sh_ref[...]).astype(BF16)
    return _rows(body, "modulate_fwd", [x], [g, scale, shift], [_sds(x.shape, BF16)], [])[0]


def _modulate_bwd(x, g, scale, shift, dh):
    def body(i, x_ref, dh_ref, g_ref, sc_ref, dx_ref, dg_ref, dsc_ref, dsh_ref):
        x = x_ref[...]
        dh = dh_ref[...]
        gv = g_ref[...]
        r = _rstd(x)
        xh = x * r
        dxn = dh * (1.0 + sc_ref[...])
        dxh = dxn * gv
        dx_ref[...] = r * (dxh - xh * jnp.mean(dxh * xh, axis=-1, keepdims=True))
        _acc(i, dg_ref, _colsum(dxn * xh))
        _acc(i, dsc_ref, _colsum(dh * (xh * gv)))
        _acc(i, dsh_ref, _colsum(dh))

    v = _sds(g.shape)
    return _rows(body, "modulate_bwd", [x, dh], [g, scale], [_sds(x.shape)], [v, v, v])


@jax.custom_vjp
def mod_mm(x, g, scale, shift, w):
    return _mm(_modulate(x, g, scale, shift), w, name="mm_in")


def _mod_mm_f(x, g, scale, shift, w):
    h = _modulate(x, g, scale, shift)
    return _mm(h, w, name="mm_in"), (x, g, scale, shift, w, h)


def _mod_mm_b(res, dproj):
    x, g, scale, shift, w, h = res
    dh = _mm(dproj, w, tb=True, name="mm_in_dh")
    dw = _mm(h, dproj, ta=True, out_dtype=w.dtype, name="mm_in_dw")
    dx, dg, dsc, dsh = _modulate_bwd(x, g, scale, shift, dh)
    return dx, dg, dsc, dsh, dw


mod_mm.defvjp(_mod_mm_f, _mod_mm_b)


def _sigmoid(z):
    return 1.0 / (1.0 + jnp.exp(-z))


def _gate_mul(o, z):
    def body(i, o_ref, z_ref, y_ref):
        z = z_ref[...]
        y_ref[...] = (o_ref[...] * (z * _sigmoid(z))).astype(BF16)
    return _rows(body, "gate_mul_fwd", [o, z], [], [_sds(o.shape, BF16)], [])[0]


def _gate_mul_bwd(o, z, dy):
    def body(i, o_ref, z_ref, dy_ref, do_ref, dz_ref):
        z = z_ref[...]
        dy = dy_ref[...]
        s = _sigmoid(z)
        do_ref[...] = dy * (z * s)
        dz_ref[...] = dy * o_ref[...] * (s * (1.0 + z * (1.0 - s)))
    return _rows(body, "gate_mul_bwd", [o, z, dy], [], [_sds(o.shape), _sds(o.shape)], [])


@jax.custom_vjp
def gated_out(o, z, w):
    return _mm(_gate_mul(o, z), w, name="mm_out")


def _gated_out_f(o, z, w):
    y = _gate_mul(o, z)
    return _mm(y, w, name="mm_out"), (o, z, w, y)


def _gated_out_b(res, du):
    o, z, w, y = res
    dy = _mm(du, w, tb=True, name="mm_out_dy")
    dw = _mm(y, du, ta=True, out_dtype=w.dtype, name="mm_out_dw")
    do, dz = _gate_mul_bwd(o, z, dy)
    return do, dz, dw


gated_out.defvjp(_gated_out_f, _gated_out_b)


@jax.custom_vjp
def residual(x, u, gate):
    def body(i, x_ref, u_ref, g_ref, o_ref):
        o_ref[...] = x_ref[...] + g_ref[...] * u_ref[...]
    return _rows(body, "residual_fwd", [x, u], [gate], [_sds(x.shape)], [])[0]


def _residual_f(x, u, gate):
    return residual(x, u, gate), (u, gate)


def _residual_b(res, d):
    u, gate = res

    def body(i, d_ref, u_ref, g_ref, du_ref, dg_ref):
        d = d_ref[...]
        du_ref[...] = g_ref[...] * d
        _acc(i, dg_ref, _colsum(d * u_ref[...]))

    du, dg = _rows(body, "residual_bwd", [d, u], [gate], [_sds(u.shape)], [_sds(gate.shape)])
    return d, du, dg


residual.defvjp(_residual_f, _residual_b)


@jax.custom_vjp
def gate_act(u, b):
    def body(i, u_ref, b_ref, o_ref):
        t = u_ref[...] + b_ref[...]
        o_ref[...] = (jnp.minimum(t, 0.0) - jnp.log(1.0 + jnp.exp(-jnp.abs(t)))) / GLA_TEMP
    return _rows(body, "gate_act_fwd", [u], [b], [_sds(u.shape)], [])[0]


def _gate_act_f(u, b):
    return gate_act(u, b), (u, b)


def _gate_act_b(res, d):
    u, b = res

    def body(i, u_ref, d_ref, b_ref, du_ref, db_ref):
        t = u_ref[...] + b_ref[...]
        du = d_ref[...] * _sigmoid(-t) / GLA_TEMP
        du_ref[...] = du
        _acc(i, db_ref, _colsum(du))

    du, db = _rows(body, "gate_act_bwd", [u, d], [b], [_sds(u.shape)], [_sds(b.shape)])
    return du, db


gate_act.defvjp(_gate_act_f, _gate_act_b)


@jax.custom_vjp
def fma(a, b, c, d):
    def body(i, a_ref, b_ref, c_ref, d_ref, o_ref):
        o_ref[...] = a_ref[...] * b_ref[...] + c_ref[...] * d_ref[...]
    return _rows(body, "fma_fwd", [a, b, c, d], [], [_sds(a.shape)], [])[0]


def _fma_f(a, b, c, d):
    return fma(a, b, c, d), (b, d)


def _fma_b(res, g):
    b, d = res

    def body(i, g_ref, b_ref, d_ref, da_ref, dc_ref):
        g = g_ref[...]
        da_ref[...] = g * b_ref[...]
        dc_ref[...] = g * d_ref[...]

    da, dc = _rows(body, "fma_bwd", [g, b, d], [], [_sds(g.shape), _sds(g.shape)], [])
    return da, jnp.zeros_like(b), dc, jnp.zeros_like(d)


fma.defvjp(_fma_f, _fma_b)


def _silu_rows(c):
    def body(i, c_ref, o_ref):
        v = c_ref[...]
        o_ref[...] = v * _sigmoid(v)
    return _rows(body, "silu", [c], [], [_sds(c.shape)], [])[0]


@jax.custom_vjp
def loss_op(y, t):
    return _loss_fwd(y, t)[0]


def _loss_fwd(y, t):
    inv = 1.0 / y.shape[1]

    def body(i, y_ref, t_ref, d_ref, l_ref):
        e = y_ref[...] - t_ref[...]
        d_ref[...] = e * inv
        _acc(i, l_ref, jnp.sum(_colsum(e * e), axis=1, keepdims=True) * (0.5 * inv))

    d, l = _rows(body, "loss_fwd", [y, t], [], [_sds(y.shape)], [_sds((1, 1))])
    return l, d


def _loss_f(y, t):
    l, d = _loss_fwd(y, t)
    return l, d


def _loss_b(d, g):
    return d * g, jnp.zeros_like(d)


loss_op.defvjp(_loss_f, _loss_b)


def _conv_terms(cc, cx, rows, n):
    u = cc * cx
    up = jnp.where(rows == 0, 0.0, pltpu.roll(u, 1, 0))
    un = jnp.where(rows == n - 1, 0.0, pltpu.roll(u, n - 1, 0))
    return u, up, un


CONV_COLS = 128


def _conv_specs(s, n_in):
    blk = pl.BlockSpec((s, CONV_COLS), lambda j: (0, j))
    wblk = pl.BlockSpec((8, CONV_COLS), lambda j: (0, j))
    return [blk] * n_in + [wblk], blk, wblk


@jax.custom_vjp
def conv_op(cb, cc, cx, w):
    s, ch = cb.shape

    def body(cb_ref, cc_ref, cx_ref, w_ref, o_ref):
        rows = lax.broadcasted_iota(jnp.int32, (s, CONV_COLS), 0)
        u, up, un = _conv_terms(cc_ref[...], cx_ref[...], rows, s)
        conv = up * w_ref[0:1, :] + u * w_ref[1:2, :] + un * w_ref[2:3, :]
        o_ref[...] = cb_ref[...] * conv

    in_specs, blk, _ = _conv_specs(s, 3)
    return pl.pallas_call(
        body, name="conv_fwd", grid=(ch // CONV_COLS,), in_specs=in_specs, out_specs=blk,
        out_shape=_sds(cb.shape), compiler_params=_params(("parallel",)),
    )(cb, cc, cx, w)


def _conv_f(cb, cc, cx, w):
    return conv_op(cb, cc, cx, w), (cb, cc, cx, w)


def _conv_b(res, d):
    cb, cc, cx, w = res
    s, ch = cb.shape

    def body(cb_ref, cc_ref, cx_ref, d_ref, w_ref, dcb_ref, dcc_ref, dcx_ref, dw_ref):
        rows = lax.broadcasted_iota(jnp.int32, (s, CONV_COLS), 0)
        cc_v = cc_ref[...]
        cx_v = cx_ref[...]
        u, up, un = _conv_terms(cc_v, cx_v, rows, s)
        w0, w1, w2 = w_ref[0:1, :], w_ref[1:2, :], w_ref[2:3, :]
        dv = d_ref[...]
        dcb_ref[...] = dv * (up * w0 + u * w1 + un * w2)
        dconv = dv * cb_ref[...]
        d_next = jnp.where(rows == s - 1, 0.0, pltpu.roll(dconv, s - 1, 0))
        d_prev = jnp.where(rows == 0, 0.0, pltpu.roll(dconv, 1, 0))
        du = w0 * d_next + w1 * dconv + w2 * d_prev
        dcc_ref[...] = du * cx_v
        dcx_ref[...] = du * cc_v
        dw_ref[...] = jnp.zeros_like(dw_ref)
        dw_ref[0:1, :] = _colsum(dconv * up)
        dw_ref[1:2, :] = _colsum(dconv * u)
        dw_ref[2:3, :] = _colsum(dconv * un)

    in_specs, blk, wblk = _conv_specs(s, 4)
    v = _sds(cb.shape)
    return tuple(pl.pallas_call(
        body, name="conv_bwd", grid=(ch // CONV_COLS,), in_specs=in_specs, out_specs=[blk, blk, blk, wblk],
        out_shape=[v, v, v, _sds(w.shape)], compiler_params=_params(("parallel",)),
    )(cb, cc, cx, d, w))


conv_op.defvjp(_conv_f, _conv_b)


def _gla_masks(rev):
    c = GLA_CHUNK
    row = lax.broadcasted_iota(jnp.int32, (c, c), 0)
    col = lax.broadcasted_iota(jnp.int32, (c, c), 1)
    tri = jnp.where((row <= col) if rev else (row >= col), 1.0, 0.0).astype(F32)
    mask = (row < col) if rev else (row >= col)
    return tri, mask


def _gla_chunk(q_ref, k_ref, v_ref, g_ref, n, rev, tri):
    c = GLA_CHUNK
    rows = pl.ds(pl.multiple_of(n * c, c), c)
    q = q_ref[0, rows, :] * (GLA_DK ** -0.5)
    k = k_ref[0, rows, :]
    v = v_ref[0, rows, :]
    g = g_ref[0, rows, :]
    b = _dot(tri, g, 1, 0, HIGHEST)
    bt = _colsum(g)
    return rows, q, k, v, b, bt


def _gla_fwd(q, k, v, gf, gb):
    h, s, dk = q.shape
    dv = v.shape[2]
    c = GLA_CHUNK
    n_chunks = s // c

    def body(q_ref, k_ref, v_ref, gf_ref, gb_ref, o_ref, sf_ref, sb_ref):
        for rev, g_ref, s_ref in ((False, gf_ref, sf_ref), (True, gb_ref, sb_ref)):
            tri, mask = _gla_masks(rev)

            def step(i, st, rev=rev, g_ref=g_ref, s_ref=s_ref, tri=tri, mask=mask):
                n = (n_chunks - 1 - i) if rev else i
                rows, qv, kv, vv, b, bt = _gla_chunk(q_ref, k_ref, v_ref, g_ref, n, rev, tri)
                vb = vv.astype(BF16)
                qd = (qv * jnp.exp(b)).astype(BF16)
                ki = (kv * jnp.exp(-b)).astype(BF16)
                ke = (kv * jnp.exp(bt - b)).astype(BF16)
                a = jnp.where(mask, _dot(qd, ki, 1, 1), 0.0).astype(BF16)
                o = _dot(a, vb, 1, 0) + _dot(qd, st.astype(BF16), 1, 1)
                if rev:
                    o_ref[0, rows, :] += o
                else:
                    o_ref[0, rows, :] = o
                s_ref[0, n] = st
                return st * jnp.exp(bt) + _dot(vb, ke, 0, 0)

            lax.fori_loop(0, n_chunks, step, jnp.zeros((dv, dk), F32), unroll=2)

    blk_k = pl.BlockSpec((1, s, dk), lambda i: (i, 0, 0))
    blk_v = pl.BlockSpec((1, s, dv), lambda i: (i, 0, 0))
    blk_s = pl.BlockSpec((1, n_chunks, dv, dk), lambda i: (i, 0, 0, 0))
    st = _sds((h, n_chunks, dv, dk))
    return pl.pallas_call(
        body, name="gla_fwd", grid=(h,), in_specs=[blk_k, blk_k, blk_v, blk_k, blk_k],
        out_specs=[blk_v, blk_s, blk_s], out_shape=[_sds(v.shape), st, st],
        compiler_params=_params(("parallel",)),
    )(q, k, v, gf, gb)


def _gla_bwd(q, k, v, gf, gb, sf, sb, do):
    h, s, dk = q.shape
    dv = v.shape[2]
    c = GLA_CHUNK
    n_chunks = s // c
    scale = GLA_DK ** -0.5

    def body(q_ref, k_ref, v_ref, gf_ref, gb_ref, sf_ref, sb_ref, do_ref, dq_ref, dk_ref, dv_ref, dgf_ref, dgb_ref,
             dst_ref):
        rowc = lax.broadcasted_iota(jnp.int32, (c, dk), 0)
        for rev, g_ref, s_ref, dg_ref in ((False, gf_ref, sf_ref, dgf_ref), (True, gb_ref, sb_ref, dgb_ref)):
            tri, mask = _gla_masks(rev)
            dst_ref[...] = jnp.zeros_like(dst_ref)

            def step(i, carry, rev=rev, g_ref=g_ref, s_ref=s_ref, dg_ref=dg_ref, tri=tri, mask=mask):
                dst = dst_ref[...]
                n = i if rev else (n_chunks - 1 - i)
                rows, qv, kv, vv, b, bt = _gla_chunk(q_ref, k_ref, v_ref, g_ref, n, rev, tri)
                eb = jnp.exp(b)
                enb = jnp.exp(-b)
                etb = jnp.exp(bt - b)
                ebt = jnp.exp(bt)
                qd = qv * eb
                ki = kv * enb
                ke = kv * etb
                qd_b, ki_b, ke_b = qd.astype(BF16), ki.astype(BF16), ke.astype(BF16)
                st = s_ref[0, n]
                vb = vv.astype(BF16)
                do_b = do_ref[0, rows, :].astype(BF16)
                dst_b = dst.astype(BF16)
                a = jnp.where(mask, _dot(qd_b, ki_b, 1, 1), 0.0).astype(BF16)
                da = jnp.where(mask, _dot(do_b, vb, 1, 1), 0.0).astype(BF16)
                dvv = _dot(a, do_b, 0, 0) + _dot(ke_b, dst_b, 1, 1)
                dqd = _dot(da, ki_b, 1, 0) + _dot(do_b, st.astype(BF16), 1, 0)
                dki = _dot(da, qd_b, 0, 0)
                dke = _dot(vb, dst_b, 1, 0)
                dbt = _colsum(st * dst) * ebt + _colsum(dke * ke)
                db = dqd * qd - dki * ki - dke * ke
                db = db + jnp.where(rowc == (0 if rev else c - 1), dbt, 0.0)
                dg_ref[0, rows, :] = _dot(tri, db, 0, 0, HIGHEST)
                dqv = dqd * eb * scale
                dkv = dki * enb + dke * etb
                if rev:
                    dq_ref[0, rows, :] += dqv
                    dk_ref[0, rows, :] += dkv
                    dv_ref[0, rows, :] += dvv
                else:
                    dq_ref[0, rows, :] = dqv
                    dk_ref[0, rows, :] = dkv
                    dv_ref[0, rows, :] = dvv
                dst_ref[...] = _dot(do_b, qd_b, 0, 0) + dst * ebt
                return carry

            lax.fori_loop(0, n_chunks, step, 0, unroll=2)

    blk_k = pl.BlockSpec((1, s, dk), lambda i: (i, 0, 0))
    blk_v = pl.BlockSpec((1, s, dv), lambda i: (i, 0, 0))
    blk_s = pl.BlockSpec((1, n_chunks, dv, dk), lambda i: (i, 0, 0, 0))
    vk, vv = _sds(q.shape), _sds(v.shape)
    return pl.pallas_call(
        body, name="gla_bwd", grid=(h,), in_specs=[blk_k, blk_k, blk_v, blk_k, blk_k, blk_s, blk_s, blk_v],
        out_specs=[blk_k, blk_k, blk_v, blk_k, blk_k], out_shape=[vk, vk, vv, vk, vk],
        scratch_shapes=[pltpu.VMEM((dv, dk), F32)], compiler_params=_params(("parallel",)),
    )(q, k, v, gf, gb, sf, sb, do)


@jax.custom_vjp
def gla(q, k, v, gf, gb):
    return _gla_fwd(q, k, v, gf, gb)[0]


def _gla_f(q, k, v, gf, gb):
    o, sf, sb = _gla_fwd(q, k, v, gf, gb)
    return o, (q, k, v, gf, gb, sf, sb)


def _gla_b(res, do):
    return tuple(_gla_bwd(*res, do))


gla.defvjp(_gla_f, _gla_b)


ATTN_TQ = 256


def _attn_fwd(q, k, v):
    h, s, dq = q.shape
    dv = v.shape[2]
    tq = min(ATTN_TQ, s)
    scale = (MLA_NOPE + MLA_ROPE) ** -0.5

    def body(q_ref, k_ref, v_ref, o_ref, lse_ref):
        sc = _dot(q_ref[0].astype(BF16), k_ref[0].astype(BF16), 1, 1) * scale
        m = jnp.max(sc, axis=-1, keepdims=True)
        p = jnp.exp(sc - m)
        l = jnp.sum(p, axis=-1, keepdims=True)
        p = p * (1.0 / l)
        o_ref[0] = _dot(p.astype(BF16), v_ref[0].astype(BF16), 1, 0)
        lse_ref[0] = m + jnp.log(l)

    return pl.pallas_call(
        body, name="attn_fwd", grid=(h, s // tq),
        in_specs=[pl.BlockSpec((1, tq, dq), lambda i, j: (i, j, 0)), pl.BlockSpec((1, s, dq), lambda i, j: (i, 0, 0)),
                  pl.BlockSpec((1, s, dv), lambda i, j: (i, 0, 0))],
        out_specs=[pl.BlockSpec((1, tq, dv), lambda i, j: (i, j, 0)), pl.BlockSpec((1, tq, 1), lambda i, j: (i, j, 0))],
        out_shape=[_sds((h, s, dv)), _sds((h, s, 1))],
        compiler_params=_params(("parallel", "parallel")),
    )(q, k, v)


def _attn_bwd(q, k, v, o, lse, do):
    h, s, dq = q.shape
    dv = v.shape[2]
    tq = min(ATTN_TQ, s)
    scale = (MLA_NOPE + MLA_ROPE) ** -0.5

    def body(q_ref, k_ref, v_ref, o_ref, lse_ref, do_ref, dq_ref, dk_ref, dv_ref):
        j = pl.program_id(1)
        qb = q_ref[0].astype(BF16)
        kb = k_ref[0].astype(BF16)
        do = do_ref[0]
        do_b = do.astype(BF16)
        p = jnp.exp(_dot(qb, kb, 1, 1) * scale - lse_ref[0])
        dp = _dot(do_b, v_ref[0].astype(BF16), 1, 1)
        delta = jnp.sum(do * o_ref[0], axis=-1, keepdims=True)
        ds = (p * (dp - delta) * scale).astype(BF16)
        dq_ref[0] = _dot(ds, kb, 1, 0)
        dk_c = _dot(ds, qb, 0, 0)
        dv_c = _dot(p.astype(BF16), do_b, 0, 0)

        @pl.when(j == 0)
        def _():
            dk_ref[0] = dk_c
            dv_ref[0] = dv_c

        @pl.when(j != 0)
        def _():
            dk_ref[0] += dk_c
            dv_ref[0] += dv_c

    qblk = pl.BlockSpec((1, tq, dq), lambda i, j: (i, j, 0))
    kblk = pl.BlockSpec((1, s, dq), lambda i, j: (i, 0, 0))
    vblk = pl.BlockSpec((1, s, dv), lambda i, j: (i, 0, 0))
    oblk = pl.BlockSpec((1, tq, dv), lambda i, j: (i, j, 0))
    lblk = pl.BlockSpec((1, tq, 1), lambda i, j: (i, j, 0))
    return pl.pallas_call(
        body, name="attn_bwd", grid=(h, s // tq),
        in_specs=[qblk, kblk, vblk, oblk, lblk, oblk], out_specs=[qblk, kblk, vblk],
        out_shape=[_sds(q.shape), _sds(k.shape), _sds(v.shape)],
        compiler_params=_params(("parallel", "arbitrary")),
    )(q, k, v, o, lse, do)


@jax.custom_vjp
def attn(q, k, v):
    return _attn_fwd(q, k, v)[0]


def _attn_f(q, k, v):
    o, lse = _attn_fwd(q, k, v)
    return o, (q, k, v, o, lse)


def _attn_b(res, do):
    return tuple(_attn_bwd(*res, do))


attn.defvjp(_attn_f, _attn_b)


@jax.custom_vjp
def split_proj(proj):
    out, at = [], 0
    for _, _, _, wp in PROJ_SEGS:
        out.append(proj[:, at:at + wp])
        at += wp
    return tuple(out)


def _split_f(proj):
    return split_proj(proj), None


def _split_b(_, gs):
    return (jnp.concatenate(gs, axis=1),)


split_proj.defvjp(_split_f, _split_b)


def _add_pair(stacked, theirs, c_idx):
    shp = theirs.shape
    w = shp[-1]
    a3, b2 = stacked.reshape(2, -1, w), theirs.reshape(-1, w)
    r = b2.shape[0]
    tr = _pick_rows(r, w)

    def body(c_ref, a_ref, b_ref, o_ref):
        o_ref[...] = (a_ref[0].astype(F32) + b_ref[...].astype(F32)).astype(BF16)

    spec = pltpu.PrefetchScalarGridSpec(
        num_scalar_prefetch=1, grid=(r // tr,),
        in_specs=[pl.BlockSpec((1, tr, w), lambda i, c: (c[0], i, 0)), pl.BlockSpec((tr, w), lambda i, c: (i, 0))],
        out_specs=pl.BlockSpec((tr, w), lambda i, c: (i, 0)))
    return pl.pallas_call(body, name="add_pair", grid_spec=spec, out_shape=_sds(b2.shape, BF16),
                          compiler_params=_params(("parallel",)))(c_idx, a3, b2).reshape(shp)


def _add_chips(pair, landed, chip_idx):
    shp = pair.shape[1:]
    w = shp[-1]
    p3, l3 = pair.reshape(N_CHIPS, -1, w), landed.reshape(N_CHIPS - 1, -1, w)
    r = p3.shape[1]
    tr = _pick_rows(r, w)

    def body(c_ref, p_ref, l0_ref, l1_ref, l2_ref, o_ref):
        o_ref[...] = ((p_ref[0].astype(F32) + l0_ref[0].astype(F32)) + l1_ref[0].astype(F32)) + l2_ref[0].astype(F32)

    specs = [pl.BlockSpec((1, tr, w), lambda i, c: (c[0], i, 0))]
    specs += [pl.BlockSpec((1, tr, w), functools.partial(lambda i, c, j: (j, i, 0), j=j)) for j in range(N_CHIPS - 1)]
    spec = pltpu.PrefetchScalarGridSpec(num_scalar_prefetch=1, grid=(r // tr,), in_specs=specs,
                                        out_specs=pl.BlockSpec((tr, w), lambda i, c: (i, 0)))
    return pl.pallas_call(body, name="add_chips", grid_spec=spec, out_shape=_sds((r, w)),
                          compiler_params=_params(("parallel",)))(chip_idx, p3, l3, l3, l3).reshape(shp)


def _sum_devices(g):
    n = g.shape[2]

    def body(g_ref, o_ref):
        t = g_ref[0]
        for j in range(1, N_DEV):
            t = t + g_ref[j]
        o_ref[...] = t

    return pl.pallas_call(body, name="sum_devices", out_shape=_sds((1, n)), compiler_params=_params())(g)


def _adamw(w, g, m, v):
    shp = w.shape
    w2, g2, m2, v2 = (t.reshape(-1, shp[-1]) for t in (w, g, m, v))
    c1 = 1.0 - ADAM_B1 ** ADAM_STEP
    c2 = 1.0 - ADAM_B2 ** ADAM_STEP

    def body(i, w_ref, g_ref, m_ref, v_ref, d_ref, mo_ref, vo_ref):
        gv = g_ref[...]
        mn = ADAM_B1 * m_ref[...] + (1.0 - ADAM_B1) * gv
        vn = ADAM_B2 * v_ref[...] + (1.0 - ADAM_B2) * (gv * gv)
        d_ref[...] = -ADAM_LR * ((mn / c1) / (jnp.sqrt(vn / c2) + ADAM_EPS) + ADAM_WD * w_ref[...])
        mo_ref[...] = mn
        vo_ref[...] = vn

    s2 = _sds(w2.shape)
    tr = _pick_rows(w2.shape[0], w2.shape[1], 8)
    d, mn, vn = _rows(body, "adamw", [w2, g2, m2, v2], [], [s2, s2, s2], [], tr=tr)
    return d.reshape(shp), mn.reshape(shp), vn.reshape(shp)


ANY = pl.BlockSpec(memory_space=pl.ANY)
PIECE_BYTES = 1 << 20


def _place():
    return lax.axis_index("x"), lax.axis_index("y"), lax.axis_index("c")


def _pieces(shape, itemsize):
    if len(shape) >= 3:
        return [(i,) + p for i in range(shape[0]) for p in _pieces(shape[1:], itemsize)]
    rows = shape[0]
    row_bytes = itemsize
    for dsz in shape[1:]:
        row_bytes *= dsz
    k = 1
    while rows % (2 * k) == 0 and (rows // (2 * k)) % 16 == 0 and (rows // k) * row_bytes > PIECE_BYTES:
        k *= 2
    step = rows // k
    return [(pl.ds(j * step, step),) for j in range(k)]


def _split_start(make, src, dst, pieces):
    for p in pieces:
        make(src.at[p], dst.at[p]).start()
    return make(src, dst)


def _comm_call(body, name, arrs, out_shapes, n_remote, n_local):
    return pl.pallas_call(
        body, name=name, in_specs=[ANY] * len(arrs), out_specs=[ANY] * len(out_shapes), out_shape=out_shapes,
        scratch_shapes=[pltpu.SemaphoreType.DMA((n_remote,)), pltpu.SemaphoreType.DMA((n_remote,)),
                        pltpu.SemaphoreType.DMA((n_local,))],
    )(*arrs)


def all_gather8(arrs, name):
    n = len(arrs)
    pieces = [_pieces(a.shape, a.dtype.itemsize) for a in arrs]

    def body(*refs):
        ins, outs = refs[:n], refs[n:2 * n]
        send, recv, lsem = refs[2 * n:]
        x, y, c = _place()
        me, sib = (x, y, c), (x, y, 1 - c)
        chips = [(1 - x, y), (x, 1 - y), (1 - x, 1 - y)]

        def slot(p):
            return 4 * p[0] + 2 * p[1] + p[2]

        def maker(t, k, to):
            def make(s, d):
                return pltpu.make_async_remote_copy(src_ref=s, dst_ref=d, send_sem=send.at[7 * t + k],
                                                    recv_sem=recv.at[7 * t + k], device_id=to, device_id_type=MESH)
            return make

        def landing(t, k, block):
            dst = outs[t].at[slot(block)]
            return maker(t, k, me)(dst, dst)

        mine, sent = [], []
        for t in range(n):
            dst = outs[t].at[slot(me)]
            mine.append(_split_start(lambda s, d, t=t: pltpu.make_async_copy(s, d, lsem.at[t]), ins[t], dst, pieces[t]))
            sent.append(_split_start(maker(t, 0, sib), ins[t], dst, pieces[t]))
            for j, chip in enumerate(chips):
                sent.append(_split_start(maker(t, 1 + j, (*chip, c)), ins[t], dst, pieces[t]))
        for j, chip in enumerate(chips):
            for t in range(n):
                landing(t, 1 + j, (*chip, c)).wait_recv()
                blk = outs[t].at[slot((*chip, c))]
                sent.append(_split_start(maker(t, 4 + j, sib), blk, blk, pieces[t]))
        for t in range(n):
            landing(t, 0, sib).wait_recv()
            for j, chip in enumerate(chips):
                landing(t, 4 + j, (*chip, 1 - c)).wait_recv()
        for cp in sent:
            cp.wait_send()
        for cp in mine:
            cp.wait()

    outs = [_sds((N_DEV,) + a.shape, a.dtype) for a in arrs]
    return _comm_call(body, name, arrs, outs, 7 * n, n)


def sibling_send(arrs, name):
    n = len(arrs)
    pieces = [_pieces(a.shape[1:], a.dtype.itemsize) for a in arrs]

    def body(*refs):
        ins, theirs = refs[:n], refs[n:2 * n]
        send, recv, _ = refs[2 * n:]
        x, y, c = _place()
        rem = []
        for t in range(n):
            def make(s, d, t=t):
                return pltpu.make_async_remote_copy(src_ref=s, dst_ref=d, send_sem=send.at[t], recv_sem=recv.at[t],
                                                    device_id=(x, y, 1 - c), device_id_type=MESH)
            rem.append(_split_start(make, ins[t].at[1 - c], theirs[t], pieces[t]))
        for cp in rem:
            cp.wait_recv()
        for cp in rem:
            cp.wait_send()

    outs = [_sds(a.shape[1:], a.dtype) for a in arrs]
    return _comm_call(body, name, arrs, outs, n, 1)


def exchange_chips(arrs, name):
    n = len(arrs)
    pieces = [_pieces(a.shape[1:], a.dtype.itemsize) for a in arrs]

    def body(*refs):
        ins, outs = refs[:n], refs[n:2 * n]
        send, recv, _ = refs[2 * n:]
        x, y, c = _place()
        peers = [(1 - x, y), (x, 1 - y), (1 - x, 1 - y)]
        rem = []
        for t in range(n):
            for j, (px, py) in enumerate(peers):
                def make(s, d, t=t, j=j, px=px, py=py):
                    return pltpu.make_async_remote_copy(
                        src_ref=s, dst_ref=d, send_sem=send.at[3 * t + j], recv_sem=recv.at[3 * t + j],
                        device_id=(px, py, c), device_id_type=MESH)
                rem.append(_split_start(make, ins[t].at[2 * px + py], outs[t].at[j], pieces[t]))
        for cp in rem:
            cp.wait_recv()
        for cp in rem:
            cp.wait_send()

    outs = [_sds((N_CHIPS - 1,) + a.shape[1:], a.dtype) for a in arrs]
    return _comm_call(body, name, arrs, outs, 3 * n, 1)


def sibling_all_gather(arrs, name):
    n = len(arrs)
    pieces = [_pieces(a.shape, a.dtype.itemsize) for a in arrs]

    def body(*refs):
        ins, outs = refs[:n], refs[n:2 * n]
        send, recv, lsem = refs[2 * n:]
        x, y, c = _place()
        loc, rem = [], []
        for t in range(n):
            def make(s, d, t=t):
                return pltpu.make_async_remote_copy(src_ref=s, dst_ref=d, send_sem=send.at[t], recv_sem=recv.at[t],
                                                    device_id=(x, y, 1 - c), device_id_type=MESH)
            loc.append(_split_start(lambda s, d, t=t: pltpu.make_async_copy(s, d, lsem.at[t]), ins[t], outs[t].at[c],
                                    pieces[t]))
            rem.append(_split_start(make, ins[t], outs[t].at[c], pieces[t]))
        for t in range(n):
            got = outs[t].at[1 - c]
            pltpu.make_async_remote_copy(src_ref=got, dst_ref=got, send_sem=send.at[t], recv_sem=recv.at[t],
                                         device_id=(x, y, 1 - c), device_id_type=MESH).wait_recv()
        for cp in rem:
            cp.wait_send()
        for cp in loc:
            cp.wait()

    outs = [_sds((2,) + a.shape, a.dtype) for a in arrs]
    return _comm_call(body, name, arrs, outs, n, n)


def _build_w_in(w4):
    full = jnp.concatenate([w4[j] for j in range(N_CHIPS)], axis=1)
    parts = []
    for _, start, width, wp in PROJ_SEGS:
        if width:
            parts.append(full[:, start:start + width])
        if wp > width:
            parts.append(jnp.zeros((full.shape[0], wp - width), full.dtype))
    return jnp.concatenate(parts, axis=1)


def _permute_w_uq(w):
    w3 = w.reshape(w.shape[0], MLA_HEADS, MLA_NOPE + MLA_ROPE)
    hr = MLA_ROPE // 2
    return jnp.concatenate([
        w3[:, :, :MLA_NOPE].reshape(w.shape[0], -1),
        w3[:, :, MLA_NOPE:MLA_NOPE + hr].reshape(w.shape[0], -1),
        w3[:, :, MLA_NOPE + hr:].reshape(w.shape[0], -1)], axis=1)


def _heads(t, d):
    return t.reshape(t.shape[0], -1, d).transpose(1, 0, 2)


def _unheads(t):
    return t.transpose(1, 0, 2).reshape(t.shape[1], -1)


def _layer(xh, mod, big, small, rope_q, rope_k):
    s = xh.shape[0]
    d = D_MODEL
    shift, scale, gate = mod[None, 0:d], mod[None, d:2 * d], mod[None, 2 * d:3 * d]
    w_al = _build_w_in(big["w_in"])
    proj = mod_mm(xh, small["norm_g"][None], scale, shift, w_al)
    gq, gk, gv, glr, mq, mkv, mkr, cb, cc, cx, _, z = split_proj(proj)

    rk = GLA_RANK
    hk = GLA_HEADS * GLA_DK
    wg = jnp.zeros((128, 2 * hk), F32)
    wg = wg.at[0:rk, 0:hk].set(small["gla_wg_f"]).at[rk:2 * rk, hk:].set(small["gla_wg_b"])
    bg = jnp.concatenate([small["gla_bg_f"], small["gla_bg_b"]])[None]
    la = gate_act(mm(glr, wg), bg)
    o_gla = gla(_heads(gq, GLA_DK), _heads(gk, GLA_DK), _heads(gv, GLA_DV),
                _heads(la[:, :hk], GLA_DK), _heads(la[:, hk:], GLA_DK))
    o_gla = rmsnorm(o_gla.reshape(GLA_HEADS * s, GLA_DV), small["gla_norm_g"][None])
    o_gla = _unheads(o_gla.reshape(GLA_HEADS, s, GLA_DV))

    nh = MLA_HEADS * MLA_NOPE
    hr = MLA_HEADS * MLA_ROPE // 2
    cq = rmsnorm(mq, small["mla_q_norm_g"][None])
    qm = mm(cq, _permute_w_uq(jnp.concatenate([big["w_uq"][j] for j in range(N_CHIPS)], axis=1)))
    qr = qm[:, nh:]
    qr = fma(qr, rope_q[0], jnp.concatenate([qr[:, hr:], qr[:, :hr]], axis=1), rope_q[1])
    ckv = rmsnorm(mkv, small["mla_kv_norm_g"][None])
    kv = mm(ckv, jnp.concatenate([big["w_ukv"][j] for j in range(N_CHIPS)], axis=1))
    kr = mkr[:, :MLA_ROPE]
    kr = fma(kr, rope_k[0], jnp.concatenate([kr[:, MLA_ROPE // 2:], kr[:, :MLA_ROPE // 2]], axis=1), rope_k[1])
    q3 = jnp.concatenate([qm[:, :nh].reshape(s, MLA_HEADS, MLA_NOPE), qr[:, :hr].reshape(s, MLA_HEADS, -1),
                          qr[:, hr:].reshape(s, MLA_HEADS, -1)], axis=-1).transpose(1, 0, 2)
    kv3 = kv.reshape(s, MLA_HEADS, MLA_NOPE + MLA_DV)
    k3 = jnp.concatenate([kv3[:, :, :MLA_NOPE], jnp.broadcast_to(kr[:, None, :], (s, MLA_HEADS, MLA_ROPE))],
                         axis=-1).transpose(1, 0, 2)
    v3 = kv3[:, :, MLA_NOPE:].transpose(1, 0, 2)
    o_mla = rmsnorm(_unheads(attn(q3, k3, v3)), small["mla_out_g"][None])

    cw = jnp.concatenate([small["conv_w"], jnp.zeros((5, CONV_CH), F32)], axis=0)
    o_conv = rmsnorm(conv_op(cb, cc, cx, cw), small["conv_out_g"][None])

    o = jnp.concatenate([o_gla, o_mla, o_conv], axis=1)
    w_out = big["w_out"].reshape(d, d)
    return residual(xh, gated_out(o, z, w_out), gate)


def _loss_fn(xh, mods, bigs, smalls, final_g, target, rope_q, rope_k):
    h = xh
    for l in range(DEPTH):
        h = _layer(h, mods[l], bigs[l], smalls[l], rope_q, rope_k)
    return loss_op(rmsnorm(h, final_g[None]), target)[0, 0]


SMALL_REPL = ("norm_g", "gla_bg_f", "gla_bg_b", "gla_norm_g", "mla_q_norm_g", "mla_kv_norm_g", "mla_out_g",
              "conv_out_g")
SMALL_SHARDED = ("gla_wg_f", "gla_wg_b", "conv_w")
BIG = ("w_in", "w_out", "w_uq", "w_ukv")


def kernel(x, c, positions, ada_w, ada_b, norm_g, w_in, gla_wg_f, gla_bg_f, gla_wg_b, gla_bg_b, gla_norm_g, mla_q_norm_g, mla_kv_norm_g, mla_w_uq, mla_w_ukv, mla_out_g, conv_w, conv_out_g, w_out, final_g, loss_target, m_ada_w, m_ada_b, m_norm_g, m_w_in, m_gla_wg_f, m_gla_bg_f, m_gla_wg_b, m_gla_bg_b, m_gla_norm_g, m_mla_q_norm_g, m_mla_kv_norm_g, m_mla_w_uq, m_mla_w_ukv, m_mla_out_g, m_conv_w, m_conv_out_g, m_w_out, m_final_g, v_ada_w, v_ada_b, v_norm_g, v_w_in, v_gla_wg_f, v_gla_bg_f, v_gla_wg_b, v_gla_bg_b, v_gla_norm_g, v_mla_q_norm_g, v_mla_kv_norm_g, v_mla_w_uq, v_mla_w_ukv, v_mla_out_g, v_conv_w, v_conv_out_g, v_w_out, v_final_g):
    xi, yi, ci = _place()
    chip = 2 * xi + yi
    dev = 2 * chip + ci
    s = x.shape[1]
    d = D_MODEL
    weights = dict(ada_w=ada_w, ada_b=ada_b, norm_g=norm_g, w_in=w_in, gla_wg_f=gla_wg_f, gla_bg_f=gla_bg_f,
                   gla_wg_b=gla_wg_b, gla_bg_b=gla_bg_b, gla_norm_g=gla_norm_g, mla_q_norm_g=mla_q_norm_g,
                   mla_kv_norm_g=mla_kv_norm_g, mla_w_uq=mla_w_uq, mla_w_ukv=mla_w_ukv, mla_out_g=mla_out_g,
                   conv_w=conv_w, conv_out_g=conv_out_g, w_out=w_out, final_g=final_g)
    m_in = dict(ada_w=m_ada_w, ada_b=m_ada_b, norm_g=m_norm_g, w_in=m_w_in, gla_wg_f=m_gla_wg_f, gla_bg_f=m_gla_bg_f,
                gla_wg_b=m_gla_wg_b, gla_bg_b=m_gla_bg_b, gla_norm_g=m_gla_norm_g, mla_q_norm_g=m_mla_q_norm_g,
                mla_kv_norm_g=m_mla_kv_norm_g, mla_w_uq=m_mla_w_uq, mla_w_ukv=m_mla_w_ukv, mla_out_g=m_mla_out_g,
                conv_w=m_conv_w, conv_out_g=m_conv_out_g, w_out=m_w_out, final_g=m_final_g)
    v_in = dict(ada_w=v_ada_w, ada_b=v_ada_b, norm_g=v_norm_g, w_in=v_w_in, gla_wg_f=v_gla_wg_f, gla_bg_f=v_gla_bg_f,
                gla_wg_b=v_gla_wg_b, gla_bg_b=v_gla_bg_b, gla_norm_g=v_gla_norm_g, mla_q_norm_g=v_mla_q_norm_g,
                mla_kv_norm_g=v_mla_kv_norm_g, mla_w_uq=v_mla_w_uq, mla_w_ukv=v_mla_w_ukv, mla_out_g=v_mla_out_g,
                conv_w=v_conv_w, conv_out_g=v_conv_out_g, w_out=v_w_out, final_g=v_final_g)

    def mine_bf16(w):
        return lax.dynamic_index_in_dim(w, ci, 0, keepdims=False).astype(BF16)

    g_c, g_in, g_out, g_uq, g_ukv, g_wgf, g_wgb, g_cw = all_gather8(
        [c, mine_bf16(w_in), mine_bf16(w_out), mine_bf16(mla_w_uq), mine_bf16(mla_w_ukv), gla_wg_f, gla_wg_b, conv_w],
        "gather_weights")

    def by_layer(g):
        g4 = g.reshape((N_CHIPS, 2) + g.shape[1:])
        return [g4[:, l] for l in range(DEPTH)]

    bigs = [dict(w_in=a, w_out=b, w_uq=u, w_ukv=k)
            for a, b, u, k in zip(by_layer(g_in), by_layer(g_out), by_layer(g_uq), by_layer(g_ukv))]

    def unshard_cols(g):
        g4 = g[0::2]
        return g4.transpose(1, 2, 0, 3).reshape(g4.shape[1], g4.shape[2], -1)

    small_full = dict(gla_wg_f=unshard_cols(g_wgf), gla_wg_b=unshard_cols(g_wgb), conv_w=unshard_cols(g_cw))
    for nme in SMALL_REPL:
        small_full[nme] = weights[nme]
    smalls = [{nme: small_full[nme][l] for nme in SMALL_REPL + SMALL_SHARDED} for l in range(DEPTH)]

    c_act = _silu_rows(g_c[:, 0, :])
    c_act16 = jnp.concatenate([c_act, jnp.zeros_like(c_act)], axis=0)
    n_ada = ada_w.shape[2]
    parts = []
    for l in range(DEPTH):
        bias = lax.dynamic_slice_in_dim(ada_b[l], chip * n_ada, n_ada)[None]
        parts.append(_mm(c_act16, ada_w[l], bias=bias, name="ada_fwd"))
    g_mod, = all_gather8([jnp.stack(parts)], "gather_mod")
    mod_mine = lax.dynamic_index_in_dim(g_mod[0::2], dev, 2, keepdims=False)
    mods = mod_mine.transpose(1, 0, 2).reshape(DEPTH, 3 * d)

    inv_freq = ROPE_THETA ** (-jnp.arange(0, MLA_ROPE, 2, dtype=F32) / MLA_ROPE)
    ang = positions[0].astype(F32)[:, None] * inv_freq
    cos, sin = jnp.cos(ang), jnp.sin(ang)
    cos_h, sin_h = jnp.tile(cos, (1, MLA_HEADS)), jnp.tile(sin, (1, MLA_HEADS))
    rope_q = (jnp.concatenate([cos_h, cos_h], axis=1), jnp.concatenate([-sin_h, sin_h], axis=1))
    rope_k = (jnp.concatenate([cos, cos], axis=1), jnp.concatenate([-sin, sin], axis=1))

    loss_dev, (dx, dmods, dbigs, dsmalls, dfinal) = jax.value_and_grad(_loss_fn, argnums=(0, 1, 2, 3, 4))(
        x[0], mods, bigs, smalls, final_g, loss_target[0], rope_q, rope_k)
    loss = lax.psum(loss_dev, ("x", "y", "c"))

    pieces = [dmods.reshape(-1), dfinal]
    for nme in SMALL_REPL + SMALL_SHARDED:
        pieces.append(jnp.stack([dsmalls[l][nme] for l in range(DEPTH)]).reshape(-1))
    sizes = [p.shape[0] for p in pieces]
    flat = jnp.concatenate(pieces)
    padn = (-flat.shape[0]) % 128
    flat = jnp.concatenate([flat, jnp.zeros((padn,), F32)])[None]
    g_small, = all_gather8([flat], "gather_small_grads")
    total = _sum_devices(g_small)[0]
    offs, at = [], 0
    for n_el in sizes:
        offs.append(at)
        at += n_el

    def piece(i, shape):
        return total[offs[i]:offs[i] + sizes[i]].reshape(shape)

    grads = {"ada_b": piece(0, (DEPTH, 3 * d)), "final_g": piece(1, (d,))}
    for i, nme in enumerate(SMALL_REPL + SMALL_SHARDED):
        full = piece(2 + i, small_full[nme].shape)
        if nme in SMALL_SHARDED:
            ncol = weights[nme].shape[2]
            full = lax.dynamic_slice_in_dim(full, chip * ncol, ncol, axis=2)
        grads[nme] = full

    dmod_all = g_small[:, 0, :DEPTH * 3 * d].reshape(N_DEV, DEPTH, 3 * d)
    dmod_cols = lax.dynamic_slice_in_dim(dmod_all, chip * n_ada, n_ada, axis=2)
    g_ada = []
    for l in range(DEPTH):
        dm16 = jnp.concatenate([dmod_cols[:, l], jnp.zeros((N_DEV, n_ada), F32)], axis=0)
        g_ada.append(_mm(c_act16, dm16, ta=True, name="ada_bwd"))
    grads["ada_w"] = jnp.stack(g_ada)

    stacked = [jnp.stack([dbigs[l][nme] for l in range(DEPTH)]) for nme in BIG]
    c_idx = jnp.reshape(ci, (1,)).astype(jnp.int32)
    chip_idx = jnp.reshape(chip, (1,)).astype(jnp.int32)
    theirs = sibling_send(stacked, "reduce_sibling")
    pair = [_add_pair(a, b, c_idx) for a, b in zip(stacked, theirs)]
    landed = exchange_chips(pair, "reduce_chips")
    reduced = [_add_chips(p, q, chip_idx) for p, q in zip(pair, landed)]
    both = sibling_all_gather(reduced, "share_sibling")
    for nme, g in zip(("w_in", "w_out", "mla_w_uq", "mla_w_ukv"), both):
        grads[nme] = g

    order = list(weights)
    delta, new_m, new_v = {}, {}, {}
    for nme in order:
        delta[nme], new_m[nme], new_v[nme] = _adamw(weights[nme], grads[nme], m_in[nme], v_in[nme])
    return (loss, dx[None], *[grads[n_] for n_ in order], *[delta[n_] for n_ in order],
            *[new_m[n_] for n_ in order], *[new_v[n_] for n_ in order])
```

```python
import functools

import jax
import jax.numpy as jnp
from jax import lax
from jax.experimental import pallas as pl
from jax.experimental.pallas import tpu as pltpu

F32 = jnp.float32
BF16 = jnp.bfloat16
MESH = pl.DeviceIdType.MESH
HIGHEST = lax.Precision.HIGHEST

DEPTH = 2
D_MODEL = 2048
GLA_HEADS = 6
GLA_DK = 64
GLA_DV = 128
GLA_RANK = 16
GLA_TEMP = 16.0
GLA_CHUNK = 64
GLA_W = GLA_HEADS * GLA_DV
MLA_HEADS = 6
MLA_QL = 384
MLA_KVL = 256
MLA_NOPE = 128
MLA_ROPE = 64
MLA_DV = 128
MLA_W = MLA_HEADS * MLA_DV
CONV_CH = D_MODEL - GLA_W - MLA_W
ROPE_THETA = 10000.0
EPS = 1e-6
IN_DIM = 5856
N_CHIPS = 4
N_DEV = 8

ADAM_LR = 0.001
ADAM_B1 = 0.9
ADAM_B2 = 0.999
ADAM_EPS = 1e-08
ADAM_WD = 0.01
ADAM_STEP = 10

PROJ_SEGS = (
    ("gq", 0, 384, 384), ("gk", 384, 384, 384), ("gv", 768, 768, 768), ("glr", 1536, 32, 128),
    ("mq", 1568, 384, 384), ("mkv", 1952, 256, 256), ("mkr", 2208, 64, 128),
    ("cb", 2272, 512, 512), ("cc", 2784, 512, 512), ("cx", 3296, 512, 512),
    ("pad", 3808, 0, 128), ("z", 3808, 2048, 2048),
)
PROJ_AL = sum(s[3] for s in PROJ_SEGS)

VMEM_LIMIT = 48 * 1024 * 1024
BLOCK_BYTES = 2 * 1024 * 1024


def _params(sem=None):
    return pltpu.CompilerParams(dimension_semantics=sem, vmem_limit_bytes=VMEM_LIMIT)


def _dot(a, b, ca, cb, precision=None):
    return lax.dot_general(a, b, (((ca,), (cb,)), ((), ())), preferred_element_type=F32, precision=precision)


def _tile(dim, prefs):
    for t in prefs:
        if dim % t == 0:
            return t
    return dim


def _pick_rows(rows, width, itemsize=4):
    for t in (2048, 1024, 512, 256, 128, 64, 32, 16, 8):
        if rows % t == 0 and t * width * itemsize <= BLOCK_BYTES:
            return t
    return rows


def _mm(a, b, *, ta=False, tb=False, bias=None, out_dtype=F32, name="mm"):
    if ta:
        K, M = a.shape
    else:
        M, K = a.shape
    if tb:
        N, Kb = b.shape
    else:
        Kb, N = b.shape
    assert K == Kb, (a.shape, b.shape, ta, tb)
    tm = _tile(M, (512, 256, 128))
    tn = _tile(N, (1024, 512, 384, 256, 128))
    tk = _tile(K, (2048, 1024, 512, 256, 128))
    nk = K // tk
    has_bias = bias is not None

    def body(*refs):
        a_ref, b_ref = refs[0], refs[1]
        bias_ref = refs[2] if has_bias else None
        o_ref = refs[3 if has_bias else 2]
        part = _dot(a_ref[...].astype(BF16), b_ref[...].astype(BF16), 0 if ta else 1, 1 if tb else 0)

        def finish(r):
            if has_bias:
                r = r + bias_ref[...]
            o_ref[...] = r.astype(out_dtype)

        if nk == 1:
            finish(part)
            return
        acc_ref = refs[-1]
        k = pl.program_id(2)

        @pl.when(k == 0)
        def _():
            acc_ref[...] = part

        @pl.when(k != 0)
        def _():
            acc_ref[...] += part

        @pl.when(k == nk - 1)
        def _():
            finish(acc_ref[...])

    a_spec = pl.BlockSpec((tk, tm), lambda i, j, k: (k, i)) if ta else pl.BlockSpec((tm, tk), lambda i, j, k: (i, k))
    b_spec = pl.BlockSpec((tn, tk), lambda i, j, k: (j, k)) if tb else pl.BlockSpec((tk, tn), lambda i, j, k: (k, j))
    in_specs = [a_spec, b_spec]
    args = [a, b]
    if has_bias:
        in_specs.append(pl.BlockSpec((1, tn), lambda i, j, k: (0, j)))
        args.append(bias)
    return pl.pallas_call(
        body, name=name, grid=(M // tm, N // tn, nk),
        in_specs=in_specs, out_specs=pl.BlockSpec((tm, tn), lambda i, j, k: (i, j)),
        out_shape=jax.ShapeDtypeStruct((M, N), out_dtype),
        scratch_shapes=[pltpu.VMEM((tm, tn), F32)] if nk > 1 else [],
        compiler_params=_params(("parallel", "parallel", "arbitrary")),
    )(*args)


@jax.custom_vjp
def mm(a, b):
    return _mm(a, b, name="mm_fwd")


def _mm_f(a, b):
    return _mm(a, b, name="mm_fwd"), (a, b)


def _mm_b(res, g):
    a, b = res
    return _mm(g, b, tb=True, out_dtype=a.dtype, name="mm_da"), _mm(a, g, ta=True, out_dtype=b.dtype, name="mm_db")


mm.defvjp(_mm_f, _mm_b)


def _rows(body, name, tiled, full, tiled_out, acc_out, tr=None):
    rows = tiled[0].shape[0]
    if tr is None:
        width = max([a.shape[1] for a in tiled] + [s.shape[1] for s in tiled_out])
        tr = _pick_rows(rows, width)
    in_specs = [pl.BlockSpec((tr, a.shape[1]), lambda i: (i, 0)) for a in tiled]
    in_specs += [pl.BlockSpec(a.shape, lambda i: (0, 0)) for a in full]
    out_specs = [pl.BlockSpec((tr, s.shape[1]), lambda i: (i, 0)) for s in tiled_out]
    out_specs += [pl.BlockSpec(s.shape, lambda i: (0, 0)) for s in acc_out]

    def wrapped(*refs):
        body(pl.program_id(0), *refs)

    outs = pl.pallas_call(
        wrapped, name=name, grid=(rows // tr,), in_specs=in_specs, out_specs=out_specs,
        out_shape=list(tiled_out) + list(acc_out),
        compiler_params=_params(("arbitrary",)),
    )(*tiled, *full)
    return outs


def _sds(shape, dtype=F32):
    return jax.ShapeDtypeStruct(tuple(shape), dtype)


def _acc(step, ref, val):
    @pl.when(step == 0)
    def _():
        ref[...] = val

    @pl.when(step != 0)
    def _():
        ref[...] += val


def _colsum(v):
    return jnp.sum(v, axis=0, keepdims=True)


def _rstd(x):
    return lax.rsqrt(jnp.mean(x * x, axis=-1, keepdims=True) + EPS)


@jax.custom_vjp
def rmsnorm(x, g):
    def body(i, x_ref, g_ref, o_ref):
        x = x_ref[...]
        o_ref[...] = x * _rstd(x) * g_ref[...]
    return _rows(body, "rmsnorm_fwd", [x], [g], [_sds(x.shape)], [])[0]


def _rmsnorm_f(x, g):
    return rmsnorm(x, g), (x, g)


def _rmsnorm_b(res, dy):
    x, g = res

    def body(i, x_ref, dy_ref, g_ref, dx_ref, dg_ref):
        x = x_ref[...]
        dy = dy_ref[...]
        r = _rstd(x)
        xh = x * r
        dxh = dy * g_ref[...]
        dx_ref[...] = r * (dxh - xh * jnp.mean(dxh * xh, axis=-1, keepdims=True))
        _acc(i, dg_ref, _colsum(dy * xh))

    dx, dg = _rows(body, "rmsnorm_bwd", [x, dy], [g], [_sds(x.shape)], [_sds(g.shape)])
    return dx, dg


rmsnorm.defvjp(_rmsnorm_f, _rmsnorm_b)


def _modulate(x, g, scale, shift):
    def body(i, x_ref, g_ref, sc_ref, sh_ref, o_ref):
        x = x_ref[...]
        xn = x * _rstd(x) * g_ref[...]
        o_ref[...] = (xn * (1.0 + sc_ref[...]) + sh_ref[...]).astype(BF16)
    return _rows(body, "modulate_fwd", [x], [g, scale, shift], [_sds(x.shape, BF16)], [])[0]


def _modulate_bwd(x, g, scale, shift, dh):
    def body(i, x_ref, dh_ref, g_ref, sc_ref, dx_ref, dg_ref, dsc_ref, dsh_ref):
        x = x_ref[...]
        dh = dh_ref[...]
        gv = g_ref[...]
        r = _rstd(x)
        xh = x * r
        dxn = dh * (1.0 + sc_ref[...])
        dxh = dxn * gv
        dx_ref[...] = r * (dxh - xh * jnp.mean(dxh * xh, axis=-1, keepdims=True))
        _acc(i, dg_ref, _colsum(dxn * xh))
        _acc(i, dsc_ref, _colsum(dh * (xh * gv)))
        _acc(i, dsh_ref, _colsum(dh))

    v = _sds(g.shape)
    return _rows(body, "modulate_bwd", [x, dh], [g, scale], [_sds(x.shape)], [v, v, v])


@jax.custom_vjp
def mod_mm(x, g, scale, shift, w):
    return _mm(_modulate(x, g, scale, shift), w, name="mm_in")


def _mod_mm_f(x, g, scale, shift, w):
    h = _modulate(x, g, scale, shift)
    return _mm(h, w, name="mm_in"), (x, g, scale, shift, w, h)


def _mod_mm_b(res, dproj):
    x, g, scale, shift, w, h = res
    dproj = dproj.astype(BF16)
    dh = _mm(dproj, w, tb=True, name="mm_in_dh")
    dw = _mm(h, dproj, ta=True, out_dtype=w.dtype, name="mm_in_dw")
    dx, dg, dsc, dsh = _modulate_bwd(x, g, scale, shift, dh)
    return dx, dg, dsc, dsh, dw


mod_mm.defvjp(_mod_mm_f, _mod_mm_b)


def _sigmoid(z):
    return 1.0 / (1.0 + jnp.exp(-z))


def _gate_mul(o, z):
    def body(i, o_ref, z_ref, y_ref):
        z = z_ref[...]
        y_ref[...] = (o_ref[...] * (z * _sigmoid(z))).astype(BF16)
    return _rows(body, "gate_mul_fwd", [o, z], [], [_sds(o.shape, BF16)], [])[0]


def _gate_mul_bwd(o, z, dy):
    def body(i, o_ref, z_ref, dy_ref, do_ref, dz_ref):
        z = z_ref[...]
        dy = dy_ref[...]
        s = _sigmoid(z)
        do_ref[...] = dy * (z * s)
        dz_ref[...] = dy * o_ref[...] * (s * (1.0 + z * (1.0 - s)))
    return _rows(body, "gate_mul_bwd", [o, z, dy], [], [_sds(o.shape), _sds(o.shape)], [])


def _residual(x, u, gate):
    def body(i, x_ref, u_ref, g_ref, o_ref):
        o_ref[...] = x_ref[...] + g_ref[...] * u_ref[...]
    return _rows(body, "residual_fwd", [x, u], [gate], [_sds(x.shape)], [])[0]


def _residual_bwd(d, u, gate):
    def body(i, d_ref, u_ref, g_ref, du_ref, dg_ref):
        d = d_ref[...]
        du_ref[...] = (g_ref[...] * d).astype(BF16)
        _acc(i, dg_ref, _colsum(d * u_ref[...]))

    return _rows(body, "residual_bwd", [d, u], [gate], [_sds(u.shape, BF16)], [_sds(gate.shape)])


@jax.custom_vjp
def out_block(o, z, w, x, gate):
    return _residual(x, _mm(_gate_mul(o, z), w, name="mm_out"), gate)


def _out_block_f(o, z, w, x, gate):
    y = _gate_mul(o, z)
    u = _mm(y, w, name="mm_out")
    return _residual(x, u, gate), (o, z, w, y, u, gate)


def _out_block_b(res, d):
    o, z, w, y, u, gate = res
    du, dgate = _residual_bwd(d, u, gate)
    dy = _mm(du, w, tb=True, name="mm_out_dy")
    dw = _mm(y, du, ta=True, out_dtype=w.dtype, name="mm_out_dw")
    do, dz = _gate_mul_bwd(o, z, dy)
    return do, dz, dw, d, dgate


out_block.defvjp(_out_block_f, _out_block_b)


@jax.custom_vjp
def gate_act(u, b):
    def body(i, u_ref, b_ref, o_ref):
        t = u_ref[...] + b_ref[...]
        o_ref[...] = (jnp.minimum(t, 0.0) - jnp.log(1.0 + jnp.exp(-jnp.abs(t)))) / GLA_TEMP
    return _rows(body, "gate_act_fwd", [u], [b], [_sds(u.shape)], [])[0]


def _gate_act_f(u, b):
    return gate_act(u, b), (u, b)


def _gate_act_b(res, d):
    u, b = res

    def body(i, u_ref, d_ref, b_ref, du_ref, db_ref):
        t = u_ref[...] + b_ref[...]
        du = d_ref[...] * _sigmoid(-t) / GLA_TEMP
        du_ref[...] = du
        _acc(i, db_ref, _colsum(du))

    du, db = _rows(body, "gate_act_bwd", [u, d], [b], [_sds(u.shape)], [_sds(b.shape)])
    return du, db


gate_act.defvjp(_gate_act_f, _gate_act_b)


@jax.custom_vjp
def fma(a, b, c, d):
    def body(i, a_ref, b_ref, c_ref, d_ref, o_ref):
        o_ref[...] = a_ref[...] * b_ref[...] + c_ref[...] * d_ref[...]
    return _rows(body, "fma_fwd", [a, b, c, d], [], [_sds(a.shape)], [])[0]


def _fma_f(a, b, c, d):
    return fma(a, b, c, d), (b, d)


def _fma_b(res, g):
    b, d = res

    def body(i, g_ref, b_ref, d_ref, da_ref, dc_ref):
        g = g_ref[...]
        da_ref[...] = g * b_ref[...]
        dc_ref[...] = g * d_ref[...]

    da, dc = _rows(body, "fma_bwd", [g, b, d], [], [_sds(g.shape), _sds(g.shape)], [])
    return da, jnp.zeros_like(b), dc, jnp.zeros_like(d)


fma.defvjp(_fma_f, _fma_b)


def _silu_rows(c):
    def body(i, c_ref, o_ref):
        v = c_ref[...]
        o_ref[...] = v * _sigmoid(v)
    return _rows(body, "silu", [c], [], [_sds(c.shape)], [])[0]


@jax.custom_vjp
def loss_op(y, t):
    return _loss_fwd(y, t)[0]


def _loss_fwd(y, t):
    inv = 1.0 / y.shape[1]

    def body(i, y_ref, t_ref, d_ref, l_ref):
        e = y_ref[...] - t_ref[...]
        d_ref[...] = e * inv
        _acc(i, l_ref, jnp.sum(_colsum(e * e), axis=1, keepdims=True) * (0.5 * inv))

    d, l = _rows(body, "loss_fwd", [y, t], [], [_sds(y.shape)], [_sds((1, 1))])
    return l, d


def _loss_f(y, t):
    l, d = _loss_fwd(y, t)
    return l, d


def _loss_b(d, g):
    return d * g, jnp.zeros_like(d)


loss_op.defvjp(_loss_f, _loss_b)


def _conv_terms(cc, cx, rows, n):
    u = cc * cx
    up = jnp.where(rows == 0, 0.0, pltpu.roll(u, 1, 0))
    un = jnp.where(rows == n - 1, 0.0, pltpu.roll(u, n - 1, 0))
    return u, up, un


CONV_COLS = 128


def _conv_specs(s, n_in):
    blk = pl.BlockSpec((s, CONV_COLS), lambda j: (0, j))
    wblk = pl.BlockSpec((8, CONV_COLS), lambda j: (0, j))
    return [blk] * n_in + [wblk], blk, wblk


@jax.custom_vjp
def conv_op(cb, cc, cx, w):
    s, ch = cb.shape

    def body(cb_ref, cc_ref, cx_ref, w_ref, o_ref):
        rows = lax.broadcasted_iota(jnp.int32, (s, CONV_COLS), 0)
        u, up, un = _conv_terms(cc_ref[...], cx_ref[...], rows, s)
        conv = up * w_ref[0:1, :] + u * w_ref[1:2, :] + un * w_ref[2:3, :]
        o_ref[...] = cb_ref[...] * conv

    in_specs, blk, _ = _conv_specs(s, 3)
    return pl.pallas_call(
        body, name="conv_fwd", grid=(ch // CONV_COLS,), in_specs=in_specs, out_specs=blk,
        out_shape=_sds(cb.shape), compiler_params=_params(("parallel",)),
    )(cb, cc, cx, w)


def _conv_f(cb, cc, cx, w):
    return conv_op(cb, cc, cx, w), (cb, cc, cx, w)


def _conv_b(res, d):
    cb, cc, cx, w = res
    s, ch = cb.shape

    def body(cb_ref, cc_ref, cx_ref, d_ref, w_ref, dcb_ref, dcc_ref, dcx_ref, dw_ref):
        rows = lax.broadcasted_iota(jnp.int32, (s, CONV_COLS), 0)
        cc_v = cc_ref[...]
        cx_v = cx_ref[...]
        u, up, un = _conv_terms(cc_v, cx_v, rows, s)
        w0, w1, w2 = w_ref[0:1, :], w_ref[1:2, :], w_ref[2:3, :]
        dv = d_ref[...]
        dcb_ref[...] = dv * (up * w0 + u * w1 + un * w2)
        dconv = dv * cb_ref[...]
        d_next = jnp.where(rows == s - 1, 0.0, pltpu.roll(dconv, s - 1, 0))
        d_prev = jnp.where(rows == 0, 0.0, pltpu.roll(dconv, 1, 0))
        du = w0 * d_next + w1 * dconv + w2 * d_prev
        dcc_ref[...] = du * cx_v
        dcx_ref[...] = du * cc_v
        dw_ref[...] = jnp.zeros_like(dw_ref)
        dw_ref[0:1, :] = _colsum(dconv * up)
        dw_ref[1:2, :] = _colsum(dconv * u)
        dw_ref[2:3, :] = _colsum(dconv * un)

    in_specs, blk, wblk = _conv_specs(s, 4)
    v = _sds(cb.shape)
    return tuple(pl.pallas_call(
        body, name="conv_bwd", grid=(ch // CONV_COLS,), in_specs=in_specs, out_specs=[blk, blk, blk, wblk],
        out_shape=[v, v, v, _sds(w.shape)], compiler_params=_params(("parallel",)),
    )(cb, cc, cx, d, w))


conv_op.defvjp(_conv_f, _conv_b)


def _gla_masks(rev):
    c = GLA_CHUNK
    row = lax.broadcasted_iota(jnp.int32, (c, c), 0)
    col = lax.broadcasted_iota(jnp.int32, (c, c), 1)
    tri = jnp.where((row <= col) if rev else (row >= col), 1.0, 0.0).astype(F32)
    mask = (row < col) if rev else (row >= col)
    return tri, mask


def _gla_chunk(q_ref, k_ref, v_ref, g_ref, n, rev, tri):
    c = GLA_CHUNK
    rows = pl.ds(pl.multiple_of(n * c, c), c)
    q = q_ref[0, rows, :] * (GLA_DK ** -0.5)
    k = k_ref[0, rows, :]
    v = v_ref[0, rows, :]
    g = g_ref[0, rows, :]
    b = _dot(tri, g, 1, 0, HIGHEST)
    bt = _colsum(g)
    return rows, q, k, v, b, bt


def _gla_fwd(q, k, v, gf, gb):
    h, s, dk = q.shape
    dv = v.shape[2]
    c = GLA_CHUNK
    n_chunks = s // c

    def body(q_ref, k_ref, v_ref, gf_ref, gb_ref, o_ref, sf_ref, sb_ref):
        for rev, g_ref, s_ref in ((False, gf_ref, sf_ref), (True, gb_ref, sb_ref)):
            tri, mask = _gla_masks(rev)

            def step(i, st, rev=rev, g_ref=g_ref, s_ref=s_ref, tri=tri, mask=mask):
                n = (n_chunks - 1 - i) if rev else i
                rows, qv, kv, vv, b, bt = _gla_chunk(q_ref, k_ref, v_ref, g_ref, n, rev, tri)
                vb = vv.astype(BF16)
                qd = (qv * jnp.exp(b)).astype(BF16)
                ki = (kv * jnp.exp(-b)).astype(BF16)
                ke = (kv * jnp.exp(bt - b)).astype(BF16)
                a = jnp.where(mask, _dot(qd, ki, 1, 1), 0.0).astype(BF16)
                o = _dot(a, vb, 1, 0) + _dot(qd, st.astype(BF16), 1, 1)
                if rev:
                    o_ref[0, rows, :] += o
                else:
                    o_ref[0, rows, :] = o
                s_ref[0, n] = st
                return st * jnp.exp(bt) + _dot(vb, ke, 0, 0)

            lax.fori_loop(0, n_chunks, step, jnp.zeros((dv, dk), F32), unroll=2)

    blk_k = pl.BlockSpec((1, s, dk), lambda i: (i, 0, 0))
    blk_v = pl.BlockSpec((1, s, dv), lambda i: (i, 0, 0))
    blk_s = pl.BlockSpec((1, n_chunks, dv, dk), lambda i: (i, 0, 0, 0))
    st = _sds((h, n_chunks, dv, dk))
    return pl.pallas_call(
        body, name="gla_fwd", grid=(h,), in_specs=[blk_k, blk_k, blk_v, blk_k, blk_k],
        out_specs=[blk_v, blk_s, blk_s], out_shape=[_sds(v.shape), st, st],
        compiler_params=_params(("parallel",)),
    )(q, k, v, gf, gb)


def _gla_bwd(q, k, v, gf, gb, sf, sb, do):
    h, s, dk = q.shape
    dv = v.shape[2]
    c = GLA_CHUNK
    n_chunks = s // c
    scale = GLA_DK ** -0.5

    def body(q_ref, k_ref, v_ref, gf_ref, gb_ref, sf_ref, sb_ref, do_ref, dq_ref, dk_ref, dv_ref, dgf_ref, dgb_ref,
             dst_ref):
        rowc = lax.broadcasted_iota(jnp.int32, (c, dk), 0)
        for rev, g_ref, s_ref, dg_ref in ((False, gf_ref, sf_ref, dgf_ref), (True, gb_ref, sb_ref, dgb_ref)):
            tri, mask = _gla_masks(rev)
            dst_ref[...] = jnp.zeros_like(dst_ref)

            def step(i, carry, rev=rev, g_ref=g_ref, s_ref=s_ref, dg_ref=dg_ref, tri=tri, mask=mask):
                dst = dst_ref[...]
                n = i if rev else (n_chunks - 1 - i)
                rows, qv, kv, vv, b, bt = _gla_chunk(q_ref, k_ref, v_ref, g_ref, n, rev, tri)
                eb = jnp.exp(b)
                enb = jnp.exp(-b)
                etb = jnp.exp(bt - b)
                ebt = jnp.exp(bt)
                qd = qv * eb
                ki = kv * enb
                ke = kv * etb
                qd_b, ki_b, ke_b = qd.astype(BF16), ki.astype(BF16), ke.astype(BF16)
                st = s_ref[0, n]
                vb = vv.astype(BF16)
                do_b = do_ref[0, rows, :].astype(BF16)
                dst_b = dst.astype(BF16)
                a = jnp.where(mask, _dot(qd_b, ki_b, 1, 1), 0.0).astype(BF16)
                da = jnp.where(mask, _dot(do_b, vb, 1, 1), 0.0).astype(BF16)
                dvv = _dot(a, do_b, 0, 0) + _dot(ke_b, dst_b, 1, 1)
                dqd = _dot(da, ki_b, 1, 0) + _dot(do_b, st.astype(BF16), 1, 0)
                dki = _dot(da, qd_b, 0, 0)
                dke = _dot(vb, dst_b, 1, 0)
                dbt = _colsum(st * dst) * ebt + _colsum(dke * ke)
                db = dqd * qd - dki * ki - dke * ke
                db = db + jnp.where(rowc == (0 if rev else c - 1), dbt, 0.0)
                dg_ref[0, rows, :] = _dot(tri, db, 0, 0, HIGHEST)
                dqv = dqd * eb * scale
                dkv = dki * enb + dke * etb
                if rev:
                    dq_ref[0, rows, :] += dqv
                    dk_ref[0, rows, :] += dkv
                    dv_ref[0, rows, :] += dvv
                else:
                    dq_ref[0, rows, :] = dqv
                    dk_ref[0, rows, :] = dkv
                    dv_ref[0, rows, :] = dvv
                dst_ref[...] = _dot(do_b, qd_b, 0, 0) + dst * ebt
                return carry

            lax.fori_loop(0, n_chunks, step, 0, unroll=2)

    blk_k = pl.BlockSpec((1, s, dk), lambda i: (i, 0, 0))
    blk_v = pl.BlockSpec((1, s, dv), lambda i: (i, 0, 0))
    blk_s = pl.BlockSpec((1, n_chunks, dv, dk), lambda i: (i, 0, 0, 0))
    vk, vv = _sds(q.shape), _sds(v.shape)
    return pl.pallas_call(
        body, name="gla_bwd", grid=(h,), in_specs=[blk_k, blk_k, blk_v, blk_k, blk_k, blk_s, blk_s, blk_v],
        out_specs=[blk_k, blk_k, blk_v, blk_k, blk_k], out_shape=[vk, vk, vv, vk, vk],
        scratch_shapes=[pltpu.VMEM((dv, dk), F32)], compiler_params=_params(("parallel",)),
    )(q, k, v, gf, gb, sf, sb, do)


@jax.custom_vjp
def gla(q, k, v, gf, gb):
    return _gla_fwd(q, k, v, gf, gb)[0]


def _gla_f(q, k, v, gf, gb):
    o, sf, sb = _gla_fwd(q, k, v, gf, gb)
    return o, (q, k, v, gf, gb, sf, sb)


def _gla_b(res, do):
    return tuple(_gla_bwd(*res, do))


gla.defvjp(_gla_f, _gla_b)


ATTN_TQ = 256


def _attn_fwd(q, k, v):
    h, s, dq = q.shape
    dv = v.shape[2]
    tq = min(ATTN_TQ, s)
    scale = (MLA_NOPE + MLA_ROPE) ** -0.5

    def body(q_ref, k_ref, v_ref, o_ref, lse_ref):
        sc = _dot(q_ref[0].astype(BF16), k_ref[0].astype(BF16), 1, 1) * scale
        m = jnp.max(sc, axis=-1, keepdims=True)
        p = jnp.exp(sc - m)
        l = jnp.sum(p, axis=-1, keepdims=True)
        p = p * (1.0 / l)
        o_ref[0] = _dot(p.astype(BF16), v_ref[0].astype(BF16), 1, 0)
        lse_ref[0] = m + jnp.log(l)

    return pl.pallas_call(
        body, name="attn_fwd", grid=(h, s // tq),
        in_specs=[pl.BlockSpec((1, tq, dq), lambda i, j: (i, j, 0)), pl.BlockSpec((1, s, dq), lambda i, j: (i, 0, 0)),
                  pl.BlockSpec((1, s, dv), lambda i, j: (i, 0, 0))],
        out_specs=[pl.BlockSpec((1, tq, dv), lambda i, j: (i, j, 0)), pl.BlockSpec((1, tq, 1), lambda i, j: (i, j, 0))],
        out_shape=[_sds((h, s, dv)), _sds((h, s, 1))],
        compiler_params=_params(("parallel", "parallel")),
    )(q, k, v)


def _attn_bwd(q, k, v, o, lse, do):
    h, s, dq = q.shape
    dv = v.shape[2]
    tq = min(ATTN_TQ, s)
    scale = (MLA_NOPE + MLA_ROPE) ** -0.5

    def body(q_ref, k_ref, v_ref, o_ref, lse_ref, do_ref, dq_ref, dk_ref, dv_ref):
        j = pl.program_id(1)
        qb = q_ref[0].astype(BF16)
        kb = k_ref[0].astype(BF16)
        do = do_ref[0]
        do_b = do.astype(BF16)
        p = jnp.exp(_dot(qb, kb, 1, 1) * scale - lse_ref[0])
        dp = _dot(do_b, v_ref[0].astype(BF16), 1, 1)
        delta = jnp.sum(do * o_ref[0], axis=-1, keepdims=True)
        ds = (p * (dp - delta) * scale).astype(BF16)
        dq_ref[0] = _dot(ds, kb, 1, 0)
        dk_c = _dot(ds, qb, 0, 0)
        dv_c = _dot(p.astype(BF16), do_b, 0, 0)

        @pl.when(j == 0)
        def _():
            dk_ref[0] = dk_c
            dv_ref[0] = dv_c

        @pl.when(j != 0)
        def _():
            dk_ref[0] += dk_c
            dv_ref[0] += dv_c

    qblk = pl.BlockSpec((1, tq, dq), lambda i, j: (i, j, 0))
    kblk = pl.BlockSpec((1, s, dq), lambda i, j: (i, 0, 0))
    vblk = pl.BlockSpec((1, s, dv), lambda i, j: (i, 0, 0))
    oblk = pl.BlockSpec((1, tq, dv), lambda i, j: (i, j, 0))
    lblk = pl.BlockSpec((1, tq, 1), lambda i, j: (i, j, 0))
    return pl.pallas_call(
        body, name="attn_bwd", grid=(h, s // tq),
        in_specs=[qblk, kblk, vblk, oblk, lblk, oblk], out_specs=[qblk, kblk, vblk],
        out_shape=[_sds(q.shape), _sds(k.shape), _sds(v.shape)],
        compiler_params=_params(("parallel", "arbitrary")),
    )(q, k, v, o, lse, do)


@jax.custom_vjp
def attn(q, k, v):
    return _attn_fwd(q, k, v)[0]


def _attn_f(q, k, v):
    o, lse = _attn_fwd(q, k, v)
    return o, (q, k, v, o, lse)


def _attn_b(res, do):
    return tuple(_attn_bwd(*res, do))


attn.defvjp(_attn_f, _attn_b)


@jax.custom_vjp
def split_proj(proj):
    out, at = [], 0
    for _, _, _, wp in PROJ_SEGS:
        out.append(proj[:, at:at + wp])
        at += wp
    return tuple(out)


def _split_f(proj):
    return split_proj(proj), None


def _split_b(_, gs):
    return (jnp.concatenate(gs, axis=1),)


split_proj.defvjp(_split_f, _split_b)


def _add_pair(stacked, theirs, c_idx):
    shp = theirs.shape
    w = shp[-1]
    a3, b2 = stacked.reshape(2, -1, w), theirs.reshape(-1, w)
    r = b2.shape[0]
    tr = _pick_rows(r, w)

    def body(c_ref, a_ref, b_ref, o_ref):
        o_ref[...] = (a_ref[0].astype(F32) + b_ref[...].astype(F32)).astype(BF16)

    spec = pltpu.PrefetchScalarGridSpec(
        num_scalar_prefetch=1, grid=(r // tr,),
        in_specs=[pl.BlockSpec((1, tr, w), lambda i, c: (c[0], i, 0)), pl.BlockSpec((tr, w), lambda i, c: (i, 0))],
        out_specs=pl.BlockSpec((tr, w), lambda i, c: (i, 0)))
    return pl.pallas_call(body, name="add_pair", grid_spec=spec, out_shape=_sds(b2.shape, BF16),
                          compiler_params=_params(("parallel",)))(c_idx, a3, b2).reshape(shp)


def _add_chips(pair, landed, chip_idx):
    shp = pair.shape[1:]
    w = shp[-1]
    p3, l3 = pair.reshape(N_CHIPS, -1, w), landed.reshape(N_CHIPS - 1, -1, w)
    r = p3.shape[1]
    tr = _pick_rows(r, w)

    def body(c_ref, p_ref, l0_ref, l1_ref, l2_ref, o_ref):
        o_ref[...] = ((p_ref[0].astype(F32) + l0_ref[0].astype(F32)) + l1_ref[0].astype(F32)) + l2_ref[0].astype(F32)

    specs = [pl.BlockSpec((1, tr, w), lambda i, c: (c[0], i, 0))]
    specs += [pl.BlockSpec((1, tr, w), functools.partial(lambda i, c, j: (j, i, 0), j=j)) for j in range(N_CHIPS - 1)]
    spec = pltpu.PrefetchScalarGridSpec(num_scalar_prefetch=1, grid=(r // tr,), in_specs=specs,
                                        out_specs=pl.BlockSpec((tr, w), lambda i, c: (i, 0)))
    return pl.pallas_call(body, name="add_chips", grid_spec=spec, out_shape=_sds((r, w)),
                          compiler_params=_params(("parallel",)))(chip_idx, p3, l3, l3, l3).reshape(shp)


def _sum_devices(g):
    n = g.shape[2]

    def body(g_ref, o_ref):
        t = g_ref[0]
        for j in range(1, N_DEV):
            t = t + g_ref[j]
        o_ref[...] = t

    return pl.pallas_call(body, name="sum_devices", out_shape=_sds((1, n)), compiler_params=_params())(g)


def _adamw(w, g, m, v):
    shp = w.shape
    shp3 = (1, 1, shp[0]) if len(shp) == 1 else (-1,) + tuple(shp[-2:])
    w3, g3, m3, v3 = (t.reshape(shp3) for t in (w, g, m, v))
    c1 = 1.0 - ADAM_B1 ** ADAM_STEP
    c2 = 1.0 - ADAM_B2 ** ADAM_STEP

    def body(w_ref, g_ref, m_ref, v_ref, d_ref, mo_ref, vo_ref):
        gv = g_ref[...]
        mn = ADAM_B1 * m_ref[...] + (1.0 - ADAM_B1) * gv
        vn = ADAM_B2 * v_ref[...] + (1.0 - ADAM_B2) * (gv * gv)
        d_ref[...] = -ADAM_LR * ((mn / c1) / (jnp.sqrt(vn / c2) + ADAM_EPS) + ADAM_WD * w_ref[...])
        mo_ref[...] = mn
        vo_ref[...] = vn

    nl, r, wd = w3.shape
    tr = _pick_rows(r, wd, 8)
    blk = pl.BlockSpec((1, tr, wd), lambda l, i: (l, i, 0))
    s3 = _sds(w3.shape)
    d, mn, vn = pl.pallas_call(
        body, name="adamw", grid=(nl, r // tr), in_specs=[blk] * 4, out_specs=[blk] * 3, out_shape=[s3, s3, s3],
        compiler_params=_params(("parallel", "parallel")),
    )(w3, g3, m3, v3)
    return d.reshape(shp), mn.reshape(shp), vn.reshape(shp)


ANY = pl.BlockSpec(memory_space=pl.ANY)
PIECE_BYTES = 1 << 20


def _place():
    return lax.axis_index("x"), lax.axis_index("y"), lax.axis_index("c")


def _pieces(shape, itemsize):
    if len(shape) >= 3:
        return [(i,) + p for i in range(shape[0]) for p in _pieces(shape[1:], itemsize)]
    rows = shape[0]
    row_bytes = itemsize
    for dsz in shape[1:]:
        row_bytes *= dsz
    k = 1
    while rows % (2 * k) == 0 and (rows // (2 * k)) % 16 == 0 and (rows // k) * row_bytes > PIECE_BYTES:
        k *= 2
    step = rows // k
    return [(pl.ds(j * step, step),) for j in range(k)]


def _split_start(make, src, dst, pieces):
    for p in pieces:
        make(src.at[p], dst.at[p]).start()
    return make(src, dst)


def _comm_call(body, name, arrs, out_shapes, n_remote, n_local):
    return pl.pallas_call(
        body, name=name, in_specs=[ANY] * len(arrs), out_specs=[ANY] * len(out_shapes), out_shape=out_shapes,
        scratch_shapes=[pltpu.SemaphoreType.DMA((n_remote,)), pltpu.SemaphoreType.DMA((n_remote,)),
                        pltpu.SemaphoreType.DMA((n_local,))],
    )(*arrs)


def all_gather8(arrs, name):
    n = len(arrs)
    pieces = [_pieces(a.shape, a.dtype.itemsize) for a in arrs]

    def body(*refs):
        ins, outs = refs[:n], refs[n:2 * n]
        send, recv, _ = refs[2 * n:]
        x, y, c = _place()
        me, sib = (x, y, c), (x, y, 1 - c)
        chips = [(1 - x, y), (x, 1 - y), (1 - x, 1 - y)]

        def slot(p):
            return 4 * p[0] + 2 * p[1] + p[2]

        def maker(t, k, to):
            def make(s, d):
                return pltpu.make_async_remote_copy(src_ref=s, dst_ref=d, send_sem=send.at[7 * t + k],
                                                    recv_sem=recv.at[7 * t + k], device_id=to, device_id_type=MESH)
            return make

        def landing(t, k, block):
            dst = outs[t].at[slot(block)]
            return maker(t, k, me)(dst, dst)

        sent = []
        for t in range(n):
            dst = outs[t].at[slot(me)]
            sent.append(_split_start(maker(t, 0, sib), ins[t], dst, pieces[t]))
            for j, chip in enumerate(chips):
                sent.append(_split_start(maker(t, 1 + j, (*chip, c)), ins[t], dst, pieces[t]))
        for j, chip in enumerate(chips):
            for t in range(n):
                landing(t, 1 + j, (*chip, c)).wait_recv()
                blk = outs[t].at[slot((*chip, c))]
                sent.append(_split_start(maker(t, 4 + j, sib), blk, blk, pieces[t]))
        for t in range(n):
            landing(t, 0, sib).wait_recv()
            for j, chip in enumerate(chips):
                landing(t, 4 + j, (*chip, 1 - c)).wait_recv()
        for cp in sent:
            cp.wait_send()

    outs = [_sds((N_DEV,) + a.shape, a.dtype) for a in arrs]
    got = _comm_call(body, name, arrs, outs, 7 * n, 1)
    x, y, c = _place()
    return [lax.dynamic_update_index_in_dim(g, a, 4 * x + 2 * y + c, 0) for g, a in zip(got, arrs)]


def sibling_send(arrs, name):
    n = len(arrs)
    pieces = [_pieces(a.shape[1:], a.dtype.itemsize) for a in arrs]

    def body(*refs):
        ins, theirs = refs[:n], refs[n:2 * n]
        send, recv, _ = refs[2 * n:]
        x, y, c = _place()
        rem = []
        for t in range(n):
            def make(s, d, t=t):
                return pltpu.make_async_remote_copy(src_ref=s, dst_ref=d, send_sem=send.at[t], recv_sem=recv.at[t],
                                                    device_id=(x, y, 1 - c), device_id_type=MESH)
            rem.append(_split_start(make, ins[t].at[1 - c], theirs[t], pieces[t]))
        for cp in rem:
            cp.wait_recv()
        for cp in rem:
            cp.wait_send()

    outs = [_sds(a.shape[1:], a.dtype) for a in arrs]
    return _comm_call(body, name, arrs, outs, n, 1)


def exchange_chips(arrs, name):
    n = len(arrs)
    pieces = [_pieces(a.shape[1:], a.dtype.itemsize) for a in arrs]

    def body(*refs):
        ins, outs = refs[:n], refs[n:2 * n]
        send, recv, _ = refs[2 * n:]
        x, y, c = _place()
        peers = [(1 - x, y), (x, 1 - y), (1 - x, 1 - y)]
        rem = []
        for t in range(n):
            for j, (px, py) in enumerate(peers):
                def make(s, d, t=t, j=j, px=px, py=py):
                    return pltpu.make_async_remote_copy(
                        src_ref=s, dst_ref=d, send_sem=send.at[3 * t + j], recv_sem=recv.at[3 * t + j],
                        device_id=(px, py, c), device_id_type=MESH)
                rem.append(_split_start(make, ins[t].at[2 * px + py], outs[t].at[j], pieces[t]))
        for cp in rem:
            cp.wait_recv()
        for cp in rem:
            cp.wait_send()

    outs = [_sds((N_CHIPS - 1,) + a.shape[1:], a.dtype) for a in arrs]
    return _comm_call(body, name, arrs, outs, 3 * n, 1)


def sibling_swap(arrs, name):
    n = len(arrs)
    pieces = [_pieces(a.shape, a.dtype.itemsize) for a in arrs]

    def body(*refs):
        ins, outs = refs[:n], refs[n:2 * n]
        send, recv, _ = refs[2 * n:]
        x, y, c = _place()
        rem = []
        for t in range(n):
            def make(s, d, t=t):
                return pltpu.make_async_remote_copy(src_ref=s, dst_ref=d, send_sem=send.at[t], recv_sem=recv.at[t],
                                                    device_id=(x, y, 1 - c), device_id_type=MESH)
            rem.append(_split_start(make, ins[t], outs[t], pieces[t]))
        for cp in rem:
            cp.wait_recv()
        for cp in rem:
            cp.wait_send()

    outs = [_sds(a.shape, a.dtype) for a in arrs]
    return _comm_call(body, name, arrs, outs, n, 1)


def _build_w_in(w4):
    full = jnp.concatenate([w4[j] for j in range(N_CHIPS)], axis=1)
    parts = []
    for _, start, width, wp in PROJ_SEGS:
        if width:
            parts.append(full[:, start:start + width])
        if wp > width:
            parts.append(jnp.zeros((full.shape[0], wp - width), full.dtype))
    return jnp.concatenate(parts, axis=1)


def _permute_w_uq(w):
    w3 = w.reshape(w.shape[0], MLA_HEADS, MLA_NOPE + MLA_ROPE)
    hr = MLA_ROPE // 2
    return jnp.concatenate([
        w3[:, :, :MLA_NOPE].reshape(w.shape[0], -1),
        w3[:, :, MLA_NOPE:MLA_NOPE + hr].reshape(w.shape[0], -1),
        w3[:, :, MLA_NOPE + hr:].reshape(w.shape[0], -1)], axis=1)


def _heads(t, d):
    return t.reshape(t.shape[0], -1, d).transpose(1, 0, 2)


def _unheads(t):
    return t.transpose(1, 0, 2).reshape(t.shape[1], -1)


def _layer(xh, mod, big, small, rope_q, rope_k):
    s = xh.shape[0]
    d = D_MODEL
    shift, scale, gate = mod[None, 0:d], mod[None, d:2 * d], mod[None, 2 * d:3 * d]
    w_al = _build_w_in(big["w_in"])
    proj = mod_mm(xh, small["norm_g"][None], scale, shift, w_al)
    gq, gk, gv, glr, mq, mkv, mkr, cb, cc, cx, _, z = split_proj(proj)

    rk = GLA_RANK
    hk = GLA_HEADS * GLA_DK
    wg = jnp.zeros((128, 2 * hk), F32)
    wg = wg.at[0:rk, 0:hk].set(small["gla_wg_f"]).at[rk:2 * rk, hk:].set(small["gla_wg_b"])
    bg = jnp.concatenate([small["gla_bg_f"], small["gla_bg_b"]])[None]
    la = gate_act(mm(glr, wg), bg)
    o_gla = gla(_heads(gq, GLA_DK), _heads(gk, GLA_DK), _heads(gv, GLA_DV),
                _heads(la[:, :hk], GLA_DK), _heads(la[:, hk:], GLA_DK))
    o_gla = rmsnorm(o_gla.reshape(GLA_HEADS * s, GLA_DV), small["gla_norm_g"][None])
    o_gla = _unheads(o_gla.reshape(GLA_HEADS, s, GLA_DV))

    nh = MLA_HEADS * MLA_NOPE
    hr = MLA_HEADS * MLA_ROPE // 2
    cq = rmsnorm(mq, small["mla_q_norm_g"][None])
    qm = mm(cq, _permute_w_uq(jnp.concatenate([big["w_uq"][j] for j in range(N_CHIPS)], axis=1)))
    qr = qm[:, nh:]
    qr = fma(qr, rope_q[0], jnp.concatenate([qr[:, hr:], qr[:, :hr]], axis=1), rope_q[1])
    ckv = rmsnorm(mkv, small["mla_kv_norm_g"][None])
    kv = mm(ckv, jnp.concatenate([big["w_ukv"][j] for j in range(N_CHIPS)], axis=1))
    kr = mkr[:, :MLA_ROPE]
    kr = fma(kr, rope_k[0], jnp.concatenate([kr[:, MLA_ROPE // 2:], kr[:, :MLA_ROPE // 2]], axis=1), rope_k[1])
    q3 = jnp.concatenate([qm[:, :nh].reshape(s, MLA_HEADS, MLA_NOPE), qr[:, :hr].reshape(s, MLA_HEADS, -1),
                          qr[:, hr:].reshape(s, MLA_HEADS, -1)], axis=-1).transpose(1, 0, 2)
    kv3 = kv.reshape(s, MLA_HEADS, MLA_NOPE + MLA_DV)
    k3 = jnp.concatenate([kv3[:, :, :MLA_NOPE], jnp.broadcast_to(kr[:, None, :], (s, MLA_HEADS, MLA_ROPE))],
                         axis=-1).transpose(1, 0, 2)
    v3 = kv3[:, :, MLA_NOPE:].transpose(1, 0, 2)
    o_mla = rmsnorm(_unheads(attn(q3, k3, v3)), small["mla_out_g"][None])

    cw = jnp.concatenate([small["conv_w"], jnp.zeros((5, CONV_CH), F32)], axis=0)
    o_conv = rmsnorm(conv_op(cb, cc, cx, cw), small["conv_out_g"][None])

    o = jnp.concatenate([o_gla, o_mla, o_conv], axis=1)
    w_out = big["w_out"].reshape(d, d)
    return out_block(o, z, w_out, xh, gate)


def _loss_fn(xh, mods, bigs, smalls, final_g, target, rope_q, rope_k):
    h = xh
    for l in range(DEPTH):
        h = _layer(h, mods[l], bigs[l], smalls[l], rope_q, rope_k)
    return loss_op(rmsnorm(h, final_g[None]), target)[0, 0]


SMALL_REPL = ("norm_g", "gla_bg_f", "gla_bg_b", "gla_norm_g", "mla_q_norm_g", "mla_kv_norm_g", "mla_out_g",
              "conv_out_g")
SMALL_SHARDED = ("gla_wg_f", "gla_wg_b", "conv_w")
BIG = ("w_in", "w_out", "w_uq", "w_ukv")


def kernel(x, c, positions, ada_w, ada_b, norm_g, w_in, gla_wg_f, gla_bg_f, gla_wg_b, gla_bg_b, gla_norm_g, mla_q_norm_g, mla_kv_norm_g, mla_w_uq, mla_w_ukv, mla_out_g, conv_w, conv_out_g, w_out, final_g, loss_target, m_ada_w, m_ada_b, m_norm_g, m_w_in, m_gla_wg_f, m_gla_bg_f, m_gla_wg_b, m_gla_bg_b, m_gla_norm_g, m_mla_q_norm_g, m_mla_kv_norm_g, m_mla_w_uq, m_mla_w_ukv, m_mla_out_g, m_conv_w, m_conv_out_g, m_w_out, m_final_g, v_ada_w, v_ada_b, v_norm_g, v_w_in, v_gla_wg_f, v_gla_bg_f, v_gla_wg_b, v_gla_bg_b, v_gla_norm_g, v_mla_q_norm_g, v_mla_kv_norm_g, v_mla_w_uq, v_mla_w_ukv, v_mla_out_g, v_conv_w, v_conv_out_g, v_w_out, v_final_g):
    xi, yi, ci = _place()
    chip = 2 * xi + yi
    dev = 2 * chip + ci
    s = x.shape[1]
    d = D_MODEL
    weights = dict(ada_w=ada_w, ada_b=ada_b, norm_g=norm_g, w_in=w_in, gla_wg_f=gla_wg_f, gla_bg_f=gla_bg_f,
                   gla_wg_b=gla_wg_b, gla_bg_b=gla_bg_b, gla_norm_g=gla_norm_g, mla_q_norm_g=mla_q_norm_g,
                   mla_kv_norm_g=mla_kv_norm_g, mla_w_uq=mla_w_uq, mla_w_ukv=mla_w_ukv, mla_out_g=mla_out_g,
                   conv_w=conv_w, conv_out_g=conv_out_g, w_out=w_out, final_g=final_g)
    m_in = dict(ada_w=m_ada_w, ada_b=m_ada_b, norm_g=m_norm_g, w_in=m_w_in, gla_wg_f=m_gla_wg_f, gla_bg_f=m_gla_bg_f,
                gla_wg_b=m_gla_wg_b, gla_bg_b=m_gla_bg_b, gla_norm_g=m_gla_norm_g, mla_q_norm_g=m_mla_q_norm_g,
                mla_kv_norm_g=m_mla_kv_norm_g, mla_w_uq=m_mla_w_uq, mla_w_ukv=m_mla_w_ukv, mla_out_g=m_mla_out_g,
                conv_w=m_conv_w, conv_out_g=m_conv_out_g, w_out=m_w_out, final_g=m_final_g)
    v_in = dict(ada_w=v_ada_w, ada_b=v_ada_b, norm_g=v_norm_g, w_in=v_w_in, gla_wg_f=v_gla_wg_f, gla_bg_f=v_gla_bg_f,
                gla_wg_b=v_gla_wg_b, gla_bg_b=v_gla_bg_b, gla_norm_g=v_gla_norm_g, mla_q_norm_g=v_mla_q_norm_g,
                mla_kv_norm_g=v_mla_kv_norm_g, mla_w_uq=v_mla_w_uq, mla_w_ukv=v_mla_w_ukv, mla_out_g=v_mla_out_g,
                conv_w=v_conv_w, conv_out_g=v_conv_out_g, w_out=v_w_out, final_g=v_final_g)

    def mine_bf16(w):
        return lax.dynamic_index_in_dim(w, ci, 0, keepdims=False).astype(BF16)

    g_c, g_in, g_out, g_uq, g_ukv, g_wgf, g_wgb, g_cw = all_gather8(
        [c, mine_bf16(w_in), mine_bf16(w_out), mine_bf16(mla_w_uq), mine_bf16(mla_w_ukv), gla_wg_f, gla_wg_b, conv_w],
        "gather_weights")

    def by_layer(g):
        g4 = g.reshape((N_CHIPS, 2) + g.shape[1:])
        return [g4[:, l] for l in range(DEPTH)]

    bigs = [dict(w_in=a, w_out=b, w_uq=u, w_ukv=k)
            for a, b, u, k in zip(by_layer(g_in), by_layer(g_out), by_layer(g_uq), by_layer(g_ukv))]

    def unshard_cols(g):
        g4 = g[0::2]
        return g4.transpose(1, 2, 0, 3).reshape(g4.shape[1], g4.shape[2], -1)

    small_full = dict(gla_wg_f=unshard_cols(g_wgf), gla_wg_b=unshard_cols(g_wgb), conv_w=unshard_cols(g_cw))
    for nme in SMALL_REPL:
        small_full[nme] = weights[nme]
    smalls = [{nme: small_full[nme][l] for nme in SMALL_REPL + SMALL_SHARDED} for l in range(DEPTH)]

    c_act = _silu_rows(g_c[:, 0, :])
    c_act16 = jnp.concatenate([c_act, jnp.zeros_like(c_act)], axis=0)
    n_ada = ada_w.shape[2]
    parts = []
    for l in range(DEPTH):
        bias = lax.dynamic_slice_in_dim(ada_b[l], chip * n_ada, n_ada)[None]
        parts.append(_mm(c_act16, ada_w[l], bias=bias, name="ada_fwd"))
    g_mod, = all_gather8([jnp.stack(parts)], "gather_mod")
    mod_mine = lax.dynamic_index_in_dim(g_mod[0::2], dev, 2, keepdims=False)
    mods = mod_mine.transpose(1, 0, 2).reshape(DEPTH, 3 * d)

    inv_freq = ROPE_THETA ** (-jnp.arange(0, MLA_ROPE, 2, dtype=F32) / MLA_ROPE)
    ang = positions[0].astype(F32)[:, None] * inv_freq
    cos, sin = jnp.cos(ang), jnp.sin(ang)
    cos_h, sin_h = jnp.tile(cos, (1, MLA_HEADS)), jnp.tile(sin, (1, MLA_HEADS))
    rope_q = (jnp.concatenate([cos_h, cos_h], axis=1), jnp.concatenate([-sin_h, sin_h], axis=1))
    rope_k = (jnp.concatenate([cos, cos], axis=1), jnp.concatenate([-sin, sin], axis=1))

    loss_dev, (dx, dmods, dbigs, dsmalls, dfinal) = jax.value_and_grad(_loss_fn, argnums=(0, 1, 2, 3, 4))(
        x[0], mods, bigs, smalls, final_g, loss_target[0], rope_q, rope_k)
    loss = lax.psum(loss_dev, ("x", "y", "c"))

    pieces = [dmods.reshape(-1), dfinal]
    for nme in SMALL_REPL + SMALL_SHARDED:
        pieces.append(jnp.stack([dsmalls[l][nme] for l in range(DEPTH)]).reshape(-1))
    sizes = [p.shape[0] for p in pieces]
    flat = jnp.concatenate(pieces)
    padn = (-flat.shape[0]) % 128
    flat = jnp.concatenate([flat, jnp.zeros((padn,), F32)])[None]
    g_small, = all_gather8([flat], "gather_small_grads")
    total = _sum_devices(g_small)[0]
    offs, at = [], 0
    for n_el in sizes:
        offs.append(at)
        at += n_el

    def piece(i, shape):
        return total[offs[i]:offs[i] + sizes[i]].reshape(shape)

    grads = {"ada_b": piece(0, (DEPTH, 3 * d)), "final_g": piece(1, (d,))}
    for i, nme in enumerate(SMALL_REPL + SMALL_SHARDED):
        full = piece(2 + i, small_full[nme].shape)
        if nme in SMALL_SHARDED:
            ncol = weights[nme].shape[2]
            full = lax.dynamic_slice_in_dim(full, chip * ncol, ncol, axis=2)
        grads[nme] = full

    dmod_all = g_small[:, 0, :DEPTH * 3 * d].reshape(N_DEV, DEPTH, 3 * d)
    dmod_cols = lax.dynamic_slice_in_dim(dmod_all, chip * n_ada, n_ada, axis=2)
    g_ada = []
    for l in range(DEPTH):
        dm16 = jnp.concatenate([dmod_cols[:, l], jnp.zeros((N_DEV, n_ada), F32)], axis=0)
        g_ada.append(_mm(c_act16, dm16, ta=True, name="ada_bwd"))
    grads["ada_w"] = jnp.stack(g_ada)

    stacked = [jnp.stack([dbigs[l][nme] for l in range(DEPTH)]) for nme in BIG]
    c_idx = jnp.reshape(ci, (1,)).astype(jnp.int32)
    chip_idx = jnp.reshape(chip, (1,)).astype(jnp.int32)
    theirs = sibling_send(stacked, "reduce_sibling")
    pair = [_add_pair(a, b, c_idx) for a, b in zip(stacked, theirs)]
    landed = exchange_chips(pair, "reduce_chips")
    reduced = [_add_chips(p, q, chip_idx) for p, q in zip(pair, landed)]
    others = sibling_swap(reduced, "share_sibling")
    for nme, own, other in zip(("w_in", "w_out", "mla_w_uq", "mla_w_ukv"), reduced, others):
        grads[nme] = jnp.stack([jnp.where(ci == l, own, other) for l in range(DEPTH)])

    order = list(weights)
    delta, new_m, new_v = {}, {}, {}
    for nme in order:
        delta[nme], new_m[nme], new_v[nme] = _adamw(weights[nme], grads[nme], m_in[nme], v_in[nme])
    return (loss, dx[None], *[grads[n_] for n_ in order], *[delta[n_] for n_ in order],
            *[new_m[n_] for n_ in order], *[new_v[n_] for n_ in order])
```

```python
import functools

import jax
import jax.numpy as jnp
from jax import lax
from jax.experimental import pallas as pl
from jax.experimental.pallas import tpu as pltpu

F32 = jnp.float32
BF16 = jnp.bfloat16
MESH = pl.DeviceIdType.MESH
HIGHEST = lax.Precision.HIGHEST

DEPTH = 2
D_MODEL = 2048
GLA_HEADS = 6
GLA_DK = 64
GLA_DV = 128
GLA_RANK = 16
GLA_TEMP = 16.0
GLA_CHUNK = 64
GLA_W = GLA_HEADS * GLA_DV
MLA_HEADS = 6
MLA_QL = 384
MLA_KVL = 256
MLA_NOPE = 128
MLA_ROPE = 64
MLA_DV = 128
MLA_W = MLA_HEADS * MLA_DV
CONV_CH = D_MODEL - GLA_W - MLA_W
ROPE_THETA = 10000.0
EPS = 1e-6
IN_DIM = 5856
N_CHIPS = 4
N_DEV = 8

ADAM_LR = 0.001
ADAM_B1 = 0.9
ADAM_B2 = 0.999
ADAM_EPS = 1e-08
ADAM_WD = 0.01
ADAM_STEP = 10

PROJ_SEGS = (
    ("gq", 0, 384, 384), ("gk", 384, 384, 384), ("gv", 768, 768, 768), ("glr", 1536, 32, 128),
    ("mq", 1568, 384, 384), ("mkv", 1952, 256, 256), ("mkr", 2208, 64, 128),
    ("cb", 2272, 512, 512), ("cc", 2784, 512, 512), ("cx", 3296, 512, 512),
    ("pad", 3808, 0, 128), ("z", 3808, 2048, 2048),
)
PROJ_AL = sum(s[3] for s in PROJ_SEGS)

VMEM_LIMIT = 48 * 1024 * 1024
BLOCK_BYTES = 2 * 1024 * 1024


def _params(sem=None):
    return pltpu.CompilerParams(dimension_semantics=sem, vmem_limit_bytes=VMEM_LIMIT)


def _dot(a, b, ca, cb, precision=None):
    return lax.dot_general(a, b, (((ca,), (cb,)), ((), ())), preferred_element_type=F32, precision=precision)


def _tile(dim, prefs):
    for t in prefs:
        if dim % t == 0:
            return t
    return dim


def _pick_rows(rows, width, itemsize=4):
    for t in (2048, 1024, 512, 256, 128, 64, 32, 16, 8):
        if rows % t == 0 and t * width * itemsize <= BLOCK_BYTES:
            return t
    return rows


def _mm(a, b, *, ta=False, tb=False, bias=None, out_dtype=F32, name="mm"):
    if ta:
        K, M = a.shape
    else:
        M, K = a.shape
    if tb:
        N, Kb = b.shape
    else:
        Kb, N = b.shape
    assert K == Kb, (a.shape, b.shape, ta, tb)
    tm = _tile(M, (512, 256, 128))
    tn = _tile(N, (1024, 512, 384, 256, 128))
    tk = _tile(K, (2048, 1024, 512, 256, 128))
    nk = K // tk
    has_bias = bias is not None

    def body(*refs):
        a_ref, b_ref = refs[0], refs[1]
        bias_ref = refs[2] if has_bias else None
        o_ref = refs[3 if has_bias else 2]
        part = _dot(a_ref[...].astype(BF16), b_ref[...].astype(BF16), 0 if ta else 1, 1 if tb else 0)

        def finish(r):
            if has_bias:
                r = r + bias_ref[...]
            o_ref[...] = r.astype(out_dtype)

        if nk == 1:
            finish(part)
            return
        acc_ref = refs[-1]
        k = pl.program_id(2)

        @pl.when(k == 0)
        def _():
            acc_ref[...] = part

        @pl.when(k != 0)
        def _():
            acc_ref[...] += part

        @pl.when(k == nk - 1)
        def _():
            finish(acc_ref[...])

    a_spec = pl.BlockSpec((tk, tm), lambda i, j, k: (k, i)) if ta else pl.BlockSpec((tm, tk), lambda i, j, k: (i, k))
    b_spec = pl.BlockSpec((tn, tk), lambda i, j, k: (j, k)) if tb else pl.BlockSpec((tk, tn), lambda i, j, k: (k, j))
    in_specs = [a_spec, b_spec]
    args = [a, b]
    if has_bias:
        in_specs.append(pl.BlockSpec((1, tn), lambda i, j, k: (0, j)))
        args.append(bias)
    return pl.pallas_call(
        body, name=name, grid=(M // tm, N // tn, nk),
        in_specs=in_specs, out_specs=pl.BlockSpec((tm, tn), lambda i, j, k: (i, j)),
        out_shape=jax.ShapeDtypeStruct((M, N), out_dtype),
        scratch_shapes=[pltpu.VMEM((tm, tn), F32)] if nk > 1 else [],
        compiler_params=_params(("parallel", "parallel", "arbitrary")),
    )(*args)


@jax.custom_vjp
def mm(a, b):
    return _mm(a, b, name="mm_fwd")


def _mm_f(a, b):
    return _mm(a, b, name="mm_fwd"), (a, b)


def _mm_b(res, g):
    a, b = res
    return _mm(g, b, tb=True, out_dtype=a.dtype, name="mm_da"), _mm(a, g, ta=True, out_dtype=b.dtype, name="mm_db")


mm.defvjp(_mm_f, _mm_b)


def _rows(body, name, tiled, full, tiled_out, acc_out, tr=None):
    rows = tiled[0].shape[0]
    if tr is None:
        width = max([a.shape[1] for a in tiled] + [s.shape[1] for s in tiled_out])
        tr = _pick_rows(rows, width)
    in_specs = [pl.BlockSpec((tr, a.shape[1]), lambda i: (i, 0)) for a in tiled]
    in_specs += [pl.BlockSpec(a.shape, lambda i: (0, 0)) for a in full]
    out_specs = [pl.BlockSpec((tr, s.shape[1]), lambda i: (i, 0)) for s in tiled_out]
    out_specs += [pl.BlockSpec(s.shape, lambda i: (0, 0)) for s in acc_out]

    def wrapped(*refs):
        body(pl.program_id(0), *refs)

    outs = pl.pallas_call(
        wrapped, name=name, grid=(rows // tr,), in_specs=in_specs, out_specs=out_specs,
        out_shape=list(tiled_out) + list(acc_out),
        compiler_params=_params(("arbitrary",)),
    )(*tiled, *full)
    return outs


def _sds(shape, dtype=F32):
    return jax.ShapeDtypeStruct(tuple(shape), dtype)


def _acc(step, ref, val):
    @pl.when(step == 0)
    def _():
        ref[...] = val

    @pl.when(step != 0)
    def _():
        ref[...] += val


def _colsum(v):
    return jnp.sum(v, axis=0, keepdims=True)


def _rstd(x):
    return lax.rsqrt(jnp.mean(x * x, axis=-1, keepdims=True) + EPS)


@jax.custom_vjp
def rmsnorm(x, g):
    def body(i, x_ref, g_ref, o_ref):
        x = x_ref[...]
        o_ref[...] = x * _rstd(x) * g_ref[...]
    return _rows(body, "rmsnorm_fwd", [x], [g], [_sds(x.shape)], [])[0]


def _rmsnorm_f(x, g):
    return rmsnorm(x, g), (x, g)


def _rmsnorm_b(res, dy):
    x, g = res

    def body(i, x_ref, dy_ref, g_ref, dx_ref, dg_ref):
        x = x_ref[...]
        dy = dy_ref[...]
        r = _rstd(x)
        xh = x * r
        dxh = dy * g_ref[...]
        dx_ref[...] = r * (dxh - xh * jnp.mean(dxh * xh, axis=-1, keepdims=True))
        _acc(i, dg_ref, _colsum(dy * xh))

    dx, dg = _rows(body, "rmsnorm_bwd", [x, dy], [g], [_sds(x.shape)], [_sds(g.shape)])
    return dx, dg


rmsnorm.defvjp(_rmsnorm_f, _rmsnorm_b)


def _modulate(x, g, scale, shift):
    def body(i, x_ref, g_ref, sc_ref, sh_ref, o_ref):
        x = x_ref[...]
        xn = x * _rstd(x) * g_ref[...]
        o_ref[...] = (xn * (1.0 + sc_ref[...]) + sh_ref[...]).astype(BF16)
    return _rows(body, "modulate_fwd", [x], [g, scale, shift], [_sds(x.shape, BF16)], [])[0]


def _modulate_bwd(x, g, scale, shift, dh):
    def body(i, x_ref, dh_ref, g_ref, sc_ref, dx_ref, dg_ref, dsc_ref, dsh_ref):
        x = x_ref[...]
        dh = dh_ref[...]
        gv = g_ref[...]
        r = _rstd(x)
        xh = x * r
        dxn = dh * (1.0 + sc_ref[...])
        dxh = dxn * gv
        dx_ref[...] = r * (dxh - xh * jnp.mean(dxh * xh, axis=-1, keepdims=True))
        _acc(i, dg_ref, _colsum(dxn * xh))
        _acc(i, dsc_ref, _colsum(dh * (xh * gv)))
        _acc(i, dsh_ref, _colsum(dh))

    v = _sds(g.shape)
    return _rows(body, "modulate_bwd", [x, dh], [g, scale], [_sds(x.shape)], [v, v, v])


@jax.custom_vjp
def mod_mm(x, g, scale, shift, wt):
    return _mm(_modulate(x, g, scale, shift), wt, tb=True, name="mm_in")


def _mod_mm_f(x, g, scale, shift, wt):
    h = _modulate(x, g, scale, shift)
    return _mm(h, wt, tb=True, name="mm_in"), (x, g, scale, shift, wt, h)


def _mod_mm_b(res, dproj):
    x, g, scale, shift, wt, h = res
    dproj = dproj.astype(BF16)
    dh = _mm(dproj, wt, name="mm_in_dh")
    dwt = _mm(dproj, h, ta=True, out_dtype=wt.dtype, name="mm_in_dw")
    dx, dg, dsc, dsh = _modulate_bwd(x, g, scale, shift, dh)
    return dx, dg, dsc, dsh, dwt


mod_mm.defvjp(_mod_mm_f, _mod_mm_b)


def _sigmoid(z):
    return 1.0 / (1.0 + jnp.exp(-z))


def _gate_mul(o, z):
    def body(i, o_ref, z_ref, y_ref):
        z = z_ref[...]
        y_ref[...] = (o_ref[...] * (z * _sigmoid(z))).astype(BF16)
    return _rows(body, "gate_mul_fwd", [o, z], [], [_sds(o.shape, BF16)], [])[0]


def _gate_mul_bwd(o, z, dy):
    def body(i, o_ref, z_ref, dy_ref, do_ref, dz_ref):
        z = z_ref[...]
        dy = dy_ref[...]
        s = _sigmoid(z)
        do_ref[...] = dy * (z * s)
        dz_ref[...] = dy * o_ref[...] * (s * (1.0 + z * (1.0 - s)))
    return _rows(body, "gate_mul_bwd", [o, z, dy], [], [_sds(o.shape), _sds(o.shape)], [])


def _residual(x, u, gate):
    def body(i, x_ref, u_ref, g_ref, o_ref):
        o_ref[...] = x_ref[...] + g_ref[...] * u_ref[...]
    return _rows(body, "residual_fwd", [x, u], [gate], [_sds(x.shape)], [])[0]


def _residual_bwd(d, u, gate):
    def body(i, d_ref, u_ref, g_ref, du_ref, dg_ref):
        d = d_ref[...]
        du_ref[...] = (g_ref[...] * d).astype(BF16)
        _acc(i, dg_ref, _colsum(d * u_ref[...]))

    return _rows(body, "residual_bwd", [d, u], [gate], [_sds(u.shape, BF16)], [_sds(gate.shape)])


@jax.custom_vjp
def out_block(o, z, w, x, gate):
    return _residual(x, _mm(_gate_mul(o, z), w, name="mm_out"), gate)


def _out_block_f(o, z, w, x, gate):
    y = _gate_mul(o, z)
    u = _mm(y, w, name="mm_out")
    return _residual(x, u, gate), (o, z, w, y, u, gate)


def _out_block_b(res, d):
    o, z, w, y, u, gate = res
    du, dgate = _residual_bwd(d, u, gate)
    dy = _mm(du, w, tb=True, name="mm_out_dy")
    dw = _mm(y, du, ta=True, out_dtype=w.dtype, name="mm_out_dw")
    do, dz = _gate_mul_bwd(o, z, dy)
    return do, dz, dw, d, dgate


out_block.defvjp(_out_block_f, _out_block_b)


@jax.custom_vjp
def gate_act(u, b):
    def body(i, u_ref, b_ref, o_ref):
        t = u_ref[...] + b_ref[...]
        o_ref[...] = (jnp.minimum(t, 0.0) - jnp.log(1.0 + jnp.exp(-jnp.abs(t)))) / GLA_TEMP
    return _rows(body, "gate_act_fwd", [u], [b], [_sds(u.shape)], [])[0]


def _gate_act_f(u, b):
    return gate_act(u, b), (u, b)


def _gate_act_b(res, d):
    u, b = res

    def body(i, u_ref, d_ref, b_ref, du_ref, db_ref):
        t = u_ref[...] + b_ref[...]
        du = d_ref[...] * _sigmoid(-t) / GLA_TEMP
        du_ref[...] = du
        _acc(i, db_ref, _colsum(du))

    du, db = _rows(body, "gate_act_bwd", [u, d], [b], [_sds(u.shape)], [_sds(b.shape)])
    return du, db


gate_act.defvjp(_gate_act_f, _gate_act_b)


@jax.custom_vjp
def fma(a, b, c, d):
    def body(i, a_ref, b_ref, c_ref, d_ref, o_ref):
        o_ref[...] = a_ref[...] * b_ref[...] + c_ref[...] * d_ref[...]
    return _rows(body, "fma_fwd", [a, b, c, d], [], [_sds(a.shape)], [])[0]


def _fma_f(a, b, c, d):
    return fma(a, b, c, d), (b, d)


def _fma_b(res, g):
    b, d = res

    def body(i, g_ref, b_ref, d_ref, da_ref, dc_ref):
        g = g_ref[...]
        da_ref[...] = g * b_ref[...]
        dc_ref[...] = g * d_ref[...]

    da, dc = _rows(body, "fma_bwd", [g, b, d], [], [_sds(g.shape), _sds(g.shape)], [])
    return da, jnp.zeros_like(b), dc, jnp.zeros_like(d)


fma.defvjp(_fma_f, _fma_b)


def _silu_rows(c):
    def body(i, c_ref, o_ref):
        v = c_ref[...]
        o_ref[...] = v * _sigmoid(v)
    return _rows(body, "silu", [c], [], [_sds(c.shape)], [])[0]


@jax.custom_vjp
def loss_op(y, t):
    return _loss_fwd(y, t)[0]


def _loss_fwd(y, t):
    inv = 1.0 / y.shape[1]

    def body(i, y_ref, t_ref, d_ref, l_ref):
        e = y_ref[...] - t_ref[...]
        d_ref[...] = e * inv
        _acc(i, l_ref, jnp.sum(_colsum(e * e), axis=1, keepdims=True) * (0.5 * inv))

    d, l = _rows(body, "loss_fwd", [y, t], [], [_sds(y.shape)], [_sds((1, 1))])
    return l, d


def _loss_f(y, t):
    l, d = _loss_fwd(y, t)
    return l, d


def _loss_b(d, g):
    return d * g, jnp.zeros_like(d)


loss_op.defvjp(_loss_f, _loss_b)


def _conv_terms(cc, cx, rows, n):
    u = cc * cx
    up = jnp.where(rows == 0, 0.0, pltpu.roll(u, 1, 0))
    un = jnp.where(rows == n - 1, 0.0, pltpu.roll(u, n - 1, 0))
    return u, up, un


CONV_COLS = 128


def _conv_specs(s, n_in):
    blk = pl.BlockSpec((s, CONV_COLS), lambda j: (0, j))
    wblk = pl.BlockSpec((8, CONV_COLS), lambda j: (0, j))
    return [blk] * n_in + [wblk], blk, wblk


@jax.custom_vjp
def conv_op(cb, cc, cx, w):
    s, ch = cb.shape

    def body(cb_ref, cc_ref, cx_ref, w_ref, o_ref):
        rows = lax.broadcasted_iota(jnp.int32, (s, CONV_COLS), 0)
        u, up, un = _conv_terms(cc_ref[...], cx_ref[...], rows, s)
        conv = up * w_ref[0:1, :] + u * w_ref[1:2, :] + un * w_ref[2:3, :]
        o_ref[...] = cb_ref[...] * conv

    in_specs, blk, _ = _conv_specs(s, 3)
    return pl.pallas_call(
        body, name="conv_fwd", grid=(ch // CONV_COLS,), in_specs=in_specs, out_specs=blk,
        out_shape=_sds(cb.shape), compiler_params=_params(("parallel",)),
    )(cb, cc, cx, w)


def _conv_f(cb, cc, cx, w):
    return conv_op(cb, cc, cx, w), (cb, cc, cx, w)


def _conv_b(res, d):
    cb, cc, cx, w = res
    s, ch = cb.shape

    def body(cb_ref, cc_ref, cx_ref, d_ref, w_ref, dcb_ref, dcc_ref, dcx_ref, dw_ref):
        rows = lax.broadcasted_iota(jnp.int32, (s, CONV_COLS), 0)
        cc_v = cc_ref[...]
        cx_v = cx_ref[...]
        u, up, un = _conv_terms(cc_v, cx_v, rows, s)
        w0, w1, w2 = w_ref[0:1, :], w_ref[1:2, :], w_ref[2:3, :]
        dv = d_ref[...]
        dcb_ref[...] = dv * (up * w0 + u * w1 + un * w2)
        dconv = dv * cb_ref[...]
        d_next = jnp.where(rows == s - 1, 0.0, pltpu.roll(dconv, s - 1, 0))
        d_prev = jnp.where(rows == 0, 0.0, pltpu.roll(dconv, 1, 0))
        du = w0 * d_next + w1 * dconv + w2 * d_prev
        dcc_ref[...] = du * cx_v
        dcx_ref[...] = du * cc_v
        dw_ref[...] = jnp.zeros_like(dw_ref)
        dw_ref[0:1, :] = _colsum(dconv * up)
        dw_ref[1:2, :] = _colsum(dconv * u)
        dw_ref[2:3, :] = _colsum(dconv * un)

    in_specs, blk, wblk = _conv_specs(s, 4)
    v = _sds(cb.shape)
    return tuple(pl.pallas_call(
        body, name="conv_bwd", grid=(ch // CONV_COLS,), in_specs=in_specs, out_specs=[blk, blk, blk, wblk],
        out_shape=[v, v, v, _sds(w.shape)], compiler_params=_params(("parallel",)),
    )(cb, cc, cx, d, w))


conv_op.defvjp(_conv_f, _conv_b)


def _gla_masks(rev):
    c = GLA_CHUNK
    row = lax.broadcasted_iota(jnp.int32, (c, c), 0)
    col = lax.broadcasted_iota(jnp.int32, (c, c), 1)
    tri = jnp.where((row <= col) if rev else (row >= col), 1.0, 0.0).astype(F32)
    mask = (row < col) if rev else (row >= col)
    return tri, mask


GLA_UNROLL = 4


def _gla_rows(n):
    return pl.ds(pl.multiple_of(n * GLA_CHUNK, GLA_CHUNK), GLA_CHUNK)


def _gla_scan(s_ref, bt_ref, st_ref, n_chunks, descending):
    st_ref[...] = jnp.zeros_like(st_ref)

    def step(i, carry):
        n = (n_chunks - 1 - i) if descending else i
        own = s_ref[0, n]
        st = st_ref[...]
        s_ref[0, n] = st
        st_ref[...] = st * jnp.exp(bt_ref[n]) + own
        return carry

    lax.fori_loop(0, n_chunks, step, 0)


def _gla_fwd(q, k, v, gf, gb):
    h, s, dk = q.shape
    dv = v.shape[2]
    n_chunks = s // GLA_CHUNK
    scale = GLA_DK ** -0.5

    def body(q_ref, k_ref, v_ref, gf_ref, gb_ref, o_ref, sf_ref, sb_ref, bf_ref, bb_ref, btf_ref, btb_ref, st_ref):
        dirs = ((False, gf_ref, sf_ref, bf_ref, btf_ref), (True, gb_ref, sb_ref, bb_ref, btb_ref))
        masks = [_gla_masks(rev) for rev in (False, True)]

        def prepare(n, carry):
            rows = _gla_rows(n)
            kk = k_ref[0, rows, :]
            vb = v_ref[0, rows, :].astype(BF16)
            for (rev, g_ref, s_ref, b_ref, bt_ref), (tri, _) in zip(dirs, masks):
                g = g_ref[0, rows, :]
                b = _dot(tri, g, 1, 0, HIGHEST)
                bt = _colsum(g)
                b_ref[rows, :] = b
                bt_ref[n] = bt
                s_ref[0, n] = _dot(vb, (kk * jnp.exp(bt - b)).astype(BF16), 0, 0)
            return carry

        lax.fori_loop(0, n_chunks, prepare, 0, unroll=GLA_UNROLL)
        for rev, _, s_ref, _, bt_ref in dirs:
            _gla_scan(s_ref, bt_ref, st_ref, n_chunks, descending=rev)

        def emit(n, carry):
            rows = _gla_rows(n)
            qs = q_ref[0, rows, :] * scale
            kk = k_ref[0, rows, :]
            vb = v_ref[0, rows, :].astype(BF16)
            o = None
            for (rev, _, s_ref, b_ref, _), (_, mask) in zip(dirs, masks):
                b = b_ref[rows, :]
                qd = (qs * jnp.exp(b)).astype(BF16)
                ki = (kk * jnp.exp(-b)).astype(BF16)
                a = jnp.where(mask, _dot(qd, ki, 1, 1), 0.0).astype(BF16)
                od = _dot(a, vb, 1, 0) + _dot(qd, s_ref[0, n].astype(BF16), 1, 1)
                o = od if o is None else o + od
            o_ref[0, rows, :] = o
            return carry

        lax.fori_loop(0, n_chunks, emit, 0, unroll=GLA_UNROLL)

    blk_k = pl.BlockSpec((1, s, dk), lambda i: (i, 0, 0))
    blk_v = pl.BlockSpec((1, s, dv), lambda i: (i, 0, 0))
    blk_s = pl.BlockSpec((1, n_chunks, dv, dk), lambda i: (i, 0, 0, 0))
    st = _sds((h, n_chunks, dv, dk))
    scratch = [pltpu.VMEM((s, dk), F32), pltpu.VMEM((s, dk), F32), pltpu.VMEM((n_chunks, 1, dk), F32),
               pltpu.VMEM((n_chunks, 1, dk), F32), pltpu.VMEM((dv, dk), F32)]
    return pl.pallas_call(
        body, name="gla_fwd", grid=(h,), in_specs=[blk_k, blk_k, blk_v, blk_k, blk_k],
        out_specs=[blk_v, blk_s, blk_s], out_shape=[_sds(v.shape), st, st], scratch_shapes=scratch,
        compiler_params=_params(("parallel",)),
    )(q, k, v, gf, gb)


def _gla_bwd(q, k, v, gf, gb, sf, sb, do):
    h, s, dk = q.shape
    dv = v.shape[2]
    c = GLA_CHUNK
    n_chunks = s // c
    scale = GLA_DK ** -0.5

    def body(q_ref, k_ref, v_ref, gf_ref, gb_ref, sf_ref, sb_ref, do_ref, dq_ref, dk_ref, dv_ref, dgf_ref, dgb_ref,
             bf_ref, bb_ref, btf_ref, btb_ref, dsf_ref, dsb_ref, st_ref):
        dirs = ((False, gf_ref, sf_ref, bf_ref, btf_ref, dsf_ref, dgf_ref),
                (True, gb_ref, sb_ref, bb_ref, btb_ref, dsb_ref, dgb_ref))
        masks = [_gla_masks(rev) for rev in (False, True)]
        rowc = lax.broadcasted_iota(jnp.int32, (c, dk), 0)

        def prepare(n, carry):
            rows = _gla_rows(n)
            qs = q_ref[0, rows, :] * scale
            do_b = do_ref[0, rows, :].astype(BF16)
            for (rev, g_ref, _, b_ref, bt_ref, ds_ref, _), (tri, _) in zip(dirs, masks):
                g = g_ref[0, rows, :]
                b = _dot(tri, g, 1, 0, HIGHEST)
                b_ref[rows, :] = b
                bt_ref[n] = _colsum(g)
                ds_ref[0, n] = _dot(do_b, (qs * jnp.exp(b)).astype(BF16), 0, 0)
            return carry

        lax.fori_loop(0, n_chunks, prepare, 0, unroll=GLA_UNROLL)
        for rev, _, _, _, bt_ref, ds_ref, _ in dirs:
            _gla_scan(ds_ref, bt_ref, st_ref, n_chunks, descending=not rev)

        def emit(n, carry):
            rows = _gla_rows(n)
            qs = q_ref[0, rows, :] * scale
            kk = k_ref[0, rows, :]
            vb = v_ref[0, rows, :].astype(BF16)
            do_b = do_ref[0, rows, :].astype(BF16)
            dq = dkk = dvv = None
            for (rev, _, s_ref, b_ref, bt_ref, ds_ref, dg_ref), (tri, mask) in zip(dirs, masks):
                b = b_ref[rows, :]
                bt = bt_ref[n]
                eb = jnp.exp(b)
                enb = jnp.exp(-b)
                etb = jnp.exp(bt - b)
                ebt = jnp.exp(bt)
                qd = qs * eb
                ki = kk * enb
                ke = kk * etb
                qd_b, ki_b, ke_b = qd.astype(BF16), ki.astype(BF16), ke.astype(BF16)
                st = s_ref[0, n]
                dst = ds_ref[0, n]
                dst_b = dst.astype(BF16)
                a = jnp.where(mask, _dot(qd_b, ki_b, 1, 1), 0.0).astype(BF16)
                da = jnp.where(mask, _dot(do_b, vb, 1, 1), 0.0).astype(BF16)
                dv_d = _dot(a, do_b, 0, 0) + _dot(ke_b, dst_b, 1, 1)
                dqd = _dot(da, ki_b, 1, 0) + _dot(do_b, st.astype(BF16), 1, 0)
                dki = _dot(da, qd_b, 0, 0)
                dke = _dot(vb, dst_b, 1, 0)
                dbt = _colsum(st * dst) * ebt + _colsum(dke * ke)
                db = dqd * qd - dki * ki - dke * ke
                db = db + jnp.where(rowc == (0 if rev else c - 1), dbt, 0.0)
                dg_ref[0, rows, :] = _dot(tri, db, 0, 0, HIGHEST)
                dq_d = dqd * eb * scale
                dk_d = dki * enb + dke * etb
                dq = dq_d if dq is None else dq + dq_d
                dkk = dk_d if dkk is None else dkk + dk_d
                dvv = dv_d if dvv is None else dvv + dv_d
            dq_ref[0, rows, :] = dq
            dk_ref[0, rows, :] = dkk
            dv_ref[0, rows, :] = dvv
            return carry

        lax.fori_loop(0, n_chunks, emit, 0, unroll=2)

    blk_k = pl.BlockSpec((1, s, dk), lambda i: (i, 0, 0))
    blk_v = pl.BlockSpec((1, s, dv), lambda i: (i, 0, 0))
    blk_s = pl.BlockSpec((1, n_chunks, dv, dk), lambda i: (i, 0, 0, 0))
    vk, vv = _sds(q.shape), _sds(v.shape)
    scratch = [pltpu.VMEM((s, dk), F32), pltpu.VMEM((s, dk), F32), pltpu.VMEM((n_chunks, 1, dk), F32),
               pltpu.VMEM((n_chunks, 1, dk), F32), pltpu.VMEM((1, n_chunks, dv, dk), F32),
               pltpu.VMEM((1, n_chunks, dv, dk), F32), pltpu.VMEM((dv, dk), F32)]
    return pl.pallas_call(
        body, name="gla_bwd", grid=(h,), in_specs=[blk_k, blk_k, blk_v, blk_k, blk_k, blk_s, blk_s, blk_v],
        out_specs=[blk_k, blk_k, blk_v, blk_k, blk_k], out_shape=[vk, vk, vv, vk, vk],
        scratch_shapes=scratch, compiler_params=_params(("parallel",)),
    )(q, k, v, gf, gb, sf, sb, do)


@jax.custom_vjp
def gla(q, k, v, gf, gb):
    return _gla_fwd(q, k, v, gf, gb)[0]


def _gla_f(q, k, v, gf, gb):
    o, sf, sb = _gla_fwd(q, k, v, gf, gb)
    return o, (q, k, v, gf, gb, sf, sb)


def _gla_b(res, do):
    return tuple(_gla_bwd(*res, do))


gla.defvjp(_gla_f, _gla_b)


ATTN_TQ = 256


def _attn_fwd(q, k, v):
    h, s, dq = q.shape
    dv = v.shape[2]
    tq = min(ATTN_TQ, s)
    scale = (MLA_NOPE + MLA_ROPE) ** -0.5

    def body(q_ref, k_ref, v_ref, o_ref, lse_ref):
        sc = _dot(q_ref[0].astype(BF16), k_ref[0].astype(BF16), 1, 1) * scale
        m = jnp.max(sc, axis=-1, keepdims=True)
        p = jnp.exp(sc - m)
        l = jnp.sum(p, axis=-1, keepdims=True)
        p = p * (1.0 / l)
        o_ref[0] = _dot(p.astype(BF16), v_ref[0].astype(BF16), 1, 0)
        lse_ref[0] = m + jnp.log(l)

    return pl.pallas_call(
        body, name="attn_fwd", grid=(h, s // tq),
        in_specs=[pl.BlockSpec((1, tq, dq), lambda i, j: (i, j, 0)), pl.BlockSpec((1, s, dq), lambda i, j: (i, 0, 0)),
                  pl.BlockSpec((1, s, dv), lambda i, j: (i, 0, 0))],
        out_specs=[pl.BlockSpec((1, tq, dv), lambda i, j: (i, j, 0)), pl.BlockSpec((1, tq, 1), lambda i, j: (i, j, 0))],
        out_shape=[_sds((h, s, dv)), _sds((h, s, 1))],
        compiler_params=_params(("parallel", "parallel")),
    )(q, k, v)


def _attn_bwd(q, k, v, o, lse, do):
    h, s, dq = q.shape
    dv = v.shape[2]
    tq = min(ATTN_TQ, s)
    scale = (MLA_NOPE + MLA_ROPE) ** -0.5

    def body(q_ref, k_ref, v_ref, o_ref, lse_ref, do_ref, dq_ref, dk_ref, dv_ref):
        j = pl.program_id(1)
        qb = q_ref[0].astype(BF16)
        kb = k_ref[0].astype(BF16)
        do = do_ref[0]
        do_b = do.astype(BF16)
        p = jnp.exp(_dot(qb, kb, 1, 1) * scale - lse_ref[0])
        dp = _dot(do_b, v_ref[0].astype(BF16), 1, 1)
        delta = jnp.sum(do * o_ref[0], axis=-1, keepdims=True)
        ds = (p * (dp - delta) * scale).astype(BF16)
        dq_ref[0] = _dot(ds, kb, 1, 0)
        dk_c = _dot(ds, qb, 0, 0)
        dv_c = _dot(p.astype(BF16), do_b, 0, 0)

        @pl.when(j == 0)
        def _():
            dk_ref[0] = dk_c
            dv_ref[0] = dv_c

        @pl.when(j != 0)
        def _():
            dk_ref[0] += dk_c
            dv_ref[0] += dv_c

    qblk = pl.BlockSpec((1, tq, dq), lambda i, j: (i, j, 0))
    kblk = pl.BlockSpec((1, s, dq), lambda i, j: (i, 0, 0))
    vblk = pl.BlockSpec((1, s, dv), lambda i, j: (i, 0, 0))
    oblk = pl.BlockSpec((1, tq, dv), lambda i, j: (i, j, 0))
    lblk = pl.BlockSpec((1, tq, 1), lambda i, j: (i, j, 0))
    return pl.pallas_call(
        body, name="attn_bwd", grid=(h, s // tq),
        in_specs=[qblk, kblk, vblk, oblk, lblk, oblk], out_specs=[qblk, kblk, vblk],
        out_shape=[_sds(q.shape), _sds(k.shape), _sds(v.shape)],
        compiler_params=_params(("parallel", "arbitrary")),
    )(q, k, v, o, lse, do)


@jax.custom_vjp
def attn(q, k, v):
    return _attn_fwd(q, k, v)[0]


def _attn_f(q, k, v):
    o, lse = _attn_fwd(q, k, v)
    return o, (q, k, v, o, lse)


def _attn_b(res, do):
    return tuple(_attn_bwd(*res, do))


attn.defvjp(_attn_f, _attn_b)


@jax.custom_vjp
def split_proj(proj):
    out, at = [], 0
    for _, _, _, wp in PROJ_SEGS:
        out.append(proj[:, at:at + wp])
        at += wp
    return tuple(out)


def _split_f(proj):
    return split_proj(proj), None


def _split_b(_, gs):
    return (jnp.concatenate(gs, axis=1),)


split_proj.defvjp(_split_f, _split_b)


def _tile2d(rows, width, limit=BLOCK_BYTES):
    fits = [t for t in range(16, rows + 1, 16) if rows % t == 0 and t * width * 4 <= limit]
    if fits and (fits[-1] >= 64 or fits[-1] == rows):
        return fits[-1], width
    if rows * width * 4 <= limit:
        return rows, width
    cols = [t for t in range(128, width + 1, 128) if width % t == 0 and rows * t * 4 <= limit]
    return (rows, cols[-1]) if cols else (rows, width)


def _add_pair(stacked, theirs, c_idx):
    g, r, w = theirs.shape
    tr, tc = _tile2d(r, w)

    def body(c_ref, a_ref, b_ref, o_ref):
        o_ref[0] = (a_ref[0, 0].astype(F32) + b_ref[0].astype(F32)).astype(BF16)

    blk = pl.BlockSpec((1, tr, tc), lambda k, i, j, c: (k, i, j))
    spec = pltpu.PrefetchScalarGridSpec(
        num_scalar_prefetch=1, grid=(g, r // tr, w // tc),
        in_specs=[pl.BlockSpec((1, 1, tr, tc), lambda k, i, j, c: (c[0], k, i, j)), blk], out_specs=blk)
    return pl.pallas_call(body, name="add_pair", grid_spec=spec, out_shape=_sds(theirs.shape, BF16),
                          compiler_params=_params(("parallel", "parallel", "parallel")))(c_idx, stacked, theirs)


def _add_chips(pair, landed, chip_idx):
    _, r, w = pair.shape
    tr, tc = _tile2d(r, w)

    def body(c_ref, p_ref, l0_ref, l1_ref, l2_ref, o_ref):
        o_ref[...] = ((p_ref[0].astype(F32) + l0_ref[0].astype(F32)) + l1_ref[0].astype(F32)) + l2_ref[0].astype(F32)

    specs = [pl.BlockSpec((1, tr, tc), lambda i, j, c: (c[0], i, j))]
    specs += [pl.BlockSpec((1, tr, tc), functools.partial(lambda i, j, c, k: (k, i, j), k=k)) for k in range(N_CHIPS - 1)]
    spec = pltpu.PrefetchScalarGridSpec(num_scalar_prefetch=1, grid=(r // tr, w // tc), in_specs=specs,
                                        out_specs=pl.BlockSpec((tr, tc), lambda i, j, c: (i, j)))
    return pl.pallas_call(body, name="add_chips", grid_spec=spec, out_shape=_sds((r, w)),
                          compiler_params=_params(("parallel", "parallel")))(chip_idx, pair, landed, landed, landed)


def _sum_devices(g):
    n = g.shape[2]

    def body(g_ref, o_ref):
        t = g_ref[0]
        for j in range(1, N_DEV):
            t = t + g_ref[j]
        o_ref[...] = t

    return pl.pallas_call(body, name="sum_devices", out_shape=_sds((1, n)), compiler_params=_params())(g)


def _adamw(w, g, m, v):
    shp = w.shape
    shp3 = (1, 1, shp[0]) if len(shp) == 1 else (-1,) + tuple(shp[-2:])
    w3, g3, m3, v3 = (t.reshape(shp3) for t in (w, g, m, v))
    c1 = 1.0 - ADAM_B1 ** ADAM_STEP
    c2 = 1.0 - ADAM_B2 ** ADAM_STEP

    def body(w_ref, g_ref, m_ref, v_ref, d_ref, mo_ref, vo_ref):
        gv = g_ref[...]
        mn = ADAM_B1 * m_ref[...] + (1.0 - ADAM_B1) * gv
        vn = ADAM_B2 * v_ref[...] + (1.0 - ADAM_B2) * (gv * gv)
        d_ref[...] = -ADAM_LR * ((mn / c1) / (jnp.sqrt(vn / c2) + ADAM_EPS) + ADAM_WD * w_ref[...])
        mo_ref[...] = mn
        vo_ref[...] = vn

    nl, r, wd = w3.shape
    tr, tc = _tile2d(r, wd, BLOCK_BYTES // 2)
    blk = pl.BlockSpec((1, tr, tc), lambda l, i, j: (l, i, j))
    s3 = _sds(w3.shape)
    d, mn, vn = pl.pallas_call(
        body, name="adamw", grid=(nl, r // tr, wd // tc), in_specs=[blk] * 4, out_specs=[blk] * 3,
        out_shape=[s3, s3, s3], compiler_params=_params(("parallel", "parallel", "parallel")),
    )(w3, g3, m3, v3)
    return d.reshape(shp), mn.reshape(shp), vn.reshape(shp)


ANY = pl.BlockSpec(memory_space=pl.ANY)
PIECE_BYTES = 1 << 20


def _place():
    return lax.axis_index("x"), lax.axis_index("y"), lax.axis_index("c")


def _pieces(shape, itemsize):
    if len(shape) >= 3:
        return [(i,) + p for i in range(shape[0]) for p in _pieces(shape[1:], itemsize)]
    rows = shape[0]
    row_bytes = itemsize
    for dsz in shape[1:]:
        row_bytes *= dsz
    k = 1
    while rows % (2 * k) == 0 and (rows // (2 * k)) % 16 == 0 and (rows // k) * row_bytes > PIECE_BYTES:
        k *= 2
    step = rows // k
    return [(pl.ds(j * step, step),) for j in range(k)]


def _split_start(make, src, dst, pieces):
    for p in pieces:
        make(src.at[p], dst.at[p]).start()
    return make(src, dst)


def _comm_call(body, name, arrs, out_shapes, n_remote, n_local):
    return pl.pallas_call(
        body, name=name, in_specs=[ANY] * len(arrs), out_specs=[ANY] * len(out_shapes), out_shape=out_shapes,
        scratch_shapes=[pltpu.SemaphoreType.DMA((n_remote,)), pltpu.SemaphoreType.DMA((n_remote,)),
                        pltpu.SemaphoreType.DMA((n_local,))],
    )(*arrs)


def all_gather8(arrs, name):
    n = len(arrs)
    pieces = [_pieces(a.shape, a.dtype.itemsize) for a in arrs]

    def body(*refs):
        ins, outs = refs[:n], refs[n:2 * n]
        send, recv, _ = refs[2 * n:]
        x, y, c = _place()
        me, sib = (x, y, c), (x, y, 1 - c)
        chips = [(1 - x, y), (x, 1 - y), (1 - x, 1 - y)]

        def slot(p):
            return 4 * p[0] + 2 * p[1] + p[2]

        def maker(t, k, to):
            def make(s, d):
                return pltpu.make_async_remote_copy(src_ref=s, dst_ref=d, send_sem=send.at[7 * t + k],
                                                    recv_sem=recv.at[7 * t + k], device_id=to, device_id_type=MESH)
            return make

        def landing(t, k, block):
            dst = outs[t].at[slot(block)]
            return maker(t, k, me)(dst, dst)

        sent = []
        for t in range(n):
            dst = outs[t].at[slot(me)]
            sent.append(_split_start(maker(t, 0, sib), ins[t], dst, pieces[t]))
            for j, chip in enumerate(chips):
                sent.append(_split_start(maker(t, 1 + j, (*chip, c)), ins[t], dst, pieces[t]))
        for j, chip in enumerate(chips):
            for t in range(n):
                landing(t, 1 + j, (*chip, c)).wait_recv()
                blk = outs[t].at[slot((*chip, c))]
                sent.append(_split_start(maker(t, 4 + j, sib), blk, blk, pieces[t]))
        for t in range(n):
            landing(t, 0, sib).wait_recv()
            for j, chip in enumerate(chips):
                landing(t, 4 + j, (*chip, 1 - c)).wait_recv()
        for cp in sent:
            cp.wait_send()

    outs = [_sds((N_DEV,) + a.shape, a.dtype) for a in arrs]
    got = _comm_call(body, name, arrs, outs, 7 * n, 1)
    x, y, c = _place()
    return [lax.dynamic_update_index_in_dim(g, a, 4 * x + 2 * y + c, 0) for g, a in zip(got, arrs)]


def sibling_send(arrs, name):
    n = len(arrs)
    pieces = [_pieces(a.shape[1:], a.dtype.itemsize) for a in arrs]

    def body(*refs):
        ins, theirs = refs[:n], refs[n:2 * n]
        send, recv, _ = refs[2 * n:]
        x, y, c = _place()
        rem = []
        for t in range(n):
            def make(s, d, t=t):
                return pltpu.make_async_remote_copy(src_ref=s, dst_ref=d, send_sem=send.at[t], recv_sem=recv.at[t],
                                                    device_id=(x, y, 1 - c), device_id_type=MESH)
            rem.append(_split_start(make, ins[t].at[1 - c], theirs[t], pieces[t]))
        for cp in rem:
            cp.wait_recv()
        for cp in rem:
            cp.wait_send()

    outs = [_sds(a.shape[1:], a.dtype) for a in arrs]
    return _comm_call(body, name, arrs, outs, n, 1)


def exchange_chips(arrs, name):
    n = len(arrs)
    pieces = [_pieces(a.shape[1:], a.dtype.itemsize) for a in arrs]

    def body(*refs):
        ins, outs = refs[:n], refs[n:2 * n]
        send, recv, _ = refs[2 * n:]
        x, y, c = _place()
        peers = [(1 - x, y), (x, 1 - y), (1 - x, 1 - y)]
        rem = []
        for t in range(n):
            for j, (px, py) in enumerate(peers):
                def make(s, d, t=t, j=j, px=px, py=py):
                    return pltpu.make_async_remote_copy(
                        src_ref=s, dst_ref=d, send_sem=send.at[3 * t + j], recv_sem=recv.at[3 * t + j],
                        device_id=(px, py, c), device_id_type=MESH)
                rem.append(_split_start(make, ins[t].at[2 * px + py], outs[t].at[j], pieces[t]))
        for cp in rem:
            cp.wait_recv()
        for cp in rem:
            cp.wait_send()

    outs = [_sds((N_CHIPS - 1,) + a.shape[1:], a.dtype) for a in arrs]
    return _comm_call(body, name, arrs, outs, 3 * n, 1)


def sibling_swap(arrs, name):
    n = len(arrs)
    pieces = [_pieces(a.shape, a.dtype.itemsize) for a in arrs]

    def body(*refs):
        ins, outs = refs[:n], refs[n:2 * n]
        send, recv, _ = refs[2 * n:]
        x, y, c = _place()
        rem = []
        for t in range(n):
            def make(s, d, t=t):
                return pltpu.make_async_remote_copy(src_ref=s, dst_ref=d, send_sem=send.at[t], recv_sem=recv.at[t],
                                                    device_id=(x, y, 1 - c), device_id_type=MESH)
            rem.append(_split_start(make, ins[t], outs[t], pieces[t]))
        for cp in rem:
            cp.wait_recv()
        for cp in rem:
            cp.wait_send()

    outs = [_sds(a.shape, a.dtype) for a in arrs]
    return _comm_call(body, name, arrs, outs, n, 1)


@jax.custom_vjp
def _build_w_in(w4):
    full = w4.reshape(-1, w4.shape[-1])
    parts = []
    for _, start, width, wp in PROJ_SEGS:
        if width:
            parts.append(full[start:start + width])
        if wp > width:
            parts.append(jnp.zeros((wp - width, full.shape[1]), full.dtype))
    return jnp.concatenate(parts, axis=0)


def _build_w_in_f(w4):
    return _build_w_in(w4), None


def _build_w_in_b(_, g):
    parts, at = [], 0
    for _, _, width, wp in PROJ_SEGS:
        if width:
            parts.append(g[at:at + width])
        at += wp
    return (jnp.concatenate(parts, axis=0).reshape(N_CHIPS, -1, g.shape[1]),)


_build_w_in.defvjp(_build_w_in_f, _build_w_in_b)


def _permute_w_uq(w):
    w3 = w.reshape(w.shape[0], MLA_HEADS, MLA_NOPE + MLA_ROPE)
    hr = MLA_ROPE // 2
    return jnp.concatenate([
        w3[:, :, :MLA_NOPE].reshape(w.shape[0], -1),
        w3[:, :, MLA_NOPE:MLA_NOPE + hr].reshape(w.shape[0], -1),
        w3[:, :, MLA_NOPE + hr:].reshape(w.shape[0], -1)], axis=1)


def _heads(t, d):
    return t.reshape(t.shape[0], -1, d).transpose(1, 0, 2)


def _unheads(t):
    return t.transpose(1, 0, 2).reshape(t.shape[1], -1)


def _layer(xh, mod, big, small, rope_q, rope_k):
    s = xh.shape[0]
    d = D_MODEL
    shift, scale, gate = mod[None, 0:d], mod[None, d:2 * d], mod[None, 2 * d:3 * d]
    w_al = _build_w_in(big["w_in"])
    proj = mod_mm(xh, small["norm_g"][None], scale, shift, w_al)
    gq, gk, gv, glr, mq, mkv, mkr, cb, cc, cx, _, z = split_proj(proj)

    rk = GLA_RANK
    hk = GLA_HEADS * GLA_DK
    wg = jnp.zeros((128, 2 * hk), F32)
    wg = wg.at[0:rk, 0:hk].set(small["gla_wg_f"]).at[rk:2 * rk, hk:].set(small["gla_wg_b"])
    bg = jnp.concatenate([small["gla_bg_f"], small["gla_bg_b"]])[None]
    la = gate_act(mm(glr, wg), bg)
    o_gla = gla(_heads(gq, GLA_DK), _heads(gk, GLA_DK), _heads(gv, GLA_DV),
                _heads(la[:, :hk], GLA_DK), _heads(la[:, hk:], GLA_DK))
    o_gla = rmsnorm(o_gla.reshape(GLA_HEADS * s, GLA_DV), small["gla_norm_g"][None])
    o_gla = _unheads(o_gla.reshape(GLA_HEADS, s, GLA_DV))

    nh = MLA_HEADS * MLA_NOPE
    hr = MLA_HEADS * MLA_ROPE // 2
    cq = rmsnorm(mq, small["mla_q_norm_g"][None])
    qm = mm(cq, _permute_w_uq(jnp.concatenate([big["w_uq"][j] for j in range(N_CHIPS)], axis=1)))
    qr = qm[:, nh:]
    qr = fma(qr, rope_q[0], jnp.concatenate([qr[:, hr:], qr[:, :hr]], axis=1), rope_q[1])
    ckv = rmsnorm(mkv, small["mla_kv_norm_g"][None])
    kv = mm(ckv, jnp.concatenate([big["w_ukv"][j] for j in range(N_CHIPS)], axis=1))
    kr = mkr[:, :MLA_ROPE]
    kr = fma(kr, rope_k[0], jnp.concatenate([kr[:, MLA_ROPE // 2:], kr[:, :MLA_ROPE // 2]], axis=1), rope_k[1])
    q3 = jnp.concatenate([qm[:, :nh].reshape(s, MLA_HEADS, MLA_NOPE), qr[:, :hr].reshape(s, MLA_HEADS, -1),
                          qr[:, hr:].reshape(s, MLA_HEADS, -1)], axis=-1).transpose(1, 0, 2)
    kv3 = kv.reshape(s, MLA_HEADS, MLA_NOPE + MLA_DV)
    k3 = jnp.concatenate([kv3[:, :, :MLA_NOPE], jnp.broadcast_to(kr[:, None, :], (s, MLA_HEADS, MLA_ROPE))],
                         axis=-1).transpose(1, 0, 2)
    v3 = kv3[:, :, MLA_NOPE:].transpose(1, 0, 2)
    o_mla = rmsnorm(_unheads(attn(q3, k3, v3)), small["mla_out_g"][None])

    cw = jnp.concatenate([small["conv_w"], jnp.zeros((5, CONV_CH), F32)], axis=0)
    o_conv = rmsnorm(conv_op(cb, cc, cx, cw), small["conv_out_g"][None])

    o = jnp.concatenate([o_gla, o_mla, o_conv], axis=1)
    w_out = big["w_out"].reshape(d, d)
    return out_block(o, z, w_out, xh, gate)


def _loss_fn(xh, mods, bigs, smalls, final_g, target, rope_q, rope_k):
    h = xh
    for l in range(DEPTH):
        h = _layer(h, mods[l], bigs[l], smalls[l], rope_q, rope_k)
    return loss_op(rmsnorm(h, final_g[None]), target)[0, 0]


SMALL_REPL = ("norm_g", "gla_bg_f", "gla_bg_b", "gla_norm_g", "mla_q_norm_g", "mla_kv_norm_g", "mla_out_g",
              "conv_out_g")
SMALL_SHARDED = ("gla_wg_f", "gla_wg_b", "conv_w")
BIG = ("w_in", "w_out", "w_uq", "w_ukv")


def kernel(x, c, positions, ada_w, ada_b, norm_g, w_in, gla_wg_f, gla_bg_f, gla_wg_b, gla_bg_b, gla_norm_g, mla_q_norm_g, mla_kv_norm_g, mla_w_uq, mla_w_ukv, mla_out_g, conv_w, conv_out_g, w_out, final_g, loss_target, m_ada_w, m_ada_b, m_norm_g, m_w_in, m_gla_wg_f, m_gla_bg_f, m_gla_wg_b, m_gla_bg_b, m_gla_norm_g, m_mla_q_norm_g, m_mla_kv_norm_g, m_mla_w_uq, m_mla_w_ukv, m_mla_out_g, m_conv_w, m_conv_out_g, m_w_out, m_final_g, v_ada_w, v_ada_b, v_norm_g, v_w_in, v_gla_wg_f, v_gla_bg_f, v_gla_wg_b, v_gla_bg_b, v_gla_norm_g, v_mla_q_norm_g, v_mla_kv_norm_g, v_mla_w_uq, v_mla_w_ukv, v_mla_out_g, v_conv_w, v_conv_out_g, v_w_out, v_final_g):
    xi, yi, ci = _place()
    chip = 2 * xi + yi
    dev = 2 * chip + ci
    s = x.shape[1]
    d = D_MODEL
    weights = dict(ada_w=ada_w, ada_b=ada_b, norm_g=norm_g, w_in=w_in, gla_wg_f=gla_wg_f, gla_bg_f=gla_bg_f,
                   gla_wg_b=gla_wg_b, gla_bg_b=gla_bg_b, gla_norm_g=gla_norm_g, mla_q_norm_g=mla_q_norm_g,
                   mla_kv_norm_g=mla_kv_norm_g, mla_w_uq=mla_w_uq, mla_w_ukv=mla_w_ukv, mla_out_g=mla_out_g,
                   conv_w=conv_w, conv_out_g=conv_out_g, w_out=w_out, final_g=final_g)
    m_in = dict(ada_w=m_ada_w, ada_b=m_ada_b, norm_g=m_norm_g, w_in=m_w_in, gla_wg_f=m_gla_wg_f, gla_bg_f=m_gla_bg_f,
                gla_wg_b=m_gla_wg_b, gla_bg_b=m_gla_bg_b, gla_norm_g=m_gla_norm_g, mla_q_norm_g=m_mla_q_norm_g,
                mla_kv_norm_g=m_mla_kv_norm_g, mla_w_uq=m_mla_w_uq, mla_w_ukv=m_mla_w_ukv, mla_out_g=m_mla_out_g,
                conv_w=m_conv_w, conv_out_g=m_conv_out_g, w_out=m_w_out, final_g=m_final_g)
    v_in = dict(ada_w=v_ada_w, ada_b=v_ada_b, norm_g=v_norm_g, w_in=v_w_in, gla_wg_f=v_gla_wg_f, gla_bg_f=v_gla_bg_f,
                gla_wg_b=v_gla_wg_b, gla_bg_b=v_gla_bg_b, gla_norm_g=v_gla_norm_g, mla_q_norm_g=v_mla_q_norm_g,
                mla_kv_norm_g=v_mla_kv_norm_g, mla_w_uq=v_mla_w_uq, mla_w_ukv=v_mla_w_ukv, mla_out_g=v_mla_out_g,
                conv_w=v_conv_w, conv_out_g=v_conv_out_g, w_out=v_w_out, final_g=v_final_g)

    def mine_bf16(w):
        return lax.dynamic_index_in_dim(w, ci, 0, keepdims=False).astype(BF16)

    g_c, g_in, g_out, g_uq, g_ukv, g_wgf, g_wgb, g_cw = all_gather8(
        [c, mine_bf16(jnp.swapaxes(w_in, 1, 2)), mine_bf16(w_out), mine_bf16(mla_w_uq), mine_bf16(mla_w_ukv),
         gla_wg_f, gla_wg_b, conv_w], "gather_weights")

    def by_layer(g):
        g4 = g.reshape((N_CHIPS, 2) + g.shape[1:])
        return [g4[:, l] for l in range(DEPTH)]

    bigs = [dict(w_in=a, w_out=b, w_uq=u, w_ukv=k)
            for a, b, u, k in zip(by_layer(g_in), by_layer(g_out), by_layer(g_uq), by_layer(g_ukv))]

    def unshard_cols(g):
        g4 = g[0::2]
        return g4.transpose(1, 2, 0, 3).reshape(g4.shape[1], g4.shape[2], -1)

    small_full = dict(gla_wg_f=unshard_cols(g_wgf), gla_wg_b=unshard_cols(g_wgb), conv_w=unshard_cols(g_cw))
    for nme in SMALL_REPL:
        small_full[nme] = weights[nme]
    smalls = [{nme: small_full[nme][l] for nme in SMALL_REPL + SMALL_SHARDED} for l in range(DEPTH)]

    c_act = _silu_rows(g_c[:, 0, :])
    c_act16 = jnp.concatenate([c_act, jnp.zeros_like(c_act)], axis=0)
    n_ada = ada_w.shape[2]
    parts = []
    for l in range(DEPTH):
        bias = lax.dynamic_slice_in_dim(ada_b[l], chip * n_ada, n_ada)[None]
        parts.append(_mm(c_act16, ada_w[l], bias=bias, name="ada_fwd"))
    g_mod, = all_gather8([jnp.stack(parts)], "gather_mod")
    mod_mine = lax.dynamic_index_in_dim(g_mod[0::2], dev, 2, keepdims=False)
    mods = mod_mine.transpose(1, 0, 2).reshape(DEPTH, 3 * d)

    inv_freq = ROPE_THETA ** (-jnp.arange(0, MLA_ROPE, 2, dtype=F32) / MLA_ROPE)
    ang = positions[0].astype(F32)[:, None] * inv_freq
    cos, sin = jnp.cos(ang), jnp.sin(ang)
    cos_h, sin_h = jnp.tile(cos, (1, MLA_HEADS)), jnp.tile(sin, (1, MLA_HEADS))
    rope_q = (jnp.concatenate([cos_h, cos_h], axis=1), jnp.concatenate([-sin_h, sin_h], axis=1))
    rope_k = (jnp.concatenate([cos, cos], axis=1), jnp.concatenate([-sin, sin], axis=1))

    loss_dev, (dx, dmods, dbigs, dsmalls, dfinal) = jax.value_and_grad(_loss_fn, argnums=(0, 1, 2, 3, 4))(
        x[0], mods, bigs, smalls, final_g, loss_target[0], rope_q, rope_k)
    loss = lax.psum(loss_dev, ("x", "y", "c"))

    pieces = [dmods.reshape(-1), dfinal]
    for nme in SMALL_REPL + SMALL_SHARDED:
        pieces.append(jnp.stack([dsmalls[l][nme] for l in range(DEPTH)]).reshape(-1))
    sizes = [p.shape[0] for p in pieces]
    flat = jnp.concatenate(pieces)
    padn = (-flat.shape[0]) % 128
    flat = jnp.concatenate([flat, jnp.zeros((padn,), F32)])[None]
    g_small, = all_gather8([flat], "gather_small_grads")
    total = _sum_devices(g_small)[0]
    offs, at = [], 0
    for n_el in sizes:
        offs.append(at)
        at += n_el

    def piece(i, shape):
        return total[offs[i]:offs[i] + sizes[i]].reshape(shape)

    grads = {"ada_b": piece(0, (DEPTH, 3 * d)), "final_g": piece(1, (d,))}
    for i, nme in enumerate(SMALL_REPL + SMALL_SHARDED):
        full = piece(2 + i, small_full[nme].shape)
        if nme in SMALL_SHARDED:
            ncol = weights[nme].shape[2]
            full = lax.dynamic_slice_in_dim(full, chip * ncol, ncol, axis=2)
        grads[nme] = full

    dmod_all = g_small[:, 0, :DEPTH * 3 * d].reshape(N_DEV, DEPTH, 3 * d)
    dmod_cols = lax.dynamic_slice_in_dim(dmod_all, chip * n_ada, n_ada, axis=2)
    g_ada = []
    for l in range(DEPTH):
        dm16 = jnp.concatenate([dmod_cols[:, l], jnp.zeros((N_DEV, n_ada), F32)], axis=0)
        g_ada.append(_mm(c_act16, dm16, ta=True, name="ada_bwd"))
    grads["ada_w"] = jnp.stack(g_ada)

    stacked = [jnp.stack([dbigs[l][nme] for l in range(DEPTH)]) for nme in BIG]
    c_idx = jnp.reshape(ci, (1,)).astype(jnp.int32)
    chip_idx = jnp.reshape(chip, (1,)).astype(jnp.int32)
    theirs = sibling_send(stacked, "reduce_sibling")
    pair = [_add_pair(a, b, c_idx) for a, b in zip(stacked, theirs)]
    landed = exchange_chips(pair, "reduce_chips")
    reduced = [_add_chips(p, q, chip_idx) for p, q in zip(pair, landed)]
    others = sibling_swap(reduced, "share_sibling")
    for nme, own, other in zip(("w_in", "w_out", "mla_w_uq", "mla_w_ukv"), reduced, others):
        grads[nme] = jnp.stack([jnp.where(ci == l, own, other) for l in range(DEPTH)])

    order = list(weights)
    delta, new_m, new_v = {}, {}, {}
    for nme in order:
        if nme == "w_in":
            w_t, m_t, v_t = (jnp.swapaxes(t, 1, 2) for t in (w_in, m_w_in, v_w_in))
            res = _adamw(w_t, grads[nme], m_t, v_t)
            delta[nme], new_m[nme], new_v[nme] = (jnp.swapaxes(t, 1, 2) for t in res)
            grads[nme] = jnp.swapaxes(grads[nme], 1, 2)
            continue
        delta[nme], new_m[nme], new_v[nme] = _adamw(weights[nme], grads[nme], m_in[nme], v_in[nme])
    return (loss, dx[None], *[grads[n_] for n_ in order], *[delta[n_] for n_ in order],
            *[new_m[n_] for n_ in order], *[new_v[n_] for n_ in order])
```

```python
import functools

import jax
import jax.numpy as jnp
from jax import lax
from jax.experimental import pallas as pl
from jax.experimental.pallas import tpu as pltpu

F32 = jnp.float32
BF16 = jnp.bfloat16
MESH = pl.DeviceIdType.MESH
HIGHEST = lax.Precision.HIGHEST

DEPTH = 2
D_MODEL = 2048
GLA_HEADS = 6
GLA_DK = 64
GLA_DV = 128
GLA_RANK = 16
GLA_TEMP = 16.0
GLA_CHUNK = 64
GLA_W = GLA_HEADS * GLA_DV
MLA_HEADS = 6
MLA_QL = 384
MLA_KVL = 256
MLA_NOPE = 128
MLA_ROPE = 64
MLA_DV = 128
MLA_W = MLA_HEADS * MLA_DV
CONV_CH = D_MODEL - GLA_W - MLA_W
ROPE_THETA = 10000.0
EPS = 1e-6
IN_DIM = 5856
N_CHIPS = 4
N_DEV = 8

ADAM_LR = 0.001
ADAM_B1 = 0.9
ADAM_B2 = 0.999
ADAM_EPS = 1e-08
ADAM_WD = 0.01
ADAM_STEP = 10

PROJ_SEGS = (
    ("gq", 0, 384, 384), ("gk", 384, 384, 384), ("gv", 768, 768, 768), ("glr", 1536, 32, 128),
    ("mq", 1568, 384, 384), ("mkv", 1952, 256, 256), ("mkr", 2208, 64, 128),
    ("cb", 2272, 512, 512), ("cc", 2784, 512, 512), ("cx", 3296, 512, 512),
    ("pad", 3808, 0, 128), ("z", 3808, 2048, 2048),
)
PROJ_AL = sum(s[3] for s in PROJ_SEGS)

VMEM_LIMIT = 48 * 1024 * 1024
BLOCK_BYTES = 2 * 1024 * 1024


def _params(sem=None):
    return pltpu.CompilerParams(dimension_semantics=sem, vmem_limit_bytes=VMEM_LIMIT)


def _dot(a, b, ca, cb, precision=None):
    return lax.dot_general(a, b, (((ca,), (cb,)), ((), ())), preferred_element_type=F32, precision=precision)


def _tile(dim, prefs):
    for t in prefs:
        if dim % t == 0:
            return t
    return dim


def _pick_rows(rows, width, itemsize=4):
    for t in (2048, 1024, 512, 256, 128, 64, 32, 16, 8):
        if rows % t == 0 and t * width * itemsize <= BLOCK_BYTES:
            return t
    return rows


def _mm(a, b, *, ta=False, tb=False, bias=None, out_dtype=F32, name="mm"):
    if ta:
        K, M = a.shape
    else:
        M, K = a.shape
    if tb:
        N, Kb = b.shape
    else:
        Kb, N = b.shape
    assert K == Kb, (a.shape, b.shape, ta, tb)
    tm = _tile(M, (512, 256, 128))
    tn = _tile(N, (1024, 512, 384, 256, 128))
    tk = _tile(K, (2048, 1024, 512, 256, 128))
    nk = K // tk
    has_bias = bias is not None

    def body(*refs):
        a_ref, b_ref = refs[0], refs[1]
        bias_ref = refs[2] if has_bias else None
        o_ref = refs[3 if has_bias else 2]
        part = _dot(a_ref[...].astype(BF16), b_ref[...].astype(BF16), 0 if ta else 1, 1 if tb else 0)

        def finish(r):
            if has_bias:
                r = r + bias_ref[...]
            o_ref[...] = r.astype(out_dtype)

        if nk == 1:
            finish(part)
            return
        acc_ref = refs[-1]
        k = pl.program_id(2)

        @pl.when(k == 0)
        def _():
            acc_ref[...] = part

        @pl.when(k != 0)
        def _():
            acc_ref[...] += part

        @pl.when(k == nk - 1)
        def _():
            finish(acc_ref[...])

    a_spec = pl.BlockSpec((tk, tm), lambda i, j, k: (k, i)) if ta else pl.BlockSpec((tm, tk), lambda i, j, k: (i, k))
    b_spec = pl.BlockSpec((tn, tk), lambda i, j, k: (j, k)) if tb else pl.BlockSpec((tk, tn), lambda i, j, k: (k, j))
    in_specs = [a_spec, b_spec]
    args = [a, b]
    if has_bias:
        in_specs.append(pl.BlockSpec((1, tn), lambda i, j, k: (0, j)))
        args.append(bias)
    return pl.pallas_call(
        body, name=name, grid=(M // tm, N // tn, nk),
        in_specs=in_specs, out_specs=pl.BlockSpec((tm, tn), lambda i, j, k: (i, j)),
        out_shape=jax.ShapeDtypeStruct((M, N), out_dtype),
        scratch_shapes=[pltpu.VMEM((tm, tn), F32)] if nk > 1 else [],
        compiler_params=_params(("parallel", "parallel", "arbitrary")),
    )(*args)


@jax.custom_vjp
def mm(a, b):
    return _mm(a, b, name="mm_fwd")


def _mm_f(a, b):
    return _mm(a, b, name="mm_fwd"), (a, b)


def _mm_b(res, g):
    a, b = res
    return _mm(g, b, tb=True, out_dtype=a.dtype, name="mm_da"), _mm(a, g, ta=True, out_dtype=b.dtype, name="mm_db")


mm.defvjp(_mm_f, _mm_b)


def _rows(body, name, tiled, full, tiled_out, acc_out, tr=None):
    rows = tiled[0].shape[0]
    if tr is None:
        width = max([a.shape[1] for a in tiled] + [s.shape[1] for s in tiled_out])
        tr = _pick_rows(rows, width)
    in_specs = [pl.BlockSpec((tr, a.shape[1]), lambda i: (i, 0)) for a in tiled]
    in_specs += [pl.BlockSpec(a.shape, lambda i: (0, 0)) for a in full]
    out_specs = [pl.BlockSpec((tr, s.shape[1]), lambda i: (i, 0)) for s in tiled_out]
    out_specs += [pl.BlockSpec(s.shape, lambda i: (0, 0)) for s in acc_out]

    def wrapped(*refs):
        body(pl.program_id(0), *refs)

    outs = pl.pallas_call(
        wrapped, name=name, grid=(rows // tr,), in_specs=in_specs, out_specs=out_specs,
        out_shape=list(tiled_out) + list(acc_out),
        compiler_params=_params(("arbitrary",)),
    )(*tiled, *full)
    return outs


def _sds(shape, dtype=F32):
    return jax.ShapeDtypeStruct(tuple(shape), dtype)


def _acc(step, ref, val):
    @pl.when(step == 0)
    def _():
        ref[...] = val

    @pl.when(step != 0)
    def _():
        ref[...] += val


def _colsum(v):
    return jnp.sum(v, axis=0, keepdims=True)


def _rstd(x):
    return lax.rsqrt(jnp.mean(x * x, axis=-1, keepdims=True) + EPS)


@jax.custom_vjp
def rmsnorm(x, g):
    def body(i, x_ref, g_ref, o_ref):
        x = x_ref[...]
        o_ref[...] = x * _rstd(x) * g_ref[...]
    return _rows(body, "rmsnorm_fwd", [x], [g], [_sds(x.shape)], [])[0]


def _rmsnorm_f(x, g):
    return rmsnorm(x, g), (x, g)


def _rmsnorm_b(res, dy):
    x, g = res

    def body(i, x_ref, dy_ref, g_ref, dx_ref, dg_ref):
        x = x_ref[...]
        dy = dy_ref[...]
        r = _rstd(x)
        xh = x * r
        dxh = dy * g_ref[...]
        dx_ref[...] = r * (dxh - xh * jnp.mean(dxh * xh, axis=-1, keepdims=True))
        _acc(i, dg_ref, _colsum(dy * xh))

    dx, dg = _rows(body, "rmsnorm_bwd", [x, dy], [g], [_sds(x.shape)], [_sds(g.shape)])
    return dx, dg


rmsnorm.defvjp(_rmsnorm_f, _rmsnorm_b)


def _modulate(x, g, scale, shift):
    def body(i, x_ref, g_ref, sc_ref, sh_ref, o_ref):
        x = x_ref[...]
        xn = x * _rstd(x) * g_ref[...]
        o_ref[...] = (xn * (1.0 + sc_ref[...]) + sh_ref[...]).astype(BF16)
    return _rows(body, "modulate_fwd", [x], [g, scale, shift], [_sds(x.shape, BF16)], [])[0]


def _modulate_bwd(x, g, scale, shift, dh):
    def body(i, x_ref, dh_ref, g_ref, sc_ref, dx_ref, dg_ref, dsc_ref, dsh_ref):
        x = x_ref[...]
        dh = dh_ref[...]
        gv = g_ref[...]
        r = _rstd(x)
        xh = x * r
        dxn = dh * (1.0 + sc_ref[...])
        dxh = dxn * gv
        dx_ref[...] = r * (dxh - xh * jnp.mean(dxh * xh, axis=-1, keepdims=True))
        _acc(i, dg_ref, _colsum(dxn * xh))
        _acc(i, dsc_ref, _colsum(dh * (xh * gv)))
        _acc(i, dsh_ref, _colsum(dh))

    v = _sds(g.shape)
    return _rows(body, "modulate_bwd", [x, dh], [g, scale], [_sds(x.shape)], [v, v, v])


@jax.custom_vjp
def mod_mm(x, g, scale, shift, wt):
    return _mm(_modulate(x, g, scale, shift), wt, tb=True, name="mm_in")


def _mod_mm_f(x, g, scale, shift, wt):
    h = _modulate(x, g, scale, shift)
    return _mm(h, wt, tb=True, name="mm_in"), (x, g, scale, shift, wt, h)


def _mod_mm_b(res, dproj):
    x, g, scale, shift, wt, h = res
    dproj = dproj.astype(BF16)
    dh = _mm(dproj, wt, name="mm_in_dh")
    dwt = _mm(dproj, h, ta=True, out_dtype=wt.dtype, name="mm_in_dw")
    dx, dg, dsc, dsh = _modulate_bwd(x, g, scale, shift, dh)
    return dx, dg, dsc, dsh, dwt


mod_mm.defvjp(_mod_mm_f, _mod_mm_b)


def _sigmoid(z):
    return 1.0 / (1.0 + jnp.exp(-z))


def _gate_mul(o, z):
    def body(i, o_ref, z_ref, y_ref):
        z = z_ref[...]
        y_ref[...] = (o_ref[...] * (z * _sigmoid(z))).astype(BF16)
    return _rows(body, "gate_mul_fwd", [o, z], [], [_sds(o.shape, BF16)], [])[0]


def _gate_mul_bwd(o, z, dy):
    def body(i, o_ref, z_ref, dy_ref, do_ref, dz_ref):
        z = z_ref[...]
        dy = dy_ref[...]
        s = _sigmoid(z)
        do_ref[...] = dy * (z * s)
        dz_ref[...] = dy * o_ref[...] * (s * (1.0 + z * (1.0 - s)))
    return _rows(body, "gate_mul_bwd", [o, z, dy], [], [_sds(o.shape), _sds(o.shape)], [])


def _residual(x, u, gate):
    def body(i, x_ref, u_ref, g_ref, o_ref):
        o_ref[...] = x_ref[...] + g_ref[...] * u_ref[...]
    return _rows(body, "residual_fwd", [x, u], [gate], [_sds(x.shape)], [])[0]


def _residual_bwd(d, u, gate):
    def body(i, d_ref, u_ref, g_ref, du_ref, dg_ref):
        d = d_ref[...]
        du_ref[...] = (g_ref[...] * d).astype(BF16)
        _acc(i, dg_ref, _colsum(d * u_ref[...]))

    return _rows(body, "residual_bwd", [d, u], [gate], [_sds(u.shape, BF16)], [_sds(gate.shape)])


@jax.custom_vjp
def out_block(o, z, w, x, gate):
    return _residual(x, _mm(_gate_mul(o, z), w, name="mm_out"), gate)


def _out_block_f(o, z, w, x, gate):
    y = _gate_mul(o, z)
    u = _mm(y, w, name="mm_out")
    return _residual(x, u, gate), (o, z, w, y, u, gate)


def _out_block_b(res, d):
    o, z, w, y, u, gate = res
    du, dgate = _residual_bwd(d, u, gate)
    dy = _mm(du, w, tb=True, name="mm_out_dy")
    dw = _mm(y, du, ta=True, out_dtype=w.dtype, name="mm_out_dw")
    do, dz = _gate_mul_bwd(o, z, dy)
    return do, dz, dw, d, dgate


out_block.defvjp(_out_block_f, _out_block_b)


@jax.custom_vjp
def gate_act(u, b):
    def body(i, u_ref, b_ref, o_ref):
        t = u_ref[...] + b_ref[...]
        o_ref[...] = (jnp.minimum(t, 0.0) - jnp.log(1.0 + jnp.exp(-jnp.abs(t)))) / GLA_TEMP
    return _rows(body, "gate_act_fwd", [u], [b], [_sds(u.shape)], [])[0]


def _gate_act_f(u, b):
    return gate_act(u, b), (u, b)


def _gate_act_b(res, d):
    u, b = res

    def body(i, u_ref, d_ref, b_ref, du_ref, db_ref):
        t = u_ref[...] + b_ref[...]
        du = d_ref[...] * _sigmoid(-t) / GLA_TEMP
        du_ref[...] = du
        _acc(i, db_ref, _colsum(du))

    du, db = _rows(body, "gate_act_bwd", [u, d], [b], [_sds(u.shape)], [_sds(b.shape)])
    return du, db


gate_act.defvjp(_gate_act_f, _gate_act_b)


@jax.custom_vjp
def fma(a, b, c, d):
    def body(i, a_ref, b_ref, c_ref, d_ref, o_ref):
        o_ref[...] = a_ref[...] * b_ref[...] + c_ref[...] * d_ref[...]
    return _rows(body, "fma_fwd", [a, b, c, d], [], [_sds(a.shape)], [])[0]


def _fma_f(a, b, c, d):
    return fma(a, b, c, d), (b, d)


def _fma_b(res, g):
    b, d = res

    def body(i, g_ref, b_ref, d_ref, da_ref, dc_ref):
        g = g_ref[...]
        da_ref[...] = g * b_ref[...]
        dc_ref[...] = g * d_ref[...]

    da, dc = _rows(body, "fma_bwd", [g, b, d], [], [_sds(g.shape), _sds(g.shape)], [])
    return da, jnp.zeros_like(b), dc, jnp.zeros_like(d)


fma.defvjp(_fma_f, _fma_b)


def _silu_rows(c):
    def body(i, c_ref, o_ref):
        v = c_ref[...]
        o_ref[...] = v * _sigmoid(v)
    return _rows(body, "silu", [c], [], [_sds(c.shape)], [])[0]


@jax.custom_vjp
def loss_op(y, t):
    return _loss_fwd(y, t)[0]


def _loss_fwd(y, t):
    inv = 1.0 / y.shape[1]

    def body(i, y_ref, t_ref, d_ref, l_ref):
        e = y_ref[...] - t_ref[...]
        d_ref[...] = e * inv
        _acc(i, l_ref, jnp.sum(_colsum(e * e), axis=1, keepdims=True) * (0.5 * inv))

    d, l = _rows(body, "loss_fwd", [y, t], [], [_sds(y.shape)], [_sds((1, 1))])
    return l, d


def _loss_f(y, t):
    l, d = _loss_fwd(y, t)
    return l, d


def _loss_b(d, g):
    return d * g, jnp.zeros_like(d)


loss_op.defvjp(_loss_f, _loss_b)


def _conv_terms(cc, cx, rows, n):
    u = cc * cx
    up = jnp.where(rows == 0, 0.0, pltpu.roll(u, 1, 0))
    un = jnp.where(rows == n - 1, 0.0, pltpu.roll(u, n - 1, 0))
    return u, up, un


CONV_COLS = 128


def _conv_specs(s, n_in):
    blk = pl.BlockSpec((s, CONV_COLS), lambda j: (0, j))
    wblk = pl.BlockSpec((8, CONV_COLS), lambda j: (0, j))
    return [blk] * n_in + [wblk], blk, wblk


@jax.custom_vjp
def conv_op(cb, cc, cx, w):
    s, ch = cb.shape

    def body(cb_ref, cc_ref, cx_ref, w_ref, o_ref):
        rows = lax.broadcasted_iota(jnp.int32, (s, CONV_COLS), 0)
        u, up, un = _conv_terms(cc_ref[...], cx_ref[...], rows, s)
        conv = up * w_ref[0:1, :] + u * w_ref[1:2, :] + un * w_ref[2:3, :]
        o_ref[...] = cb_ref[...] * conv

    in_specs, blk, _ = _conv_specs(s, 3)
    return pl.pallas_call(
        body, name="conv_fwd", grid=(ch // CONV_COLS,), in_specs=in_specs, out_specs=blk,
        out_shape=_sds(cb.shape), compiler_params=_params(("parallel",)),
    )(cb, cc, cx, w)


def _conv_f(cb, cc, cx, w):
    return conv_op(cb, cc, cx, w), (cb, cc, cx, w)


def _conv_b(res, d):
    cb, cc, cx, w = res
    s, ch = cb.shape

    def body(cb_ref, cc_ref, cx_ref, d_ref, w_ref, dcb_ref, dcc_ref, dcx_ref, dw_ref):
        rows = lax.broadcasted_iota(jnp.int32, (s, CONV_COLS), 0)
        cc_v = cc_ref[...]
        cx_v = cx_ref[...]
        u, up, un = _conv_terms(cc_v, cx_v, rows, s)
        w0, w1, w2 = w_ref[0:1, :], w_ref[1:2, :], w_ref[2:3, :]
        dv = d_ref[...]
        dcb_ref[...] = dv * (up * w0 + u * w1 + un * w2)
        dconv = dv * cb_ref[...]
        d_next = jnp.where(rows == s - 1, 0.0, pltpu.roll(dconv, s - 1, 0))
        d_prev = jnp.where(rows == 0, 0.0, pltpu.roll(dconv, 1, 0))
        du = w0 * d_next + w1 * dconv + w2 * d_prev
        dcc_ref[...] = du * cx_v
        dcx_ref[...] = du * cc_v
        dw_ref[...] = jnp.zeros_like(dw_ref)
        dw_ref[0:1, :] = _colsum(dconv * up)
        dw_ref[1:2, :] = _colsum(dconv * u)
        dw_ref[2:3, :] = _colsum(dconv * un)

    in_specs, blk, wblk = _conv_specs(s, 4)
    v = _sds(cb.shape)
    return tuple(pl.pallas_call(
        body, name="conv_bwd", grid=(ch // CONV_COLS,), in_specs=in_specs, out_specs=[blk, blk, blk, wblk],
        out_shape=[v, v, v, _sds(w.shape)], compiler_params=_params(("parallel",)),
    )(cb, cc, cx, d, w))


conv_op.defvjp(_conv_f, _conv_b)


def _gla_masks(rev):
    c = GLA_CHUNK
    row = lax.broadcasted_iota(jnp.int32, (c, c), 0)
    col = lax.broadcasted_iota(jnp.int32, (c, c), 1)
    mask = (row < col) if rev else (row >= col)
    return rev, mask


def _chunk_cumsum(g, rev):
    c = g.shape[0]
    row = lax.broadcasted_iota(jnp.int32, g.shape, 0)
    b = g
    s = 1
    while s < c:
        if rev:
            b = b + jnp.where(row < c - s, pltpu.roll(b, c - s, 0), 0.0)
        else:
            b = b + jnp.where(row >= s, pltpu.roll(b, s, 0), 0.0)
        s *= 2
    return b


GLA_UNROLL = 4


def _gla_rows(n):
    return pl.ds(pl.multiple_of(n * GLA_CHUNK, GLA_CHUNK), GLA_CHUNK)


def _gla_scan(s_ref, bt_ref, st_ref, n_chunks, descending):
    st_ref[...] = jnp.zeros_like(st_ref)

    def step(i, carry):
        n = (n_chunks - 1 - i) if descending else i
        own = s_ref[0, n]
        st = st_ref[...]
        s_ref[0, n] = st
        st_ref[...] = st * jnp.exp(bt_ref[n]) + own
        return carry

    lax.fori_loop(0, n_chunks, step, 0)


def _gla_fwd(q, k, v, gf, gb):
    h, s, dk = q.shape
    dv = v.shape[2]
    n_chunks = s // GLA_CHUNK
    scale = GLA_DK ** -0.5

    def body(q_ref, k_ref, v_ref, gf_ref, gb_ref, o_ref, sf_ref, sb_ref, bf_ref, bb_ref, btf_ref, btb_ref, st_ref):
        dirs = ((False, gf_ref, sf_ref, bf_ref, btf_ref), (True, gb_ref, sb_ref, bb_ref, btb_ref))
        masks = [_gla_masks(rev) for rev in (False, True)]

        def prepare(n, carry):
            rows = _gla_rows(n)
            kk = k_ref[0, rows, :]
            vb = v_ref[0, rows, :].astype(BF16)
            for rev, g_ref, s_ref, b_ref, bt_ref in dirs:
                g = g_ref[0, rows, :]
                b = _chunk_cumsum(g, rev)
                bt = _colsum(g)
                b_ref[rows, :] = b
                bt_ref[n] = bt
                s_ref[0, n] = _dot(vb, (kk * jnp.exp(bt - b)).astype(BF16), 0, 0)
            return carry

        lax.fori_loop(0, n_chunks, prepare, 0, unroll=GLA_UNROLL)
        for rev, _, s_ref, _, bt_ref in dirs:
            _gla_scan(s_ref, bt_ref, st_ref, n_chunks, descending=rev)

        def emit(n, carry):
            rows = _gla_rows(n)
            qs = q_ref[0, rows, :] * scale
            kk = k_ref[0, rows, :]
            vb = v_ref[0, rows, :].astype(BF16)
            o = None
            for (rev, _, s_ref, b_ref, _), (_, mask) in zip(dirs, masks):
                b = b_ref[rows, :]
                qd = (qs * jnp.exp(b)).astype(BF16)
                ki = (kk * jnp.exp(-b)).astype(BF16)
                a = jnp.where(mask, _dot(qd, ki, 1, 1), 0.0).astype(BF16)
                od = _dot(a, vb, 1, 0) + _dot(qd, s_ref[0, n].astype(BF16), 1, 1)
                o = od if o is None else o + od
            o_ref[0, rows, :] = o
            return carry

        lax.fori_loop(0, n_chunks, emit, 0, unroll=GLA_UNROLL)

    blk_k = pl.BlockSpec((1, s, dk), lambda i: (i, 0, 0))
    blk_v = pl.BlockSpec((1, s, dv), lambda i: (i, 0, 0))
    blk_s = pl.BlockSpec((1, n_chunks, dv, dk), lambda i: (i, 0, 0, 0))
    st = _sds((h, n_chunks, dv, dk))
    scratch = [pltpu.VMEM((s, dk), F32), pltpu.VMEM((s, dk), F32), pltpu.VMEM((n_chunks, 1, dk), F32),
               pltpu.VMEM((n_chunks, 1, dk), F32), pltpu.VMEM((dv, dk), F32)]
    return pl.pallas_call(
        body, name="gla_fwd", grid=(h,), in_specs=[blk_k, blk_k, blk_v, blk_k, blk_k],
        out_specs=[blk_v, blk_s, blk_s], out_shape=[_sds(v.shape), st, st], scratch_shapes=scratch,
        compiler_params=_params(("parallel",)),
    )(q, k, v, gf, gb)


def _gla_bwd(q, k, v, gf, gb, sf, sb, do):
    h, s, dk = q.shape
    dv = v.shape[2]
    c = GLA_CHUNK
    n_chunks = s // c
    scale = GLA_DK ** -0.5

    def body(q_ref, k_ref, v_ref, gf_ref, gb_ref, sf_ref, sb_ref, do_ref, dq_ref, dk_ref, dv_ref, dgf_ref, dgb_ref,
             bf_ref, bb_ref, btf_ref, btb_ref, dsf_ref, dsb_ref, st_ref):
        dirs = ((False, gf_ref, sf_ref, bf_ref, btf_ref, dsf_ref, dgf_ref),
                (True, gb_ref, sb_ref, bb_ref, btb_ref, dsb_ref, dgb_ref))
        masks = [_gla_masks(rev) for rev in (False, True)]
        rowc = lax.broadcasted_iota(jnp.int32, (c, dk), 0)

        def prepare(n, carry):
            rows = _gla_rows(n)
            qs = q_ref[0, rows, :] * scale
            do_b = do_ref[0, rows, :].astype(BF16)
            for rev, g_ref, _, b_ref, bt_ref, ds_ref, _ in dirs:
                g = g_ref[0, rows, :]
                b = _chunk_cumsum(g, rev)
                b_ref[rows, :] = b
                bt_ref[n] = _colsum(g)
                ds_ref[0, n] = _dot(do_b, (qs * jnp.exp(b)).astype(BF16), 0, 0)
            return carry

        lax.fori_loop(0, n_chunks, prepare, 0, unroll=GLA_UNROLL)
        for rev, _, _, _, bt_ref, ds_ref, _ in dirs:
            _gla_scan(ds_ref, bt_ref, st_ref, n_chunks, descending=not rev)

        def emit(n, carry):
            rows = _gla_rows(n)
            qs = q_ref[0, rows, :] * scale
            kk = k_ref[0, rows, :]
            vb = v_ref[0, rows, :].astype(BF16)
            do_b = do_ref[0, rows, :].astype(BF16)
            dq = dkk = dvv = None
            for (rev, _, s_ref, b_ref, bt_ref, ds_ref, dg_ref), (_, mask) in zip(dirs, masks):
                b = b_ref[rows, :]
                bt = bt_ref[n]
                eb = jnp.exp(b)
                enb = jnp.exp(-b)
                etb = jnp.exp(bt - b)
                ebt = jnp.exp(bt)
                qd = qs * eb
                ki = kk * enb
                ke = kk * etb
                qd_b, ki_b, ke_b = qd.astype(BF16), ki.astype(BF16), ke.astype(BF16)
                st = s_ref[0, n]
                dst = ds_ref[0, n]
                dst_b = dst.astype(BF16)
                a = jnp.where(mask, _dot(qd_b, ki_b, 1, 1), 0.0).astype(BF16)
                da = jnp.where(mask, _dot(do_b, vb, 1, 1), 0.0).astype(BF16)
                dv_d = _dot(a, do_b, 0, 0) + _dot(ke_b, dst_b, 1, 1)
                dqd = _dot(da, ki_b, 1, 0) + _dot(do_b, st.astype(BF16), 1, 0)
                dki = _dot(da, qd_b, 0, 0)
                dke = _dot(vb, dst_b, 1, 0)
                dbt = _colsum(st * dst) * ebt + _colsum(dke * ke)
                db = dqd * qd - dki * ki - dke * ke
                db = db + jnp.where(rowc == (0 if rev else c - 1), dbt, 0.0)
                dg_ref[0, rows, :] = _chunk_cumsum(db, not rev)
                dq_d = dqd * eb * scale
                dk_d = dki * enb + dke * etb
                dq = dq_d if dq is None else dq + dq_d
                dkk = dk_d if dkk is None else dkk + dk_d
                dvv = dv_d if dvv is None else dvv + dv_d
            dq_ref[0, rows, :] = dq
            dk_ref[0, rows, :] = dkk
            dv_ref[0, rows, :] = dvv
            return carry

        lax.fori_loop(0, n_chunks, emit, 0, unroll=2)

    blk_k = pl.BlockSpec((1, s, dk), lambda i: (i, 0, 0))
    blk_v = pl.BlockSpec((1, s, dv), lambda i: (i, 0, 0))
    blk_s = pl.BlockSpec((1, n_chunks, dv, dk), lambda i: (i, 0, 0, 0))
    vk, vv = _sds(q.shape), _sds(v.shape)
    scratch = [pltpu.VMEM((s, dk), F32), pltpu.VMEM((s, dk), F32), pltpu.VMEM((n_chunks, 1, dk), F32),
               pltpu.VMEM((n_chunks, 1, dk), F32), pltpu.VMEM((1, n_chunks, dv, dk), F32),
               pltpu.VMEM((1, n_chunks, dv, dk), F32), pltpu.VMEM((dv, dk), F32)]
    return pl.pallas_call(
        body, name="gla_bwd", grid=(h,), in_specs=[blk_k, blk_k, blk_v, blk_k, blk_k, blk_s, blk_s, blk_v],
        out_specs=[blk_k, blk_k, blk_v, blk_k, blk_k], out_shape=[vk, vk, vv, vk, vk],
        scratch_shapes=scratch, compiler_params=_params(("parallel",)),
    )(q, k, v, gf, gb, sf, sb, do)


@jax.custom_vjp
def gla(q, k, v, gf, gb):
    return _gla_fwd(q, k, v, gf, gb)[0]


def _gla_f(q, k, v, gf, gb):
    o, sf, sb = _gla_fwd(q, k, v, gf, gb)
    return o, (q, k, v, gf, gb, sf, sb)


def _gla_b(res, do):
    return tuple(_gla_bwd(*res, do))


gla.defvjp(_gla_f, _gla_b)


ATTN_TQ = 256


def _attn_fwd(q, k, v):
    h, s, dq = q.shape
    dv = v.shape[2]
    tq = min(ATTN_TQ, s)
    scale = (MLA_NOPE + MLA_ROPE) ** -0.5

    def body(q_ref, k_ref, v_ref, o_ref, lse_ref):
        sc = _dot(q_ref[0].astype(BF16), k_ref[0].astype(BF16), 1, 1) * scale
        m = jnp.max(sc, axis=-1, keepdims=True)
        p = jnp.exp(sc - m)
        l = jnp.sum(p, axis=-1, keepdims=True)
        p = p * (1.0 / l)
        o_ref[0] = _dot(p.astype(BF16), v_ref[0].astype(BF16), 1, 0)
        lse_ref[0] = m + jnp.log(l)

    return pl.pallas_call(
        body, name="attn_fwd", grid=(h, s // tq),
        in_specs=[pl.BlockSpec((1, tq, dq), lambda i, j: (i, j, 0)), pl.BlockSpec((1, s, dq), lambda i, j: (i, 0, 0)),
                  pl.BlockSpec((1, s, dv), lambda i, j: (i, 0, 0))],
        out_specs=[pl.BlockSpec((1, tq, dv), lambda i, j: (i, j, 0)), pl.BlockSpec((1, tq, 1), lambda i, j: (i, j, 0))],
        out_shape=[_sds((h, s, dv)), _sds((h, s, 1))],
        compiler_params=_params(("parallel", "parallel")),
    )(q, k, v)


def _attn_bwd(q, k, v, o, lse, do):
    h, s, dq = q.shape
    dv = v.shape[2]
    tq = min(ATTN_TQ, s)
    scale = (MLA_NOPE + MLA_ROPE) ** -0.5

    def body(q_ref, k_ref, v_ref, o_ref, lse_ref, do_ref, dq_ref, dk_ref, dv_ref):
        j = pl.program_id(1)
        qb = q_ref[0].astype(BF16)
        kb = k_ref[0].astype(BF16)
        do = do_ref[0]
        do_b = do.astype(BF16)
        p = jnp.exp(_dot(qb, kb, 1, 1) * scale - lse_ref[0])
        dp = _dot(do_b, v_ref[0].astype(BF16), 1, 1)
        delta = jnp.sum(do * o_ref[0], axis=-1, keepdims=True)
        ds = (p * (dp - delta) * scale).astype(BF16)
        dq_ref[0] = _dot(ds, kb, 1, 0)
        dk_c = _dot(ds, qb, 0, 0)
        dv_c = _dot(p.astype(BF16), do_b, 0, 0)

        @pl.when(j == 0)
        def _():
            dk_ref[0] = dk_c
            dv_ref[0] = dv_c

        @pl.when(j != 0)
        def _():
            dk_ref[0] += dk_c
            dv_ref[0] += dv_c

    qblk = pl.BlockSpec((1, tq, dq), lambda i, j: (i, j, 0))
    kblk = pl.BlockSpec((1, s, dq), lambda i, j: (i, 0, 0))
    vblk = pl.BlockSpec((1, s, dv), lambda i, j: (i, 0, 0))
    oblk = pl.BlockSpec((1, tq, dv), lambda i, j: (i, j, 0))
    lblk = pl.BlockSpec((1, tq, 1), lambda i, j: (i, j, 0))
    return pl.pallas_call(
        body, name="attn_bwd", grid=(h, s // tq),
        in_specs=[qblk, kblk, vblk, oblk, lblk, oblk], out_specs=[qblk, kblk, vblk],
        out_shape=[_sds(q.shape), _sds(k.shape), _sds(v.shape)],
        compiler_params=_params(("parallel", "arbitrary")),
    )(q, k, v, o, lse, do)


@jax.custom_vjp
def attn(q, k, v):
    return _attn_f(q, k, v)[0]


def _attn_f(q, k, v):
    qb, kb, vb = q.astype(BF16), k.astype(BF16), v.astype(BF16)
    o, lse = _attn_fwd(qb, kb, vb)
    return o, (qb, kb, vb, o, lse)


def _attn_b(res, do):
    qb, kb, vb, o, lse = res
    return tuple(_attn_bwd(qb, kb, vb, o, lse, do))


attn.defvjp(_attn_f, _attn_b)


@jax.custom_vjp
def split_proj(proj):
    out, at = [], 0
    for _, _, _, wp in PROJ_SEGS:
        out.append(proj[:, at:at + wp])
        at += wp
    return tuple(out)


def _split_f(proj):
    return split_proj(proj), None


def _split_b(_, gs):
    return (jnp.concatenate(gs, axis=1),)


split_proj.defvjp(_split_f, _split_b)


def _tile2d(rows, width, limit=BLOCK_BYTES):
    fits = [t for t in range(16, rows + 1, 16) if rows % t == 0 and t * width * 4 <= limit]
    if fits and (fits[-1] >= 64 or fits[-1] == rows):
        return fits[-1], width
    if rows * width * 4 <= limit:
        return rows, width
    cols = [t for t in range(128, width + 1, 128) if width % t == 0 and rows * t * 4 <= limit]
    return (rows, cols[-1]) if cols else (rows, width)


def _add_pair(stacked, theirs, c_idx):
    g, r, w = theirs.shape
    tr, tc = _tile2d(r, w)

    def body(c_ref, a_ref, b_ref, o_ref):
        o_ref[0] = (a_ref[0, 0].astype(F32) + b_ref[0].astype(F32)).astype(BF16)

    blk = pl.BlockSpec((1, tr, tc), lambda k, i, j, c: (k, i, j))
    spec = pltpu.PrefetchScalarGridSpec(
        num_scalar_prefetch=1, grid=(g, r // tr, w // tc),
        in_specs=[pl.BlockSpec((1, 1, tr, tc), lambda k, i, j, c: (c[0], k, i, j)), blk], out_specs=blk)
    return pl.pallas_call(body, name="add_pair", grid_spec=spec, out_shape=_sds(theirs.shape, BF16),
                          compiler_params=_params(("parallel", "parallel", "parallel")))(c_idx, stacked, theirs)


def _add_chips(pair, landed, chip_idx):
    _, r, w = pair.shape
    tr, tc = _tile2d(r, w)

    def body(c_ref, p_ref, l0_ref, l1_ref, l2_ref, o_ref):
        o_ref[...] = ((p_ref[0].astype(F32) + l0_ref[0].astype(F32)) + l1_ref[0].astype(F32)) + l2_ref[0].astype(F32)

    specs = [pl.BlockSpec((1, tr, tc), lambda i, j, c: (c[0], i, j))]
    specs += [pl.BlockSpec((1, tr, tc), functools.partial(lambda i, j, c, k: (k, i, j), k=k)) for k in range(N_CHIPS - 1)]
    spec = pltpu.PrefetchScalarGridSpec(num_scalar_prefetch=1, grid=(r // tr, w // tc), in_specs=specs,
                                        out_specs=pl.BlockSpec((tr, tc), lambda i, j, c: (i, j)))
    return pl.pallas_call(body, name="add_chips", grid_spec=spec, out_shape=_sds((r, w)),
                          compiler_params=_params(("parallel", "parallel")))(chip_idx, pair, landed, landed, landed)


def _sum_devices(g):
    n = g.shape[2]

    def body(g_ref, o_ref):
        t = g_ref[0]
        for j in range(1, N_DEV):
            t = t + g_ref[j]
        o_ref[...] = t

    return pl.pallas_call(body, name="sum_devices", out_shape=_sds((1, n)), compiler_params=_params())(g)


def _adamw(w, g, m, v):
    shp = w.shape
    shp3 = (1, 1, shp[0]) if len(shp) == 1 else (-1,) + tuple(shp[-2:])
    w3, g3, m3, v3 = (t.reshape(shp3) for t in (w, g, m, v))
    c1 = 1.0 - ADAM_B1 ** ADAM_STEP
    c2 = 1.0 - ADAM_B2 ** ADAM_STEP

    def body(w_ref, g_ref, m_ref, v_ref, d_ref, mo_ref, vo_ref):
        gv = g_ref[...]
        mn = ADAM_B1 * m_ref[...] + (1.0 - ADAM_B1) * gv
        vn = ADAM_B2 * v_ref[...] + (1.0 - ADAM_B2) * (gv * gv)
        d_ref[...] = -ADAM_LR * ((mn / c1) / (jnp.sqrt(vn / c2) + ADAM_EPS) + ADAM_WD * w_ref[...])
        mo_ref[...] = mn
        vo_ref[...] = vn

    nl, r, wd = w3.shape
    tr, tc = _tile2d(r, wd, BLOCK_BYTES // 2)
    blk = pl.BlockSpec((1, tr, tc), lambda l, i, j: (l, i, j))
    s3 = _sds(w3.shape)
    d, mn, vn = pl.pallas_call(
        body, name="adamw", grid=(nl, r // tr, wd // tc), in_specs=[blk] * 4, out_specs=[blk] * 3,
        out_shape=[s3, s3, s3], compiler_params=_params(("parallel", "parallel", "parallel")),
    )(w3, g3, m3, v3)
    return d.reshape(shp), mn.reshape(shp), vn.reshape(shp)


ANY = pl.BlockSpec(memory_space=pl.ANY)
PIECE_BYTES = 1 << 20


def _place():
    return lax.axis_index("x"), lax.axis_index("y"), lax.axis_index("c")


def _pieces(shape, itemsize):
    if len(shape) >= 3:
        return [(i,) + p for i in range(shape[0]) for p in _pieces(shape[1:], itemsize)]
    rows = shape[0]
    row_bytes = itemsize
    for dsz in shape[1:]:
        row_bytes *= dsz
    k = 1
    while rows % (2 * k) == 0 and (rows // (2 * k)) % 16 == 0 and (rows // k) * row_bytes > PIECE_BYTES:
        k *= 2
    step = rows // k
    return [(pl.ds(j * step, step),) for j in range(k)]


def _split_start(make, src, dst, pieces):
    for p in pieces:
        make(src.at[p], dst.at[p]).start()
    return make(src, dst)


def _comm_call(body, name, arrs, out_shapes, n_remote, n_local):
    return pl.pallas_call(
        body, name=name, in_specs=[ANY] * len(arrs), out_specs=[ANY] * len(out_shapes), out_shape=out_shapes,
        scratch_shapes=[pltpu.SemaphoreType.DMA((n_remote,)), pltpu.SemaphoreType.DMA((n_remote,)),
                        pltpu.SemaphoreType.DMA((n_local,))],
    )(*arrs)


def all_gather8(arrs, name):
    n = len(arrs)
    pieces = [_pieces(a.shape, a.dtype.itemsize) for a in arrs]

    def body(*refs):
        ins, outs = refs[:n], refs[n:2 * n]
        send, recv, _ = refs[2 * n:]
        x, y, c = _place()
        me, sib = (x, y, c), (x, y, 1 - c)
        chips = [(1 - x, y), (x, 1 - y), (1 - x, 1 - y)]

        def slot(p):
            return 4 * p[0] + 2 * p[1] + p[2]

        def maker(t, k, to):
            def make(s, d):
                return pltpu.make_async_remote_copy(src_ref=s, dst_ref=d, send_sem=send.at[7 * t + k],
                                                    recv_sem=recv.at[7 * t + k], device_id=to, device_id_type=MESH)
            return make

        def landing(t, k, block):
            dst = outs[t].at[slot(block)]
            return maker(t, k, me)(dst, dst)

        sent = []
        for t in range(n):
            dst = outs[t].at[slot(me)]
            sent.append(_split_start(maker(t, 0, sib), ins[t], dst, pieces[t]))
            for j, chip in enumerate(chips):
                sent.append(_split_start(maker(t, 1 + j, (*chip, c)), ins[t], dst, pieces[t]))
        for j, chip in enumerate(chips):
            for t in range(n):
                landing(t, 1 + j, (*chip, c)).wait_recv()
                blk = outs[t].at[slot((*chip, c))]
                sent.append(_split_start(maker(t, 4 + j, sib), blk, blk, pieces[t]))
        for t in range(n):
            landing(t, 0, sib).wait_recv()
            for j, chip in enumerate(chips):
                landing(t, 4 + j, (*chip, 1 - c)).wait_recv()
        for cp in sent:
            cp.wait_send()

    outs = [_sds((N_DEV,) + a.shape, a.dtype) for a in arrs]
    got = _comm_call(body, name, arrs, outs, 7 * n, 1)
    x, y, c = _place()
    return [lax.dynamic_update_index_in_dim(g, a, 4 * x + 2 * y + c, 0) for g, a in zip(got, arrs)]


def sibling_send(arrs, name):
    n = len(arrs)
    pieces = [_pieces(a.shape[1:], a.dtype.itemsize) for a in arrs]

    def body(*refs):
        ins, theirs = refs[:n], refs[n:2 * n]
        send, recv, _ = refs[2 * n:]
        x, y, c = _place()
        rem = []
        for t in range(n):
            def make(s, d, t=t):
                return pltpu.make_async_remote_copy(src_ref=s, dst_ref=d, send_sem=send.at[t], recv_sem=recv.at[t],
                                                    device_id=(x, y, 1 - c), device_id_type=MESH)
            rem.append(_split_start(make, ins[t].at[1 - c], theirs[t], pieces[t]))
        for cp in rem:
            cp.wait_recv()
        for cp in rem:
            cp.wait_send()

    outs = [_sds(a.shape[1:], a.dtype) for a in arrs]
    return _comm_call(body, name, arrs, outs, n, 1)


def exchange_chips(arrs, name):
    n = len(arrs)
    pieces = [_pieces(a.shape[1:], a.dtype.itemsize) for a in arrs]

    def body(*refs):
        ins, outs = refs[:n], refs[n:2 * n]
        send, recv, _ = refs[2 * n:]
        x, y, c = _place()
        peers = [(1 - x, y), (x, 1 - y), (1 - x, 1 - y)]
        rem = []
        for t in range(n):
            for j, (px, py) in enumerate(peers):
                def make(s, d, t=t, j=j, px=px, py=py):
                    return pltpu.make_async_remote_copy(
                        src_ref=s, dst_ref=d, send_sem=send.at[3 * t + j], recv_sem=recv.at[3 * t + j],
                        device_id=(px, py, c), device_id_type=MESH)
                rem.append(_split_start(make, ins[t].at[2 * px + py], outs[t].at[j], pieces[t]))
        for cp in rem:
            cp.wait_recv()
        for cp in rem:
            cp.wait_send()

    outs = [_sds((N_CHIPS - 1,) + a.shape[1:], a.dtype) for a in arrs]
    return _comm_call(body, name, arrs, outs, 3 * n, 1)


def sibling_swap(arrs, name):
    n = len(arrs)
    pieces = [_pieces(a.shape, a.dtype.itemsize) for a in arrs]

    def body(*refs):
        ins, outs = refs[:n], refs[n:2 * n]
        send, recv, _ = refs[2 * n:]
        x, y, c = _place()
        rem = []
        for t in range(n):
            def make(s, d, t=t):
                return pltpu.make_async_remote_copy(src_ref=s, dst_ref=d, send_sem=send.at[t], recv_sem=recv.at[t],
                                                    device_id=(x, y, 1 - c), device_id_type=MESH)
            rem.append(_split_start(make, ins[t], outs[t], pieces[t]))
        for cp in rem:
            cp.wait_recv()
        for cp in rem:
            cp.wait_send()

    outs = [_sds(a.shape, a.dtype) for a in arrs]
    return _comm_call(body, name, arrs, outs, n, 1)


@jax.custom_vjp
def _build_w_in(w4):
    full = w4.reshape(-1, w4.shape[-1])
    parts = []
    for _, start, width, wp in PROJ_SEGS:
        if width:
            parts.append(full[start:start + width])
        if wp > width:
            parts.append(jnp.zeros((wp - width, full.shape[1]), full.dtype))
    return jnp.concatenate(parts, axis=0)


def _build_w_in_f(w4):
    return _build_w_in(w4), None


def _build_w_in_b(_, g):
    parts, at = [], 0
    for _, _, width, wp in PROJ_SEGS:
        if width:
            parts.append(g[at:at + width])
        at += wp
    return (jnp.concatenate(parts, axis=0).reshape(N_CHIPS, -1, g.shape[1]),)


_build_w_in.defvjp(_build_w_in_f, _build_w_in_b)


def _permute_w_uq(w):
    w3 = w.reshape(w.shape[0], MLA_HEADS, MLA_NOPE + MLA_ROPE)
    hr = MLA_ROPE // 2
    return jnp.concatenate([
        w3[:, :, :MLA_NOPE].reshape(w.shape[0], -1),
        w3[:, :, MLA_NOPE:MLA_NOPE + hr].reshape(w.shape[0], -1),
        w3[:, :, MLA_NOPE + hr:].reshape(w.shape[0], -1)], axis=1)


def _heads(t, d):
    return t.reshape(t.shape[0], -1, d).transpose(1, 0, 2)


def _unheads(t):
    return t.transpose(1, 0, 2).reshape(t.shape[1], -1)


def _layer(xh, mod, big, small, rope_q, rope_k):
    s = xh.shape[0]
    d = D_MODEL
    shift, scale, gate = mod[None, 0:d], mod[None, d:2 * d], mod[None, 2 * d:3 * d]
    w_al = _build_w_in(big["w_in"])
    proj = mod_mm(xh, small["norm_g"][None], scale, shift, w_al)
    gq, gk, gv, glr, mq, mkv, mkr, cb, cc, cx, _, z = split_proj(proj)

    rk = GLA_RANK
    hk = GLA_HEADS * GLA_DK
    wg = jnp.zeros((128, 2 * hk), F32)
    wg = wg.at[0:rk, 0:hk].set(small["gla_wg_f"]).at[rk:2 * rk, hk:].set(small["gla_wg_b"])
    bg = jnp.concatenate([small["gla_bg_f"], small["gla_bg_b"]])[None]
    la = gate_act(mm(glr, wg), bg)
    o_gla = gla(_heads(gq, GLA_DK), _heads(gk, GLA_DK), _heads(gv, GLA_DV),
                _heads(la[:, :hk], GLA_DK), _heads(la[:, hk:], GLA_DK))
    o_gla = rmsnorm(o_gla.reshape(GLA_HEADS * s, GLA_DV), small["gla_norm_g"][None])
    o_gla = _unheads(o_gla.reshape(GLA_HEADS, s, GLA_DV))

    nh = MLA_HEADS * MLA_NOPE
    hr = MLA_HEADS * MLA_ROPE // 2
    cq = rmsnorm(mq, small["mla_q_norm_g"][None])
    qm = mm(cq, _permute_w_uq(jnp.concatenate([big["w_uq"][j] for j in range(N_CHIPS)], axis=1)))
    qr = qm[:, nh:]
    qr = fma(qr, rope_q[0], jnp.concatenate([qr[:, hr:], qr[:, :hr]], axis=1), rope_q[1])
    ckv = rmsnorm(mkv, small["mla_kv_norm_g"][None])
    kv = mm(ckv, jnp.concatenate([big["w_ukv"][j] for j in range(N_CHIPS)], axis=1))
    kr = mkr[:, :MLA_ROPE]
    kr = fma(kr, rope_k[0], jnp.concatenate([kr[:, MLA_ROPE // 2:], kr[:, :MLA_ROPE // 2]], axis=1), rope_k[1])
    q3 = jnp.concatenate([qm[:, :nh].reshape(s, MLA_HEADS, MLA_NOPE), qr[:, :hr].reshape(s, MLA_HEADS, -1),
                          qr[:, hr:].reshape(s, MLA_HEADS, -1)], axis=-1).transpose(1, 0, 2)
    kv3 = kv.reshape(s, MLA_HEADS, MLA_NOPE + MLA_DV)
    k3 = jnp.concatenate([kv3[:, :, :MLA_NOPE], jnp.broadcast_to(kr[:, None, :], (s, MLA_HEADS, MLA_ROPE))],
                         axis=-1).transpose(1, 0, 2)
    v3 = kv3[:, :, MLA_NOPE:].transpose(1, 0, 2)
    o_mla = rmsnorm(_unheads(attn(q3, k3, v3)), small["mla_out_g"][None])

    cw = jnp.concatenate([small["conv_w"], jnp.zeros((5, CONV_CH), F32)], axis=0)
    o_conv = rmsnorm(conv_op(cb, cc, cx, cw), small["conv_out_g"][None])

    o = jnp.concatenate([o_gla, o_mla, o_conv], axis=1)
    w_out = big["w_out"].reshape(d, d)
    return out_block(o, z, w_out, xh, gate)


def _loss_fn(xh, mods, bigs, smalls, final_g, target, rope_q, rope_k):
    h = xh
    for l in range(DEPTH):
        h = _layer(h, mods[l], bigs[l], smalls[l], rope_q, rope_k)
    return loss_op(rmsnorm(h, final_g[None]), target)[0, 0]


SMALL_REPL = ("norm_g", "gla_bg_f", "gla_bg_b", "gla_norm_g", "mla_q_norm_g", "mla_kv_norm_g", "mla_out_g",
              "conv_out_g")
SMALL_SHARDED = ("gla_wg_f", "gla_wg_b", "conv_w")
BIG = ("w_in", "w_out", "w_uq", "w_ukv")


def kernel(x, c, positions, ada_w, ada_b, norm_g, w_in, gla_wg_f, gla_bg_f, gla_wg_b, gla_bg_b, gla_norm_g, mla_q_norm_g, mla_kv_norm_g, mla_w_uq, mla_w_ukv, mla_out_g, conv_w, conv_out_g, w_out, final_g, loss_target, m_ada_w, m_ada_b, m_norm_g, m_w_in, m_gla_wg_f, m_gla_bg_f, m_gla_wg_b, m_gla_bg_b, m_gla_norm_g, m_mla_q_norm_g, m_mla_kv_norm_g, m_mla_w_uq, m_mla_w_ukv, m_mla_out_g, m_conv_w, m_conv_out_g, m_w_out, m_final_g, v_ada_w, v_ada_b, v_norm_g, v_w_in, v_gla_wg_f, v_gla_bg_f, v_gla_wg_b, v_gla_bg_b, v_gla_norm_g, v_mla_q_norm_g, v_mla_kv_norm_g, v_mla_w_uq, v_mla_w_ukv, v_mla_out_g, v_conv_w, v_conv_out_g, v_w_out, v_final_g):
    xi, yi, ci = _place()
    chip = 2 * xi + yi
    dev = 2 * chip + ci
    s = x.shape[1]
    d = D_MODEL
    weights = dict(ada_w=ada_w, ada_b=ada_b, norm_g=norm_g, w_in=w_in, gla_wg_f=gla_wg_f, gla_bg_f=gla_bg_f,
                   gla_wg_b=gla_wg_b, gla_bg_b=gla_bg_b, gla_norm_g=gla_norm_g, mla_q_norm_g=mla_q_norm_g,
                   mla_kv_norm_g=mla_kv_norm_g, mla_w_uq=mla_w_uq, mla_w_ukv=mla_w_ukv, mla_out_g=mla_out_g,
                   conv_w=conv_w, conv_out_g=conv_out_g, w_out=w_out, final_g=final_g)
    m_in = dict(ada_w=m_ada_w, ada_b=m_ada_b, norm_g=m_norm_g, w_in=m_w_in, gla_wg_f=m_gla_wg_f, gla_bg_f=m_gla_bg_f,
                gla_wg_b=m_gla_wg_b, gla_bg_b=m_gla_bg_b, gla_norm_g=m_gla_norm_g, mla_q_norm_g=m_mla_q_norm_g,
                mla_kv_norm_g=m_mla_kv_norm_g, mla_w_uq=m_mla_w_uq, mla_w_ukv=m_mla_w_ukv, mla_out_g=m_mla_out_g,
                conv_w=m_conv_w, conv_out_g=m_conv_out_g, w_out=m_w_out, final_g=m_final_g)
    v_in = dict(ada_w=v_ada_w, ada_b=v_ada_b, norm_g=v_norm_g, w_in=v_w_in, gla_wg_f=v_gla_wg_f, gla_bg_f=v_gla_bg_f,
                gla_wg_b=v_gla_wg_b, gla_bg_b=v_gla_bg_b, gla_norm_g=v_gla_norm_g, mla_q_norm_g=v_mla_q_norm_g,
                mla_kv_norm_g=v_mla_kv_norm_g, mla_w_uq=v_mla_w_uq, mla_w_ukv=v_mla_w_ukv, mla_out_g=v_mla_out_g,
                conv_w=v_conv_w, conv_out_g=v_conv_out_g, w_out=v_w_out, final_g=v_final_g)

    def mine_bf16(w):
        return lax.dynamic_index_in_dim(w, ci, 0, keepdims=False).astype(BF16)

    g_c, g_in, g_out, g_uq, g_ukv, g_wgf, g_wgb, g_cw = all_gather8(
        [c, mine_bf16(jnp.swapaxes(w_in, 1, 2)), mine_bf16(w_out), mine_bf16(mla_w_uq), mine_bf16(mla_w_ukv),
         gla_wg_f, gla_wg_b, conv_w], "gather_weights")

    def by_layer(g):
        g4 = g.reshape((N_CHIPS, 2) + g.shape[1:])
        return [g4[:, l] for l in range(DEPTH)]

    bigs = [dict(w_in=a, w_out=b, w_uq=u, w_ukv=k)
            for a, b, u, k in zip(by_layer(g_in), by_layer(g_out), by_layer(g_uq), by_layer(g_ukv))]

    def unshard_cols(g):
        g4 = g[0::2]
        return g4.transpose(1, 2, 0, 3).reshape(g4.shape[1], g4.shape[2], -1)

    small_full = dict(gla_wg_f=unshard_cols(g_wgf), gla_wg_b=unshard_cols(g_wgb), conv_w=unshard_cols(g_cw))
    for nme in SMALL_REPL:
        small_full[nme] = weights[nme]
    smalls = [{nme: small_full[nme][l] for nme in SMALL_REPL + SMALL_SHARDED} for l in range(DEPTH)]

    c_act = _silu_rows(g_c[:, 0, :])
    c_act16 = jnp.concatenate([c_act, jnp.zeros_like(c_act)], axis=0)
    n_ada = ada_w.shape[2]
    parts = []
    for l in range(DEPTH):
        bias = lax.dynamic_slice_in_dim(ada_b[l], chip * n_ada, n_ada)[None]
        parts.append(_mm(c_act16, ada_w[l], bias=bias, name="ada_fwd"))
    g_mod, = all_gather8([jnp.stack(parts)], "gather_mod")
    mod_mine = lax.dynamic_index_in_dim(g_mod[0::2], dev, 2, keepdims=False)
    mods = mod_mine.transpose(1, 0, 2).reshape(DEPTH, 3 * d)

    inv_freq = ROPE_THETA ** (-jnp.arange(0, MLA_ROPE, 2, dtype=F32) / MLA_ROPE)
    ang = positions[0].astype(F32)[:, None] * inv_freq
    cos, sin = jnp.cos(ang), jnp.sin(ang)
    cos_h, sin_h = jnp.tile(cos, (1, MLA_HEADS)), jnp.tile(sin, (1, MLA_HEADS))
    rope_q = (jnp.concatenate([cos_h, cos_h], axis=1), jnp.concatenate([-sin_h, sin_h], axis=1))
    rope_k = (jnp.concatenate([cos, cos], axis=1), jnp.concatenate([-sin, sin], axis=1))

    loss_dev, (dx, dmods, dbigs, dsmalls, dfinal) = jax.value_and_grad(_loss_fn, argnums=(0, 1, 2, 3, 4))(
        x[0], mods, bigs, smalls, final_g, loss_target[0], rope_q, rope_k)
    loss = lax.psum(loss_dev, ("x", "y", "c"))

    pieces = [dmods.reshape(-1), dfinal]
    for nme in SMALL_REPL + SMALL_SHARDED:
        pieces.append(jnp.stack([dsmalls[l][nme] for l in range(DEPTH)]).reshape(-1))
    sizes = [p.shape[0] for p in pieces]
    flat = jnp.concatenate(pieces)
    padn = (-flat.shape[0]) % 128
    flat = jnp.concatenate([flat, jnp.zeros((padn,), F32)])[None]
    g_small, = all_gather8([flat], "gather_small_grads")
    total = _sum_devices(g_small)[0]
    offs, at = [], 0
    for n_el in sizes:
        offs.append(at)
        at += n_el

    def piece(i, shape):
        return total[offs[i]:offs[i] + sizes[i]].reshape(shape)

    grads = {"ada_b": piece(0, (DEPTH, 3 * d)), "final_g": piece(1, (d,))}
    for i, nme in enumerate(SMALL_REPL + SMALL_SHARDED):
        full = piece(2 + i, small_full[nme].shape)
        if nme in SMALL_SHARDED:
            ncol = weights[nme].shape[2]
            full = lax.dynamic_slice_in_dim(full, chip * ncol, ncol, axis=2)
        grads[nme] = full

    dmod_all = g_small[:, 0, :DEPTH * 3 * d].reshape(N_DEV, DEPTH, 3 * d)
    dmod_cols = lax.dynamic_slice_in_dim(dmod_all, chip * n_ada, n_ada, axis=2)
    g_ada = []
    for l in range(DEPTH):
        dm16 = jnp.concatenate([dmod_cols[:, l], jnp.zeros((N_DEV, n_ada), F32)], axis=0)
        g_ada.append(_mm(c_act16, dm16, ta=True, name="ada_bwd"))
    grads["ada_w"] = jnp.stack(g_ada)

    stacked = [jnp.stack([dbigs[l][nme] for l in range(DEPTH)]) for nme in BIG]
    c_idx = jnp.reshape(ci, (1,)).astype(jnp.int32)
    chip_idx = jnp.reshape(chip, (1,)).astype(jnp.int32)
    theirs = sibling_send(stacked, "reduce_sibling")
    pair = [_add_pair(a, b, c_idx) for a, b in zip(stacked, theirs)]
    landed = exchange_chips(pair, "reduce_chips")
    reduced = [_add_chips(p, q, chip_idx) for p, q in zip(pair, landed)]
    others = sibling_swap(reduced, "share_sibling")
    for nme, own, other in zip(("w_in", "w_out", "mla_w_uq", "mla_w_ukv"), reduced, others):
        grads[nme] = jnp.stack([jnp.where(ci == l, own, other) for l in range(DEPTH)])

    order = list(weights)
    delta, new_m, new_v = {}, {}, {}
    for nme in order:
        if nme == "w_in":
            w_t, m_t, v_t = (jnp.swapaxes(t, 1, 2) for t in (w_in, m_w_in, v_w_in))
            res = _adamw(w_t, grads[nme], m_t, v_t)
            delta[nme], new_m[nme], new_v[nme] = (jnp.swapaxes(t, 1, 2) for t in res)
            grads[nme] = jnp.swapaxes(grads[nme], 1, 2)
            continue
        delta[nme], new_m[nme], new_v[nme] = _adamw(weights[nme], grads[nme], m_in[nme], v_in[nme])
    return (loss, dx[None], *[grads[n_] for n_ in order], *[delta[n_] for n_ in order],
            *[new_m[n_] for n_ in order], *[new_v[n_] for n_ in order])
```

```python
import functools

import jax
import jax.numpy as jnp
from jax import lax
from jax.experimental import pallas as pl
from jax.experimental.pallas import tpu as pltpu

F32 = jnp.float32
BF16 = jnp.bfloat16
MESH = pl.DeviceIdType.MESH
HIGHEST = lax.Precision.HIGHEST

DEPTH = 2
D_MODEL = 2048
GLA_HEADS = 6
GLA_DK = 64
GLA_DV = 128
GLA_RANK = 16
GLA_TEMP = 16.0
GLA_CHUNK = 64
GLA_W = GLA_HEADS * GLA_DV
MLA_HEADS = 6
MLA_QL = 384
MLA_KVL = 256
MLA_NOPE = 128
MLA_ROPE = 64
MLA_DV = 128
MLA_W = MLA_HEADS * MLA_DV
CONV_CH = D_MODEL - GLA_W - MLA_W
ROPE_THETA = 10000.0
EPS = 1e-6
IN_DIM = 5856
N_CHIPS = 4
N_DEV = 8

ADAM_LR = 0.001
ADAM_B1 = 0.9
ADAM_B2 = 0.999
ADAM_EPS = 1e-08
ADAM_WD = 0.01
ADAM_STEP = 10

PROJ_SEGS = (
    ("gq", 0, 384, 384), ("gk", 384, 384, 384), ("gv", 768, 768, 768), ("glr", 1536, 32, 128),
    ("mq", 1568, 384, 384), ("mkv", 1952, 256, 256), ("mkr", 2208, 64, 128),
    ("cb", 2272, 512, 512), ("cc", 2784, 512, 512), ("cx", 3296, 512, 512),
    ("pad", 3808, 0, 128), ("z", 3808, 2048, 2048),
)
PROJ_AL = sum(s[3] for s in PROJ_SEGS)

VMEM_LIMIT = 48 * 1024 * 1024
BLOCK_BYTES = 2 * 1024 * 1024


def _params(sem=None):
    return pltpu.CompilerParams(dimension_semantics=sem, vmem_limit_bytes=VMEM_LIMIT)


def _dot(a, b, ca, cb, precision=None):
    return lax.dot_general(a, b, (((ca,), (cb,)), ((), ())), preferred_element_type=F32, precision=precision)


def _tile(dim, prefs):
    for t in prefs:
        if dim % t == 0:
            return t
    return dim


def _pick_rows(rows, width, itemsize=4):
    for t in (2048, 1024, 512, 256, 128, 64, 32, 16, 8):
        if rows % t == 0 and t * width * itemsize <= BLOCK_BYTES:
            return t
    return rows


def _mm(a, b, *, ta=False, tb=False, bias=None, out_dtype=F32, name="mm"):
    if ta:
        K, M = a.shape
    else:
        M, K = a.shape
    if tb:
        N, Kb = b.shape
    else:
        Kb, N = b.shape
    assert K == Kb, (a.shape, b.shape, ta, tb)
    tm = _tile(M, (512, 256, 128))
    tn = _tile(N, (1024, 512, 384, 256, 128))
    tk = _tile(K, (2048, 1024, 512, 256, 128))
    nk = K // tk
    has_bias = bias is not None

    def body(*refs):
        a_ref, b_ref = refs[0], refs[1]
        bias_ref = refs[2] if has_bias else None
        o_ref = refs[3 if has_bias else 2]
        part = _dot(a_ref[...].astype(BF16), b_ref[...].astype(BF16), 0 if ta else 1, 1 if tb else 0)

        def finish(r):
            if has_bias:
                r = r + bias_ref[...]
            o_ref[...] = r.astype(out_dtype)

        if nk == 1:
            finish(part)
            return
        acc_ref = refs[-1]
        k = pl.program_id(2)

        @pl.when(k == 0)
        def _():
            acc_ref[...] = part

        @pl.when(k != 0)
        def _():
            acc_ref[...] += part

        @pl.when(k == nk - 1)
        def _():
            finish(acc_ref[...])

    a_spec = pl.BlockSpec((tk, tm), lambda i, j, k: (k, i)) if ta else pl.BlockSpec((tm, tk), lambda i, j, k: (i, k))
    b_spec = pl.BlockSpec((tn, tk), lambda i, j, k: (j, k)) if tb else pl.BlockSpec((tk, tn), lambda i, j, k: (k, j))
    in_specs = [a_spec, b_spec]
    args = [a, b]
    if has_bias:
        in_specs.append(pl.BlockSpec((1, tn), lambda i, j, k: (0, j)))
        args.append(bias)
    return pl.pallas_call(
        body, name=name, grid=(M // tm, N // tn, nk),
        in_specs=in_specs, out_specs=pl.BlockSpec((tm, tn), lambda i, j, k: (i, j)),
        out_shape=jax.ShapeDtypeStruct((M, N), out_dtype),
        scratch_shapes=[pltpu.VMEM((tm, tn), F32)] if nk > 1 else [],
        compiler_params=_params(("parallel", "parallel", "arbitrary")),
    )(*args)


@jax.custom_vjp
def mm(a, b):
    return _mm(a, b, name="mm_fwd")


def _mm_f(a, b):
    return _mm(a, b, name="mm_fwd"), (a, b)


def _mm_b(res, g):
    a, b = res
    return _mm(g, b, tb=True, out_dtype=a.dtype, name="mm_da"), _mm(a, g, ta=True, out_dtype=b.dtype, name="mm_db")


mm.defvjp(_mm_f, _mm_b)


@jax.custom_vjp
def mm16(a, b):
    return _mm(a, b, out_dtype=BF16, name="mm16_fwd")


def _mm16_f(a, b):
    return mm16(a, b), (a, b)


mm16.defvjp(_mm16_f, _mm_b)


def _rows(body, name, tiled, full, tiled_out, acc_out, tr=None):
    rows = tiled[0].shape[0]
    if tr is None:
        width = max([a.shape[1] for a in tiled] + [s.shape[1] for s in tiled_out])
        tr = _pick_rows(rows, width)
    in_specs = [pl.BlockSpec((tr, a.shape[1]), lambda i: (i, 0)) for a in tiled]
    in_specs += [pl.BlockSpec(a.shape, lambda i: (0, 0)) for a in full]
    out_specs = [pl.BlockSpec((tr, s.shape[1]), lambda i: (i, 0)) for s in tiled_out]
    out_specs += [pl.BlockSpec(s.shape, lambda i: (0, 0)) for s in acc_out]

    def wrapped(*refs):
        body(pl.program_id(0), *refs)

    outs = pl.pallas_call(
        wrapped, name=name, grid=(rows // tr,), in_specs=in_specs, out_specs=out_specs,
        out_shape=list(tiled_out) + list(acc_out),
        compiler_params=_params(("arbitrary",)),
    )(*tiled, *full)
    return outs


def _sds(shape, dtype=F32):
    return jax.ShapeDtypeStruct(tuple(shape), dtype)


def _acc(step, ref, val):
    @pl.when(step == 0)
    def _():
        ref[...] = val

    @pl.when(step != 0)
    def _():
        ref[...] += val


def _colsum(v):
    return jnp.sum(v, axis=0, keepdims=True)


def _rstd(x):
    return lax.rsqrt(jnp.mean(x * x, axis=-1, keepdims=True) + EPS)


def _norm_grid(x, g):
    rows, w = x.shape[0], g.shape[1]
    tr = _pick_rows(rows, w)
    blk = pl.BlockSpec((tr, w), lambda i, j: (i, j))
    gblk = pl.BlockSpec((1, w), lambda i, j: (0, 0))
    return (rows // tr, x.shape[1] // w), blk, gblk


@jax.custom_vjp
def rmsnorm(x, g):
    def body(x_ref, g_ref, o_ref):
        x = x_ref[...]
        o_ref[...] = x * _rstd(x) * g_ref[...]

    grid, blk, gblk = _norm_grid(x, g)
    return pl.pallas_call(body, name="rmsnorm_fwd", grid=grid, in_specs=[blk, gblk], out_specs=blk,
                          out_shape=_sds(x.shape), compiler_params=_params(("parallel", "parallel")))(x, g)


def _rmsnorm_f(x, g):
    return rmsnorm(x, g), (x, g)


def _rmsnorm_b(res, dy):
    x, g = res

    def body(x_ref, dy_ref, g_ref, dx_ref, dg_ref):
        x = x_ref[...]
        dy = dy_ref[...]
        r = _rstd(x)
        xh = x * r
        dxh = dy * g_ref[...]
        dx_ref[...] = r * (dxh - xh * jnp.mean(dxh * xh, axis=-1, keepdims=True))
        first = jnp.logical_and(pl.program_id(0) == 0, pl.program_id(1) == 0)
        _acc(jnp.where(first, 0, 1), dg_ref, _colsum(dy * xh))

    grid, blk, gblk = _norm_grid(x, g)
    dx, dg = pl.pallas_call(body, name="rmsnorm_bwd", grid=grid, in_specs=[blk, blk, gblk], out_specs=[blk, gblk],
                            out_shape=[_sds(x.shape), _sds(g.shape)],
                            compiler_params=_params(("arbitrary", "arbitrary")))(x, dy, g)
    return dx, dg


rmsnorm.defvjp(_rmsnorm_f, _rmsnorm_b)


def _modulate(x, g, scale, shift):
    def body(i, x_ref, g_ref, sc_ref, sh_ref, o_ref):
        x = x_ref[...]
        xn = x * _rstd(x) * g_ref[...]
        o_ref[...] = (xn * (1.0 + sc_ref[...]) + sh_ref[...]).astype(BF16)
    return _rows(body, "modulate_fwd", [x], [g, scale, shift], [_sds(x.shape, BF16)], [])[0]


def _modulate_bwd(x, g, scale, shift, dh):
    def body(i, x_ref, dh_ref, g_ref, sc_ref, dx_ref, dg_ref, dsc_ref, dsh_ref):
        x = x_ref[...]
        dh = dh_ref[...]
        gv = g_ref[...]
        r = _rstd(x)
        xh = x * r
        dxn = dh * (1.0 + sc_ref[...])
        dxh = dxn * gv
        dx_ref[...] = r * (dxh - xh * jnp.mean(dxh * xh, axis=-1, keepdims=True))
        _acc(i, dg_ref, _colsum(dxn * xh))
        _acc(i, dsc_ref, _colsum(dh * (xh * gv)))
        _acc(i, dsh_ref, _colsum(dh))

    v = _sds(g.shape)
    return _rows(body, "modulate_bwd", [x, dh], [g, scale], [_sds(x.shape)], [v, v, v])


@jax.custom_vjp
def mod_mm(x, g, scale, shift, wt):
    return _mm(_modulate(x, g, scale, shift), wt, tb=True, name="mm_in")


def _mod_mm_f(x, g, scale, shift, wt):
    h = _modulate(x, g, scale, shift)
    return _mm(h, wt, tb=True, name="mm_in"), (x, g, scale, shift, wt, h)


def _mod_mm_b(res, dproj):
    x, g, scale, shift, wt, h = res
    dproj = dproj.astype(BF16)
    dh = _mm(dproj, wt, name="mm_in_dh")
    dwt = _mm(dproj, h, ta=True, out_dtype=wt.dtype, name="mm_in_dw")
    dx, dg, dsc, dsh = _modulate_bwd(x, g, scale, shift, dh)
    return dx, dg, dsc, dsh, dwt


mod_mm.defvjp(_mod_mm_f, _mod_mm_b)


def _sigmoid(z):
    return 1.0 / (1.0 + jnp.exp(-z))


def _gate_mul(o, z):
    def body(i, o_ref, z_ref, y_ref):
        z = z_ref[...]
        y_ref[...] = (o_ref[...] * (z * _sigmoid(z))).astype(BF16)
    return _rows(body, "gate_mul_fwd", [o, z], [], [_sds(o.shape, BF16)], [])[0]


def _gate_mul_bwd(o, z, dy):
    def body(i, o_ref, z_ref, dy_ref, do_ref, dz_ref):
        z = z_ref[...]
        dy = dy_ref[...]
        s = _sigmoid(z)
        do_ref[...] = dy * (z * s)
        dz_ref[...] = dy * o_ref[...] * (s * (1.0 + z * (1.0 - s)))
    return _rows(body, "gate_mul_bwd", [o, z, dy], [], [_sds(o.shape), _sds(o.shape)], [])


def _residual(x, u, gate):
    def body(i, x_ref, u_ref, g_ref, o_ref):
        o_ref[...] = x_ref[...] + g_ref[...] * u_ref[...]
    return _rows(body, "residual_fwd", [x, u], [gate], [_sds(x.shape)], [])[0]


def _residual_bwd(d, u, gate):
    def body(i, d_ref, u_ref, g_ref, du_ref, dg_ref):
        d = d_ref[...]
        du_ref[...] = (g_ref[...] * d).astype(BF16)
        _acc(i, dg_ref, _colsum(d * u_ref[...]))

    return _rows(body, "residual_bwd", [d, u], [gate], [_sds(u.shape, BF16)], [_sds(gate.shape)])


@jax.custom_vjp
def out_block(o, z, w, x, gate):
    return _residual(x, _mm(_gate_mul(o, z), w, name="mm_out"), gate)


def _out_block_f(o, z, w, x, gate):
    y = _gate_mul(o, z)
    u = _mm(y, w, name="mm_out")
    return _residual(x, u, gate), (o, z, w, y, u, gate)


def _out_block_b(res, d):
    o, z, w, y, u, gate = res
    du, dgate = _residual_bwd(d, u, gate)
    dy = _mm(du, w, tb=True, name="mm_out_dy")
    dw = _mm(y, du, ta=True, out_dtype=w.dtype, name="mm_out_dw")
    do, dz = _gate_mul_bwd(o, z, dy)
    return do, dz, dw, d, dgate


out_block.defvjp(_out_block_f, _out_block_b)


@jax.custom_vjp
def gate_act(u, b):
    def body(i, u_ref, b_ref, o_ref):
        t = u_ref[...] + b_ref[...]
        o_ref[...] = (jnp.minimum(t, 0.0) - jnp.log(1.0 + jnp.exp(-jnp.abs(t)))) / GLA_TEMP
    return _rows(body, "gate_act_fwd", [u], [b], [_sds(u.shape)], [])[0]


def _gate_act_f(u, b):
    return gate_act(u, b), (u, b)


def _gate_act_b(res, d):
    u, b = res

    def body(i, u_ref, d_ref, b_ref, du_ref, db_ref):
        t = u_ref[...] + b_ref[...]
        du = d_ref[...] * _sigmoid(-t) / GLA_TEMP
        du_ref[...] = du
        _acc(i, db_ref, _colsum(du))

    du, db = _rows(body, "gate_act_bwd", [u, d], [b], [_sds(u.shape)], [_sds(b.shape)])
    return du, db


gate_act.defvjp(_gate_act_f, _gate_act_b)


@jax.custom_vjp
def fma(a, b, c, d):
    def body(i, a_ref, b_ref, c_ref, d_ref, o_ref):
        o_ref[...] = a_ref[...] * b_ref[...] + c_ref[...] * d_ref[...]
    return _rows(body, "fma_fwd", [a, b, c, d], [], [_sds(a.shape)], [])[0]


def _fma_f(a, b, c, d):
    return fma(a, b, c, d), (b, d)


def _fma_b(res, g):
    b, d = res

    def body(i, g_ref, b_ref, d_ref, da_ref, dc_ref):
        g = g_ref[...]
        da_ref[...] = g * b_ref[...]
        dc_ref[...] = g * d_ref[...]

    da, dc = _rows(body, "fma_bwd", [g, b, d], [], [_sds(g.shape), _sds(g.shape)], [])
    return da, jnp.zeros_like(b), dc, jnp.zeros_like(d)


fma.defvjp(_fma_f, _fma_b)


def _silu_rows(c):
    def body(i, c_ref, o_ref):
        v = c_ref[...]
        o_ref[...] = v * _sigmoid(v)
    return _rows(body, "silu", [c], [], [_sds(c.shape)], [])[0]


@jax.custom_vjp
def loss_op(y, t):
    return _loss_fwd(y, t)[0]


def _loss_fwd(y, t):
    inv = 1.0 / y.shape[1]

    def body(i, y_ref, t_ref, d_ref, l_ref):
        e = y_ref[...] - t_ref[...]
        d_ref[...] = e * inv
        _acc(i, l_ref, jnp.sum(_colsum(e * e), axis=1, keepdims=True) * (0.5 * inv))

    d, l = _rows(body, "loss_fwd", [y, t], [], [_sds(y.shape)], [_sds((1, 1))])
    return l, d


def _loss_f(y, t):
    l, d = _loss_fwd(y, t)
    return l, d


def _loss_b(d, g):
    return d * g, jnp.zeros_like(d)


loss_op.defvjp(_loss_f, _loss_b)


def _conv_terms(cc, cx, rows, n):
    u = cc * cx
    up = jnp.where(rows == 0, 0.0, pltpu.roll(u, 1, 0))
    un = jnp.where(rows == n - 1, 0.0, pltpu.roll(u, n - 1, 0))
    return u, up, un


CONV_COLS = 128


def _conv_specs(s, n_in):
    blk = pl.BlockSpec((s, CONV_COLS), lambda j: (0, j))
    wblk = pl.BlockSpec((8, CONV_COLS), lambda j: (0, j))
    return [blk] * n_in + [wblk], blk, wblk


@jax.custom_vjp
def conv_op(cb, cc, cx, w):
    s, ch = cb.shape

    def body(cb_ref, cc_ref, cx_ref, w_ref, o_ref):
        rows = lax.broadcasted_iota(jnp.int32, (s, CONV_COLS), 0)
        u, up, un = _conv_terms(cc_ref[...], cx_ref[...], rows, s)
        conv = up * w_ref[0:1, :] + u * w_ref[1:2, :] + un * w_ref[2:3, :]
        o_ref[...] = cb_ref[...] * conv

    in_specs, blk, _ = _conv_specs(s, 3)
    return pl.pallas_call(
        body, name="conv_fwd", grid=(ch // CONV_COLS,), in_specs=in_specs, out_specs=blk,
        out_shape=_sds(cb.shape), compiler_params=_params(("parallel",)),
    )(cb, cc, cx, w)


def _conv_f(cb, cc, cx, w):
    return conv_op(cb, cc, cx, w), (cb, cc, cx, w)


def _conv_b(res, d):
    cb, cc, cx, w = res
    s, ch = cb.shape

    def body(cb_ref, cc_ref, cx_ref, d_ref, w_ref, dcb_ref, dcc_ref, dcx_ref, dw_ref):
        rows = lax.broadcasted_iota(jnp.int32, (s, CONV_COLS), 0)
        cc_v = cc_ref[...]
        cx_v = cx_ref[...]
        u, up, un = _conv_terms(cc_v, cx_v, rows, s)
        w0, w1, w2 = w_ref[0:1, :], w_ref[1:2, :], w_ref[2:3, :]
        dv = d_ref[...]
        dcb_ref[...] = dv * (up * w0 + u * w1 + un * w2)
        dconv = dv * cb_ref[...]
        d_next = jnp.where(rows == s - 1, 0.0, pltpu.roll(dconv, s - 1, 0))
        d_prev = jnp.where(rows == 0, 0.0, pltpu.roll(dconv, 1, 0))
        du = w0 * d_next + w1 * dconv + w2 * d_prev
        dcc_ref[...] = du * cx_v
        dcx_ref[...] = du * cc_v
        dw_ref[...] = jnp.zeros_like(dw_ref)
        dw_ref[0:1, :] = _colsum(dconv * up)
        dw_ref[1:2, :] = _colsum(dconv * u)
        dw_ref[2:3, :] = _colsum(dconv * un)

    in_specs, blk, wblk = _conv_specs(s, 4)
    v = _sds(cb.shape)
    return tuple(pl.pallas_call(
        body, name="conv_bwd", grid=(ch // CONV_COLS,), in_specs=in_specs, out_specs=[blk, blk, blk, wblk],
        out_shape=[v, v, v, _sds(w.shape)], compiler_params=_params(("parallel",)),
    )(cb, cc, cx, d, w))


conv_op.defvjp(_conv_f, _conv_b)


def _gla_masks(rev):
    c = GLA_CHUNK
    row = lax.broadcasted_iota(jnp.int32, (c, c), 0)
    col = lax.broadcasted_iota(jnp.int32, (c, c), 1)
    mask = (row < col) if rev else (row >= col)
    return rev, mask


def _chunk_cumsum(g, rev):
    c = g.shape[0]
    row = lax.broadcasted_iota(jnp.int32, g.shape, 0)
    b = g
    s = 1
    while s < c:
        if rev:
            b = b + jnp.where(row < c - s, pltpu.roll(b, c - s, 0), 0.0)
        else:
            b = b + jnp.where(row >= s, pltpu.roll(b, s, 0), 0.0)
        s *= 2
    return b


GLA_UNROLL = 4


def _gla_rows(n):
    return pl.ds(pl.multiple_of(n * GLA_CHUNK, GLA_CHUNK), GLA_CHUNK)


def _gla_scan(s_ref, bt_ref, st_ref, n_chunks, descending):
    st_ref[...] = jnp.zeros_like(st_ref)

    def step(i, carry):
        n = (n_chunks - 1 - i) if descending else i
        own = s_ref[n]
        st = st_ref[...]
        s_ref[n] = st
        st_ref[...] = st * jnp.exp(bt_ref[n]) + own
        return carry

    lax.fori_loop(0, n_chunks, step, 0)


GLA_PAIR = 2


def _gla_specs(s):
    dk, dv = GLA_DK, GLA_DV
    n_pairs = GLA_HEADS // GLA_PAIR
    blk_k = pl.BlockSpec((s, GLA_PAIR * dk), lambda p: (0, p))
    blk_gb = pl.BlockSpec((s, GLA_PAIR * dk), lambda p: (0, n_pairs + p))
    blk_v = pl.BlockSpec((s, GLA_PAIR * dv), lambda p: (0, p))
    return n_pairs, blk_k, blk_gb, blk_v


def _gla_fwd(q, k, v, la):
    s = q.shape[0]
    dk, dv = GLA_DK, GLA_DV
    n_chunks = s // GLA_CHUNK
    scale = GLA_DK ** -0.5

    def body(q_ref, k_ref, v_ref, gf_ref, gb_ref, o_ref, sf_ref, sb_ref, bf_ref, bb_ref, btf_ref, btb_ref, st_ref):
        masks = [_gla_masks(rev) for rev in (False, True)]
        for hh in range(GLA_PAIR):
            kl = slice(hh * dk, (hh + 1) * dk)
            vl = slice(hh * dv, (hh + 1) * dv)
            dirs = ((False, gf_ref, sf_ref, bf_ref, btf_ref), (True, gb_ref, sb_ref, bb_ref, btb_ref))

            def prepare(n, carry, kl=kl, vl=vl, dirs=dirs):
                rows = _gla_rows(n)
                kk = k_ref[rows, kl]
                vb = v_ref[rows, vl].astype(BF16)
                for rev, g_ref, s_ref, b_ref, bt_ref in dirs:
                    g = g_ref[rows, kl]
                    b = _chunk_cumsum(g, rev)
                    bt = _colsum(g)
                    b_ref[rows, :] = b
                    bt_ref[n] = bt
                    s_ref[n] = _dot(vb, (kk * jnp.exp(bt - b)).astype(BF16), 0, 0)
                return carry

            lax.fori_loop(0, n_chunks, prepare, 0, unroll=GLA_UNROLL)
            for rev, _, s_ref, _, bt_ref in dirs:
                _gla_scan(s_ref, bt_ref, st_ref, n_chunks, descending=rev)

            def emit(n, carry, kl=kl, vl=vl, dirs=dirs):
                rows = _gla_rows(n)
                qs = q_ref[rows, kl] * scale
                kk = k_ref[rows, kl]
                vb = v_ref[rows, vl].astype(BF16)
                o = None
                for (rev, _, s_ref, b_ref, _), (_, mask) in zip(dirs, masks):
                    b = b_ref[rows, :]
                    qd = (qs * jnp.exp(b)).astype(BF16)
                    ki = (kk * jnp.exp(-b)).astype(BF16)
                    a = jnp.where(mask, _dot(qd, ki, 1, 1), 0.0).astype(BF16)
                    od = _dot(a, vb, 1, 0) + _dot(qd, s_ref[n].astype(BF16), 1, 1)
                    o = od if o is None else o + od
                o_ref[rows, vl] = o
                return carry

            lax.fori_loop(0, n_chunks, emit, 0, unroll=GLA_UNROLL)

    n_pairs, blk_k, blk_gb, blk_v = _gla_specs(s)
    state = pltpu.VMEM((n_chunks, dv, dk), F32)
    scratch = [state, state, pltpu.VMEM((s, dk), F32), pltpu.VMEM((s, dk), F32), pltpu.VMEM((n_chunks, 1, dk), F32),
               pltpu.VMEM((n_chunks, 1, dk), F32), pltpu.VMEM((dv, dk), F32)]
    return pl.pallas_call(
        body, name="gla_fwd", grid=(n_pairs,), in_specs=[blk_k, blk_k, blk_v, blk_k, blk_gb],
        out_specs=blk_v, out_shape=_sds(v.shape), scratch_shapes=scratch,
        compiler_params=_params(("parallel",)),
    )(q, k, v, la, la)


def _gla_bwd(q, k, v, la, do):
    s = q.shape[0]
    dk, dv = GLA_DK, GLA_DV
    c = GLA_CHUNK
    n_chunks = s // c
    scale = GLA_DK ** -0.5

    def body(q_ref, k_ref, v_ref, gf_ref, gb_ref, do_ref, dq_ref, dk_ref, dv_ref, dgf_ref, dgb_ref,
             sf_ref, sb_ref, bf_ref, bb_ref, btf_ref, btb_ref, dsf_ref, dsb_ref, st_ref):
        masks = [_gla_masks(rev) for rev in (False, True)]
        rowc = lax.broadcasted_iota(jnp.int32, (c, dk), 0)
        for hh in range(GLA_PAIR):
            kl = slice(hh * dk, (hh + 1) * dk)
            vl = slice(hh * dv, (hh + 1) * dv)
            dirs = ((False, gf_ref, sf_ref, bf_ref, btf_ref, dsf_ref, dgf_ref),
                    (True, gb_ref, sb_ref, bb_ref, btb_ref, dsb_ref, dgb_ref))

            def prepare(n, carry, kl=kl, vl=vl, dirs=dirs):
                rows = _gla_rows(n)
                qs = q_ref[rows, kl] * scale
                kk = k_ref[rows, kl]
                vb = v_ref[rows, vl].astype(BF16)
                do_b = do_ref[rows, vl].astype(BF16)
                for rev, g_ref, s_ref, b_ref, bt_ref, ds_ref, _ in dirs:
                    g = g_ref[rows, kl]
                    b = _chunk_cumsum(g, rev)
                    bt = _colsum(g)
                    b_ref[rows, :] = b
                    bt_ref[n] = bt
                    s_ref[n] = _dot(vb, (kk * jnp.exp(bt - b)).astype(BF16), 0, 0)
                    ds_ref[n] = _dot(do_b, (qs * jnp.exp(b)).astype(BF16), 0, 0)
                return carry

            lax.fori_loop(0, n_chunks, prepare, 0, unroll=GLA_UNROLL)
            for rev, _, s_ref, _, bt_ref, ds_ref, _ in dirs:
                _gla_scan(s_ref, bt_ref, st_ref, n_chunks, descending=rev)
                _gla_scan(ds_ref, bt_ref, st_ref, n_chunks, descending=not rev)

            def emit(n, carry, kl=kl, vl=vl, dirs=dirs):
                rows = _gla_rows(n)
                qs = q_ref[rows, kl] * scale
                kk = k_ref[rows, kl]
                vb = v_ref[rows, vl].astype(BF16)
                do_b = do_ref[rows, vl].astype(BF16)
                dq = dkk = dvv = None
                for (rev, _, s_ref, b_ref, bt_ref, ds_ref, dg_ref), (_, mask) in zip(dirs, masks):
                    b = b_ref[rows, :]
                    bt = bt_ref[n]
                    eb = jnp.exp(b)
                    enb = jnp.exp(-b)
                    etb = jnp.exp(bt - b)
                    ebt = jnp.exp(bt)
                    qd = qs * eb
                    ki = kk * enb
                    ke = kk * etb
                    qd_b, ki_b, ke_b = qd.astype(BF16), ki.astype(BF16), ke.astype(BF16)
                    st = s_ref[n]
                    dst = ds_ref[n]
                    dst_b = dst.astype(BF16)
                    a = jnp.where(mask, _dot(qd_b, ki_b, 1, 1), 0.0).astype(BF16)
                    da = jnp.where(mask, _dot(do_b, vb, 1, 1), 0.0).astype(BF16)
                    dv_d = _dot(a, do_b, 0, 0) + _dot(ke_b, dst_b, 1, 1)
                    dqd = _dot(da, ki_b, 1, 0) + _dot(do_b, st.astype(BF16), 1, 0)
                    dki = _dot(da, qd_b, 0, 0)
                    dke = _dot(vb, dst_b, 1, 0)
                    dbt = _colsum(st * dst) * ebt + _colsum(dke * ke)
                    db = dqd * qd - dki * ki - dke * ke
                    db = db + jnp.where(rowc == (0 if rev else c - 1), dbt, 0.0)
                    dg_ref[rows, kl] = _chunk_cumsum(db, not rev)
                    dq_d = dqd * eb * scale
                    dk_d = dki * enb + dke * etb
                    dq = dq_d if dq is None else dq + dq_d
                    dkk = dk_d if dkk is None else dkk + dk_d
                    dvv = dv_d if dvv is None else dvv + dv_d
                dq_ref[rows, kl] = dq
                dk_ref[rows, kl] = dkk
                dv_ref[rows, vl] = dvv
                return carry

            lax.fori_loop(0, n_chunks, emit, 0, unroll=2)

    n_pairs, blk_k, blk_gb, blk_v = _gla_specs(s)
    vk, vv = _sds(q.shape), _sds(v.shape)
    state = pltpu.VMEM((n_chunks, dv, dk), F32)
    scratch = [state, state, pltpu.VMEM((s, dk), F32), pltpu.VMEM((s, dk), F32), pltpu.VMEM((n_chunks, 1, dk), F32),
               pltpu.VMEM((n_chunks, 1, dk), F32), state, state, pltpu.VMEM((dv, dk), F32)]
    return pl.pallas_call(
        body, name="gla_bwd", grid=(n_pairs,), in_specs=[blk_k, blk_k, blk_v, blk_k, blk_gb, blk_v],
        out_specs=[blk_k, blk_k, blk_v, blk_k, blk_k], out_shape=[vk, vk, vv, vk, vk],
        scratch_shapes=scratch, compiler_params=_params(("parallel",)),
    )(q, k, v, la, la, do)


@jax.custom_vjp
def gla(q, k, v, la):
    return _gla_fwd(q, k, v, la)


def _gla_f(q, k, v, la):
    return _gla_fwd(q, k, v, la), (q, k, v, la)


def _gla_b(res, do):
    dq, dk, dv, dgf, dgb = _gla_bwd(*res, do)
    return dq, dk, dv, jnp.concatenate([dgf, dgb], axis=1)


gla.defvjp(_gla_f, _gla_b)


ATTN_TQ = 256
HEAD_LANES = 128


def _attn_blocks(s, tq):
    per_q = pl.BlockSpec((tq, HEAD_LANES), lambda h, j: (j, h))
    k_nope = pl.BlockSpec((s, HEAD_LANES), lambda h, j: (0, 2 * h))
    v_blk = pl.BlockSpec((s, HEAD_LANES), lambda h, j: (0, 2 * h + 1))
    k_rope = pl.BlockSpec((s, HEAD_LANES), lambda h, j: (0, 0))
    lse = pl.BlockSpec((1, tq, 1), lambda h, j: (h, j, 0))
    return per_q, k_nope, v_blk, k_rope, lse


def _attn_fwd(qn, qr, kv, kr):
    s = qn.shape[0]
    tq = min(ATTN_TQ, s)
    scale = (MLA_NOPE + MLA_ROPE) ** -0.5

    def body(qn_ref, qr_ref, kn_ref, v_ref, kr_ref, o_ref, lse_ref):
        sc = (_dot(qn_ref[...], kn_ref[...], 1, 1) + _dot(qr_ref[...], kr_ref[...], 1, 1)) * scale
        m = jnp.max(sc, axis=-1, keepdims=True)
        p = jnp.exp(sc - m)
        l = jnp.sum(p, axis=-1, keepdims=True)
        p = p * (1.0 / l)
        o_ref[...] = _dot(p.astype(BF16), v_ref[...], 1, 0)
        lse_ref[0] = m + jnp.log(l)

    per_q, k_nope, v_blk, k_rope, lse = _attn_blocks(s, tq)
    return pl.pallas_call(
        body, name="attn_fwd", grid=(MLA_HEADS, s // tq), in_specs=[per_q, per_q, k_nope, v_blk, k_rope],
        out_specs=[per_q, lse], out_shape=[_sds(qn.shape), _sds((MLA_HEADS, s, 1))],
        compiler_params=_params(("parallel", "parallel")),
    )(qn, qr, kv, kv, kr)


def _attn_bwd(qn, qr, kv, kr, o, lse, do):
    s = qn.shape[0]
    tq = min(ATTN_TQ, s)
    n_q = s // tq
    scale = (MLA_NOPE + MLA_ROPE) ** -0.5

    def body(qn_ref, qr_ref, kn_ref, v_ref, kr_ref, o_ref, lse_ref, do_ref, dqn_ref, dqr_ref, dkv_ref, dkr_ref,
             dk_acc, dv_acc, dkr_acc):
        h, j = pl.program_id(0), pl.program_id(1)
        qn_v, qr_v, kn_v, kr_v = qn_ref[...], qr_ref[...], kn_ref[...], kr_ref[...]
        do = do_ref[...]
        do_b = do.astype(BF16)
        p = jnp.exp((_dot(qn_v, kn_v, 1, 1) + _dot(qr_v, kr_v, 1, 1)) * scale - lse_ref[0])
        dp = _dot(do_b, v_ref[...], 1, 1)
        delta = jnp.sum(do * o_ref[...], axis=-1, keepdims=True)
        ds = (p * (dp - delta) * scale).astype(BF16)
        dqn_ref[...] = _dot(ds, kn_v, 1, 0).astype(BF16)
        dqr_ref[...] = _dot(ds, kr_v, 1, 0).astype(BF16)
        _acc(j, dk_acc, _dot(ds, qn_v, 0, 0))
        _acc(j, dv_acc, _dot(p.astype(BF16), do_b, 0, 0))
        _acc(jnp.where(jnp.logical_and(h == 0, j == 0), 0, 1), dkr_acc, _dot(ds, qr_v, 0, 0))

        @pl.when(j == n_q - 1)
        def _():
            dkv_ref[:, 0:HEAD_LANES] = dk_acc[...].astype(BF16)
            dkv_ref[:, HEAD_LANES:2 * HEAD_LANES] = dv_acc[...].astype(BF16)

        @pl.when(jnp.logical_and(h == MLA_HEADS - 1, j == n_q - 1))
        def _():
            dkr_ref[...] = dkr_acc[...].astype(BF16)

    per_q, k_nope, v_blk, k_rope, lse_blk = _attn_blocks(s, tq)
    dkv_blk = pl.BlockSpec((s, 2 * HEAD_LANES), lambda h, j: (0, h))
    acc = pltpu.VMEM((s, HEAD_LANES), F32)
    return pl.pallas_call(
        body, name="attn_bwd", grid=(MLA_HEADS, n_q),
        in_specs=[per_q, per_q, k_nope, v_blk, k_rope, per_q, lse_blk, per_q],
        out_specs=[per_q, per_q, dkv_blk, k_rope],
        out_shape=[_sds(qn.shape, BF16), _sds(qr.shape, BF16), _sds(kv.shape, BF16), _sds(kr.shape, BF16)],
        scratch_shapes=[acc, acc, acc], compiler_params=_params(("arbitrary", "arbitrary")),
    )(qn, qr, kv, kv, kr, o, lse, do)


@jax.custom_vjp
def attn(qn, qr, kv, kr):
    return _attn_fwd(qn, qr, kv, kr)[0]


def _attn_f(qn, qr, kv, kr):
    o, lse = _attn_fwd(qn, qr, kv, kr)
    return o, (qn, qr, kv, kr, o, lse)


def _attn_b(res, do):
    return tuple(_attn_bwd(*res, do))


attn.defvjp(_attn_f, _attn_b)


@jax.custom_vjp
def split_proj(proj):
    out, at = [], 0
    for _, _, _, wp in PROJ_SEGS:
        out.append(proj[:, at:at + wp])
        at += wp
    return tuple(out)


def _split_f(proj):
    return split_proj(proj), None


def _split_b(_, gs):
    return (jnp.concatenate(gs, axis=1),)


split_proj.defvjp(_split_f, _split_b)


def _tile2d(rows, width, limit=BLOCK_BYTES):
    fits = [t for t in range(16, rows + 1, 16) if rows % t == 0 and t * width * 4 <= limit]
    if fits and (fits[-1] >= 64 or fits[-1] == rows):
        return fits[-1], width
    if rows * width * 4 <= limit:
        return rows, width
    cols = [t for t in range(128, width + 1, 128) if width % t == 0 and rows * t * 4 <= limit]
    return (rows, cols[-1]) if cols else (rows, width)


def _add_pair(stacked, theirs, c_idx):
    g, r, w = theirs.shape
    tr, tc = _tile2d(r, w)

    def body(c_ref, a_ref, b_ref, o_ref):
        o_ref[0] = (a_ref[0, 0].astype(F32) + b_ref[0].astype(F32)).astype(BF16)

    blk = pl.BlockSpec((1, tr, tc), lambda k, i, j, c: (k, i, j))
    spec = pltpu.PrefetchScalarGridSpec(
        num_scalar_prefetch=1, grid=(g, r // tr, w // tc),
        in_specs=[pl.BlockSpec((1, 1, tr, tc), lambda k, i, j, c: (c[0], k, i, j)), blk], out_specs=blk)
    return pl.pallas_call(body, name="add_pair", grid_spec=spec, out_shape=_sds(theirs.shape, BF16),
                          compiler_params=_params(("parallel", "parallel", "parallel")))(c_idx, stacked, theirs)


def _add_chips(pair, landed, chip_idx):
    _, r, w = pair.shape
    tr, tc = _tile2d(r, w)

    def body(c_ref, p_ref, l0_ref, l1_ref, l2_ref, o_ref):
        o_ref[...] = ((p_ref[0].astype(F32) + l0_ref[0].astype(F32)) + l1_ref[0].astype(F32)) + l2_ref[0].astype(F32)

    specs = [pl.BlockSpec((1, tr, tc), lambda i, j, c: (c[0], i, j))]
    specs += [pl.BlockSpec((1, tr, tc), functools.partial(lambda i, j, c, k: (k, i, j), k=k)) for k in range(N_CHIPS - 1)]
    spec = pltpu.PrefetchScalarGridSpec(num_scalar_prefetch=1, grid=(r // tr, w // tc), in_specs=specs,
                                        out_specs=pl.BlockSpec((tr, tc), lambda i, j, c: (i, j)))
    return pl.pallas_call(body, name="add_chips", grid_spec=spec, out_shape=_sds((r, w)),
                          compiler_params=_params(("parallel", "parallel")))(chip_idx, pair, landed, landed, landed)


def _sum_devices(g):
    n = g.shape[2]

    def body(g_ref, o_ref):
        t = g_ref[0]
        for j in range(1, N_DEV):
            t = t + g_ref[j]
        o_ref[...] = t

    return pl.pallas_call(body, name="sum_devices", out_shape=_sds((1, n)), compiler_params=_params())(g)


def _adamw(w, g, m, v):
    shp = w.shape
    shp3 = (1, 1, shp[0]) if len(shp) == 1 else (-1,) + tuple(shp[-2:])
    w3, g3, m3, v3 = (t.reshape(shp3) for t in (w, g, m, v))
    c1 = 1.0 - ADAM_B1 ** ADAM_STEP
    c2 = 1.0 - ADAM_B2 ** ADAM_STEP

    def body(w_ref, g_ref, m_ref, v_ref, d_ref, mo_ref, vo_ref):
        gv = g_ref[...]
        mn = ADAM_B1 * m_ref[...] + (1.0 - ADAM_B1) * gv
        vn = ADAM_B2 * v_ref[...] + (1.0 - ADAM_B2) * (gv * gv)
        d_ref[...] = -ADAM_LR * ((mn / c1) / (jnp.sqrt(vn / c2) + ADAM_EPS) + ADAM_WD * w_ref[...])
        mo_ref[...] = mn
        vo_ref[...] = vn

    nl, r, wd = w3.shape
    tr, tc = _tile2d(r, wd, BLOCK_BYTES // 2)
    blk = pl.BlockSpec((1, tr, tc), lambda l, i, j: (l, i, j))
    s3 = _sds(w3.shape)
    d, mn, vn = pl.pallas_call(
        body, name="adamw", grid=(nl, r // tr, wd // tc), in_specs=[blk] * 4, out_specs=[blk] * 3,
        out_shape=[s3, s3, s3], compiler_params=_params(("parallel", "parallel", "parallel")),
    )(w3, g3, m3, v3)
    return d.reshape(shp), mn.reshape(shp), vn.reshape(shp)


ANY = pl.BlockSpec(memory_space=pl.ANY)
PIECE_BYTES = 1 << 20


def _place():
    return lax.axis_index("x"), lax.axis_index("y"), lax.axis_index("c")


def _pieces(shape, itemsize):
    if len(shape) >= 3:
        return [(i,) + p for i in range(shape[0]) for p in _pieces(shape[1:], itemsize)]
    rows = shape[0]
    row_bytes = itemsize
    for dsz in shape[1:]:
        row_bytes *= dsz
    k = 1
    while rows % (2 * k) == 0 and (rows // (2 * k)) % 16 == 0 and (rows // k) * row_bytes > PIECE_BYTES:
        k *= 2
    step = rows // k
    return [(pl.ds(j * step, step),) for j in range(k)]


def _split_start(make, src, dst, pieces):
    for p in pieces:
        make(src.at[p], dst.at[p]).start()
    return make(src, dst)


def _comm_call(body, name, arrs, out_shapes, n_remote, n_local):
    return pl.pallas_call(
        body, name=name, in_specs=[ANY] * len(arrs), out_specs=[ANY] * len(out_shapes), out_shape=out_shapes,
        scratch_shapes=[pltpu.SemaphoreType.DMA((n_remote,)), pltpu.SemaphoreType.DMA((n_remote,)),
                        pltpu.SemaphoreType.DMA((n_local,))],
    )(*arrs)


def all_gather8(arrs, name):
    n = len(arrs)
    pieces = [_pieces(a.shape, a.dtype.itemsize) for a in arrs]

    def body(*refs):
        ins, outs = refs[:n], refs[n:2 * n]
        send, recv, _ = refs[2 * n:]
        x, y, c = _place()
        me, sib = (x, y, c), (x, y, 1 - c)
        chips = [(1 - x, y), (x, 1 - y), (1 - x, 1 - y)]

        def slot(p):
            return 4 * p[0] + 2 * p[1] + p[2]

        def maker(t, k, to):
            def make(s, d):
                return pltpu.make_async_remote_copy(src_ref=s, dst_ref=d, send_sem=send.at[7 * t + k],
                                                    recv_sem=recv.at[7 * t + k], device_id=to, device_id_type=MESH)
            return make

        def landing(t, k, block):
            dst = outs[t].at[slot(block)]
            return maker(t, k, me)(dst, dst)

        sent = []
        for t in range(n):
            dst = outs[t].at[slot(me)]
            sent.append(_split_start(maker(t, 0, sib), ins[t], dst, pieces[t]))
            for j, chip in enumerate(chips):
                sent.append(_split_start(maker(t, 1 + j, (*chip, c)), ins[t], dst, pieces[t]))
        for j, chip in enumerate(chips):
            for t in range(n):
                landing(t, 1 + j, (*chip, c)).wait_recv()
                blk = outs[t].at[slot((*chip, c))]
                sent.append(_split_start(maker(t, 4 + j, sib), blk, blk, pieces[t]))
        for t in range(n):
            landing(t, 0, sib).wait_recv()
            for j, chip in enumerate(chips):
                landing(t, 4 + j, (*chip, 1 - c)).wait_recv()
        for cp in sent:
            cp.wait_send()

    outs = [_sds((N_DEV,) + a.shape, a.dtype) for a in arrs]
    got = _comm_call(body, name, arrs, outs, 7 * n, 1)
    x, y, c = _place()
    return [lax.dynamic_update_index_in_dim(g, a, 4 * x + 2 * y + c, 0) for g, a in zip(got, arrs)]


def sibling_send(arrs, name):
    n = len(arrs)
    pieces = [_pieces(a.shape[1:], a.dtype.itemsize) for a in arrs]

    def body(*refs):
        ins, theirs = refs[:n], refs[n:2 * n]
        send, recv, _ = refs[2 * n:]
        x, y, c = _place()
        rem = []
        for t in range(n):
            def make(s, d, t=t):
                return pltpu.make_async_remote_copy(src_ref=s, dst_ref=d, send_sem=send.at[t], recv_sem=recv.at[t],
                                                    device_id=(x, y, 1 - c), device_id_type=MESH)
            rem.append(_split_start(make, ins[t].at[1 - c], theirs[t], pieces[t]))
        for cp in rem:
            cp.wait_recv()
        for cp in rem:
            cp.wait_send()

    outs = [_sds(a.shape[1:], a.dtype) for a in arrs]
    return _comm_call(body, name, arrs, outs, n, 1)


def exchange_chips(arrs, name):
    n = len(arrs)
    pieces = [_pieces(a.shape[1:], a.dtype.itemsize) for a in arrs]

    def body(*refs):
        ins, outs = refs[:n], refs[n:2 * n]
        send, recv, _ = refs[2 * n:]
        x, y, c = _place()
        peers = [(1 - x, y), (x, 1 - y), (1 - x, 1 - y)]
        rem = []
        for t in range(n):
            for j, (px, py) in enumerate(peers):
                def make(s, d, t=t, j=j, px=px, py=py):
                    return pltpu.make_async_remote_copy(
                        src_ref=s, dst_ref=d, send_sem=send.at[3 * t + j], recv_sem=recv.at[3 * t + j],
                        device_id=(px, py, c), device_id_type=MESH)
                rem.append(_split_start(make, ins[t].at[2 * px + py], outs[t].at[j], pieces[t]))
        for cp in rem:
            cp.wait_recv()
        for cp in rem:
            cp.wait_send()

    outs = [_sds((N_CHIPS - 1,) + a.shape[1:], a.dtype) for a in arrs]
    return _comm_call(body, name, arrs, outs, 3 * n, 1)


def sibling_swap(arrs, name):
    n = len(arrs)
    pieces = [_pieces(a.shape, a.dtype.itemsize) for a in arrs]

    def body(*refs):
        ins, outs = refs[:n], refs[n:2 * n]
        send, recv, _ = refs[2 * n:]
        x, y, c = _place()
        rem = []
        for t in range(n):
            def make(s, d, t=t):
                return pltpu.make_async_remote_copy(src_ref=s, dst_ref=d, send_sem=send.at[t], recv_sem=recv.at[t],
                                                    device_id=(x, y, 1 - c), device_id_type=MESH)
            rem.append(_split_start(make, ins[t], outs[t], pieces[t]))
        for cp in rem:
            cp.wait_recv()
        for cp in rem:
            cp.wait_send()

    outs = [_sds(a.shape, a.dtype) for a in arrs]
    return _comm_call(body, name, arrs, outs, n, 1)


@jax.custom_vjp
def _build_w_in(w4):
    full = w4.reshape(-1, w4.shape[-1])
    parts = []
    for _, start, width, wp in PROJ_SEGS:
        if width:
            parts.append(full[start:start + width])
        if wp > width:
            parts.append(jnp.zeros((wp - width, full.shape[1]), full.dtype))
    return jnp.concatenate(parts, axis=0)


def _build_w_in_f(w4):
    return _build_w_in(w4), None


def _build_w_in_b(_, g):
    parts, at = [], 0
    for _, _, width, wp in PROJ_SEGS:
        if width:
            parts.append(g[at:at + width])
        at += wp
    return (jnp.concatenate(parts, axis=0).reshape(N_CHIPS, -1, g.shape[1]),)


_build_w_in.defvjp(_build_w_in_f, _build_w_in_b)


def _split_w_uq(w):
    w3 = w.reshape(w.shape[0], MLA_HEADS, MLA_NOPE + MLA_ROPE)
    return w3[:, :, :MLA_NOPE].reshape(w.shape[0], -1), w3[:, :, MLA_NOPE:].reshape(w.shape[0], -1)


def _swap_halves(t, width):
    t3 = t.reshape(t.shape[0], -1, 2, width // 2)
    return jnp.concatenate([t3[:, :, 1:], t3[:, :, :1]], axis=2).reshape(t.shape)


def _pad_heads(t, width):
    t3 = t.reshape(t.shape[0], -1, width)
    t3 = jnp.pad(t3, ((0, 0), (0, 0), (0, HEAD_LANES - width)))
    return t3.reshape(t.shape[0], -1).astype(BF16)


def _layer(xh, mod, big, small, rope_q, rope_k):
    d = D_MODEL
    shift, scale, gate = mod[None, 0:d], mod[None, d:2 * d], mod[None, 2 * d:3 * d]
    w_al = _build_w_in(big["w_in"])
    proj = mod_mm(xh, small["norm_g"][None], scale, shift, w_al)
    gq, gk, gv, glr, mq, mkv, mkr, cb, cc, cx, _, z = split_proj(proj)

    rk = GLA_RANK
    hk = GLA_HEADS * GLA_DK
    wg = jnp.zeros((128, 2 * hk), F32)
    wg = wg.at[0:rk, 0:hk].set(small["gla_wg_f"]).at[rk:2 * rk, hk:].set(small["gla_wg_b"])
    bg = jnp.concatenate([small["gla_bg_f"], small["gla_bg_b"]])[None]
    la = gate_act(mm(glr, wg), bg)
    o_gla = rmsnorm(gla(gq, gk, gv, la), small["gla_norm_g"][None])

    cq = rmsnorm(mq, small["mla_q_norm_g"][None])
    w_nope, w_rope = _split_w_uq(jnp.concatenate([big["w_uq"][j] for j in range(N_CHIPS)], axis=1))
    qn = mm16(cq, w_nope)
    qr = mm(cq, w_rope)
    qr = fma(qr, rope_q[0], _swap_halves(qr, MLA_ROPE), rope_q[1])
    ckv = rmsnorm(mkv, small["mla_kv_norm_g"][None])
    kv = mm16(ckv, jnp.concatenate([big["w_ukv"][j] for j in range(N_CHIPS)], axis=1))
    kr = mkr[:, :MLA_ROPE]
    kr = fma(kr, rope_k[0], _swap_halves(kr, MLA_ROPE), rope_k[1])
    o_mla = rmsnorm(attn(qn, _pad_heads(qr, MLA_ROPE), kv, _pad_heads(kr, MLA_ROPE)), small["mla_out_g"][None])

    cw = jnp.concatenate([small["conv_w"], jnp.zeros((5, CONV_CH), F32)], axis=0)
    o_conv = rmsnorm(conv_op(cb, cc, cx, cw), small["conv_out_g"][None])

    o = jnp.concatenate([o_gla, o_mla, o_conv], axis=1)
    w_out = big["w_out"].reshape(d, d)
    return out_block(o, z, w_out, xh, gate)


def _loss_fn(xh, mods, bigs, smalls, final_g, target, rope_q, rope_k):
    h = xh
    for l in range(DEPTH):
        h = _layer(h, mods[l], bigs[l], smalls[l], rope_q, rope_k)
    return loss_op(rmsnorm(h, final_g[None]), target)[0, 0]


SMALL_REPL = ("norm_g", "gla_bg_f", "gla_bg_b", "gla_norm_g", "mla_q_norm_g", "mla_kv_norm_g", "mla_out_g",
              "conv_out_g")
SMALL_SHARDED = ("gla_wg_f", "gla_wg_b", "conv_w")
BIG = ("w_in", "w_out", "w_uq", "w_ukv")


def kernel(x, c, positions, ada_w, ada_b, norm_g, w_in, gla_wg_f, gla_bg_f, gla_wg_b, gla_bg_b, gla_norm_g, mla_q_norm_g, mla_kv_norm_g, mla_w_uq, mla_w_ukv, mla_out_g, conv_w, conv_out_g, w_out, final_g, loss_target, m_ada_w, m_ada_b, m_norm_g, m_w_in, m_gla_wg_f, m_gla_bg_f, m_gla_wg_b, m_gla_bg_b, m_gla_norm_g, m_mla_q_norm_g, m_mla_kv_norm_g, m_mla_w_uq, m_mla_w_ukv, m_mla_out_g, m_conv_w, m_conv_out_g, m_w_out, m_final_g, v_ada_w, v_ada_b, v_norm_g, v_w_in, v_gla_wg_f, v_gla_bg_f, v_gla_wg_b, v_gla_bg_b, v_gla_norm_g, v_mla_q_norm_g, v_mla_kv_norm_g, v_mla_w_uq, v_mla_w_ukv, v_mla_out_g, v_conv_w, v_conv_out_g, v_w_out, v_final_g):
    xi, yi, ci = _place()
    chip = 2 * xi + yi
    dev = 2 * chip + ci
    s = x.shape[1]
    d = D_MODEL
    weights = dict(ada_w=ada_w, ada_b=ada_b, norm_g=norm_g, w_in=w_in, gla_wg_f=gla_wg_f, gla_bg_f=gla_bg_f,
                   gla_wg_b=gla_wg_b, gla_bg_b=gla_bg_b, gla_norm_g=gla_norm_g, mla_q_norm_g=mla_q_norm_g,
                   mla_kv_norm_g=mla_kv_norm_g, mla_w_uq=mla_w_uq, mla_w_ukv=mla_w_ukv, mla_out_g=mla_out_g,
                   conv_w=conv_w, conv_out_g=conv_out_g, w_out=w_out, final_g=final_g)
    m_in = dict(ada_w=m_ada_w, ada_b=m_ada_b, norm_g=m_norm_g, w_in=m_w_in, gla_wg_f=m_gla_wg_f, gla_bg_f=m_gla_bg_f,
                gla_wg_b=m_gla_wg_b, gla_bg_b=m_gla_bg_b, gla_norm_g=m_gla_norm_g, mla_q_norm_g=m_mla_q_norm_g,
                mla_kv_norm_g=m_mla_kv_norm_g, mla_w_uq=m_mla_w_uq, mla_w_ukv=m_mla_w_ukv, mla_out_g=m_mla_out_g,
                conv_w=m_conv_w, conv_out_g=m_conv_out_g, w_out=m_w_out, final_g=m_final_g)
    v_in = dict(ada_w=v_ada_w, ada_b=v_ada_b, norm_g=v_norm_g, w_in=v_w_in, gla_wg_f=v_gla_wg_f, gla_bg_f=v_gla_bg_f,
                gla_wg_b=v_gla_wg_b, gla_bg_b=v_gla_bg_b, gla_norm_g=v_gla_norm_g, mla_q_norm_g=v_mla_q_norm_g,
                mla_kv_norm_g=v_mla_kv_norm_g, mla_w_uq=v_mla_w_uq, mla_w_ukv=v_mla_w_ukv, mla_out_g=v_mla_out_g,
                conv_w=v_conv_w, conv_out_g=v_conv_out_g, w_out=v_w_out, final_g=v_final_g)

    def mine_bf16(w):
        return lax.dynamic_index_in_dim(w, ci, 0, keepdims=False).astype(BF16)

    g_c, g_in, g_out, g_uq, g_ukv, g_wgf, g_wgb, g_cw = all_gather8(
        [c, mine_bf16(jnp.swapaxes(w_in, 1, 2)), mine_bf16(w_out), mine_bf16(mla_w_uq), mine_bf16(mla_w_ukv),
         gla_wg_f, gla_wg_b, conv_w], "gather_weights")

    def by_layer(g):
        g4 = g.reshape((N_CHIPS, 2) + g.shape[1:])
        return [g4[:, l] for l in range(DEPTH)]

    bigs = [dict(w_in=a, w_out=b, w_uq=u, w_ukv=k)
            for a, b, u, k in zip(by_layer(g_in), by_layer(g_out), by_layer(g_uq), by_layer(g_ukv))]

    def unshard_cols(g):
        g4 = g[0::2]
        return g4.transpose(1, 2, 0, 3).reshape(g4.shape[1], g4.shape[2], -1)

    small_full = dict(gla_wg_f=unshard_cols(g_wgf), gla_wg_b=unshard_cols(g_wgb), conv_w=unshard_cols(g_cw))
    for nme in SMALL_REPL:
        small_full[nme] = weights[nme]
    smalls = [{nme: small_full[nme][l] for nme in SMALL_REPL + SMALL_SHARDED} for l in range(DEPTH)]

    c_act = _silu_rows(g_c[:, 0, :])
    c_act16 = jnp.concatenate([c_act, jnp.zeros_like(c_act)], axis=0)
    n_ada = ada_w.shape[2]
    parts = []
    for l in range(DEPTH):
        bias = lax.dynamic_slice_in_dim(ada_b[l], chip * n_ada, n_ada)[None]
        parts.append(_mm(c_act16, ada_w[l], bias=bias, name="ada_fwd"))
    g_mod, = all_gather8([jnp.stack(parts)], "gather_mod")
    mod_mine = lax.dynamic_index_in_dim(g_mod[0::2], dev, 2, keepdims=False)
    mods = mod_mine.transpose(1, 0, 2).reshape(DEPTH, 3 * d)

    inv_freq = ROPE_THETA ** (-jnp.arange(0, MLA_ROPE, 2, dtype=F32) / MLA_ROPE)
    ang = positions[0].astype(F32)[:, None] * inv_freq
    cos, sin = jnp.cos(ang), jnp.sin(ang)
    rope_k = (jnp.concatenate([cos, cos], axis=1), jnp.concatenate([-sin, sin], axis=1))
    rope_q = (jnp.tile(rope_k[0], (1, MLA_HEADS)), jnp.tile(rope_k[1], (1, MLA_HEADS)))

    loss_dev, (dx, dmods, dbigs, dsmalls, dfinal) = jax.value_and_grad(_loss_fn, argnums=(0, 1, 2, 3, 4))(
        x[0], mods, bigs, smalls, final_g, loss_target[0], rope_q, rope_k)
    loss = lax.psum(loss_dev, ("x", "y", "c"))

    pieces = [dmods.reshape(-1), dfinal]
    for nme in SMALL_REPL + SMALL_SHARDED:
        pieces.append(jnp.stack([dsmalls[l][nme] for l in range(DEPTH)]).reshape(-1))
    sizes = [p.shape[0] for p in pieces]
    flat = jnp.concatenate(pieces)
    padn = (-flat.shape[0]) % 128
    flat = jnp.concatenate([flat, jnp.zeros((padn,), F32)])[None]
    g_small, = all_gather8([flat], "gather_small_grads")
    total = _sum_devices(g_small)[0]
    offs, at = [], 0
    for n_el in sizes:
        offs.append(at)
        at += n_el

    def piece(i, shape):
        return total[offs[i]:offs[i] + sizes[i]].reshape(shape)

    grads = {"ada_b": piece(0, (DEPTH, 3 * d)), "final_g": piece(1, (d,))}
    for i, nme in enumerate(SMALL_REPL + SMALL_SHARDED):
        full = piece(2 + i, small_full[nme].shape)
        if nme in SMALL_SHARDED:
            ncol = weights[nme].shape[2]
            full = lax.dynamic_slice_in_dim(full, chip * ncol, ncol, axis=2)
        grads[nme] = full

    dmod_all = g_small[:, 0, :DEPTH * 3 * d].reshape(N_DEV, DEPTH, 3 * d)
    dmod_cols = lax.dynamic_slice_in_dim(dmod_all, chip * n_ada, n_ada, axis=2)
    g_ada = []
    for l in range(DEPTH):
        dm16 = jnp.concatenate([dmod_cols[:, l], jnp.zeros((N_DEV, n_ada), F32)], axis=0)
        g_ada.append(_mm(c_act16, dm16, ta=True, name="ada_bwd"))
    grads["ada_w"] = jnp.stack(g_ada)

    stacked = [jnp.stack([dbigs[l][nme] for l in range(DEPTH)]) for nme in BIG]
    c_idx = jnp.reshape(ci, (1,)).astype(jnp.int32)
    chip_idx = jnp.reshape(chip, (1,)).astype(jnp.int32)
    theirs = sibling_send(stacked, "reduce_sibling")
    pair = [_add_pair(a, b, c_idx) for a, b in zip(stacked, theirs)]
    landed = exchange_chips(pair, "reduce_chips")
    reduced = [_add_chips(p, q, chip_idx) for p, q in zip(pair, landed)]
    others = sibling_swap(reduced, "share_sibling")
    for nme, own, other in zip(("w_in", "w_out", "mla_w_uq", "mla_w_ukv"), reduced, others):
        grads[nme] = jnp.stack([jnp.where(ci == l, own, other) for l in range(DEPTH)])

    order = list(weights)
    delta, new_m, new_v = {}, {}, {}
    for nme in order:
        if nme == "w_in":
            w_t, m_t, v_t = (jnp.swapaxes(t, 1, 2) for t in (w_in, m_w_in, v_w_in))
            res = _adamw(w_t, grads[nme], m_t, v_t)
            delta[nme], new_m[nme], new_v[nme] = (jnp.swapaxes(t, 1, 2) for t in res)
            grads[nme] = jnp.swapaxes(grads[nme], 1, 2)
            continue
        delta[nme], new_m[nme], new_v[nme] = _adamw(weights[nme], grads[nme], m_in[nme], v_in[nme])
    return (loss, dx[None], *[grads[n_] for n_ in order], *[delta[n_] for n_ in order],
            *[new_m[n_] for n_ in order], *[new_v[n_] for n_ in order])
```

```python
import functools

import jax
import jax.numpy as jnp
from jax import lax
from jax.experimental import pallas as pl
from jax.experimental.pallas import tpu as pltpu

F32 = jnp.float32
BF16 = jnp.bfloat16
MESH = pl.DeviceIdType.MESH
HIGHEST = lax.Precision.HIGHEST

DEPTH = 2
D_MODEL = 2048
GLA_HEADS = 6
GLA_DK = 64
GLA_DV = 128
GLA_RANK = 16
GLA_TEMP = 16.0
GLA_CHUNK = 64
GLA_W = GLA_HEADS * GLA_DV
MLA_HEADS = 6
MLA_QL = 384
MLA_KVL = 256
MLA_NOPE = 128
MLA_ROPE = 64
MLA_DV = 128
MLA_W = MLA_HEADS * MLA_DV
CONV_CH = D_MODEL - GLA_W - MLA_W
ROPE_THETA = 10000.0
EPS = 1e-6
IN_DIM = 5856
N_CHIPS = 4
N_DEV = 8

ADAM_LR = 0.001
ADAM_B1 = 0.9
ADAM_B2 = 0.999
ADAM_EPS = 1e-08
ADAM_WD = 0.01
ADAM_STEP = 10

PROJ_SEGS = (
    ("gq", 0, 384, 384), ("gk", 384, 384, 384), ("gv", 768, 768, 768), ("glr", 1536, 32, 128),
    ("mq", 1568, 384, 384), ("mkv", 1952, 256, 256), ("mkr", 2208, 64, 128),
    ("cb", 2272, 512, 512), ("cc", 2784, 512, 512), ("cx", 3296, 512, 512),
    ("pad", 3808, 0, 128), ("z", 3808, 2048, 2048),
)
PROJ_AL = sum(s[3] for s in PROJ_SEGS)

VMEM_LIMIT = 48 * 1024 * 1024
BLOCK_BYTES = 2 * 1024 * 1024


def _params(sem=None):
    return pltpu.CompilerParams(dimension_semantics=sem, vmem_limit_bytes=VMEM_LIMIT)


def _dot(a, b, ca, cb, precision=None):
    return lax.dot_general(a, b, (((ca,), (cb,)), ((), ())), preferred_element_type=F32, precision=precision)


def _tile(dim, prefs):
    for t in prefs:
        if dim % t == 0:
            return t
    return dim


def _pick_rows(rows, width, itemsize=4):
    for t in (2048, 1024, 512, 256, 128, 64, 32, 16, 8):
        if rows % t == 0 and t * width * itemsize <= BLOCK_BYTES:
            return t
    return rows


def _mm(a, b, *, ta=False, tb=False, bias=None, out_dtype=F32, name="mm"):
    if ta:
        K, M = a.shape
    else:
        M, K = a.shape
    if tb:
        N, Kb = b.shape
    else:
        Kb, N = b.shape
    assert K == Kb, (a.shape, b.shape, ta, tb)
    tm = _tile(M, (512, 256, 128))
    tn = _tile(N, (1024, 512, 384, 256, 128))
    tk = _tile(K, (2048, 1024, 512, 256, 128))
    nk = K // tk
    has_bias = bias is not None

    def body(*refs):
        a_ref, b_ref = refs[0], refs[1]
        bias_ref = refs[2] if has_bias else None
        o_ref = refs[3 if has_bias else 2]
        part = _dot(a_ref[...].astype(BF16), b_ref[...].astype(BF16), 0 if ta else 1, 1 if tb else 0)

        def finish(r):
            if has_bias:
                r = r + bias_ref[...]
            o_ref[...] = r.astype(out_dtype)

        if nk == 1:
            finish(part)
            return
        acc_ref = refs[-1]
        k = pl.program_id(2)

        @pl.when(k == 0)
        def _():
            acc_ref[...] = part

        @pl.when(k != 0)
        def _():
            acc_ref[...] += part

        @pl.when(k == nk - 1)
        def _():
            finish(acc_ref[...])

    a_spec = pl.BlockSpec((tk, tm), lambda i, j, k: (k, i)) if ta else pl.BlockSpec((tm, tk), lambda i, j, k: (i, k))
    b_spec = pl.BlockSpec((tn, tk), lambda i, j, k: (j, k)) if tb else pl.BlockSpec((tk, tn), lambda i, j, k: (k, j))
    in_specs = [a_spec, b_spec]
    args = [a, b]
    if has_bias:
        in_specs.append(pl.BlockSpec((1, tn), lambda i, j, k: (0, j)))
        args.append(bias)
    return pl.pallas_call(
        body, name=name, grid=(M // tm, N // tn, nk),
        in_specs=in_specs, out_specs=pl.BlockSpec((tm, tn), lambda i, j, k: (i, j)),
        out_shape=jax.ShapeDtypeStruct((M, N), out_dtype),
        scratch_shapes=[pltpu.VMEM((tm, tn), F32)] if nk > 1 else [],
        compiler_params=_params(("parallel", "parallel", "arbitrary")),
    )(*args)


@jax.custom_vjp
def mm(a, b):
    return _mm(a, b, name="mm_fwd")


def _mm_f(a, b):
    return _mm(a, b, name="mm_fwd"), (a, b)


def _mm_b(res, g):
    a, b = res
    return _mm(g, b, tb=True, out_dtype=a.dtype, name="mm_da"), _mm(a, g, ta=True, out_dtype=b.dtype, name="mm_db")


mm.defvjp(_mm_f, _mm_b)


@jax.custom_vjp
def mm16(a, b):
    return _mm(a, b, out_dtype=BF16, name="mm16_fwd")


def _mm16_f(a, b):
    return mm16(a, b), (a, b)


mm16.defvjp(_mm16_f, _mm_b)


def _rows(body, name, tiled, full, tiled_out, acc_out, tr=None):
    rows = tiled[0].shape[0]
    if tr is None:
        width = max([a.shape[1] for a in tiled] + [s.shape[1] for s in tiled_out])
        tr = _pick_rows(rows, width)
    in_specs = [pl.BlockSpec((tr, a.shape[1]), lambda i: (i, 0)) for a in tiled]
    in_specs += [pl.BlockSpec(a.shape, lambda i: (0, 0)) for a in full]
    out_specs = [pl.BlockSpec((tr, s.shape[1]), lambda i: (i, 0)) for s in tiled_out]
    out_specs += [pl.BlockSpec(s.shape, lambda i: (0, 0)) for s in acc_out]

    def wrapped(*refs):
        body(pl.program_id(0), *refs)

    outs = pl.pallas_call(
        wrapped, name=name, grid=(rows // tr,), in_specs=in_specs, out_specs=out_specs,
        out_shape=list(tiled_out) + list(acc_out),
        compiler_params=_params(("arbitrary",)),
    )(*tiled, *full)
    return outs


def _sds(shape, dtype=F32):
    return jax.ShapeDtypeStruct(tuple(shape), dtype)


def _acc(step, ref, val):
    @pl.when(step == 0)
    def _():
        ref[...] = val

    @pl.when(step != 0)
    def _():
        ref[...] += val


def _colsum(v):
    return jnp.sum(v, axis=0, keepdims=True)


def _rstd(x):
    return lax.rsqrt(jnp.mean(x * x, axis=-1, keepdims=True) + EPS)


def _norm_grid(x, g):
    rows, w = x.shape[0], g.shape[1]
    tr = _pick_rows(rows, w)
    blk = pl.BlockSpec((tr, w), lambda i, j: (i, j))
    gblk = pl.BlockSpec((1, w), lambda i, j: (0, 0))
    return (rows // tr, x.shape[1] // w), blk, gblk


@jax.custom_vjp
def rmsnorm(x, g):
    def body(x_ref, g_ref, o_ref):
        x = x_ref[...]
        o_ref[...] = x * _rstd(x) * g_ref[...]

    grid, blk, gblk = _norm_grid(x, g)
    return pl.pallas_call(body, name="rmsnorm_fwd", grid=grid, in_specs=[blk, gblk], out_specs=blk,
                          out_shape=_sds(x.shape), compiler_params=_params(("parallel", "parallel")))(x, g)


def _rmsnorm_f(x, g):
    return rmsnorm(x, g), (x, g)


def _rmsnorm_b(res, dy):
    x, g = res

    def body(x_ref, dy_ref, g_ref, dx_ref, dg_ref):
        x = x_ref[...]
        dy = dy_ref[...]
        r = _rstd(x)
        xh = x * r
        dxh = dy * g_ref[...]
        dx_ref[...] = r * (dxh - xh * jnp.mean(dxh * xh, axis=-1, keepdims=True))
        first = jnp.logical_and(pl.program_id(0) == 0, pl.program_id(1) == 0)
        _acc(jnp.where(first, 0, 1), dg_ref, _colsum(dy * xh))

    grid, blk, gblk = _norm_grid(x, g)
    dx, dg = pl.pallas_call(body, name="rmsnorm_bwd", grid=grid, in_specs=[blk, blk, gblk], out_specs=[blk, gblk],
                            out_shape=[_sds(x.shape), _sds(g.shape)],
                            compiler_params=_params(("arbitrary", "arbitrary")))(x, dy, g)
    return dx, dg


rmsnorm.defvjp(_rmsnorm_f, _rmsnorm_b)


def _modulate(x, g, scale, shift):
    def body(i, x_ref, g_ref, sc_ref, sh_ref, o_ref):
        x = x_ref[...]
        xn = x * _rstd(x) * g_ref[...]
        o_ref[...] = (xn * (1.0 + sc_ref[...]) + sh_ref[...]).astype(BF16)
    return _rows(body, "modulate_fwd", [x], [g, scale, shift], [_sds(x.shape, BF16)], [])[0]


def _modulate_bwd(x, g, scale, shift, dh):
    def body(i, x_ref, dh_ref, g_ref, sc_ref, dx_ref, dg_ref, dsc_ref, dsh_ref):
        x = x_ref[...]
        dh = dh_ref[...]
        gv = g_ref[...]
        r = _rstd(x)
        xh = x * r
        dxn = dh * (1.0 + sc_ref[...])
        dxh = dxn * gv
        dx_ref[...] = r * (dxh - xh * jnp.mean(dxh * xh, axis=-1, keepdims=True))
        _acc(i, dg_ref, _colsum(dxn * xh))
        _acc(i, dsc_ref, _colsum(dh * (xh * gv)))
        _acc(i, dsh_ref, _colsum(dh))

    v = _sds(g.shape)
    return _rows(body, "modulate_bwd", [x, dh], [g, scale], [_sds(x.shape)], [v, v, v])


@jax.custom_vjp
def mod_mm(x, g, scale, shift, wt):
    return _mm(_modulate(x, g, scale, shift), wt, tb=True, name="mm_in")


def _mod_mm_f(x, g, scale, shift, wt):
    h = _modulate(x, g, scale, shift)
    return _mm(h, wt, tb=True, name="mm_in"), (x, g, scale, shift, wt, h)


def _mod_mm_b(res, dproj):
    x, g, scale, shift, wt, h = res
    dproj = dproj.astype(BF16)
    dh = _mm(dproj, wt, name="mm_in_dh")
    dwt = _mm(dproj, h, ta=True, out_dtype=wt.dtype, name="mm_in_dw")
    dx, dg, dsc, dsh = _modulate_bwd(x, g, scale, shift, dh)
    return dx, dg, dsc, dsh, dwt


mod_mm.defvjp(_mod_mm_f, _mod_mm_b)


def _sigmoid(z):
    return 1.0 / (1.0 + jnp.exp(-z))


def _gate_mul(o, z):
    def body(i, o_ref, z_ref, y_ref):
        z = z_ref[...]
        y_ref[...] = (o_ref[...] * (z * _sigmoid(z))).astype(BF16)
    return _rows(body, "gate_mul_fwd", [o, z], [], [_sds(o.shape, BF16)], [])[0]


def _gate_mul_bwd(o, z, dy):
    def body(i, o_ref, z_ref, dy_ref, do_ref, dz_ref):
        z = z_ref[...]
        dy = dy_ref[...]
        s = _sigmoid(z)
        do_ref[...] = dy * (z * s)
        dz_ref[...] = dy * o_ref[...] * (s * (1.0 + z * (1.0 - s)))
    return _rows(body, "gate_mul_bwd", [o, z, dy], [], [_sds(o.shape), _sds(o.shape)], [])


def _residual(x, u, gate):
    def body(i, x_ref, u_ref, g_ref, o_ref):
        o_ref[...] = x_ref[...] + g_ref[...] * u_ref[...]
    return _rows(body, "residual_fwd", [x, u], [gate], [_sds(x.shape)], [])[0]


def _residual_bwd(d, u, gate):
    def body(i, d_ref, u_ref, g_ref, du_ref, dg_ref):
        d = d_ref[...]
        du_ref[...] = (g_ref[...] * d).astype(BF16)
        _acc(i, dg_ref, _colsum(d * u_ref[...]))

    return _rows(body, "residual_bwd", [d, u], [gate], [_sds(u.shape, BF16)], [_sds(gate.shape)])


@jax.custom_vjp
def out_block(o, z, w, x, gate):
    return _residual(x, _mm(_gate_mul(o, z), w, name="mm_out"), gate)


def _out_block_f(o, z, w, x, gate):
    y = _gate_mul(o, z)
    u = _mm(y, w, name="mm_out")
    return _residual(x, u, gate), (o, z, w, y, u, gate)


def _out_block_b(res, d):
    o, z, w, y, u, gate = res
    du, dgate = _residual_bwd(d, u, gate)
    dy = _mm(du, w, tb=True, name="mm_out_dy")
    dw = _mm(y, du, ta=True, out_dtype=w.dtype, name="mm_out_dw")
    do, dz = _gate_mul_bwd(o, z, dy)
    return do, dz, dw, d, dgate


out_block.defvjp(_out_block_f, _out_block_b)


@jax.custom_vjp
def gate_act(u, b):
    def body(i, u_ref, b_ref, o_ref):
        t = u_ref[...] + b_ref[...]
        o_ref[...] = (jnp.minimum(t, 0.0) - jnp.log(1.0 + jnp.exp(-jnp.abs(t)))) / GLA_TEMP
    return _rows(body, "gate_act_fwd", [u], [b], [_sds(u.shape)], [])[0]


def _gate_act_f(u, b):
    return gate_act(u, b), (u, b)


def _gate_act_b(res, d):
    u, b = res

    def body(i, u_ref, d_ref, b_ref, du_ref, db_ref):
        t = u_ref[...] + b_ref[...]
        du = d_ref[...] * _sigmoid(-t) / GLA_TEMP
        du_ref[...] = du
        _acc(i, db_ref, _colsum(du))

    du, db = _rows(body, "gate_act_bwd", [u, d], [b], [_sds(u.shape)], [_sds(b.shape)])
    return du, db


gate_act.defvjp(_gate_act_f, _gate_act_b)


@jax.custom_vjp
def fma(a, b, c, d):
    def body(i, a_ref, b_ref, c_ref, d_ref, o_ref):
        o_ref[...] = a_ref[...] * b_ref[...] + c_ref[...] * d_ref[...]
    return _rows(body, "fma_fwd", [a, b, c, d], [], [_sds(a.shape)], [])[0]


def _fma_f(a, b, c, d):
    return fma(a, b, c, d), (b, d)


def _fma_b(res, g):
    b, d = res

    def body(i, g_ref, b_ref, d_ref, da_ref, dc_ref):
        g = g_ref[...]
        da_ref[...] = g * b_ref[...]
        dc_ref[...] = g * d_ref[...]

    da, dc = _rows(body, "fma_bwd", [g, b, d], [], [_sds(g.shape), _sds(g.shape)], [])
    return da, jnp.zeros_like(b), dc, jnp.zeros_like(d)


fma.defvjp(_fma_f, _fma_b)


def _silu_rows(c):
    def body(i, c_ref, o_ref):
        v = c_ref[...]
        o_ref[...] = v * _sigmoid(v)
    return _rows(body, "silu", [c], [], [_sds(c.shape)], [])[0]


@jax.custom_vjp
def loss_op(y, t):
    return _loss_fwd(y, t)[0]


def _loss_fwd(y, t):
    inv = 1.0 / y.shape[1]

    def body(i, y_ref, t_ref, d_ref, l_ref):
        e = y_ref[...] - t_ref[...]
        d_ref[...] = e * inv
        _acc(i, l_ref, jnp.sum(_colsum(e * e), axis=1, keepdims=True) * (0.5 * inv))

    d, l = _rows(body, "loss_fwd", [y, t], [], [_sds(y.shape)], [_sds((1, 1))])
    return l, d


def _loss_f(y, t):
    l, d = _loss_fwd(y, t)
    return l, d


def _loss_b(d, g):
    return d * g, jnp.zeros_like(d)


loss_op.defvjp(_loss_f, _loss_b)


def _conv_terms(cc, cx, rows, n):
    u = cc * cx
    up = jnp.where(rows == 0, 0.0, pltpu.roll(u, 1, 0))
    un = jnp.where(rows == n - 1, 0.0, pltpu.roll(u, n - 1, 0))
    return u, up, un


CONV_COLS = 128


def _conv_specs(s, n_in):
    blk = pl.BlockSpec((s, CONV_COLS), lambda j: (0, j))
    wblk = pl.BlockSpec((8, CONV_COLS), lambda j: (0, j))
    return [blk] * n_in + [wblk], blk, wblk


@jax.custom_vjp
def conv_op(cb, cc, cx, w):
    s, ch = cb.shape

    def body(cb_ref, cc_ref, cx_ref, w_ref, o_ref):
        rows = lax.broadcasted_iota(jnp.int32, (s, CONV_COLS), 0)
        u, up, un = _conv_terms(cc_ref[...], cx_ref[...], rows, s)
        conv = up * w_ref[0:1, :] + u * w_ref[1:2, :] + un * w_ref[2:3, :]
        o_ref[...] = cb_ref[...] * conv

    in_specs, blk, _ = _conv_specs(s, 3)
    return pl.pallas_call(
        body, name="conv_fwd", grid=(ch // CONV_COLS,), in_specs=in_specs, out_specs=blk,
        out_shape=_sds(cb.shape), compiler_params=_params(("parallel",)),
    )(cb, cc, cx, w)


def _conv_f(cb, cc, cx, w):
    return conv_op(cb, cc, cx, w), (cb, cc, cx, w)


def _conv_b(res, d):
    cb, cc, cx, w = res
    s, ch = cb.shape

    def body(cb_ref, cc_ref, cx_ref, d_ref, w_ref, dcb_ref, dcc_ref, dcx_ref, dw_ref):
        rows = lax.broadcasted_iota(jnp.int32, (s, CONV_COLS), 0)
        cc_v = cc_ref[...]
        cx_v = cx_ref[...]
        u, up, un = _conv_terms(cc_v, cx_v, rows, s)
        w0, w1, w2 = w_ref[0:1, :], w_ref[1:2, :], w_ref[2:3, :]
        dv = d_ref[...]
        dcb_ref[...] = dv * (up * w0 + u * w1 + un * w2)
        dconv = dv * cb_ref[...]
        d_next = jnp.where(rows == s - 1, 0.0, pltpu.roll(dconv, s - 1, 0))
        d_prev = jnp.where(rows == 0, 0.0, pltpu.roll(dconv, 1, 0))
        du = w0 * d_next + w1 * dconv + w2 * d_prev
        dcc_ref[...] = du * cx_v
        dcx_ref[...] = du * cc_v
        dw_ref[...] = jnp.zeros_like(dw_ref)
        dw_ref[0:1, :] = _colsum(dconv * up)
        dw_ref[1:2, :] = _colsum(dconv * u)
        dw_ref[2:3, :] = _colsum(dconv * un)

    in_specs, blk, wblk = _conv_specs(s, 4)
    v = _sds(cb.shape)
    return tuple(pl.pallas_call(
        body, name="conv_bwd", grid=(ch // CONV_COLS,), in_specs=in_specs, out_specs=[blk, blk, blk, wblk],
        out_shape=[v, v, v, _sds(w.shape)], compiler_params=_params(("parallel",)),
    )(cb, cc, cx, d, w))


conv_op.defvjp(_conv_f, _conv_b)


def _gla_masks(rev):
    c = GLA_CHUNK
    row = lax.broadcasted_iota(jnp.int32, (c, c), 0)
    col = lax.broadcasted_iota(jnp.int32, (c, c), 1)
    mask = (row < col) if rev else (row >= col)
    return rev, mask


def _chunk_cumsum(g, rev):
    c = g.shape[0]
    row = lax.broadcasted_iota(jnp.int32, g.shape, 0)
    b = g
    s = 1
    while s < c:
        if rev:
            b = b + jnp.where(row < c - s, pltpu.roll(b, c - s, 0), 0.0)
        else:
            b = b + jnp.where(row >= s, pltpu.roll(b, s, 0), 0.0)
        s *= 2
    return b


GLA_UNROLL = 4


def _gla_rows(n):
    return pl.ds(pl.multiple_of(n * GLA_CHUNK, GLA_CHUNK), GLA_CHUNK)


def _gla_scan(s_ref, bt_ref, st_ref, n_chunks, descending):
    st_ref[...] = jnp.zeros_like(st_ref)

    def step(i, carry):
        n = (n_chunks - 1 - i) if descending else i
        own = s_ref[n]
        st = st_ref[...]
        s_ref[n] = st
        st_ref[...] = st * jnp.exp(bt_ref[n]) + own
        return carry

    lax.fori_loop(0, n_chunks, step, 0)


GLA_PAIR = 2


def _gla_specs(s):
    dk, dv = GLA_DK, GLA_DV
    n_pairs = GLA_HEADS // GLA_PAIR
    blk_k = pl.BlockSpec((s, GLA_PAIR * dk), lambda p: (0, p))
    blk_gb = pl.BlockSpec((s, GLA_PAIR * dk), lambda p: (0, n_pairs + p))
    blk_v = pl.BlockSpec((s, GLA_PAIR * dv), lambda p: (0, p))
    return n_pairs, blk_k, blk_gb, blk_v


def _head_lanes(hh):
    lane = lax.broadcasted_iota(jnp.int32, (1, GLA_PAIR * GLA_DK), 1)
    return jnp.logical_and(lane >= hh * GLA_DK, lane < (hh + 1) * GLA_DK)


def _gla_fwd(q, k, v, la):
    s = q.shape[0]
    dk, dv = GLA_DK, GLA_DV
    pw = GLA_PAIR * dk
    n_chunks = s // GLA_CHUNK
    scale = GLA_DK ** -0.5

    def body(q_ref, k_ref, v_ref, gf_ref, gb_ref, o_ref, sf_ref, sb_ref, bf_ref, bb_ref, btf_ref, btb_ref, st_ref):
        masks = [_gla_masks(rev) for rev in (False, True)]
        dirs = ((False, gf_ref, sf_ref, bf_ref, btf_ref), (True, gb_ref, sb_ref, bb_ref, btb_ref))

        def decays(n, carry):
            rows = _gla_rows(n)
            for rev, g_ref, _, b_ref, bt_ref in dirs:
                g = g_ref[rows, :]
                b_ref[rows, :] = _chunk_cumsum(g, rev)
                bt_ref[n] = _colsum(g)
            return carry

        lax.fori_loop(0, n_chunks, decays, 0, unroll=GLA_UNROLL)
        for hh in range(GLA_PAIR):
            m = _head_lanes(hh)
            vl = slice(hh * dv, (hh + 1) * dv)

            def prepare(n, carry, m=m, vl=vl):
                rows = _gla_rows(n)
                kk = k_ref[rows, :]
                vb = v_ref[rows, vl].astype(BF16)
                for rev, _, s_ref, b_ref, bt_ref in dirs:
                    ke = jnp.where(m, kk * jnp.exp(bt_ref[n] - b_ref[rows, :]), 0.0).astype(BF16)
                    s_ref[n] = _dot(vb, ke, 0, 0)
                return carry

            lax.fori_loop(0, n_chunks, prepare, 0, unroll=GLA_UNROLL)
            for rev, _, s_ref, _, bt_ref in dirs:
                _gla_scan(s_ref, bt_ref, st_ref, n_chunks, descending=rev)

            def emit(n, carry, m=m, vl=vl):
                rows = _gla_rows(n)
                qs = q_ref[rows, :] * scale
                kk = k_ref[rows, :]
                vb = v_ref[rows, vl].astype(BF16)
                o = None
                for (rev, _, s_ref, b_ref, _), (_, mask) in zip(dirs, masks):
                    b = b_ref[rows, :]
                    qd = jnp.where(m, qs * jnp.exp(b), 0.0).astype(BF16)
                    ki = jnp.where(m, kk * jnp.exp(-b), 0.0).astype(BF16)
                    a = jnp.where(mask, _dot(qd, ki, 1, 1), 0.0).astype(BF16)
                    od = _dot(a, vb, 1, 0) + _dot(qd, s_ref[n].astype(BF16), 1, 1)
                    o = od if o is None else o + od
                o_ref[rows, vl] = o
                return carry

            lax.fori_loop(0, n_chunks, emit, 0, unroll=GLA_UNROLL)

    n_pairs, blk_k, blk_gb, blk_v = _gla_specs(s)
    state = pltpu.VMEM((n_chunks, dv, pw), F32)
    scratch = [state, state, pltpu.VMEM((s, pw), F32), pltpu.VMEM((s, pw), F32), pltpu.VMEM((n_chunks, 1, pw), F32),
               pltpu.VMEM((n_chunks, 1, pw), F32), pltpu.VMEM((dv, pw), F32)]
    return pl.pallas_call(
        body, name="gla_fwd", grid=(n_pairs,), in_specs=[blk_k, blk_k, blk_v, blk_k, blk_gb],
        out_specs=blk_v, out_shape=_sds(v.shape), scratch_shapes=scratch,
        compiler_params=_params(("parallel",)),
    )(q, k, v, la, la)


def _gla_bwd(q, k, v, la, do):
    s = q.shape[0]
    dk, dv = GLA_DK, GLA_DV
    pw = GLA_PAIR * dk
    c = GLA_CHUNK
    n_chunks = s // c
    scale = GLA_DK ** -0.5

    def body(q_ref, k_ref, v_ref, gf_ref, gb_ref, do_ref, dq_ref, dk_ref, dv_ref, dgf_ref, dgb_ref,
             sf_ref, sb_ref, bf_ref, bb_ref, btf_ref, btb_ref, dsf_ref, dsb_ref, st_ref):
        masks = [_gla_masks(rev) for rev in (False, True)]
        rowc = lax.broadcasted_iota(jnp.int32, (c, pw), 0)
        dirs = ((False, gf_ref, sf_ref, bf_ref, btf_ref, dsf_ref, dgf_ref),
                (True, gb_ref, sb_ref, bb_ref, btb_ref, dsb_ref, dgb_ref))

        def decays(n, carry):
            rows = _gla_rows(n)
            for rev, g_ref, _, b_ref, bt_ref, _, _ in dirs:
                g = g_ref[rows, :]
                b_ref[rows, :] = _chunk_cumsum(g, rev)
                bt_ref[n] = _colsum(g)
            return carry

        lax.fori_loop(0, n_chunks, decays, 0, unroll=GLA_UNROLL)
        for hh in range(GLA_PAIR):
            m = _head_lanes(hh)
            vl = slice(hh * dv, (hh + 1) * dv)

            def prepare(n, carry, m=m, vl=vl):
                rows = _gla_rows(n)
                qs = q_ref[rows, :] * scale
                kk = k_ref[rows, :]
                vb = v_ref[rows, vl].astype(BF16)
                do_b = do_ref[rows, vl].astype(BF16)
                for rev, _, s_ref, b_ref, bt_ref, ds_ref, _ in dirs:
                    b = b_ref[rows, :]
                    ke = jnp.where(m, kk * jnp.exp(bt_ref[n] - b), 0.0).astype(BF16)
                    qd = jnp.where(m, qs * jnp.exp(b), 0.0).astype(BF16)
                    s_ref[n] = _dot(vb, ke, 0, 0)
                    ds_ref[n] = _dot(do_b, qd, 0, 0)
                return carry

            lax.fori_loop(0, n_chunks, prepare, 0, unroll=GLA_UNROLL)
            for rev, _, s_ref, _, bt_ref, ds_ref, _ in dirs:
                _gla_scan(s_ref, bt_ref, st_ref, n_chunks, descending=rev)
                _gla_scan(ds_ref, bt_ref, st_ref, n_chunks, descending=not rev)

            def emit(n, carry, m=m, vl=vl, first=(hh == 0)):
                rows = _gla_rows(n)
                qs = q_ref[rows, :] * scale
                kk = k_ref[rows, :]
                vb = v_ref[rows, vl].astype(BF16)
                do_b = do_ref[rows, vl].astype(BF16)
                dq = dkk = dvv = None
                for (rev, _, s_ref, b_ref, bt_ref, ds_ref, dg_ref), (_, mask) in zip(dirs, masks):
                    b = b_ref[rows, :]
                    bt = bt_ref[n]
                    eb = jnp.where(m, jnp.exp(b), 0.0)
                    enb = jnp.where(m, jnp.exp(-b), 0.0)
                    etb = jnp.where(m, jnp.exp(bt - b), 0.0)
                    ebt = jnp.exp(bt)
                    qd = qs * eb
                    ki = kk * enb
                    ke = kk * etb
                    qd_b, ki_b, ke_b = qd.astype(BF16), ki.astype(BF16), ke.astype(BF16)
                    st = s_ref[n]
                    dst = ds_ref[n]
                    dst_b = dst.astype(BF16)
                    a = jnp.where(mask, _dot(qd_b, ki_b, 1, 1), 0.0).astype(BF16)
                    da = jnp.where(mask, _dot(do_b, vb, 1, 1), 0.0).astype(BF16)
                    dv_d = _dot(a, do_b, 0, 0) + _dot(ke_b, dst_b, 1, 1)
                    dqd = _dot(da, ki_b, 1, 0) + _dot(do_b, st.astype(BF16), 1, 0)
                    dki = _dot(da, qd_b, 0, 0)
                    dke = _dot(vb, dst_b, 1, 0)
                    dbt = _colsum(st * dst) * ebt + _colsum(dke * ke)
                    db = dqd * qd - dki * ki - dke * ke
                    db = db + jnp.where(rowc == (0 if rev else c - 1), dbt, 0.0)
                    dg = _chunk_cumsum(db, not rev)
                    if first:
                        dg_ref[rows, :] = dg
                    else:
                        dg_ref[rows, :] += dg
                    dq_d = dqd * eb * scale
                    dk_d = dki * enb + dke * etb
                    dq = dq_d if dq is None else dq + dq_d
                    dkk = dk_d if dkk is None else dkk + dk_d
                    dvv = dv_d if dvv is None else dvv + dv_d
                if first:
                    dq_ref[rows, :] = dq
                    dk_ref[rows, :] = dkk
                else:
                    dq_ref[rows, :] += dq
                    dk_ref[rows, :] += dkk
                dv_ref[rows, vl] = dvv
                return carry

            lax.fori_loop(0, n_chunks, emit, 0, unroll=2)

    n_pairs, blk_k, blk_gb, blk_v = _gla_specs(s)
    vk, vv = _sds(q.shape), _sds(v.shape)
    state = pltpu.VMEM((n_chunks, dv, pw), F32)
    scratch = [state, state, pltpu.VMEM((s, pw), F32), pltpu.VMEM((s, pw), F32), pltpu.VMEM((n_chunks, 1, pw), F32),
               pltpu.VMEM((n_chunks, 1, pw), F32), state, state, pltpu.VMEM((dv, pw), F32)]
    return pl.pallas_call(
        body, name="gla_bwd", grid=(n_pairs,), in_specs=[blk_k, blk_k, blk_v, blk_k, blk_gb, blk_v],
        out_specs=[blk_k, blk_k, blk_v, blk_k, blk_k], out_shape=[vk, vk, vv, vk, vk],
        scratch_shapes=scratch, compiler_params=_params(("parallel",)),
    )(q, k, v, la, la, do)


@jax.custom_vjp
def gla(q, k, v, la):
    return _gla_fwd(q, k, v, la)


def _gla_f(q, k, v, la):
    return _gla_fwd(q, k, v, la), (q, k, v, la)


def _gla_b(res, do):
    dq, dk, dv, dgf, dgb = _gla_bwd(*res, do)
    return dq, dk, dv, jnp.concatenate([dgf, dgb], axis=1)


gla.defvjp(_gla_f, _gla_b)


ATTN_TQ = 256
HEAD_LANES = 128


def _attn_blocks(s, tq):
    per_q = pl.BlockSpec((tq, HEAD_LANES), lambda h, j: (j, h))
    k_nope = pl.BlockSpec((s, HEAD_LANES), lambda h, j: (0, 2 * h))
    v_blk = pl.BlockSpec((s, HEAD_LANES), lambda h, j: (0, 2 * h + 1))
    k_rope = pl.BlockSpec((s, HEAD_LANES), lambda h, j: (0, 0))
    lse = pl.BlockSpec((1, tq, 1), lambda h, j: (h, j, 0))
    return per_q, k_nope, v_blk, k_rope, lse


def _attn_fwd(qn, qr, kv, kr):
    s = qn.shape[0]
    tq = min(ATTN_TQ, s)
    scale = (MLA_NOPE + MLA_ROPE) ** -0.5

    def body(qn_ref, qr_ref, kn_ref, v_ref, kr_ref, o_ref, lse_ref):
        q = jnp.concatenate([qn_ref[...], qr_ref[...]], axis=1)
        k = jnp.concatenate([kn_ref[...], kr_ref[...]], axis=1)
        sc = _dot(q, k, 1, 1) * scale
        m = jnp.max(sc, axis=-1, keepdims=True)
        p = jnp.exp(sc - m)
        l = jnp.sum(p, axis=-1, keepdims=True)
        p = p * (1.0 / l)
        o_ref[...] = _dot(p.astype(BF16), v_ref[...], 1, 0)
        lse_ref[0] = m + jnp.log(l)

    per_q, k_nope, v_blk, k_rope, lse = _attn_blocks(s, tq)
    return pl.pallas_call(
        body, name="attn_fwd", grid=(MLA_HEADS, s // tq), in_specs=[per_q, per_q, k_nope, v_blk, k_rope],
        out_specs=[per_q, lse], out_shape=[_sds(qn.shape), _sds((MLA_HEADS, s, 1))],
        compiler_params=_params(("parallel", "parallel")),
    )(qn, qr, kv, kv, kr)


def _attn_bwd(qn, qr, kv, kr, o, lse, do):
    s = qn.shape[0]
    tq = min(ATTN_TQ, s)
    n_q = s // tq
    scale = (MLA_NOPE + MLA_ROPE) ** -0.5

    def body(qn_ref, qr_ref, kn_ref, v_ref, kr_ref, o_ref, lse_ref, do_ref, dqn_ref, dqr_ref, dkv_ref, dkr_ref,
             dk_acc, dv_acc, dkr_acc):
        h, j = pl.program_id(0), pl.program_id(1)
        q = jnp.concatenate([qn_ref[...], qr_ref[...]], axis=1)
        k = jnp.concatenate([kn_ref[...], kr_ref[...]], axis=1)
        do = do_ref[...]
        do_b = do.astype(BF16)
        p = jnp.exp(_dot(q, k, 1, 1) * scale - lse_ref[0])
        dp = _dot(do_b, v_ref[...], 1, 1)
        delta = jnp.sum(do * o_ref[...], axis=-1, keepdims=True)
        ds = (p * (dp - delta) * scale).astype(BF16)
        dq = _dot(ds, k, 1, 0)
        dqn_ref[...] = dq[:, :HEAD_LANES].astype(BF16)
        dqr_ref[...] = dq[:, HEAD_LANES:].astype(BF16)
        dk = _dot(ds, q, 0, 0)
        _acc(j, dk_acc, dk[:, :HEAD_LANES])
        _acc(j, dv_acc, _dot(p.astype(BF16), do_b, 0, 0))
        _acc(jnp.where(jnp.logical_and(h == 0, j == 0), 0, 1), dkr_acc, dk[:, HEAD_LANES:])

        @pl.when(j == n_q - 1)
        def _():
            dkv_ref[:, 0:HEAD_LANES] = dk_acc[...].astype(BF16)
            dkv_ref[:, HEAD_LANES:2 * HEAD_LANES] = dv_acc[...].astype(BF16)

        @pl.when(jnp.logical_and(h == MLA_HEADS - 1, j == n_q - 1))
        def _():
            dkr_ref[...] = dkr_acc[...].astype(BF16)

    per_q, k_nope, v_blk, k_rope, lse_blk = _attn_blocks(s, tq)
    dkv_blk = pl.BlockSpec((s, 2 * HEAD_LANES), lambda h, j: (0, h))
    acc = pltpu.VMEM((s, HEAD_LANES), F32)
    return pl.pallas_call(
        body, name="attn_bwd", grid=(MLA_HEADS, n_q),
        in_specs=[per_q, per_q, k_nope, v_blk, k_rope, per_q, lse_blk, per_q],
        out_specs=[per_q, per_q, dkv_blk, k_rope],
        out_shape=[_sds(qn.shape, BF16), _sds(qr.shape, BF16), _sds(kv.shape, BF16), _sds(kr.shape, BF16)],
        scratch_shapes=[acc, acc, acc], compiler_params=_params(("arbitrary", "arbitrary")),
    )(qn, qr, kv, kv, kr, o, lse, do)


@jax.custom_vjp
def attn(qn, qr, kv, kr):
    return _attn_fwd(qn, qr, kv, kr)[0]


def _attn_f(qn, qr, kv, kr):
    o, lse = _attn_fwd(qn, qr, kv, kr)
    return o, (qn, qr, kv, kr, o, lse)


def _attn_b(res, do):
    return tuple(_attn_bwd(*res, do))


attn.defvjp(_attn_f, _attn_b)


@jax.custom_vjp
def split_proj(proj):
    out, at = [], 0
    for _, _, _, wp in PROJ_SEGS:
        out.append(proj[:, at:at + wp])
        at += wp
    return tuple(out)


def _split_f(proj):
    return split_proj(proj), None


def _split_b(_, gs):
    return (jnp.concatenate(gs, axis=1),)


split_proj.defvjp(_split_f, _split_b)


def _tile2d(rows, width, limit=BLOCK_BYTES):
    fits = [t for t in range(16, rows + 1, 16) if rows % t == 0 and t * width * 4 <= limit]
    if fits and (fits[-1] >= 64 or fits[-1] == rows):
        return fits[-1], width
    if rows * width * 4 <= limit:
        return rows, width
    cols = [t for t in range(128, width + 1, 128) if width % t == 0 and rows * t * 4 <= limit]
    return (rows, cols[-1]) if cols else (rows, width)


def _add_pair(stacked, theirs, c_idx):
    g, r, w = theirs.shape
    tr, tc = _tile2d(r, w)

    def body(c_ref, a_ref, b_ref, o_ref):
        o_ref[0] = (a_ref[0, 0].astype(F32) + b_ref[0].astype(F32)).astype(BF16)

    blk = pl.BlockSpec((1, tr, tc), lambda k, i, j, c: (k, i, j))
    spec = pltpu.PrefetchScalarGridSpec(
        num_scalar_prefetch=1, grid=(g, r // tr, w // tc),
        in_specs=[pl.BlockSpec((1, 1, tr, tc), lambda k, i, j, c: (c[0], k, i, j)), blk], out_specs=blk)
    return pl.pallas_call(body, name="add_pair", grid_spec=spec, out_shape=_sds(theirs.shape, BF16),
                          compiler_params=_params(("parallel", "parallel", "parallel")))(c_idx, stacked, theirs)


def _add_chips(pair, landed, chip_idx):
    _, r, w = pair.shape
    tr, tc = _tile2d(r, w)

    def body(c_ref, p_ref, l0_ref, l1_ref, l2_ref, o_ref):
        o_ref[...] = ((p_ref[0].astype(F32) + l0_ref[0].astype(F32)) + l1_ref[0].astype(F32)) + l2_ref[0].astype(F32)

    specs = [pl.BlockSpec((1, tr, tc), lambda i, j, c: (c[0], i, j))]
    specs += [pl.BlockSpec((1, tr, tc), functools.partial(lambda i, j, c, k: (k, i, j), k=k)) for k in range(N_CHIPS - 1)]
    spec = pltpu.PrefetchScalarGridSpec(num_scalar_prefetch=1, grid=(r // tr, w // tc), in_specs=specs,
                                        out_specs=pl.BlockSpec((tr, tc), lambda i, j, c: (i, j)))
    return pl.pallas_call(body, name="add_chips", grid_spec=spec, out_shape=_sds((r, w)),
                          compiler_params=_params(("parallel", "parallel")))(chip_idx, pair, landed, landed, landed)


def _sum_devices(g):
    n = g.shape[2]

    def body(g_ref, o_ref):
        t = g_ref[0]
        for j in range(1, N_DEV):
            t = t + g_ref[j]
        o_ref[...] = t

    return pl.pallas_call(body, name="sum_devices", out_shape=_sds((1, n)), compiler_params=_params())(g)


def _adamw(w, g, m, v):
    shp = w.shape
    shp3 = (1, 1, shp[0]) if len(shp) == 1 else (-1,) + tuple(shp[-2:])
    w3, g3, m3, v3 = (t.reshape(shp3) for t in (w, g, m, v))
    c1 = 1.0 - ADAM_B1 ** ADAM_STEP
    c2 = 1.0 - ADAM_B2 ** ADAM_STEP

    def body(w_ref, g_ref, m_ref, v_ref, d_ref, mo_ref, vo_ref):
        gv = g_ref[...]
        mn = ADAM_B1 * m_ref[...] + (1.0 - ADAM_B1) * gv
        vn = ADAM_B2 * v_ref[...] + (1.0 - ADAM_B2) * (gv * gv)
        d_ref[...] = -ADAM_LR * ((mn / c1) / (jnp.sqrt(vn / c2) + ADAM_EPS) + ADAM_WD * w_ref[...])
        mo_ref[...] = mn
        vo_ref[...] = vn

    nl, r, wd = w3.shape
    tr, tc = _tile2d(r, wd, BLOCK_BYTES // 2)
    blk = pl.BlockSpec((1, tr, tc), lambda l, i, j: (l, i, j))
    s3 = _sds(w3.shape)
    d, mn, vn = pl.pallas_call(
        body, name="adamw", grid=(nl, r // tr, wd // tc), in_specs=[blk] * 4, out_specs=[blk] * 3,
        out_shape=[s3, s3, s3], compiler_params=_params(("parallel", "parallel", "parallel")),
    )(w3, g3, m3, v3)
    return d.reshape(shp), mn.reshape(shp), vn.reshape(shp)


ANY = pl.BlockSpec(memory_space=pl.ANY)
PIECE_BYTES = 1 << 20


def _place():
    return lax.axis_index("x"), lax.axis_index("y"), lax.axis_index("c")


def _pieces(shape, itemsize):
    if len(shape) >= 3:
        return [(i,) + p for i in range(shape[0]) for p in _pieces(shape[1:], itemsize)]
    rows = shape[0]
    row_bytes = itemsize
    for dsz in shape[1:]:
        row_bytes *= dsz
    k = 1
    while rows % (2 * k) == 0 and (rows // (2 * k)) % 16 == 0 and (rows // k) * row_bytes > PIECE_BYTES:
        k *= 2
    step = rows // k
    return [(pl.ds(j * step, step),) for j in range(k)]


def _split_start(make, src, dst, pieces):
    for p in pieces:
        make(src.at[p], dst.at[p]).start()
    return make(src, dst)


def _comm_call(body, name, arrs, out_shapes, n_remote, n_local):
    return pl.pallas_call(
        body, name=name, in_specs=[ANY] * len(arrs), out_specs=[ANY] * len(out_shapes), out_shape=out_shapes,
        scratch_shapes=[pltpu.SemaphoreType.DMA((n_remote,)), pltpu.SemaphoreType.DMA((n_remote,)),
                        pltpu.SemaphoreType.DMA((n_local,))],
    )(*arrs)


def all_gather8(arrs, name):
    n = len(arrs)
    pieces = [_pieces(a.shape, a.dtype.itemsize) for a in arrs]

    def body(*refs):
        ins, outs = refs[:n], refs[n:2 * n]
        send, recv, _ = refs[2 * n:]
        x, y, c = _place()
        me, sib = (x, y, c), (x, y, 1 - c)
        chips = [(1 - x, y), (x, 1 - y), (1 - x, 1 - y)]

        def slot(p):
            return 4 * p[0] + 2 * p[1] + p[2]

        def maker(t, k, to):
            def make(s, d):
                return pltpu.make_async_remote_copy(src_ref=s, dst_ref=d, send_sem=send.at[7 * t + k],
                                                    recv_sem=recv.at[7 * t + k], device_id=to, device_id_type=MESH)
            return make

        def landing(t, k, block):
            dst = outs[t].at[slot(block)]
            return maker(t, k, me)(dst, dst)

        sent = []
        for t in range(n):
            dst = outs[t].at[slot(me)]
            sent.append(_split_start(maker(t, 0, sib), ins[t], dst, pieces[t]))
            for j, chip in enumerate(chips):
                sent.append(_split_start(maker(t, 1 + j, (*chip, c)), ins[t], dst, pieces[t]))
        for j, chip in enumerate(chips):
            for t in range(n):
                landing(t, 1 + j, (*chip, c)).wait_recv()
                blk = outs[t].at[slot((*chip, c))]
                sent.append(_split_start(maker(t, 4 + j, sib), blk, blk, pieces[t]))
        for t in range(n):
            landing(t, 0, sib).wait_recv()
            for j, chip in enumerate(chips):
                landing(t, 4 + j, (*chip, 1 - c)).wait_recv()
        for cp in sent:
            cp.wait_send()

    outs = [_sds((N_DEV,) + a.shape, a.dtype) for a in arrs]
    got = _comm_call(body, name, arrs, outs, 7 * n, 1)
    x, y, c = _place()
    return [lax.dynamic_update_index_in_dim(g, a, 4 * x + 2 * y + c, 0) for g, a in zip(got, arrs)]


def sibling_send(arrs, name):
    n = len(arrs)
    pieces = [_pieces(a.shape[1:], a.dtype.itemsize) for a in arrs]

    def body(*refs):
        ins, theirs = refs[:n], refs[n:2 * n]
        send, recv, _ = refs[2 * n:]
        x, y, c = _place()
        rem = []
        for t in range(n):
            def make(s, d, t=t):
                return pltpu.make_async_remote_copy(src_ref=s, dst_ref=d, send_sem=send.at[t], recv_sem=recv.at[t],
                                                    device_id=(x, y, 1 - c), device_id_type=MESH)
            rem.append(_split_start(make, ins[t].at[1 - c], theirs[t], pieces[t]))
        for cp in rem:
            cp.wait_recv()
        for cp in rem:
            cp.wait_send()

    outs = [_sds(a.shape[1:], a.dtype) for a in arrs]
    return _comm_call(body, name, arrs, outs, n, 1)


def exchange_chips(arrs, name):
    n = len(arrs)
    pieces = [_pieces(a.shape[1:], a.dtype.itemsize) for a in arrs]

    def body(*refs):
        ins, outs = refs[:n], refs[n:2 * n]
        send, recv, _ = refs[2 * n:]
        x, y, c = _place()
        peers = [(1 - x, y), (x, 1 - y), (1 - x, 1 - y)]
        rem = []
        for t in range(n):
            for j, (px, py) in enumerate(peers):
                def make(s, d, t=t, j=j, px=px, py=py):
                    return pltpu.make_async_remote_copy(
                        src_ref=s, dst_ref=d, send_sem=send.at[3 * t + j], recv_sem=recv.at[3 * t + j],
                        device_id=(px, py, c), device_id_type=MESH)
                rem.append(_split_start(make, ins[t].at[2 * px + py], outs[t].at[j], pieces[t]))
        for cp in rem:
            cp.wait_recv()
        for cp in rem:
            cp.wait_send()

    outs = [_sds((N_CHIPS - 1,) + a.shape[1:], a.dtype) for a in arrs]
    return _comm_call(body, name, arrs, outs, 3 * n, 1)


def sibling_swap(arrs, name):
    n = len(arrs)
    pieces = [_pieces(a.shape, a.dtype.itemsize) for a in arrs]

    def body(*refs):
        ins, outs = refs[:n], refs[n:2 * n]
        send, recv, _ = refs[2 * n:]
        x, y, c = _place()
        rem = []
        for t in range(n):
            def make(s, d, t=t):
                return pltpu.make_async_remote_copy(src_ref=s, dst_ref=d, send_sem=send.at[t], recv_sem=recv.at[t],
                                                    device_id=(x, y, 1 - c), device_id_type=MESH)
            rem.append(_split_start(make, ins[t], outs[t], pieces[t]))
        for cp in rem:
            cp.wait_recv()
        for cp in rem:
            cp.wait_send()

    outs = [_sds(a.shape, a.dtype) for a in arrs]
    return _comm_call(body, name, arrs, outs, n, 1)


@jax.custom_vjp
def _build_w_in(w4):
    full = w4.reshape(-1, w4.shape[-1])
    parts = []
    for _, start, width, wp in PROJ_SEGS:
        if width:
            parts.append(full[start:start + width])
        if wp > width:
            parts.append(jnp.zeros((wp - width, full.shape[1]), full.dtype))
    return jnp.concatenate(parts, axis=0)


def _build_w_in_f(w4):
    return _build_w_in(w4), None


def _build_w_in_b(_, g):
    parts, at = [], 0
    for _, _, width, wp in PROJ_SEGS:
        if width:
            parts.append(g[at:at + width])
        at += wp
    return (jnp.concatenate(parts, axis=0).reshape(N_CHIPS, -1, g.shape[1]),)


_build_w_in.defvjp(_build_w_in_f, _build_w_in_b)


def _split_w_uq(w):
    w3 = w.reshape(w.shape[0], MLA_HEADS, MLA_NOPE + MLA_ROPE)
    return w3[:, :, :MLA_NOPE].reshape(w.shape[0], -1), w3[:, :, MLA_NOPE:].reshape(w.shape[0], -1)


def _swap_halves(t, width):
    t3 = t.reshape(t.shape[0], -1, 2, width // 2)
    return jnp.concatenate([t3[:, :, 1:], t3[:, :, :1]], axis=2).reshape(t.shape)


def _pad_heads(t, width):
    t3 = t.reshape(t.shape[0], -1, width)
    t3 = jnp.pad(t3, ((0, 0), (0, 0), (0, HEAD_LANES - width)))
    return t3.reshape(t.shape[0], -1).astype(BF16)


def _layer(xh, mod, big, small, rope_q, rope_k):
    d = D_MODEL
    shift, scale, gate = mod[None, 0:d], mod[None, d:2 * d], mod[None, 2 * d:3 * d]
    w_al = _build_w_in(big["w_in"])
    proj = mod_mm(xh, small["norm_g"][None], scale, shift, w_al)
    gq, gk, gv, glr, mq, mkv, mkr, cb, cc, cx, _, z = split_proj(proj)

    rk = GLA_RANK
    hk = GLA_HEADS * GLA_DK
    wg = jnp.zeros((128, 2 * hk), F32)
    wg = wg.at[0:rk, 0:hk].set(small["gla_wg_f"]).at[rk:2 * rk, hk:].set(small["gla_wg_b"])
    bg = jnp.concatenate([small["gla_bg_f"], small["gla_bg_b"]])[None]
    la = gate_act(mm(glr, wg), bg)
    o_gla = rmsnorm(gla(gq, gk, gv, la), small["gla_norm_g"][None])

    cq = rmsnorm(mq, small["mla_q_norm_g"][None])
    w_nope, w_rope = _split_w_uq(jnp.concatenate([big["w_uq"][j] for j in range(N_CHIPS)], axis=1))
    qn = mm16(cq, w_nope)
    qr = mm(cq, w_rope)
    qr = fma(qr, rope_q[0], _swap_halves(qr, MLA_ROPE), rope_q[1])
    ckv = rmsnorm(mkv, small["mla_kv_norm_g"][None])
    kv = mm16(ckv, jnp.concatenate([big["w_ukv"][j] for j in range(N_CHIPS)], axis=1))
    kr = mkr[:, :MLA_ROPE]
    kr = fma(kr, rope_k[0], _swap_halves(kr, MLA_ROPE), rope_k[1])
    o_mla = rmsnorm(attn(qn, _pad_heads(qr, MLA_ROPE), kv, _pad_heads(kr, MLA_ROPE)), small["mla_out_g"][None])

    cw = jnp.concatenate([small["conv_w"], jnp.zeros((5, CONV_CH), F32)], axis=0)
    o_conv = rmsnorm(conv_op(cb, cc, cx, cw), small["conv_out_g"][None])

    o = jnp.concatenate([o_gla, o_mla, o_conv], axis=1)
    w_out = big["w_out"].reshape(d, d)
    return out_block(o, z, w_out, xh, gate)


def _loss_fn(xh, mods, bigs, smalls, final_g, target, rope_q, rope_k):
    h = xh
    for l in range(DEPTH):
        h = _layer(h, mods[l], bigs[l], smalls[l], rope_q, rope_k)
    return loss_op(rmsnorm(h, final_g[None]), target)[0, 0]


SMALL_REPL = ("norm_g", "gla_bg_f", "gla_bg_b", "gla_norm_g", "mla_q_norm_g", "mla_kv_norm_g", "mla_out_g",
              "conv_out_g")
SMALL_SHARDED = ("gla_wg_f", "gla_wg_b", "conv_w")
BIG = ("w_in", "w_out", "w_uq", "w_ukv")


def kernel(x, c, positions, ada_w, ada_b, norm_g, w_in, gla_wg_f, gla_bg_f, gla_wg_b, gla_bg_b, gla_norm_g, mla_q_norm_g, mla_kv_norm_g, mla_w_uq, mla_w_ukv, mla_out_g, conv_w, conv_out_g, w_out, final_g, loss_target, m_ada_w, m_ada_b, m_norm_g, m_w_in, m_gla_wg_f, m_gla_bg_f, m_gla_wg_b, m_gla_bg_b, m_gla_norm_g, m_mla_q_norm_g, m_mla_kv_norm_g, m_mla_w_uq, m_mla_w_ukv, m_mla_out_g, m_conv_w, m_conv_out_g, m_w_out, m_final_g, v_ada_w, v_ada_b, v_norm_g, v_w_in, v_gla_wg_f, v_gla_bg_f, v_gla_wg_b, v_gla_bg_b, v_gla_norm_g, v_mla_q_norm_g, v_mla_kv_norm_g, v_mla_w_uq, v_mla_w_ukv, v_mla_out_g, v_conv_w, v_conv_out_g, v_w_out, v_final_g):
    xi, yi, ci = _place()
    chip = 2 * xi + yi
    dev = 2 * chip + ci
    s = x.shape[1]
    d = D_MODEL
    weights = dict(ada_w=ada_w, ada_b=ada_b, norm_g=norm_g, w_in=w_in, gla_wg_f=gla_wg_f, gla_bg_f=gla_bg_f,
                   gla_wg_b=gla_wg_b, gla_bg_b=gla_bg_b, gla_norm_g=gla_norm_g, mla_q_norm_g=mla_q_norm_g,
                   mla_kv_norm_g=mla_kv_norm_g, mla_w_uq=mla_w_uq, mla_w_ukv=mla_w_ukv, mla_out_g=mla_out_g,
                   conv_w=conv_w, conv_out_g=conv_out_g, w_out=w_out, final_g=final_g)
    m_in = dict(ada_w=m_ada_w, ada_b=m_ada_b, norm_g=m_norm_g, w_in=m_w_in, gla_wg_f=m_gla_wg_f, gla_bg_f=m_gla_bg_f,
                gla_wg_b=m_gla_wg_b, gla_bg_b=m_gla_bg_b, gla_norm_g=m_gla_norm_g, mla_q_norm_g=m_mla_q_norm_g,
                mla_kv_norm_g=m_mla_kv_norm_g, mla_w_uq=m_mla_w_uq, mla_w_ukv=m_mla_w_ukv, mla_out_g=m_mla_out_g,
                conv_w=m_conv_w, conv_out_g=m_conv_out_g, w_out=m_w_out, final_g=m_final_g)
    v_in = dict(ada_w=v_ada_w, ada_b=v_ada_b, norm_g=v_norm_g, w_in=v_w_in, gla_wg_f=v_gla_wg_f, gla_bg_f=v_gla_bg_f,
                gla_wg_b=v_gla_wg_b, gla_bg_b=v_gla_bg_b, gla_norm_g=v_gla_norm_g, mla_q_norm_g=v_mla_q_norm_g,
                mla_kv_norm_g=v_mla_kv_norm_g, mla_w_uq=v_mla_w_uq, mla_w_ukv=v_mla_w_ukv, mla_out_g=v_mla_out_g,
                conv_w=v_conv_w, conv_out_g=v_conv_out_g, w_out=v_w_out, final_g=v_final_g)

    def mine_bf16(w):
        return lax.dynamic_index_in_dim(w, ci, 0, keepdims=False).astype(BF16)

    g_c, g_in, g_out, g_uq, g_ukv, g_wgf, g_wgb, g_cw = all_gather8(
        [c, mine_bf16(jnp.swapaxes(w_in, 1, 2)), mine_bf16(w_out), mine_bf16(mla_w_uq), mine_bf16(mla_w_ukv),
         gla_wg_f, gla_wg_b, conv_w], "gather_weights")

    def by_layer(g):
        g4 = g.reshape((N_CHIPS, 2) + g.shape[1:])
        return [g4[:, l] for l in range(DEPTH)]

    bigs = [dict(w_in=a, w_out=b, w_uq=u, w_ukv=k)
            for a, b, u, k in zip(by_layer(g_in), by_layer(g_out), by_layer(g_uq), by_layer(g_ukv))]

    def unshard_cols(g):
        g4 = g[0::2]
        return g4.transpose(1, 2, 0, 3).reshape(g4.shape[1], g4.shape[2], -1)

    small_full = dict(gla_wg_f=unshard_cols(g_wgf), gla_wg_b=unshard_cols(g_wgb), conv_w=unshard_cols(g_cw))
    for nme in SMALL_REPL:
        small_full[nme] = weights[nme]
    smalls = [{nme: small_full[nme][l] for nme in SMALL_REPL + SMALL_SHARDED} for l in range(DEPTH)]

    c_act = _silu_rows(g_c[:, 0, :])
    c_act16 = jnp.concatenate([c_act, jnp.zeros_like(c_act)], axis=0)
    n_ada = ada_w.shape[2]
    parts = []
    for l in range(DEPTH):
        bias = lax.dynamic_slice_in_dim(ada_b[l], chip * n_ada, n_ada)[None]
        parts.append(_mm(c_act16, ada_w[l], bias=bias, name="ada_fwd"))
    g_mod, = all_gather8([jnp.stack(parts)], "gather_mod")
    mod_mine = lax.dynamic_index_in_dim(g_mod[0::2], dev, 2, keepdims=False)
    mods = mod_mine.transpose(1, 0, 2).reshape(DEPTH, 3 * d)

    inv_freq = ROPE_THETA ** (-jnp.arange(0, MLA_ROPE, 2, dtype=F32) / MLA_ROPE)
    ang = positions[0].astype(F32)[:, None] * inv_freq
    cos, sin = jnp.cos(ang), jnp.sin(ang)
    rope_k = (jnp.concatenate([cos, cos], axis=1), jnp.concatenate([-sin, sin], axis=1))
    rope_q = (jnp.tile(rope_k[0], (1, MLA_HEADS)), jnp.tile(rope_k[1], (1, MLA_HEADS)))

    loss_dev, (dx, dmods, dbigs, dsmalls, dfinal) = jax.value_and_grad(_loss_fn, argnums=(0, 1, 2, 3, 4))(
        x[0], mods, bigs, smalls, final_g, loss_target[0], rope_q, rope_k)
    loss = lax.psum(loss_dev, ("x", "y", "c"))

    pieces = [dmods.reshape(-1), dfinal]
    for nme in SMALL_REPL + SMALL_SHARDED:
        pieces.append(jnp.stack([dsmalls[l][nme] for l in range(DEPTH)]).reshape(-1))
    sizes = [p.shape[0] for p in pieces]
    flat = jnp.concatenate(pieces)
    padn = (-flat.shape[0]) % 128
    flat = jnp.concatenate([flat, jnp.zeros((padn,), F32)])[None]
    g_small, = all_gather8([flat], "gather_small_grads")
    total = _sum_devices(g_small)[0]
    offs, at = [], 0
    for n_el in sizes:
        offs.append(at)
        at += n_el

    def piece(i, shape):
        return total[offs[i]:offs[i] + sizes[i]].reshape(shape)

    grads = {"ada_b": piece(0, (DEPTH, 3 * d)), "final_g": piece(1, (d,))}
    for i, nme in enumerate(SMALL_REPL + SMALL_SHARDED):
        full = piece(2 + i, small_full[nme].shape)
        if nme in SMALL_SHARDED:
            ncol = weights[nme].shape[2]
            full = lax.dynamic_slice_in_dim(full, chip * ncol, ncol, axis=2)
        grads[nme] = full

    dmod_all = g_small[:, 0, :DEPTH * 3 * d].reshape(N_DEV, DEPTH, 3 * d)
    dmod_cols = lax.dynamic_slice_in_dim(dmod_all, chip * n_ada, n_ada, axis=2)
    g_ada = []
    for l in range(DEPTH):
        dm16 = jnp.concatenate([dmod_cols[:, l], jnp.zeros((N_DEV, n_ada), F32)], axis=0)
        g_ada.append(_mm(c_act16, dm16, ta=True, name="ada_bwd"))
    grads["ada_w"] = jnp.stack(g_ada)

    stacked = [jnp.stack([dbigs[l][nme] for l in range(DEPTH)]) for nme in BIG]
    c_idx = jnp.reshape(ci, (1,)).astype(jnp.int32)
    chip_idx = jnp.reshape(chip, (1,)).astype(jnp.int32)
    theirs = sibling_send(stacked, "reduce_sibling")
    pair = [_add_pair(a, b, c_idx) for a, b in zip(stacked, theirs)]
    landed = exchange_chips(pair, "reduce_chips")
    reduced = [_add_chips(p, q, chip_idx) for p, q in zip(pair, landed)]
    others = sibling_swap(reduced, "share_sibling")
    for nme, own, other in zip(("w_in", "w_out", "mla_w_uq", "mla_w_ukv"), reduced, others):
        grads[nme] = jnp.stack([jnp.where(ci == l, own, other) for l in range(DEPTH)])

    order = list(weights)
    delta, new_m, new_v = {}, {}, {}
    for nme in order:
        if nme == "w_in":
            w_t, m_t, v_t = (jnp.swapaxes(t, 1, 2) for t in (w_in, m_w_in, v_w_in))
            res = _adamw(w_t, grads[nme], m_t, v_t)
            delta[nme], new_m[nme], new_v[nme] = (jnp.swapaxes(t, 1, 2) for t in res)
            grads[nme] = jnp.swapaxes(grads[nme], 1, 2)
            continue
        delta[nme], new_m[nme], new_v[nme] = _adamw(weights[nme], grads[nme], m_in[nme], v_in[nme])
    return (loss, dx[None], *[grads[n_] for n_ in order], *[delta[n_] for n_ in order],
            *[new_m[n_] for n_ in order], *[new_v[n_] for n_ in order])
```

```python
import functools

import jax
import jax.numpy as jnp
from jax import lax
from jax.experimental import pallas as pl
from jax.experimental.pallas import tpu as pltpu

F32 = jnp.float32
BF16 = jnp.bfloat16
MESH = pl.DeviceIdType.MESH
HIGHEST = lax.Precision.HIGHEST

DEPTH = 2
D_MODEL = 2048
GLA_HEADS = 6
GLA_DK = 64
GLA_DV = 128
GLA_RANK = 16
GLA_TEMP = 16.0
GLA_CHUNK = 64
GLA_W = GLA_HEADS * GLA_DV
MLA_HEADS = 6
MLA_QL = 384
MLA_KVL = 256
MLA_NOPE = 128
MLA_ROPE = 64
MLA_DV = 128
MLA_W = MLA_HEADS * MLA_DV
CONV_CH = D_MODEL - GLA_W - MLA_W
ROPE_THETA = 10000.0
EPS = 1e-6
IN_DIM = 5856
N_CHIPS = 4
N_DEV = 8

ADAM_LR = 0.001
ADAM_B1 = 0.9
ADAM_B2 = 0.999
ADAM_EPS = 1e-08
ADAM_WD = 0.01
ADAM_STEP = 10

PROJ_SEGS = (
    ("gq", 0, 384, 384), ("gk", 384, 384, 384), ("gv", 768, 768, 768), ("glr", 1536, 32, 128),
    ("mq", 1568, 384, 384), ("mkv", 1952, 256, 256), ("mkr", 2208, 64, 128),
    ("cb", 2272, 512, 512), ("cc", 2784, 512, 512), ("cx", 3296, 512, 512),
    ("pad", 3808, 0, 128), ("z", 3808, 2048, 2048),
)
PROJ_AL = sum(s[3] for s in PROJ_SEGS)

VMEM_LIMIT = 48 * 1024 * 1024
BLOCK_BYTES = 2 * 1024 * 1024


def _params(sem=None):
    return pltpu.CompilerParams(dimension_semantics=sem, vmem_limit_bytes=VMEM_LIMIT)


def _dot(a, b, ca, cb, precision=None):
    return lax.dot_general(a, b, (((ca,), (cb,)), ((), ())), preferred_element_type=F32, precision=precision)


def _tile(dim, prefs):
    for t in prefs:
        if dim % t == 0:
            return t
    return dim


def _pick_rows(rows, width, itemsize=4):
    for t in (2048, 1024, 512, 256, 128, 64, 32, 16, 8):
        if rows % t == 0 and t * width * itemsize <= BLOCK_BYTES:
            return t
    return rows


def _mm(a, b, *, ta=False, tb=False, bias=None, out_dtype=F32, name="mm"):
    if ta:
        K, M = a.shape
    else:
        M, K = a.shape
    if tb:
        N, Kb = b.shape
    else:
        Kb, N = b.shape
    assert K == Kb, (a.shape, b.shape, ta, tb)
    tm = _tile(M, (512, 256, 128))
    tn = _tile(N, (1024, 512, 384, 256, 128))
    tk = _tile(K, (2048, 1024, 512, 256, 128))
    nk = K // tk
    has_bias = bias is not None

    def body(*refs):
        a_ref, b_ref = refs[0], refs[1]
        bias_ref = refs[2] if has_bias else None
        o_ref = refs[3 if has_bias else 2]
        part = _dot(a_ref[...].astype(BF16), b_ref[...].astype(BF16), 0 if ta else 1, 1 if tb else 0)

        def finish(r):
            if has_bias:
                r = r + bias_ref[...]
            o_ref[...] = r.astype(out_dtype)

        if nk == 1:
            finish(part)
            return
        acc_ref = refs[-1]
        k = pl.program_id(2)

        @pl.when(k == 0)
        def _():
            acc_ref[...] = part

        @pl.when(k != 0)
        def _():
            acc_ref[...] += part

        @pl.when(k == nk - 1)
        def _():
            finish(acc_ref[...])

    a_spec = pl.BlockSpec((tk, tm), lambda i, j, k: (k, i)) if ta else pl.BlockSpec((tm, tk), lambda i, j, k: (i, k))
    b_spec = pl.BlockSpec((tn, tk), lambda i, j, k: (j, k)) if tb else pl.BlockSpec((tk, tn), lambda i, j, k: (k, j))
    in_specs = [a_spec, b_spec]
    args = [a, b]
    if has_bias:
        in_specs.append(pl.BlockSpec((1, tn), lambda i, j, k: (0, j)))
        args.append(bias)
    return pl.pallas_call(
        body, name=name, grid=(M // tm, N // tn, nk),
        in_specs=in_specs, out_specs=pl.BlockSpec((tm, tn), lambda i, j, k: (i, j)),
        out_shape=jax.ShapeDtypeStruct((M, N), out_dtype),
        scratch_shapes=[pltpu.VMEM((tm, tn), F32)] if nk > 1 else [],
        compiler_params=_params(("parallel", "parallel", "arbitrary")),
    )(*args)


@jax.custom_vjp
def mm(a, b):
    return _mm(a, b, name="mm_fwd")


def _mm_f(a, b):
    return _mm(a, b, name="mm_fwd"), (a, b)


def _mm_b(res, g):
    a, b = res
    return _mm(g, b, tb=True, out_dtype=a.dtype, name="mm_da"), _mm(a, g, ta=True, out_dtype=b.dtype, name="mm_db")


mm.defvjp(_mm_f, _mm_b)


@jax.custom_vjp
def mm16(a, b):
    return _mm(a, b, out_dtype=BF16, name="mm16_fwd")


def _mm16_f(a, b):
    return mm16(a, b), (a, b)


mm16.defvjp(_mm16_f, _mm_b)


def _rows(body, name, tiled, full, tiled_out, acc_out, tr=None):
    rows = tiled[0].shape[0]
    if tr is None:
        width = max([a.shape[1] for a in tiled] + [s.shape[1] for s in tiled_out])
        tr = _pick_rows(rows, width)
    in_specs = [pl.BlockSpec((tr, a.shape[1]), lambda i: (i, 0)) for a in tiled]
    in_specs += [pl.BlockSpec(a.shape, lambda i: (0, 0)) for a in full]
    out_specs = [pl.BlockSpec((tr, s.shape[1]), lambda i: (i, 0)) for s in tiled_out]
    out_specs += [pl.BlockSpec(s.shape, lambda i: (0, 0)) for s in acc_out]

    def wrapped(*refs):
        body(pl.program_id(0), *refs)

    outs = pl.pallas_call(
        wrapped, name=name, grid=(rows // tr,), in_specs=in_specs, out_specs=out_specs,
        out_shape=list(tiled_out) + list(acc_out),
        compiler_params=_params(("arbitrary",)),
    )(*tiled, *full)
    return outs


def _sds(shape, dtype=F32):
    return jax.ShapeDtypeStruct(tuple(shape), dtype)


def _acc(step, ref, val):
    @pl.when(step == 0)
    def _():
        ref[...] = val

    @pl.when(step != 0)
    def _():
        ref[...] += val


def _colsum(v):
    return jnp.sum(v, axis=0, keepdims=True)


def _rstd(x):
    return lax.rsqrt(jnp.mean(x * x, axis=-1, keepdims=True) + EPS)


def _norm_grid(x, g):
    rows, w = x.shape[0], g.shape[1]
    tr = _pick_rows(rows, w)
    blk = pl.BlockSpec((tr, w), lambda i, j: (i, j))
    gblk = pl.BlockSpec((1, w), lambda i, j: (0, 0))
    return (rows // tr, x.shape[1] // w), blk, gblk


@jax.custom_vjp
def rmsnorm(x, g):
    def body(x_ref, g_ref, o_ref):
        x = x_ref[...]
        o_ref[...] = x * _rstd(x) * g_ref[...]

    grid, blk, gblk = _norm_grid(x, g)
    return pl.pallas_call(body, name="rmsnorm_fwd", grid=grid, in_specs=[blk, gblk], out_specs=blk,
                          out_shape=_sds(x.shape), compiler_params=_params(("parallel", "parallel")))(x, g)


def _rmsnorm_f(x, g):
    return rmsnorm(x, g), (x, g)


def _rmsnorm_b(res, dy):
    x, g = res

    def body(x_ref, dy_ref, g_ref, dx_ref, dg_ref):
        x = x_ref[...]
        dy = dy_ref[...]
        r = _rstd(x)
        xh = x * r
        dxh = dy * g_ref[...]
        dx_ref[...] = r * (dxh - xh * jnp.mean(dxh * xh, axis=-1, keepdims=True))
        first = jnp.logical_and(pl.program_id(0) == 0, pl.program_id(1) == 0)
        _acc(jnp.where(first, 0, 1), dg_ref, _colsum(dy * xh))

    grid, blk, gblk = _norm_grid(x, g)
    dx, dg = pl.pallas_call(body, name="rmsnorm_bwd", grid=grid, in_specs=[blk, blk, gblk], out_specs=[blk, gblk],
                            out_shape=[_sds(x.shape), _sds(g.shape)],
                            compiler_params=_params(("arbitrary", "arbitrary")))(x, dy, g)
    return dx, dg


rmsnorm.defvjp(_rmsnorm_f, _rmsnorm_b)


def _modulate(x, g, scale, shift):
    def body(i, x_ref, g_ref, sc_ref, sh_ref, o_ref):
        x = x_ref[...]
        xn = x * _rstd(x) * g_ref[...]
        o_ref[...] = (xn * (1.0 + sc_ref[...]) + sh_ref[...]).astype(BF16)
    return _rows(body, "modulate_fwd", [x], [g, scale, shift], [_sds(x.shape, BF16)], [])[0]


def _modulate_bwd(x, g, scale, shift, dh):
    def body(i, x_ref, dh_ref, g_ref, sc_ref, dx_ref, dg_ref, dsc_ref, dsh_ref):
        x = x_ref[...]
        dh = dh_ref[...]
        gv = g_ref[...]
        r = _rstd(x)
        xh = x * r
        dxn = dh * (1.0 + sc_ref[...])
        dxh = dxn * gv
        dx_ref[...] = r * (dxh - xh * jnp.mean(dxh * xh, axis=-1, keepdims=True))
        _acc(i, dg_ref, _colsum(dxn * xh))
        _acc(i, dsc_ref, _colsum(dh * (xh * gv)))
        _acc(i, dsh_ref, _colsum(dh))

    v = _sds(g.shape)
    return _rows(body, "modulate_bwd", [x, dh], [g, scale], [_sds(x.shape)], [v, v, v])


@jax.custom_vjp
def mod_mm(x, g, scale, shift, wt):
    return _mm(_modulate(x, g, scale, shift), wt, tb=True, name="mm_in")


def _mod_mm_f(x, g, scale, shift, wt):
    h = _modulate(x, g, scale, shift)
    return _mm(h, wt, tb=True, name="mm_in"), (x, g, scale, shift, wt, h)


def _mod_mm_b(res, dproj):
    x, g, scale, shift, wt, h = res
    dproj = dproj.astype(BF16)
    dh = _mm(dproj, wt, name="mm_in_dh")
    dwt = _mm(dproj, h, ta=True, out_dtype=wt.dtype, name="mm_in_dw")
    dx, dg, dsc, dsh = _modulate_bwd(x, g, scale, shift, dh)
    return dx, dg, dsc, dsh, dwt


mod_mm.defvjp(_mod_mm_f, _mod_mm_b)


def _sigmoid(z):
    return 1.0 / (1.0 + jnp.exp(-z))


def _gate_mul(o, z):
    def body(i, o_ref, z_ref, y_ref):
        z = z_ref[...]
        y_ref[...] = (o_ref[...] * (z * _sigmoid(z))).astype(BF16)
    return _rows(body, "gate_mul_fwd", [o, z], [], [_sds(o.shape, BF16)], [])[0]


def _gate_mul_bwd(o, z, dy):
    def body(i, o_ref, z_ref, dy_ref, do_ref, dz_ref):
        z = z_ref[...]
        dy = dy_ref[...]
        s = _sigmoid(z)
        do_ref[...] = dy * (z * s)
        dz_ref[...] = dy * o_ref[...] * (s * (1.0 + z * (1.0 - s)))
    return _rows(body, "gate_mul_bwd", [o, z, dy], [], [_sds(o.shape), _sds(o.shape)], [])


def _residual(x, u, gate):
    def body(i, x_ref, u_ref, g_ref, o_ref):
        o_ref[...] = x_ref[...] + g_ref[...] * u_ref[...]
    return _rows(body, "residual_fwd", [x, u], [gate], [_sds(x.shape)], [])[0]


def _residual_bwd(d, u, gate):
    def body(i, d_ref, u_ref, g_ref, du_ref, dg_ref):
        d = d_ref[...]
        du_ref[...] = (g_ref[...] * d).astype(BF16)
        _acc(i, dg_ref, _colsum(d * u_ref[...]))

    return _rows(body, "residual_bwd", [d, u], [gate], [_sds(u.shape, BF16)], [_sds(gate.shape)])


@jax.custom_vjp
def out_block(o, z, w, x, gate):
    return _residual(x, _mm(_gate_mul(o, z), w, name="mm_out"), gate)


def _out_block_f(o, z, w, x, gate):
    y = _gate_mul(o, z)
    u = _mm(y, w, name="mm_out")
    return _residual(x, u, gate), (o, z, w, y, u, gate)


def _out_block_b(res, d):
    o, z, w, y, u, gate = res
    du, dgate = _residual_bwd(d, u, gate)
    dy = _mm(du, w, tb=True, name="mm_out_dy")
    dw = _mm(y, du, ta=True, out_dtype=w.dtype, name="mm_out_dw")
    do, dz = _gate_mul_bwd(o, z, dy)
    return do, dz, dw, d, dgate


out_block.defvjp(_out_block_f, _out_block_b)


@jax.custom_vjp
def gate_act(u, b):
    def body(i, u_ref, b_ref, o_ref):
        t = u_ref[...] + b_ref[...]
        o_ref[...] = (jnp.minimum(t, 0.0) - jnp.log(1.0 + jnp.exp(-jnp.abs(t)))) / GLA_TEMP
    return _rows(body, "gate_act_fwd", [u], [b], [_sds(u.shape)], [])[0]


def _gate_act_f(u, b):
    return gate_act(u, b), (u, b)


def _gate_act_b(res, d):
    u, b = res

    def body(i, u_ref, d_ref, b_ref, du_ref, db_ref):
        t = u_ref[...] + b_ref[...]
        du = d_ref[...] * _sigmoid(-t) / GLA_TEMP
        du_ref[...] = du
        _acc(i, db_ref, _colsum(du))

    du, db = _rows(body, "gate_act_bwd", [u, d], [b], [_sds(u.shape)], [_sds(b.shape)])
    return du, db


gate_act.defvjp(_gate_act_f, _gate_act_b)


@jax.custom_vjp
def fma(a, b, c, d):
    def body(i, a_ref, b_ref, c_ref, d_ref, o_ref):
        o_ref[...] = a_ref[...] * b_ref[...] + c_ref[...] * d_ref[...]
    return _rows(body, "fma_fwd", [a, b, c, d], [], [_sds(a.shape)], [])[0]


def _fma_f(a, b, c, d):
    return fma(a, b, c, d), (b, d)


def _fma_b(res, g):
    b, d = res

    def body(i, g_ref, b_ref, d_ref, da_ref, dc_ref):
        g = g_ref[...]
        da_ref[...] = g * b_ref[...]
        dc_ref[...] = g * d_ref[...]

    da, dc = _rows(body, "fma_bwd", [g, b, d], [], [_sds(g.shape), _sds(g.shape)], [])
    return da, jnp.zeros_like(b), dc, jnp.zeros_like(d)


fma.defvjp(_fma_f, _fma_b)


def _silu_rows(c):
    def body(i, c_ref, o_ref):
        v = c_ref[...]
        o_ref[...] = v * _sigmoid(v)
    return _rows(body, "silu", [c], [], [_sds(c.shape)], [])[0]


@jax.custom_vjp
def loss_op(y, t):
    return _loss_fwd(y, t)[0]


def _loss_fwd(y, t):
    inv = 1.0 / y.shape[1]

    def body(i, y_ref, t_ref, d_ref, l_ref):
        e = y_ref[...] - t_ref[...]
        d_ref[...] = e * inv
        _acc(i, l_ref, jnp.sum(_colsum(e * e), axis=1, keepdims=True) * (0.5 * inv))

    d, l = _rows(body, "loss_fwd", [y, t], [], [_sds(y.shape)], [_sds((1, 1))])
    return l, d


def _loss_f(y, t):
    l, d = _loss_fwd(y, t)
    return l, d


def _loss_b(d, g):
    return d * g, jnp.zeros_like(d)


loss_op.defvjp(_loss_f, _loss_b)


def _conv_terms(cc, cx, rows, n):
    u = cc * cx
    up = jnp.where(rows == 0, 0.0, pltpu.roll(u, 1, 0))
    un = jnp.where(rows == n - 1, 0.0, pltpu.roll(u, n - 1, 0))
    return u, up, un


CONV_COLS = 128


def _conv_specs(s, n_in):
    blk = pl.BlockSpec((s, CONV_COLS), lambda j: (0, j))
    wblk = pl.BlockSpec((8, CONV_COLS), lambda j: (0, j))
    return [blk] * n_in + [wblk], blk, wblk


@jax.custom_vjp
def conv_op(cb, cc, cx, w):
    s, ch = cb.shape

    def body(cb_ref, cc_ref, cx_ref, w_ref, o_ref):
        rows = lax.broadcasted_iota(jnp.int32, (s, CONV_COLS), 0)
        u, up, un = _conv_terms(cc_ref[...], cx_ref[...], rows, s)
        conv = up * w_ref[0:1, :] + u * w_ref[1:2, :] + un * w_ref[2:3, :]
        o_ref[...] = cb_ref[...] * conv

    in_specs, blk, _ = _conv_specs(s, 3)
    return pl.pallas_call(
        body, name="conv_fwd", grid=(ch // CONV_COLS,), in_specs=in_specs, out_specs=blk,
        out_shape=_sds(cb.shape), compiler_params=_params(("parallel",)),
    )(cb, cc, cx, w)


def _conv_f(cb, cc, cx, w):
    return conv_op(cb, cc, cx, w), (cb, cc, cx, w)


def _conv_b(res, d):
    cb, cc, cx, w = res
    s, ch = cb.shape

    def body(cb_ref, cc_ref, cx_ref, d_ref, w_ref, dcb_ref, dcc_ref, dcx_ref, dw_ref):
        rows = lax.broadcasted_iota(jnp.int32, (s, CONV_COLS), 0)
        cc_v = cc_ref[...]
        cx_v = cx_ref[...]
        u, up, un = _conv_terms(cc_v, cx_v, rows, s)
        w0, w1, w2 = w_ref[0:1, :], w_ref[1:2, :], w_ref[2:3, :]
        dv = d_ref[...]
        dcb_ref[...] = dv * (up * w0 + u * w1 + un * w2)
        dconv = dv * cb_ref[...]
        d_next = jnp.where(rows == s - 1, 0.0, pltpu.roll(dconv, s - 1, 0))
        d_prev = jnp.where(rows == 0, 0.0, pltpu.roll(dconv, 1, 0))
        du = w0 * d_next + w1 * dconv + w2 * d_prev
        dcc_ref[...] = du * cx_v
        dcx_ref[...] = du * cc_v
        dw_ref[...] = jnp.zeros_like(dw_ref)
        dw_ref[0:1, :] = _colsum(dconv * up)
        dw_ref[1:2, :] = _colsum(dconv * u)
        dw_ref[2:3, :] = _colsum(dconv * un)

    in_specs, blk, wblk = _conv_specs(s, 4)
    v = _sds(cb.shape)
    return tuple(pl.pallas_call(
        body, name="conv_bwd", grid=(ch // CONV_COLS,), in_specs=in_specs, out_specs=[blk, blk, blk, wblk],
        out_shape=[v, v, v, _sds(w.shape)], compiler_params=_params(("parallel",)),
    )(cb, cc, cx, d, w))


conv_op.defvjp(_conv_f, _conv_b)


def _gla_masks(rev):
    c = GLA_CHUNK
    row = lax.broadcasted_iota(jnp.int32, (c, c), 0)
    col = lax.broadcasted_iota(jnp.int32, (c, c), 1)
    mask = (row < col) if rev else (row >= col)
    return rev, mask


def _chunk_cumsum(g, rev):
    c = g.shape[0]
    row = lax.broadcasted_iota(jnp.int32, g.shape, 0)
    b = g
    s = 1
    while s < c:
        if rev:
            b = b + jnp.where(row < c - s, pltpu.roll(b, c - s, 0), 0.0)
        else:
            b = b + jnp.where(row >= s, pltpu.roll(b, s, 0), 0.0)
        s *= 2
    return b


GLA_UNROLL = 4


def _gla_rows(n):
    return pl.ds(pl.multiple_of(n * GLA_CHUNK, GLA_CHUNK), GLA_CHUNK)


def _gla_scan(s_ref, bt_ref, st_ref, n_chunks, descending):
    st_ref[...] = jnp.zeros_like(st_ref)

    def step(i, carry):
        n = (n_chunks - 1 - i) if descending else i
        own = s_ref[n]
        st = st_ref[...]
        s_ref[n] = st
        st_ref[...] = st * jnp.exp(bt_ref[n]) + own
        return carry

    lax.fori_loop(0, n_chunks, step, 0)


GLA_PAIR = 2


def _gla_specs(s):
    dk, dv = GLA_DK, GLA_DV
    n_pairs = GLA_HEADS // GLA_PAIR
    blk_k = pl.BlockSpec((s, GLA_PAIR * dk), lambda p: (0, p))
    blk_gb = pl.BlockSpec((s, GLA_PAIR * dk), lambda p: (0, n_pairs + p))
    blk_v = pl.BlockSpec((s, GLA_PAIR * dv), lambda p: (0, p))
    return n_pairs, blk_k, blk_gb, blk_v


def _head_lanes(hh):
    lane = lax.broadcasted_iota(jnp.int32, (1, GLA_PAIR * GLA_DK), 1)
    return jnp.logical_and(lane >= hh * GLA_DK, lane < (hh + 1) * GLA_DK)


def _gla_fwd(q, k, v, la):
    s = q.shape[0]
    dk, dv = GLA_DK, GLA_DV
    pw = GLA_PAIR * dk
    n_chunks = s // GLA_CHUNK
    scale = GLA_DK ** -0.5

    def body(q_ref, k_ref, v_ref, gf_ref, gb_ref, o_ref, sf_ref, sb_ref, bf_ref, bb_ref, btf_ref, btb_ref, st_ref):
        masks = [_gla_masks(rev) for rev in (False, True)]
        dirs = ((False, gf_ref, sf_ref, bf_ref, btf_ref), (True, gb_ref, sb_ref, bb_ref, btb_ref))

        def decays(n, carry):
            rows = _gla_rows(n)
            for rev, g_ref, _, b_ref, bt_ref in dirs:
                g = g_ref[rows, :]
                b_ref[rows, :] = _chunk_cumsum(g, rev)
                bt_ref[n] = _colsum(g)
            return carry

        lax.fori_loop(0, n_chunks, decays, 0, unroll=GLA_UNROLL)
        for hh in range(GLA_PAIR):
            m = _head_lanes(hh)
            vl = slice(hh * dv, (hh + 1) * dv)

            def prepare(n, carry, m=m, vl=vl):
                rows = _gla_rows(n)
                kk = k_ref[rows, :]
                vb = v_ref[rows, vl].astype(BF16)
                for rev, _, s_ref, b_ref, bt_ref in dirs:
                    ke = jnp.where(m, kk * jnp.exp(bt_ref[n] - b_ref[rows, :]), 0.0).astype(BF16)
                    s_ref[n] = _dot(vb, ke, 0, 0)
                return carry

            lax.fori_loop(0, n_chunks, prepare, 0, unroll=GLA_UNROLL)
            for rev, _, s_ref, _, bt_ref in dirs:
                _gla_scan(s_ref, bt_ref, st_ref, n_chunks, descending=rev)

            def emit(n, carry, m=m, vl=vl):
                rows = _gla_rows(n)
                qs = q_ref[rows, :] * scale
                kk = k_ref[rows, :]
                vb = v_ref[rows, vl].astype(BF16)
                o = None
                for (rev, _, s_ref, b_ref, _), (_, mask) in zip(dirs, masks):
                    b = b_ref[rows, :]
                    qd = jnp.where(m, qs * jnp.exp(b), 0.0).astype(BF16)
                    ki = jnp.where(m, kk * jnp.exp(-b), 0.0).astype(BF16)
                    a = jnp.where(mask, _dot(qd, ki, 1, 1), 0.0).astype(BF16)
                    od = _dot(a, vb, 1, 0) + _dot(qd, s_ref[n].astype(BF16), 1, 1)
                    o = od if o is None else o + od
                o_ref[rows, vl] = o
                return carry

            lax.fori_loop(0, n_chunks, emit, 0, unroll=GLA_UNROLL)

    n_pairs, blk_k, blk_gb, blk_v = _gla_specs(s)
    state = pltpu.VMEM((n_chunks, dv, pw), F32)
    scratch = [state, state, pltpu.VMEM((s, pw), F32), pltpu.VMEM((s, pw), F32), pltpu.VMEM((n_chunks, 1, pw), F32),
               pltpu.VMEM((n_chunks, 1, pw), F32), pltpu.VMEM((dv, pw), F32)]
    return pl.pallas_call(
        body, name="gla_fwd", grid=(n_pairs,), in_specs=[blk_k, blk_k, blk_v, blk_k, blk_gb],
        out_specs=blk_v, out_shape=_sds(v.shape), scratch_shapes=scratch,
        compiler_params=_params(("parallel",)),
    )(q, k, v, la, la)


def _gla_bwd(q, k, v, la, do):
    s = q.shape[0]
    dk, dv = GLA_DK, GLA_DV
    pw = GLA_PAIR * dk
    c = GLA_CHUNK
    n_chunks = s // c
    scale = GLA_DK ** -0.5

    def body(q_ref, k_ref, v_ref, gf_ref, gb_ref, do_ref, dq_ref, dk_ref, dv_ref, dgf_ref, dgb_ref,
             sf_ref, sb_ref, bf_ref, bb_ref, btf_ref, btb_ref, dsf_ref, dsb_ref, st_ref):
        masks = [_gla_masks(rev) for rev in (False, True)]
        rowc = lax.broadcasted_iota(jnp.int32, (c, pw), 0)
        dirs = ((False, gf_ref, sf_ref, bf_ref, btf_ref, dsf_ref, dgf_ref),
                (True, gb_ref, sb_ref, bb_ref, btb_ref, dsb_ref, dgb_ref))

        def decays(n, carry):
            rows = _gla_rows(n)
            for rev, g_ref, _, b_ref, bt_ref, _, _ in dirs:
                g = g_ref[rows, :]
                b_ref[rows, :] = _chunk_cumsum(g, rev)
                bt_ref[n] = _colsum(g)
            return carry

        lax.fori_loop(0, n_chunks, decays, 0, unroll=GLA_UNROLL)
        for hh in range(GLA_PAIR):
            m = _head_lanes(hh)
            vl = slice(hh * dv, (hh + 1) * dv)

            def prepare(n, carry, m=m, vl=vl):
                rows = _gla_rows(n)
                qs = q_ref[rows, :] * scale
                kk = k_ref[rows, :]
                vb = v_ref[rows, vl].astype(BF16)
                do_b = do_ref[rows, vl].astype(BF16)
                for rev, _, s_ref, b_ref, bt_ref, ds_ref, _ in dirs:
                    b = b_ref[rows, :]
                    ke = jnp.where(m, kk * jnp.exp(bt_ref[n] - b), 0.0).astype(BF16)
                    qd = jnp.where(m, qs * jnp.exp(b), 0.0).astype(BF16)
                    s_ref[n] = _dot(vb, ke, 0, 0)
                    ds_ref[n] = _dot(do_b, qd, 0, 0)
                return carry

            lax.fori_loop(0, n_chunks, prepare, 0, unroll=GLA_UNROLL)
            for rev, _, s_ref, _, bt_ref, ds_ref, _ in dirs:
                _gla_scan(s_ref, bt_ref, st_ref, n_chunks, descending=rev)
                _gla_scan(ds_ref, bt_ref, st_ref, n_chunks, descending=not rev)

            def emit(n, carry, m=m, vl=vl, first=(hh == 0)):
                rows = _gla_rows(n)
                qs = q_ref[rows, :] * scale
                kk = k_ref[rows, :]
                vb = v_ref[rows, vl].astype(BF16)
                do_b = do_ref[rows, vl].astype(BF16)
                dq = dkk = dvv = None
                for (rev, _, s_ref, b_ref, bt_ref, ds_ref, dg_ref), (_, mask) in zip(dirs, masks):
                    b = b_ref[rows, :]
                    bt = bt_ref[n]
                    eb = jnp.where(m, jnp.exp(b), 0.0)
                    enb = jnp.where(m, jnp.exp(-b), 0.0)
                    etb = jnp.where(m, jnp.exp(bt - b), 0.0)
                    ebt = jnp.exp(bt)
                    qd = qs * eb
                    ki = kk * enb
                    ke = kk * etb
                    qd_b, ki_b, ke_b = qd.astype(BF16), ki.astype(BF16), ke.astype(BF16)
                    st = s_ref[n]
                    dst = ds_ref[n]
                    dst_b = dst.astype(BF16)
                    a = jnp.where(mask, _dot(qd_b, ki_b, 1, 1), 0.0).astype(BF16)
                    da = jnp.where(mask, _dot(do_b, vb, 1, 1), 0.0).astype(BF16)
                    dv_d = _dot(a, do_b, 0, 0) + _dot(ke_b, dst_b, 1, 1)
                    dqd = _dot(da, ki_b, 1, 0) + _dot(do_b, st.astype(BF16), 1, 0)
                    dki = _dot(da, qd_b, 0, 0)
                    dke = _dot(vb, dst_b, 1, 0)
                    dbt = _colsum(st * dst) * ebt + _colsum(dke * ke)
                    db = dqd * qd - dki * ki - dke * ke
                    db = db + jnp.where(rowc == (0 if rev else c - 1), dbt, 0.0)
                    dg = _chunk_cumsum(db, not rev)
                    if first:
                        dg_ref[rows, :] = dg
                    else:
                        dg_ref[rows, :] += dg
                    dq_d = dqd * eb * scale
                    dk_d = dki * enb + dke * etb
                    dq = dq_d if dq is None else dq + dq_d
                    dkk = dk_d if dkk is None else dkk + dk_d
                    dvv = dv_d if dvv is None else dvv + dv_d
                if first:
                    dq_ref[rows, :] = dq
                    dk_ref[rows, :] = dkk
                else:
                    dq_ref[rows, :] += dq
                    dk_ref[rows, :] += dkk
                dv_ref[rows, vl] = dvv
                return carry

            lax.fori_loop(0, n_chunks, emit, 0, unroll=2)

    n_pairs, blk_k, blk_gb, blk_v = _gla_specs(s)
    vk, vv = _sds(q.shape), _sds(v.shape)
    state = pltpu.VMEM((n_chunks, dv, pw), F32)
    scratch = [state, state, pltpu.VMEM((s, pw), F32), pltpu.VMEM((s, pw), F32), pltpu.VMEM((n_chunks, 1, pw), F32),
               pltpu.VMEM((n_chunks, 1, pw), F32), state, state, pltpu.VMEM((dv, pw), F32)]
    return pl.pallas_call(
        body, name="gla_bwd", grid=(n_pairs,), in_specs=[blk_k, blk_k, blk_v, blk_k, blk_gb, blk_v],
        out_specs=[blk_k, blk_k, blk_v, blk_k, blk_k], out_shape=[vk, vk, vv, vk, vk],
        scratch_shapes=scratch, compiler_params=_params(("parallel",)),
    )(q, k, v, la, la, do)


@jax.custom_vjp
def gla(q, k, v, la):
    return _gla_fwd(q, k, v, la)


def _gla_f(q, k, v, la):
    return _gla_fwd(q, k, v, la), (q, k, v, la)


def _gla_b(res, do):
    dq, dk, dv, dgf, dgb = _gla_bwd(*res, do)
    return dq, dk, dv, jnp.concatenate([dgf, dgb], axis=1)


gla.defvjp(_gla_f, _gla_b)


ATTN_TQ = 256
HEAD_LANES = 128


def _attn_blocks(s, tq):
    per_q = pl.BlockSpec((tq, HEAD_LANES), lambda h, j: (j, h))
    k_nope = pl.BlockSpec((s, HEAD_LANES), lambda h, j: (0, 2 * h))
    v_blk = pl.BlockSpec((s, HEAD_LANES), lambda h, j: (0, 2 * h + 1))
    k_rope = pl.BlockSpec((s, HEAD_LANES), lambda h, j: (0, 0))
    lse = pl.BlockSpec((1, tq, 1), lambda h, j: (h, j, 0))
    return per_q, k_nope, v_blk, k_rope, lse


def _attn_fwd(qn, qr, kv, kr):
    s = qn.shape[0]
    tq = min(ATTN_TQ, s)
    scale = (MLA_NOPE + MLA_ROPE) ** -0.5

    def body(qn_ref, qr_ref, kn_ref, v_ref, kr_ref, o_ref, lse_ref):
        q = jnp.concatenate([qn_ref[...], qr_ref[...]], axis=1)
        k = jnp.concatenate([kn_ref[...], kr_ref[...]], axis=1)
        sc = _dot(q, k, 1, 1) * scale
        m = jnp.max(sc, axis=-1, keepdims=True)
        p = jnp.exp(sc - m)
        l = jnp.sum(p, axis=-1, keepdims=True)
        p = p * (1.0 / l)
        o_ref[...] = _dot(p.astype(BF16), v_ref[...], 1, 0)
        lse_ref[0] = m + jnp.log(l)

    per_q, k_nope, v_blk, k_rope, lse = _attn_blocks(s, tq)
    return pl.pallas_call(
        body, name="attn_fwd", grid=(MLA_HEADS, s // tq), in_specs=[per_q, per_q, k_nope, v_blk, k_rope],
        out_specs=[per_q, lse], out_shape=[_sds(qn.shape), _sds((MLA_HEADS, s, 1))],
        compiler_params=_params(("parallel", "parallel")),
    )(qn, qr, kv, kv, kr)


def _attn_bwd(qn, qr, kv, kr, o, lse, do):
    s = qn.shape[0]
    tq = min(ATTN_TQ, s)
    n_q = s // tq
    scale = (MLA_NOPE + MLA_ROPE) ** -0.5

    def body(qn_ref, qr_ref, kn_ref, v_ref, kr_ref, o_ref, lse_ref, do_ref, dqn_ref, dqr_ref, dkv_ref, dkr_ref,
             dk_acc, dv_acc, dkr_acc):
        h, j = pl.program_id(0), pl.program_id(1)
        q = jnp.concatenate([qn_ref[...], qr_ref[...]], axis=1)
        k = jnp.concatenate([kn_ref[...], kr_ref[...]], axis=1)
        do = do_ref[...]
        do_b = do.astype(BF16)
        p = jnp.exp(_dot(q, k, 1, 1) * scale - lse_ref[0])
        dp = _dot(do_b, v_ref[...], 1, 1)
        delta = jnp.sum(do * o_ref[...], axis=-1, keepdims=True)
        ds = (p * (dp - delta) * scale).astype(BF16)
        dq = _dot(ds, k, 1, 0)
        dqn_ref[...] = dq[:, :HEAD_LANES].astype(BF16)
        dqr_ref[...] = dq[:, HEAD_LANES:].astype(BF16)
        dk = _dot(ds, q, 0, 0)
        _acc(j, dk_acc, dk[:, :HEAD_LANES])
        _acc(j, dv_acc, _dot(p.astype(BF16), do_b, 0, 0))
        _acc(jnp.where(jnp.logical_and(h == 0, j == 0), 0, 1), dkr_acc, dk[:, HEAD_LANES:])

        @pl.when(j == n_q - 1)
        def _():
            dkv_ref[:, 0:HEAD_LANES] = dk_acc[...].astype(BF16)
            dkv_ref[:, HEAD_LANES:2 * HEAD_LANES] = dv_acc[...].astype(BF16)

        @pl.when(jnp.logical_and(h == MLA_HEADS - 1, j == n_q - 1))
        def _():
            dkr_ref[...] = dkr_acc[...].astype(BF16)

    per_q, k_nope, v_blk, k_rope, lse_blk = _attn_blocks(s, tq)
    dkv_blk = pl.BlockSpec((s, 2 * HEAD_LANES), lambda h, j: (0, h))
    acc = pltpu.VMEM((s, HEAD_LANES), F32)
    return pl.pallas_call(
        body, name="attn_bwd", grid=(MLA_HEADS, n_q),
        in_specs=[per_q, per_q, k_nope, v_blk, k_rope, per_q, lse_blk, per_q],
        out_specs=[per_q, per_q, dkv_blk, k_rope],
        out_shape=[_sds(qn.shape, BF16), _sds(qr.shape, BF16), _sds(kv.shape, BF16), _sds(kr.shape, BF16)],
        scratch_shapes=[acc, acc, acc], compiler_params=_params(("arbitrary", "arbitrary")),
    )(qn, qr, kv, kv, kr, o, lse, do)


@jax.custom_vjp
def attn(qn, qr, kv, kr):
    return _attn_fwd(qn, qr, kv, kr)[0]


def _attn_f(qn, qr, kv, kr):
    o, lse = _attn_fwd(qn, qr, kv, kr)
    return o, (qn, qr, kv, kr, o, lse)


def _attn_b(res, do):
    return tuple(_attn_bwd(*res, do))


attn.defvjp(_attn_f, _attn_b)


@jax.custom_vjp
def split_proj(proj):
    out, at = [], 0
    for _, _, _, wp in PROJ_SEGS:
        out.append(proj[:, at:at + wp])
        at += wp
    return tuple(out)


def _split_f(proj):
    return split_proj(proj), None


def _split_b(_, gs):
    return (jnp.concatenate(gs, axis=1),)


split_proj.defvjp(_split_f, _split_b)


def _tile2d(rows, width, limit=BLOCK_BYTES):
    fits = [t for t in range(16, rows + 1, 16) if rows % t == 0 and t * width * 4 <= limit]
    if fits and (fits[-1] >= 64 or fits[-1] == rows):
        return fits[-1], width
    if rows * width * 4 <= limit:
        return rows, width
    cols = [t for t in range(128, width + 1, 128) if width % t == 0 and rows * t * 4 <= limit]
    return (rows, cols[-1]) if cols else (rows, width)


def _add_pair(stacked, theirs, c_idx):
    g, r, w = theirs.shape
    tr, tc = _tile2d(r, w)

    def body(c_ref, a_ref, b_ref, o_ref):
        o_ref[0] = (a_ref[0, 0].astype(F32) + b_ref[0].astype(F32)).astype(BF16)

    blk = pl.BlockSpec((1, tr, tc), lambda k, i, j, c: (k, i, j))
    spec = pltpu.PrefetchScalarGridSpec(
        num_scalar_prefetch=1, grid=(g, r // tr, w // tc),
        in_specs=[pl.BlockSpec((1, 1, tr, tc), lambda k, i, j, c: (c[0], k, i, j)), blk], out_specs=blk)
    return pl.pallas_call(body, name="add_pair", grid_spec=spec, out_shape=_sds(theirs.shape, BF16),
                          compiler_params=_params(("parallel", "parallel", "parallel")))(c_idx, stacked, theirs)


def _add_chips(pair, landed, chip_idx):
    _, r, w = pair.shape
    tr, tc = _tile2d(r, w)

    def body(c_ref, p_ref, l0_ref, l1_ref, l2_ref, o_ref):
        o_ref[...] = ((p_ref[0].astype(F32) + l0_ref[0].astype(F32)) + l1_ref[0].astype(F32)) + l2_ref[0].astype(F32)

    specs = [pl.BlockSpec((1, tr, tc), lambda i, j, c: (c[0], i, j))]
    specs += [pl.BlockSpec((1, tr, tc), functools.partial(lambda i, j, c, k: (k, i, j), k=k)) for k in range(N_CHIPS - 1)]
    spec = pltpu.PrefetchScalarGridSpec(num_scalar_prefetch=1, grid=(r // tr, w // tc), in_specs=specs,
                                        out_specs=pl.BlockSpec((tr, tc), lambda i, j, c: (i, j)))
    return pl.pallas_call(body, name="add_chips", grid_spec=spec, out_shape=_sds((r, w)),
                          compiler_params=_params(("parallel", "parallel")))(chip_idx, pair, landed, landed, landed)


def _sum_devices(g):
    n = g.shape[2]

    def body(g_ref, o_ref):
        t = g_ref[0]
        for j in range(1, N_DEV):
            t = t + g_ref[j]
        o_ref[...] = t

    return pl.pallas_call(body, name="sum_devices", out_shape=_sds((1, n)), compiler_params=_params())(g)


def _adamw(w, g, m, v):
    shp = w.shape
    shp3 = (1, 1, shp[0]) if len(shp) == 1 else (-1,) + tuple(shp[-2:])
    w3, g3, m3, v3 = (t.reshape(shp3) for t in (w, g, m, v))
    c1 = 1.0 - ADAM_B1 ** ADAM_STEP
    c2 = 1.0 - ADAM_B2 ** ADAM_STEP

    def body(w_ref, g_ref, m_ref, v_ref, d_ref, mo_ref, vo_ref):
        gv = g_ref[...]
        mn = ADAM_B1 * m_ref[...] + (1.0 - ADAM_B1) * gv
        vn = ADAM_B2 * v_ref[...] + (1.0 - ADAM_B2) * (gv * gv)
        d_ref[...] = -ADAM_LR * ((mn / c1) / (jnp.sqrt(vn / c2) + ADAM_EPS) + ADAM_WD * w_ref[...])
        mo_ref[...] = mn
        vo_ref[...] = vn

    nl, r, wd = w3.shape
    tr, tc = _tile2d(r, wd, BLOCK_BYTES // 2)
    blk = pl.BlockSpec((1, tr, tc), lambda l, i, j: (l, i, j))
    s3 = _sds(w3.shape)
    d, mn, vn = pl.pallas_call(
        body, name="adamw", grid=(nl, r // tr, wd // tc), in_specs=[blk] * 4, out_specs=[blk] * 3,
        out_shape=[s3, s3, s3], compiler_params=_params(("parallel", "parallel", "parallel")),
    )(w3, g3, m3, v3)
    return d.reshape(shp), mn.reshape(shp), vn.reshape(shp)


ANY = pl.BlockSpec(memory_space=pl.ANY)
PIECE_BYTES = 1 << 20


def _place():
    return lax.axis_index("x"), lax.axis_index("y"), lax.axis_index("c")


def _pieces(shape, itemsize):
    if len(shape) >= 3:
        return [(i,) + p for i in range(shape[0]) for p in _pieces(shape[1:], itemsize)]
    rows = shape[0]
    row_bytes = itemsize
    for dsz in shape[1:]:
        row_bytes *= dsz
    k = 1
    while rows % (2 * k) == 0 and (rows // (2 * k)) % 16 == 0 and (rows // k) * row_bytes > PIECE_BYTES:
        k *= 2
    step = rows // k
    return [(pl.ds(j * step, step),) for j in range(k)]


def _split_start(make, src, dst, pieces):
    for p in pieces:
        make(src.at[p], dst.at[p]).start()
    return make(src, dst)


def _comm_call(body, name, arrs, out_shapes, n_remote, n_local):
    return pl.pallas_call(
        body, name=name, in_specs=[ANY] * len(arrs), out_specs=[ANY] * len(out_shapes), out_shape=out_shapes,
        scratch_shapes=[pltpu.SemaphoreType.DMA((n_remote,)), pltpu.SemaphoreType.DMA((n_remote,)),
                        pltpu.SemaphoreType.DMA((n_local,))],
    )(*arrs)


def all_gather8(arrs, name):
    n = len(arrs)
    pieces = [_pieces(a.shape, a.dtype.itemsize) for a in arrs]

    def body(*refs):
        ins, outs = refs[:n], refs[n:2 * n]
        send, recv, _ = refs[2 * n:]
        x, y, c = _place()
        me, sib = (x, y, c), (x, y, 1 - c)
        chips = [(1 - x, y), (x, 1 - y), (1 - x, 1 - y)]

        def slot(p):
            return 4 * p[0] + 2 * p[1] + p[2]

        def maker(t, k, to):
            def make(s, d):
                return pltpu.make_async_remote_copy(src_ref=s, dst_ref=d, send_sem=send.at[7 * t + k],
                                                    recv_sem=recv.at[7 * t + k], device_id=to, device_id_type=MESH)
            return make

        def landing(t, k, block):
            dst = outs[t].at[slot(block)]
            return maker(t, k, me)(dst, dst)

        sent = []
        for t in range(n):
            dst = outs[t].at[slot(me)]
            sent.append(_split_start(maker(t, 0, sib), ins[t], dst, pieces[t]))
            for j, chip in enumerate(chips):
                sent.append(_split_start(maker(t, 1 + j, (*chip, c)), ins[t], dst, pieces[t]))
        for j, chip in enumerate(chips):
            for t in range(n):
                landing(t, 1 + j, (*chip, c)).wait_recv()
                blk = outs[t].at[slot((*chip, c))]
                sent.append(_split_start(maker(t, 4 + j, sib), blk, blk, pieces[t]))
        for t in range(n):
            landing(t, 0, sib).wait_recv()
            for j, chip in enumerate(chips):
                landing(t, 4 + j, (*chip, 1 - c)).wait_recv()
        for cp in sent:
            cp.wait_send()

    outs = [_sds((N_DEV,) + a.shape, a.dtype) for a in arrs]
    got = _comm_call(body, name, arrs, outs, 7 * n, 1)
    x, y, c = _place()
    return [lax.dynamic_update_index_in_dim(g, a, 4 * x + 2 * y + c, 0) for g, a in zip(got, arrs)]


def sibling_send(arrs, name):
    n = len(arrs)
    pieces = [_pieces(a.shape[1:], a.dtype.itemsize) for a in arrs]

    def body(*refs):
        ins, theirs = refs[:n], refs[n:2 * n]
        send, recv, _ = refs[2 * n:]
        x, y, c = _place()
        rem = []
        for t in range(n):
            def make(s, d, t=t):
                return pltpu.make_async_remote_copy(src_ref=s, dst_ref=d, send_sem=send.at[t], recv_sem=recv.at[t],
                                                    device_id=(x, y, 1 - c), device_id_type=MESH)
            rem.append(_split_start(make, ins[t].at[1 - c], theirs[t], pieces[t]))
        for cp in rem:
            cp.wait_recv()
        for cp in rem:
            cp.wait_send()

    outs = [_sds(a.shape[1:], a.dtype) for a in arrs]
    return _comm_call(body, name, arrs, outs, n, 1)


def exchange_chips(arrs, name):
    n = len(arrs)
    pieces = [_pieces(a.shape[1:], a.dtype.itemsize) for a in arrs]

    def body(*refs):
        ins, outs = refs[:n], refs[n:2 * n]
        send, recv, _ = refs[2 * n:]
        x, y, c = _place()
        peers = [(1 - x, y), (x, 1 - y), (1 - x, 1 - y)]
        rem = []
        for t in range(n):
            for j, (px, py) in enumerate(peers):
                def make(s, d, t=t, j=j, px=px, py=py):
                    return pltpu.make_async_remote_copy(
                        src_ref=s, dst_ref=d, send_sem=send.at[3 * t + j], recv_sem=recv.at[3 * t + j],
                        device_id=(px, py, c), device_id_type=MESH)
                rem.append(_split_start(make, ins[t].at[2 * px + py], outs[t].at[j], pieces[t]))
        for cp in rem:
            cp.wait_recv()
        for cp in rem:
            cp.wait_send()

    outs = [_sds((N_CHIPS - 1,) + a.shape[1:], a.dtype) for a in arrs]
    return _comm_call(body, name, arrs, outs, 3 * n, 1)


def sibling_swap(arrs, name):
    n = len(arrs)
    pieces = [_pieces(a.shape, a.dtype.itemsize) for a in arrs]

    def body(*refs):
        ins, outs = refs[:n], refs[n:2 * n]
        send, recv, _ = refs[2 * n:]
        x, y, c = _place()
        rem = []
        for t in range(n):
            def make(s, d, t=t):
                return pltpu.make_async_remote_copy(src_ref=s, dst_ref=d, send_sem=send.at[t], recv_sem=recv.at[t],
                                                    device_id=(x, y, 1 - c), device_id_type=MESH)
            rem.append(_split_start(make, ins[t], outs[t], pieces[t]))
        for cp in rem:
            cp.wait_recv()
        for cp in rem:
            cp.wait_send()

    outs = [_sds(a.shape, a.dtype) for a in arrs]
    return _comm_call(body, name, arrs, outs, n, 1)


def _peer_copies(srcs, lands, send, recv, mode):
    x, y, c = _place()
    my_chip = 2 * x + y
    out = []
    for t in range(len(srcs)):
        for j, (px, py) in enumerate([(1 - x, y), (x, 1 - y), (1 - x, 1 - y)]):
            if mode == "gather":
                s, dst = srcs[t], lands[t].at[c, my_chip]
            else:
                s, dst = srcs[t].at[2 * px + py], lands[t].at[j]
            out.append(pltpu.make_async_remote_copy(
                src_ref=s, dst_ref=dst, send_sem=send.at[3 * t + j], recv_sem=recv.at[3 * t + j],
                device_id=(px, py, c), device_id_type=MESH))
    return out


HBM = pl.BlockSpec(memory_space=pltpu.HBM)
SEM = pl.BlockSpec(memory_space=pltpu.SEMAPHORE)
EFFECT = pltpu.SideEffectType.DATAFLOW_SIDE_EFFECTING


def ici_start(srcs, lands, mode, name):
    n = len(srcs)

    def body(*refs):
        send, recv = refs[2 * n], refs[2 * n + 1]
        for cp in _peer_copies(refs[:n], refs[n:2 * n], send, recv, mode):
            cp.start()
        refs[-1][...] = jnp.zeros_like(refs[-1])

    thru = [pltpu.HBM(a.shape, a.dtype) for a in list(srcs) + list(lands)]
    outs = pl.pallas_call(
        body, name=name, in_specs=[HBM] * (2 * n), out_specs=[SEM, SEM] + [HBM] * (2 * n) + [pl.BlockSpec(memory_space=pltpu.VMEM)],
        out_shape=[pltpu.SemaphoreType.DMA((3 * n,)), pltpu.SemaphoreType.DMA((3 * n,))] + thru + [_sds((8, 128))],
        input_output_aliases={i: 2 + i for i in range(2 * n)},
        compiler_params=pltpu.CompilerParams(has_side_effects=EFFECT),
    )(*[pltpu.with_memory_space_constraint(a, pltpu.HBM) for a in list(srcs) + list(lands)])
    return dict(send=outs[0], recv=outs[1], srcs=outs[2:2 + n], lands=outs[2 + n:2 + 2 * n], token=outs[-1])


def ici_wait(handle, after, mode, name):
    n = len(handle["srcs"])

    def body(*refs):
        send, recv = refs[2 * n], refs[2 * n + 1]
        for cp in _peer_copies(refs[:n], refs[n:2 * n], send, recv, mode):
            cp.wait_send()
            cp.wait_recv()

    arrs = list(handle["srcs"]) + list(handle["lands"])
    outs = pl.pallas_call(
        body, name=name, in_specs=[HBM] * (2 * n) + [SEM, SEM, ANY], out_specs=[HBM] * (2 * n),
        out_shape=[pltpu.HBM(a.shape, a.dtype) for a in arrs], input_output_aliases={i: i for i in range(2 * n)},
        compiler_params=pltpu.CompilerParams(has_side_effects=EFFECT),
    )(*arrs, handle["send"], handle["recv"], after)
    return outs[:n], outs[n:]


def gather_share(blocks, lands, name):
    n = len(blocks)

    def body(*refs):
        own, buf = refs[:n], refs[2 * n:3 * n]
        send, recv, _ = refs[3 * n:]
        x, y, c = _place()
        my_chip = 2 * x + y
        chips = [2 * (1 - x) + y, 2 * x + (1 - y), 2 * (1 - x) + (1 - y)]
        sent = []
        for t in range(n):
            def make(s, d, k, t=t):
                return pltpu.make_async_remote_copy(src_ref=s, dst_ref=d, send_sem=send.at[4 * t + k],
                                                    recv_sem=recv.at[4 * t + k], device_id=(x, y, 1 - c),
                                                    device_id_type=MESH)
            cp = make(own[t], buf[t].at[c, my_chip], 0)
            cp.start()
            sent.append(cp)
            for k, pc in enumerate(chips):
                cp = make(buf[t].at[c, pc], buf[t].at[c, pc], 1 + k)
                cp.start()
                sent.append(cp)
        for t in range(n):
            for k in range(4):
                got = buf[t].at[1 - c, k]
                pltpu.make_async_remote_copy(src_ref=got, dst_ref=got, send_sem=send.at[4 * t + k],
                                             recv_sem=recv.at[4 * t + k], device_id=(x, y, 1 - c),
                                             device_id_type=MESH).wait_recv()
        for cp in sent:
            cp.wait_send()

    return pl.pallas_call(
        body, name=name, in_specs=[ANY] * (2 * n), out_specs=[ANY] * n,
        out_shape=[_sds(a.shape, a.dtype) for a in lands], input_output_aliases={n + t: t for t in range(n)},
        scratch_shapes=[pltpu.SemaphoreType.DMA((4 * n,)), pltpu.SemaphoreType.DMA((4 * n,)),
                        pltpu.SemaphoreType.DMA((1,))],
    )(*blocks, *lands)


@jax.custom_vjp
def _build_w_in(w4):
    full = w4.reshape(-1, w4.shape[-1])
    parts = []
    for _, start, width, wp in PROJ_SEGS:
        if width:
            parts.append(full[start:start + width])
        if wp > width:
            parts.append(jnp.zeros((wp - width, full.shape[1]), full.dtype))
    return jnp.concatenate(parts, axis=0)


def _build_w_in_f(w4):
    return _build_w_in(w4), None


def _build_w_in_b(_, g):
    parts, at = [], 0
    for _, _, width, wp in PROJ_SEGS:
        if width:
            parts.append(g[at:at + width])
        at += wp
    return (jnp.concatenate(parts, axis=0).reshape(N_CHIPS, -1, g.shape[1]),)


_build_w_in.defvjp(_build_w_in_f, _build_w_in_b)


def _split_w_uq(w):
    w3 = w.reshape(w.shape[0], MLA_HEADS, MLA_NOPE + MLA_ROPE)
    return w3[:, :, :MLA_NOPE].reshape(w.shape[0], -1), w3[:, :, MLA_NOPE:].reshape(w.shape[0], -1)


def _swap_halves(t, width):
    t3 = t.reshape(t.shape[0], -1, 2, width // 2)
    return jnp.concatenate([t3[:, :, 1:], t3[:, :, :1]], axis=2).reshape(t.shape)


def _pad_heads(t, width):
    t3 = t.reshape(t.shape[0], -1, width)
    t3 = jnp.pad(t3, ((0, 0), (0, 0), (0, HEAD_LANES - width)))
    return t3.reshape(t.shape[0], -1).astype(BF16)


def _layer(xh, mod, big, small, rope_q, rope_k):
    d = D_MODEL
    shift, scale, gate = mod[None, 0:d], mod[None, d:2 * d], mod[None, 2 * d:3 * d]
    w_al = _build_w_in(big["w_in"])
    proj = mod_mm(xh, small["norm_g"][None], scale, shift, w_al)
    gq, gk, gv, glr, mq, mkv, mkr, cb, cc, cx, _, z = split_proj(proj)

    rk = GLA_RANK
    hk = GLA_HEADS * GLA_DK
    wg = jnp.zeros((128, 2 * hk), F32)
    wg = wg.at[0:rk, 0:hk].set(small["gla_wg_f"]).at[rk:2 * rk, hk:].set(small["gla_wg_b"])
    bg = jnp.concatenate([small["gla_bg_f"], small["gla_bg_b"]])[None]
    la = gate_act(mm(glr, wg), bg)
    o_gla = rmsnorm(gla(gq, gk, gv, la), small["gla_norm_g"][None])

    cq = rmsnorm(mq, small["mla_q_norm_g"][None])
    w_nope, w_rope = _split_w_uq(jnp.concatenate([big["w_uq"][j] for j in range(N_CHIPS)], axis=1))
    qn = mm16(cq, w_nope)
    qr = mm(cq, w_rope)
    qr = fma(qr, rope_q[0], _swap_halves(qr, MLA_ROPE), rope_q[1])
    ckv = rmsnorm(mkv, small["mla_kv_norm_g"][None])
    kv = mm16(ckv, jnp.concatenate([big["w_ukv"][j] for j in range(N_CHIPS)], axis=1))
    kr = mkr[:, :MLA_ROPE]
    kr = fma(kr, rope_k[0], _swap_halves(kr, MLA_ROPE), rope_k[1])
    o_mla = rmsnorm(attn(qn, _pad_heads(qr, MLA_ROPE), kv, _pad_heads(kr, MLA_ROPE)), small["mla_out_g"][None])

    cw = jnp.concatenate([small["conv_w"], jnp.zeros((5, CONV_CH), F32)], axis=0)
    o_conv = rmsnorm(conv_op(cb, cc, cx, cw), small["conv_out_g"][None])

    o = jnp.concatenate([o_gla, o_mla, o_conv], axis=1)
    w_out = big["w_out"].reshape(d, d)
    return out_block(o, z, w_out, xh, gate)


SMALL_REPL = ("norm_g", "gla_bg_f", "gla_bg_b", "gla_norm_g", "mla_q_norm_g", "mla_kv_norm_g", "mla_out_g",
              "conv_out_g")
SMALL_SHARDED = ("gla_wg_f", "gla_wg_b", "conv_w")
BIG = ("w_in", "w_out", "w_uq", "w_ukv")
HALF_AXIS = (1, 0, 0, 0)


def kernel(x, c, positions, ada_w, ada_b, norm_g, w_in, gla_wg_f, gla_bg_f, gla_wg_b, gla_bg_b, gla_norm_g, mla_q_norm_g, mla_kv_norm_g, mla_w_uq, mla_w_ukv, mla_out_g, conv_w, conv_out_g, w_out, final_g, loss_target, m_ada_w, m_ada_b, m_norm_g, m_w_in, m_gla_wg_f, m_gla_bg_f, m_gla_wg_b, m_gla_bg_b, m_gla_norm_g, m_mla_q_norm_g, m_mla_kv_norm_g, m_mla_w_uq, m_mla_w_ukv, m_mla_out_g, m_conv_w, m_conv_out_g, m_w_out, m_final_g, v_ada_w, v_ada_b, v_norm_g, v_w_in, v_gla_wg_f, v_gla_bg_f, v_gla_wg_b, v_gla_bg_b, v_gla_norm_g, v_mla_q_norm_g, v_mla_kv_norm_g, v_mla_w_uq, v_mla_w_ukv, v_mla_out_g, v_conv_w, v_conv_out_g, v_w_out, v_final_g):
    xi, yi, ci = _place()
    chip = 2 * xi + yi
    dev = 2 * chip + ci
    s = x.shape[1]
    d = D_MODEL
    weights = dict(ada_w=ada_w, ada_b=ada_b, norm_g=norm_g, w_in=w_in, gla_wg_f=gla_wg_f, gla_bg_f=gla_bg_f,
                   gla_wg_b=gla_wg_b, gla_bg_b=gla_bg_b, gla_norm_g=gla_norm_g, mla_q_norm_g=mla_q_norm_g,
                   mla_kv_norm_g=mla_kv_norm_g, mla_w_uq=mla_w_uq, mla_w_ukv=mla_w_ukv, mla_out_g=mla_out_g,
                   conv_w=conv_w, conv_out_g=conv_out_g, w_out=w_out, final_g=final_g)
    m_in = dict(ada_w=m_ada_w, ada_b=m_ada_b, norm_g=m_norm_g, w_in=m_w_in, gla_wg_f=m_gla_wg_f, gla_bg_f=m_gla_bg_f,
                gla_wg_b=m_gla_wg_b, gla_bg_b=m_gla_bg_b, gla_norm_g=m_gla_norm_g, mla_q_norm_g=m_mla_q_norm_g,
                mla_kv_norm_g=m_mla_kv_norm_g, mla_w_uq=m_mla_w_uq, mla_w_ukv=m_mla_w_ukv, mla_out_g=m_mla_out_g,
                conv_w=m_conv_w, conv_out_g=m_conv_out_g, w_out=m_w_out, final_g=m_final_g)
    v_in = dict(ada_w=v_ada_w, ada_b=v_ada_b, norm_g=v_norm_g, w_in=v_w_in, gla_wg_f=v_gla_wg_f, gla_bg_f=v_gla_bg_f,
                gla_wg_b=v_gla_wg_b, gla_bg_b=v_gla_bg_b, gla_norm_g=v_gla_norm_g, mla_q_norm_g=v_mla_q_norm_g,
                mla_kv_norm_g=v_mla_kv_norm_g, mla_w_uq=v_mla_w_uq, mla_w_ukv=v_mla_w_ukv, mla_out_g=v_mla_out_g,
                conv_w=v_conv_w, conv_out_g=v_conv_out_g, w_out=v_w_out, final_g=v_final_g)

    g_c, g_wgf, g_wgb, g_cw = all_gather8([c, gla_wg_f, gla_wg_b, conv_w], "gather_small")

    def unshard_cols(g):
        g4 = g[0::2]
        return g4.transpose(1, 2, 0, 3).reshape(g4.shape[1], g4.shape[2], -1)

    small_full = dict(gla_wg_f=unshard_cols(g_wgf), gla_wg_b=unshard_cols(g_wgb), conv_w=unshard_cols(g_cw))
    for nme in SMALL_REPL:
        small_full[nme] = weights[nme]
    smalls = [{nme: small_full[nme][l] for nme in SMALL_REPL + SMALL_SHARDED} for l in range(DEPTH)]

    big_src = (jnp.swapaxes(w_in, 1, 2), w_out, mla_w_uq, mla_w_ukv)

    def my_half(t, a):
        n_half = a.shape[HALF_AXIS[t]] // 2
        return lax.dynamic_slice_in_dim(a, ci * n_half, n_half, axis=HALF_AXIS[t]).astype(BF16)

    halves = [[my_half(t, a[l]) for t, a in enumerate(big_src)] for l in range(DEPTH)]

    def landing(blocks):
        return [lax.empty((2, N_CHIPS) + b.shape, b.dtype) for b in blocks]

    def finish_gather(handle, after, tag):
        blocks, lands = ici_wait(handle, after, "gather", "gather_wait" + tag)
        lands = gather_share(blocks, lands, "gather_share" + tag)
        return [lax.dynamic_update_slice(g, b[None, None], (ci, chip) + (0,) * b.ndim) for g, b in zip(lands, blocks)]

    started0 = ici_start(halves[0], landing(halves[0]), "gather", "gather_start0")

    c_act = _silu_rows(g_c[:, 0, :])
    c_act16 = jnp.concatenate([c_act, jnp.zeros_like(c_act)], axis=0)
    n_ada = ada_w.shape[2]
    parts = []
    for l in range(DEPTH):
        bias = lax.dynamic_slice_in_dim(ada_b[l], chip * n_ada, n_ada)[None]
        parts.append(_mm(c_act16, ada_w[l], bias=bias, name="ada_fwd"))
    g_mod, = all_gather8([jnp.stack(parts)], "gather_mod")
    mod_mine = lax.dynamic_index_in_dim(g_mod[0::2], dev, 2, keepdims=False)
    mods = mod_mine.transpose(1, 0, 2).reshape(DEPTH, 3 * d)

    inv_freq = ROPE_THETA ** (-jnp.arange(0, MLA_ROPE, 2, dtype=F32) / MLA_ROPE)
    ang = positions[0].astype(F32)[:, None] * inv_freq
    cos, sin = jnp.cos(ang), jnp.sin(ang)
    rope_k = (jnp.concatenate([cos, cos], axis=1), jnp.concatenate([-sin, sin], axis=1))
    rope_q = (jnp.tile(rope_k[0], (1, MLA_HEADS)), jnp.tile(rope_k[1], (1, MLA_HEADS)))

    def run_layer(xh, mod, gathered, small):
        big = {nme: jnp.concatenate([g[0], g[1]], axis=HALF_AXIS[t] + 1) for t, (nme, g) in enumerate(zip(BIG, gathered))}
        return _layer(xh, mod, big, small, rope_q, rope_k)

    def head(hh, fg):
        return loss_op(rmsnorm(hh, fg[None]), loss_target[0])[0, 0]

    gathered0 = finish_gather(started0, mods, "0")
    blocks1, _ = lax.optimization_barrier((halves[1], gathered0[0]))
    started1 = ici_start(blocks1, landing(blocks1), "gather", "gather_start1")
    x0, _ = lax.optimization_barrier((x[0], started1["token"]))
    h1, vjp0 = jax.vjp(run_layer, x0, mods[0], gathered0, smalls[0])
    gathered1 = finish_gather(started1, h1, "1")
    h2, vjp1 = jax.vjp(run_layer, h1, mods[1], gathered1, smalls[1])
    loss_dev, vjp_head = jax.vjp(head, h2, final_g)
    loss = lax.psum(loss_dev, ("x", "y", "c"))
    dh2, dfinal = vjp_head(jnp.ones((), F32))

    c_idx = jnp.reshape(ci, (1,)).astype(jnp.int32)
    chip_idx = jnp.reshape(chip, (1,)).astype(jnp.int32)

    def reduce_begin(dgath, tag):
        theirs = sibling_send(dgath, "reduce_sibling" + tag)
        pair = [_add_pair(a, b, c_idx) for a, b in zip(dgath, theirs)]
        lands = [lax.empty((N_CHIPS - 1,) + p.shape[1:], p.dtype) for p in pair]
        return ici_start(pair, lands, "reduce", "reduce_start" + tag)

    def reduce_end(handle, after, tag):
        pair, landed = ici_wait(handle, after, "reduce", "reduce_wait" + tag)
        reduced = [_add_chips(p, q, chip_idx) for p, q in zip(pair, landed)]
        others = sibling_swap(reduced, "share_sibling" + tag)
        return [jnp.where(ci == 0, jnp.concatenate([own, other], axis=HALF_AXIS[t]),
                          jnp.concatenate([other, own], axis=HALF_AXIS[t]))
                for t, (own, other) in enumerate(zip(reduced, others))]

    dh1, dmod1, dgath1, dsmall1 = vjp1(dh2)
    reducing1 = reduce_begin(dgath1, "1")
    dh1, _ = lax.optimization_barrier((dh1, reducing1["token"]))
    dx, dmod0, dgath0, dsmall0 = vjp0(dh1)
    big_grads1 = reduce_end(reducing1, dx, "1")
    reducing0 = reduce_begin(dgath0, "0")
    dmods = jnp.stack([dmod0, dmod1])
    dsmalls = [dsmall0, dsmall1]

    pieces = [dmods.reshape(-1), dfinal]
    for nme in SMALL_REPL + SMALL_SHARDED:
        pieces.append(jnp.stack([dsmalls[l][nme] for l in range(DEPTH)]).reshape(-1))
    sizes = [p.shape[0] for p in pieces]
    flat = jnp.concatenate(pieces)
    padn = (-flat.shape[0]) % 128
    flat = jnp.concatenate([flat, jnp.zeros((padn,), F32)])[None]
    flat, _ = lax.optimization_barrier((flat, reducing0["token"]))
    g_small, = all_gather8([flat], "gather_small_grads")
    total = _sum_devices(g_small)[0]
    offs, at = [], 0
    for n_el in sizes:
        offs.append(at)
        at += n_el

    def piece(i, shape):
        return total[offs[i]:offs[i] + sizes[i]].reshape(shape)

    grads = {"ada_b": piece(0, (DEPTH, 3 * d)), "final_g": piece(1, (d,))}
    for i, nme in enumerate(SMALL_REPL + SMALL_SHARDED):
        full = piece(2 + i, small_full[nme].shape)
        if nme in SMALL_SHARDED:
            ncol = weights[nme].shape[2]
            full = lax.dynamic_slice_in_dim(full, chip * ncol, ncol, axis=2)
        grads[nme] = full

    dmod_all = g_small[:, 0, :DEPTH * 3 * d].reshape(N_DEV, DEPTH, 3 * d)
    dmod_cols = lax.dynamic_slice_in_dim(dmod_all, chip * n_ada, n_ada, axis=2)
    g_ada = []
    for l in range(DEPTH):
        dm16 = jnp.concatenate([dmod_cols[:, l], jnp.zeros((N_DEV, n_ada), F32)], axis=0)
        g_ada.append(_mm(c_act16, dm16, ta=True, name="ada_bwd"))
    grads["ada_w"] = jnp.stack(g_ada)

    order = list(weights)
    big_names = ("w_in", "w_out", "mla_w_uq", "mla_w_ukv")
    delta, new_m, new_v = {}, {}, {}
    for nme in order:
        if nme not in big_names:
            delta[nme], new_m[nme], new_v[nme] = _adamw(weights[nme], grads[nme], m_in[nme], v_in[nme])
    big_grads0 = reduce_end(reducing0, delta["ada_w"], "0")
    for nme, g0, g1 in zip(big_names, big_grads0, big_grads1):
        grads[nme] = jnp.stack([g0, g1])
    for nme in big_names:
        if nme == "w_in":
            w_t, m_t, v_t = (jnp.swapaxes(t, 1, 2) for t in (w_in, m_w_in, v_w_in))
            res = _adamw(w_t, grads[nme], m_t, v_t)
            delta[nme], new_m[nme], new_v[nme] = (jnp.swapaxes(t, 1, 2) for t in res)
            grads[nme] = jnp.swapaxes(grads[nme], 1, 2)
            continue
        delta[nme], new_m[nme], new_v[nme] = _adamw(weights[nme], grads[nme], m_in[nme], v_in[nme])
    return (loss, dx[None], *[grads[n_] for n_ in order], *[delta[n_] for n_ in order],
            *[new_m[n_] for n_ in order], *[new_v[n_] for n_ in order])
```

```python
import functools

import jax
import jax.numpy as jnp
from jax import lax
from jax.experimental import pallas as pl
from jax.experimental.pallas import tpu as pltpu

F32 = jnp.float32
BF16 = jnp.bfloat16
MESH = pl.DeviceIdType.MESH
HIGHEST = lax.Precision.HIGHEST

DEPTH = 2
D_MODEL = 2048
GLA_HEADS = 6
GLA_DK = 64
GLA_DV = 128
GLA_RANK = 16
GLA_TEMP = 16.0
GLA_CHUNK = 64
GLA_W = GLA_HEADS * GLA_DV
MLA_HEADS = 6
MLA_QL = 384
MLA_KVL = 256
MLA_NOPE = 128
MLA_ROPE = 64
MLA_DV = 128
MLA_W = MLA_HEADS * MLA_DV
CONV_CH = D_MODEL - GLA_W - MLA_W
ROPE_THETA = 10000.0
EPS = 1e-6
IN_DIM = 5856
N_CHIPS = 4
N_DEV = 8

ADAM_LR = 0.001
ADAM_B1 = 0.9
ADAM_B2 = 0.999
ADAM_EPS = 1e-08
ADAM_WD = 0.01
ADAM_STEP = 10

PROJ_SEGS = (
    ("gq", 0, 384, 384), ("gk", 384, 384, 384), ("gv", 768, 768, 768), ("glr", 1536, 32, 128),
    ("mq", 1568, 384, 384), ("mkv", 1952, 256, 256), ("mkr", 2208, 64, 128),
    ("cb", 2272, 512, 512), ("cc", 2784, 512, 512), ("cx", 3296, 512, 512),
    ("pad", 3808, 0, 128), ("z", 3808, 2048, 2048),
)
PROJ_AL = sum(s[3] for s in PROJ_SEGS)

VMEM_LIMIT = 48 * 1024 * 1024
BLOCK_BYTES = 2 * 1024 * 1024


def _params(sem=None):
    return pltpu.CompilerParams(dimension_semantics=sem, vmem_limit_bytes=VMEM_LIMIT)


def _dot(a, b, ca, cb, precision=None):
    return lax.dot_general(a, b, (((ca,), (cb,)), ((), ())), preferred_element_type=F32, precision=precision)


def _tile(dim, prefs):
    for t in prefs:
        if dim % t == 0:
            return t
    return dim


def _pick_rows(rows, width, itemsize=4):
    for t in (2048, 1024, 512, 256, 128, 64, 32, 16, 8):
        if rows % t == 0 and t * width * itemsize <= BLOCK_BYTES:
            return t
    return rows


def _mm(a, b, *, ta=False, tb=False, bias=None, out_dtype=F32, name="mm"):
    if ta:
        K, M = a.shape
    else:
        M, K = a.shape
    if tb:
        N, Kb = b.shape
    else:
        Kb, N = b.shape
    assert K == Kb, (a.shape, b.shape, ta, tb)
    tm = _tile(M, (512, 256, 128))
    tn = _tile(N, (1024, 512, 384, 256, 128))
    tk = _tile(K, (2048, 1024, 512, 256, 128))
    nk = K // tk
    has_bias = bias is not None

    def body(*refs):
        a_ref, b_ref = refs[0], refs[1]
        bias_ref = refs[2] if has_bias else None
        o_ref = refs[3 if has_bias else 2]
        part = _dot(a_ref[...].astype(BF16), b_ref[...].astype(BF16), 0 if ta else 1, 1 if tb else 0)

        def finish(r):
            if has_bias:
                r = r + bias_ref[...]
            o_ref[...] = r.astype(out_dtype)

        if nk == 1:
            finish(part)
            return
        acc_ref = refs[-1]
        k = pl.program_id(2)

        @pl.when(k == 0)
        def _():
            acc_ref[...] = part

        @pl.when(k != 0)
        def _():
            acc_ref[...] += part

        @pl.when(k == nk - 1)
        def _():
            finish(acc_ref[...])

    a_spec = pl.BlockSpec((tk, tm), lambda i, j, k: (k, i)) if ta else pl.BlockSpec((tm, tk), lambda i, j, k: (i, k))
    b_spec = pl.BlockSpec((tn, tk), lambda i, j, k: (j, k)) if tb else pl.BlockSpec((tk, tn), lambda i, j, k: (k, j))
    in_specs = [a_spec, b_spec]
    args = [a, b]
    if has_bias:
        in_specs.append(pl.BlockSpec((1, tn), lambda i, j, k: (0, j)))
        args.append(bias)
    return pl.pallas_call(
        body, name=name, grid=(M // tm, N // tn, nk),
        in_specs=in_specs, out_specs=pl.BlockSpec((tm, tn), lambda i, j, k: (i, j)),
        out_shape=jax.ShapeDtypeStruct((M, N), out_dtype),
        scratch_shapes=[pltpu.VMEM((tm, tn), F32)] if nk > 1 else [],
        compiler_params=_params(("parallel", "parallel", "arbitrary")),
    )(*args)


@jax.custom_vjp
def mm(a, b):
    return _mm(a, b, name="mm_fwd")


def _mm_f(a, b):
    return _mm(a, b, name="mm_fwd"), (a, b)


def _mm_b(res, g):
    a, b = res
    return _mm(g, b, tb=True, out_dtype=a.dtype, name="mm_da"), _mm(a, g, ta=True, out_dtype=b.dtype, name="mm_db")


mm.defvjp(_mm_f, _mm_b)


@jax.custom_vjp
def mm16(a, b):
    return _mm(a, b, out_dtype=BF16, name="mm16_fwd")


def _mm16_f(a, b):
    return mm16(a, b), (a, b)


mm16.defvjp(_mm16_f, _mm_b)


def _rows(body, name, tiled, full, tiled_out, acc_out, tr=None):
    rows = tiled[0].shape[0]
    if tr is None:
        width = max([a.shape[1] for a in tiled] + [s.shape[1] for s in tiled_out])
        tr = _pick_rows(rows, width)
    in_specs = [pl.BlockSpec((tr, a.shape[1]), lambda i: (i, 0)) for a in tiled]
    in_specs += [pl.BlockSpec(a.shape, lambda i: (0, 0)) for a in full]
    out_specs = [pl.BlockSpec((tr, s.shape[1]), lambda i: (i, 0)) for s in tiled_out]
    out_specs += [pl.BlockSpec(s.shape, lambda i: (0, 0)) for s in acc_out]

    def wrapped(*refs):
        body(pl.program_id(0), *refs)

    outs = pl.pallas_call(
        wrapped, name=name, grid=(rows // tr,), in_specs=in_specs, out_specs=out_specs,
        out_shape=list(tiled_out) + list(acc_out),
        compiler_params=_params(("arbitrary",)),
    )(*tiled, *full)
    return outs


def _sds(shape, dtype=F32):
    return jax.ShapeDtypeStruct(tuple(shape), dtype)


def _acc(step, ref, val):
    @pl.when(step == 0)
    def _():
        ref[...] = val

    @pl.when(step != 0)
    def _():
        ref[...] += val


def _colsum(v):
    return jnp.sum(v, axis=0, keepdims=True)


def _rstd(x):
    return lax.rsqrt(jnp.mean(x * x, axis=-1, keepdims=True) + EPS)


def _norm_grid(x, g):
    rows, w = x.shape[0], g.shape[1]
    tr = _pick_rows(rows, w)
    blk = pl.BlockSpec((tr, w), lambda i, j: (i, j))
    gblk = pl.BlockSpec((1, w), lambda i, j: (0, 0))
    return (rows // tr, x.shape[1] // w), blk, gblk


@jax.custom_vjp
def rmsnorm(x, g):
    def body(x_ref, g_ref, o_ref):
        x = x_ref[...]
        o_ref[...] = x * _rstd(x) * g_ref[...]

    grid, blk, gblk = _norm_grid(x, g)
    return pl.pallas_call(body, name="rmsnorm_fwd", grid=grid, in_specs=[blk, gblk], out_specs=blk,
                          out_shape=_sds(x.shape), compiler_params=_params(("parallel", "parallel")))(x, g)


def _rmsnorm_f(x, g):
    return rmsnorm(x, g), (x, g)


def _rmsnorm_b(res, dy):
    x, g = res

    def body(x_ref, dy_ref, g_ref, dx_ref, dg_ref):
        x = x_ref[...]
        dy = dy_ref[...]
        r = _rstd(x)
        xh = x * r
        dxh = dy * g_ref[...]
        dx_ref[...] = r * (dxh - xh * jnp.mean(dxh * xh, axis=-1, keepdims=True))
        first = jnp.logical_and(pl.program_id(0) == 0, pl.program_id(1) == 0)
        _acc(jnp.where(first, 0, 1), dg_ref, _colsum(dy * xh))

    grid, blk, gblk = _norm_grid(x, g)
    dx, dg = pl.pallas_call(body, name="rmsnorm_bwd", grid=grid, in_specs=[blk, blk, gblk], out_specs=[blk, gblk],
                            out_shape=[_sds(x.shape), _sds(g.shape)],
                            compiler_params=_params(("arbitrary", "arbitrary")))(x, dy, g)
    return dx, dg


rmsnorm.defvjp(_rmsnorm_f, _rmsnorm_b)


def _modulate(x, g, scale, shift):
    def body(i, x_ref, g_ref, sc_ref, sh_ref, o_ref):
        x = x_ref[...]
        xn = x * _rstd(x) * g_ref[...]
        o_ref[...] = (xn * (1.0 + sc_ref[...]) + sh_ref[...]).astype(BF16)
    return _rows(body, "modulate_fwd", [x], [g, scale, shift], [_sds(x.shape, BF16)], [])[0]


def _modulate_bwd(x, g, scale, shift, dh):
    def body(i, x_ref, dh_ref, g_ref, sc_ref, dx_ref, dg_ref, dsc_ref, dsh_ref):
        x = x_ref[...]
        dh = dh_ref[...]
        gv = g_ref[...]
        r = _rstd(x)
        xh = x * r
        dxn = dh * (1.0 + sc_ref[...])
        dxh = dxn * gv
        dx_ref[...] = r * (dxh - xh * jnp.mean(dxh * xh, axis=-1, keepdims=True))
        _acc(i, dg_ref, _colsum(dxn * xh))
        _acc(i, dsc_ref, _colsum(dh * (xh * gv)))
        _acc(i, dsh_ref, _colsum(dh))

    v = _sds(g.shape)
    return _rows(body, "modulate_bwd", [x, dh], [g, scale], [_sds(x.shape)], [v, v, v])


@jax.custom_vjp
def mod_mm(x, g, scale, shift, wt):
    return _mm(_modulate(x, g, scale, shift), wt, tb=True, name="mm_in")


def _mod_mm_f(x, g, scale, shift, wt):
    h = _modulate(x, g, scale, shift)
    return _mm(h, wt, tb=True, name="mm_in"), (x, g, scale, shift, wt, h)


def _mod_mm_b(res, dproj):
    x, g, scale, shift, wt, h = res
    dproj = dproj.astype(BF16)
    dh = _mm(dproj, wt, name="mm_in_dh")
    dwt = _mm(dproj, h, ta=True, out_dtype=wt.dtype, name="mm_in_dw")
    dx, dg, dsc, dsh = _modulate_bwd(x, g, scale, shift, dh)
    return dx, dg, dsc, dsh, dwt


mod_mm.defvjp(_mod_mm_f, _mod_mm_b)


def _sigmoid(z):
    return 1.0 / (1.0 + jnp.exp(-z))


def _gate_mul(o, z):
    def body(i, o_ref, z_ref, y_ref):
        z = z_ref[...]
        y_ref[...] = (o_ref[...] * (z * _sigmoid(z))).astype(BF16)
    return _rows(body, "gate_mul_fwd", [o, z], [], [_sds(o.shape, BF16)], [])[0]


def _gate_mul_bwd(o, z, dy):
    def body(i, o_ref, z_ref, dy_ref, do_ref, dz_ref):
        z = z_ref[...]
        dy = dy_ref[...]
        s = _sigmoid(z)
        do_ref[...] = dy * (z * s)
        dz_ref[...] = dy * o_ref[...] * (s * (1.0 + z * (1.0 - s)))
    return _rows(body, "gate_mul_bwd", [o, z, dy], [], [_sds(o.shape), _sds(o.shape)], [])


def _residual(x, u, gate):
    def body(i, x_ref, u_ref, g_ref, o_ref):
        o_ref[...] = x_ref[...] + g_ref[...] * u_ref[...]
    return _rows(body, "residual_fwd", [x, u], [gate], [_sds(x.shape)], [])[0]


def _residual_bwd(d, u, gate):
    def body(i, d_ref, u_ref, g_ref, du_ref, dg_ref):
        d = d_ref[...]
        du_ref[...] = (g_ref[...] * d).astype(BF16)
        _acc(i, dg_ref, _colsum(d * u_ref[...]))

    return _rows(body, "residual_bwd", [d, u], [gate], [_sds(u.shape, BF16)], [_sds(gate.shape)])


@jax.custom_vjp
def out_block(o, z, w, x, gate):
    return _residual(x, _mm(_gate_mul(o, z), w, name="mm_out"), gate)


def _out_block_f(o, z, w, x, gate):
    y = _gate_mul(o, z)
    u = _mm(y, w, name="mm_out")
    return _residual(x, u, gate), (o, z, w, y, u, gate)


def _out_block_b(res, d):
    o, z, w, y, u, gate = res
    du, dgate = _residual_bwd(d, u, gate)
    dy = _mm(du, w, tb=True, name="mm_out_dy")
    dw = _mm(y, du, ta=True, out_dtype=w.dtype, name="mm_out_dw")
    do, dz = _gate_mul_bwd(o, z, dy)
    return do, dz, dw, d, dgate


out_block.defvjp(_out_block_f, _out_block_b)


@jax.custom_vjp
def gate_act(u, b):
    def body(i, u_ref, b_ref, o_ref):
        t = u_ref[...] + b_ref[...]
        o_ref[...] = (jnp.minimum(t, 0.0) - jnp.log(1.0 + jnp.exp(-jnp.abs(t)))) / GLA_TEMP
    return _rows(body, "gate_act_fwd", [u], [b], [_sds(u.shape)], [])[0]


def _gate_act_f(u, b):
    return gate_act(u, b), (u, b)


def _gate_act_b(res, d):
    u, b = res

    def body(i, u_ref, d_ref, b_ref, du_ref, db_ref):
        t = u_ref[...] + b_ref[...]
        du = d_ref[...] * _sigmoid(-t) / GLA_TEMP
        du_ref[...] = du
        _acc(i, db_ref, _colsum(du))

    du, db = _rows(body, "gate_act_bwd", [u, d], [b], [_sds(u.shape)], [_sds(b.shape)])
    return du, db


gate_act.defvjp(_gate_act_f, _gate_act_b)


@jax.custom_vjp
def fma(a, b, c, d):
    def body(i, a_ref, b_ref, c_ref, d_ref, o_ref):
        o_ref[...] = a_ref[...] * b_ref[...] + c_ref[...] * d_ref[...]
    return _rows(body, "fma_fwd", [a, b, c, d], [], [_sds(a.shape)], [])[0]


def _fma_f(a, b, c, d):
    return fma(a, b, c, d), (b, d)


def _fma_b(res, g):
    b, d = res

    def body(i, g_ref, b_ref, d_ref, da_ref, dc_ref):
        g = g_ref[...]
        da_ref[...] = g * b_ref[...]
        dc_ref[...] = g * d_ref[...]

    da, dc = _rows(body, "fma_bwd", [g, b, d], [], [_sds(g.shape), _sds(g.shape)], [])
    return da, jnp.zeros_like(b), dc, jnp.zeros_like(d)


fma.defvjp(_fma_f, _fma_b)


def _silu_rows(c):
    def body(i, c_ref, o_ref):
        v = c_ref[...]
        o_ref[...] = v * _sigmoid(v)
    return _rows(body, "silu", [c], [], [_sds(c.shape)], [])[0]


@jax.custom_vjp
def loss_op(y, t):
    return _loss_fwd(y, t)[0]


def _loss_fwd(y, t):
    inv = 1.0 / y.shape[1]

    def body(i, y_ref, t_ref, d_ref, l_ref):
        e = y_ref[...] - t_ref[...]
        d_ref[...] = e * inv
        _acc(i, l_ref, jnp.sum(_colsum(e * e), axis=1, keepdims=True) * (0.5 * inv))

    d, l = _rows(body, "loss_fwd", [y, t], [], [_sds(y.shape)], [_sds((1, 1))])
    return l, d


def _loss_f(y, t):
    l, d = _loss_fwd(y, t)
    return l, d


def _loss_b(d, g):
    return d * g, jnp.zeros_like(d)


loss_op.defvjp(_loss_f, _loss_b)


def _conv_terms(cc, cx, rows, n):
    u = cc * cx
    up = jnp.where(rows == 0, 0.0, pltpu.roll(u, 1, 0))
    un = jnp.where(rows == n - 1, 0.0, pltpu.roll(u, n - 1, 0))
    return u, up, un


CONV_COLS = 128


def _conv_specs(s, n_in):
    blk = pl.BlockSpec((s, CONV_COLS), lambda j: (0, j))
    wblk = pl.BlockSpec((8, CONV_COLS), lambda j: (0, j))
    return [blk] * n_in + [wblk], blk, wblk


@jax.custom_vjp
def conv_op(cb, cc, cx, w):
    s, ch = cb.shape

    def body(cb_ref, cc_ref, cx_ref, w_ref, o_ref):
        rows = lax.broadcasted_iota(jnp.int32, (s, CONV_COLS), 0)
        u, up, un = _conv_terms(cc_ref[...], cx_ref[...], rows, s)
        conv = up * w_ref[0:1, :] + u * w_ref[1:2, :] + un * w_ref[2:3, :]
        o_ref[...] = cb_ref[...] * conv

    in_specs, blk, _ = _conv_specs(s, 3)
    return pl.pallas_call(
        body, name="conv_fwd", grid=(ch // CONV_COLS,), in_specs=in_specs, out_specs=blk,
        out_shape=_sds(cb.shape), compiler_params=_params(("parallel",)),
    )(cb, cc, cx, w)


def _conv_f(cb, cc, cx, w):
    return conv_op(cb, cc, cx, w), (cb, cc, cx, w)


def _conv_b(res, d):
    cb, cc, cx, w = res
    s, ch = cb.shape

    def body(cb_ref, cc_ref, cx_ref, d_ref, w_ref, dcb_ref, dcc_ref, dcx_ref, dw_ref):
        rows = lax.broadcasted_iota(jnp.int32, (s, CONV_COLS), 0)
        cc_v = cc_ref[...]
        cx_v = cx_ref[...]
        u, up, un = _conv_terms(cc_v, cx_v, rows, s)
        w0, w1, w2 = w_ref[0:1, :], w_ref[1:2, :], w_ref[2:3, :]
        dv = d_ref[...]
        dcb_ref[...] = dv * (up * w0 + u * w1 + un * w2)
        dconv = dv * cb_ref[...]
        d_next = jnp.where(rows == s - 1, 0.0, pltpu.roll(dconv, s - 1, 0))
        d_prev = jnp.where(rows == 0, 0.0, pltpu.roll(dconv, 1, 0))
        du = w0 * d_next + w1 * dconv + w2 * d_prev
        dcc_ref[...] = du * cx_v
        dcx_ref[...] = du * cc_v
        dw_ref[...] = jnp.zeros_like(dw_ref)
        dw_ref[0:1, :] = _colsum(dconv * up)
        dw_ref[1:2, :] = _colsum(dconv * u)
        dw_ref[2:3, :] = _colsum(dconv * un)

    in_specs, blk, wblk = _conv_specs(s, 4)
    v = _sds(cb.shape)
    return tuple(pl.pallas_call(
        body, name="conv_bwd", grid=(ch // CONV_COLS,), in_specs=in_specs, out_specs=[blk, blk, blk, wblk],
        out_shape=[v, v, v, _sds(w.shape)], compiler_params=_params(("parallel",)),
    )(cb, cc, cx, d, w))


conv_op.defvjp(_conv_f, _conv_b)


def _gla_masks(rev):
    c = GLA_CHUNK
    row = lax.broadcasted_iota(jnp.int32, (c, c), 0)
    col = lax.broadcasted_iota(jnp.int32, (c, c), 1)
    mask = (row < col) if rev else (row >= col)
    return rev, mask


def _chunk_cumsum(g, rev):
    c = g.shape[0]
    row = lax.broadcasted_iota(jnp.int32, g.shape, 0)
    b = g
    s = 1
    while s < c:
        if rev:
            b = b + jnp.where(row < c - s, pltpu.roll(b, c - s, 0), 0.0)
        else:
            b = b + jnp.where(row >= s, pltpu.roll(b, s, 0), 0.0)
        s *= 2
    return b


GLA_UNROLL = 4


def _gla_rows(n):
    return pl.ds(pl.multiple_of(n * GLA_CHUNK, GLA_CHUNK), GLA_CHUNK)


def _gla_scan(s_ref, bt_ref, st_ref, n_chunks, descending):
    st_ref[...] = jnp.zeros_like(st_ref)

    def step(i, carry):
        n = (n_chunks - 1 - i) if descending else i
        own = s_ref[n]
        st = st_ref[...]
        s_ref[n] = st
        st_ref[...] = st * jnp.exp(bt_ref[n]) + own
        return carry

    lax.fori_loop(0, n_chunks, step, 0)


GLA_PAIR = 2


def _gla_specs(s):
    dk, dv = GLA_DK, GLA_DV
    n_pairs = GLA_HEADS // GLA_PAIR
    blk_k = pl.BlockSpec((s, GLA_PAIR * dk), lambda p: (0, p))
    blk_gb = pl.BlockSpec((s, GLA_PAIR * dk), lambda p: (0, n_pairs + p))
    blk_v = pl.BlockSpec((s, GLA_PAIR * dv), lambda p: (0, p))
    return n_pairs, blk_k, blk_gb, blk_v


def _head_lanes(hh):
    lane = lax.broadcasted_iota(jnp.int32, (1, GLA_PAIR * GLA_DK), 1)
    return jnp.logical_and(lane >= hh * GLA_DK, lane < (hh + 1) * GLA_DK)


def _gla_fwd(q, k, v, la):
    s = q.shape[0]
    dk, dv = GLA_DK, GLA_DV
    pw = GLA_PAIR * dk
    n_chunks = s // GLA_CHUNK
    scale = GLA_DK ** -0.5

    def body(q_ref, k_ref, v_ref, gf_ref, gb_ref, o_ref, sf_ref, sb_ref, bf_ref, bb_ref, btf_ref, btb_ref, st_ref):
        masks = [_gla_masks(rev) for rev in (False, True)]
        dirs = ((False, gf_ref, sf_ref, bf_ref, btf_ref), (True, gb_ref, sb_ref, bb_ref, btb_ref))

        def decays(n, carry):
            rows = _gla_rows(n)
            for rev, g_ref, _, b_ref, bt_ref in dirs:
                g = g_ref[rows, :]
                b_ref[rows, :] = _chunk_cumsum(g, rev)
                bt_ref[n] = _colsum(g)
            return carry

        lax.fori_loop(0, n_chunks, decays, 0, unroll=GLA_UNROLL)
        for hh in range(GLA_PAIR):
            m = _head_lanes(hh)
            vl = slice(hh * dv, (hh + 1) * dv)

            def prepare(n, carry, m=m, vl=vl):
                rows = _gla_rows(n)
                kk = k_ref[rows, :]
                vb = v_ref[rows, vl].astype(BF16)
                for rev, _, s_ref, b_ref, bt_ref in dirs:
                    ke = jnp.where(m, kk * jnp.exp(bt_ref[n] - b_ref[rows, :]), 0.0).astype(BF16)
                    s_ref[n] = _dot(vb, ke, 0, 0)
                return carry

            lax.fori_loop(0, n_chunks, prepare, 0, unroll=GLA_UNROLL)
            for rev, _, s_ref, _, bt_ref in dirs:
                _gla_scan(s_ref, bt_ref, st_ref, n_chunks, descending=rev)

            def emit(n, carry, m=m, vl=vl):
                rows = _gla_rows(n)
                qs = q_ref[rows, :] * scale
                kk = k_ref[rows, :]
                vb = v_ref[rows, vl].astype(BF16)
                o = None
                for (rev, _, s_ref, b_ref, _), (_, mask) in zip(dirs, masks):
                    b = b_ref[rows, :]
                    qd = jnp.where(m, qs * jnp.exp(b), 0.0).astype(BF16)
                    ki = jnp.where(m, kk * jnp.exp(-b), 0.0).astype(BF16)
                    a = jnp.where(mask, _dot(qd, ki, 1, 1), 0.0).astype(BF16)
                    od = _dot(a, vb, 1, 0) + _dot(qd, s_ref[n].astype(BF16), 1, 1)
                    o = od if o is None else o + od
                o_ref[rows, vl] = o
                return carry

            lax.fori_loop(0, n_chunks, emit, 0, unroll=GLA_UNROLL)

    n_pairs, blk_k, blk_gb, blk_v = _gla_specs(s)
    state = pltpu.VMEM((n_chunks, dv, pw), F32)
    scratch = [state, state, pltpu.VMEM((s, pw), F32), pltpu.VMEM((s, pw), F32), pltpu.VMEM((n_chunks, 1, pw), F32),
               pltpu.VMEM((n_chunks, 1, pw), F32), pltpu.VMEM((dv, pw), F32)]
    return pl.pallas_call(
        body, name="gla_fwd", grid=(n_pairs,), in_specs=[blk_k, blk_k, blk_v, blk_k, blk_gb],
        out_specs=blk_v, out_shape=_sds(v.shape), scratch_shapes=scratch,
        compiler_params=_params(("parallel",)),
    )(q, k, v, la, la)


def _gla_bwd(q, k, v, la, do):
    s = q.shape[0]
    dk, dv = GLA_DK, GLA_DV
    pw = GLA_PAIR * dk
    c = GLA_CHUNK
    n_chunks = s // c
    scale = GLA_DK ** -0.5

    def body(q_ref, k_ref, v_ref, gf_ref, gb_ref, do_ref, dq_ref, dk_ref, dv_ref, dgf_ref, dgb_ref,
             sf_ref, sb_ref, bf_ref, bb_ref, btf_ref, btb_ref, dsf_ref, dsb_ref, st_ref):
        masks = [_gla_masks(rev) for rev in (False, True)]
        rowc = lax.broadcasted_iota(jnp.int32, (c, pw), 0)
        dirs = ((False, gf_ref, sf_ref, bf_ref, btf_ref, dsf_ref, dgf_ref),
                (True, gb_ref, sb_ref, bb_ref, btb_ref, dsb_ref, dgb_ref))

        def decays(n, carry):
            rows = _gla_rows(n)
            for rev, g_ref, _, b_ref, bt_ref, _, _ in dirs:
                g = g_ref[rows, :]
                b_ref[rows, :] = _chunk_cumsum(g, rev)
                bt_ref[n] = _colsum(g)
            return carry

        lax.fori_loop(0, n_chunks, decays, 0, unroll=GLA_UNROLL)
        for hh in range(GLA_PAIR):
            m = _head_lanes(hh)
            vl = slice(hh * dv, (hh + 1) * dv)

            def prepare(n, carry, m=m, vl=vl):
                rows = _gla_rows(n)
                qs = q_ref[rows, :] * scale
                kk = k_ref[rows, :]
                vb = v_ref[rows, vl].astype(BF16)
                do_b = do_ref[rows, vl].astype(BF16)
                for rev, _, s_ref, b_ref, bt_ref, ds_ref, _ in dirs:
                    b = b_ref[rows, :]
                    ke = jnp.where(m, kk * jnp.exp(bt_ref[n] - b), 0.0).astype(BF16)
                    qd = jnp.where(m, qs * jnp.exp(b), 0.0).astype(BF16)
                    s_ref[n] = _dot(vb, ke, 0, 0)
                    ds_ref[n] = _dot(do_b, qd, 0, 0)
                return carry

            lax.fori_loop(0, n_chunks, prepare, 0, unroll=GLA_UNROLL)
            for rev, _, s_ref, _, bt_ref, ds_ref, _ in dirs:
                _gla_scan(s_ref, bt_ref, st_ref, n_chunks, descending=rev)
                _gla_scan(ds_ref, bt_ref, st_ref, n_chunks, descending=not rev)

            def emit(n, carry, m=m, vl=vl, first=(hh == 0)):
                rows = _gla_rows(n)
                qs = q_ref[rows, :] * scale
                kk = k_ref[rows, :]
                vb = v_ref[rows, vl].astype(BF16)
                do_b = do_ref[rows, vl].astype(BF16)
                dq = dkk = dvv = None
                for (rev, _, s_ref, b_ref, bt_ref, ds_ref, dg_ref), (_, mask) in zip(dirs, masks):
                    b = b_ref[rows, :]
                    bt = bt_ref[n]
                    eb = jnp.where(m, jnp.exp(b), 0.0)
                    enb = jnp.where(m, jnp.exp(-b), 0.0)
                    etb = jnp.where(m, jnp.exp(bt - b), 0.0)
                    ebt = jnp.exp(bt)
                    qd = qs * eb
                    ki = kk * enb
                    ke = kk * etb
                    qd_b, ki_b, ke_b = qd.astype(BF16), ki.astype(BF16), ke.astype(BF16)
                    st = s_ref[n]
                    dst = ds_ref[n]
                    dst_b = dst.astype(BF16)
                    a = jnp.where(mask, _dot(qd_b, ki_b, 1, 1), 0.0).astype(BF16)
                    da = jnp.where(mask, _dot(do_b, vb, 1, 1), 0.0).astype(BF16)
                    dv_d = _dot(a, do_b, 0, 0) + _dot(ke_b, dst_b, 1, 1)
                    dqd = _dot(da, ki_b, 1, 0) + _dot(do_b, st.astype(BF16), 1, 0)
                    dki = _dot(da, qd_b, 0, 0)
                    dke = _dot(vb, dst_b, 1, 0)
                    dbt = _colsum(st * dst) * ebt + _colsum(dke * ke)
                    db = dqd * qd - dki * ki - dke * ke
                    db = db + jnp.where(rowc == (0 if rev else c - 1), dbt, 0.0)
                    dg = _chunk_cumsum(db, not rev)
                    if first:
                        dg_ref[rows, :] = dg
                    else:
                        dg_ref[rows, :] += dg
                    dq_d = dqd * eb * scale
                    dk_d = dki * enb + dke * etb
                    dq = dq_d if dq is None else dq + dq_d
                    dkk = dk_d if dkk is None else dkk + dk_d
                    dvv = dv_d if dvv is None else dvv + dv_d
                if first:
                    dq_ref[rows, :] = dq
                    dk_ref[rows, :] = dkk
                else:
                    dq_ref[rows, :] += dq
                    dk_ref[rows, :] += dkk
                dv_ref[rows, vl] = dvv
                return carry

            lax.fori_loop(0, n_chunks, emit, 0, unroll=2)

    n_pairs, blk_k, blk_gb, blk_v = _gla_specs(s)
    vk, vv = _sds(q.shape), _sds(v.shape)
    state = pltpu.VMEM((n_chunks, dv, pw), F32)
    scratch = [state, state, pltpu.VMEM((s, pw), F32), pltpu.VMEM((s, pw), F32), pltpu.VMEM((n_chunks, 1, pw), F32),
               pltpu.VMEM((n_chunks, 1, pw), F32), state, state, pltpu.VMEM((dv, pw), F32)]
    return pl.pallas_call(
        body, name="gla_bwd", grid=(n_pairs,), in_specs=[blk_k, blk_k, blk_v, blk_k, blk_gb, blk_v],
        out_specs=[blk_k, blk_k, blk_v, blk_k, blk_k], out_shape=[vk, vk, vv, vk, vk],
        scratch_shapes=scratch, compiler_params=_params(("parallel",)),
    )(q, k, v, la, la, do)


@jax.custom_vjp
def gla(q, k, v, la):
    return _gla_fwd(q, k, v, la)


def _gla_f(q, k, v, la):
    return _gla_fwd(q, k, v, la), (q, k, v, la)


def _gla_b(res, do):
    dq, dk, dv, dgf, dgb = _gla_bwd(*res, do)
    return dq, dk, dv, jnp.concatenate([dgf, dgb], axis=1)


gla.defvjp(_gla_f, _gla_b)


ATTN_TQ = 256
HEAD_LANES = 128


def _attn_blocks(s, tq):
    per_q = pl.BlockSpec((tq, HEAD_LANES), lambda h, j: (j, h))
    k_nope = pl.BlockSpec((s, HEAD_LANES), lambda h, j: (0, 2 * h))
    v_blk = pl.BlockSpec((s, HEAD_LANES), lambda h, j: (0, 2 * h + 1))
    k_rope = pl.BlockSpec((s, HEAD_LANES), lambda h, j: (0, 0))
    lse = pl.BlockSpec((1, tq, 1), lambda h, j: (h, j, 0))
    return per_q, k_nope, v_blk, k_rope, lse


def _attn_fwd(qn, qr, kv, kr):
    s = qn.shape[0]
    tq = min(ATTN_TQ, s)
    scale = (MLA_NOPE + MLA_ROPE) ** -0.5

    def body(qn_ref, qr_ref, kn_ref, v_ref, kr_ref, o_ref, lse_ref):
        q = jnp.concatenate([qn_ref[...], qr_ref[...]], axis=1)
        k = jnp.concatenate([kn_ref[...], kr_ref[...]], axis=1)
        sc = _dot(q, k, 1, 1) * scale
        m = jnp.max(sc, axis=-1, keepdims=True)
        p = jnp.exp(sc - m)
        l = jnp.sum(p, axis=-1, keepdims=True)
        p = p * (1.0 / l)
        o_ref[...] = _dot(p.astype(BF16), v_ref[...], 1, 0)
        lse_ref[0] = m + jnp.log(l)

    per_q, k_nope, v_blk, k_rope, lse = _attn_blocks(s, tq)
    return pl.pallas_call(
        body, name="attn_fwd", grid=(MLA_HEADS, s // tq), in_specs=[per_q, per_q, k_nope, v_blk, k_rope],
        out_specs=[per_q, lse], out_shape=[_sds(qn.shape), _sds((MLA_HEADS, s, 1))],
        compiler_params=_params(("parallel", "parallel")),
    )(qn, qr, kv, kv, kr)


def _attn_bwd(qn, qr, kv, kr, o, lse, do):
    s = qn.shape[0]
    tq = min(ATTN_TQ, s)
    n_q = s // tq
    scale = (MLA_NOPE + MLA_ROPE) ** -0.5

    def body(qn_ref, qr_ref, kn_ref, v_ref, kr_ref, o_ref, lse_ref, do_ref, dqn_ref, dqr_ref, dkv_ref, dkr_ref,
             dk_acc, dv_acc, dkr_acc):
        h, j = pl.program_id(0), pl.program_id(1)
        q = jnp.concatenate([qn_ref[...], qr_ref[...]], axis=1)
        k = jnp.concatenate([kn_ref[...], kr_ref[...]], axis=1)
        do = do_ref[...]
        do_b = do.astype(BF16)
        p = jnp.exp(_dot(q, k, 1, 1) * scale - lse_ref[0])
        dp = _dot(do_b, v_ref[...], 1, 1)
        delta = jnp.sum(do * o_ref[...], axis=-1, keepdims=True)
        ds = (p * (dp - delta) * scale).astype(BF16)
        dq = _dot(ds, k, 1, 0)
        dqn_ref[...] = dq[:, :HEAD_LANES].astype(BF16)
        dqr_ref[...] = dq[:, HEAD_LANES:].astype(BF16)
        dk = _dot(ds, q, 0, 0)
        _acc(j, dk_acc, dk[:, :HEAD_LANES])
        _acc(j, dv_acc, _dot(p.astype(BF16), do_b, 0, 0))
        _acc(jnp.where(jnp.logical_and(h == 0, j == 0), 0, 1), dkr_acc, dk[:, HEAD_LANES:])

        @pl.when(j == n_q - 1)
        def _():
            dkv_ref[:, 0:HEAD_LANES] = dk_acc[...].astype(BF16)
            dkv_ref[:, HEAD_LANES:2 * HEAD_LANES] = dv_acc[...].astype(BF16)

        @pl.when(jnp.logical_and(h == MLA_HEADS - 1, j == n_q - 1))
        def _():
            dkr_ref[...] = dkr_acc[...].astype(BF16)

    per_q, k_nope, v_blk, k_rope, lse_blk = _attn_blocks(s, tq)
    dkv_blk = pl.BlockSpec((s, 2 * HEAD_LANES), lambda h, j: (0, h))
    acc = pltpu.VMEM((s, HEAD_LANES), F32)
    return pl.pallas_call(
        body, name="attn_bwd", grid=(MLA_HEADS, n_q),
        in_specs=[per_q, per_q, k_nope, v_blk, k_rope, per_q, lse_blk, per_q],
        out_specs=[per_q, per_q, dkv_blk, k_rope],
        out_shape=[_sds(qn.shape, BF16), _sds(qr.shape, BF16), _sds(kv.shape, BF16), _sds(kr.shape, BF16)],
        scratch_shapes=[acc, acc, acc], compiler_params=_params(("arbitrary", "arbitrary")),
    )(qn, qr, kv, kv, kr, o, lse, do)


@jax.custom_vjp
def attn(qn, qr, kv, kr):
    return _attn_fwd(qn, qr, kv, kr)[0]


def _attn_f(qn, qr, kv, kr):
    o, lse = _attn_fwd(qn, qr, kv, kr)
    return o, (qn, qr, kv, kr, o, lse)


def _attn_b(res, do):
    return tuple(_attn_bwd(*res, do))


attn.defvjp(_attn_f, _attn_b)


@jax.custom_vjp
def split_proj(proj):
    out, at = [], 0
    for _, _, _, wp in PROJ_SEGS:
        out.append(proj[:, at:at + wp])
        at += wp
    return tuple(out)


def _split_f(proj):
    return split_proj(proj), None


def _split_b(_, gs):
    return (jnp.concatenate(gs, axis=1),)


split_proj.defvjp(_split_f, _split_b)


def _tile2d(rows, width, limit=BLOCK_BYTES):
    fits = [t for t in range(16, rows + 1, 16) if rows % t == 0 and t * width * 4 <= limit]
    if fits and (fits[-1] >= 64 or fits[-1] == rows):
        return fits[-1], width
    if rows * width * 4 <= limit:
        return rows, width
    cols = [t for t in range(128, width + 1, 128) if width % t == 0 and rows * t * 4 <= limit]
    return (rows, cols[-1]) if cols else (rows, width)


def _add_pair(stacked, theirs, c_idx):
    g, r, w = theirs.shape
    tr, tc = _tile2d(r, w)

    def body(c_ref, a_ref, b_ref, o_ref):
        o_ref[0] = (a_ref[0, 0].astype(F32) + b_ref[0].astype(F32)).astype(BF16)

    blk = pl.BlockSpec((1, tr, tc), lambda k, i, j, c: (k, i, j))
    spec = pltpu.PrefetchScalarGridSpec(
        num_scalar_prefetch=1, grid=(g, r // tr, w // tc),
        in_specs=[pl.BlockSpec((1, 1, tr, tc), lambda k, i, j, c: (c[0], k, i, j)), blk], out_specs=blk)
    return pl.pallas_call(body, name="add_pair", grid_spec=spec, out_shape=_sds(theirs.shape, BF16),
                          compiler_params=_params(("parallel", "parallel", "parallel")))(c_idx, stacked, theirs)


def _add_chips(pair, landed, chip_idx):
    _, r, w = pair.shape
    tr, tc = _tile2d(r, w)

    def body(c_ref, p_ref, l0_ref, l1_ref, l2_ref, o_ref):
        o_ref[...] = ((p_ref[0].astype(F32) + l0_ref[0].astype(F32)) + l1_ref[0].astype(F32)) + l2_ref[0].astype(F32)

    specs = [pl.BlockSpec((1, tr, tc), lambda i, j, c: (c[0], i, j))]
    specs += [pl.BlockSpec((1, tr, tc), functools.partial(lambda i, j, c, k: (k, i, j), k=k)) for k in range(N_CHIPS - 1)]
    spec = pltpu.PrefetchScalarGridSpec(num_scalar_prefetch=1, grid=(r // tr, w // tc), in_specs=specs,
                                        out_specs=pl.BlockSpec((tr, tc), lambda i, j, c: (i, j)))
    return pl.pallas_call(body, name="add_chips", grid_spec=spec, out_shape=_sds((r, w)),
                          compiler_params=_params(("parallel", "parallel")))(chip_idx, pair, landed, landed, landed)


def _sum_devices(g):
    n = g.shape[2]

    def body(g_ref, o_ref):
        t = g_ref[0]
        for j in range(1, N_DEV):
            t = t + g_ref[j]
        o_ref[...] = t

    return pl.pallas_call(body, name="sum_devices", out_shape=_sds((1, n)), compiler_params=_params())(g)


def _adamw(w, g, m, v):
    shp = w.shape
    shp3 = (1, 1, shp[0]) if len(shp) == 1 else (-1,) + tuple(shp[-2:])
    w3, g3, m3, v3 = (t.reshape(shp3) for t in (w, g, m, v))
    c1 = 1.0 - ADAM_B1 ** ADAM_STEP
    c2 = 1.0 - ADAM_B2 ** ADAM_STEP

    def body(w_ref, g_ref, m_ref, v_ref, d_ref, mo_ref, vo_ref):
        gv = g_ref[...]
        mn = ADAM_B1 * m_ref[...] + (1.0 - ADAM_B1) * gv
        vn = ADAM_B2 * v_ref[...] + (1.0 - ADAM_B2) * (gv * gv)
        d_ref[...] = -ADAM_LR * ((mn / c1) / (jnp.sqrt(vn / c2) + ADAM_EPS) + ADAM_WD * w_ref[...])
        mo_ref[...] = mn
        vo_ref[...] = vn

    nl, r, wd = w3.shape
    tr, tc = _tile2d(r, wd, BLOCK_BYTES // 2)
    blk = pl.BlockSpec((1, tr, tc), lambda l, i, j: (l, i, j))
    s3 = _sds(w3.shape)
    d, mn, vn = pl.pallas_call(
        body, name="adamw", grid=(nl, r // tr, wd // tc), in_specs=[blk] * 4, out_specs=[blk] * 3,
        out_shape=[s3, s3, s3], compiler_params=_params(("parallel", "parallel", "parallel")),
    )(w3, g3, m3, v3)
    return d.reshape(shp), mn.reshape(shp), vn.reshape(shp)


ANY = pl.BlockSpec(memory_space=pl.ANY)
PIECE_BYTES = 1 << 20


def _place():
    return lax.axis_index("x"), lax.axis_index("y"), lax.axis_index("c")


def _pieces(shape, itemsize):
    if len(shape) >= 3:
        return [(i,) + p for i in range(shape[0]) for p in _pieces(shape[1:], itemsize)]
    rows = shape[0]
    row_bytes = itemsize
    for dsz in shape[1:]:
        row_bytes *= dsz
    k = 1
    while rows % (2 * k) == 0 and (rows // (2 * k)) % 16 == 0 and (rows // k) * row_bytes > PIECE_BYTES:
        k *= 2
    step = rows // k
    return [(pl.ds(j * step, step),) for j in range(k)]


def _split_start(make, src, dst, pieces):
    for p in pieces:
        make(src.at[p], dst.at[p]).start()
    return make(src, dst)


def _comm_call(body, name, arrs, out_shapes, n_remote, n_local):
    return pl.pallas_call(
        body, name=name, in_specs=[ANY] * len(arrs), out_specs=[ANY] * len(out_shapes), out_shape=out_shapes,
        scratch_shapes=[pltpu.SemaphoreType.DMA((n_remote,)), pltpu.SemaphoreType.DMA((n_remote,)),
                        pltpu.SemaphoreType.DMA((n_local,))],
    )(*arrs)


def all_gather8(arrs, name):
    n = len(arrs)
    pieces = [_pieces(a.shape, a.dtype.itemsize) for a in arrs]

    def body(*refs):
        ins, outs = refs[:n], refs[n:2 * n]
        send, recv, _ = refs[2 * n:]
        x, y, c = _place()
        me, sib = (x, y, c), (x, y, 1 - c)
        chips = [(1 - x, y), (x, 1 - y), (1 - x, 1 - y)]

        def slot(p):
            return 4 * p[0] + 2 * p[1] + p[2]

        def maker(t, k, to):
            def make(s, d):
                return pltpu.make_async_remote_copy(src_ref=s, dst_ref=d, send_sem=send.at[7 * t + k],
                                                    recv_sem=recv.at[7 * t + k], device_id=to, device_id_type=MESH)
            return make

        def landing(t, k, block):
            dst = outs[t].at[slot(block)]
            return maker(t, k, me)(dst, dst)

        sent = []
        for t in range(n):
            dst = outs[t].at[slot(me)]
            sent.append(_split_start(maker(t, 0, sib), ins[t], dst, pieces[t]))
            for j, chip in enumerate(chips):
                sent.append(_split_start(maker(t, 1 + j, (*chip, c)), ins[t], dst, pieces[t]))
        for j, chip in enumerate(chips):
            for t in range(n):
                landing(t, 1 + j, (*chip, c)).wait_recv()
                blk = outs[t].at[slot((*chip, c))]
                sent.append(_split_start(maker(t, 4 + j, sib), blk, blk, pieces[t]))
        for t in range(n):
            landing(t, 0, sib).wait_recv()
            for j, chip in enumerate(chips):
                landing(t, 4 + j, (*chip, 1 - c)).wait_recv()
        for cp in sent:
            cp.wait_send()

    outs = [_sds((N_DEV,) + a.shape, a.dtype) for a in arrs]
    got = _comm_call(body, name, arrs, outs, 7 * n, 1)
    x, y, c = _place()
    return [lax.dynamic_update_index_in_dim(g, a, 4 * x + 2 * y + c, 0) for g, a in zip(got, arrs)]


def sibling_send(arrs, name):
    n = len(arrs)
    pieces = [_pieces(a.shape[1:], a.dtype.itemsize) for a in arrs]

    def body(*refs):
        ins, theirs = refs[:n], refs[n:2 * n]
        send, recv, _ = refs[2 * n:]
        x, y, c = _place()
        rem = []
        for t in range(n):
            def make(s, d, t=t):
                return pltpu.make_async_remote_copy(src_ref=s, dst_ref=d, send_sem=send.at[t], recv_sem=recv.at[t],
                                                    device_id=(x, y, 1 - c), device_id_type=MESH)
            rem.append(_split_start(make, ins[t].at[1 - c], theirs[t], pieces[t]))
        for cp in rem:
            cp.wait_recv()
        for cp in rem:
            cp.wait_send()

    outs = [_sds(a.shape[1:], a.dtype) for a in arrs]
    return _comm_call(body, name, arrs, outs, n, 1)


def exchange_chips(arrs, name):
    n = len(arrs)
    pieces = [_pieces(a.shape[1:], a.dtype.itemsize) for a in arrs]

    def body(*refs):
        ins, outs = refs[:n], refs[n:2 * n]
        send, recv, _ = refs[2 * n:]
        x, y, c = _place()
        peers = [(1 - x, y), (x, 1 - y), (1 - x, 1 - y)]
        rem = []
        for t in range(n):
            for j, (px, py) in enumerate(peers):
                def make(s, d, t=t, j=j, px=px, py=py):
                    return pltpu.make_async_remote_copy(
                        src_ref=s, dst_ref=d, send_sem=send.at[3 * t + j], recv_sem=recv.at[3 * t + j],
                        device_id=(px, py, c), device_id_type=MESH)
                rem.append(_split_start(make, ins[t].at[2 * px + py], outs[t].at[j], pieces[t]))
        for cp in rem:
            cp.wait_recv()
        for cp in rem:
            cp.wait_send()

    outs = [_sds((N_CHIPS - 1,) + a.shape[1:], a.dtype) for a in arrs]
    return _comm_call(body, name, arrs, outs, 3 * n, 1)


def sibling_swap(arrs, name):
    n = len(arrs)
    pieces = [_pieces(a.shape, a.dtype.itemsize) for a in arrs]

    def body(*refs):
        ins, outs = refs[:n], refs[n:2 * n]
        send, recv, _ = refs[2 * n:]
        x, y, c = _place()
        rem = []
        for t in range(n):
            def make(s, d, t=t):
                return pltpu.make_async_remote_copy(src_ref=s, dst_ref=d, send_sem=send.at[t], recv_sem=recv.at[t],
                                                    device_id=(x, y, 1 - c), device_id_type=MESH)
            rem.append(_split_start(make, ins[t], outs[t], pieces[t]))
        for cp in rem:
            cp.wait_recv()
        for cp in rem:
            cp.wait_send()

    outs = [_sds(a.shape, a.dtype) for a in arrs]
    return _comm_call(body, name, arrs, outs, n, 1)


def _peer_copies(srcs, lands, send, recv, mode):
    x, y, c = _place()
    my_chip = 2 * x + y
    out = []
    for t in range(len(srcs)):
        for j, (px, py) in enumerate([(1 - x, y), (x, 1 - y), (1 - x, 1 - y)]):
            if mode == "gather":
                s, dst = srcs[t], lands[t].at[c, my_chip]
            else:
                s, dst = srcs[t].at[2 * px + py], lands[t].at[j]
            out.append(pltpu.make_async_remote_copy(
                src_ref=s, dst_ref=dst, send_sem=send.at[3 * t + j], recv_sem=recv.at[3 * t + j],
                device_id=(px, py, c), device_id_type=MESH))
    return out


HBM = pl.BlockSpec(memory_space=pltpu.HBM)
SEM = pl.BlockSpec(memory_space=pltpu.SEMAPHORE)
EFFECT = pltpu.SideEffectType.DATAFLOW_SIDE_EFFECTING


def ici_start(srcs, lands, mode, name):
    n = len(srcs)

    def body(*refs):
        send, recv = refs[2 * n], refs[2 * n + 1]
        for cp in _peer_copies(refs[:n], refs[n:2 * n], send, recv, mode):
            cp.start()
        refs[-1][...] = jnp.zeros_like(refs[-1])

    thru = [pltpu.HBM(a.shape, a.dtype) for a in list(srcs) + list(lands)]
    outs = pl.pallas_call(
        body, name=name, in_specs=[HBM] * (2 * n), out_specs=[SEM, SEM] + [HBM] * (2 * n) + [pl.BlockSpec(memory_space=pltpu.VMEM)],
        out_shape=[pltpu.SemaphoreType.DMA((3 * n,)), pltpu.SemaphoreType.DMA((3 * n,))] + thru + [_sds((8, 128))],
        input_output_aliases={i: 2 + i for i in range(2 * n)},
        compiler_params=pltpu.CompilerParams(has_side_effects=EFFECT),
    )(*[pltpu.with_memory_space_constraint(a, pltpu.HBM) for a in list(srcs) + list(lands)])
    return dict(send=outs[0], recv=outs[1], srcs=outs[2:2 + n], lands=outs[2 + n:2 + 2 * n], token=outs[-1])


def ici_wait(handle, after, mode, name):
    n = len(handle["srcs"])

    def body(*refs):
        send, recv = refs[2 * n], refs[2 * n + 1]
        for cp in _peer_copies(refs[:n], refs[n:2 * n], send, recv, mode):
            cp.wait_send()
            cp.wait_recv()

    arrs = list(handle["srcs"]) + list(handle["lands"])
    outs = pl.pallas_call(
        body, name=name, in_specs=[HBM] * (2 * n) + [SEM, SEM, ANY], out_specs=[HBM] * (2 * n),
        out_shape=[pltpu.HBM(a.shape, a.dtype) for a in arrs], input_output_aliases={i: i for i in range(2 * n)},
        compiler_params=pltpu.CompilerParams(has_side_effects=EFFECT),
    )(*arrs, handle["send"], handle["recv"], after)
    return outs[:n], outs[n:]


def gather_share(blocks, lands, name):
    n = len(blocks)

    def body(*refs):
        own, buf = refs[:n], refs[2 * n:3 * n]
        done, send, recv = refs[3 * n:]
        x, y, c = _place()
        my_chip = 2 * x + y
        chips = [2 * (1 - x) + y, 2 * x + (1 - y), 2 * (1 - x) + (1 - y)]
        sent = []
        for t in range(n):
            def make(s, d, k, t=t):
                return pltpu.make_async_remote_copy(src_ref=s, dst_ref=d, send_sem=send.at[4 * t + k],
                                                    recv_sem=recv.at[4 * t + k], device_id=(x, y, 1 - c),
                                                    device_id_type=MESH)
            cp = make(own[t], buf[t].at[c, my_chip], 0)
            cp.start()
            sent.append(cp)
            for k, pc in enumerate(chips):
                cp = make(buf[t].at[c, pc], buf[t].at[c, pc], 1 + k)
                cp.start()
                sent.append(cp)
        for t in range(n):
            for k in range(4):
                got = buf[t].at[1 - c, k]
                pltpu.make_async_remote_copy(src_ref=got, dst_ref=got, send_sem=send.at[4 * t + k],
                                             recv_sem=recv.at[4 * t + k], device_id=(x, y, 1 - c),
                                             device_id_type=MESH).wait_recv()
        for cp in sent:
            cp.wait_send()
        done[...] = jnp.zeros_like(done)

    outs = pl.pallas_call(
        body, name=name, in_specs=[ANY] * (2 * n), out_specs=[ANY] * n + [pl.BlockSpec(memory_space=pltpu.VMEM)],
        out_shape=[_sds(a.shape, a.dtype) for a in lands] + [_sds((8, 128))],
        input_output_aliases={n + t: t for t in range(n)},
        scratch_shapes=[pltpu.SemaphoreType.DMA((4 * n,)), pltpu.SemaphoreType.DMA((4 * n,))],
    )(*blocks, *lands)
    return outs[:n], outs[n]


@jax.custom_vjp
def _build_w_in(w4):
    full = w4.reshape(-1, w4.shape[-1])
    parts = []
    for _, start, width, wp in PROJ_SEGS:
        if width:
            parts.append(full[start:start + width])
        if wp > width:
            parts.append(jnp.zeros((wp - width, full.shape[1]), full.dtype))
    return jnp.concatenate(parts, axis=0)


def _build_w_in_f(w4):
    return _build_w_in(w4), None


def _build_w_in_b(_, g):
    parts, at = [], 0
    for _, _, width, wp in PROJ_SEGS:
        if width:
            parts.append(g[at:at + width])
        at += wp
    return (jnp.concatenate(parts, axis=0).reshape(N_CHIPS, -1, g.shape[1]),)


_build_w_in.defvjp(_build_w_in_f, _build_w_in_b)


def _split_w_uq(w):
    w3 = w.reshape(w.shape[0], MLA_HEADS, MLA_NOPE + MLA_ROPE)
    return w3[:, :, :MLA_NOPE].reshape(w.shape[0], -1), w3[:, :, MLA_NOPE:].reshape(w.shape[0], -1)


def _swap_halves(t, width):
    t3 = t.reshape(t.shape[0], -1, 2, width // 2)
    return jnp.concatenate([t3[:, :, 1:], t3[:, :, :1]], axis=2).reshape(t.shape)


def _pad_heads(t, width):
    t3 = t.reshape(t.shape[0], -1, width)
    t3 = jnp.pad(t3, ((0, 0), (0, 0), (0, HEAD_LANES - width)))
    return t3.reshape(t.shape[0], -1).astype(BF16)


def _layer(xh, mod, big, small, rope_q, rope_k):
    d = D_MODEL
    shift, scale, gate = mod[None, 0:d], mod[None, d:2 * d], mod[None, 2 * d:3 * d]
    w_al = _build_w_in(big["w_in"])
    proj = mod_mm(xh, small["norm_g"][None], scale, shift, w_al)
    gq, gk, gv, glr, mq, mkv, mkr, cb, cc, cx, _, z = split_proj(proj)

    rk = GLA_RANK
    hk = GLA_HEADS * GLA_DK
    wg = jnp.zeros((128, 2 * hk), F32)
    wg = wg.at[0:rk, 0:hk].set(small["gla_wg_f"]).at[rk:2 * rk, hk:].set(small["gla_wg_b"])
    bg = jnp.concatenate([small["gla_bg_f"], small["gla_bg_b"]])[None]
    la = gate_act(mm(glr, wg), bg)
    o_gla = rmsnorm(gla(gq, gk, gv, la), small["gla_norm_g"][None])

    cq = rmsnorm(mq, small["mla_q_norm_g"][None])
    w_nope, w_rope = _split_w_uq(jnp.concatenate([big["w_uq"][j] for j in range(N_CHIPS)], axis=1))
    qn = mm16(cq, w_nope)
    qr = mm(cq, w_rope)
    qr = fma(qr, rope_q[0], _swap_halves(qr, MLA_ROPE), rope_q[1])
    ckv = rmsnorm(mkv, small["mla_kv_norm_g"][None])
    kv = mm16(ckv, jnp.concatenate([big["w_ukv"][j] for j in range(N_CHIPS)], axis=1))
    kr = mkr[:, :MLA_ROPE]
    kr = fma(kr, rope_k[0], _swap_halves(kr, MLA_ROPE), rope_k[1])
    o_mla = rmsnorm(attn(qn, _pad_heads(qr, MLA_ROPE), kv, _pad_heads(kr, MLA_ROPE)), small["mla_out_g"][None])

    cw = jnp.concatenate([small["conv_w"], jnp.zeros((5, CONV_CH), F32)], axis=0)
    o_conv = rmsnorm(conv_op(cb, cc, cx, cw), small["conv_out_g"][None])

    o = jnp.concatenate([o_gla, o_mla, o_conv], axis=1)
    w_out = big["w_out"].reshape(d, d)
    return out_block(o, z, w_out, xh, gate)


SMALL_REPL = ("norm_g", "gla_bg_f", "gla_bg_b", "gla_norm_g", "mla_q_norm_g", "mla_kv_norm_g", "mla_out_g",
              "conv_out_g")
SMALL_SHARDED = ("gla_wg_f", "gla_wg_b", "conv_w")
BIG = ("w_in", "w_out", "w_uq", "w_ukv")
HALF_AXIS = (1, 0, 0, 0)


def kernel(x, c, positions, ada_w, ada_b, norm_g, w_in, gla_wg_f, gla_bg_f, gla_wg_b, gla_bg_b, gla_norm_g, mla_q_norm_g, mla_kv_norm_g, mla_w_uq, mla_w_ukv, mla_out_g, conv_w, conv_out_g, w_out, final_g, loss_target, m_ada_w, m_ada_b, m_norm_g, m_w_in, m_gla_wg_f, m_gla_bg_f, m_gla_wg_b, m_gla_bg_b, m_gla_norm_g, m_mla_q_norm_g, m_mla_kv_norm_g, m_mla_w_uq, m_mla_w_ukv, m_mla_out_g, m_conv_w, m_conv_out_g, m_w_out, m_final_g, v_ada_w, v_ada_b, v_norm_g, v_w_in, v_gla_wg_f, v_gla_bg_f, v_gla_wg_b, v_gla_bg_b, v_gla_norm_g, v_mla_q_norm_g, v_mla_kv_norm_g, v_mla_w_uq, v_mla_w_ukv, v_mla_out_g, v_conv_w, v_conv_out_g, v_w_out, v_final_g):
    xi, yi, ci = _place()
    chip = 2 * xi + yi
    dev = 2 * chip + ci
    s = x.shape[1]
    d = D_MODEL
    weights = dict(ada_w=ada_w, ada_b=ada_b, norm_g=norm_g, w_in=w_in, gla_wg_f=gla_wg_f, gla_bg_f=gla_bg_f,
                   gla_wg_b=gla_wg_b, gla_bg_b=gla_bg_b, gla_norm_g=gla_norm_g, mla_q_norm_g=mla_q_norm_g,
                   mla_kv_norm_g=mla_kv_norm_g, mla_w_uq=mla_w_uq, mla_w_ukv=mla_w_ukv, mla_out_g=mla_out_g,
                   conv_w=conv_w, conv_out_g=conv_out_g, w_out=w_out, final_g=final_g)
    m_in = dict(ada_w=m_ada_w, ada_b=m_ada_b, norm_g=m_norm_g, w_in=m_w_in, gla_wg_f=m_gla_wg_f, gla_bg_f=m_gla_bg_f,
                gla_wg_b=m_gla_wg_b, gla_bg_b=m_gla_bg_b, gla_norm_g=m_gla_norm_g, mla_q_norm_g=m_mla_q_norm_g,
                mla_kv_norm_g=m_mla_kv_norm_g, mla_w_uq=m_mla_w_uq, mla_w_ukv=m_mla_w_ukv, mla_out_g=m_mla_out_g,
                conv_w=m_conv_w, conv_out_g=m_conv_out_g, w_out=m_w_out, final_g=m_final_g)
    v_in = dict(ada_w=v_ada_w, ada_b=v_ada_b, norm_g=v_norm_g, w_in=v_w_in, gla_wg_f=v_gla_wg_f, gla_bg_f=v_gla_bg_f,
                gla_wg_b=v_gla_wg_b, gla_bg_b=v_gla_bg_b, gla_norm_g=v_gla_norm_g, mla_q_norm_g=v_mla_q_norm_g,
                mla_kv_norm_g=v_mla_kv_norm_g, mla_w_uq=v_mla_w_uq, mla_w_ukv=v_mla_w_ukv, mla_out_g=v_mla_out_g,
                conv_w=v_conv_w, conv_out_g=v_conv_out_g, w_out=v_w_out, final_g=v_final_g)

    g_c, g_wgf, g_wgb, g_cw = all_gather8([c, gla_wg_f, gla_wg_b, conv_w], "gather_small")

    def unshard_cols(g):
        g4 = g[0::2]
        return g4.transpose(1, 2, 0, 3).reshape(g4.shape[1], g4.shape[2], -1)

    small_full = dict(gla_wg_f=unshard_cols(g_wgf), gla_wg_b=unshard_cols(g_wgb), conv_w=unshard_cols(g_cw))
    for nme in SMALL_REPL:
        small_full[nme] = weights[nme]
    smalls = [{nme: small_full[nme][l] for nme in SMALL_REPL + SMALL_SHARDED} for l in range(DEPTH)]

    big_src = (jnp.swapaxes(w_in, 1, 2), w_out, mla_w_uq, mla_w_ukv)

    def my_halves(l, zero=0):
        out = []
        for t, a in enumerate(big_src):
            n_half = a.shape[1 + HALF_AXIS[t]] // 2
            out.append(lax.dynamic_slice_in_dim(a[l], ci * n_half + zero, n_half, axis=HALF_AXIS[t]).astype(BF16))
        return out

    def landing(blocks):
        return [lax.empty((2, N_CHIPS) + b.shape, b.dtype) for b in blocks]

    def finish_gather(handle, after, tag):
        blocks, lands = ici_wait(handle, after, "gather", "gather_wait" + tag)
        lands, done = gather_share(blocks, lands, "gather_share" + tag)
        full = [lax.dynamic_update_slice(g, b[None, None], (ci, chip) + (0,) * b.ndim) for g, b in zip(lands, blocks)]
        return full, done

    halves0 = my_halves(0)
    started0 = ici_start(halves0, landing(halves0), "gather", "gather_start0")

    c_act = _silu_rows(g_c[:, 0, :])
    c_act16 = jnp.concatenate([c_act, jnp.zeros_like(c_act)], axis=0)
    n_ada = ada_w.shape[2]
    parts = []
    for l in range(DEPTH):
        bias = lax.dynamic_slice_in_dim(ada_b[l], chip * n_ada, n_ada)[None]
        parts.append(_mm(c_act16, ada_w[l], bias=bias, name="ada_fwd"))
    g_mod, = all_gather8([jnp.stack(parts)], "gather_mod")
    mod_mine = lax.dynamic_index_in_dim(g_mod[0::2], dev, 2, keepdims=False)
    mods = mod_mine.transpose(1, 0, 2).reshape(DEPTH, 3 * d)

    inv_freq = ROPE_THETA ** (-jnp.arange(0, MLA_ROPE, 2, dtype=F32) / MLA_ROPE)
    ang = positions[0].astype(F32)[:, None] * inv_freq
    cos, sin = jnp.cos(ang), jnp.sin(ang)
    rope_k = (jnp.concatenate([cos, cos], axis=1), jnp.concatenate([-sin, sin], axis=1))
    rope_q = (jnp.tile(rope_k[0], (1, MLA_HEADS)), jnp.tile(rope_k[1], (1, MLA_HEADS)))

    def run_layer(xh, mod, gathered, small):
        big = {nme: jnp.concatenate([g[0], g[1]], axis=HALF_AXIS[t] + 1) for t, (nme, g) in enumerate(zip(BIG, gathered))}
        return _layer(xh, mod, big, small, rope_q, rope_k)

    def head(hh, fg):
        return loss_op(rmsnorm(hh, fg[None]), loss_target[0])[0, 0]

    gathered0, done0 = finish_gather(started0, mods, "0")
    halves1 = my_halves(1, done0[0, 0].astype(jnp.int32))
    started1 = ici_start(halves1, landing(halves1), "gather", "gather_start1")
    h1, vjp0 = jax.vjp(run_layer, x[0], mods[0] + started1["token"][0, 0], gathered0, smalls[0])
    gathered1, _ = finish_gather(started1, h1, "1")
    h2, vjp1 = jax.vjp(run_layer, h1, mods[1], gathered1, smalls[1])
    loss_dev, vjp_head = jax.vjp(head, h2, final_g)
    loss = lax.psum(loss_dev, ("x", "y", "c"))
    dh2, dfinal = vjp_head(jnp.ones((), F32))

    c_idx = jnp.reshape(ci, (1,)).astype(jnp.int32)
    chip_idx = jnp.reshape(chip, (1,)).astype(jnp.int32)

    def reduce_begin(dgath, tag):
        theirs = sibling_send(dgath, "reduce_sibling" + tag)
        pair = [_add_pair(a, b, c_idx) for a, b in zip(dgath, theirs)]
        lands = [lax.empty((N_CHIPS - 1,) + p.shape[1:], p.dtype) for p in pair]
        return ici_start(pair, lands, "reduce", "reduce_start" + tag)

    def reduce_end(handle, after, tag):
        pair, landed = ici_wait(handle, after, "reduce", "reduce_wait" + tag)
        reduced = [_add_chips(p, q, chip_idx) for p, q in zip(pair, landed)]
        others = sibling_swap(reduced, "share_sibling" + tag)
        return [jnp.where(ci == 0, jnp.concatenate([own, other], axis=HALF_AXIS[t]),
                          jnp.concatenate([other, own], axis=HALF_AXIS[t]))
                for t, (own, other) in enumerate(zip(reduced, others))]

    dh1, dmod1, dgath1, dsmall1 = vjp1(dh2)
    reducing1 = reduce_begin(dgath1, "1")
    dx, dmod0, dgath0, dsmall0 = vjp0(dh1 + reducing1["token"][0, 0])
    big_grads1 = reduce_end(reducing1, dx, "1")
    reducing0 = reduce_begin(dgath0, "0")
    dmods = jnp.stack([dmod0, dmod1])
    dsmalls = [dsmall0, dsmall1]

    pieces = [dmods.reshape(-1), dfinal]
    for nme in SMALL_REPL + SMALL_SHARDED:
        pieces.append(jnp.stack([dsmalls[l][nme] for l in range(DEPTH)]).reshape(-1))
    sizes = [p.shape[0] for p in pieces]
    flat = jnp.concatenate(pieces)
    padn = (-flat.shape[0]) % 128
    flat = jnp.concatenate([flat, jnp.zeros((padn,), F32)])[None]
    flat = flat + reducing0["token"][0, 0]
    g_small, = all_gather8([flat], "gather_small_grads")
    total = _sum_devices(g_small)[0]
    offs, at = [], 0
    for n_el in sizes:
        offs.append(at)
        at += n_el

    def piece(i, shape):
        return total[offs[i]:offs[i] + sizes[i]].reshape(shape)

    grads = {"ada_b": piece(0, (DEPTH, 3 * d)), "final_g": piece(1, (d,))}
    for i, nme in enumerate(SMALL_REPL + SMALL_SHARDED):
        full = piece(2 + i, small_full[nme].shape)
        if nme in SMALL_SHARDED:
            ncol = weights[nme].shape[2]
            full = lax.dynamic_slice_in_dim(full, chip * ncol, ncol, axis=2)
        grads[nme] = full

    dmod_all = g_small[:, 0, :DEPTH * 3 * d].reshape(N_DEV, DEPTH, 3 * d)
    dmod_cols = lax.dynamic_slice_in_dim(dmod_all, chip * n_ada, n_ada, axis=2)
    g_ada = []
    for l in range(DEPTH):
        dm16 = jnp.concatenate([dmod_cols[:, l], jnp.zeros((N_DEV, n_ada), F32)], axis=0)
        g_ada.append(_mm(c_act16, dm16, ta=True, name="ada_bwd"))
    grads["ada_w"] = jnp.stack(g_ada)

    order = list(weights)
    big_names = ("w_in", "w_out", "mla_w_uq", "mla_w_ukv")
    delta, new_m, new_v = {}, {}, {}
    for nme in order:
        if nme not in big_names:
            delta[nme], new_m[nme], new_v[nme] = _adamw(weights[nme], grads[nme], m_in[nme], v_in[nme])
    big_grads0 = reduce_end(reducing0, delta["ada_w"], "0")
    for nme, g0, g1 in zip(big_names, big_grads0, big_grads1):
        grads[nme] = jnp.stack([g0, g1])
    for nme in big_names:
        if nme == "w_in":
            w_t, m_t, v_t = (jnp.swapaxes(t, 1, 2) for t in (w_in, m_w_in, v_w_in))
            res = _adamw(w_t, grads[nme], m_t, v_t)
            delta[nme], new_m[nme], new_v[nme] = (jnp.swapaxes(t, 1, 2) for t in res)
            grads[nme] = jnp.swapaxes(grads[nme], 1, 2)
            continue
        delta[nme], new_m[nme], new_v[nme] = _adamw(weights[nme], grads[nme], m_in[nme], v_in[nme])
    return (loss, dx[None], *[grads[n_] for n_ in order], *[delta[n_] for n_ in order],
            *[new_m[n_] for n_ in order], *[new_v[n_] for n_ in order])
```

```python
import functools

import jax
import jax.numpy as jnp
from jax import lax
from jax.experimental import pallas as pl
from jax.experimental.pallas import tpu as pltpu

F32 = jnp.float32
BF16 = jnp.bfloat16
MESH = pl.DeviceIdType.MESH
HIGHEST = lax.Precision.HIGHEST

DEPTH = 2
D_MODEL = 2048
GLA_HEADS = 6
GLA_DK = 64
GLA_DV = 128
GLA_RANK = 16
GLA_TEMP = 16.0
GLA_CHUNK = 64
GLA_W = GLA_HEADS * GLA_DV
MLA_HEADS = 6
MLA_QL = 384
MLA_KVL = 256
MLA_NOPE = 128
MLA_ROPE = 64
MLA_DV = 128
MLA_W = MLA_HEADS * MLA_DV
CONV_CH = D_MODEL - GLA_W - MLA_W
ROPE_THETA = 10000.0
EPS = 1e-6
IN_DIM = 5856
N_CHIPS = 4
N_DEV = 8

ADAM_LR = 0.001
ADAM_B1 = 0.9
ADAM_B2 = 0.999
ADAM_EPS = 1e-08
ADAM_WD = 0.01
ADAM_STEP = 10

PROJ_SEGS = (
    ("gq", 0, 384, 384), ("gk", 384, 384, 384), ("gv", 768, 768, 768), ("glr", 1536, 32, 128),
    ("mq", 1568, 384, 384), ("mkv", 1952, 256, 256), ("mkr", 2208, 64, 128),
    ("cb", 2272, 512, 512), ("cc", 2784, 512, 512), ("cx", 3296, 512, 512),
    ("pad", 3808, 0, 128), ("z", 3808, 2048, 2048),
)
PROJ_AL = sum(s[3] for s in PROJ_SEGS)

VMEM_LIMIT = 48 * 1024 * 1024
BLOCK_BYTES = 2 * 1024 * 1024


def _params(sem=None):
    return pltpu.CompilerParams(dimension_semantics=sem, vmem_limit_bytes=VMEM_LIMIT)


def _dot(a, b, ca, cb, precision=None):
    return lax.dot_general(a, b, (((ca,), (cb,)), ((), ())), preferred_element_type=F32, precision=precision)


def _tile(dim, prefs):
    for t in prefs:
        if dim % t == 0:
            return t
    return dim


def _pick_rows(rows, width, itemsize=4):
    for t in (2048, 1024, 512, 256, 128, 64, 32, 16, 8):
        if rows % t == 0 and t * width * itemsize <= BLOCK_BYTES:
            return t
    return rows


def _mm(a, b, *, ta=False, tb=False, bias=None, out_dtype=F32, name="mm"):
    if ta:
        K, M = a.shape
    else:
        M, K = a.shape
    if tb:
        N, Kb = b.shape
    else:
        Kb, N = b.shape
    assert K == Kb, (a.shape, b.shape, ta, tb)
    tm = _tile(M, (512, 256, 128))
    tn = _tile(N, (1024, 512, 384, 256, 128))
    tk = _tile(K, (2048, 1024, 512, 256, 128))
    nk = K // tk
    has_bias = bias is not None

    def body(*refs):
        a_ref, b_ref = refs[0], refs[1]
        bias_ref = refs[2] if has_bias else None
        o_ref = refs[3 if has_bias else 2]
        part = _dot(a_ref[...].astype(BF16), b_ref[...].astype(BF16), 0 if ta else 1, 1 if tb else 0)

        def finish(r):
            if has_bias:
                r = r + bias_ref[...]
            o_ref[...] = r.astype(out_dtype)

        if nk == 1:
            finish(part)
            return
        acc_ref = refs[-1]
        k = pl.program_id(2)

        @pl.when(k == 0)
        def _():
            acc_ref[...] = part

        @pl.when(k != 0)
        def _():
            acc_ref[...] += part

        @pl.when(k == nk - 1)
        def _():
            finish(acc_ref[...])

    a_spec = pl.BlockSpec((tk, tm), lambda i, j, k: (k, i)) if ta else pl.BlockSpec((tm, tk), lambda i, j, k: (i, k))
    b_spec = pl.BlockSpec((tn, tk), lambda i, j, k: (j, k)) if tb else pl.BlockSpec((tk, tn), lambda i, j, k: (k, j))
    in_specs = [a_spec, b_spec]
    args = [a, b]
    if has_bias:
        in_specs.append(pl.BlockSpec((1, tn), lambda i, j, k: (0, j)))
        args.append(bias)
    return pl.pallas_call(
        body, name=name, grid=(M // tm, N // tn, nk),
        in_specs=in_specs, out_specs=pl.BlockSpec((tm, tn), lambda i, j, k: (i, j)),
        out_shape=jax.ShapeDtypeStruct((M, N), out_dtype),
        scratch_shapes=[pltpu.VMEM((tm, tn), F32)] if nk > 1 else [],
        compiler_params=_params(("parallel", "parallel", "arbitrary")),
    )(*args)


@jax.custom_vjp
def mm(a, b):
    return _mm(a, b, name="mm_fwd")


def _mm_f(a, b):
    return _mm(a, b, name="mm_fwd"), (a, b)


def _mm_b(res, g):
    a, b = res
    return _mm(g, b, tb=True, out_dtype=a.dtype, name="mm_da"), _mm(a, g, ta=True, out_dtype=b.dtype, name="mm_db")


mm.defvjp(_mm_f, _mm_b)


@jax.custom_vjp
def mm16(a, b):
    return _mm(a, b, out_dtype=BF16, name="mm16_fwd")


def _mm16_f(a, b):
    return mm16(a, b), (a, b)


mm16.defvjp(_mm16_f, _mm_b)


def _rows(body, name, tiled, full, tiled_out, acc_out, tr=None):
    rows = tiled[0].shape[0]
    if tr is None:
        width = max([a.shape[1] for a in tiled] + [s.shape[1] for s in tiled_out])
        tr = _pick_rows(rows, width)
    in_specs = [pl.BlockSpec((tr, a.shape[1]), lambda i: (i, 0)) for a in tiled]
    in_specs += [pl.BlockSpec(a.shape, lambda i: (0, 0)) for a in full]
    out_specs = [pl.BlockSpec((tr, s.shape[1]), lambda i: (i, 0)) for s in tiled_out]
    out_specs += [pl.BlockSpec(s.shape, lambda i: (0, 0)) for s in acc_out]

    def wrapped(*refs):
        body(pl.program_id(0), *refs)

    outs = pl.pallas_call(
        wrapped, name=name, grid=(rows // tr,), in_specs=in_specs, out_specs=out_specs,
        out_shape=list(tiled_out) + list(acc_out),
        compiler_params=_params(("arbitrary",)),
    )(*tiled, *full)
    return outs


def _sds(shape, dtype=F32):
    return jax.ShapeDtypeStruct(tuple(shape), dtype)


def _acc(step, ref, val):
    @pl.when(step == 0)
    def _():
        ref[...] = val

    @pl.when(step != 0)
    def _():
        ref[...] += val


def _colsum(v):
    return jnp.sum(v, axis=0, keepdims=True)


def _rstd(x):
    return lax.rsqrt(jnp.mean(x * x, axis=-1, keepdims=True) + EPS)


def _norm_grid(x, g):
    rows, w = x.shape[0], g.shape[1]
    tr = _pick_rows(rows, w)
    blk = pl.BlockSpec((tr, w), lambda i, j: (i, j))
    gblk = pl.BlockSpec((1, w), lambda i, j: (0, 0))
    return (rows // tr, x.shape[1] // w), blk, gblk


@jax.custom_vjp
def rmsnorm(x, g):
    def body(x_ref, g_ref, o_ref):
        x = x_ref[...]
        o_ref[...] = x * _rstd(x) * g_ref[...]

    grid, blk, gblk = _norm_grid(x, g)
    return pl.pallas_call(body, name="rmsnorm_fwd", grid=grid, in_specs=[blk, gblk], out_specs=blk,
                          out_shape=_sds(x.shape), compiler_params=_params(("parallel", "parallel")))(x, g)


def _rmsnorm_f(x, g):
    return rmsnorm(x, g), (x, g)


def _rmsnorm_b(res, dy):
    x, g = res

    def body(x_ref, dy_ref, g_ref, dx_ref, dg_ref):
        x = x_ref[...]
        dy = dy_ref[...]
        r = _rstd(x)
        xh = x * r
        dxh = dy * g_ref[...]
        dx_ref[...] = r * (dxh - xh * jnp.mean(dxh * xh, axis=-1, keepdims=True))
        first = jnp.logical_and(pl.program_id(0) == 0, pl.program_id(1) == 0)
        _acc(jnp.where(first, 0, 1), dg_ref, _colsum(dy * xh))

    grid, blk, gblk = _norm_grid(x, g)
    dx, dg = pl.pallas_call(body, name="rmsnorm_bwd", grid=grid, in_specs=[blk, blk, gblk], out_specs=[blk, gblk],
                            out_shape=[_sds(x.shape), _sds(g.shape)],
                            compiler_params=_params(("arbitrary", "arbitrary")))(x, dy, g)
    return dx, dg


rmsnorm.defvjp(_rmsnorm_f, _rmsnorm_b)


def _modulate(x, g, scale, shift):
    def body(i, x_ref, g_ref, sc_ref, sh_ref, o_ref):
        x = x_ref[...]
        xn = x * _rstd(x) * g_ref[...]
        o_ref[...] = (xn * (1.0 + sc_ref[...]) + sh_ref[...]).astype(BF16)
    return _rows(body, "modulate_fwd", [x], [g, scale, shift], [_sds(x.shape, BF16)], [])[0]


def _modulate_bwd(x, g, scale, shift, dh):
    def body(i, x_ref, dh_ref, g_ref, sc_ref, dx_ref, dg_ref, dsc_ref, dsh_ref):
        x = x_ref[...]
        dh = dh_ref[...]
        gv = g_ref[...]
        r = _rstd(x)
        xh = x * r
        dxn = dh * (1.0 + sc_ref[...])
        dxh = dxn * gv
        dx_ref[...] = r * (dxh - xh * jnp.mean(dxh * xh, axis=-1, keepdims=True))
        _acc(i, dg_ref, _colsum(dxn * xh))
        _acc(i, dsc_ref, _colsum(dh * (xh * gv)))
        _acc(i, dsh_ref, _colsum(dh))

    v = _sds(g.shape)
    return _rows(body, "modulate_bwd", [x, dh], [g, scale], [_sds(x.shape)], [v, v, v])


@jax.custom_vjp
def mod_mm(x, g, scale, shift, wt):
    return _mm(_modulate(x, g, scale, shift), wt, tb=True, name="mm_in")


def _mod_mm_f(x, g, scale, shift, wt):
    h = _modulate(x, g, scale, shift)
    return _mm(h, wt, tb=True, name="mm_in"), (x, g, scale, shift, wt, h)


def _mod_mm_b(res, dproj):
    x, g, scale, shift, wt, h = res
    dproj = dproj.astype(BF16)
    dh = _mm(dproj, wt, name="mm_in_dh")
    dwt = _mm(dproj, h, ta=True, out_dtype=wt.dtype, name="mm_in_dw")
    dx, dg, dsc, dsh = _modulate_bwd(x, g, scale, shift, dh)
    return dx, dg, dsc, dsh, dwt


mod_mm.defvjp(_mod_mm_f, _mod_mm_b)


def _sigmoid(z):
    return 1.0 / (1.0 + jnp.exp(-z))


def _gate_mul(o, z):
    def body(i, o_ref, z_ref, y_ref):
        z = z_ref[...]
        y_ref[...] = (o_ref[...] * (z * _sigmoid(z))).astype(BF16)
    return _rows(body, "gate_mul_fwd", [o, z], [], [_sds(o.shape, BF16)], [])[0]


def _gate_mul_bwd(o, z, dy):
    def body(i, o_ref, z_ref, dy_ref, do_ref, dz_ref):
        z = z_ref[...]
        dy = dy_ref[...]
        s = _sigmoid(z)
        do_ref[...] = dy * (z * s)
        dz_ref[...] = dy * o_ref[...] * (s * (1.0 + z * (1.0 - s)))
    return _rows(body, "gate_mul_bwd", [o, z, dy], [], [_sds(o.shape), _sds(o.shape)], [])


def _residual(x, u, gate):
    def body(i, x_ref, u_ref, g_ref, o_ref):
        o_ref[...] = x_ref[...] + g_ref[...] * u_ref[...]
    return _rows(body, "residual_fwd", [x, u], [gate], [_sds(x.shape)], [])[0]


def _residual_bwd(d, u, gate):
    def body(i, d_ref, u_ref, g_ref, du_ref, dg_ref):
        d = d_ref[...]
        du_ref[...] = (g_ref[...] * d).astype(BF16)
        _acc(i, dg_ref, _colsum(d * u_ref[...]))

    return _rows(body, "residual_bwd", [d, u], [gate], [_sds(u.shape, BF16)], [_sds(gate.shape)])


@jax.custom_vjp
def out_block(o, z, w, x, gate):
    return _residual(x, _mm(_gate_mul(o, z), w, name="mm_out"), gate)


def _out_block_f(o, z, w, x, gate):
    y = _gate_mul(o, z)
    u = _mm(y, w, name="mm_out")
    return _residual(x, u, gate), (o, z, w, y, u, gate)


def _out_block_b(res, d):
    o, z, w, y, u, gate = res
    du, dgate = _residual_bwd(d, u, gate)
    dy = _mm(du, w, tb=True, name="mm_out_dy")
    dw = _mm(y, du, ta=True, out_dtype=w.dtype, name="mm_out_dw")
    do, dz = _gate_mul_bwd(o, z, dy)
    return do, dz, dw, d, dgate


out_block.defvjp(_out_block_f, _out_block_b)


@jax.custom_vjp
def gate_act(u, b):
    def body(i, u_ref, b_ref, o_ref):
        t = u_ref[...] + b_ref[...]
        o_ref[...] = (jnp.minimum(t, 0.0) - jnp.log(1.0 + jnp.exp(-jnp.abs(t)))) / GLA_TEMP
    return _rows(body, "gate_act_fwd", [u], [b], [_sds(u.shape)], [])[0]


def _gate_act_f(u, b):
    return gate_act(u, b), (u, b)


def _gate_act_b(res, d):
    u, b = res

    def body(i, u_ref, d_ref, b_ref, du_ref, db_ref):
        t = u_ref[...] + b_ref[...]
        du = d_ref[...] * _sigmoid(-t) / GLA_TEMP
        du_ref[...] = du
        _acc(i, db_ref, _colsum(du))

    du, db = _rows(body, "gate_act_bwd", [u, d], [b], [_sds(u.shape)], [_sds(b.shape)])
    return du, db


gate_act.defvjp(_gate_act_f, _gate_act_b)


@jax.custom_vjp
def fma(a, b, c, d):
    def body(i, a_ref, b_ref, c_ref, d_ref, o_ref):
        o_ref[...] = a_ref[...] * b_ref[...] + c_ref[...] * d_ref[...]
    return _rows(body, "fma_fwd", [a, b, c, d], [], [_sds(a.shape)], [])[0]


def _fma_f(a, b, c, d):
    return fma(a, b, c, d), (b, d)


def _fma_b(res, g):
    b, d = res

    def body(i, g_ref, b_ref, d_ref, da_ref, dc_ref):
        g = g_ref[...]
        da_ref[...] = g * b_ref[...]
        dc_ref[...] = g * d_ref[...]

    da, dc = _rows(body, "fma_bwd", [g, b, d], [], [_sds(g.shape), _sds(g.shape)], [])
    return da, jnp.zeros_like(b), dc, jnp.zeros_like(d)


fma.defvjp(_fma_f, _fma_b)


def _silu_rows(c):
    def body(i, c_ref, o_ref):
        v = c_ref[...]
        o_ref[...] = v * _sigmoid(v)
    return _rows(body, "silu", [c], [], [_sds(c.shape)], [])[0]


@jax.custom_vjp
def loss_op(y, t):
    return _loss_fwd(y, t)[0]


def _loss_fwd(y, t):
    inv = 1.0 / y.shape[1]

    def body(i, y_ref, t_ref, d_ref, l_ref):
        e = y_ref[...] - t_ref[...]
        d_ref[...] = e * inv
        _acc(i, l_ref, jnp.sum(_colsum(e * e), axis=1, keepdims=True) * (0.5 * inv))

    d, l = _rows(body, "loss_fwd", [y, t], [], [_sds(y.shape)], [_sds((1, 1))])
    return l, d


def _loss_f(y, t):
    l, d = _loss_fwd(y, t)
    return l, d


def _loss_b(d, g):
    return d * g, jnp.zeros_like(d)


loss_op.defvjp(_loss_f, _loss_b)


def _conv_terms(cc, cx, rows, n):
    u = cc * cx
    up = jnp.where(rows == 0, 0.0, pltpu.roll(u, 1, 0))
    un = jnp.where(rows == n - 1, 0.0, pltpu.roll(u, n - 1, 0))
    return u, up, un


CONV_COLS = 128


def _conv_specs(s, n_in):
    blk = pl.BlockSpec((s, CONV_COLS), lambda j: (0, j))
    wblk = pl.BlockSpec((8, CONV_COLS), lambda j: (0, j))
    return [blk] * n_in + [wblk], blk, wblk


@jax.custom_vjp
def conv_op(cb, cc, cx, w):
    s, ch = cb.shape

    def body(cb_ref, cc_ref, cx_ref, w_ref, o_ref):
        rows = lax.broadcasted_iota(jnp.int32, (s, CONV_COLS), 0)
        u, up, un = _conv_terms(cc_ref[...], cx_ref[...], rows, s)
        conv = up * w_ref[0:1, :] + u * w_ref[1:2, :] + un * w_ref[2:3, :]
        o_ref[...] = cb_ref[...] * conv

    in_specs, blk, _ = _conv_specs(s, 3)
    return pl.pallas_call(
        body, name="conv_fwd", grid=(ch // CONV_COLS,), in_specs=in_specs, out_specs=blk,
        out_shape=_sds(cb.shape), compiler_params=_params(("parallel",)),
    )(cb, cc, cx, w)


def _conv_f(cb, cc, cx, w):
    return conv_op(cb, cc, cx, w), (cb, cc, cx, w)


def _conv_b(res, d):
    cb, cc, cx, w = res
    s, ch = cb.shape

    def body(cb_ref, cc_ref, cx_ref, d_ref, w_ref, dcb_ref, dcc_ref, dcx_ref, dw_ref):
        rows = lax.broadcasted_iota(jnp.int32, (s, CONV_COLS), 0)
        cc_v = cc_ref[...]
        cx_v = cx_ref[...]
        u, up, un = _conv_terms(cc_v, cx_v, rows, s)
        w0, w1, w2 = w_ref[0:1, :], w_ref[1:2, :], w_ref[2:3, :]
        dv = d_ref[...]
        dcb_ref[...] = dv * (up * w0 + u * w1 + un * w2)
        dconv = dv * cb_ref[...]
        d_next = jnp.where(rows == s - 1, 0.0, pltpu.roll(dconv, s - 1, 0))
        d_prev = jnp.where(rows == 0, 0.0, pltpu.roll(dconv, 1, 0))
        du = w0 * d_next + w1 * dconv + w2 * d_prev
        dcc_ref[...] = du * cx_v
        dcx_ref[...] = du * cc_v
        dw_ref[...] = jnp.zeros_like(dw_ref)
        dw_ref[0:1, :] = _colsum(dconv * up)
        dw_ref[1:2, :] = _colsum(dconv * u)
        dw_ref[2:3, :] = _colsum(dconv * un)

    in_specs, blk, wblk = _conv_specs(s, 4)
    v = _sds(cb.shape)
    return tuple(pl.pallas_call(
        body, name="conv_bwd", grid=(ch // CONV_COLS,), in_specs=in_specs, out_specs=[blk, blk, blk, wblk],
        out_shape=[v, v, v, _sds(w.shape)], compiler_params=_params(("parallel",)),
    )(cb, cc, cx, d, w))


conv_op.defvjp(_conv_f, _conv_b)


def _gla_masks(rev):
    c = GLA_CHUNK
    row = lax.broadcasted_iota(jnp.int32, (c, c), 0)
    col = lax.broadcasted_iota(jnp.int32, (c, c), 1)
    mask = (row < col) if rev else (row >= col)
    return rev, mask


def _chunk_cumsum(g, rev):
    c = g.shape[0]
    row = lax.broadcasted_iota(jnp.int32, g.shape, 0)
    b = g
    s = 1
    while s < c:
        if rev:
            b = b + jnp.where(row < c - s, pltpu.roll(b, c - s, 0), 0.0)
        else:
            b = b + jnp.where(row >= s, pltpu.roll(b, s, 0), 0.0)
        s *= 2
    return b


GLA_UNROLL = 4


def _gla_rows(n):
    return pl.ds(pl.multiple_of(n * GLA_CHUNK, GLA_CHUNK), GLA_CHUNK)


def _gla_scan(s_ref, bt_ref, st_ref, n_chunks, descending):
    st_ref[...] = jnp.zeros_like(st_ref)

    def step(i, carry):
        n = (n_chunks - 1 - i) if descending else i
        own = s_ref[n]
        st = st_ref[...]
        s_ref[n] = st
        st_ref[...] = st * jnp.exp(bt_ref[n]) + own
        return carry

    lax.fori_loop(0, n_chunks, step, 0)


GLA_PAIR = 2


def _gla_specs(s):
    dk, dv = GLA_DK, GLA_DV
    n_pairs = GLA_HEADS // GLA_PAIR
    blk_k = pl.BlockSpec((s, GLA_PAIR * dk), lambda p: (0, p))
    blk_gb = pl.BlockSpec((s, GLA_PAIR * dk), lambda p: (0, n_pairs + p))
    blk_v = pl.BlockSpec((s, GLA_PAIR * dv), lambda p: (0, p))
    return n_pairs, blk_k, blk_gb, blk_v


def _head_lanes(hh):
    lane = lax.broadcasted_iota(jnp.int32, (1, GLA_PAIR * GLA_DK), 1)
    return jnp.logical_and(lane >= hh * GLA_DK, lane < (hh + 1) * GLA_DK)


def _gla_fwd(q, k, v, la):
    s = q.shape[0]
    dk, dv = GLA_DK, GLA_DV
    pw = GLA_PAIR * dk
    n_chunks = s // GLA_CHUNK
    scale = GLA_DK ** -0.5

    def body(q_ref, k_ref, v_ref, gf_ref, gb_ref, o_ref, sf_ref, sb_ref, bf_ref, bb_ref, btf_ref, btb_ref, st_ref):
        masks = [_gla_masks(rev) for rev in (False, True)]
        dirs = ((False, gf_ref, sf_ref, bf_ref, btf_ref), (True, gb_ref, sb_ref, bb_ref, btb_ref))

        def decays(n, carry):
            rows = _gla_rows(n)
            for rev, g_ref, _, b_ref, bt_ref in dirs:
                g = g_ref[rows, :]
                b_ref[rows, :] = _chunk_cumsum(g, rev)
                bt_ref[n] = _colsum(g)
            return carry

        lax.fori_loop(0, n_chunks, decays, 0, unroll=GLA_UNROLL)
        for hh in range(GLA_PAIR):
            m = _head_lanes(hh)
            vl = slice(hh * dv, (hh + 1) * dv)

            def prepare(n, carry, m=m, vl=vl):
                rows = _gla_rows(n)
                kk = k_ref[rows, :]
                vb = v_ref[rows, vl].astype(BF16)
                for rev, _, s_ref, b_ref, bt_ref in dirs:
                    ke = jnp.where(m, kk * jnp.exp(bt_ref[n] - b_ref[rows, :]), 0.0).astype(BF16)
                    s_ref[n] = _dot(vb, ke, 0, 0)
                return carry

            lax.fori_loop(0, n_chunks, prepare, 0, unroll=GLA_UNROLL)
            for rev, _, s_ref, _, bt_ref in dirs:
                _gla_scan(s_ref, bt_ref, st_ref, n_chunks, descending=rev)

            def emit(n, carry, m=m, vl=vl):
                rows = _gla_rows(n)
                qs = q_ref[rows, :] * scale
                kk = k_ref[rows, :]
                vb = v_ref[rows, vl].astype(BF16)
                o = None
                for (rev, _, s_ref, b_ref, _), (_, mask) in zip(dirs, masks):
                    b = b_ref[rows, :]
                    qd = jnp.where(m, qs * jnp.exp(b), 0.0).astype(BF16)
                    ki = jnp.where(m, kk * jnp.exp(-b), 0.0).astype(BF16)
                    a = jnp.where(mask, _dot(qd, ki, 1, 1), 0.0).astype(BF16)
                    od = _dot(a, vb, 1, 0) + _dot(qd, s_ref[n].astype(BF16), 1, 1)
                    o = od if o is None else o + od
                o_ref[rows, vl] = o
                return carry

            lax.fori_loop(0, n_chunks, emit, 0, unroll=GLA_UNROLL)

    n_pairs, blk_k, blk_gb, blk_v = _gla_specs(s)
    state = pltpu.VMEM((n_chunks, dv, pw), F32)
    scratch = [state, state, pltpu.VMEM((s, pw), F32), pltpu.VMEM((s, pw), F32), pltpu.VMEM((n_chunks, 1, pw), F32),
               pltpu.VMEM((n_chunks, 1, pw), F32), pltpu.VMEM((dv, pw), F32)]
    return pl.pallas_call(
        body, name="gla_fwd", grid=(n_pairs,), in_specs=[blk_k, blk_k, blk_v, blk_k, blk_gb],
        out_specs=blk_v, out_shape=_sds(v.shape), scratch_shapes=scratch,
        compiler_params=_params(("parallel",)),
    )(q, k, v, la, la)


def _gla_bwd(q, k, v, la, do):
    s = q.shape[0]
    dk, dv = GLA_DK, GLA_DV
    pw = GLA_PAIR * dk
    c = GLA_CHUNK
    n_chunks = s // c
    scale = GLA_DK ** -0.5

    def body(q_ref, k_ref, v_ref, gf_ref, gb_ref, do_ref, dq_ref, dk_ref, dv_ref, dgf_ref, dgb_ref,
             sf_ref, sb_ref, bf_ref, bb_ref, btf_ref, btb_ref, dsf_ref, dsb_ref, st_ref):
        masks = [_gla_masks(rev) for rev in (False, True)]
        rowc = lax.broadcasted_iota(jnp.int32, (c, pw), 0)
        dirs = ((False, gf_ref, sf_ref, bf_ref, btf_ref, dsf_ref, dgf_ref),
                (True, gb_ref, sb_ref, bb_ref, btb_ref, dsb_ref, dgb_ref))

        def decays(n, carry):
            rows = _gla_rows(n)
            for rev, g_ref, _, b_ref, bt_ref, _, _ in dirs:
                g = g_ref[rows, :]
                b_ref[rows, :] = _chunk_cumsum(g, rev)
                bt_ref[n] = _colsum(g)
            return carry

        lax.fori_loop(0, n_chunks, decays, 0, unroll=GLA_UNROLL)
        for hh in range(GLA_PAIR):
            m = _head_lanes(hh)
            vl = slice(hh * dv, (hh + 1) * dv)

            def prepare(n, carry, m=m, vl=vl):
                rows = _gla_rows(n)
                qs = q_ref[rows, :] * scale
                kk = k_ref[rows, :]
                vb = v_ref[rows, vl].astype(BF16)
                do_b = do_ref[rows, vl].astype(BF16)
                for rev, _, s_ref, b_ref, bt_ref, ds_ref, _ in dirs:
                    b = b_ref[rows, :]
                    ke = jnp.where(m, kk * jnp.exp(bt_ref[n] - b), 0.0).astype(BF16)
                    qd = jnp.where(m, qs * jnp.exp(b), 0.0).astype(BF16)
                    s_ref[n] = _dot(vb, ke, 0, 0)
                    ds_ref[n] = _dot(do_b, qd, 0, 0)
                return carry

            lax.fori_loop(0, n_chunks, prepare, 0, unroll=GLA_UNROLL)
            for rev, _, s_ref, _, bt_ref, ds_ref, _ in dirs:
                _gla_scan(s_ref, bt_ref, st_ref, n_chunks, descending=rev)
                _gla_scan(ds_ref, bt_ref, st_ref, n_chunks, descending=not rev)

            def emit(n, carry, m=m, vl=vl, first=(hh == 0)):
                rows = _gla_rows(n)
                qs = q_ref[rows, :] * scale
                kk = k_ref[rows, :]
                vb = v_ref[rows, vl].astype(BF16)
                do_b = do_ref[rows, vl].astype(BF16)
                dq = dkk = dvv = None
                for (rev, _, s_ref, b_ref, bt_ref, ds_ref, dg_ref), (_, mask) in zip(dirs, masks):
                    b = b_ref[rows, :]
                    bt = bt_ref[n]
                    eb = jnp.where(m, jnp.exp(b), 0.0)
                    enb = jnp.where(m, jnp.exp(-b), 0.0)
                    etb = jnp.where(m, jnp.exp(bt - b), 0.0)
                    ebt = jnp.exp(bt)
                    qd = qs * eb
                    ki = kk * enb
                    ke = kk * etb
                    qd_b, ki_b, ke_b = qd.astype(BF16), ki.astype(BF16), ke.astype(BF16)
                    st = s_ref[n]
                    dst = ds_ref[n]
                    dst_b = dst.astype(BF16)
                    a = jnp.where(mask, _dot(qd_b, ki_b, 1, 1), 0.0).astype(BF16)
                    da = jnp.where(mask, _dot(do_b, vb, 1, 1), 0.0).astype(BF16)
                    dv_d = _dot(a, do_b, 0, 0) + _dot(ke_b, dst_b, 1, 1)
                    dqd = _dot(da, ki_b, 1, 0) + _dot(do_b, st.astype(BF16), 1, 0)
                    dki = _dot(da, qd_b, 0, 0)
                    dke = _dot(vb, dst_b, 1, 0)
                    dbt = _colsum(st * dst) * ebt + _colsum(dke * ke)
                    db = dqd * qd - dki * ki - dke * ke
                    db = db + jnp.where(rowc == (0 if rev else c - 1), dbt, 0.0)
                    dg = _chunk_cumsum(db, not rev)
                    if first:
                        dg_ref[rows, :] = dg
                    else:
                        dg_ref[rows, :] += dg
                    dq_d = dqd * eb * scale
                    dk_d = dki * enb + dke * etb
                    dq = dq_d if dq is None else dq + dq_d
                    dkk = dk_d if dkk is None else dkk + dk_d
                    dvv = dv_d if dvv is None else dvv + dv_d
                if first:
                    dq_ref[rows, :] = dq
                    dk_ref[rows, :] = dkk
                else:
                    dq_ref[rows, :] += dq
                    dk_ref[rows, :] += dkk
                dv_ref[rows, vl] = dvv
                return carry

            lax.fori_loop(0, n_chunks, emit, 0, unroll=2)

    n_pairs, blk_k, blk_gb, blk_v = _gla_specs(s)
    vk, vv = _sds(q.shape), _sds(v.shape)
    state = pltpu.VMEM((n_chunks, dv, pw), F32)
    scratch = [state, state, pltpu.VMEM((s, pw), F32), pltpu.VMEM((s, pw), F32), pltpu.VMEM((n_chunks, 1, pw), F32),
               pltpu.VMEM((n_chunks, 1, pw), F32), state, state, pltpu.VMEM((dv, pw), F32)]
    return pl.pallas_call(
        body, name="gla_bwd", grid=(n_pairs,), in_specs=[blk_k, blk_k, blk_v, blk_k, blk_gb, blk_v],
        out_specs=[blk_k, blk_k, blk_v, blk_k, blk_k], out_shape=[vk, vk, vv, vk, vk],
        scratch_shapes=scratch, compiler_params=_params(("parallel",)),
    )(q, k, v, la, la, do)


@jax.custom_vjp
def gla(q, k, v, la):
    return _gla_fwd(q, k, v, la)


def _gla_f(q, k, v, la):
    return _gla_fwd(q, k, v, la), (q, k, v, la)


def _gla_b(res, do):
    dq, dk, dv, dgf, dgb = _gla_bwd(*res, do)
    return dq, dk, dv, jnp.concatenate([dgf, dgb], axis=1)


gla.defvjp(_gla_f, _gla_b)


ATTN_TQ = 256
HEAD_LANES = 128


def _attn_blocks(s, tq):
    per_q = pl.BlockSpec((tq, HEAD_LANES), lambda h, j: (j, h))
    k_nope = pl.BlockSpec((s, HEAD_LANES), lambda h, j: (0, 2 * h))
    v_blk = pl.BlockSpec((s, HEAD_LANES), lambda h, j: (0, 2 * h + 1))
    k_rope = pl.BlockSpec((s, HEAD_LANES), lambda h, j: (0, 0))
    lse = pl.BlockSpec((1, tq, 1), lambda h, j: (h, j, 0))
    return per_q, k_nope, v_blk, k_rope, lse


def _attn_fwd(qn, qr, kv, kr):
    s = qn.shape[0]
    tq = min(ATTN_TQ, s)
    scale = (MLA_NOPE + MLA_ROPE) ** -0.5

    def body(qn_ref, qr_ref, kn_ref, v_ref, kr_ref, o_ref, lse_ref):
        q = jnp.concatenate([qn_ref[...], qr_ref[...]], axis=1)
        k = jnp.concatenate([kn_ref[...], kr_ref[...]], axis=1)
        sc = _dot(q, k, 1, 1) * scale
        m = jnp.max(sc, axis=-1, keepdims=True)
        p = jnp.exp(sc - m)
        l = jnp.sum(p, axis=-1, keepdims=True)
        p = p * (1.0 / l)
        o_ref[...] = _dot(p.astype(BF16), v_ref[...], 1, 0)
        lse_ref[0] = m + jnp.log(l)

    per_q, k_nope, v_blk, k_rope, lse = _attn_blocks(s, tq)
    return pl.pallas_call(
        body, name="attn_fwd", grid=(MLA_HEADS, s // tq), in_specs=[per_q, per_q, k_nope, v_blk, k_rope],
        out_specs=[per_q, lse], out_shape=[_sds(qn.shape), _sds((MLA_HEADS, s, 1))],
        compiler_params=_params(("parallel", "parallel")),
    )(qn, qr, kv, kv, kr)


def _attn_bwd(qn, qr, kv, kr, o, lse, do):
    s = qn.shape[0]
    tq = min(ATTN_TQ, s)
    n_q = s // tq
    scale = (MLA_NOPE + MLA_ROPE) ** -0.5

    def body(qn_ref, qr_ref, kn_ref, v_ref, kr_ref, o_ref, lse_ref, do_ref, dqn_ref, dqr_ref, dkv_ref, dkr_ref,
             dk_acc, dv_acc, dkr_acc):
        h, j = pl.program_id(0), pl.program_id(1)
        q = jnp.concatenate([qn_ref[...], qr_ref[...]], axis=1)
        k = jnp.concatenate([kn_ref[...], kr_ref[...]], axis=1)
        do = do_ref[...]
        do_b = do.astype(BF16)
        p = jnp.exp(_dot(q, k, 1, 1) * scale - lse_ref[0])
        dp = _dot(do_b, v_ref[...], 1, 1)
        delta = jnp.sum(do * o_ref[...], axis=-1, keepdims=True)
        ds = (p * (dp - delta) * scale).astype(BF16)
        dq = _dot(ds, k, 1, 0)
        dqn_ref[...] = dq[:, :HEAD_LANES].astype(BF16)
        dqr_ref[...] = dq[:, HEAD_LANES:].astype(BF16)
        dk = _dot(ds, q, 0, 0)
        _acc(j, dk_acc, dk[:, :HEAD_LANES])
        _acc(j, dv_acc, _dot(p.astype(BF16), do_b, 0, 0))
        _acc(jnp.where(jnp.logical_and(h == 0, j == 0), 0, 1), dkr_acc, dk[:, HEAD_LANES:])

        @pl.when(j == n_q - 1)
        def _():
            dkv_ref[:, 0:HEAD_LANES] = dk_acc[...].astype(BF16)
            dkv_ref[:, HEAD_LANES:2 * HEAD_LANES] = dv_acc[...].astype(BF16)

        @pl.when(jnp.logical_and(h == MLA_HEADS - 1, j == n_q - 1))
        def _():
            dkr_ref[...] = dkr_acc[...].astype(BF16)

    per_q, k_nope, v_blk, k_rope, lse_blk = _attn_blocks(s, tq)
    dkv_blk = pl.BlockSpec((s, 2 * HEAD_LANES), lambda h, j: (0, h))
    acc = pltpu.VMEM((s, HEAD_LANES), F32)
    return pl.pallas_call(
        body, name="attn_bwd", grid=(MLA_HEADS, n_q),
        in_specs=[per_q, per_q, k_nope, v_blk, k_rope, per_q, lse_blk, per_q],
        out_specs=[per_q, per_q, dkv_blk, k_rope],
        out_shape=[_sds(qn.shape, BF16), _sds(qr.shape, BF16), _sds(kv.shape, BF16), _sds(kr.shape, BF16)],
        scratch_shapes=[acc, acc, acc], compiler_params=_params(("arbitrary", "arbitrary")),
    )(qn, qr, kv, kv, kr, o, lse, do)


@jax.custom_vjp
def attn(qn, qr, kv, kr):
    return _attn_fwd(qn, qr, kv, kr)[0]


def _attn_f(qn, qr, kv, kr):
    o, lse = _attn_fwd(qn, qr, kv, kr)
    return o, (qn, qr, kv, kr, o, lse)


def _attn_b(res, do):
    return tuple(_attn_bwd(*res, do))


attn.defvjp(_attn_f, _attn_b)


@jax.custom_vjp
def split_proj(proj):
    out, at = [], 0
    for _, _, _, wp in PROJ_SEGS:
        out.append(proj[:, at:at + wp])
        at += wp
    return tuple(out)


def _split_f(proj):
    return split_proj(proj), None


def _split_b(_, gs):
    return (jnp.concatenate(gs, axis=1),)


split_proj.defvjp(_split_f, _split_b)


def _tile2d(rows, width, limit=BLOCK_BYTES):
    fits = [t for t in range(16, rows + 1, 16) if rows % t == 0 and t * width * 4 <= limit]
    if fits and (fits[-1] >= 64 or fits[-1] == rows):
        return fits[-1], width
    if rows * width * 4 <= limit:
        return rows, width
    cols = [t for t in range(128, width + 1, 128) if width % t == 0 and rows * t * 4 <= limit]
    return (rows, cols[-1]) if cols else (rows, width)


def _add_pair(stacked, theirs, c_idx):
    g, r, w = theirs.shape
    tr, tc = _tile2d(r, w)

    def body(c_ref, a_ref, b_ref, o_ref):
        o_ref[0] = (a_ref[0, 0].astype(F32) + b_ref[0].astype(F32)).astype(BF16)

    blk = pl.BlockSpec((1, tr, tc), lambda k, i, j, c: (k, i, j))
    spec = pltpu.PrefetchScalarGridSpec(
        num_scalar_prefetch=1, grid=(g, r // tr, w // tc),
        in_specs=[pl.BlockSpec((1, 1, tr, tc), lambda k, i, j, c: (c[0], k, i, j)), blk], out_specs=blk)
    return pl.pallas_call(body, name="add_pair", grid_spec=spec, out_shape=_sds(theirs.shape, BF16),
                          compiler_params=_params(("parallel", "parallel", "parallel")))(c_idx, stacked, theirs)


def _add_chips(pair, landed, chip_idx):
    _, r, w = pair.shape
    tr, tc = _tile2d(r, w)

    def body(c_ref, p_ref, l0_ref, l1_ref, l2_ref, o_ref):
        o_ref[...] = ((p_ref[0].astype(F32) + l0_ref[0].astype(F32)) + l1_ref[0].astype(F32)) + l2_ref[0].astype(F32)

    specs = [pl.BlockSpec((1, tr, tc), lambda i, j, c: (c[0], i, j))]
    specs += [pl.BlockSpec((1, tr, tc), functools.partial(lambda i, j, c, k: (k, i, j), k=k)) for k in range(N_CHIPS - 1)]
    spec = pltpu.PrefetchScalarGridSpec(num_scalar_prefetch=1, grid=(r // tr, w // tc), in_specs=specs,
                                        out_specs=pl.BlockSpec((tr, tc), lambda i, j, c: (i, j)))
    return pl.pallas_call(body, name="add_chips", grid_spec=spec, out_shape=_sds((r, w)),
                          compiler_params=_params(("parallel", "parallel")))(chip_idx, pair, landed, landed, landed)


def _sum_devices(g):
    n = g.shape[2]

    def body(g_ref, o_ref, done_ref):
        t = g_ref[0]
        for j in range(1, N_DEV):
            t = t + g_ref[j]
        o_ref[...] = t
        done_ref[...] = jnp.zeros_like(done_ref)

    return pl.pallas_call(body, name="sum_devices", out_shape=[_sds((1, n)), _sds((8, 128))],
                          compiler_params=_params())(g)


def _adamw(w, g, m, v):
    shp = w.shape
    shp3 = (1, 1, shp[0]) if len(shp) == 1 else (-1,) + tuple(shp[-2:])
    w3, g3, m3, v3 = (t.reshape(shp3) for t in (w, g, m, v))
    c1 = 1.0 - ADAM_B1 ** ADAM_STEP
    c2 = 1.0 - ADAM_B2 ** ADAM_STEP

    def body(w_ref, g_ref, m_ref, v_ref, d_ref, mo_ref, vo_ref):
        gv = g_ref[...]
        mn = ADAM_B1 * m_ref[...] + (1.0 - ADAM_B1) * gv
        vn = ADAM_B2 * v_ref[...] + (1.0 - ADAM_B2) * (gv * gv)
        d_ref[...] = -ADAM_LR * ((mn / c1) / (jnp.sqrt(vn / c2) + ADAM_EPS) + ADAM_WD * w_ref[...])
        mo_ref[...] = mn
        vo_ref[...] = vn

    nl, r, wd = w3.shape
    tr, tc = _tile2d(r, wd, BLOCK_BYTES // 2)
    blk = pl.BlockSpec((1, tr, tc), lambda l, i, j: (l, i, j))
    s3 = _sds(w3.shape)
    d, mn, vn = pl.pallas_call(
        body, name="adamw", grid=(nl, r // tr, wd // tc), in_specs=[blk] * 4, out_specs=[blk] * 3,
        out_shape=[s3, s3, s3], compiler_params=_params(("parallel", "parallel", "parallel")),
    )(w3, g3, m3, v3)
    return d.reshape(shp), mn.reshape(shp), vn.reshape(shp)


ANY = pl.BlockSpec(memory_space=pl.ANY)
PIECE_BYTES = 1 << 20


def _place():
    return lax.axis_index("x"), lax.axis_index("y"), lax.axis_index("c")


def _pieces(shape, itemsize):
    if len(shape) >= 3:
        return [(i,) + p for i in range(shape[0]) for p in _pieces(shape[1:], itemsize)]
    rows = shape[0]
    row_bytes = itemsize
    for dsz in shape[1:]:
        row_bytes *= dsz
    k = 1
    while rows % (2 * k) == 0 and (rows // (2 * k)) % 16 == 0 and (rows // k) * row_bytes > PIECE_BYTES:
        k *= 2
    step = rows // k
    return [(pl.ds(j * step, step),) for j in range(k)]


def _split_start(make, src, dst, pieces):
    for p in pieces:
        make(src.at[p], dst.at[p]).start()
    return make(src, dst)


def _comm_call(body, name, arrs, out_shapes, n_remote, n_local):
    return pl.pallas_call(
        body, name=name, in_specs=[ANY] * len(arrs), out_specs=[ANY] * len(out_shapes), out_shape=out_shapes,
        scratch_shapes=[pltpu.SemaphoreType.DMA((n_remote,)), pltpu.SemaphoreType.DMA((n_remote,)),
                        pltpu.SemaphoreType.DMA((n_local,))],
    )(*arrs)


def all_gather8(arrs, name):
    n = len(arrs)
    pieces = [_pieces(a.shape, a.dtype.itemsize) for a in arrs]

    def body(*refs):
        ins, outs = refs[:n], refs[n:2 * n]
        send, recv, _ = refs[2 * n:]
        x, y, c = _place()
        me, sib = (x, y, c), (x, y, 1 - c)
        chips = [(1 - x, y), (x, 1 - y), (1 - x, 1 - y)]

        def slot(p):
            return 4 * p[0] + 2 * p[1] + p[2]

        def maker(t, k, to):
            def make(s, d):
                return pltpu.make_async_remote_copy(src_ref=s, dst_ref=d, send_sem=send.at[7 * t + k],
                                                    recv_sem=recv.at[7 * t + k], device_id=to, device_id_type=MESH)
            return make

        def landing(t, k, block):
            dst = outs[t].at[slot(block)]
            return maker(t, k, me)(dst, dst)

        sent = []
        for t in range(n):
            dst = outs[t].at[slot(me)]
            sent.append(_split_start(maker(t, 0, sib), ins[t], dst, pieces[t]))
            for j, chip in enumerate(chips):
                sent.append(_split_start(maker(t, 1 + j, (*chip, c)), ins[t], dst, pieces[t]))
        for j, chip in enumerate(chips):
            for t in range(n):
                landing(t, 1 + j, (*chip, c)).wait_recv()
                blk = outs[t].at[slot((*chip, c))]
                sent.append(_split_start(maker(t, 4 + j, sib), blk, blk, pieces[t]))
        for t in range(n):
            landing(t, 0, sib).wait_recv()
            for j, chip in enumerate(chips):
                landing(t, 4 + j, (*chip, 1 - c)).wait_recv()
        for cp in sent:
            cp.wait_send()

    outs = [_sds((N_DEV,) + a.shape, a.dtype) for a in arrs]
    got = _comm_call(body, name, arrs, outs, 7 * n, 1)
    x, y, c = _place()
    return [lax.dynamic_update_index_in_dim(g, a, 4 * x + 2 * y + c, 0) for g, a in zip(got, arrs)]


def sibling_send(arrs, name):
    n = len(arrs)
    pieces = [_pieces(a.shape[1:], a.dtype.itemsize) for a in arrs]

    def body(*refs):
        ins, theirs = refs[:n], refs[n:2 * n]
        send, recv, _ = refs[2 * n:]
        x, y, c = _place()
        rem = []
        for t in range(n):
            def make(s, d, t=t):
                return pltpu.make_async_remote_copy(src_ref=s, dst_ref=d, send_sem=send.at[t], recv_sem=recv.at[t],
                                                    device_id=(x, y, 1 - c), device_id_type=MESH)
            rem.append(_split_start(make, ins[t].at[1 - c], theirs[t], pieces[t]))
        for cp in rem:
            cp.wait_recv()
        for cp in rem:
            cp.wait_send()

    outs = [_sds(a.shape[1:], a.dtype) for a in arrs]
    return _comm_call(body, name, arrs, outs, n, 1)


def exchange_chips(arrs, name):
    n = len(arrs)
    pieces = [_pieces(a.shape[1:], a.dtype.itemsize) for a in arrs]

    def body(*refs):
        ins, outs = refs[:n], refs[n:2 * n]
        send, recv, _ = refs[2 * n:]
        x, y, c = _place()
        peers = [(1 - x, y), (x, 1 - y), (1 - x, 1 - y)]
        rem = []
        for t in range(n):
            for j, (px, py) in enumerate(peers):
                def make(s, d, t=t, j=j, px=px, py=py):
                    return pltpu.make_async_remote_copy(
                        src_ref=s, dst_ref=d, send_sem=send.at[3 * t + j], recv_sem=recv.at[3 * t + j],
                        device_id=(px, py, c), device_id_type=MESH)
                rem.append(_split_start(make, ins[t].at[2 * px + py], outs[t].at[j], pieces[t]))
        for cp in rem:
            cp.wait_recv()
        for cp in rem:
            cp.wait_send()

    outs = [_sds((N_CHIPS - 1,) + a.shape[1:], a.dtype) for a in arrs]
    return _comm_call(body, name, arrs, outs, 3 * n, 1)


def sibling_swap(arrs, name):
    n = len(arrs)
    pieces = [_pieces(a.shape, a.dtype.itemsize) for a in arrs]

    def body(*refs):
        ins, outs = refs[:n], refs[n:2 * n]
        send, recv, _ = refs[2 * n:]
        x, y, c = _place()
        rem = []
        for t in range(n):
            def make(s, d, t=t):
                return pltpu.make_async_remote_copy(src_ref=s, dst_ref=d, send_sem=send.at[t], recv_sem=recv.at[t],
                                                    device_id=(x, y, 1 - c), device_id_type=MESH)
            rem.append(_split_start(make, ins[t], outs[t], pieces[t]))
        for cp in rem:
            cp.wait_recv()
        for cp in rem:
            cp.wait_send()

    outs = [_sds(a.shape, a.dtype) for a in arrs]
    return _comm_call(body, name, arrs, outs, n, 1)


def _peer_copies(srcs, lands, send, recv, mode):
    x, y, c = _place()
    my_chip = 2 * x + y
    out = []
    for t in range(len(srcs)):
        for j, (px, py) in enumerate([(1 - x, y), (x, 1 - y), (1 - x, 1 - y)]):
            if mode == "gather":
                s, dst = srcs[t], lands[t].at[c, my_chip]
            else:
                s, dst = srcs[t].at[2 * px + py], lands[t].at[j]
            out.append(pltpu.make_async_remote_copy(
                src_ref=s, dst_ref=dst, send_sem=send.at[3 * t + j], recv_sem=recv.at[3 * t + j],
                device_id=(px, py, c), device_id_type=MESH))
    return out


HBM = pl.BlockSpec(memory_space=pltpu.HBM)
SEM = pl.BlockSpec(memory_space=pltpu.SEMAPHORE)
EFFECT = pltpu.SideEffectType.DATAFLOW_SIDE_EFFECTING


def ici_start(srcs, lands, mode, name):
    n = len(srcs)

    def body(*refs):
        send, recv = refs[2 * n], refs[2 * n + 1]
        for cp in _peer_copies(refs[:n], refs[n:2 * n], send, recv, mode):
            cp.start()
        refs[-1][...] = jnp.zeros_like(refs[-1])

    thru = [pltpu.HBM(a.shape, a.dtype) for a in list(srcs) + list(lands)]
    outs = pl.pallas_call(
        body, name=name, in_specs=[HBM] * (2 * n), out_specs=[SEM, SEM] + [HBM] * (2 * n) + [pl.BlockSpec(memory_space=pltpu.VMEM)],
        out_shape=[pltpu.SemaphoreType.DMA((3 * n,)), pltpu.SemaphoreType.DMA((3 * n,))] + thru + [_sds((8, 128))],
        input_output_aliases={i: 2 + i for i in range(2 * n)},
        compiler_params=pltpu.CompilerParams(has_side_effects=EFFECT),
    )(*[pltpu.with_memory_space_constraint(a, pltpu.HBM) for a in list(srcs) + list(lands)])
    return dict(send=outs[0], recv=outs[1], srcs=outs[2:2 + n], lands=outs[2 + n:2 + 2 * n], token=outs[-1])


def ici_wait(handle, after, mode, name):
    n = len(handle["srcs"])

    def body(*refs):
        send, recv = refs[2 * n], refs[2 * n + 1]
        for cp in _peer_copies(refs[:n], refs[n:2 * n], send, recv, mode):
            cp.wait_send()
            cp.wait_recv()

    arrs = list(handle["srcs"]) + list(handle["lands"])
    outs = pl.pallas_call(
        body, name=name, in_specs=[HBM] * (2 * n) + [SEM, SEM, ANY], out_specs=[HBM] * (2 * n),
        out_shape=[pltpu.HBM(a.shape, a.dtype) for a in arrs], input_output_aliases={i: i for i in range(2 * n)},
        compiler_params=pltpu.CompilerParams(has_side_effects=EFFECT),
    )(*arrs, handle["send"], handle["recv"], after)
    return outs[:n], outs[n:]


def gather_share(blocks, lands, name):
    n = len(blocks)

    def body(*refs):
        own, buf = refs[:n], refs[2 * n:3 * n]
        done, send, recv = refs[3 * n:]
        x, y, c = _place()
        my_chip = 2 * x + y
        chips = [2 * (1 - x) + y, 2 * x + (1 - y), 2 * (1 - x) + (1 - y)]
        sent = []
        for t in range(n):
            def make(s, d, k, t=t):
                return pltpu.make_async_remote_copy(src_ref=s, dst_ref=d, send_sem=send.at[4 * t + k],
                                                    recv_sem=recv.at[4 * t + k], device_id=(x, y, 1 - c),
                                                    device_id_type=MESH)
            cp = make(own[t], buf[t].at[c, my_chip], 0)
            cp.start()
            sent.append(cp)
            for k, pc in enumerate(chips):
                cp = make(buf[t].at[c, pc], buf[t].at[c, pc], 1 + k)
                cp.start()
                sent.append(cp)
        for t in range(n):
            for k in range(4):
                got = buf[t].at[1 - c, k]
                pltpu.make_async_remote_copy(src_ref=got, dst_ref=got, send_sem=send.at[4 * t + k],
                                             recv_sem=recv.at[4 * t + k], device_id=(x, y, 1 - c),
                                             device_id_type=MESH).wait_recv()
        for cp in sent:
            cp.wait_send()
        done[...] = jnp.zeros_like(done)

    outs = pl.pallas_call(
        body, name=name, in_specs=[ANY] * (2 * n), out_specs=[ANY] * n + [pl.BlockSpec(memory_space=pltpu.VMEM)],
        out_shape=[_sds(a.shape, a.dtype) for a in lands] + [_sds((8, 128))],
        input_output_aliases={n + t: t for t in range(n)},
        scratch_shapes=[pltpu.SemaphoreType.DMA((4 * n,)), pltpu.SemaphoreType.DMA((4 * n,))],
    )(*blocks, *lands)
    return outs[:n], outs[n]


@jax.custom_vjp
def _build_w_in(w4):
    full = w4.reshape(-1, w4.shape[-1])
    parts = []
    for _, start, width, wp in PROJ_SEGS:
        if width:
            parts.append(full[start:start + width])
        if wp > width:
            parts.append(jnp.zeros((wp - width, full.shape[1]), full.dtype))
    return jnp.concatenate(parts, axis=0)


def _build_w_in_f(w4):
    return _build_w_in(w4), None


def _build_w_in_b(_, g):
    parts, at = [], 0
    for _, _, width, wp in PROJ_SEGS:
        if width:
            parts.append(g[at:at + width])
        at += wp
    return (jnp.concatenate(parts, axis=0).reshape(N_CHIPS, -1, g.shape[1]),)


_build_w_in.defvjp(_build_w_in_f, _build_w_in_b)


def _split_w_uq(w):
    w3 = w.reshape(w.shape[0], MLA_HEADS, MLA_NOPE + MLA_ROPE)
    return w3[:, :, :MLA_NOPE].reshape(w.shape[0], -1), w3[:, :, MLA_NOPE:].reshape(w.shape[0], -1)


def _swap_halves(t, width):
    t3 = t.reshape(t.shape[0], -1, 2, width // 2)
    return jnp.concatenate([t3[:, :, 1:], t3[:, :, :1]], axis=2).reshape(t.shape)


def _pad_heads(t, width):
    t3 = t.reshape(t.shape[0], -1, width)
    t3 = jnp.pad(t3, ((0, 0), (0, 0), (0, HEAD_LANES - width)))
    return t3.reshape(t.shape[0], -1).astype(BF16)


def _layer(xh, mod, big, small, rope_q, rope_k):
    d = D_MODEL
    shift, scale, gate = mod[None, 0:d], mod[None, d:2 * d], mod[None, 2 * d:3 * d]
    w_al = _build_w_in(big["w_in"])
    proj = mod_mm(xh, small["norm_g"][None], scale, shift, w_al)
    gq, gk, gv, glr, mq, mkv, mkr, cb, cc, cx, _, z = split_proj(proj)

    rk = GLA_RANK
    hk = GLA_HEADS * GLA_DK
    wg = jnp.zeros((128, 2 * hk), F32)
    wg = wg.at[0:rk, 0:hk].set(small["gla_wg_f"]).at[rk:2 * rk, hk:].set(small["gla_wg_b"])
    bg = jnp.concatenate([small["gla_bg_f"], small["gla_bg_b"]])[None]
    la = gate_act(mm(glr, wg), bg)
    o_gla = rmsnorm(gla(gq, gk, gv, la), small["gla_norm_g"][None])

    cq = rmsnorm(mq, small["mla_q_norm_g"][None])
    w_nope, w_rope = _split_w_uq(jnp.concatenate([big["w_uq"][j] for j in range(N_CHIPS)], axis=1))
    qn = mm16(cq, w_nope)
    qr = mm(cq, w_rope)
    qr = fma(qr, rope_q[0], _swap_halves(qr, MLA_ROPE), rope_q[1])
    ckv = rmsnorm(mkv, small["mla_kv_norm_g"][None])
    kv = mm16(ckv, jnp.concatenate([big["w_ukv"][j] for j in range(N_CHIPS)], axis=1))
    kr = mkr[:, :MLA_ROPE]
    kr = fma(kr, rope_k[0], _swap_halves(kr, MLA_ROPE), rope_k[1])
    o_mla = rmsnorm(attn(qn, _pad_heads(qr, MLA_ROPE), kv, _pad_heads(kr, MLA_ROPE)), small["mla_out_g"][None])

    cw = jnp.concatenate([small["conv_w"], jnp.zeros((5, CONV_CH), F32)], axis=0)
    o_conv = rmsnorm(conv_op(cb, cc, cx, cw), small["conv_out_g"][None])

    o = jnp.concatenate([o_gla, o_mla, o_conv], axis=1)
    w_out = big["w_out"].reshape(d, d)
    return out_block(o, z, w_out, xh, gate)


SMALL_REPL = ("norm_g", "gla_bg_f", "gla_bg_b", "gla_norm_g", "mla_q_norm_g", "mla_kv_norm_g", "mla_out_g",
              "conv_out_g")
SMALL_SHARDED = ("gla_wg_f", "gla_wg_b", "conv_w")
BIG = ("w_in", "w_out", "w_uq", "w_ukv")
HALF_AXIS = (1, 0, 0, 0)


def kernel(x, c, positions, ada_w, ada_b, norm_g, w_in, gla_wg_f, gla_bg_f, gla_wg_b, gla_bg_b, gla_norm_g, mla_q_norm_g, mla_kv_norm_g, mla_w_uq, mla_w_ukv, mla_out_g, conv_w, conv_out_g, w_out, final_g, loss_target, m_ada_w, m_ada_b, m_norm_g, m_w_in, m_gla_wg_f, m_gla_bg_f, m_gla_wg_b, m_gla_bg_b, m_gla_norm_g, m_mla_q_norm_g, m_mla_kv_norm_g, m_mla_w_uq, m_mla_w_ukv, m_mla_out_g, m_conv_w, m_conv_out_g, m_w_out, m_final_g, v_ada_w, v_ada_b, v_norm_g, v_w_in, v_gla_wg_f, v_gla_bg_f, v_gla_wg_b, v_gla_bg_b, v_gla_norm_g, v_mla_q_norm_g, v_mla_kv_norm_g, v_mla_w_uq, v_mla_w_ukv, v_mla_out_g, v_conv_w, v_conv_out_g, v_w_out, v_final_g):
    xi, yi, ci = _place()
    chip = 2 * xi + yi
    dev = 2 * chip + ci
    s = x.shape[1]
    d = D_MODEL
    weights = dict(ada_w=ada_w, ada_b=ada_b, norm_g=norm_g, w_in=w_in, gla_wg_f=gla_wg_f, gla_bg_f=gla_bg_f,
                   gla_wg_b=gla_wg_b, gla_bg_b=gla_bg_b, gla_norm_g=gla_norm_g, mla_q_norm_g=mla_q_norm_g,
                   mla_kv_norm_g=mla_kv_norm_g, mla_w_uq=mla_w_uq, mla_w_ukv=mla_w_ukv, mla_out_g=mla_out_g,
                   conv_w=conv_w, conv_out_g=conv_out_g, w_out=w_out, final_g=final_g)
    m_in = dict(ada_w=m_ada_w, ada_b=m_ada_b, norm_g=m_norm_g, w_in=m_w_in, gla_wg_f=m_gla_wg_f, gla_bg_f=m_gla_bg_f,
                gla_wg_b=m_gla_wg_b, gla_bg_b=m_gla_bg_b, gla_norm_g=m_gla_norm_g, mla_q_norm_g=m_mla_q_norm_g,
                mla_kv_norm_g=m_mla_kv_norm_g, mla_w_uq=m_mla_w_uq, mla_w_ukv=m_mla_w_ukv, mla_out_g=m_mla_out_g,
                conv_w=m_conv_w, conv_out_g=m_conv_out_g, w_out=m_w_out, final_g=m_final_g)
    v_in = dict(ada_w=v_ada_w, ada_b=v_ada_b, norm_g=v_norm_g, w_in=v_w_in, gla_wg_f=v_gla_wg_f, gla_bg_f=v_gla_bg_f,
                gla_wg_b=v_gla_wg_b, gla_bg_b=v_gla_bg_b, gla_norm_g=v_gla_norm_g, mla_q_norm_g=v_mla_q_norm_g,
                mla_kv_norm_g=v_mla_kv_norm_g, mla_w_uq=v_mla_w_uq, mla_w_ukv=v_mla_w_ukv, mla_out_g=v_mla_out_g,
                conv_w=v_conv_w, conv_out_g=v_conv_out_g, w_out=v_w_out, final_g=v_final_g)

    g_c, g_wgf, g_wgb, g_cw = all_gather8([c, gla_wg_f, gla_wg_b, conv_w], "gather_small")

    def unshard_cols(g):
        g4 = g[0::2]
        return g4.transpose(1, 2, 0, 3).reshape(g4.shape[1], g4.shape[2], -1)

    small_full = dict(gla_wg_f=unshard_cols(g_wgf), gla_wg_b=unshard_cols(g_wgb), conv_w=unshard_cols(g_cw))
    for nme in SMALL_REPL:
        small_full[nme] = weights[nme]
    smalls = [{nme: small_full[nme][l] for nme in SMALL_REPL + SMALL_SHARDED} for l in range(DEPTH)]

    big_src = (jnp.swapaxes(w_in, 1, 2), w_out, mla_w_uq, mla_w_ukv)

    def my_halves(l, zero=0):
        out = []
        for t, a in enumerate(big_src):
            n_half = a.shape[1 + HALF_AXIS[t]] // 2
            out.append(lax.dynamic_slice_in_dim(a[l], ci * n_half + zero, n_half, axis=HALF_AXIS[t]).astype(BF16))
        return out

    def landing(blocks):
        return [lax.empty((2, N_CHIPS) + b.shape, b.dtype) for b in blocks]

    def finish_gather(handle, after, tag):
        blocks, lands = ici_wait(handle, after, "gather", "gather_wait" + tag)
        lands, done = gather_share(blocks, lands, "gather_share" + tag)
        full = [lax.dynamic_update_slice(g, b[None, None], (ci, chip) + (0,) * b.ndim) for g, b in zip(lands, blocks)]
        return full, done

    halves0 = my_halves(0)
    started0 = ici_start(halves0, landing(halves0), "gather", "gather_start0")

    c_act = _silu_rows(g_c[:, 0, :])
    c_act16 = jnp.concatenate([c_act, jnp.zeros_like(c_act)], axis=0)
    n_ada = ada_w.shape[2]
    parts = []
    for l in range(DEPTH):
        bias = lax.dynamic_slice_in_dim(ada_b[l], chip * n_ada, n_ada)[None]
        parts.append(_mm(c_act16, ada_w[l], bias=bias, name="ada_fwd"))
    g_mod, = all_gather8([jnp.stack(parts)], "gather_mod")
    mod_mine = lax.dynamic_index_in_dim(g_mod[0::2], dev, 2, keepdims=False)
    mods = mod_mine.transpose(1, 0, 2).reshape(DEPTH, 3 * d)

    inv_freq = ROPE_THETA ** (-jnp.arange(0, MLA_ROPE, 2, dtype=F32) / MLA_ROPE)
    ang = positions[0].astype(F32)[:, None] * inv_freq
    cos, sin = jnp.cos(ang), jnp.sin(ang)
    rope_k = (jnp.concatenate([cos, cos], axis=1), jnp.concatenate([-sin, sin], axis=1))
    rope_q = (jnp.tile(rope_k[0], (1, MLA_HEADS)), jnp.tile(rope_k[1], (1, MLA_HEADS)))

    def run_layer(xh, mod, gathered, small):
        big = {nme: jnp.concatenate([g[0], g[1]], axis=HALF_AXIS[t] + 1) for t, (nme, g) in enumerate(zip(BIG, gathered))}
        return _layer(xh, mod, big, small, rope_q, rope_k)

    def head(hh, fg):
        return loss_op(rmsnorm(hh, fg[None]), loss_target[0])[0, 0]

    gathered0, done0 = finish_gather(started0, mods, "0")
    halves1 = my_halves(1, done0[0, 0].astype(jnp.int32))
    started1 = ici_start(halves1, landing(halves1), "gather", "gather_start1")
    h1, vjp0 = jax.vjp(run_layer, x[0], mods[0] + started1["token"][0, 0], gathered0, smalls[0])
    gathered1, _ = finish_gather(started1, h1, "1")
    h2, vjp1 = jax.vjp(run_layer, h1, mods[1], gathered1, smalls[1])
    loss_dev, vjp_head = jax.vjp(head, h2, final_g)
    loss = lax.psum(loss_dev, ("x", "y", "c"))
    dh2, dfinal = vjp_head(jnp.ones((), F32))

    c_idx = jnp.reshape(ci, (1,)).astype(jnp.int32)
    chip_idx = jnp.reshape(chip, (1,)).astype(jnp.int32)

    def reduce_begin(dgath, tag, zero=None):
        theirs = sibling_send(dgath, "reduce_sibling" + tag)
        pair = [_add_pair(a, b, c_idx) for a, b in zip(dgath, theirs)]
        shapes = [(N_CHIPS - 1,) + p.shape[1:] for p in pair]
        if zero is None:
            lands = [lax.empty(shp, BF16) for shp in shapes]
        else:
            lands = [jnp.broadcast_to(zero.astype(BF16), shp) for shp in shapes]
        return ici_start(pair, lands, "reduce", "reduce_start" + tag)

    def reduce_end(handle, after, tag):
        pair, landed = ici_wait(handle, after, "reduce", "reduce_wait" + tag)
        reduced = [_add_chips(p, q, chip_idx) for p, q in zip(pair, landed)]
        others = sibling_swap(reduced, "share_sibling" + tag)
        return [jnp.where(ci == 0, jnp.concatenate([own, other], axis=HALF_AXIS[t]),
                          jnp.concatenate([other, own], axis=HALF_AXIS[t]))
                for t, (own, other) in enumerate(zip(reduced, others))]

    dh1, dmod1, dgath1, dsmall1 = vjp1(dh2)
    reducing1 = reduce_begin(dgath1, "1")
    dx, dmod0, dgath0, dsmall0 = vjp0(dh1 + reducing1["token"][0, 0])
    big_grads1 = reduce_end(reducing1, dx, "1")
    dmods = jnp.stack([dmod0, dmod1])
    dsmalls = [dsmall0, dsmall1]

    pieces = [dmods.reshape(-1), dfinal]
    for nme in SMALL_REPL + SMALL_SHARDED:
        pieces.append(jnp.stack([dsmalls[l][nme] for l in range(DEPTH)]).reshape(-1))
    sizes = [p.shape[0] for p in pieces]
    flat = jnp.concatenate(pieces)
    padn = (-flat.shape[0]) % 128
    flat = jnp.concatenate([flat, jnp.zeros((padn,), F32)])[None]
    g_small, = all_gather8([flat], "gather_small_grads")
    total, small_done = _sum_devices(g_small)
    total = total[0]
    reducing0 = reduce_begin(dgath0, "0", small_done[0, 0])
    offs, at = [], 0
    for n_el in sizes:
        offs.append(at)
        at += n_el

    def piece(i, shape):
        return total[offs[i]:offs[i] + sizes[i]].reshape(shape)

    grads = {"ada_b": piece(0, (DEPTH, 3 * d)), "final_g": piece(1, (d,))}
    for i, nme in enumerate(SMALL_REPL + SMALL_SHARDED):
        full = piece(2 + i, small_full[nme].shape)
        if nme in SMALL_SHARDED:
            ncol = weights[nme].shape[2]
            full = lax.dynamic_slice_in_dim(full, chip * ncol, ncol, axis=2)
        grads[nme] = full

    dmod_all = g_small[:, 0, :DEPTH * 3 * d].reshape(N_DEV, DEPTH, 3 * d)
    dmod_cols = lax.dynamic_slice_in_dim(dmod_all, chip * n_ada, n_ada, axis=2)
    g_ada = []
    for l in range(DEPTH):
        dm16 = jnp.concatenate([dmod_cols[:, l], jnp.zeros((N_DEV, n_ada), F32)], axis=0)
        g_ada.append(_mm(c_act16, dm16, ta=True, name="ada_bwd"))
    grads["ada_w"] = jnp.stack(g_ada)

    order = list(weights)
    big_names = ("w_in", "w_out", "mla_w_uq", "mla_w_ukv")
    delta, new_m, new_v = {}, {}, {}
    for nme in order:
        if nme not in big_names:
            delta[nme], new_m[nme], new_v[nme] = _adamw(weights[nme], grads[nme], m_in[nme], v_in[nme])
    big_grads0 = reduce_end(reducing0, delta["ada_w"], "0")
    for nme, g0, g1 in zip(big_names, big_grads0, big_grads1):
        grads[nme] = jnp.stack([g0, g1])
    for nme in big_names:
        if nme == "w_in":
            w_t, m_t, v_t = (jnp.swapaxes(t, 1, 2) for t in (w_in, m_w_in, v_w_in))
            res = _adamw(w_t, grads[nme], m_t, v_t)
            delta[nme], new_m[nme], new_v[nme] = (jnp.swapaxes(t, 1, 2) for t in res)
            grads[nme] = jnp.swapaxes(grads[nme], 1, 2)
            continue
        delta[nme], new_m[nme], new_v[nme] = _adamw(weights[nme], grads[nme], m_in[nme], v_in[nme])
    return (loss, dx[None], *[grads[n_] for n_ in order], *[delta[n_] for n_ in order],
            *[new_m[n_] for n_ in order], *[new_v[n_] for n_ in order])
```

```python
import functools

import jax
import jax.numpy as jnp
from jax import lax
from jax.experimental import pallas as pl
from jax.experimental.pallas import tpu as pltpu

F32 = jnp.float32
BF16 = jnp.bfloat16
MESH = pl.DeviceIdType.MESH
HIGHEST = lax.Precision.HIGHEST

DEPTH = 2
D_MODEL = 2048
GLA_HEADS = 6
GLA_DK = 64
GLA_DV = 128
GLA_RANK = 16
GLA_TEMP = 16.0
GLA_CHUNK = 64
GLA_W = GLA_HEADS * GLA_DV
MLA_HEADS = 6
MLA_QL = 384
MLA_KVL = 256
MLA_NOPE = 128
MLA_ROPE = 64
MLA_DV = 128
MLA_W = MLA_HEADS * MLA_DV
CONV_CH = D_MODEL - GLA_W - MLA_W
ROPE_THETA = 10000.0
EPS = 1e-6
IN_DIM = 5856
N_CHIPS = 4
N_DEV = 8

ADAM_LR = 0.001
ADAM_B1 = 0.9
ADAM_B2 = 0.999
ADAM_EPS = 1e-08
ADAM_WD = 0.01
ADAM_STEP = 10

PROJ_SEGS = (
    ("gq", 0, 384, 384), ("gk", 384, 384, 384), ("gv", 768, 768, 768), ("glr", 1536, 32, 128),
    ("mq", 1568, 384, 384), ("mkv", 1952, 256, 256), ("mkr", 2208, 64, 128),
    ("cb", 2272, 512, 512), ("cc", 2784, 512, 512), ("cx", 3296, 512, 512),
    ("pad", 3808, 0, 128), ("z", 3808, 2048, 2048),
)
PROJ_AL = sum(s[3] for s in PROJ_SEGS)

ANY = pl.BlockSpec(memory_space=pl.ANY)
VMEM_LIMIT = 48 * 1024 * 1024
BLOCK_BYTES = 2 * 1024 * 1024


def _params(sem=None):
    return pltpu.CompilerParams(dimension_semantics=sem, vmem_limit_bytes=VMEM_LIMIT)


def _dot(a, b, ca, cb, precision=None):
    return lax.dot_general(a, b, (((ca,), (cb,)), ((), ())), preferred_element_type=F32, precision=precision)


def _tile(dim, prefs):
    for t in prefs:
        if dim % t == 0:
            return t
    return dim


def _pick_rows(rows, width, itemsize=4):
    for t in (2048, 1024, 512, 256, 128, 64, 32, 16, 8):
        if rows % t == 0 and t * width * itemsize <= BLOCK_BYTES:
            return t
    return rows


def _mm(a, b, *, ta=False, tb=False, bias=None, out_dtype=F32, name="mm"):
    if ta:
        K, M = a.shape
    else:
        M, K = a.shape
    if tb:
        N, Kb = b.shape
    else:
        Kb, N = b.shape
    assert K == Kb, (a.shape, b.shape, ta, tb)
    tm = _tile(M, (512, 256, 128))
    tn = _tile(N, (1024, 512, 384, 256, 128))
    tk = _tile(K, (2048, 1024, 512, 256, 128))
    nk = K // tk
    has_bias = bias is not None

    def body(*refs):
        a_ref, b_ref = refs[0], refs[1]
        bias_ref = refs[2] if has_bias else None
        o_ref = refs[3 if has_bias else 2]
        part = _dot(a_ref[...].astype(BF16), b_ref[...].astype(BF16), 0 if ta else 1, 1 if tb else 0)

        def finish(r):
            if has_bias:
                r = r + bias_ref[...]
            o_ref[...] = r.astype(out_dtype)

        if nk == 1:
            finish(part)
            return
        acc_ref = refs[-1]
        k = pl.program_id(2)

        @pl.when(k == 0)
        def _():
            acc_ref[...] = part

        @pl.when(k != 0)
        def _():
            acc_ref[...] += part

        @pl.when(k == nk - 1)
        def _():
            finish(acc_ref[...])

    a_spec = pl.BlockSpec((tk, tm), lambda i, j, k: (k, i)) if ta else pl.BlockSpec((tm, tk), lambda i, j, k: (i, k))
    b_spec = pl.BlockSpec((tn, tk), lambda i, j, k: (j, k)) if tb else pl.BlockSpec((tk, tn), lambda i, j, k: (k, j))
    in_specs = [a_spec, b_spec]
    args = [a, b]
    if has_bias:
        in_specs.append(pl.BlockSpec((1, tn), lambda i, j, k: (0, j)))
        args.append(bias)
    return pl.pallas_call(
        body, name=name, grid=(M // tm, N // tn, nk),
        in_specs=in_specs, out_specs=pl.BlockSpec((tm, tn), lambda i, j, k: (i, j)),
        out_shape=jax.ShapeDtypeStruct((M, N), out_dtype),
        scratch_shapes=[pltpu.VMEM((tm, tn), F32)] if nk > 1 else [],
        compiler_params=_params(("parallel", "parallel", "arbitrary")),
    )(*args)


@jax.custom_vjp
def mm(a, b):
    return _mm(a, b, name="mm_fwd")


def _mm_f(a, b):
    return _mm(a, b, name="mm_fwd"), (a, b)


def _mm_b(res, g):
    a, b = res
    return _mm(g, b, tb=True, out_dtype=a.dtype, name="mm_da"), _mm(a, g, ta=True, out_dtype=b.dtype, name="mm_db")


mm.defvjp(_mm_f, _mm_b)


@jax.custom_vjp
def mm16(a, b):
    return _mm(a, b, out_dtype=BF16, name="mm16_fwd")


def _mm16_f(a, b):
    return mm16(a, b), (a, b)


mm16.defvjp(_mm16_f, _mm_b)


def _rows(body, name, tiled, full, tiled_out, acc_out, tr=None):
    rows = tiled[0].shape[0]
    if tr is None:
        width = max([a.shape[1] for a in tiled] + [s.shape[1] for s in tiled_out])
        tr = _pick_rows(rows, width)
    in_specs = [pl.BlockSpec((tr, a.shape[1]), lambda i: (i, 0)) for a in tiled]
    in_specs += [pl.BlockSpec(a.shape, lambda i: (0, 0)) for a in full]
    out_specs = [pl.BlockSpec((tr, s.shape[1]), lambda i: (i, 0)) for s in tiled_out]
    out_specs += [pl.BlockSpec(s.shape, lambda i: (0, 0)) for s in acc_out]

    def wrapped(*refs):
        body(pl.program_id(0), *refs)

    outs = pl.pallas_call(
        wrapped, name=name, grid=(rows // tr,), in_specs=in_specs, out_specs=out_specs,
        out_shape=list(tiled_out) + list(acc_out),
        compiler_params=_params(("arbitrary",)),
    )(*tiled, *full)
    return outs


def _sds(shape, dtype=F32):
    return jax.ShapeDtypeStruct(tuple(shape), dtype)


def _acc(step, ref, val):
    @pl.when(step == 0)
    def _():
        ref[...] = val

    @pl.when(step != 0)
    def _():
        ref[...] += val


def _colsum(v):
    return jnp.sum(v, axis=0, keepdims=True)


def _rstd(x):
    return lax.rsqrt(jnp.mean(x * x, axis=-1, keepdims=True) + EPS)


def _norm_grid(x, g):
    rows, w = x.shape[0], g.shape[1]
    tr = _pick_rows(rows, w)
    blk = pl.BlockSpec((tr, w), lambda i, j: (i, j))
    gblk = pl.BlockSpec((1, w), lambda i, j: (0, 0))
    return (rows // tr, x.shape[1] // w), blk, gblk


@jax.custom_vjp
def rmsnorm(x, g):
    def body(x_ref, g_ref, o_ref):
        x = x_ref[...]
        o_ref[...] = x * _rstd(x) * g_ref[...]

    grid, blk, gblk = _norm_grid(x, g)
    return pl.pallas_call(body, name="rmsnorm_fwd", grid=grid, in_specs=[blk, gblk], out_specs=blk,
                          out_shape=_sds(x.shape), compiler_params=_params(("parallel", "parallel")))(x, g)


def _rmsnorm_f(x, g):
    return rmsnorm(x, g), (x, g)


def _rmsnorm_b(res, dy):
    x, g = res

    def body(x_ref, dy_ref, g_ref, dx_ref, dg_ref):
        x = x_ref[...]
        dy = dy_ref[...]
        r = _rstd(x)
        xh = x * r
        dxh = dy * g_ref[...]
        dx_ref[...] = r * (dxh - xh * jnp.mean(dxh * xh, axis=-1, keepdims=True))
        first = jnp.logical_and(pl.program_id(0) == 0, pl.program_id(1) == 0)
        _acc(jnp.where(first, 0, 1), dg_ref, _colsum(dy * xh))

    grid, blk, gblk = _norm_grid(x, g)
    dx, dg = pl.pallas_call(body, name="rmsnorm_bwd", grid=grid, in_specs=[blk, blk, gblk], out_specs=[blk, gblk],
                            out_shape=[_sds(x.shape), _sds(g.shape)],
                            compiler_params=_params(("arbitrary", "arbitrary")))(x, dy, g)
    return dx, dg


rmsnorm.defvjp(_rmsnorm_f, _rmsnorm_b)


def _modulate(x, g, scale, shift):
    def body(i, x_ref, g_ref, sc_ref, sh_ref, o_ref):
        x = x_ref[...]
        xn = x * _rstd(x) * g_ref[...]
        o_ref[...] = (xn * (1.0 + sc_ref[...]) + sh_ref[...]).astype(BF16)
    return _rows(body, "modulate_fwd", [x], [g, scale, shift], [_sds(x.shape, BF16)], [])[0]


def _modulate_bwd(x, g, scale, shift, dh):
    def body(i, x_ref, dh_ref, g_ref, sc_ref, dx_ref, dg_ref, dsc_ref, dsh_ref):
        x = x_ref[...]
        dh = dh_ref[...]
        gv = g_ref[...]
        r = _rstd(x)
        xh = x * r
        dxn = dh * (1.0 + sc_ref[...])
        dxh = dxn * gv
        dx_ref[...] = r * (dxh - xh * jnp.mean(dxh * xh, axis=-1, keepdims=True))
        _acc(i, dg_ref, _colsum(dxn * xh))
        _acc(i, dsc_ref, _colsum(dh * (xh * gv)))
        _acc(i, dsh_ref, _colsum(dh))

    v = _sds(g.shape)
    return _rows(body, "modulate_bwd", [x, dh], [g, scale], [_sds(x.shape)], [v, v, v])


@jax.custom_vjp
def mod_mm(x, g, scale, shift, wt):
    return _mm(_modulate(x, g, scale, shift), wt, tb=True, name="mm_in")


def _mod_mm_f(x, g, scale, shift, wt):
    h = _modulate(x, g, scale, shift)
    return _mm(h, wt, tb=True, name="mm_in"), (x, g, scale, shift, wt, h)


def _mod_mm_b(res, dproj):
    x, g, scale, shift, wt, h = res
    dproj = dproj.astype(BF16)
    dh = _mm(dproj, wt, name="mm_in_dh")
    dwt = _mm(dproj, h, ta=True, out_dtype=wt.dtype, name="mm_in_dw")
    dx, dg, dsc, dsh = _modulate_bwd(x, g, scale, shift, dh)
    return dx, dg, dsc, dsh, dwt


mod_mm.defvjp(_mod_mm_f, _mod_mm_b)


def _sigmoid(z):
    return 1.0 / (1.0 + jnp.exp(-z))


def _gate_mul(o, z):
    def body(i, o_ref, z_ref, y_ref):
        z = z_ref[...]
        y_ref[...] = (o_ref[...] * (z * _sigmoid(z))).astype(BF16)
    return _rows(body, "gate_mul_fwd", [o, z], [], [_sds(o.shape, BF16)], [])[0]


def _gate_mul_bwd(o, z, dy):
    def body(i, o_ref, z_ref, dy_ref, do_ref, dz_ref):
        z = z_ref[...]
        dy = dy_ref[...]
        s = _sigmoid(z)
        do_ref[...] = dy * (z * s)
        dz_ref[...] = dy * o_ref[...] * (s * (1.0 + z * (1.0 - s)))
    return _rows(body, "gate_mul_bwd", [o, z, dy], [], [_sds(o.shape), _sds(o.shape)], [])


def _residual(x, u, gate):
    def body(i, x_ref, u_ref, g_ref, o_ref):
        o_ref[...] = x_ref[...] + g_ref[...] * u_ref[...]
    return _rows(body, "residual_fwd", [x, u], [gate], [_sds(x.shape)], [])[0]


def _residual_bwd(d, u, gate):
    def body(i, d_ref, u_ref, g_ref, du_ref, dg_ref):
        d = d_ref[...]
        du_ref[...] = (g_ref[...] * d).astype(BF16)
        _acc(i, dg_ref, _colsum(d * u_ref[...]))

    return _rows(body, "residual_bwd", [d, u], [gate], [_sds(u.shape, BF16)], [_sds(gate.shape)])


@jax.custom_vjp
def out_block(o, z, w, x, gate):
    return _residual(x, _mm(_gate_mul(o, z), w, name="mm_out"), gate)


def _out_block_f(o, z, w, x, gate):
    y = _gate_mul(o, z)
    u = _mm(y, w, name="mm_out")
    return _residual(x, u, gate), (o, z, w, y, u, gate)


def _out_block_b(res, d):
    o, z, w, y, u, gate = res
    du, dgate = _residual_bwd(d, u, gate)
    dy = _mm(du, w, tb=True, name="mm_out_dy")
    dw = _mm(y, du, ta=True, out_dtype=w.dtype, name="mm_out_dw")
    do, dz = _gate_mul_bwd(o, z, dy)
    return do, dz, dw, d, dgate


out_block.defvjp(_out_block_f, _out_block_b)


@jax.custom_vjp
def gate_act(u, b):
    def body(i, u_ref, b_ref, o_ref):
        t = u_ref[...] + b_ref[...]
        o_ref[...] = (jnp.minimum(t, 0.0) - jnp.log(1.0 + jnp.exp(-jnp.abs(t)))) / GLA_TEMP
    return _rows(body, "gate_act_fwd", [u], [b], [_sds(u.shape)], [])[0]


def _gate_act_f(u, b):
    return gate_act(u, b), (u, b)


def _gate_act_b(res, d):
    u, b = res

    def body(i, u_ref, d_ref, b_ref, du_ref, db_ref):
        t = u_ref[...] + b_ref[...]
        du = d_ref[...] * _sigmoid(-t) / GLA_TEMP
        du_ref[...] = du
        _acc(i, db_ref, _colsum(du))

    du, db = _rows(body, "gate_act_bwd", [u, d], [b], [_sds(u.shape)], [_sds(b.shape)])
    return du, db


gate_act.defvjp(_gate_act_f, _gate_act_b)


@jax.custom_vjp
def fma(a, b, c, d):
    def body(i, a_ref, b_ref, c_ref, d_ref, o_ref):
        o_ref[...] = a_ref[...] * b_ref[...] + c_ref[...] * d_ref[...]
    return _rows(body, "fma_fwd", [a, b, c, d], [], [_sds(a.shape)], [])[0]


def _fma_f(a, b, c, d):
    return fma(a, b, c, d), (b, d)


def _fma_b(res, g):
    b, d = res

    def body(i, g_ref, b_ref, d_ref, da_ref, dc_ref):
        g = g_ref[...]
        da_ref[...] = g * b_ref[...]
        dc_ref[...] = g * d_ref[...]

    da, dc = _rows(body, "fma_bwd", [g, b, d], [], [_sds(g.shape), _sds(g.shape)], [])
    return da, jnp.zeros_like(b), dc, jnp.zeros_like(d)


fma.defvjp(_fma_f, _fma_b)


def _silu_rows(c):
    def body(i, c_ref, o_ref):
        v = c_ref[...]
        o_ref[...] = v * _sigmoid(v)
    return _rows(body, "silu", [c], [], [_sds(c.shape)], [])[0]


@jax.custom_vjp
def loss_op(y, t):
    return _loss_fwd(y, t)[0]


def _loss_fwd(y, t):
    inv = 1.0 / y.shape[1]

    def body(i, y_ref, t_ref, d_ref, l_ref):
        e = y_ref[...] - t_ref[...]
        d_ref[...] = e * inv
        _acc(i, l_ref, jnp.sum(_colsum(e * e), axis=1, keepdims=True) * (0.5 * inv))

    d, l = _rows(body, "loss_fwd", [y, t], [], [_sds(y.shape)], [_sds((1, 1))])
    return l, d


def _loss_f(y, t):
    l, d = _loss_fwd(y, t)
    return l, d


def _loss_b(d, g):
    return d * g, jnp.zeros_like(d)


loss_op.defvjp(_loss_f, _loss_b)


def _conv_terms(cc, cx, rows, n):
    u = cc * cx
    up = jnp.where(rows == 0, 0.0, pltpu.roll(u, 1, 0))
    un = jnp.where(rows == n - 1, 0.0, pltpu.roll(u, n - 1, 0))
    return u, up, un


CONV_COLS = 128


def _conv_specs(s, n_in):
    blk = pl.BlockSpec((s, CONV_COLS), lambda j: (0, j))
    wblk = pl.BlockSpec((8, CONV_COLS), lambda j: (0, j))
    return [blk] * n_in + [wblk], blk, wblk


@jax.custom_vjp
def conv_op(cb, cc, cx, w):
    s, ch = cb.shape

    def body(cb_ref, cc_ref, cx_ref, w_ref, o_ref):
        rows = lax.broadcasted_iota(jnp.int32, (s, CONV_COLS), 0)
        u, up, un = _conv_terms(cc_ref[...], cx_ref[...], rows, s)
        conv = up * w_ref[0:1, :] + u * w_ref[1:2, :] + un * w_ref[2:3, :]
        o_ref[...] = cb_ref[...] * conv

    in_specs, blk, _ = _conv_specs(s, 3)
    return pl.pallas_call(
        body, name="conv_fwd", grid=(ch // CONV_COLS,), in_specs=in_specs, out_specs=blk,
        out_shape=_sds(cb.shape), compiler_params=_params(("parallel",)),
    )(cb, cc, cx, w)


def _conv_f(cb, cc, cx, w):
    return conv_op(cb, cc, cx, w), (cb, cc, cx, w)


def _conv_b(res, d):
    cb, cc, cx, w = res
    s, ch = cb.shape

    def body(cb_ref, cc_ref, cx_ref, d_ref, w_ref, dcb_ref, dcc_ref, dcx_ref, dw_ref):
        rows = lax.broadcasted_iota(jnp.int32, (s, CONV_COLS), 0)
        cc_v = cc_ref[...]
        cx_v = cx_ref[...]
        u, up, un = _conv_terms(cc_v, cx_v, rows, s)
        w0, w1, w2 = w_ref[0:1, :], w_ref[1:2, :], w_ref[2:3, :]
        dv = d_ref[...]
        dcb_ref[...] = dv * (up * w0 + u * w1 + un * w2)
        dconv = dv * cb_ref[...]
        d_next = jnp.where(rows == s - 1, 0.0, pltpu.roll(dconv, s - 1, 0))
        d_prev = jnp.where(rows == 0, 0.0, pltpu.roll(dconv, 1, 0))
        du = w0 * d_next + w1 * dconv + w2 * d_prev
        dcc_ref[...] = du * cx_v
        dcx_ref[...] = du * cc_v
        dw_ref[...] = jnp.zeros_like(dw_ref)
        dw_ref[0:1, :] = _colsum(dconv * up)
        dw_ref[1:2, :] = _colsum(dconv * u)
        dw_ref[2:3, :] = _colsum(dconv * un)

    in_specs, blk, wblk = _conv_specs(s, 4)
    v = _sds(cb.shape)
    return tuple(pl.pallas_call(
        body, name="conv_bwd", grid=(ch // CONV_COLS,), in_specs=in_specs, out_specs=[blk, blk, blk, wblk],
        out_shape=[v, v, v, _sds(w.shape)], compiler_params=_params(("parallel",)),
    )(cb, cc, cx, d, w))


conv_op.defvjp(_conv_f, _conv_b)


def _gla_masks(rev):
    c = GLA_CHUNK
    row = lax.broadcasted_iota(jnp.int32, (c, c), 0)
    col = lax.broadcasted_iota(jnp.int32, (c, c), 1)
    mask = (row < col) if rev else (row >= col)
    return rev, mask


def _chunk_cumsum(g, rev):
    c = g.shape[0]
    row = lax.broadcasted_iota(jnp.int32, g.shape, 0)
    b = g
    s = 1
    while s < c:
        if rev:
            b = b + jnp.where(row < c - s, pltpu.roll(b, c - s, 0), 0.0)
        else:
            b = b + jnp.where(row >= s, pltpu.roll(b, s, 0), 0.0)
        s *= 2
    return b


GLA_UNROLL = 4


def _gla_rows(n):
    return pl.ds(pl.multiple_of(n * GLA_CHUNK, GLA_CHUNK), GLA_CHUNK)


def _gla_scan(s_ref, bt_ref, st_ref, n_chunks, descending):
    st_ref[...] = jnp.zeros_like(st_ref)

    def step(i, carry):
        n = (n_chunks - 1 - i) if descending else i
        own = s_ref[n]
        st = st_ref[...]
        s_ref[n] = st
        st_ref[...] = st * jnp.exp(bt_ref[n]) + own
        return carry

    lax.fori_loop(0, n_chunks, step, 0)


GLA_PAIR = 2


def _gla_specs(s):
    dk, dv = GLA_DK, GLA_DV
    n_pairs = GLA_HEADS // GLA_PAIR
    blk_k = pl.BlockSpec((s, GLA_PAIR * dk), lambda p: (0, p))
    blk_gb = pl.BlockSpec((s, GLA_PAIR * dk), lambda p: (0, n_pairs + p))
    blk_v = pl.BlockSpec((s, GLA_PAIR * dv), lambda p: (0, p))
    return n_pairs, blk_k, blk_gb, blk_v


def _head_lanes(hh):
    lane = lax.broadcasted_iota(jnp.int32, (1, GLA_PAIR * GLA_DK), 1)
    return jnp.logical_and(lane >= hh * GLA_DK, lane < (hh + 1) * GLA_DK)


def _gla_fwd(q, k, v, la):
    s = q.shape[0]
    dk, dv = GLA_DK, GLA_DV
    pw = GLA_PAIR * dk
    n_chunks = s // GLA_CHUNK
    scale = GLA_DK ** -0.5

    def body(q_ref, k_ref, v_ref, gf_ref, gb_ref, o_ref, sf_ref, sb_ref, bf_ref, bb_ref, btf_ref, btb_ref, st_ref):
        masks = [_gla_masks(rev) for rev in (False, True)]
        dirs = ((False, gf_ref, sf_ref, bf_ref, btf_ref), (True, gb_ref, sb_ref, bb_ref, btb_ref))

        def decays(n, carry):
            rows = _gla_rows(n)
            for rev, g_ref, _, b_ref, bt_ref in dirs:
                g = g_ref[rows, :]
                b_ref[rows, :] = _chunk_cumsum(g, rev)
                bt_ref[n] = _colsum(g)
            return carry

        lax.fori_loop(0, n_chunks, decays, 0, unroll=GLA_UNROLL)
        for hh in range(GLA_PAIR):
            m = _head_lanes(hh)
            vl = slice(hh * dv, (hh + 1) * dv)

            def prepare(n, carry, m=m, vl=vl):
                rows = _gla_rows(n)
                kk = k_ref[rows, :]
                vb = v_ref[rows, vl].astype(BF16)
                for rev, _, s_ref, b_ref, bt_ref in dirs:
                    ke = jnp.where(m, kk * jnp.exp(bt_ref[n] - b_ref[rows, :]), 0.0).astype(BF16)
                    s_ref[n] = _dot(vb, ke, 0, 0)
                return carry

            lax.fori_loop(0, n_chunks, prepare, 0, unroll=GLA_UNROLL)
            for rev, _, s_ref, _, bt_ref in dirs:
                _gla_scan(s_ref, bt_ref, st_ref, n_chunks, descending=rev)

            def emit(n, carry, m=m, vl=vl):
                rows = _gla_rows(n)
                qs = q_ref[rows, :] * scale
                kk = k_ref[rows, :]
                vb = v_ref[rows, vl].astype(BF16)
                o = None
                for (rev, _, s_ref, b_ref, _), (_, mask) in zip(dirs, masks):
                    b = b_ref[rows, :]
                    qd = jnp.where(m, qs * jnp.exp(b), 0.0).astype(BF16)
                    ki = jnp.where(m, kk * jnp.exp(-b), 0.0).astype(BF16)
                    a = jnp.where(mask, _dot(qd, ki, 1, 1), 0.0).astype(BF16)
                    od = _dot(a, vb, 1, 0) + _dot(qd, s_ref[n].astype(BF16), 1, 1)
                    o = od if o is None else o + od
                o_ref[rows, vl] = o
                return carry

            lax.fori_loop(0, n_chunks, emit, 0, unroll=GLA_UNROLL)

    n_pairs, blk_k, blk_gb, blk_v = _gla_specs(s)
    state = pltpu.VMEM((n_chunks, dv, pw), F32)
    scratch = [state, state, pltpu.VMEM((s, pw), F32), pltpu.VMEM((s, pw), F32), pltpu.VMEM((n_chunks, 1, pw), F32),
               pltpu.VMEM((n_chunks, 1, pw), F32), pltpu.VMEM((dv, pw), F32)]
    return pl.pallas_call(
        body, name="gla_fwd", grid=(n_pairs,), in_specs=[blk_k, blk_k, blk_v, blk_k, blk_gb],
        out_specs=blk_v, out_shape=_sds(v.shape), scratch_shapes=scratch,
        compiler_params=_params(("parallel",)),
    )(q, k, v, la, la)


def _gla_bwd(q, k, v, la, do):
    s = q.shape[0]
    dk, dv = GLA_DK, GLA_DV
    pw = GLA_PAIR * dk
    c = GLA_CHUNK
    n_chunks = s // c
    scale = GLA_DK ** -0.5

    def body(q_ref, k_ref, v_ref, gf_ref, gb_ref, do_ref, dq_ref, dk_ref, dv_ref, dgf_ref, dgb_ref,
             sf_ref, sb_ref, bf_ref, bb_ref, btf_ref, btb_ref, dsf_ref, dsb_ref, st_ref):
        masks = [_gla_masks(rev) for rev in (False, True)]
        rowc = lax.broadcasted_iota(jnp.int32, (c, pw), 0)
        dirs = ((False, gf_ref, sf_ref, bf_ref, btf_ref, dsf_ref, dgf_ref),
                (True, gb_ref, sb_ref, bb_ref, btb_ref, dsb_ref, dgb_ref))

        def decays(n, carry):
            rows = _gla_rows(n)
            for rev, g_ref, _, b_ref, bt_ref, _, _ in dirs:
                g = g_ref[rows, :]
                b_ref[rows, :] = _chunk_cumsum(g, rev)
                bt_ref[n] = _colsum(g)
            return carry

        lax.fori_loop(0, n_chunks, decays, 0, unroll=GLA_UNROLL)
        for hh in range(GLA_PAIR):
            m = _head_lanes(hh)
            vl = slice(hh * dv, (hh + 1) * dv)

            def prepare(n, carry, m=m, vl=vl):
                rows = _gla_rows(n)
                qs = q_ref[rows, :] * scale
                kk = k_ref[rows, :]
                vb = v_ref[rows, vl].astype(BF16)
                do_b = do_ref[rows, vl].astype(BF16)
                for rev, _, s_ref, b_ref, bt_ref, ds_ref, _ in dirs:
                    b = b_ref[rows, :]
                    ke = jnp.where(m, kk * jnp.exp(bt_ref[n] - b), 0.0).astype(BF16)
                    qd = jnp.where(m, qs * jnp.exp(b), 0.0).astype(BF16)
                    s_ref[n] = _dot(vb, ke, 0, 0)
                    ds_ref[n] = _dot(do_b, qd, 0, 0)
                return carry

            lax.fori_loop(0, n_chunks, prepare, 0, unroll=GLA_UNROLL)
            for rev, _, s_ref, _, bt_ref, ds_ref, _ in dirs:
                _gla_scan(s_ref, bt_ref, st_ref, n_chunks, descending=rev)
                _gla_scan(ds_ref, bt_ref, st_ref, n_chunks, descending=not rev)

            def emit(n, carry, m=m, vl=vl, first=(hh == 0)):
                rows = _gla_rows(n)
                qs = q_ref[rows, :] * scale
                kk = k_ref[rows, :]
                vb = v_ref[rows, vl].astype(BF16)
                do_b = do_ref[rows, vl].astype(BF16)
                dq = dkk = dvv = None
                for (rev, _, s_ref, b_ref, bt_ref, ds_ref, dg_ref), (_, mask) in zip(dirs, masks):
                    b = b_ref[rows, :]
                    bt = bt_ref[n]
                    eb = jnp.where(m, jnp.exp(b), 0.0)
                    enb = jnp.where(m, jnp.exp(-b), 0.0)
                    etb = jnp.where(m, jnp.exp(bt - b), 0.0)
                    ebt = jnp.exp(bt)
                    qd = qs * eb
                    ki = kk * enb
                    ke = kk * etb
                    qd_b, ki_b, ke_b = qd.astype(BF16), ki.astype(BF16), ke.astype(BF16)
                    st = s_ref[n]
                    dst = ds_ref[n]
                    dst_b = dst.astype(BF16)
                    a = jnp.where(mask, _dot(qd_b, ki_b, 1, 1), 0.0).astype(BF16)
                    da = jnp.where(mask, _dot(do_b, vb, 1, 1), 0.0).astype(BF16)
                    dv_d = _dot(a, do_b, 0, 0) + _dot(ke_b, dst_b, 1, 1)
                    dqd = _dot(da, ki_b, 1, 0) + _dot(do_b, st.astype(BF16), 1, 0)
                    dki = _dot(da, qd_b, 0, 0)
                    dke = _dot(vb, dst_b, 1, 0)
                    dbt = _colsum(st * dst) * ebt + _colsum(dke * ke)
                    db = dqd * qd - dki * ki - dke * ke
                    db = db + jnp.where(rowc == (0 if rev else c - 1), dbt, 0.0)
                    dg = _chunk_cumsum(db, not rev)
                    if first:
                        dg_ref[rows, :] = dg
                    else:
                        dg_ref[rows, :] += dg
                    dq_d = dqd * eb * scale
                    dk_d = dki * enb + dke * etb
                    dq = dq_d if dq is None else dq + dq_d
                    dkk = dk_d if dkk is None else dkk + dk_d
                    dvv = dv_d if dvv is None else dvv + dv_d
                if first:
                    dq_ref[rows, :] = dq
                    dk_ref[rows, :] = dkk
                else:
                    dq_ref[rows, :] += dq
                    dk_ref[rows, :] += dkk
                dv_ref[rows, vl] = dvv
                return carry

            lax.fori_loop(0, n_chunks, emit, 0, unroll=2)

    n_pairs, blk_k, blk_gb, blk_v = _gla_specs(s)
    vk, vv = _sds(q.shape), _sds(v.shape)
    state = pltpu.VMEM((n_chunks, dv, pw), F32)
    scratch = [state, state, pltpu.VMEM((s, pw), F32), pltpu.VMEM((s, pw), F32), pltpu.VMEM((n_chunks, 1, pw), F32),
               pltpu.VMEM((n_chunks, 1, pw), F32), state, state, pltpu.VMEM((dv, pw), F32)]
    return pl.pallas_call(
        body, name="gla_bwd", grid=(n_pairs,), in_specs=[blk_k, blk_k, blk_v, blk_k, blk_gb, blk_v],
        out_specs=[blk_k, blk_k, blk_v, blk_k, blk_k], out_shape=[vk, vk, vv, vk, vk],
        scratch_shapes=scratch, compiler_params=_params(("parallel",)),
    )(q, k, v, la, la, do)


@jax.custom_vjp
def gla(q, k, v, la):
    return _gla_fwd(q, k, v, la)


def _gla_f(q, k, v, la):
    return _gla_fwd(q, k, v, la), (q, k, v, la)


def _gla_b(res, do):
    dq, dk, dv, dgf, dgb = _gla_bwd(*res, do)
    return dq, dk, dv, jnp.concatenate([dgf, dgb], axis=1)


gla.defvjp(_gla_f, _gla_b)


ATTN_TQ = 256
HEAD_LANES = 128


def _attn_blocks(s, tq):
    per_q = pl.BlockSpec((tq, HEAD_LANES), lambda h, j: (j, h))
    k_nope = pl.BlockSpec((s, HEAD_LANES), lambda h, j: (0, 2 * h))
    v_blk = pl.BlockSpec((s, HEAD_LANES), lambda h, j: (0, 2 * h + 1))
    k_rope = pl.BlockSpec((s, HEAD_LANES), lambda h, j: (0, 0))
    lse = pl.BlockSpec((1, tq, 1), lambda h, j: (h, j, 0))
    return per_q, k_nope, v_blk, k_rope, lse


def _attn_fwd(qn, qr, kv, kr):
    s = qn.shape[0]
    tq = min(ATTN_TQ, s)
    scale = (MLA_NOPE + MLA_ROPE) ** -0.5

    def body(qn_ref, qr_ref, kn_ref, v_ref, kr_ref, o_ref, lse_ref):
        q = jnp.concatenate([qn_ref[...], qr_ref[...]], axis=1)
        k = jnp.concatenate([kn_ref[...], kr_ref[...]], axis=1)
        sc = _dot(q, k, 1, 1) * scale
        m = jnp.max(sc, axis=-1, keepdims=True)
        p = jnp.exp(sc - m)
        l = jnp.sum(p, axis=-1, keepdims=True)
        p = p * (1.0 / l)
        o_ref[...] = _dot(p.astype(BF16), v_ref[...], 1, 0)
        lse_ref[0] = m + jnp.log(l)

    per_q, k_nope, v_blk, k_rope, lse = _attn_blocks(s, tq)
    return pl.pallas_call(
        body, name="attn_fwd", grid=(MLA_HEADS, s // tq), in_specs=[per_q, per_q, k_nope, v_blk, k_rope],
        out_specs=[per_q, lse], out_shape=[_sds(qn.shape), _sds((MLA_HEADS, s, 1))],
        compiler_params=_params(("parallel", "parallel")),
    )(qn, qr, kv, kv, kr)


def _attn_bwd(qn, qr, kv, kr, o, lse, do):
    s = qn.shape[0]
    tq = min(ATTN_TQ, s)
    n_q = s // tq
    scale = (MLA_NOPE + MLA_ROPE) ** -0.5

    def body(qn_ref, qr_ref, kn_ref, v_ref, kr_ref, o_ref, lse_ref, do_ref, dqn_ref, dqr_ref, dkv_ref, dkr_ref,
             dk_acc, dv_acc, dkr_acc):
        h, j = pl.program_id(0), pl.program_id(1)
        q = jnp.concatenate([qn_ref[...], qr_ref[...]], axis=1)
        k = jnp.concatenate([kn_ref[...], kr_ref[...]], axis=1)
        do = do_ref[...]
        do_b = do.astype(BF16)
        p = jnp.exp(_dot(q, k, 1, 1) * scale - lse_ref[0])
        dp = _dot(do_b, v_ref[...], 1, 1)
        delta = jnp.sum(do * o_ref[...], axis=-1, keepdims=True)
        ds = (p * (dp - delta) * scale).astype(BF16)
        dq = _dot(ds, k, 1, 0)
        dqn_ref[...] = dq[:, :HEAD_LANES].astype(BF16)
        dqr_ref[...] = dq[:, HEAD_LANES:].astype(BF16)
        dk = _dot(ds, q, 0, 0)
        _acc(j, dk_acc, dk[:, :HEAD_LANES])
        _acc(j, dv_acc, _dot(p.astype(BF16), do_b, 0, 0))
        _acc(jnp.where(jnp.logical_and(h == 0, j == 0), 0, 1), dkr_acc, dk[:, HEAD_LANES:])

        @pl.when(j == n_q - 1)
        def _():
            dkv_ref[:, 0:HEAD_LANES] = dk_acc[...].astype(BF16)
            dkv_ref[:, HEAD_LANES:2 * HEAD_LANES] = dv_acc[...].astype(BF16)

        @pl.when(jnp.logical_and(h == MLA_HEADS - 1, j == n_q - 1))
        def _():
            dkr_ref[...] = dkr_acc[...].astype(BF16)

    per_q, k_nope, v_blk, k_rope, lse_blk = _attn_blocks(s, tq)
    dkv_blk = pl.BlockSpec((s, 2 * HEAD_LANES), lambda h, j: (0, h))
    acc = pltpu.VMEM((s, HEAD_LANES), F32)
    return pl.pallas_call(
        body, name="attn_bwd", grid=(MLA_HEADS, n_q),
        in_specs=[per_q, per_q, k_nope, v_blk, k_rope, per_q, lse_blk, per_q],
        out_specs=[per_q, per_q, dkv_blk, k_rope],
        out_shape=[_sds(qn.shape, BF16), _sds(qr.shape, BF16), _sds(kv.shape, BF16), _sds(kr.shape, BF16)],
        scratch_shapes=[acc, acc, acc], compiler_params=_params(("arbitrary", "arbitrary")),
    )(qn, qr, kv, kv, kr, o, lse, do)


@jax.custom_vjp
def attn(qn, qr, kv, kr):
    return _attn_fwd(qn, qr, kv, kr)[0]


def _attn_f(qn, qr, kv, kr):
    o, lse = _attn_fwd(qn, qr, kv, kr)
    return o, (qn, qr, kv, kr, o, lse)


def _attn_b(res, do):
    return tuple(_attn_bwd(*res, do))


attn.defvjp(_attn_f, _attn_b)


@jax.custom_vjp
def split_proj(proj):
    out, at = [], 0
    for _, _, _, wp in PROJ_SEGS:
        out.append(proj[:, at:at + wp])
        at += wp
    return tuple(out)


def _split_f(proj):
    return split_proj(proj), None


def _split_b(_, gs):
    return (jnp.concatenate(gs, axis=1),)


split_proj.defvjp(_split_f, _split_b)


def _tile2d(rows, width, limit=BLOCK_BYTES):
    fits = [t for t in range(16, rows + 1, 16) if rows % t == 0 and t * width * 4 <= limit]
    if fits and (fits[-1] >= 64 or fits[-1] == rows):
        return fits[-1], width
    if rows * width * 4 <= limit:
        return rows, width
    cols = [t for t in range(128, width + 1, 128) if width % t == 0 and rows * t * 4 <= limit]
    return (rows, cols[-1]) if cols else (rows, width)


def _add_pair(stacked, theirs, c_idx):
    g, r, w = theirs.shape
    tr, tc = _tile2d(r, w)

    def body(c_ref, a_ref, b_ref, o_ref):
        o_ref[0] = (a_ref[0, 0].astype(F32) + b_ref[0].astype(F32)).astype(BF16)

    blk = pl.BlockSpec((1, tr, tc), lambda k, i, j, c: (k, i, j))
    spec = pltpu.PrefetchScalarGridSpec(
        num_scalar_prefetch=1, grid=(g, r // tr, w // tc),
        in_specs=[pl.BlockSpec((1, 1, tr, tc), lambda k, i, j, c: (c[0], k, i, j)), blk], out_specs=blk)
    return pl.pallas_call(body, name="add_pair", grid_spec=spec, out_shape=_sds(theirs.shape, BF16),
                          compiler_params=_params(("parallel", "parallel", "parallel")))(c_idx, stacked, theirs)


def _add_chips(pair, landed, chip_idx):
    _, r, w = pair.shape
    tr, tc = _tile2d(r, w)

    def body(c_ref, p_ref, l0_ref, l1_ref, l2_ref, o_ref):
        o_ref[...] = ((p_ref[0].astype(F32) + l0_ref[0].astype(F32)) + l1_ref[0].astype(F32)) + l2_ref[0].astype(F32)

    specs = [pl.BlockSpec((1, tr, tc), lambda i, j, c: (c[0], i, j))]
    specs += [pl.BlockSpec((1, tr, tc), functools.partial(lambda i, j, c, k: (k, i, j), k=k)) for k in range(N_CHIPS - 1)]
    spec = pltpu.PrefetchScalarGridSpec(num_scalar_prefetch=1, grid=(r // tr, w // tc), in_specs=specs,
                                        out_specs=pl.BlockSpec((tr, tc), lambda i, j, c: (i, j)))
    return pl.pallas_call(body, name="add_chips", grid_spec=spec, out_shape=_sds((r, w)),
                          compiler_params=_params(("parallel", "parallel")))(chip_idx, pair, landed, landed, landed)


def _sum_devices(g):
    n = g.shape[2]

    def body(g_ref, o_ref, done_ref):
        t = g_ref[0]
        for j in range(1, N_DEV):
            t = t + g_ref[j]
        o_ref[...] = t
        done_ref[...] = jnp.zeros_like(done_ref)

    return pl.pallas_call(body, name="sum_devices", out_shape=[_sds((1, n)), _sds((8, 128))],
                          compiler_params=_params())(g)


def _adamw_math(w, gv, m, v):
    c1 = 1.0 - ADAM_B1 ** ADAM_STEP
    c2 = 1.0 - ADAM_B2 ** ADAM_STEP
    mn = ADAM_B1 * m + (1.0 - ADAM_B1) * gv
    vn = ADAM_B2 * v + (1.0 - ADAM_B2) * (gv * gv)
    return -ADAM_LR * ((mn / c1) / (jnp.sqrt(vn / c2) + ADAM_EPS) + ADAM_WD * w), mn, vn


def _adamw(w, g, m, v):
    shp = w.shape
    shp3 = (1, 1, shp[0]) if len(shp) == 1 else (-1,) + tuple(shp[-2:])
    w3, g3, m3, v3 = (t.reshape(shp3) for t in (w, g, m, v))

    def body(w_ref, g_ref, m_ref, v_ref, d_ref, mo_ref, vo_ref):
        d_ref[...], mo_ref[...], vo_ref[...] = _adamw_math(w_ref[...], g_ref[...], m_ref[...], v_ref[...])

    nl, r, wd = w3.shape
    tr, tc = _tile2d(r, wd, BLOCK_BYTES // 2)
    blk = pl.BlockSpec((1, tr, tc), lambda l, i, j: (l, i, j))
    s3 = _sds(w3.shape)
    d, mn, vn = pl.pallas_call(
        body, name="adamw", grid=(nl, r // tr, wd // tc), in_specs=[blk] * 4, out_specs=[blk] * 3,
        out_shape=[s3, s3, s3], compiler_params=_params(("parallel", "parallel", "parallel")),
    )(w3, g3, m3, v3)
    return d.reshape(shp), mn.reshape(shp), vn.reshape(shp)


def _adamw_layer(w, g, m, v, layer, prev=None):
    nl, r, wd = w.shape
    tr, tc = _tile2d(r, wd, BLOCK_BYTES // 2)
    n_prev = 0 if prev is None else len(prev)

    def body(*refs):
        w_ref, g_ref, m_ref, v_ref = refs[:4]
        d_ref, mo_ref, vo_ref = refs[4 + n_prev:]
        d_ref[...], mo_ref[...], vo_ref[...] = _adamw_math(w_ref[...], g_ref[...], m_ref[...], v_ref[...])

    blk = pl.BlockSpec((1, tr, tc), lambda i, j: (layer, i, j))
    gblk = pl.BlockSpec((1, tr, tc), lambda i, j: (0, i, j))
    s3 = _sds(w.shape)
    return pl.pallas_call(
        body, name="adamw_layer%d" % layer, grid=(r // tr, wd // tc),
        in_specs=[blk, gblk, blk, blk] + [ANY] * n_prev, out_specs=[blk] * 3, out_shape=[s3, s3, s3],
        input_output_aliases={4 + k: k for k in range(n_prev)},
        compiler_params=_params(("parallel", "parallel")),
    )(w, g[None], m, v, *(prev or ()))


PIECE_BYTES = 1 << 20


def _place():
    return lax.axis_index("x"), lax.axis_index("y"), lax.axis_index("c")


def _pieces(shape, itemsize):
    if len(shape) >= 3:
        return [(i,) + p for i in range(shape[0]) for p in _pieces(shape[1:], itemsize)]
    rows = shape[0]
    row_bytes = itemsize
    for dsz in shape[1:]:
        row_bytes *= dsz
    k = 1
    while rows % (2 * k) == 0 and (rows // (2 * k)) % 16 == 0 and (rows // k) * row_bytes > PIECE_BYTES:
        k *= 2
    step = rows // k
    return [(pl.ds(j * step, step),) for j in range(k)]


def _split_start(make, src, dst, pieces):
    for p in pieces:
        make(src.at[p], dst.at[p]).start()
    return make(src, dst)


def _comm_call(body, name, arrs, out_shapes, n_remote, n_local):
    return pl.pallas_call(
        body, name=name, in_specs=[ANY] * len(arrs), out_specs=[ANY] * len(out_shapes), out_shape=out_shapes,
        scratch_shapes=[pltpu.SemaphoreType.DMA((n_remote,)), pltpu.SemaphoreType.DMA((n_remote,)),
                        pltpu.SemaphoreType.DMA((n_local,))],
    )(*arrs)


def all_gather8(arrs, name):
    n = len(arrs)
    pieces = [_pieces(a.shape, a.dtype.itemsize) for a in arrs]

    def body(*refs):
        ins, outs = refs[:n], refs[n:2 * n]
        send, recv, _ = refs[2 * n:]
        x, y, c = _place()
        me, sib = (x, y, c), (x, y, 1 - c)
        chips = [(1 - x, y), (x, 1 - y), (1 - x, 1 - y)]

        def slot(p):
            return 4 * p[0] + 2 * p[1] + p[2]

        def maker(t, k, to):
            def make(s, d):
                return pltpu.make_async_remote_copy(src_ref=s, dst_ref=d, send_sem=send.at[7 * t + k],
                                                    recv_sem=recv.at[7 * t + k], device_id=to, device_id_type=MESH)
            return make

        def landing(t, k, block):
            dst = outs[t].at[slot(block)]
            return maker(t, k, me)(dst, dst)

        sent = []
        for t in range(n):
            dst = outs[t].at[slot(me)]
            sent.append(_split_start(maker(t, 0, sib), ins[t], dst, pieces[t]))
            for j, chip in enumerate(chips):
                sent.append(_split_start(maker(t, 1 + j, (*chip, c)), ins[t], dst, pieces[t]))
        for j, chip in enumerate(chips):
            for t in range(n):
                landing(t, 1 + j, (*chip, c)).wait_recv()
                blk = outs[t].at[slot((*chip, c))]
                sent.append(_split_start(maker(t, 4 + j, sib), blk, blk, pieces[t]))
        for t in range(n):
            landing(t, 0, sib).wait_recv()
            for j, chip in enumerate(chips):
                landing(t, 4 + j, (*chip, 1 - c)).wait_recv()
        for cp in sent:
            cp.wait_send()

    outs = [_sds((N_DEV,) + a.shape, a.dtype) for a in arrs]
    got = _comm_call(body, name, arrs, outs, 7 * n, 1)
    x, y, c = _place()
    return [lax.dynamic_update_index_in_dim(g, a, 4 * x + 2 * y + c, 0) for g, a in zip(got, arrs)]


def sibling_send(arrs, name):
    n = len(arrs)
    pieces = [_pieces(a.shape[1:], a.dtype.itemsize) for a in arrs]

    def body(*refs):
        ins, theirs = refs[:n], refs[n:2 * n]
        send, recv, _ = refs[2 * n:]
        x, y, c = _place()
        rem = []
        for t in range(n):
            def make(s, d, t=t):
                return pltpu.make_async_remote_copy(src_ref=s, dst_ref=d, send_sem=send.at[t], recv_sem=recv.at[t],
                                                    device_id=(x, y, 1 - c), device_id_type=MESH)
            rem.append(_split_start(make, ins[t].at[1 - c], theirs[t], pieces[t]))
        for cp in rem:
            cp.wait_recv()
        for cp in rem:
            cp.wait_send()

    outs = [_sds(a.shape[1:], a.dtype) for a in arrs]
    return _comm_call(body, name, arrs, outs, n, 1)


def exchange_chips(arrs, name):
    n = len(arrs)
    pieces = [_pieces(a.shape[1:], a.dtype.itemsize) for a in arrs]

    def body(*refs):
        ins, outs = refs[:n], refs[n:2 * n]
        send, recv, _ = refs[2 * n:]
        x, y, c = _place()
        peers = [(1 - x, y), (x, 1 - y), (1 - x, 1 - y)]
        rem = []
        for t in range(n):
            for j, (px, py) in enumerate(peers):
                def make(s, d, t=t, j=j, px=px, py=py):
                    return pltpu.make_async_remote_copy(
                        src_ref=s, dst_ref=d, send_sem=send.at[3 * t + j], recv_sem=recv.at[3 * t + j],
                        device_id=(px, py, c), device_id_type=MESH)
                rem.append(_split_start(make, ins[t].at[2 * px + py], outs[t].at[j], pieces[t]))
        for cp in rem:
            cp.wait_recv()
        for cp in rem:
            cp.wait_send()

    outs = [_sds((N_CHIPS - 1,) + a.shape[1:], a.dtype) for a in arrs]
    return _comm_call(body, name, arrs, outs, 3 * n, 1)


def sibling_swap(arrs, name):
    n = len(arrs)
    pieces = [_pieces(a.shape, a.dtype.itemsize) for a in arrs]

    def body(*refs):
        ins, outs = refs[:n], refs[n:2 * n]
        send, recv, _ = refs[2 * n:]
        x, y, c = _place()
        rem = []
        for t in range(n):
            def make(s, d, t=t):
                return pltpu.make_async_remote_copy(src_ref=s, dst_ref=d, send_sem=send.at[t], recv_sem=recv.at[t],
                                                    device_id=(x, y, 1 - c), device_id_type=MESH)
            rem.append(_split_start(make, ins[t], outs[t], pieces[t]))
        for cp in rem:
            cp.wait_recv()
        for cp in rem:
            cp.wait_send()

    outs = [_sds(a.shape, a.dtype) for a in arrs]
    return _comm_call(body, name, arrs, outs, n, 1)


def _peer_copies(srcs, lands, send, recv, mode):
    x, y, c = _place()
    my_chip = 2 * x + y
    out = []
    for t in range(len(srcs)):
        for j, (px, py) in enumerate([(1 - x, y), (x, 1 - y), (1 - x, 1 - y)]):
            if mode == "gather":
                s, dst = srcs[t], lands[t].at[c, my_chip]
            else:
                s, dst = srcs[t].at[2 * px + py], lands[t].at[j]
            out.append(pltpu.make_async_remote_copy(
                src_ref=s, dst_ref=dst, send_sem=send.at[3 * t + j], recv_sem=recv.at[3 * t + j],
                device_id=(px, py, c), device_id_type=MESH))
    return out


HBM = pl.BlockSpec(memory_space=pltpu.HBM)
SEM = pl.BlockSpec(memory_space=pltpu.SEMAPHORE)
EFFECT = pltpu.SideEffectType.DATAFLOW_SIDE_EFFECTING


def ici_start(srcs, lands, mode, name):
    n = len(srcs)

    def body(*refs):
        send, recv = refs[2 * n], refs[2 * n + 1]
        for cp in _peer_copies(refs[:n], refs[n:2 * n], send, recv, mode):
            cp.start()
        refs[-1][...] = jnp.zeros_like(refs[-1])

    thru = [pltpu.HBM(a.shape, a.dtype) for a in list(srcs) + list(lands)]
    outs = pl.pallas_call(
        body, name=name, in_specs=[HBM] * (2 * n), out_specs=[SEM, SEM] + [HBM] * (2 * n) + [pl.BlockSpec(memory_space=pltpu.VMEM)],
        out_shape=[pltpu.SemaphoreType.DMA((3 * n,)), pltpu.SemaphoreType.DMA((3 * n,))] + thru + [_sds((8, 128))],
        input_output_aliases={i: 2 + i for i in range(2 * n)},
        compiler_params=pltpu.CompilerParams(has_side_effects=EFFECT),
    )(*[pltpu.with_memory_space_constraint(a, pltpu.HBM) for a in list(srcs) + list(lands)])
    return dict(send=outs[0], recv=outs[1], srcs=outs[2:2 + n], lands=outs[2 + n:2 + 2 * n], token=outs[-1])


def ici_wait(handle, after, mode, name):
    n = len(handle["srcs"])

    def body(*refs):
        send, recv = refs[2 * n], refs[2 * n + 1]
        for cp in _peer_copies(refs[:n], refs[n:2 * n], send, recv, mode):
            cp.wait_send()
            cp.wait_recv()

    arrs = list(handle["srcs"]) + list(handle["lands"])
    outs = pl.pallas_call(
        body, name=name, in_specs=[HBM] * (2 * n) + [SEM, SEM, ANY], out_specs=[HBM] * (2 * n),
        out_shape=[pltpu.HBM(a.shape, a.dtype) for a in arrs], input_output_aliases={i: i for i in range(2 * n)},
        compiler_params=pltpu.CompilerParams(has_side_effects=EFFECT),
    )(*arrs, handle["send"], handle["recv"], after)
    return outs[:n], outs[n:]


def gather_share(blocks, lands, name):
    n = len(blocks)

    def body(*refs):
        own, buf = refs[:n], refs[2 * n:3 * n]
        done, send, recv = refs[3 * n:]
        x, y, c = _place()
        my_chip = 2 * x + y
        chips = [2 * (1 - x) + y, 2 * x + (1 - y), 2 * (1 - x) + (1 - y)]
        sent = []
        for t in range(n):
            def make(s, d, k, t=t):
                return pltpu.make_async_remote_copy(src_ref=s, dst_ref=d, send_sem=send.at[4 * t + k],
                                                    recv_sem=recv.at[4 * t + k], device_id=(x, y, 1 - c),
                                                    device_id_type=MESH)
            cp = make(own[t], buf[t].at[c, my_chip], 0)
            cp.start()
            sent.append(cp)
            for k, pc in enumerate(chips):
                cp = make(buf[t].at[c, pc], buf[t].at[c, pc], 1 + k)
                cp.start()
                sent.append(cp)
        for t in range(n):
            for k in range(4):
                got = buf[t].at[1 - c, k]
                pltpu.make_async_remote_copy(src_ref=got, dst_ref=got, send_sem=send.at[4 * t + k],
                                             recv_sem=recv.at[4 * t + k], device_id=(x, y, 1 - c),
                                             device_id_type=MESH).wait_recv()
        for cp in sent:
            cp.wait_send()
        done[...] = jnp.zeros_like(done)

    outs = pl.pallas_call(
        body, name=name, in_specs=[ANY] * (2 * n), out_specs=[ANY] * n + [pl.BlockSpec(memory_space=pltpu.VMEM)],
        out_shape=[_sds(a.shape, a.dtype) for a in lands] + [_sds((8, 128))],
        input_output_aliases={n + t: t for t in range(n)},
        scratch_shapes=[pltpu.SemaphoreType.DMA((4 * n,)), pltpu.SemaphoreType.DMA((4 * n,))],
    )(*blocks, *lands)
    return outs[:n], outs[n]


@jax.custom_vjp
def _build_w_in(w4):
    full = w4.reshape(-1, w4.shape[-1])
    parts = []
    for _, start, width, wp in PROJ_SEGS:
        if width:
            parts.append(full[start:start + width])
        if wp > width:
            parts.append(jnp.zeros((wp - width, full.shape[1]), full.dtype))
    return jnp.concatenate(parts, axis=0)


def _build_w_in_f(w4):
    return _build_w_in(w4), None


def _build_w_in_b(_, g):
    parts, at = [], 0
    for _, _, width, wp in PROJ_SEGS:
        if width:
            parts.append(g[at:at + width])
        at += wp
    return (jnp.concatenate(parts, axis=0).reshape(N_CHIPS, -1, g.shape[1]),)


_build_w_in.defvjp(_build_w_in_f, _build_w_in_b)


def _split_w_uq(w):
    w3 = w.reshape(w.shape[0], MLA_HEADS, MLA_NOPE + MLA_ROPE)
    return w3[:, :, :MLA_NOPE].reshape(w.shape[0], -1), w3[:, :, MLA_NOPE:].reshape(w.shape[0], -1)


def _swap_halves(t, width):
    t3 = t.reshape(t.shape[0], -1, 2, width // 2)
    return jnp.concatenate([t3[:, :, 1:], t3[:, :, :1]], axis=2).reshape(t.shape)


def _pad_heads(t, width):
    t3 = t.reshape(t.shape[0], -1, width)
    t3 = jnp.pad(t3, ((0, 0), (0, 0), (0, HEAD_LANES - width)))
    return t3.reshape(t.shape[0], -1).astype(BF16)


def _layer(xh, mod, big, small, rope_q, rope_k):
    d = D_MODEL
    shift, scale, gate = mod[None, 0:d], mod[None, d:2 * d], mod[None, 2 * d:3 * d]
    w_al = _build_w_in(big["w_in"])
    proj = mod_mm(xh, small["norm_g"][None], scale, shift, w_al)
    gq, gk, gv, glr, mq, mkv, mkr, cb, cc, cx, _, z = split_proj(proj)

    rk = GLA_RANK
    hk = GLA_HEADS * GLA_DK
    wg = jnp.zeros((128, 2 * hk), F32)
    wg = wg.at[0:rk, 0:hk].set(small["gla_wg_f"]).at[rk:2 * rk, hk:].set(small["gla_wg_b"])
    bg = jnp.concatenate([small["gla_bg_f"], small["gla_bg_b"]])[None]
    la = gate_act(mm(glr, wg), bg)
    o_gla = rmsnorm(gla(gq, gk, gv, la), small["gla_norm_g"][None])

    cq = rmsnorm(mq, small["mla_q_norm_g"][None])
    w_nope, w_rope = _split_w_uq(jnp.concatenate([big["w_uq"][j] for j in range(N_CHIPS)], axis=1))
    qn = mm16(cq, w_nope)
    qr = mm(cq, w_rope)
    qr = fma(qr, rope_q[0], _swap_halves(qr, MLA_ROPE), rope_q[1])
    ckv = rmsnorm(mkv, small["mla_kv_norm_g"][None])
    kv = mm16(ckv, jnp.concatenate([big["w_ukv"][j] for j in range(N_CHIPS)], axis=1))
    kr = mkr[:, :MLA_ROPE]
    kr = fma(kr, rope_k[0], _swap_halves(kr, MLA_ROPE), rope_k[1])
    o_mla = rmsnorm(attn(qn, _pad_heads(qr, MLA_ROPE), kv, _pad_heads(kr, MLA_ROPE)), small["mla_out_g"][None])

    cw = jnp.concatenate([small["conv_w"], jnp.zeros((5, CONV_CH), F32)], axis=0)
    o_conv = rmsnorm(conv_op(cb, cc, cx, cw), small["conv_out_g"][None])

    o = jnp.concatenate([o_gla, o_mla, o_conv], axis=1)
    w_out = big["w_out"].reshape(d, d)
    return out_block(o, z, w_out, xh, gate)


SMALL_REPL = ("norm_g", "gla_bg_f", "gla_bg_b", "gla_norm_g", "mla_q_norm_g", "mla_kv_norm_g", "mla_out_g",
              "conv_out_g")
SMALL_SHARDED = ("gla_wg_f", "gla_wg_b", "conv_w")
BIG = ("w_in", "w_out", "w_uq", "w_ukv")
HALF_AXIS = (1, 0, 0, 0)


def kernel(x, c, positions, ada_w, ada_b, norm_g, w_in, gla_wg_f, gla_bg_f, gla_wg_b, gla_bg_b, gla_norm_g, mla_q_norm_g, mla_kv_norm_g, mla_w_uq, mla_w_ukv, mla_out_g, conv_w, conv_out_g, w_out, final_g, loss_target, m_ada_w, m_ada_b, m_norm_g, m_w_in, m_gla_wg_f, m_gla_bg_f, m_gla_wg_b, m_gla_bg_b, m_gla_norm_g, m_mla_q_norm_g, m_mla_kv_norm_g, m_mla_w_uq, m_mla_w_ukv, m_mla_out_g, m_conv_w, m_conv_out_g, m_w_out, m_final_g, v_ada_w, v_ada_b, v_norm_g, v_w_in, v_gla_wg_f, v_gla_bg_f, v_gla_wg_b, v_gla_bg_b, v_gla_norm_g, v_mla_q_norm_g, v_mla_kv_norm_g, v_mla_w_uq, v_mla_w_ukv, v_mla_out_g, v_conv_w, v_conv_out_g, v_w_out, v_final_g):
    xi, yi, ci = _place()
    chip = 2 * xi + yi
    dev = 2 * chip + ci
    s = x.shape[1]
    d = D_MODEL
    weights = dict(ada_w=ada_w, ada_b=ada_b, norm_g=norm_g, w_in=w_in, gla_wg_f=gla_wg_f, gla_bg_f=gla_bg_f,
                   gla_wg_b=gla_wg_b, gla_bg_b=gla_bg_b, gla_norm_g=gla_norm_g, mla_q_norm_g=mla_q_norm_g,
                   mla_kv_norm_g=mla_kv_norm_g, mla_w_uq=mla_w_uq, mla_w_ukv=mla_w_ukv, mla_out_g=mla_out_g,
                   conv_w=conv_w, conv_out_g=conv_out_g, w_out=w_out, final_g=final_g)
    m_in = dict(ada_w=m_ada_w, ada_b=m_ada_b, norm_g=m_norm_g, w_in=m_w_in, gla_wg_f=m_gla_wg_f, gla_bg_f=m_gla_bg_f,
                gla_wg_b=m_gla_wg_b, gla_bg_b=m_gla_bg_b, gla_norm_g=m_gla_norm_g, mla_q_norm_g=m_mla_q_norm_g,
                mla_kv_norm_g=m_mla_kv_norm_g, mla_w_uq=m_mla_w_uq, mla_w_ukv=m_mla_w_ukv, mla_out_g=m_mla_out_g,
                conv_w=m_conv_w, conv_out_g=m_conv_out_g, w_out=m_w_out, final_g=m_final_g)
    v_in = dict(ada_w=v_ada_w, ada_b=v_ada_b, norm_g=v_norm_g, w_in=v_w_in, gla_wg_f=v_gla_wg_f, gla_bg_f=v_gla_bg_f,
                gla_wg_b=v_gla_wg_b, gla_bg_b=v_gla_bg_b, gla_norm_g=v_gla_norm_g, mla_q_norm_g=v_mla_q_norm_g,
                mla_kv_norm_g=v_mla_kv_norm_g, mla_w_uq=v_mla_w_uq, mla_w_ukv=v_mla_w_ukv, mla_out_g=v_mla_out_g,
                conv_w=v_conv_w, conv_out_g=v_conv_out_g, w_out=v_w_out, final_g=v_final_g)

    g_c, g_wgf, g_wgb, g_cw = all_gather8([c, gla_wg_f, gla_wg_b, conv_w], "gather_small")

    def unshard_cols(g):
        g4 = g[0::2]
        return g4.transpose(1, 2, 0, 3).reshape(g4.shape[1], g4.shape[2], -1)

    small_full = dict(gla_wg_f=unshard_cols(g_wgf), gla_wg_b=unshard_cols(g_wgb), conv_w=unshard_cols(g_cw))
    for nme in SMALL_REPL:
        small_full[nme] = weights[nme]
    smalls = [{nme: small_full[nme][l] for nme in SMALL_REPL + SMALL_SHARDED} for l in range(DEPTH)]

    big_src = (jnp.swapaxes(w_in, 1, 2), w_out, mla_w_uq, mla_w_ukv)

    def my_halves(l, zero=0):
        out = []
        for t, a in enumerate(big_src):
            n_half = a.shape[1 + HALF_AXIS[t]] // 2
            out.append(lax.dynamic_slice_in_dim(a[l], ci * n_half + zero, n_half, axis=HALF_AXIS[t]).astype(BF16))
        return out

    def landing(blocks):
        return [lax.empty((2, N_CHIPS) + b.shape, b.dtype) for b in blocks]

    def finish_gather(handle, after, tag):
        blocks, lands = ici_wait(handle, after, "gather", "gather_wait" + tag)
        lands, done = gather_share(blocks, lands, "gather_share" + tag)
        full = [lax.dynamic_update_slice(g, b[None, None], (ci, chip) + (0,) * b.ndim) for g, b in zip(lands, blocks)]
        return full, done

    halves0 = my_halves(0)
    started0 = ici_start(halves0, landing(halves0), "gather", "gather_start0")

    c_act = _silu_rows(g_c[:, 0, :])
    c_act16 = jnp.concatenate([c_act, jnp.zeros_like(c_act)], axis=0)
    n_ada = ada_w.shape[2]
    parts = []
    for l in range(DEPTH):
        bias = lax.dynamic_slice_in_dim(ada_b[l], chip * n_ada, n_ada)[None]
        parts.append(_mm(c_act16, ada_w[l], bias=bias, name="ada_fwd"))
    g_mod, = all_gather8([jnp.stack(parts)], "gather_mod")
    mod_mine = lax.dynamic_index_in_dim(g_mod[0::2], dev, 2, keepdims=False)
    mods = mod_mine.transpose(1, 0, 2).reshape(DEPTH, 3 * d)

    inv_freq = ROPE_THETA ** (-jnp.arange(0, MLA_ROPE, 2, dtype=F32) / MLA_ROPE)
    ang = positions[0].astype(F32)[:, None] * inv_freq
    cos, sin = jnp.cos(ang), jnp.sin(ang)
    rope_k = (jnp.concatenate([cos, cos], axis=1), jnp.concatenate([-sin, sin], axis=1))
    rope_q = (jnp.tile(rope_k[0], (1, MLA_HEADS)), jnp.tile(rope_k[1], (1, MLA_HEADS)))

    def run_layer(xh, mod, gathered, small):
        big = {nme: jnp.concatenate([g[0], g[1]], axis=HALF_AXIS[t] + 1) for t, (nme, g) in enumerate(zip(BIG, gathered))}
        return _layer(xh, mod, big, small, rope_q, rope_k)

    def head(hh, fg):
        return loss_op(rmsnorm(hh, fg[None]), loss_target[0])[0, 0]

    gathered0, done0 = finish_gather(started0, mods, "0")
    halves1 = my_halves(1, done0[0, 0].astype(jnp.int32))
    started1 = ici_start(halves1, landing(halves1), "gather", "gather_start1")
    h1, vjp0 = jax.vjp(run_layer, x[0], mods[0] + started1["token"][0, 0], gathered0, smalls[0])
    gathered1, _ = finish_gather(started1, h1, "1")
    h2, vjp1 = jax.vjp(run_layer, h1, mods[1], gathered1, smalls[1])
    loss_dev, vjp_head = jax.vjp(head, h2, final_g)
    loss = lax.psum(loss_dev, ("x", "y", "c"))
    dh2, dfinal = vjp_head(jnp.ones((), F32))

    c_idx = jnp.reshape(ci, (1,)).astype(jnp.int32)
    chip_idx = jnp.reshape(chip, (1,)).astype(jnp.int32)

    def reduce_begin(dgath, tag, zero=None):
        theirs = sibling_send(dgath, "reduce_sibling" + tag)
        pair = [_add_pair(a, b, c_idx) for a, b in zip(dgath, theirs)]
        shapes = [(N_CHIPS - 1,) + p.shape[1:] for p in pair]
        if zero is None:
            lands = [lax.empty(shp, BF16) for shp in shapes]
        else:
            lands = [jnp.broadcast_to(zero.astype(BF16), shp) for shp in shapes]
        return ici_start(pair, lands, "reduce", "reduce_start" + tag)

    def reduce_end(handle, after, tag):
        pair, landed = ici_wait(handle, after, "reduce", "reduce_wait" + tag)
        reduced = [_add_chips(p, q, chip_idx) for p, q in zip(pair, landed)]
        others = sibling_swap(reduced, "share_sibling" + tag)
        return [jnp.where(ci == 0, jnp.concatenate([own, other], axis=HALF_AXIS[t]),
                          jnp.concatenate([other, own], axis=HALF_AXIS[t]))
                for t, (own, other) in enumerate(zip(reduced, others))]

    dh1, dmod1, dgath1, dsmall1 = vjp1(dh2)
    reducing1 = reduce_begin(dgath1, "1")
    dx, dmod0, dgath0, dsmall0 = vjp0(dh1 + reducing1["token"][0, 0])
    big_grads1 = reduce_end(reducing1, dx, "1")
    dmods = jnp.stack([dmod0, dmod1])
    dsmalls = [dsmall0, dsmall1]

    pieces = [dmods.reshape(-1), dfinal]
    for nme in SMALL_REPL + SMALL_SHARDED:
        pieces.append(jnp.stack([dsmalls[l][nme] for l in range(DEPTH)]).reshape(-1))
    sizes = [p.shape[0] for p in pieces]
    flat = jnp.concatenate(pieces)
    padn = (-flat.shape[0]) % 128
    flat = jnp.concatenate([flat, jnp.zeros((padn,), F32)])[None]
    g_small, = all_gather8([flat], "gather_small_grads")
    total, small_done = _sum_devices(g_small)
    total = total[0]
    reducing0 = reduce_begin(dgath0, "0", small_done[0, 0])
    offs, at = [], 0
    for n_el in sizes:
        offs.append(at)
        at += n_el

    def piece(i, shape):
        return total[offs[i]:offs[i] + sizes[i]].reshape(shape)

    grads = {"ada_b": piece(0, (DEPTH, 3 * d)), "final_g": piece(1, (d,))}
    for i, nme in enumerate(SMALL_REPL + SMALL_SHARDED):
        full = piece(2 + i, small_full[nme].shape)
        if nme in SMALL_SHARDED:
            ncol = weights[nme].shape[2]
            full = lax.dynamic_slice_in_dim(full, chip * ncol, ncol, axis=2)
        grads[nme] = full

    dmod_all = g_small[:, 0, :DEPTH * 3 * d].reshape(N_DEV, DEPTH, 3 * d)
    dmod_cols = lax.dynamic_slice_in_dim(dmod_all, chip * n_ada, n_ada, axis=2)
    g_ada = []
    for l in range(DEPTH):
        dm16 = jnp.concatenate([dmod_cols[:, l], jnp.zeros((N_DEV, n_ada), F32)], axis=0)
        g_ada.append(_mm(c_act16, dm16, ta=True, name="ada_bwd"))
    grads["ada_w"] = jnp.stack(g_ada)

    order = list(weights)
    big_names = ("w_in", "w_out", "mla_w_uq", "mla_w_ukv")
    delta, new_m, new_v = {}, {}, {}
    for nme in order:
        if nme not in big_names:
            delta[nme], new_m[nme], new_v[nme] = _adamw(weights[nme], grads[nme], m_in[nme], v_in[nme])

    def oriented(nme, t):
        return jnp.swapaxes(t, 1, 2) if nme == "w_in" else t

    state = {nme: [oriented(nme, t) for t in (weights[nme], m_in[nme], v_in[nme])] for nme in big_names}
    last = DEPTH - 1
    part = {nme: _adamw_layer(state[nme][0], g, state[nme][1], state[nme][2], last)
            for nme, g in zip(big_names, big_grads1)}
    done = [delta[nme].reshape(-1)[:1] for nme in order if nme not in big_names]
    done += [part[nme][0][last].reshape(-1)[:1] for nme in big_names]
    big_grads0 = reduce_end(reducing0, jnp.concatenate(done), "0")
    for nme, g0, g1 in zip(big_names, big_grads0, big_grads1):
        res = _adamw_layer(state[nme][0], g0, state[nme][1], state[nme][2], 0, prev=part[nme])
        delta[nme], new_m[nme], new_v[nme] = (oriented(nme, t) for t in res)
        grads[nme] = oriented(nme, jnp.stack([g0, g1]))
    return (loss, dx[None], *[grads[n_] for n_ in order], *[delta[n_] for n_ in order],
            *[new_m[n_] for n_ in order], *[new_v[n_] for n_ in order])
```

```python
import functools

import jax
import jax.numpy as jnp
from jax import lax
from jax.experimental import pallas as pl
from jax.experimental.pallas import tpu as pltpu

F32 = jnp.float32
BF16 = jnp.bfloat16
MESH = pl.DeviceIdType.MESH
HIGHEST = lax.Precision.HIGHEST

DEPTH = 2
D_MODEL = 2048
GLA_HEADS = 6
GLA_DK = 64
GLA_DV = 128
GLA_RANK = 16
GLA_TEMP = 16.0
GLA_CHUNK = 64
GLA_W = GLA_HEADS * GLA_DV
MLA_HEADS = 6
MLA_QL = 384
MLA_KVL = 256
MLA_NOPE = 128
MLA_ROPE = 64
MLA_DV = 128
MLA_W = MLA_HEADS * MLA_DV
CONV_CH = D_MODEL - GLA_W - MLA_W
ROPE_THETA = 10000.0
EPS = 1e-6
IN_DIM = 5856
N_CHIPS = 4
N_DEV = 8

ADAM_LR = 0.001
ADAM_B1 = 0.9
ADAM_B2 = 0.999
ADAM_EPS = 1e-08
ADAM_WD = 0.01
ADAM_STEP = 10

PROJ_SEGS = (
    ("gq", 0, 384, 384), ("gk", 384, 384, 384), ("gv", 768, 768, 768), ("glr", 1536, 32, 128),
    ("mq", 1568, 384, 384), ("mkv", 1952, 256, 256), ("mkr", 2208, 64, 128),
    ("cb", 2272, 512, 512), ("cc", 2784, 512, 512), ("cx", 3296, 512, 512),
    ("pad", 3808, 0, 128), ("z", 3808, 2048, 2048),
)
PROJ_AL = sum(s[3] for s in PROJ_SEGS)

ANY = pl.BlockSpec(memory_space=pl.ANY)
VMEM_LIMIT = 48 * 1024 * 1024
BLOCK_BYTES = 2 * 1024 * 1024


def _params(sem=None):
    return pltpu.CompilerParams(dimension_semantics=sem, vmem_limit_bytes=VMEM_LIMIT)


def _dot(a, b, ca, cb, precision=None):
    return lax.dot_general(a, b, (((ca,), (cb,)), ((), ())), preferred_element_type=F32, precision=precision)


def _tile(dim, prefs):
    for t in prefs:
        if dim % t == 0:
            return t
    return dim


def _pick_rows(rows, width, itemsize=4):
    for t in (2048, 1024, 512, 256, 128, 64, 32, 16, 8):
        if rows % t == 0 and t * width * itemsize <= BLOCK_BYTES:
            return t
    return rows


def _mm(a, b, *, ta=False, tb=False, bias=None, out_dtype=F32, name="mm"):
    if ta:
        K, M = a.shape
    else:
        M, K = a.shape
    if tb:
        N, Kb = b.shape
    else:
        Kb, N = b.shape
    assert K == Kb, (a.shape, b.shape, ta, tb)
    tm = _tile(M, (512, 256, 128))
    tn = _tile(N, (1024, 512, 384, 256, 128))
    tk = _tile(K, (2048, 1024, 512, 256, 128))
    nk = K // tk
    has_bias = bias is not None

    def body(*refs):
        a_ref, b_ref = refs[0], refs[1]
        bias_ref = refs[2] if has_bias else None
        o_ref = refs[3 if has_bias else 2]
        part = _dot(a_ref[...].astype(BF16), b_ref[...].astype(BF16), 0 if ta else 1, 1 if tb else 0)

        def finish(r):
            if has_bias:
                r = r + bias_ref[...]
            o_ref[...] = r.astype(out_dtype)

        if nk == 1:
            finish(part)
            return
        acc_ref = refs[-1]
        k = pl.program_id(2)

        @pl.when(k == 0)
        def _():
            acc_ref[...] = part

        @pl.when(k != 0)
        def _():
            acc_ref[...] += part

        @pl.when(k == nk - 1)
        def _():
            finish(acc_ref[...])

    a_spec = pl.BlockSpec((tk, tm), lambda i, j, k: (k, i)) if ta else pl.BlockSpec((tm, tk), lambda i, j, k: (i, k))
    b_spec = pl.BlockSpec((tn, tk), lambda i, j, k: (j, k)) if tb else pl.BlockSpec((tk, tn), lambda i, j, k: (k, j))
    in_specs = [a_spec, b_spec]
    args = [a, b]
    if has_bias:
        in_specs.append(pl.BlockSpec((1, tn), lambda i, j, k: (0, j)))
        args.append(bias)
    return pl.pallas_call(
        body, name=name, grid=(M // tm, N // tn, nk),
        in_specs=in_specs, out_specs=pl.BlockSpec((tm, tn), lambda i, j, k: (i, j)),
        out_shape=jax.ShapeDtypeStruct((M, N), out_dtype),
        scratch_shapes=[pltpu.VMEM((tm, tn), F32)] if nk > 1 else [],
        compiler_params=_params(("parallel", "parallel", "arbitrary")),
    )(*args)


@jax.custom_vjp
def mm(a, b):
    return _mm(a, b, name="mm_fwd")


def _mm_f(a, b):
    return _mm(a, b, name="mm_fwd"), (a, b)


def _mm_b(res, g):
    a, b = res
    return _mm(g, b, tb=True, out_dtype=a.dtype, name="mm_da"), _mm(a, g, ta=True, out_dtype=b.dtype, name="mm_db")


mm.defvjp(_mm_f, _mm_b)


@jax.custom_vjp
def mm16(a, b):
    return _mm(a, b, out_dtype=BF16, name="mm16_fwd")


def _mm16_f(a, b):
    return mm16(a, b), (a, b)


mm16.defvjp(_mm16_f, _mm_b)


def _rows(body, name, tiled, full, tiled_out, acc_out, tr=None):
    rows = tiled[0].shape[0]
    if tr is None:
        width = max([a.shape[1] for a in tiled] + [s.shape[1] for s in tiled_out])
        tr = _pick_rows(rows, width)
    in_specs = [pl.BlockSpec((tr, a.shape[1]), lambda i: (i, 0)) for a in tiled]
    in_specs += [pl.BlockSpec(a.shape, lambda i: (0, 0)) for a in full]
    out_specs = [pl.BlockSpec((tr, s.shape[1]), lambda i: (i, 0)) for s in tiled_out]
    out_specs += [pl.BlockSpec(s.shape, lambda i: (0, 0)) for s in acc_out]

    def wrapped(*refs):
        body(pl.program_id(0), *refs)

    outs = pl.pallas_call(
        wrapped, name=name, grid=(rows // tr,), in_specs=in_specs, out_specs=out_specs,
        out_shape=list(tiled_out) + list(acc_out),
        compiler_params=_params(("arbitrary",)),
    )(*tiled, *full)
    return outs


def _sds(shape, dtype=F32):
    return jax.ShapeDtypeStruct(tuple(shape), dtype)


def _acc(step, ref, val):
    @pl.when(step == 0)
    def _():
        ref[...] = val

    @pl.when(step != 0)
    def _():
        ref[...] += val


def _colsum(v):
    return jnp.sum(v, axis=0, keepdims=True)


def _rstd(x):
    return lax.rsqrt(jnp.mean(x * x, axis=-1, keepdims=True) + EPS)


def _norm_grid(x, g):
    rows, w = x.shape[0], g.shape[1]
    tr = _pick_rows(rows, w)
    blk = pl.BlockSpec((tr, w), lambda i, j: (i, j))
    gblk = pl.BlockSpec((1, w), lambda i, j: (0, 0))
    return (rows // tr, x.shape[1] // w), blk, gblk


@jax.custom_vjp
def rmsnorm(x, g):
    def body(x_ref, g_ref, o_ref):
        x = x_ref[...]
        o_ref[...] = x * _rstd(x) * g_ref[...]

    grid, blk, gblk = _norm_grid(x, g)
    return pl.pallas_call(body, name="rmsnorm_fwd", grid=grid, in_specs=[blk, gblk], out_specs=blk,
                          out_shape=_sds(x.shape), compiler_params=_params(("parallel", "parallel")))(x, g)


def _rmsnorm_f(x, g):
    return rmsnorm(x, g), (x, g)


def _rmsnorm_b(res, dy):
    x, g = res

    def body(x_ref, dy_ref, g_ref, dx_ref, dg_ref):
        x = x_ref[...]
        dy = dy_ref[...]
        r = _rstd(x)
        xh = x * r
        dxh = dy * g_ref[...]
        dx_ref[...] = r * (dxh - xh * jnp.mean(dxh * xh, axis=-1, keepdims=True))
        first = jnp.logical_and(pl.program_id(0) == 0, pl.program_id(1) == 0)
        _acc(jnp.where(first, 0, 1), dg_ref, _colsum(dy * xh))

    grid, blk, gblk = _norm_grid(x, g)
    dx, dg = pl.pallas_call(body, name="rmsnorm_bwd", grid=grid, in_specs=[blk, blk, gblk], out_specs=[blk, gblk],
                            out_shape=[_sds(x.shape), _sds(g.shape)],
                            compiler_params=_params(("arbitrary", "arbitrary")))(x, dy, g)
    return dx, dg


rmsnorm.defvjp(_rmsnorm_f, _rmsnorm_b)


def _modulate(x, g, scale, shift):
    def body(i, x_ref, g_ref, sc_ref, sh_ref, o_ref):
        x = x_ref[...]
        xn = x * _rstd(x) * g_ref[...]
        o_ref[...] = (xn * (1.0 + sc_ref[...]) + sh_ref[...]).astype(BF16)
    return _rows(body, "modulate_fwd", [x], [g, scale, shift], [_sds(x.shape, BF16)], [])[0]


def _modulate_bwd(x, g, scale, shift, dh):
    def body(i, x_ref, dh_ref, g_ref, sc_ref, dx_ref, dg_ref, dsc_ref, dsh_ref):
        x = x_ref[...]
        dh = dh_ref[...]
        gv = g_ref[...]
        r = _rstd(x)
        xh = x * r
        dxn = dh * (1.0 + sc_ref[...])
        dxh = dxn * gv
        dx_ref[...] = r * (dxh - xh * jnp.mean(dxh * xh, axis=-1, keepdims=True))
        _acc(i, dg_ref, _colsum(dxn * xh))
        _acc(i, dsc_ref, _colsum(dh * (xh * gv)))
        _acc(i, dsh_ref, _colsum(dh))

    v = _sds(g.shape)
    return _rows(body, "modulate_bwd", [x, dh], [g, scale], [_sds(x.shape)], [v, v, v])


@jax.custom_vjp
def mod_mm(x, g, scale, shift, wt):
    return _mm(_modulate(x, g, scale, shift), wt, tb=True, name="mm_in")


def _mod_mm_f(x, g, scale, shift, wt):
    h = _modulate(x, g, scale, shift)
    return _mm(h, wt, tb=True, name="mm_in"), (x, g, scale, shift, wt, h)


def _mod_mm_b(res, dproj):
    x, g, scale, shift, wt, h = res
    dproj = dproj.astype(BF16)
    dh = _mm(dproj, wt, name="mm_in_dh")
    dwt = _mm(dproj, h, ta=True, out_dtype=wt.dtype, name="mm_in_dw")
    dx, dg, dsc, dsh = _modulate_bwd(x, g, scale, shift, dh)
    return dx, dg, dsc, dsh, dwt


mod_mm.defvjp(_mod_mm_f, _mod_mm_b)


def _sigmoid(z):
    return 1.0 / (1.0 + jnp.exp(-z))


def _gate_mul(o, z):
    def body(i, o_ref, z_ref, y_ref):
        z = z_ref[...]
        y_ref[...] = (o_ref[...] * (z * _sigmoid(z))).astype(BF16)
    return _rows(body, "gate_mul_fwd", [o, z], [], [_sds(o.shape, BF16)], [])[0]


def _gate_mul_bwd(o, z, dy):
    def body(i, o_ref, z_ref, dy_ref, do_ref, dz_ref):
        z = z_ref[...]
        dy = dy_ref[...]
        s = _sigmoid(z)
        do_ref[...] = dy * (z * s)
        dz_ref[...] = dy * o_ref[...] * (s * (1.0 + z * (1.0 - s)))
    return _rows(body, "gate_mul_bwd", [o, z, dy], [], [_sds(o.shape), _sds(o.shape)], [])


def _residual(x, u, gate):
    def body(i, x_ref, u_ref, g_ref, o_ref):
        o_ref[...] = x_ref[...] + g_ref[...] * u_ref[...]
    return _rows(body, "residual_fwd", [x, u], [gate], [_sds(x.shape)], [])[0]


def _residual_bwd(d, u, gate):
    def body(i, d_ref, u_ref, g_ref, du_ref, dg_ref):
        d = d_ref[...]
        du_ref[...] = (g_ref[...] * d).astype(BF16)
        _acc(i, dg_ref, _colsum(d * u_ref[...]))

    return _rows(body, "residual_bwd", [d, u], [gate], [_sds(u.shape, BF16)], [_sds(gate.shape)])


@jax.custom_vjp
def out_block(o, z, w, x, gate):
    return _residual(x, _mm(_gate_mul(o, z), w, name="mm_out"), gate)


def _out_block_f(o, z, w, x, gate):
    y = _gate_mul(o, z)
    u = _mm(y, w, name="mm_out")
    return _residual(x, u, gate), (o, z, w, y, u, gate)


def _out_block_b(res, d):
    o, z, w, y, u, gate = res
    du, dgate = _residual_bwd(d, u, gate)
    dy = _mm(du, w, tb=True, name="mm_out_dy")
    dw = _mm(y, du, ta=True, out_dtype=w.dtype, name="mm_out_dw")
    do, dz = _gate_mul_bwd(o, z, dy)
    return do, dz, dw, d, dgate


out_block.defvjp(_out_block_f, _out_block_b)


@jax.custom_vjp
def gate_act(u, b):
    def body(i, u_ref, b_ref, o_ref):
        t = u_ref[...] + b_ref[...]
        o_ref[...] = (jnp.minimum(t, 0.0) - jnp.log(1.0 + jnp.exp(-jnp.abs(t)))) / GLA_TEMP
    return _rows(body, "gate_act_fwd", [u], [b], [_sds(u.shape)], [])[0]


def _gate_act_f(u, b):
    return gate_act(u, b), (u, b)


def _gate_act_b(res, d):
    u, b = res

    def body(i, u_ref, d_ref, b_ref, du_ref, db_ref):
        t = u_ref[...] + b_ref[...]
        du = d_ref[...] * _sigmoid(-t) / GLA_TEMP
        du_ref[...] = du
        _acc(i, db_ref, _colsum(du))

    du, db = _rows(body, "gate_act_bwd", [u, d], [b], [_sds(u.shape)], [_sds(b.shape)])
    return du, db


gate_act.defvjp(_gate_act_f, _gate_act_b)


@jax.custom_vjp
def fma(a, b, c, d):
    def body(i, a_ref, b_ref, c_ref, d_ref, o_ref):
        o_ref[...] = a_ref[...] * b_ref[...] + c_ref[...] * d_ref[...]
    return _rows(body, "fma_fwd", [a, b, c, d], [], [_sds(a.shape)], [])[0]


def _fma_f(a, b, c, d):
    return fma(a, b, c, d), (b, d)


def _fma_b(res, g):
    b, d = res

    def body(i, g_ref, b_ref, d_ref, da_ref, dc_ref):
        g = g_ref[...]
        da_ref[...] = g * b_ref[...]
        dc_ref[...] = g * d_ref[...]

    da, dc = _rows(body, "fma_bwd", [g, b, d], [], [_sds(g.shape), _sds(g.shape)], [])
    return da, jnp.zeros_like(b), dc, jnp.zeros_like(d)


fma.defvjp(_fma_f, _fma_b)


def _silu_rows(c):
    def body(i, c_ref, o_ref):
        v = c_ref[...]
        o_ref[...] = v * _sigmoid(v)
    return _rows(body, "silu", [c], [], [_sds(c.shape)], [])[0]


@jax.custom_vjp
def loss_op(y, t):
    return _loss_fwd(y, t)[0]


def _loss_fwd(y, t):
    inv = 1.0 / y.shape[1]

    def body(i, y_ref, t_ref, d_ref, l_ref):
        e = y_ref[...] - t_ref[...]
        d_ref[...] = e * inv
        _acc(i, l_ref, jnp.sum(_colsum(e * e), axis=1, keepdims=True) * (0.5 * inv))

    d, l = _rows(body, "loss_fwd", [y, t], [], [_sds(y.shape)], [_sds((1, 1))])
    return l, d


def _loss_f(y, t):
    l, d = _loss_fwd(y, t)
    return l, d


def _loss_b(d, g):
    return d * g, jnp.zeros_like(d)


loss_op.defvjp(_loss_f, _loss_b)


def _conv_terms(cc, cx, rows, n):
    u = cc * cx
    up = jnp.where(rows == 0, 0.0, pltpu.roll(u, 1, 0))
    un = jnp.where(rows == n - 1, 0.0, pltpu.roll(u, n - 1, 0))
    return u, up, un


CONV_COLS = 128


def _conv_specs(s, n_in):
    blk = pl.BlockSpec((s, CONV_COLS), lambda j: (0, j))
    wblk = pl.BlockSpec((8, CONV_COLS), lambda j: (0, j))
    return [blk] * n_in + [wblk], blk, wblk


@jax.custom_vjp
def conv_op(cb, cc, cx, w):
    s, ch = cb.shape

    def body(cb_ref, cc_ref, cx_ref, w_ref, o_ref):
        rows = lax.broadcasted_iota(jnp.int32, (s, CONV_COLS), 0)
        u, up, un = _conv_terms(cc_ref[...], cx_ref[...], rows, s)
        conv = up * w_ref[0:1, :] + u * w_ref[1:2, :] + un * w_ref[2:3, :]
        o_ref[...] = cb_ref[...] * conv

    in_specs, blk, _ = _conv_specs(s, 3)
    return pl.pallas_call(
        body, name="conv_fwd", grid=(ch // CONV_COLS,), in_specs=in_specs, out_specs=blk,
        out_shape=_sds(cb.shape), compiler_params=_params(("parallel",)),
    )(cb, cc, cx, w)


def _conv_f(cb, cc, cx, w):
    return conv_op(cb, cc, cx, w), (cb, cc, cx, w)


def _conv_b(res, d):
    cb, cc, cx, w = res
    s, ch = cb.shape

    def body(cb_ref, cc_ref, cx_ref, d_ref, w_ref, dcb_ref, dcc_ref, dcx_ref, dw_ref):
        rows = lax.broadcasted_iota(jnp.int32, (s, CONV_COLS), 0)
        cc_v = cc_ref[...]
        cx_v = cx_ref[...]
        u, up, un = _conv_terms(cc_v, cx_v, rows, s)
        w0, w1, w2 = w_ref[0:1, :], w_ref[1:2, :], w_ref[2:3, :]
        dv = d_ref[...]
        dcb_ref[...] = dv * (up * w0 + u * w1 + un * w2)
        dconv = dv * cb_ref[...]
        d_next = jnp.where(rows == s - 1, 0.0, pltpu.roll(dconv, s - 1, 0))
        d_prev = jnp.where(rows == 0, 0.0, pltpu.roll(dconv, 1, 0))
        du = w0 * d_next + w1 * dconv + w2 * d_prev
        dcc_ref[...] = du * cx_v
        dcx_ref[...] = du * cc_v
        dw_ref[...] = jnp.zeros_like(dw_ref)
        dw_ref[0:1, :] = _colsum(dconv * up)
        dw_ref[1:2, :] = _colsum(dconv * u)
        dw_ref[2:3, :] = _colsum(dconv * un)

    in_specs, blk, wblk = _conv_specs(s, 4)
    v = _sds(cb.shape)
    return tuple(pl.pallas_call(
        body, name="conv_bwd", grid=(ch // CONV_COLS,), in_specs=in_specs, out_specs=[blk, blk, blk, wblk],
        out_shape=[v, v, v, _sds(w.shape)], compiler_params=_params(("parallel",)),
    )(cb, cc, cx, d, w))


conv_op.defvjp(_conv_f, _conv_b)


def _gla_masks(rev):
    c = GLA_CHUNK
    row = lax.broadcasted_iota(jnp.int32, (c, c), 0)
    col = lax.broadcasted_iota(jnp.int32, (c, c), 1)
    mask = (row < col) if rev else (row >= col)
    return rev, mask


def _chunk_cumsum(g, rev):
    c = g.shape[0]
    row = lax.broadcasted_iota(jnp.int32, g.shape, 0)
    b = g
    s = 1
    while s < c:
        if rev:
            b = b + jnp.where(row < c - s, pltpu.roll(b, c - s, 0), 0.0)
        else:
            b = b + jnp.where(row >= s, pltpu.roll(b, s, 0), 0.0)
        s *= 2
    return b


GLA_UNROLL = 4


def _gla_rows(n):
    return pl.ds(pl.multiple_of(n * GLA_CHUNK, GLA_CHUNK), GLA_CHUNK)


def _gla_scan(s_ref, bt_ref, st_ref, n_chunks, descending):
    st_ref[...] = jnp.zeros_like(st_ref)

    def step(i, carry):
        n = (n_chunks - 1 - i) if descending else i
        own = s_ref[n]
        st = st_ref[...]
        s_ref[n] = st
        st_ref[...] = st * jnp.exp(bt_ref[n]) + own
        return carry

    lax.fori_loop(0, n_chunks, step, 0)


GLA_PAIR = 2


def _gla_specs(s):
    dk, dv = GLA_DK, GLA_DV
    n_pairs = GLA_HEADS // GLA_PAIR
    blk_k = pl.BlockSpec((s, GLA_PAIR * dk), lambda p: (0, p))
    blk_gb = pl.BlockSpec((s, GLA_PAIR * dk), lambda p: (0, n_pairs + p))
    blk_v = pl.BlockSpec((s, GLA_PAIR * dv), lambda p: (0, p))
    return n_pairs, blk_k, blk_gb, blk_v


def _head_lanes(hh):
    lane = lax.broadcasted_iota(jnp.int32, (1, GLA_PAIR * GLA_DK), 1)
    return jnp.logical_and(lane >= hh * GLA_DK, lane < (hh + 1) * GLA_DK)


def _gla_fwd(q, k, v, la):
    s = q.shape[0]
    dk, dv = GLA_DK, GLA_DV
    pw = GLA_PAIR * dk
    n_chunks = s // GLA_CHUNK
    scale = GLA_DK ** -0.5

    def body(q_ref, k_ref, v_ref, gf_ref, gb_ref, o_ref, sf_ref, sb_ref, bf_ref, bb_ref, btf_ref, btb_ref, st_ref):
        masks = [_gla_masks(rev) for rev in (False, True)]
        dirs = ((False, gf_ref, sf_ref, bf_ref, btf_ref), (True, gb_ref, sb_ref, bb_ref, btb_ref))

        def decays(n, carry):
            rows = _gla_rows(n)
            for rev, g_ref, _, b_ref, bt_ref in dirs:
                g = g_ref[rows, :]
                b_ref[rows, :] = _chunk_cumsum(g, rev)
                bt_ref[n] = _colsum(g)
            return carry

        lax.fori_loop(0, n_chunks, decays, 0, unroll=GLA_UNROLL)
        for hh in range(GLA_PAIR):
            m = _head_lanes(hh)
            vl = slice(hh * dv, (hh + 1) * dv)

            def prepare(n, carry, m=m, vl=vl):
                rows = _gla_rows(n)
                kk = k_ref[rows, :]
                vb = v_ref[rows, vl].astype(BF16)
                for rev, _, s_ref, b_ref, bt_ref in dirs:
                    ke = jnp.where(m, kk * jnp.exp(bt_ref[n] - b_ref[rows, :]), 0.0).astype(BF16)
                    s_ref[n] = _dot(vb, ke, 0, 0)
                return carry

            lax.fori_loop(0, n_chunks, prepare, 0, unroll=GLA_UNROLL)
            for rev, _, s_ref, _, bt_ref in dirs:
                _gla_scan(s_ref, bt_ref, st_ref, n_chunks, descending=rev)

            def emit(n, carry, m=m, vl=vl):
                rows = _gla_rows(n)
                qs = q_ref[rows, :] * scale
                kk = k_ref[rows, :]
                vb = v_ref[rows, vl].astype(BF16)
                o = None
                for (rev, _, s_ref, b_ref, _), (_, mask) in zip(dirs, masks):
                    b = b_ref[rows, :]
                    qd = jnp.where(m, qs * jnp.exp(b), 0.0).astype(BF16)
                    ki = jnp.where(m, kk * jnp.exp(-b), 0.0).astype(BF16)
                    a = jnp.where(mask, _dot(qd, ki, 1, 1), 0.0).astype(BF16)
                    od = _dot(a, vb, 1, 0) + _dot(qd, s_ref[n].astype(BF16), 1, 1)
                    o = od if o is None else o + od
                o_ref[rows, vl] = o
                return carry

            lax.fori_loop(0, n_chunks, emit, 0, unroll=GLA_UNROLL)

    n_pairs, blk_k, blk_gb, blk_v = _gla_specs(s)
    state = pltpu.VMEM((n_chunks, dv, pw), F32)
    scratch = [state, state, pltpu.VMEM((s, pw), F32), pltpu.VMEM((s, pw), F32), pltpu.VMEM((n_chunks, 1, pw), F32),
               pltpu.VMEM((n_chunks, 1, pw), F32), pltpu.VMEM((dv, pw), F32)]
    return pl.pallas_call(
        body, name="gla_fwd", grid=(n_pairs,), in_specs=[blk_k, blk_k, blk_v, blk_k, blk_gb],
        out_specs=blk_v, out_shape=_sds(v.shape), scratch_shapes=scratch,
        compiler_params=_params(("parallel",)),
    )(q, k, v, la, la)


def _gla_bwd(q, k, v, la, do):
    s = q.shape[0]
    dk, dv = GLA_DK, GLA_DV
    pw = GLA_PAIR * dk
    c = GLA_CHUNK
    n_chunks = s // c
    scale = GLA_DK ** -0.5

    def body(q_ref, k_ref, v_ref, gf_ref, gb_ref, do_ref, dq_ref, dk_ref, dv_ref, dgf_ref, dgb_ref,
             sf_ref, sb_ref, bf_ref, bb_ref, btf_ref, btb_ref, dsf_ref, dsb_ref, st_ref):
        masks = [_gla_masks(rev) for rev in (False, True)]
        rowc = lax.broadcasted_iota(jnp.int32, (c, pw), 0)
        dirs = ((False, gf_ref, sf_ref, bf_ref, btf_ref, dsf_ref, dgf_ref),
                (True, gb_ref, sb_ref, bb_ref, btb_ref, dsb_ref, dgb_ref))

        def decays(n, carry):
            rows = _gla_rows(n)
            for rev, g_ref, _, b_ref, bt_ref, _, _ in dirs:
                g = g_ref[rows, :]
                b_ref[rows, :] = _chunk_cumsum(g, rev)
                bt_ref[n] = _colsum(g)
            return carry

        lax.fori_loop(0, n_chunks, decays, 0, unroll=GLA_UNROLL)
        for hh in range(GLA_PAIR):
            m = _head_lanes(hh)
            vl = slice(hh * dv, (hh + 1) * dv)

            def prepare(n, carry, m=m, vl=vl):
                rows = _gla_rows(n)
                qs = q_ref[rows, :] * scale
                kk = k_ref[rows, :]
                vb = v_ref[rows, vl].astype(BF16)
                do_b = do_ref[rows, vl].astype(BF16)
                for rev, _, s_ref, b_ref, bt_ref, ds_ref, _ in dirs:
                    b = b_ref[rows, :]
                    ke = jnp.where(m, kk * jnp.exp(bt_ref[n] - b), 0.0).astype(BF16)
                    qd = jnp.where(m, qs * jnp.exp(b), 0.0).astype(BF16)
                    s_ref[n] = _dot(vb, ke, 0, 0)
                    ds_ref[n] = _dot(do_b, qd, 0, 0)
                return carry

            lax.fori_loop(0, n_chunks, prepare, 0, unroll=GLA_UNROLL)
            for rev, _, s_ref, _, bt_ref, ds_ref, _ in dirs:
                _gla_scan(s_ref, bt_ref, st_ref, n_chunks, descending=rev)
                _gla_scan(ds_ref, bt_ref, st_ref, n_chunks, descending=not rev)

            def emit(n, carry, m=m, vl=vl, first=(hh == 0)):
                rows = _gla_rows(n)
                qs = q_ref[rows, :] * scale
                kk = k_ref[rows, :]
                vb = v_ref[rows, vl].astype(BF16)
                do_b = do_ref[rows, vl].astype(BF16)
                dq = dkk = dvv = None
                for (rev, _, s_ref, b_ref, bt_ref, ds_ref, dg_ref), (_, mask) in zip(dirs, masks):
                    b = b_ref[rows, :]
                    bt = bt_ref[n]
                    eb = jnp.where(m, jnp.exp(b), 0.0)
                    enb = jnp.where(m, jnp.exp(-b), 0.0)
                    etb = jnp.where(m, jnp.exp(bt - b), 0.0)
                    ebt = jnp.exp(bt)
                    qd = qs * eb
                    ki = kk * enb
                    ke = kk * etb
                    qd_b, ki_b, ke_b = qd.astype(BF16), ki.astype(BF16), ke.astype(BF16)
                    st = s_ref[n]
                    dst = ds_ref[n]
                    dst_b = dst.astype(BF16)
                    a = jnp.where(mask, _dot(qd_b, ki_b, 1, 1), 0.0).astype(BF16)
                    da = jnp.where(mask, _dot(do_b, vb, 1, 1), 0.0).astype(BF16)
                    dv_d = _dot(a, do_b, 0, 0) + _dot(ke_b, dst_b, 1, 1)
                    dqd = _dot(da, ki_b, 1, 0) + _dot(do_b, st.astype(BF16), 1, 0)
                    dki = _dot(da, qd_b, 0, 0)
                    dke = _dot(vb, dst_b, 1, 0)
                    dbt = _colsum(st * dst) * ebt + _colsum(dke * ke)
                    db = dqd * qd - dki * ki - dke * ke
                    db = db + jnp.where(rowc == (0 if rev else c - 1), dbt, 0.0)
                    dg = _chunk_cumsum(db, not rev)
                    if first:
                        dg_ref[rows, :] = dg
                    else:
                        dg_ref[rows, :] += dg
                    dq_d = dqd * eb * scale
                    dk_d = dki * enb + dke * etb
                    dq = dq_d if dq is None else dq + dq_d
                    dkk = dk_d if dkk is None else dkk + dk_d
                    dvv = dv_d if dvv is None else dvv + dv_d
                if first:
                    dq_ref[rows, :] = dq
                    dk_ref[rows, :] = dkk
                else:
                    dq_ref[rows, :] += dq
                    dk_ref[rows, :] += dkk
                dv_ref[rows, vl] = dvv
                return carry

            lax.fori_loop(0, n_chunks, emit, 0, unroll=2)

    n_pairs, blk_k, blk_gb, blk_v = _gla_specs(s)
    vk, vv = _sds(q.shape), _sds(v.shape)
    state = pltpu.VMEM((n_chunks, dv, pw), F32)
    scratch = [state, state, pltpu.VMEM((s, pw), F32), pltpu.VMEM((s, pw), F32), pltpu.VMEM((n_chunks, 1, pw), F32),
               pltpu.VMEM((n_chunks, 1, pw), F32), state, state, pltpu.VMEM((dv, pw), F32)]
    return pl.pallas_call(
        body, name="gla_bwd", grid=(n_pairs,), in_specs=[blk_k, blk_k, blk_v, blk_k, blk_gb, blk_v],
        out_specs=[blk_k, blk_k, blk_v, blk_k, blk_k], out_shape=[vk, vk, vv, vk, vk],
        scratch_shapes=scratch, compiler_params=_params(("parallel",)),
    )(q, k, v, la, la, do)


@jax.custom_vjp
def gla(q, k, v, la):
    return _gla_fwd(q, k, v, la)


def _gla_f(q, k, v, la):
    return _gla_fwd(q, k, v, la), (q, k, v, la)


def _gla_b(res, do):
    dq, dk, dv, dgf, dgb = _gla_bwd(*res, do)
    return dq, dk, dv, jnp.concatenate([dgf, dgb], axis=1)


gla.defvjp(_gla_f, _gla_b)


ATTN_TQ = 256
HEAD_LANES = 128


def _attn_blocks(s, tq):
    per_q = pl.BlockSpec((tq, HEAD_LANES), lambda h, j: (j, h))
    k_nope = pl.BlockSpec((s, HEAD_LANES), lambda h, j: (0, 2 * h))
    v_blk = pl.BlockSpec((s, HEAD_LANES), lambda h, j: (0, 2 * h + 1))
    k_rope = pl.BlockSpec((s, HEAD_LANES), lambda h, j: (0, 0))
    lse = pl.BlockSpec((1, tq, 1), lambda h, j: (h, j, 0))
    return per_q, k_nope, v_blk, k_rope, lse


def _attn_fwd(qn, qr, kv, kr):
    s = qn.shape[0]
    tq = min(ATTN_TQ, s)
    scale = (MLA_NOPE + MLA_ROPE) ** -0.5

    def body(qn_ref, qr_ref, kn_ref, v_ref, kr_ref, o_ref, lse_ref):
        q = jnp.concatenate([qn_ref[...], qr_ref[...]], axis=1)
        k = jnp.concatenate([kn_ref[...], kr_ref[...]], axis=1)
        sc = _dot(q, k, 1, 1) * scale
        m = jnp.max(sc, axis=-1, keepdims=True)
        p = jnp.exp(sc - m)
        l = jnp.sum(p, axis=-1, keepdims=True)
        p = p * (1.0 / l)
        o_ref[...] = _dot(p.astype(BF16), v_ref[...], 1, 0)
        lse_ref[0] = m + jnp.log(l)

    per_q, k_nope, v_blk, k_rope, lse = _attn_blocks(s, tq)
    return pl.pallas_call(
        body, name="attn_fwd", grid=(MLA_HEADS, s // tq), in_specs=[per_q, per_q, k_nope, v_blk, k_rope],
        out_specs=[per_q, lse], out_shape=[_sds(qn.shape), _sds((MLA_HEADS, s, 1))],
        compiler_params=_params(("parallel", "parallel")),
    )(qn, qr, kv, kv, kr)


def _attn_bwd(qn, qr, kv, kr, o, lse, do):
    s = qn.shape[0]
    tq = min(ATTN_TQ, s)
    n_q = s // tq
    scale = (MLA_NOPE + MLA_ROPE) ** -0.5

    def body(qn_ref, qr_ref, kn_ref, v_ref, kr_ref, o_ref, lse_ref, do_ref, dqn_ref, dqr_ref, dkv_ref, dkr_ref,
             dk_acc, dv_acc, dkr_acc):
        h, j = pl.program_id(0), pl.program_id(1)
        q = jnp.concatenate([qn_ref[...], qr_ref[...]], axis=1)
        k = jnp.concatenate([kn_ref[...], kr_ref[...]], axis=1)
        do = do_ref[...]
        do_b = do.astype(BF16)
        p = jnp.exp(_dot(q, k, 1, 1) * scale - lse_ref[0])
        dp = _dot(do_b, v_ref[...], 1, 1)
        delta = jnp.sum(do * o_ref[...], axis=-1, keepdims=True)
        ds = (p * (dp - delta) * scale).astype(BF16)
        dq = _dot(ds, k, 1, 0)
        dqn_ref[...] = dq[:, :HEAD_LANES].astype(BF16)
        dqr_ref[...] = dq[:, HEAD_LANES:].astype(BF16)
        dk = _dot(ds, q, 0, 0)
        _acc(j, dk_acc, dk[:, :HEAD_LANES])
        _acc(j, dv_acc, _dot(p.astype(BF16), do_b, 0, 0))
        _acc(jnp.where(jnp.logical_and(h == 0, j == 0), 0, 1), dkr_acc, dk[:, HEAD_LANES:])

        @pl.when(j == n_q - 1)
        def _():
            dkv_ref[:, 0:HEAD_LANES] = dk_acc[...].astype(BF16)
            dkv_ref[:, HEAD_LANES:2 * HEAD_LANES] = dv_acc[...].astype(BF16)

        @pl.when(jnp.logical_and(h == MLA_HEADS - 1, j == n_q - 1))
        def _():
            dkr_ref[...] = dkr_acc[...].astype(BF16)

    per_q, k_nope, v_blk, k_rope, lse_blk = _attn_blocks(s, tq)
    dkv_blk = pl.BlockSpec((s, 2 * HEAD_LANES), lambda h, j: (0, h))
    acc = pltpu.VMEM((s, HEAD_LANES), F32)
    return pl.pallas_call(
        body, name="attn_bwd", grid=(MLA_HEADS, n_q),
        in_specs=[per_q, per_q, k_nope, v_blk, k_rope, per_q, lse_blk, per_q],
        out_specs=[per_q, per_q, dkv_blk, k_rope],
        out_shape=[_sds(qn.shape, BF16), _sds(qr.shape, BF16), _sds(kv.shape, BF16), _sds(kr.shape, BF16)],
        scratch_shapes=[acc, acc, acc], compiler_params=_params(("arbitrary", "arbitrary")),
    )(qn, qr, kv, kv, kr, o, lse, do)


@jax.custom_vjp
def attn(qn, qr, kv, kr):
    return _attn_fwd(qn, qr, kv, kr)[0]


def _attn_f(qn, qr, kv, kr):
    o, lse = _attn_fwd(qn, qr, kv, kr)
    return o, (qn, qr, kv, kr, o, lse)


def _attn_b(res, do):
    return tuple(_attn_bwd(*res, do))


attn.defvjp(_attn_f, _attn_b)


@jax.custom_vjp
def split_proj(proj):
    out, at = [], 0
    for _, _, _, wp in PROJ_SEGS:
        out.append(proj[:, at:at + wp])
        at += wp
    return tuple(out)


def _split_f(proj):
    return split_proj(proj), None


def _split_b(_, gs):
    return (jnp.concatenate(gs, axis=1),)


split_proj.defvjp(_split_f, _split_b)


def _tile2d(rows, width, limit=BLOCK_BYTES):
    fits = [t for t in range(16, rows + 1, 16) if rows % t == 0 and t * width * 4 <= limit]
    if fits and (fits[-1] >= 64 or fits[-1] == rows):
        return fits[-1], width
    if rows * width * 4 <= limit:
        return rows, width
    cols = [t for t in range(128, width + 1, 128) if width % t == 0 and rows * t * 4 <= limit]
    return (rows, cols[-1]) if cols else (rows, width)


def _add_pair(stacked, theirs, c_idx):
    g, r, w = theirs.shape
    tr, tc = _tile2d(r, w)

    def body(c_ref, a_ref, b_ref, o_ref):
        o_ref[0] = (a_ref[0, 0].astype(F32) + b_ref[0].astype(F32)).astype(BF16)

    blk = pl.BlockSpec((1, tr, tc), lambda k, i, j, c: (k, i, j))
    spec = pltpu.PrefetchScalarGridSpec(
        num_scalar_prefetch=1, grid=(g, r // tr, w // tc),
        in_specs=[pl.BlockSpec((1, 1, tr, tc), lambda k, i, j, c: (c[0], k, i, j)), blk], out_specs=blk)
    return pl.pallas_call(body, name="add_pair", grid_spec=spec, out_shape=_sds(theirs.shape, BF16),
                          compiler_params=_params(("parallel", "parallel", "parallel")))(c_idx, stacked, theirs)


def _add_chips(pair, landed, chip_idx):
    _, r, w = pair.shape
    tr, tc = _tile2d(r, w)

    def body(c_ref, p_ref, l0_ref, l1_ref, l2_ref, o_ref):
        o_ref[...] = ((p_ref[0].astype(F32) + l0_ref[0].astype(F32)) + l1_ref[0].astype(F32)) + l2_ref[0].astype(F32)

    specs = [pl.BlockSpec((1, tr, tc), lambda i, j, c: (c[0], i, j))]
    specs += [pl.BlockSpec((1, tr, tc), functools.partial(lambda i, j, c, k: (k, i, j), k=k)) for k in range(N_CHIPS - 1)]
    spec = pltpu.PrefetchScalarGridSpec(num_scalar_prefetch=1, grid=(r // tr, w // tc), in_specs=specs,
                                        out_specs=pl.BlockSpec((tr, tc), lambda i, j, c: (i, j)))
    return pl.pallas_call(body, name="add_chips", grid_spec=spec, out_shape=_sds((r, w)),
                          compiler_params=_params(("parallel", "parallel")))(chip_idx, pair, landed, landed, landed)


def _sum_devices(g):
    n = g.shape[2]

    def body(g_ref, o_ref, done_ref):
        t = g_ref[0]
        for j in range(1, N_DEV):
            t = t + g_ref[j]
        o_ref[...] = t
        done_ref[...] = jnp.zeros_like(done_ref)

    return pl.pallas_call(body, name="sum_devices", out_shape=[_sds((1, n)), _sds((8, 128))],
                          compiler_params=_params())(g)


def _adamw_math(w, gv, m, v):
    c1 = 1.0 - ADAM_B1 ** ADAM_STEP
    c2 = 1.0 - ADAM_B2 ** ADAM_STEP
    mn = ADAM_B1 * m + (1.0 - ADAM_B1) * gv
    vn = ADAM_B2 * v + (1.0 - ADAM_B2) * (gv * gv)
    return -ADAM_LR * ((mn / c1) / (jnp.sqrt(vn / c2) + ADAM_EPS) + ADAM_WD * w), mn, vn


def _adamw(w, g, m, v):
    shp = w.shape
    shp3 = (1, 1, shp[0]) if len(shp) == 1 else (-1,) + tuple(shp[-2:])
    w3, g3, m3, v3 = (t.reshape(shp3) for t in (w, g, m, v))

    def body(w_ref, g_ref, m_ref, v_ref, d_ref, mo_ref, vo_ref):
        d_ref[...], mo_ref[...], vo_ref[...] = _adamw_math(w_ref[...], g_ref[...], m_ref[...], v_ref[...])

    nl, r, wd = w3.shape
    tr, tc = _tile2d(r, wd, BLOCK_BYTES // 2)
    blk = pl.BlockSpec((1, tr, tc), lambda l, i, j: (l, i, j))
    s3 = _sds(w3.shape)
    d, mn, vn = pl.pallas_call(
        body, name="adamw", grid=(nl, r // tr, wd // tc), in_specs=[blk] * 4, out_specs=[blk] * 3,
        out_shape=[s3, s3, s3], compiler_params=_params(("parallel", "parallel", "parallel")),
    )(w3, g3, m3, v3)
    return d.reshape(shp), mn.reshape(shp), vn.reshape(shp)


PIECE_BYTES = 1 << 20


def _place():
    return lax.axis_index("x"), lax.axis_index("y"), lax.axis_index("c")


def _pieces(shape, itemsize):
    if len(shape) >= 3:
        return [(i,) + p for i in range(shape[0]) for p in _pieces(shape[1:], itemsize)]
    rows = shape[0]
    row_bytes = itemsize
    for dsz in shape[1:]:
        row_bytes *= dsz
    k = 1
    while rows % (2 * k) == 0 and (rows // (2 * k)) % 16 == 0 and (rows // k) * row_bytes > PIECE_BYTES:
        k *= 2
    step = rows // k
    return [(pl.ds(j * step, step),) for j in range(k)]


def _split_start(make, src, dst, pieces):
    for p in pieces:
        make(src.at[p], dst.at[p]).start()
    return make(src, dst)


def _comm_call(body, name, arrs, out_shapes, n_remote, n_local):
    return pl.pallas_call(
        body, name=name, in_specs=[ANY] * len(arrs), out_specs=[ANY] * len(out_shapes), out_shape=out_shapes,
        scratch_shapes=[pltpu.SemaphoreType.DMA((n_remote,)), pltpu.SemaphoreType.DMA((n_remote,)),
                        pltpu.SemaphoreType.DMA((n_local,))],
    )(*arrs)


def all_gather8(arrs, name):
    n = len(arrs)
    pieces = [_pieces(a.shape, a.dtype.itemsize) for a in arrs]

    def body(*refs):
        ins, outs = refs[:n], refs[n:2 * n]
        send, recv, _ = refs[2 * n:]
        x, y, c = _place()
        me, sib = (x, y, c), (x, y, 1 - c)
        chips = [(1 - x, y), (x, 1 - y), (1 - x, 1 - y)]

        def slot(p):
            return 4 * p[0] + 2 * p[1] + p[2]

        def maker(t, k, to):
            def make(s, d):
                return pltpu.make_async_remote_copy(src_ref=s, dst_ref=d, send_sem=send.at[7 * t + k],
                                                    recv_sem=recv.at[7 * t + k], device_id=to, device_id_type=MESH)
            return make

        def landing(t, k, block):
            dst = outs[t].at[slot(block)]
            return maker(t, k, me)(dst, dst)

        sent = []
        for t in range(n):
            dst = outs[t].at[slot(me)]
            sent.append(_split_start(maker(t, 0, sib), ins[t], dst, pieces[t]))
            for j, chip in enumerate(chips):
                sent.append(_split_start(maker(t, 1 + j, (*chip, c)), ins[t], dst, pieces[t]))
        for j, chip in enumerate(chips):
            for t in range(n):
                landing(t, 1 + j, (*chip, c)).wait_recv()
                blk = outs[t].at[slot((*chip, c))]
                sent.append(_split_start(maker(t, 4 + j, sib), blk, blk, pieces[t]))
        for t in range(n):
            landing(t, 0, sib).wait_recv()
            for j, chip in enumerate(chips):
                landing(t, 4 + j, (*chip, 1 - c)).wait_recv()
        for cp in sent:
            cp.wait_send()

    outs = [_sds((N_DEV,) + a.shape, a.dtype) for a in arrs]
    got = _comm_call(body, name, arrs, outs, 7 * n, 1)
    x, y, c = _place()
    return [lax.dynamic_update_index_in_dim(g, a, 4 * x + 2 * y + c, 0) for g, a in zip(got, arrs)]


def sibling_send(arrs, name):
    n = len(arrs)
    pieces = [_pieces(a.shape[1:], a.dtype.itemsize) for a in arrs]

    def body(*refs):
        ins, theirs = refs[:n], refs[n:2 * n]
        send, recv, _ = refs[2 * n:]
        x, y, c = _place()
        rem = []
        for t in range(n):
            def make(s, d, t=t):
                return pltpu.make_async_remote_copy(src_ref=s, dst_ref=d, send_sem=send.at[t], recv_sem=recv.at[t],
                                                    device_id=(x, y, 1 - c), device_id_type=MESH)
            rem.append(_split_start(make, ins[t].at[1 - c], theirs[t], pieces[t]))
        for cp in rem:
            cp.wait_recv()
        for cp in rem:
            cp.wait_send()

    outs = [_sds(a.shape[1:], a.dtype) for a in arrs]
    return _comm_call(body, name, arrs, outs, n, 1)


def exchange_chips(arrs, name):
    n = len(arrs)
    pieces = [_pieces(a.shape[1:], a.dtype.itemsize) for a in arrs]

    def body(*refs):
        ins, outs = refs[:n], refs[n:2 * n]
        send, recv, _ = refs[2 * n:]
        x, y, c = _place()
        peers = [(1 - x, y), (x, 1 - y), (1 - x, 1 - y)]
        rem = []
        for t in range(n):
            for j, (px, py) in enumerate(peers):
                def make(s, d, t=t, j=j, px=px, py=py):
                    return pltpu.make_async_remote_copy(
                        src_ref=s, dst_ref=d, send_sem=send.at[3 * t + j], recv_sem=recv.at[3 * t + j],
                        device_id=(px, py, c), device_id_type=MESH)
                rem.append(_split_start(make, ins[t].at[2 * px + py], outs[t].at[j], pieces[t]))
        for cp in rem:
            cp.wait_recv()
        for cp in rem:
            cp.wait_send()

    outs = [_sds((N_CHIPS - 1,) + a.shape[1:], a.dtype) for a in arrs]
    return _comm_call(body, name, arrs, outs, 3 * n, 1)


def sibling_swap(arrs, name):
    n = len(arrs)
    pieces = [_pieces(a.shape, a.dtype.itemsize) for a in arrs]

    def body(*refs):
        ins, outs = refs[:n], refs[n:2 * n]
        send, recv, _ = refs[2 * n:]
        x, y, c = _place()
        rem = []
        for t in range(n):
            def make(s, d, t=t):
                return pltpu.make_async_remote_copy(src_ref=s, dst_ref=d, send_sem=send.at[t], recv_sem=recv.at[t],
                                                    device_id=(x, y, 1 - c), device_id_type=MESH)
            rem.append(_split_start(make, ins[t], outs[t], pieces[t]))
        for cp in rem:
            cp.wait_recv()
        for cp in rem:
            cp.wait_send()

    outs = [_sds(a.shape, a.dtype) for a in arrs]
    return _comm_call(body, name, arrs, outs, n, 1)


def _peer_copies(srcs, lands, send, recv, mode):
    x, y, c = _place()
    my_chip = 2 * x + y
    out = []
    for t in range(len(srcs)):
        for j, (px, py) in enumerate([(1 - x, y), (x, 1 - y), (1 - x, 1 - y)]):
            if mode == "gather":
                s, dst = srcs[t], lands[t].at[c, my_chip]
            else:
                s, dst = srcs[t].at[2 * px + py], lands[t].at[j]
            out.append(pltpu.make_async_remote_copy(
                src_ref=s, dst_ref=dst, send_sem=send.at[3 * t + j], recv_sem=recv.at[3 * t + j],
                device_id=(px, py, c), device_id_type=MESH))
    return out


HBM = pl.BlockSpec(memory_space=pltpu.HBM)
SEM = pl.BlockSpec(memory_space=pltpu.SEMAPHORE)
EFFECT = pltpu.SideEffectType.DATAFLOW_SIDE_EFFECTING


def ici_start(srcs, lands, mode, name):
    n = len(srcs)

    def body(*refs):
        send, recv = refs[2 * n], refs[2 * n + 1]
        for cp in _peer_copies(refs[:n], refs[n:2 * n], send, recv, mode):
            cp.start()
        refs[-1][...] = jnp.zeros_like(refs[-1])

    thru = [pltpu.HBM(a.shape, a.dtype) for a in list(srcs) + list(lands)]
    outs = pl.pallas_call(
        body, name=name, in_specs=[HBM] * (2 * n), out_specs=[SEM, SEM] + [HBM] * (2 * n) + [pl.BlockSpec(memory_space=pltpu.VMEM)],
        out_shape=[pltpu.SemaphoreType.DMA((3 * n,)), pltpu.SemaphoreType.DMA((3 * n,))] + thru + [_sds((8, 128))],
        input_output_aliases={i: 2 + i for i in range(2 * n)},
        compiler_params=pltpu.CompilerParams(has_side_effects=EFFECT),
    )(*[pltpu.with_memory_space_constraint(a, pltpu.HBM) for a in list(srcs) + list(lands)])
    return dict(send=outs[0], recv=outs[1], srcs=outs[2:2 + n], lands=outs[2 + n:2 + 2 * n], token=outs[-1])


def ici_wait(handle, after, mode, name):
    n = len(handle["srcs"])

    def body(*refs):
        send, recv = refs[2 * n], refs[2 * n + 1]
        for cp in _peer_copies(refs[:n], refs[n:2 * n], send, recv, mode):
            cp.wait_send()
            cp.wait_recv()

    arrs = list(handle["srcs"]) + list(handle["lands"])
    outs = pl.pallas_call(
        body, name=name, in_specs=[HBM] * (2 * n) + [SEM, SEM, ANY], out_specs=[HBM] * (2 * n),
        out_shape=[pltpu.HBM(a.shape, a.dtype) for a in arrs], input_output_aliases={i: i for i in range(2 * n)},
        compiler_params=pltpu.CompilerParams(has_side_effects=EFFECT),
    )(*arrs, handle["send"], handle["recv"], after)
    return outs[:n], outs[n:]


def gather_share(blocks, lands, name):
    n = len(blocks)

    def body(*refs):
        own, buf = refs[:n], refs[2 * n:3 * n]
        done, send, recv = refs[3 * n:]
        x, y, c = _place()
        my_chip = 2 * x + y
        chips = [2 * (1 - x) + y, 2 * x + (1 - y), 2 * (1 - x) + (1 - y)]
        sent = []
        for t in range(n):
            def make(s, d, k, t=t):
                return pltpu.make_async_remote_copy(src_ref=s, dst_ref=d, send_sem=send.at[4 * t + k],
                                                    recv_sem=recv.at[4 * t + k], device_id=(x, y, 1 - c),
                                                    device_id_type=MESH)
            cp = make(own[t], buf[t].at[c, my_chip], 0)
            cp.start()
            sent.append(cp)
            for k, pc in enumerate(chips):
                cp = make(buf[t].at[c, pc], buf[t].at[c, pc], 1 + k)
                cp.start()
                sent.append(cp)
        for t in range(n):
            for k in range(4):
                got = buf[t].at[1 - c, k]
                pltpu.make_async_remote_copy(src_ref=got, dst_ref=got, send_sem=send.at[4 * t + k],
                                             recv_sem=recv.at[4 * t + k], device_id=(x, y, 1 - c),
                                             device_id_type=MESH).wait_recv()
        for cp in sent:
            cp.wait_send()
        done[...] = jnp.zeros_like(done)

    outs = pl.pallas_call(
        body, name=name, in_specs=[ANY] * (2 * n), out_specs=[ANY] * n + [pl.BlockSpec(memory_space=pltpu.VMEM)],
        out_shape=[_sds(a.shape, a.dtype) for a in lands] + [_sds((8, 128))],
        input_output_aliases={n + t: t for t in range(n)},
        scratch_shapes=[pltpu.SemaphoreType.DMA((4 * n,)), pltpu.SemaphoreType.DMA((4 * n,))],
    )(*blocks, *lands)
    return outs[:n], outs[n]


@jax.custom_vjp
def _build_w_in(w4):
    full = w4.reshape(-1, w4.shape[-1])
    parts = []
    for _, start, width, wp in PROJ_SEGS:
        if width:
            parts.append(full[start:start + width])
        if wp > width:
            parts.append(jnp.zeros((wp - width, full.shape[1]), full.dtype))
    return jnp.concatenate(parts, axis=0)


def _build_w_in_f(w4):
    return _build_w_in(w4), None


def _build_w_in_b(_, g):
    parts, at = [], 0
    for _, _, width, wp in PROJ_SEGS:
        if width:
            parts.append(g[at:at + width])
        at += wp
    return (jnp.concatenate(parts, axis=0).reshape(N_CHIPS, -1, g.shape[1]),)


_build_w_in.defvjp(_build_w_in_f, _build_w_in_b)


def _split_w_uq(w):
    w3 = w.reshape(w.shape[0], MLA_HEADS, MLA_NOPE + MLA_ROPE)
    return w3[:, :, :MLA_NOPE].reshape(w.shape[0], -1), w3[:, :, MLA_NOPE:].reshape(w.shape[0], -1)


def _swap_halves(t, width):
    t3 = t.reshape(t.shape[0], -1, 2, width // 2)
    return jnp.concatenate([t3[:, :, 1:], t3[:, :, :1]], axis=2).reshape(t.shape)


def _pad_heads(t, width):
    t3 = t.reshape(t.shape[0], -1, width)
    t3 = jnp.pad(t3, ((0, 0), (0, 0), (0, HEAD_LANES - width)))
    return t3.reshape(t.shape[0], -1).astype(BF16)


def _layer(xh, mod, big, small, rope_q, rope_k):
    d = D_MODEL
    shift, scale, gate = mod[None, 0:d], mod[None, d:2 * d], mod[None, 2 * d:3 * d]
    w_al = _build_w_in(big["w_in"])
    proj = mod_mm(xh, small["norm_g"][None], scale, shift, w_al)
    gq, gk, gv, glr, mq, mkv, mkr, cb, cc, cx, _, z = split_proj(proj)

    rk = GLA_RANK
    hk = GLA_HEADS * GLA_DK
    wg = jnp.zeros((128, 2 * hk), F32)
    wg = wg.at[0:rk, 0:hk].set(small["gla_wg_f"]).at[rk:2 * rk, hk:].set(small["gla_wg_b"])
    bg = jnp.concatenate([small["gla_bg_f"], small["gla_bg_b"]])[None]
    la = gate_act(mm(glr, wg), bg)
    o_gla = rmsnorm(gla(gq, gk, gv, la), small["gla_norm_g"][None])

    cq = rmsnorm(mq, small["mla_q_norm_g"][None])
    w_nope, w_rope = _split_w_uq(jnp.concatenate([big["w_uq"][j] for j in range(N_CHIPS)], axis=1))
    qn = mm16(cq, w_nope)
    qr = mm(cq, w_rope)
    qr = fma(qr, rope_q[0], _swap_halves(qr, MLA_ROPE), rope_q[1])
    ckv = rmsnorm(mkv, small["mla_kv_norm_g"][None])
    kv = mm16(ckv, jnp.concatenate([big["w_ukv"][j] for j in range(N_CHIPS)], axis=1))
    kr = mkr[:, :MLA_ROPE]
    kr = fma(kr, rope_k[0], _swap_halves(kr, MLA_ROPE), rope_k[1])
    o_mla = rmsnorm(attn(qn, _pad_heads(qr, MLA_ROPE), kv, _pad_heads(kr, MLA_ROPE)), small["mla_out_g"][None])

    cw = jnp.concatenate([small["conv_w"], jnp.zeros((5, CONV_CH), F32)], axis=0)
    o_conv = rmsnorm(conv_op(cb, cc, cx, cw), small["conv_out_g"][None])

    o = jnp.concatenate([o_gla, o_mla, o_conv], axis=1)
    w_out = big["w_out"].reshape(d, d)
    return out_block(o, z, w_out, xh, gate)


SMALL_REPL = ("norm_g", "gla_bg_f", "gla_bg_b", "gla_norm_g", "mla_q_norm_g", "mla_kv_norm_g", "mla_out_g",
              "conv_out_g")
SMALL_SHARDED = ("gla_wg_f", "gla_wg_b", "conv_w")
BIG = ("w_in", "w_out", "w_uq", "w_ukv")
HALF_AXIS = (1, 0, 0, 0)


def kernel(x, c, positions, ada_w, ada_b, norm_g, w_in, gla_wg_f, gla_bg_f, gla_wg_b, gla_bg_b, gla_norm_g, mla_q_norm_g, mla_kv_norm_g, mla_w_uq, mla_w_ukv, mla_out_g, conv_w, conv_out_g, w_out, final_g, loss_target, m_ada_w, m_ada_b, m_norm_g, m_w_in, m_gla_wg_f, m_gla_bg_f, m_gla_wg_b, m_gla_bg_b, m_gla_norm_g, m_mla_q_norm_g, m_mla_kv_norm_g, m_mla_w_uq, m_mla_w_ukv, m_mla_out_g, m_conv_w, m_conv_out_g, m_w_out, m_final_g, v_ada_w, v_ada_b, v_norm_g, v_w_in, v_gla_wg_f, v_gla_bg_f, v_gla_wg_b, v_gla_bg_b, v_gla_norm_g, v_mla_q_norm_g, v_mla_kv_norm_g, v_mla_w_uq, v_mla_w_ukv, v_mla_out_g, v_conv_w, v_conv_out_g, v_w_out, v_final_g):
    xi, yi, ci = _place()
    chip = 2 * xi + yi
    dev = 2 * chip + ci
    s = x.shape[1]
    d = D_MODEL
    weights = dict(ada_w=ada_w, ada_b=ada_b, norm_g=norm_g, w_in=w_in, gla_wg_f=gla_wg_f, gla_bg_f=gla_bg_f,
                   gla_wg_b=gla_wg_b, gla_bg_b=gla_bg_b, gla_norm_g=gla_norm_g, mla_q_norm_g=mla_q_norm_g,
                   mla_kv_norm_g=mla_kv_norm_g, mla_w_uq=mla_w_uq, mla_w_ukv=mla_w_ukv, mla_out_g=mla_out_g,
                   conv_w=conv_w, conv_out_g=conv_out_g, w_out=w_out, final_g=final_g)
    m_in = dict(ada_w=m_ada_w, ada_b=m_ada_b, norm_g=m_norm_g, w_in=m_w_in, gla_wg_f=m_gla_wg_f, gla_bg_f=m_gla_bg_f,
                gla_wg_b=m_gla_wg_b, gla_bg_b=m_gla_bg_b, gla_norm_g=m_gla_norm_g, mla_q_norm_g=m_mla_q_norm_g,
                mla_kv_norm_g=m_mla_kv_norm_g, mla_w_uq=m_mla_w_uq, mla_w_ukv=m_mla_w_ukv, mla_out_g=m_mla_out_g,
                conv_w=m_conv_w, conv_out_g=m_conv_out_g, w_out=m_w_out, final_g=m_final_g)
    v_in = dict(ada_w=v_ada_w, ada_b=v_ada_b, norm_g=v_norm_g, w_in=v_w_in, gla_wg_f=v_gla_wg_f, gla_bg_f=v_gla_bg_f,
                gla_wg_b=v_gla_wg_b, gla_bg_b=v_gla_bg_b, gla_norm_g=v_gla_norm_g, mla_q_norm_g=v_mla_q_norm_g,
                mla_kv_norm_g=v_mla_kv_norm_g, mla_w_uq=v_mla_w_uq, mla_w_ukv=v_mla_w_ukv, mla_out_g=v_mla_out_g,
                conv_w=v_conv_w, conv_out_g=v_conv_out_g, w_out=v_w_out, final_g=v_final_g)

    g_c, g_wgf, g_wgb, g_cw = all_gather8([c, gla_wg_f, gla_wg_b, conv_w], "gather_small")

    def unshard_cols(g):
        g4 = g[0::2]
        return g4.transpose(1, 2, 0, 3).reshape(g4.shape[1], g4.shape[2], -1)

    small_full = dict(gla_wg_f=unshard_cols(g_wgf), gla_wg_b=unshard_cols(g_wgb), conv_w=unshard_cols(g_cw))
    for nme in SMALL_REPL:
        small_full[nme] = weights[nme]
    smalls = [{nme: small_full[nme][l] for nme in SMALL_REPL + SMALL_SHARDED} for l in range(DEPTH)]

    big_src = (jnp.swapaxes(w_in, 1, 2), w_out, mla_w_uq, mla_w_ukv)

    def my_halves(l, zero=0):
        out = []
        for t, a in enumerate(big_src):
            n_half = a.shape[1 + HALF_AXIS[t]] // 2
            out.append(lax.dynamic_slice_in_dim(a[l], ci * n_half + zero, n_half, axis=HALF_AXIS[t]).astype(BF16))
        return out

    def landing(blocks):
        return [lax.empty((2, N_CHIPS) + b.shape, b.dtype) for b in blocks]

    def finish_gather(handle, after, tag):
        blocks, lands = ici_wait(handle, after, "gather", "gather_wait" + tag)
        lands, done = gather_share(blocks, lands, "gather_share" + tag)
        full = [lax.dynamic_update_slice(g, b[None, None], (ci, chip) + (0,) * b.ndim) for g, b in zip(lands, blocks)]
        return full, done

    halves0 = my_halves(0)
    started0 = ici_start(halves0, landing(halves0), "gather", "gather_start0")

    c_act = _silu_rows(g_c[:, 0, :])
    c_act16 = jnp.concatenate([c_act, jnp.zeros_like(c_act)], axis=0)
    n_ada = ada_w.shape[2]
    parts = []
    for l in range(DEPTH):
        bias = lax.dynamic_slice_in_dim(ada_b[l], chip * n_ada, n_ada)[None]
        parts.append(_mm(c_act16, ada_w[l], bias=bias, name="ada_fwd"))
    g_mod, = all_gather8([jnp.stack(parts)], "gather_mod")
    mod_mine = lax.dynamic_index_in_dim(g_mod[0::2], dev, 2, keepdims=False)
    mods = mod_mine.transpose(1, 0, 2).reshape(DEPTH, 3 * d)

    inv_freq = ROPE_THETA ** (-jnp.arange(0, MLA_ROPE, 2, dtype=F32) / MLA_ROPE)
    ang = positions[0].astype(F32)[:, None] * inv_freq
    cos, sin = jnp.cos(ang), jnp.sin(ang)
    rope_k = (jnp.concatenate([cos, cos], axis=1), jnp.concatenate([-sin, sin], axis=1))
    rope_q = (jnp.tile(rope_k[0], (1, MLA_HEADS)), jnp.tile(rope_k[1], (1, MLA_HEADS)))

    def run_layer(xh, mod, gathered, small):
        big = {nme: jnp.concatenate([g[0], g[1]], axis=HALF_AXIS[t] + 1) for t, (nme, g) in enumerate(zip(BIG, gathered))}
        return _layer(xh, mod, big, small, rope_q, rope_k)

    def head(hh, fg):
        return loss_op(rmsnorm(hh, fg[None]), loss_target[0])[0, 0]

    gathered0, done0 = finish_gather(started0, mods, "0")
    halves1 = my_halves(1, done0[0, 0].astype(jnp.int32))
    started1 = ici_start(halves1, landing(halves1), "gather", "gather_start1")
    h1, vjp0 = jax.vjp(run_layer, x[0], mods[0] + started1["token"][0, 0], gathered0, smalls[0])
    gathered1, _ = finish_gather(started1, h1, "1")
    h2, vjp1 = jax.vjp(run_layer, h1, mods[1], gathered1, smalls[1])
    loss_dev, vjp_head = jax.vjp(head, h2, final_g)
    loss = lax.psum(loss_dev, ("x", "y", "c"))
    dh2, dfinal = vjp_head(jnp.ones((), F32))

    c_idx = jnp.reshape(ci, (1,)).astype(jnp.int32)
    chip_idx = jnp.reshape(chip, (1,)).astype(jnp.int32)

    def reduce_begin(dgath, tag, zero=None):
        theirs = sibling_send(dgath, "reduce_sibling" + tag)
        pair = [_add_pair(a, b, c_idx) for a, b in zip(dgath, theirs)]
        shapes = [(N_CHIPS - 1,) + p.shape[1:] for p in pair]
        if zero is None:
            lands = [lax.empty(shp, BF16) for shp in shapes]
        else:
            lands = [jnp.broadcast_to(zero.astype(BF16), shp) for shp in shapes]
        return ici_start(pair, lands, "reduce", "reduce_start" + tag)

    def reduce_end(handle, after, tag):
        pair, landed = ici_wait(handle, after, "reduce", "reduce_wait" + tag)
        reduced = [_add_chips(p, q, chip_idx) for p, q in zip(pair, landed)]
        others = sibling_swap(reduced, "share_sibling" + tag)
        return [jnp.where(ci == 0, jnp.concatenate([own, other], axis=HALF_AXIS[t]),
                          jnp.concatenate([other, own], axis=HALF_AXIS[t]))
                for t, (own, other) in enumerate(zip(reduced, others))]

    dh1, dmod1, dgath1, dsmall1 = vjp1(dh2)
    reducing1 = reduce_begin(dgath1, "1")
    dx, dmod0, dgath0, dsmall0 = vjp0(dh1 + reducing1["token"][0, 0])
    dmods = jnp.stack([dmod0, dmod1])
    dsmalls = [dsmall0, dsmall1]

    pieces = [dmods.reshape(-1), dfinal]
    for nme in SMALL_REPL + SMALL_SHARDED:
        pieces.append(jnp.stack([dsmalls[l][nme] for l in range(DEPTH)]).reshape(-1))
    sizes = [p.shape[0] for p in pieces]
    flat = jnp.concatenate(pieces)
    padn = (-flat.shape[0]) % 128
    flat = jnp.concatenate([flat, jnp.zeros((padn,), F32)])[None]
    g_small, = all_gather8([flat], "gather_small_grads")
    total, small_done = _sum_devices(g_small)
    total = total[0]
    reducing0 = reduce_begin(dgath0, "0", small_done[0, 0])
    offs, at = [], 0
    for n_el in sizes:
        offs.append(at)
        at += n_el

    def piece(i, shape):
        return total[offs[i]:offs[i] + sizes[i]].reshape(shape)

    grads = {"ada_b": piece(0, (DEPTH, 3 * d)), "final_g": piece(1, (d,))}
    for i, nme in enumerate(SMALL_REPL + SMALL_SHARDED):
        full = piece(2 + i, small_full[nme].shape)
        if nme in SMALL_SHARDED:
            ncol = weights[nme].shape[2]
            full = lax.dynamic_slice_in_dim(full, chip * ncol, ncol, axis=2)
        grads[nme] = full

    dmod_all = g_small[:, 0, :DEPTH * 3 * d].reshape(N_DEV, DEPTH, 3 * d)
    dmod_cols = lax.dynamic_slice_in_dim(dmod_all, chip * n_ada, n_ada, axis=2)
    g_ada = []
    for l in range(DEPTH):
        dm16 = jnp.concatenate([dmod_cols[:, l], jnp.zeros((N_DEV, n_ada), F32)], axis=0)
        dm16 = dm16 + reducing0["token"][0, 0]
        g_ada.append(_mm(c_act16, dm16, ta=True, name="ada_bwd"))
    grads["ada_w"] = jnp.stack(g_ada)

    order = list(weights)
    big_names = ("w_in", "w_out", "mla_w_uq", "mla_w_ukv")
    delta, new_m, new_v = {}, {}, {}
    for nme in order:
        if nme not in big_names:
            delta[nme], new_m[nme], new_v[nme] = _adamw(weights[nme], grads[nme], m_in[nme], v_in[nme])

    big_grads1 = reduce_end(reducing1, jnp.concatenate([dx.reshape(-1)[:1], reducing0["token"].reshape(-1)[:1]]), "1")
    done = [delta[nme].reshape(-1)[:1] for nme in order if nme not in big_names]
    done += [g.reshape(-1)[:1] for g in big_grads1]
    big_grads0 = reduce_end(reducing0, jnp.concatenate(done), "0")
    for nme, g0, g1 in zip(big_names, big_grads0, big_grads1):
        grads[nme] = jnp.stack([g0, g1])
    for nme in big_names:
        if nme == "w_in":
            w_t, m_t, v_t = (jnp.swapaxes(t, 1, 2) for t in (w_in, m_w_in, v_w_in))
            res = _adamw(w_t, grads[nme], m_t, v_t)
            delta[nme], new_m[nme], new_v[nme] = (jnp.swapaxes(t, 1, 2) for t in res)
            grads[nme] = jnp.swapaxes(grads[nme], 1, 2)
            continue
        delta[nme], new_m[nme], new_v[nme] = _adamw(weights[nme], grads[nme], m_in[nme], v_in[nme])
    return (loss, dx[None], *[grads[n_] for n_ in order], *[delta[n_] for n_ in order],
            *[new_m[n_] for n_ in order], *[new_v[n_] for n_ in order])
```

```python
import functools

import jax
import jax.numpy as jnp
from jax import lax
from jax.experimental import pallas as pl
from jax.experimental.pallas import tpu as pltpu

F32 = jnp.float32
BF16 = jnp.bfloat16
MESH = pl.DeviceIdType.MESH
HIGHEST = lax.Precision.HIGHEST

DEPTH = 2
D_MODEL = 2048
GLA_HEADS = 6
GLA_DK = 64
GLA_DV = 128
GLA_RANK = 16
GLA_TEMP = 16.0
GLA_CHUNK = 64
GLA_W = GLA_HEADS * GLA_DV
MLA_HEADS = 6
MLA_QL = 384
MLA_KVL = 256
MLA_NOPE = 128
MLA_ROPE = 64
MLA_DV = 128
MLA_W = MLA_HEADS * MLA_DV
CONV_CH = D_MODEL - GLA_W - MLA_W
ROPE_THETA = 10000.0
EPS = 1e-6
IN_DIM = 5856
N_CHIPS = 4
N_DEV = 8

ADAM_LR = 0.001
ADAM_B1 = 0.9
ADAM_B2 = 0.999
ADAM_EPS = 1e-08
ADAM_WD = 0.01
ADAM_STEP = 10

PROJ_SEGS = (
    ("gq", 0, 384, 384), ("gk", 384, 384, 384), ("gv", 768, 768, 768), ("glr", 1536, 32, 128),
    ("mq", 1568, 384, 384), ("mkv", 1952, 256, 256), ("mkr", 2208, 64, 128),
    ("cb", 2272, 512, 512), ("cc", 2784, 512, 512), ("cx", 3296, 512, 512),
    ("pad", 3808, 0, 128), ("z", 3808, 2048, 2048),
)
PROJ_AL = sum(s[3] for s in PROJ_SEGS)

ANY = pl.BlockSpec(memory_space=pl.ANY)
VMEM_LIMIT = 48 * 1024 * 1024
BLOCK_BYTES = 2 * 1024 * 1024


def _params(sem=None):
    return pltpu.CompilerParams(dimension_semantics=sem, vmem_limit_bytes=VMEM_LIMIT)


def _dot(a, b, ca, cb, precision=None):
    return lax.dot_general(a, b, (((ca,), (cb,)), ((), ())), preferred_element_type=F32, precision=precision)


def _tile(dim, prefs):
    for t in prefs:
        if dim % t == 0:
            return t
    return dim


def _pick_rows(rows, width, itemsize=4):
    for t in (2048, 1024, 512, 256, 128, 64, 32, 16, 8):
        if rows % t == 0 and t * width * itemsize <= BLOCK_BYTES:
            return t
    return rows


def _mm(a, b, *, ta=False, tb=False, bias=None, out_dtype=F32, name="mm"):
    if ta:
        K, M = a.shape
    else:
        M, K = a.shape
    if tb:
        N, Kb = b.shape
    else:
        Kb, N = b.shape
    assert K == Kb, (a.shape, b.shape, ta, tb)
    tm = _tile(M, (512, 256, 128))
    tn = _tile(N, (1024, 512, 384, 256, 128))
    tk = _tile(K, (2048, 1024, 512, 256, 128))
    nk = K // tk
    has_bias = bias is not None

    def body(*refs):
        a_ref, b_ref = refs[0], refs[1]
        bias_ref = refs[2] if has_bias else None
        o_ref = refs[3 if has_bias else 2]
        part = _dot(a_ref[...].astype(BF16), b_ref[...].astype(BF16), 0 if ta else 1, 1 if tb else 0)

        def finish(r):
            if has_bias:
                r = r + bias_ref[...]
            o_ref[...] = r.astype(out_dtype)

        if nk == 1:
            finish(part)
            return
        acc_ref = refs[-1]
        k = pl.program_id(2)

        @pl.when(k == 0)
        def _():
            acc_ref[...] = part

        @pl.when(k != 0)
        def _():
            acc_ref[...] += part

        @pl.when(k == nk - 1)
        def _():
            finish(acc_ref[...])

    a_spec = pl.BlockSpec((tk, tm), lambda i, j, k: (k, i)) if ta else pl.BlockSpec((tm, tk), lambda i, j, k: (i, k))
    b_spec = pl.BlockSpec((tn, tk), lambda i, j, k: (j, k)) if tb else pl.BlockSpec((tk, tn), lambda i, j, k: (k, j))
    in_specs = [a_spec, b_spec]
    args = [a, b]
    if has_bias:
        in_specs.append(pl.BlockSpec((1, tn), lambda i, j, k: (0, j)))
        args.append(bias)
    return pl.pallas_call(
        body, name=name, grid=(M // tm, N // tn, nk),
        in_specs=in_specs, out_specs=pl.BlockSpec((tm, tn), lambda i, j, k: (i, j)),
        out_shape=jax.ShapeDtypeStruct((M, N), out_dtype),
        scratch_shapes=[pltpu.VMEM((tm, tn), F32)] if nk > 1 else [],
        compiler_params=_params(("parallel", "parallel", "arbitrary")),
    )(*args)


@jax.custom_vjp
def mm(a, b):
    return _mm(a, b, name="mm_fwd")


def _mm_f(a, b):
    return _mm(a, b, name="mm_fwd"), (a, b)


def _mm_b(res, g):
    a, b = res
    return _mm(g, b, tb=True, out_dtype=a.dtype, name="mm_da"), _mm(a, g, ta=True, out_dtype=b.dtype, name="mm_db")


mm.defvjp(_mm_f, _mm_b)


@jax.custom_vjp
def mm16(a, b):
    return _mm(a, b, out_dtype=BF16, name="mm16_fwd")


def _mm16_f(a, b):
    return mm16(a, b), (a, b)


mm16.defvjp(_mm16_f, _mm_b)


def _rows(body, name, tiled, full, tiled_out, acc_out, tr=None):
    rows = tiled[0].shape[0]
    if tr is None:
        width = max([a.shape[1] for a in tiled] + [s.shape[1] for s in tiled_out])
        tr = _pick_rows(rows, width)
    in_specs = [pl.BlockSpec((tr, a.shape[1]), lambda i: (i, 0)) for a in tiled]
    in_specs += [pl.BlockSpec(a.shape, lambda i: (0, 0)) for a in full]
    out_specs = [pl.BlockSpec((tr, s.shape[1]), lambda i: (i, 0)) for s in tiled_out]
    out_specs += [pl.BlockSpec(s.shape, lambda i: (0, 0)) for s in acc_out]

    def wrapped(*refs):
        body(pl.program_id(0), *refs)

    outs = pl.pallas_call(
        wrapped, name=name, grid=(rows // tr,), in_specs=in_specs, out_specs=out_specs,
        out_shape=list(tiled_out) + list(acc_out),
        compiler_params=_params(("arbitrary",)),
    )(*tiled, *full)
    return outs


def _sds(shape, dtype=F32):
    return jax.ShapeDtypeStruct(tuple(shape), dtype)


def _acc(step, ref, val):
    @pl.when(step == 0)
    def _():
        ref[...] = val

    @pl.when(step != 0)
    def _():
        ref[...] += val


def _colsum(v):
    return jnp.sum(v, axis=0, keepdims=True)


def _rstd(x):
    return lax.rsqrt(jnp.mean(x * x, axis=-1, keepdims=True) + EPS)


def _norm_grid(x, g):
    rows, w = x.shape[0], g.shape[1]
    tr = _pick_rows(rows, w)
    blk = pl.BlockSpec((tr, w), lambda i, j: (i, j))
    gblk = pl.BlockSpec((1, w), lambda i, j: (0, 0))
    return (rows // tr, x.shape[1] // w), blk, gblk


@jax.custom_vjp
def rmsnorm(x, g):
    def body(x_ref, g_ref, o_ref):
        x = x_ref[...]
        o_ref[...] = x * _rstd(x) * g_ref[...]

    grid, blk, gblk = _norm_grid(x, g)
    return pl.pallas_call(body, name="rmsnorm_fwd", grid=grid, in_specs=[blk, gblk], out_specs=blk,
                          out_shape=_sds(x.shape), compiler_params=_params(("parallel", "parallel")))(x, g)


def _rmsnorm_f(x, g):
    return rmsnorm(x, g), (x, g)


def _rmsnorm_b(res, dy):
    x, g = res

    def body(x_ref, dy_ref, g_ref, dx_ref, dg_ref):
        x = x_ref[...]
        dy = dy_ref[...]
        r = _rstd(x)
        xh = x * r
        dxh = dy * g_ref[...]
        dx_ref[...] = r * (dxh - xh * jnp.mean(dxh * xh, axis=-1, keepdims=True))
        first = jnp.logical_and(pl.program_id(0) == 0, pl.program_id(1) == 0)
        _acc(jnp.where(first, 0, 1), dg_ref, _colsum(dy * xh))

    grid, blk, gblk = _norm_grid(x, g)
    dx, dg = pl.pallas_call(body, name="rmsnorm_bwd", grid=grid, in_specs=[blk, blk, gblk], out_specs=[blk, gblk],
                            out_shape=[_sds(x.shape), _sds(g.shape)],
                            compiler_params=_params(("arbitrary", "arbitrary")))(x, dy, g)
    return dx, dg


rmsnorm.defvjp(_rmsnorm_f, _rmsnorm_b)


def _modulate(x, g, scale, shift):
    def body(i, x_ref, g_ref, sc_ref, sh_ref, o_ref):
        x = x_ref[...]
        xn = x * _rstd(x) * g_ref[...]
        o_ref[...] = (xn * (1.0 + sc_ref[...]) + sh_ref[...]).astype(BF16)
    return _rows(body, "modulate_fwd", [x], [g, scale, shift], [_sds(x.shape, BF16)], [])[0]


def _modulate_bwd(x, g, scale, shift, dh):
    def body(i, x_ref, dh_ref, g_ref, sc_ref, dx_ref, dg_ref, dsc_ref, dsh_ref):
        x = x_ref[...]
        dh = dh_ref[...]
        gv = g_ref[...]
        r = _rstd(x)
        xh = x * r
        dxn = dh * (1.0 + sc_ref[...])
        dxh = dxn * gv
        dx_ref[...] = r * (dxh - xh * jnp.mean(dxh * xh, axis=-1, keepdims=True))
        _acc(i, dg_ref, _colsum(dxn * xh))
        _acc(i, dsc_ref, _colsum(dh * (xh * gv)))
        _acc(i, dsh_ref, _colsum(dh))

    v = _sds(g.shape)
    return _rows(body, "modulate_bwd", [x, dh], [g, scale], [_sds(x.shape)], [v, v, v])


@jax.custom_vjp
def mod_mm(x, g, scale, shift, wt):
    return _mm(_modulate(x, g, scale, shift), wt, tb=True, name="mm_in")


def _mod_mm_f(x, g, scale, shift, wt):
    h = _modulate(x, g, scale, shift)
    return _mm(h, wt, tb=True, name="mm_in"), (x, g, scale, shift, wt, h)


def _mod_mm_b(res, dproj):
    x, g, scale, shift, wt, h = res
    dproj = dproj.astype(BF16)
    dh = _mm(dproj, wt, name="mm_in_dh")
    dwt = _mm(dproj, h, ta=True, out_dtype=wt.dtype, name="mm_in_dw")
    dx, dg, dsc, dsh = _modulate_bwd(x, g, scale, shift, dh)
    return dx, dg, dsc, dsh, dwt


mod_mm.defvjp(_mod_mm_f, _mod_mm_b)


def _sigmoid(z):
    return 1.0 / (1.0 + jnp.exp(-z))


def _gate_mul(o, z):
    def body(i, o_ref, z_ref, y_ref):
        z = z_ref[...]
        y_ref[...] = (o_ref[...] * (z * _sigmoid(z))).astype(BF16)
    return _rows(body, "gate_mul_fwd", [o, z], [], [_sds(o.shape, BF16)], [])[0]


def _gate_mul_bwd(o, z, dy):
    def body(i, o_ref, z_ref, dy_ref, do_ref, dz_ref):
        z = z_ref[...]
        dy = dy_ref[...]
        s = _sigmoid(z)
        do_ref[...] = dy * (z * s)
        dz_ref[...] = dy * o_ref[...] * (s * (1.0 + z * (1.0 - s)))
    return _rows(body, "gate_mul_bwd", [o, z, dy], [], [_sds(o.shape), _sds(o.shape)], [])


def _residual(x, u, gate):
    def body(i, x_ref, u_ref, g_ref, o_ref):
        o_ref[...] = x_ref[...] + g_ref[...] * u_ref[...]
    return _rows(body, "residual_fwd", [x, u], [gate], [_sds(x.shape)], [])[0]


def _residual_bwd(d, u, gate):
    def body(i, d_ref, u_ref, g_ref, du_ref, dg_ref):
        d = d_ref[...]
        du_ref[...] = (g_ref[...] * d).astype(BF16)
        _acc(i, dg_ref, _colsum(d * u_ref[...]))

    return _rows(body, "residual_bwd", [d, u], [gate], [_sds(u.shape, BF16)], [_sds(gate.shape)])


@jax.custom_vjp
def out_block(o, z, w, x, gate):
    return _residual(x, _mm(_gate_mul(o, z), w, name="mm_out"), gate)


def _out_block_f(o, z, w, x, gate):
    y = _gate_mul(o, z)
    u = _mm(y, w, name="mm_out")
    return _residual(x, u, gate), (o, z, w, y, u, gate)


def _out_block_b(res, d):
    o, z, w, y, u, gate = res
    du, dgate = _residual_bwd(d, u, gate)
    dy = _mm(du, w, tb=True, name="mm_out_dy")
    dw = _mm(y, du, ta=True, out_dtype=w.dtype, name="mm_out_dw")
    do, dz = _gate_mul_bwd(o, z, dy)
    return do, dz, dw, d, dgate


out_block.defvjp(_out_block_f, _out_block_b)


@jax.custom_vjp
def gate_act(u, b):
    def body(i, u_ref, b_ref, o_ref):
        t = u_ref[...] + b_ref[...]
        o_ref[...] = (jnp.minimum(t, 0.0) - jnp.log(1.0 + jnp.exp(-jnp.abs(t)))) / GLA_TEMP
    return _rows(body, "gate_act_fwd", [u], [b], [_sds(u.shape)], [])[0]


def _gate_act_f(u, b):
    return gate_act(u, b), (u, b)


def _gate_act_b(res, d):
    u, b = res

    def body(i, u_ref, d_ref, b_ref, du_ref, db_ref):
        t = u_ref[...] + b_ref[...]
        du = d_ref[...] * _sigmoid(-t) / GLA_TEMP
        du_ref[...] = du
        _acc(i, db_ref, _colsum(du))

    du, db = _rows(body, "gate_act_bwd", [u, d], [b], [_sds(u.shape)], [_sds(b.shape)])
    return du, db


gate_act.defvjp(_gate_act_f, _gate_act_b)


@jax.custom_vjp
def fma(a, b, c, d):
    def body(i, a_ref, b_ref, c_ref, d_ref, o_ref):
        o_ref[...] = a_ref[...] * b_ref[...] + c_ref[...] * d_ref[...]
    return _rows(body, "fma_fwd", [a, b, c, d], [], [_sds(a.shape)], [])[0]


def _fma_f(a, b, c, d):
    return fma(a, b, c, d), (b, d)


def _fma_b(res, g):
    b, d = res

    def body(i, g_ref, b_ref, d_ref, da_ref, dc_ref):
        g = g_ref[...]
        da_ref[...] = g * b_ref[...]
        dc_ref[...] = g * d_ref[...]

    da, dc = _rows(body, "fma_bwd", [g, b, d], [], [_sds(g.shape), _sds(g.shape)], [])
    return da, jnp.zeros_like(b), dc, jnp.zeros_like(d)


fma.defvjp(_fma_f, _fma_b)


def _silu_rows(c):
    def body(i, c_ref, o_ref):
        v = c_ref[...]
        o_ref[...] = v * _sigmoid(v)
    return _rows(body, "silu", [c], [], [_sds(c.shape)], [])[0]


@jax.custom_vjp
def loss_op(y, t):
    return _loss_fwd(y, t)[0]


def _loss_fwd(y, t):
    inv = 1.0 / y.shape[1]

    def body(i, y_ref, t_ref, d_ref, l_ref):
        e = y_ref[...] - t_ref[...]
        d_ref[...] = e * inv
        _acc(i, l_ref, jnp.sum(_colsum(e * e), axis=1, keepdims=True) * (0.5 * inv))

    d, l = _rows(body, "loss_fwd", [y, t], [], [_sds(y.shape)], [_sds((1, 1))])
    return l, d


def _loss_f(y, t):
    l, d = _loss_fwd(y, t)
    return l, d


def _loss_b(d, g):
    return d * g, jnp.zeros_like(d)


loss_op.defvjp(_loss_f, _loss_b)


def _conv_terms(cc, cx, rows, n):
    u = cc * cx
    up = jnp.where(rows == 0, 0.0, pltpu.roll(u, 1, 0))
    un = jnp.where(rows == n - 1, 0.0, pltpu.roll(u, n - 1, 0))
    return u, up, un


CONV_COLS = 128


def _conv_specs(s, n_in):
    blk = pl.BlockSpec((s, CONV_COLS), lambda j: (0, j))
    wblk = pl.BlockSpec((8, CONV_COLS), lambda j: (0, j))
    return [blk] * n_in + [wblk], blk, wblk


@jax.custom_vjp
def conv_op(cb, cc, cx, w):
    s, ch = cb.shape

    def body(cb_ref, cc_ref, cx_ref, w_ref, o_ref):
        rows = lax.broadcasted_iota(jnp.int32, (s, CONV_COLS), 0)
        u, up, un = _conv_terms(cc_ref[...], cx_ref[...], rows, s)
        conv = up * w_ref[0:1, :] + u * w_ref[1:2, :] + un * w_ref[2:3, :]
        o_ref[...] = cb_ref[...] * conv

    in_specs, blk, _ = _conv_specs(s, 3)
    return pl.pallas_call(
        body, name="conv_fwd", grid=(ch // CONV_COLS,), in_specs=in_specs, out_specs=blk,
        out_shape=_sds(cb.shape), compiler_params=_params(("parallel",)),
    )(cb, cc, cx, w)


def _conv_f(cb, cc, cx, w):
    return conv_op(cb, cc, cx, w), (cb, cc, cx, w)


def _conv_b(res, d):
    cb, cc, cx, w = res
    s, ch = cb.shape

    def body(cb_ref, cc_ref, cx_ref, d_ref, w_ref, dcb_ref, dcc_ref, dcx_ref, dw_ref):
        rows = lax.broadcasted_iota(jnp.int32, (s, CONV_COLS), 0)
        cc_v = cc_ref[...]
        cx_v = cx_ref[...]
        u, up, un = _conv_terms(cc_v, cx_v, rows, s)
        w0, w1, w2 = w_ref[0:1, :], w_ref[1:2, :], w_ref[2:3, :]
        dv = d_ref[...]
        dcb_ref[...] = dv * (up * w0 + u * w1 + un * w2)
        dconv = dv * cb_ref[...]
        d_next = jnp.where(rows == s - 1, 0.0, pltpu.roll(dconv, s - 1, 0))
        d_prev = jnp.where(rows == 0, 0.0, pltpu.roll(dconv, 1, 0))
        du = w0 * d_next + w1 * dconv + w2 * d_prev
        dcc_ref[...] = du * cx_v
        dcx_ref[...] = du * cc_v
        dw_ref[...] = jnp.zeros_like(dw_ref)
        dw_ref[0:1, :] = _colsum(dconv * up)
        dw_ref[1:2, :] = _colsum(dconv * u)
        dw_ref[2:3, :] = _colsum(dconv * un)

    in_specs, blk, wblk = _conv_specs(s, 4)
    v = _sds(cb.shape)
    return tuple(pl.pallas_call(
        body, name="conv_bwd", grid=(ch // CONV_COLS,), in_specs=in_specs, out_specs=[blk, blk, blk, wblk],
        out_shape=[v, v, v, _sds(w.shape)], compiler_params=_params(("parallel",)),
    )(cb, cc, cx, d, w))


conv_op.defvjp(_conv_f, _conv_b)


def _gla_masks(rev):
    c = GLA_CHUNK
    row = lax.broadcasted_iota(jnp.int32, (c, c), 0)
    col = lax.broadcasted_iota(jnp.int32, (c, c), 1)
    mask = (row < col) if rev else (row >= col)
    return rev, mask


def _chunk_cumsum(g, rev):
    c = g.shape[0]
    row = lax.broadcasted_iota(jnp.int32, g.shape, 0)
    b = g
    s = 1
    while s < c:
        if rev:
            b = b + jnp.where(row < c - s, pltpu.roll(b, c - s, 0), 0.0)
        else:
            b = b + jnp.where(row >= s, pltpu.roll(b, s, 0), 0.0)
        s *= 2
    return b


GLA_UNROLL = 4


def _gla_rows(n):
    return pl.ds(pl.multiple_of(n * GLA_CHUNK, GLA_CHUNK), GLA_CHUNK)


def _gla_scan(s_ref, bt_ref, st_ref, n_chunks, descending):
    st_ref[...] = jnp.zeros_like(st_ref)

    def step(i, carry):
        n = (n_chunks - 1 - i) if descending else i
        own = s_ref[n]
        st = st_ref[...]
        s_ref[n] = st
        st_ref[...] = st * jnp.exp(bt_ref[n]) + own
        return carry

    lax.fori_loop(0, n_chunks, step, 0)


GLA_PAIR = 2


def _gla_specs(s):
    dk, dv = GLA_DK, GLA_DV
    n_pairs = GLA_HEADS // GLA_PAIR
    blk_k = pl.BlockSpec((s, GLA_PAIR * dk), lambda p: (0, p))
    blk_gb = pl.BlockSpec((s, GLA_PAIR * dk), lambda p: (0, n_pairs + p))
    blk_v = pl.BlockSpec((s, GLA_PAIR * dv), lambda p: (0, p))
    return n_pairs, blk_k, blk_gb, blk_v


def _head_lanes(hh):
    lane = lax.broadcasted_iota(jnp.int32, (1, GLA_PAIR * GLA_DK), 1)
    return jnp.logical_and(lane >= hh * GLA_DK, lane < (hh + 1) * GLA_DK)


def _gla_fwd(q, k, v, la):
    s = q.shape[0]
    dk, dv = GLA_DK, GLA_DV
    pw = GLA_PAIR * dk
    n_chunks = s // GLA_CHUNK
    scale = GLA_DK ** -0.5

    def body(q_ref, k_ref, v_ref, gf_ref, gb_ref, o_ref, sf_ref, sb_ref, bf_ref, bb_ref, btf_ref, btb_ref, st_ref):
        masks = [_gla_masks(rev) for rev in (False, True)]
        dirs = ((False, gf_ref, sf_ref, bf_ref, btf_ref), (True, gb_ref, sb_ref, bb_ref, btb_ref))

        def decays(n, carry):
            rows = _gla_rows(n)
            for rev, g_ref, _, b_ref, bt_ref in dirs:
                g = g_ref[rows, :]
                b_ref[rows, :] = _chunk_cumsum(g, rev)
                bt_ref[n] = _colsum(g)
            return carry

        lax.fori_loop(0, n_chunks, decays, 0, unroll=GLA_UNROLL)
        for hh in range(GLA_PAIR):
            m = _head_lanes(hh)
            vl = slice(hh * dv, (hh + 1) * dv)

            def prepare(n, carry, m=m, vl=vl):
                rows = _gla_rows(n)
                kk = k_ref[rows, :]
                vb = v_ref[rows, vl].astype(BF16)
                for rev, _, s_ref, b_ref, bt_ref in dirs:
                    ke = jnp.where(m, kk * jnp.exp(bt_ref[n] - b_ref[rows, :]), 0.0).astype(BF16)
                    s_ref[n] = _dot(vb, ke, 0, 0)
                return carry

            lax.fori_loop(0, n_chunks, prepare, 0, unroll=GLA_UNROLL)
            for rev, _, s_ref, _, bt_ref in dirs:
                _gla_scan(s_ref, bt_ref, st_ref, n_chunks, descending=rev)

            def emit(n, carry, m=m, vl=vl):
                rows = _gla_rows(n)
                qs = q_ref[rows, :] * scale
                kk = k_ref[rows, :]
                vb = v_ref[rows, vl].astype(BF16)
                o = None
                for (rev, _, s_ref, b_ref, _), (_, mask) in zip(dirs, masks):
                    b = b_ref[rows, :]
                    qd = jnp.where(m, qs * jnp.exp(b), 0.0).astype(BF16)
                    ki = jnp.where(m, kk * jnp.exp(-b), 0.0).astype(BF16)
                    a = jnp.where(mask, _dot(qd, ki, 1, 1), 0.0).astype(BF16)
                    od = _dot(a, vb, 1, 0) + _dot(qd, s_ref[n].astype(BF16), 1, 1)
                    o = od if o is None else o + od
                o_ref[rows, vl] = o
                return carry

            lax.fori_loop(0, n_chunks, emit, 0, unroll=GLA_UNROLL)

    n_pairs, blk_k, blk_gb, blk_v = _gla_specs(s)
    state = pltpu.VMEM((n_chunks, dv, pw), F32)
    scratch = [state, state, pltpu.VMEM((s, pw), F32), pltpu.VMEM((s, pw), F32), pltpu.VMEM((n_chunks, 1, pw), F32),
               pltpu.VMEM((n_chunks, 1, pw), F32), pltpu.VMEM((dv, pw), F32)]
    return pl.pallas_call(
        body, name="gla_fwd", grid=(n_pairs,), in_specs=[blk_k, blk_k, blk_v, blk_k, blk_gb],
        out_specs=blk_v, out_shape=_sds(v.shape), scratch_shapes=scratch,
        compiler_params=_params(("parallel",)),
    )(q, k, v, la, la)


def _gla_bwd(q, k, v, la, do):
    s = q.shape[0]
    dk, dv = GLA_DK, GLA_DV
    pw = GLA_PAIR * dk
    c = GLA_CHUNK
    n_chunks = s // c
    scale = GLA_DK ** -0.5

    def body(q_ref, k_ref, v_ref, gf_ref, gb_ref, do_ref, dq_ref, dk_ref, dv_ref, dgf_ref, dgb_ref,
             sf_ref, sb_ref, bf_ref, bb_ref, btf_ref, btb_ref, dsf_ref, dsb_ref, st_ref):
        masks = [_gla_masks(rev) for rev in (False, True)]
        rowc = lax.broadcasted_iota(jnp.int32, (c, pw), 0)
        dirs = ((False, gf_ref, sf_ref, bf_ref, btf_ref, dsf_ref, dgf_ref),
                (True, gb_ref, sb_ref, bb_ref, btb_ref, dsb_ref, dgb_ref))

        def decays(n, carry):
            rows = _gla_rows(n)
            for rev, g_ref, _, b_ref, bt_ref, _, _ in dirs:
                g = g_ref[rows, :]
                b_ref[rows, :] = _chunk_cumsum(g, rev)
                bt_ref[n] = _colsum(g)
            return carry

        lax.fori_loop(0, n_chunks, decays, 0, unroll=GLA_UNROLL)
        for hh in range(GLA_PAIR):
            m = _head_lanes(hh)
            vl = slice(hh * dv, (hh + 1) * dv)

            def prepare(n, carry, m=m, vl=vl):
                rows = _gla_rows(n)
                qs = q_ref[rows, :] * scale
                kk = k_ref[rows, :]
                vb = v_ref[rows, vl].astype(BF16)
                do_b = do_ref[rows, vl].astype(BF16)
                for rev, _, s_ref, b_ref, bt_ref, ds_ref, _ in dirs:
                    b = b_ref[rows, :]
                    ke = jnp.where(m, kk * jnp.exp(bt_ref[n] - b), 0.0).astype(BF16)
                    qd = jnp.where(m, qs * jnp.exp(b), 0.0).astype(BF16)
                    s_ref[n] = _dot(vb, ke, 0, 0)
                    ds_ref[n] = _dot(do_b, qd, 0, 0)
                return carry

            lax.fori_loop(0, n_chunks, prepare, 0, unroll=GLA_UNROLL)
            for rev, _, s_ref, _, bt_ref, ds_ref, _ in dirs:
                _gla_scan(s_ref, bt_ref, st_ref, n_chunks, descending=rev)
                _gla_scan(ds_ref, bt_ref, st_ref, n_chunks, descending=not rev)

            def emit(n, carry, m=m, vl=vl, first=(hh == 0)):
                rows = _gla_rows(n)
                qs = q_ref[rows, :] * scale
                kk = k_ref[rows, :]
                vb = v_ref[rows, vl].astype(BF16)
                do_b = do_ref[rows, vl].astype(BF16)
                dq = dkk = dvv = None
                for (rev, _, s_ref, b_ref, bt_ref, ds_ref, dg_ref), (_, mask) in zip(dirs, masks):
                    b = b_ref[rows, :]
                    bt = bt_ref[n]
                    eb = jnp.where(m, jnp.exp(b), 0.0)
                    enb = jnp.where(m, jnp.exp(-b), 0.0)
                    etb = jnp.where(m, jnp.exp(bt - b), 0.0)
                    ebt = jnp.exp(bt)
                    qd = qs * eb
                    ki = kk * enb
                    ke = kk * etb
                    qd_b, ki_b, ke_b = qd.astype(BF16), ki.astype(BF16), ke.astype(BF16)
                    st = s_ref[n]
                    dst = ds_ref[n]
                    dst_b = dst.astype(BF16)
                    a = jnp.where(mask, _dot(qd_b, ki_b, 1, 1), 0.0).astype(BF16)
                    da = jnp.where(mask, _dot(do_b, vb, 1, 1), 0.0).astype(BF16)
                    dv_d = _dot(a, do_b, 0, 0) + _dot(ke_b, dst_b, 1, 1)
                    dqd = _dot(da, ki_b, 1, 0) + _dot(do_b, st.astype(BF16), 1, 0)
                    dki = _dot(da, qd_b, 0, 0)
                    dke = _dot(vb, dst_b, 1, 0)
                    dbt = _colsum(st * dst) * ebt + _colsum(dke * ke)
                    db = dqd * qd - dki * ki - dke * ke
                    db = db + jnp.where(rowc == (0 if rev else c - 1), dbt, 0.0)
                    dg = _chunk_cumsum(db, not rev)
                    if first:
                        dg_ref[rows, :] = dg
                    else:
                        dg_ref[rows, :] += dg
                    dq_d = dqd * eb * scale
                    dk_d = dki * enb + dke * etb
                    dq = dq_d if dq is None else dq + dq_d
                    dkk = dk_d if dkk is None else dkk + dk_d
                    dvv = dv_d if dvv is None else dvv + dv_d
                if first:
                    dq_ref[rows, :] = dq
                    dk_ref[rows, :] = dkk
                else:
                    dq_ref[rows, :] += dq
                    dk_ref[rows, :] += dkk
                dv_ref[rows, vl] = dvv
                return carry

            lax.fori_loop(0, n_chunks, emit, 0, unroll=2)

    n_pairs, blk_k, blk_gb, blk_v = _gla_specs(s)
    vk, vv = _sds(q.shape), _sds(v.shape)
    state = pltpu.VMEM((n_chunks, dv, pw), F32)
    scratch = [state, state, pltpu.VMEM((s, pw), F32), pltpu.VMEM((s, pw), F32), pltpu.VMEM((n_chunks, 1, pw), F32),
               pltpu.VMEM((n_chunks, 1, pw), F32), state, state, pltpu.VMEM((dv, pw), F32)]
    return pl.pallas_call(
        body, name="gla_bwd", grid=(n_pairs,), in_specs=[blk_k, blk_k, blk_v, blk_k, blk_gb, blk_v],
        out_specs=[blk_k, blk_k, blk_v, blk_k, blk_k], out_shape=[vk, vk, vv, vk, vk],
        scratch_shapes=scratch, compiler_params=_params(("parallel",)),
    )(q, k, v, la, la, do)


@jax.custom_vjp
def gla(q, k, v, la):
    return _gla_fwd(q, k, v, la)


def _gla_f(q, k, v, la):
    return _gla_fwd(q, k, v, la), (q, k, v, la)


def _gla_b(res, do):
    dq, dk, dv, dgf, dgb = _gla_bwd(*res, do)
    return dq, dk, dv, jnp.concatenate([dgf, dgb], axis=1)


gla.defvjp(_gla_f, _gla_b)


ATTN_TQ = 256
HEAD_LANES = 128


def _attn_blocks(s, tq):
    per_q = pl.BlockSpec((tq, HEAD_LANES), lambda h, j: (j, h))
    k_nope = pl.BlockSpec((s, HEAD_LANES), lambda h, j: (0, 2 * h))
    v_blk = pl.BlockSpec((s, HEAD_LANES), lambda h, j: (0, 2 * h + 1))
    k_rope = pl.BlockSpec((s, HEAD_LANES), lambda h, j: (0, 0))
    lse = pl.BlockSpec((1, tq, 1), lambda h, j: (h, j, 0))
    return per_q, k_nope, v_blk, k_rope, lse


def _attn_fwd(qn, qr, kv, kr):
    s = qn.shape[0]
    tq = min(ATTN_TQ, s)
    scale = (MLA_NOPE + MLA_ROPE) ** -0.5

    def body(qn_ref, qr_ref, kn_ref, v_ref, kr_ref, o_ref, lse_ref):
        q = jnp.concatenate([qn_ref[...], qr_ref[...]], axis=1)
        k = jnp.concatenate([kn_ref[...], kr_ref[...]], axis=1)
        sc = _dot(q, k, 1, 1) * scale
        m = jnp.max(sc, axis=-1, keepdims=True)
        p = jnp.exp(sc - m)
        l = jnp.sum(p, axis=-1, keepdims=True)
        p = p * (1.0 / l)
        o_ref[...] = _dot(p.astype(BF16), v_ref[...], 1, 0)
        lse_ref[0] = m + jnp.log(l)

    per_q, k_nope, v_blk, k_rope, lse = _attn_blocks(s, tq)
    return pl.pallas_call(
        body, name="attn_fwd", grid=(MLA_HEADS, s // tq), in_specs=[per_q, per_q, k_nope, v_blk, k_rope],
        out_specs=[per_q, lse], out_shape=[_sds(qn.shape), _sds((MLA_HEADS, s, 1))],
        compiler_params=_params(("parallel", "parallel")),
    )(qn, qr, kv, kv, kr)


def _attn_bwd(qn, qr, kv, kr, o, lse, do):
    s = qn.shape[0]
    tq = min(ATTN_TQ, s)
    n_q = s // tq
    scale = (MLA_NOPE + MLA_ROPE) ** -0.5

    def body(qn_ref, qr_ref, kn_ref, v_ref, kr_ref, o_ref, lse_ref, do_ref, dqn_ref, dqr_ref, dkv_ref, dkr_ref,
             dk_acc, dv_acc, dkr_acc):
        h, j = pl.program_id(0), pl.program_id(1)
        q = jnp.concatenate([qn_ref[...], qr_ref[...]], axis=1)
        k = jnp.concatenate([kn_ref[...], kr_ref[...]], axis=1)
        do = do_ref[...]
        do_b = do.astype(BF16)
        p = jnp.exp(_dot(q, k, 1, 1) * scale - lse_ref[0])
        dp = _dot(do_b, v_ref[...], 1, 1)
        delta = jnp.sum(do * o_ref[...], axis=-1, keepdims=True)
        ds = (p * (dp - delta) * scale).astype(BF16)
        dq = _dot(ds, k, 1, 0)
        dqn_ref[...] = dq[:, :HEAD_LANES].astype(BF16)
        dqr_ref[...] = dq[:, HEAD_LANES:].astype(BF16)
        dk = _dot(ds, q, 0, 0)
        _acc(j, dk_acc, dk[:, :HEAD_LANES])
        _acc(j, dv_acc, _dot(p.astype(BF16), do_b, 0, 0))
        _acc(jnp.where(jnp.logical_and(h == 0, j == 0), 0, 1), dkr_acc, dk[:, HEAD_LANES:])

        @pl.when(j == n_q - 1)
        def _():
            dkv_ref[:, 0:HEAD_LANES] = dk_acc[...].astype(BF16)
            dkv_ref[:, HEAD_LANES:2 * HEAD_LANES] = dv_acc[...].astype(BF16)

        @pl.when(jnp.logical_and(h == MLA_HEADS - 1, j == n_q - 1))
        def _():
            dkr_ref[...] = dkr_acc[...].astype(BF16)

    per_q, k_nope, v_blk, k_rope, lse_blk = _attn_blocks(s, tq)
    dkv_blk = pl.BlockSpec((s, 2 * HEAD_LANES), lambda h, j: (0, h))
    acc = pltpu.VMEM((s, HEAD_LANES), F32)
    return pl.pallas_call(
        body, name="attn_bwd", grid=(MLA_HEADS, n_q),
        in_specs=[per_q, per_q, k_nope, v_blk, k_rope, per_q, lse_blk, per_q],
        out_specs=[per_q, per_q, dkv_blk, k_rope],
        out_shape=[_sds(qn.shape, BF16), _sds(qr.shape, BF16), _sds(kv.shape, BF16), _sds(kr.shape, BF16)],
        scratch_shapes=[acc, acc, acc], compiler_params=_params(("arbitrary", "arbitrary")),
    )(qn, qr, kv, kv, kr, o, lse, do)


@jax.custom_vjp
def attn(qn, qr, kv, kr):
    return _attn_fwd(qn, qr, kv, kr)[0]


def _attn_f(qn, qr, kv, kr):
    o, lse = _attn_fwd(qn, qr, kv, kr)
    return o, (qn, qr, kv, kr, o, lse)


def _attn_b(res, do):
    return tuple(_attn_bwd(*res, do))


attn.defvjp(_attn_f, _attn_b)


@jax.custom_vjp
def split_proj(proj):
    out, at = [], 0
    for _, _, _, wp in PROJ_SEGS:
        out.append(proj[:, at:at + wp])
        at += wp
    return tuple(out)


def _split_f(proj):
    return split_proj(proj), None


def _split_b(_, gs):
    return (jnp.concatenate(gs, axis=1),)


split_proj.defvjp(_split_f, _split_b)


def _tile2d(rows, width, limit=BLOCK_BYTES):
    fits = [t for t in range(16, rows + 1, 16) if rows % t == 0 and t * width * 4 <= limit]
    if fits and (fits[-1] >= 64 or fits[-1] == rows):
        return fits[-1], width
    if rows * width * 4 <= limit:
        return rows, width
    cols = [t for t in range(128, width + 1, 128) if width % t == 0 and rows * t * 4 <= limit]
    return (rows, cols[-1]) if cols else (rows, width)


def _add_pair(stacked, theirs, c_idx):
    g, r, w = theirs.shape
    tr, tc = _tile2d(r, w)

    def body(c_ref, a_ref, b_ref, o_ref):
        o_ref[0] = (a_ref[0, 0].astype(F32) + b_ref[0].astype(F32)).astype(BF16)

    blk = pl.BlockSpec((1, tr, tc), lambda k, i, j, c: (k, i, j))
    spec = pltpu.PrefetchScalarGridSpec(
        num_scalar_prefetch=1, grid=(g, r // tr, w // tc),
        in_specs=[pl.BlockSpec((1, 1, tr, tc), lambda k, i, j, c: (c[0], k, i, j)), blk], out_specs=blk)
    return pl.pallas_call(body, name="add_pair", grid_spec=spec, out_shape=_sds(theirs.shape, BF16),
                          compiler_params=_params(("parallel", "parallel", "parallel")))(c_idx, stacked, theirs)


def _add_chips(pair, landed, chip_idx):
    _, r, w = pair.shape
    tr, tc = _tile2d(r, w)

    def body(c_ref, p_ref, l0_ref, l1_ref, l2_ref, o_ref):
        o_ref[...] = ((p_ref[0].astype(F32) + l0_ref[0].astype(F32)) + l1_ref[0].astype(F32)) + l2_ref[0].astype(F32)

    specs = [pl.BlockSpec((1, tr, tc), lambda i, j, c: (c[0], i, j))]
    specs += [pl.BlockSpec((1, tr, tc), functools.partial(lambda i, j, c, k: (k, i, j), k=k)) for k in range(N_CHIPS - 1)]
    spec = pltpu.PrefetchScalarGridSpec(num_scalar_prefetch=1, grid=(r // tr, w // tc), in_specs=specs,
                                        out_specs=pl.BlockSpec((tr, tc), lambda i, j, c: (i, j)))
    return pl.pallas_call(body, name="add_chips", grid_spec=spec, out_shape=_sds((r, w)),
                          compiler_params=_params(("parallel", "parallel")))(chip_idx, pair, landed, landed, landed)


def _sum_devices(g):
    n = g.shape[2]

    def body(g_ref, o_ref, done_ref):
        t = g_ref[0]
        for j in range(1, N_DEV):
            t = t + g_ref[j]
        o_ref[...] = t
        done_ref[...] = jnp.zeros_like(done_ref)

    return pl.pallas_call(body, name="sum_devices", out_shape=[_sds((1, n)), _sds((8, 128))],
                          compiler_params=_params())(g)


def _adamw_math(w, gv, m, v):
    c1 = 1.0 - ADAM_B1 ** ADAM_STEP
    c2 = 1.0 - ADAM_B2 ** ADAM_STEP
    mn = ADAM_B1 * m + (1.0 - ADAM_B1) * gv
    vn = ADAM_B2 * v + (1.0 - ADAM_B2) * (gv * gv)
    return -ADAM_LR * ((mn / c1) / (jnp.sqrt(vn / c2) + ADAM_EPS) + ADAM_WD * w), mn, vn


def _adamw(w, g, m, v):
    shp = w.shape
    shp3 = (1, 1, shp[0]) if len(shp) == 1 else (-1,) + tuple(shp[-2:])
    w3, g3, m3, v3 = (t.reshape(shp3) for t in (w, g, m, v))

    def body(w_ref, g_ref, m_ref, v_ref, d_ref, mo_ref, vo_ref):
        d_ref[...], mo_ref[...], vo_ref[...] = _adamw_math(w_ref[...], g_ref[...], m_ref[...], v_ref[...])

    nl, r, wd = w3.shape
    tr, tc = _tile2d(r, wd, BLOCK_BYTES // 2)
    blk = pl.BlockSpec((1, tr, tc), lambda l, i, j: (l, i, j))
    s3 = _sds(w3.shape)
    d, mn, vn = pl.pallas_call(
        body, name="adamw", grid=(nl, r // tr, wd // tc), in_specs=[blk] * 4, out_specs=[blk] * 3,
        out_shape=[s3, s3, s3], compiler_params=_params(("parallel", "parallel", "parallel")),
    )(w3, g3, m3, v3)
    return d.reshape(shp), mn.reshape(shp), vn.reshape(shp)


PIECE_BYTES = 1 << 20


def _place():
    return lax.axis_index("x"), lax.axis_index("y"), lax.axis_index("c")


def _pieces(shape, itemsize):
    if len(shape) >= 3:
        return [(i,) + p for i in range(shape[0]) for p in _pieces(shape[1:], itemsize)]
    rows = shape[0]
    row_bytes = itemsize
    for dsz in shape[1:]:
        row_bytes *= dsz
    k = 1
    while rows % (2 * k) == 0 and (rows // (2 * k)) % 16 == 0 and (rows // k) * row_bytes > PIECE_BYTES:
        k *= 2
    step = rows // k
    return [(pl.ds(j * step, step),) for j in range(k)]


def _split_start(make, src, dst, pieces):
    for p in pieces:
        make(src.at[p], dst.at[p]).start()
    return make(src, dst)


def _comm_call(body, name, arrs, out_shapes, n_remote, n_local):
    return pl.pallas_call(
        body, name=name, in_specs=[ANY] * len(arrs), out_specs=[ANY] * len(out_shapes), out_shape=out_shapes,
        scratch_shapes=[pltpu.SemaphoreType.DMA((n_remote,)), pltpu.SemaphoreType.DMA((n_remote,)),
                        pltpu.SemaphoreType.DMA((n_local,))],
    )(*arrs)


def all_gather8(arrs, name):
    n = len(arrs)
    pieces = [_pieces(a.shape, a.dtype.itemsize) for a in arrs]

    def body(*refs):
        ins, outs = refs[:n], refs[n:2 * n]
        send, recv, _ = refs[2 * n:]
        x, y, c = _place()
        me, sib = (x, y, c), (x, y, 1 - c)
        chips = [(1 - x, y), (x, 1 - y), (1 - x, 1 - y)]

        def slot(p):
            return 4 * p[0] + 2 * p[1] + p[2]

        def maker(t, k, to):
            def make(s, d):
                return pltpu.make_async_remote_copy(src_ref=s, dst_ref=d, send_sem=send.at[7 * t + k],
                                                    recv_sem=recv.at[7 * t + k], device_id=to, device_id_type=MESH)
            return make

        def landing(t, k, block):
            dst = outs[t].at[slot(block)]
            return maker(t, k, me)(dst, dst)

        sent = []
        for t in range(n):
            dst = outs[t].at[slot(me)]
            sent.append(_split_start(maker(t, 0, sib), ins[t], dst, pieces[t]))
            for j, chip in enumerate(chips):
                sent.append(_split_start(maker(t, 1 + j, (*chip, c)), ins[t], dst, pieces[t]))
        for j, chip in enumerate(chips):
            for t in range(n):
                landing(t, 1 + j, (*chip, c)).wait_recv()
                blk = outs[t].at[slot((*chip, c))]
                sent.append(_split_start(maker(t, 4 + j, sib), blk, blk, pieces[t]))
        for t in range(n):
            landing(t, 0, sib).wait_recv()
            for j, chip in enumerate(chips):
                landing(t, 4 + j, (*chip, 1 - c)).wait_recv()
        for cp in sent:
            cp.wait_send()

    outs = [_sds((N_DEV,) + a.shape, a.dtype) for a in arrs]
    got = _comm_call(body, name, arrs, outs, 7 * n, 1)
    x, y, c = _place()
    return [lax.dynamic_update_index_in_dim(g, a, 4 * x + 2 * y + c, 0) for g, a in zip(got, arrs)]


def sibling_send(arrs, name):
    n = len(arrs)
    pieces = [_pieces(a.shape[1:], a.dtype.itemsize) for a in arrs]

    def body(*refs):
        ins, theirs = refs[:n], refs[n:2 * n]
        send, recv, _ = refs[2 * n:]
        x, y, c = _place()
        rem = []
        for t in range(n):
            def make(s, d, t=t):
                return pltpu.make_async_remote_copy(src_ref=s, dst_ref=d, send_sem=send.at[t], recv_sem=recv.at[t],
                                                    device_id=(x, y, 1 - c), device_id_type=MESH)
            rem.append(_split_start(make, ins[t].at[1 - c], theirs[t], pieces[t]))
        for cp in rem:
            cp.wait_recv()
        for cp in rem:
            cp.wait_send()

    outs = [_sds(a.shape[1:], a.dtype) for a in arrs]
    return _comm_call(body, name, arrs, outs, n, 1)


def exchange_chips(arrs, name):
    n = len(arrs)
    pieces = [_pieces(a.shape[1:], a.dtype.itemsize) for a in arrs]

    def body(*refs):
        ins, outs = refs[:n], refs[n:2 * n]
        send, recv, _ = refs[2 * n:]
        x, y, c = _place()
        peers = [(1 - x, y), (x, 1 - y), (1 - x, 1 - y)]
        rem = []
        for t in range(n):
            for j, (px, py) in enumerate(peers):
                def make(s, d, t=t, j=j, px=px, py=py):
                    return pltpu.make_async_remote_copy(
                        src_ref=s, dst_ref=d, send_sem=send.at[3 * t + j], recv_sem=recv.at[3 * t + j],
                        device_id=(px, py, c), device_id_type=MESH)
                rem.append(_split_start(make, ins[t].at[2 * px + py], outs[t].at[j], pieces[t]))
        for cp in rem:
            cp.wait_recv()
        for cp in rem:
            cp.wait_send()

    outs = [_sds((N_CHIPS - 1,) + a.shape[1:], a.dtype) for a in arrs]
    return _comm_call(body, name, arrs, outs, 3 * n, 1)


def sibling_swap(arrs, name):
    n = len(arrs)
    pieces = [_pieces(a.shape, a.dtype.itemsize) for a in arrs]

    def body(*refs):
        ins, outs = refs[:n], refs[n:2 * n]
        send, recv, _ = refs[2 * n:]
        x, y, c = _place()
        rem = []
        for t in range(n):
            def make(s, d, t=t):
                return pltpu.make_async_remote_copy(src_ref=s, dst_ref=d, send_sem=send.at[t], recv_sem=recv.at[t],
                                                    device_id=(x, y, 1 - c), device_id_type=MESH)
            rem.append(_split_start(make, ins[t], outs[t], pieces[t]))
        for cp in rem:
            cp.wait_recv()
        for cp in rem:
            cp.wait_send()

    outs = [_sds(a.shape, a.dtype) for a in arrs]
    return _comm_call(body, name, arrs, outs, n, 1)


def _peer_copies(srcs, lands, send, recv, mode):
    x, y, c = _place()
    my_chip = 2 * x + y
    out = []
    for t in range(len(srcs)):
        for j, (px, py) in enumerate([(1 - x, y), (x, 1 - y), (1 - x, 1 - y)]):
            if mode == "gather":
                s, dst = srcs[t], lands[t].at[c, my_chip]
            else:
                s, dst = srcs[t].at[2 * px + py], lands[t].at[j]
            out.append(pltpu.make_async_remote_copy(
                src_ref=s, dst_ref=dst, send_sem=send.at[3 * t + j], recv_sem=recv.at[3 * t + j],
                device_id=(px, py, c), device_id_type=MESH))
    return out


HBM = pl.BlockSpec(memory_space=pltpu.HBM)
SEM = pl.BlockSpec(memory_space=pltpu.SEMAPHORE)
EFFECT = pltpu.SideEffectType.DATAFLOW_SIDE_EFFECTING


def ici_start(srcs, lands, mode, name):
    n = len(srcs)

    def body(*refs):
        send, recv = refs[2 * n], refs[2 * n + 1]
        for cp in _peer_copies(refs[:n], refs[n:2 * n], send, recv, mode):
            cp.start()
        refs[-1][...] = jnp.zeros_like(refs[-1])

    thru = [pltpu.HBM(a.shape, a.dtype) for a in list(srcs) + list(lands)]
    outs = pl.pallas_call(
        body, name=name, in_specs=[HBM] * (2 * n), out_specs=[SEM, SEM] + [HBM] * (2 * n) + [pl.BlockSpec(memory_space=pltpu.VMEM)],
        out_shape=[pltpu.SemaphoreType.DMA((3 * n,)), pltpu.SemaphoreType.DMA((3 * n,))] + thru + [_sds((8, 128))],
        input_output_aliases={i: 2 + i for i in range(2 * n)},
        compiler_params=pltpu.CompilerParams(has_side_effects=EFFECT),
    )(*[pltpu.with_memory_space_constraint(a, pltpu.HBM) for a in list(srcs) + list(lands)])
    return dict(send=outs[0], recv=outs[1], srcs=outs[2:2 + n], lands=outs[2 + n:2 + 2 * n], token=outs[-1])


def ici_wait(handle, after, mode, name):
    n = len(handle["srcs"])

    def body(*refs):
        send, recv = refs[2 * n], refs[2 * n + 1]
        for cp in _peer_copies(refs[:n], refs[n:2 * n], send, recv, mode):
            cp.wait_send()
            cp.wait_recv()

    arrs = list(handle["srcs"]) + list(handle["lands"])
    outs = pl.pallas_call(
        body, name=name, in_specs=[HBM] * (2 * n) + [SEM, SEM, ANY], out_specs=[HBM] * (2 * n),
        out_shape=[pltpu.HBM(a.shape, a.dtype) for a in arrs], input_output_aliases={i: i for i in range(2 * n)},
        compiler_params=pltpu.CompilerParams(has_side_effects=EFFECT),
    )(*arrs, handle["send"], handle["recv"], after)
    return outs[:n], outs[n:]


def gather_share(blocks, lands, name):
    n = len(blocks)

    def body(*refs):
        own, buf = refs[:n], refs[2 * n:3 * n]
        done, send, recv = refs[3 * n:]
        x, y, c = _place()
        my_chip = 2 * x + y
        chips = [2 * (1 - x) + y, 2 * x + (1 - y), 2 * (1 - x) + (1 - y)]
        sent = []
        for t in range(n):
            def make(s, d, k, t=t):
                return pltpu.make_async_remote_copy(src_ref=s, dst_ref=d, send_sem=send.at[4 * t + k],
                                                    recv_sem=recv.at[4 * t + k], device_id=(x, y, 1 - c),
                                                    device_id_type=MESH)
            cp = make(own[t], buf[t].at[c, my_chip], 0)
            cp.start()
            sent.append(cp)
            for k, pc in enumerate(chips):
                cp = make(buf[t].at[c, pc], buf[t].at[c, pc], 1 + k)
                cp.start()
                sent.append(cp)
        for t in range(n):
            for k in range(4):
                got = buf[t].at[1 - c, k]
                pltpu.make_async_remote_copy(src_ref=got, dst_ref=got, send_sem=send.at[4 * t + k],
                                             recv_sem=recv.at[4 * t + k], device_id=(x, y, 1 - c),
                                             device_id_type=MESH).wait_recv()
        for cp in sent:
            cp.wait_send()
        done[...] = jnp.zeros_like(done)

    outs = pl.pallas_call(
        body, name=name, in_specs=[ANY] * (2 * n), out_specs=[ANY] * n + [pl.BlockSpec(memory_space=pltpu.VMEM)],
        out_shape=[_sds(a.shape, a.dtype) for a in lands] + [_sds((8, 128))],
        input_output_aliases={n + t: t for t in range(n)},
        scratch_shapes=[pltpu.SemaphoreType.DMA((4 * n,)), pltpu.SemaphoreType.DMA((4 * n,))],
    )(*blocks, *lands)
    return outs[:n], outs[n]


@jax.custom_vjp
def _build_w_in(w4):
    full = w4.reshape(-1, w4.shape[-1])
    parts = []
    for _, start, width, wp in PROJ_SEGS:
        if width:
            parts.append(full[start:start + width])
        if wp > width:
            parts.append(jnp.zeros((wp - width, full.shape[1]), full.dtype))
    return jnp.concatenate(parts, axis=0)


def _build_w_in_f(w4):
    return _build_w_in(w4), None


def _build_w_in_b(_, g):
    parts, at = [], 0
    for _, _, width, wp in PROJ_SEGS:
        if width:
            parts.append(g[at:at + width])
        at += wp
    return (jnp.concatenate(parts, axis=0).reshape(N_CHIPS, -1, g.shape[1]),)


_build_w_in.defvjp(_build_w_in_f, _build_w_in_b)


def _split_w_uq(w):
    w3 = w.reshape(w.shape[0], MLA_HEADS, MLA_NOPE + MLA_ROPE)
    return w3[:, :, :MLA_NOPE].reshape(w.shape[0], -1), w3[:, :, MLA_NOPE:].reshape(w.shape[0], -1)


def _swap_halves(t, width):
    t3 = t.reshape(t.shape[0], -1, 2, width // 2)
    return jnp.concatenate([t3[:, :, 1:], t3[:, :, :1]], axis=2).reshape(t.shape)


def _pad_heads(t, width):
    t3 = t.reshape(t.shape[0], -1, width)
    t3 = jnp.pad(t3, ((0, 0), (0, 0), (0, HEAD_LANES - width)))
    return t3.reshape(t.shape[0], -1).astype(BF16)


def _layer(xh, mod, big, small, rope_q, rope_k):
    d = D_MODEL
    shift, scale, gate = mod[None, 0:d], mod[None, d:2 * d], mod[None, 2 * d:3 * d]
    w_al = _build_w_in(big["w_in"])
    proj = mod_mm(xh, small["norm_g"][None], scale, shift, w_al)
    gq, gk, gv, glr, mq, mkv, mkr, cb, cc, cx, _, z = split_proj(proj)

    rk = GLA_RANK
    hk = GLA_HEADS * GLA_DK
    wg = jnp.zeros((128, 2 * hk), F32)
    wg = wg.at[0:rk, 0:hk].set(small["gla_wg_f"]).at[rk:2 * rk, hk:].set(small["gla_wg_b"])
    bg = jnp.concatenate([small["gla_bg_f"], small["gla_bg_b"]])[None]
    la = gate_act(mm(glr, wg), bg)
    o_gla = rmsnorm(gla(gq, gk, gv, la), small["gla_norm_g"][None])

    cq = rmsnorm(mq, small["mla_q_norm_g"][None])
    w_nope, w_rope = _split_w_uq(jnp.concatenate([big["w_uq"][j] for j in range(N_CHIPS)], axis=1))
    qn = mm16(cq, w_nope)
    qr = mm(cq, w_rope)
    qr = fma(qr, rope_q[0], _swap_halves(qr, MLA_ROPE), rope_q[1])
    ckv = rmsnorm(mkv, small["mla_kv_norm_g"][None])
    kv = mm16(ckv, jnp.concatenate([big["w_ukv"][j] for j in range(N_CHIPS)], axis=1))
    kr = mkr[:, :MLA_ROPE]
    kr = fma(kr, rope_k[0], _swap_halves(kr, MLA_ROPE), rope_k[1])
    o_mla = rmsnorm(attn(qn, _pad_heads(qr, MLA_ROPE), kv, _pad_heads(kr, MLA_ROPE)), small["mla_out_g"][None])

    cw = jnp.concatenate([small["conv_w"], jnp.zeros((5, CONV_CH), F32)], axis=0)
    o_conv = rmsnorm(conv_op(cb, cc, cx, cw), small["conv_out_g"][None])

    o = jnp.concatenate([o_gla, o_mla, o_conv], axis=1)
    w_out = big["w_out"].reshape(d, d)
    return out_block(o, z, w_out, xh, gate)


SMALL_REPL = ("norm_g", "gla_bg_f", "gla_bg_b", "gla_norm_g", "mla_q_norm_g", "mla_kv_norm_g", "mla_out_g",
              "conv_out_g")
SMALL_SHARDED = ("gla_wg_f", "gla_wg_b", "conv_w")
BIG = ("w_in", "w_out", "w_uq", "w_ukv")
HALF_AXIS = (1, 0, 0, 0)


def kernel(x, c, positions, ada_w, ada_b, norm_g, w_in, gla_wg_f, gla_bg_f, gla_wg_b, gla_bg_b, gla_norm_g, mla_q_norm_g, mla_kv_norm_g, mla_w_uq, mla_w_ukv, mla_out_g, conv_w, conv_out_g, w_out, final_g, loss_target, m_ada_w, m_ada_b, m_norm_g, m_w_in, m_gla_wg_f, m_gla_bg_f, m_gla_wg_b, m_gla_bg_b, m_gla_norm_g, m_mla_q_norm_g, m_mla_kv_norm_g, m_mla_w_uq, m_mla_w_ukv, m_mla_out_g, m_conv_w, m_conv_out_g, m_w_out, m_final_g, v_ada_w, v_ada_b, v_norm_g, v_w_in, v_gla_wg_f, v_gla_bg_f, v_gla_wg_b, v_gla_bg_b, v_gla_norm_g, v_mla_q_norm_g, v_mla_kv_norm_g, v_mla_w_uq, v_mla_w_ukv, v_mla_out_g, v_conv_w, v_conv_out_g, v_w_out, v_final_g):
    xi, yi, ci = _place()
    chip = 2 * xi + yi
    dev = 2 * chip + ci
    s = x.shape[1]
    d = D_MODEL
    weights = dict(ada_w=ada_w, ada_b=ada_b, norm_g=norm_g, w_in=w_in, gla_wg_f=gla_wg_f, gla_bg_f=gla_bg_f,
                   gla_wg_b=gla_wg_b, gla_bg_b=gla_bg_b, gla_norm_g=gla_norm_g, mla_q_norm_g=mla_q_norm_g,
                   mla_kv_norm_g=mla_kv_norm_g, mla_w_uq=mla_w_uq, mla_w_ukv=mla_w_ukv, mla_out_g=mla_out_g,
                   conv_w=conv_w, conv_out_g=conv_out_g, w_out=w_out, final_g=final_g)
    m_in = dict(ada_w=m_ada_w, ada_b=m_ada_b, norm_g=m_norm_g, w_in=m_w_in, gla_wg_f=m_gla_wg_f, gla_bg_f=m_gla_bg_f,
                gla_wg_b=m_gla_wg_b, gla_bg_b=m_gla_bg_b, gla_norm_g=m_gla_norm_g, mla_q_norm_g=m_mla_q_norm_g,
                mla_kv_norm_g=m_mla_kv_norm_g, mla_w_uq=m_mla_w_uq, mla_w_ukv=m_mla_w_ukv, mla_out_g=m_mla_out_g,
                conv_w=m_conv_w, conv_out_g=m_conv_out_g, w_out=m_w_out, final_g=m_final_g)
    v_in = dict(ada_w=v_ada_w, ada_b=v_ada_b, norm_g=v_norm_g, w_in=v_w_in, gla_wg_f=v_gla_wg_f, gla_bg_f=v_gla_bg_f,
                gla_wg_b=v_gla_wg_b, gla_bg_b=v_gla_bg_b, gla_norm_g=v_gla_norm_g, mla_q_norm_g=v_mla_q_norm_g,
                mla_kv_norm_g=v_mla_kv_norm_g, mla_w_uq=v_mla_w_uq, mla_w_ukv=v_mla_w_ukv, mla_out_g=v_mla_out_g,
                conv_w=v_conv_w, conv_out_g=v_conv_out_g, w_out=v_w_out, final_g=v_final_g)

    g_c, g_wgf, g_wgb, g_cw = all_gather8([c, gla_wg_f, gla_wg_b, conv_w], "gather_small")

    def unshard_cols(g):
        g4 = g[0::2]
        return g4.transpose(1, 2, 0, 3).reshape(g4.shape[1], g4.shape[2], -1)

    small_full = dict(gla_wg_f=unshard_cols(g_wgf), gla_wg_b=unshard_cols(g_wgb), conv_w=unshard_cols(g_cw))
    for nme in SMALL_REPL:
        small_full[nme] = weights[nme]
    smalls = [{nme: small_full[nme][l] for nme in SMALL_REPL + SMALL_SHARDED} for l in range(DEPTH)]

    big_src = (jnp.swapaxes(w_in, 1, 2), w_out, mla_w_uq, mla_w_ukv)

    def my_halves(l, zero=0):
        out = []
        for t, a in enumerate(big_src):
            n_half = a.shape[1 + HALF_AXIS[t]] // 2
            out.append(lax.dynamic_slice_in_dim(a[l], ci * n_half + zero, n_half, axis=HALF_AXIS[t]).astype(BF16))
        return out

    def landing(blocks):
        return [lax.empty((2, N_CHIPS) + b.shape, b.dtype) for b in blocks]

    def finish_gather(handle, after, tag):
        blocks, lands = ici_wait(handle, after, "gather", "gather_wait" + tag)
        lands, done = gather_share(blocks, lands, "gather_share" + tag)
        full = [lax.dynamic_update_slice(g, b[None, None], (ci, chip) + (0,) * b.ndim) for g, b in zip(lands, blocks)]
        return full, done

    halves0 = my_halves(0)
    started0 = ici_start(halves0, landing(halves0), "gather", "gather_start0")

    c_act = _silu_rows(g_c[:, 0, :])
    c_act16 = jnp.concatenate([c_act, jnp.zeros_like(c_act)], axis=0)
    n_ada = ada_w.shape[2]
    parts = []
    for l in range(DEPTH):
        bias = lax.dynamic_slice_in_dim(ada_b[l], chip * n_ada, n_ada)[None]
        parts.append(_mm(c_act16, ada_w[l], bias=bias, name="ada_fwd"))
    g_mod, = all_gather8([jnp.stack(parts)], "gather_mod")
    mod_mine = lax.dynamic_index_in_dim(g_mod[0::2], dev, 2, keepdims=False)
    mods = mod_mine.transpose(1, 0, 2).reshape(DEPTH, 3 * d)

    inv_freq = ROPE_THETA ** (-jnp.arange(0, MLA_ROPE, 2, dtype=F32) / MLA_ROPE)
    ang = positions[0].astype(F32)[:, None] * inv_freq
    cos, sin = jnp.cos(ang), jnp.sin(ang)
    rope_k = (jnp.concatenate([cos, cos], axis=1), jnp.concatenate([-sin, sin], axis=1))
    rope_q = (jnp.tile(rope_k[0], (1, MLA_HEADS)), jnp.tile(rope_k[1], (1, MLA_HEADS)))

    def run_layer(xh, mod, gathered, small):
        big = {nme: jnp.concatenate([g[0], g[1]], axis=HALF_AXIS[t] + 1) for t, (nme, g) in enumerate(zip(BIG, gathered))}
        return _layer(xh, mod, big, small, rope_q, rope_k)

    def head(hh, fg):
        return loss_op(rmsnorm(hh, fg[None]), loss_target[0])[0, 0]

    gathered0, done0 = finish_gather(started0, mods, "0")
    halves1 = my_halves(1, done0[0, 0].astype(jnp.int32))
    started1 = ici_start(halves1, landing(halves1), "gather", "gather_start1")
    h1, vjp0 = jax.vjp(run_layer, x[0], mods[0] + started1["token"][0, 0], gathered0, smalls[0])
    gathered1, _ = finish_gather(started1, h1, "1")
    h2, vjp1 = jax.vjp(run_layer, h1, mods[1], gathered1, smalls[1])
    loss_dev, vjp_head = jax.vjp(head, h2, final_g)
    dh2, dfinal = vjp_head(jnp.ones((), F32))

    c_idx = jnp.reshape(ci, (1,)).astype(jnp.int32)
    chip_idx = jnp.reshape(chip, (1,)).astype(jnp.int32)

    def reduce_begin(dgath, tag, zero=None):
        theirs = sibling_send(dgath, "reduce_sibling" + tag)
        pair = [_add_pair(a, b, c_idx) for a, b in zip(dgath, theirs)]
        shapes = [(N_CHIPS - 1,) + p.shape[1:] for p in pair]
        if zero is None:
            lands = [lax.empty(shp, BF16) for shp in shapes]
        else:
            lands = [jnp.broadcast_to(zero.astype(BF16), shp) for shp in shapes]
        return ici_start(pair, lands, "reduce", "reduce_start" + tag)

    def reduce_end(handle, after, tag):
        pair, landed = ici_wait(handle, after, "reduce", "reduce_wait" + tag)
        reduced = [_add_chips(p, q, chip_idx) for p, q in zip(pair, landed)]
        others = sibling_swap(reduced, "share_sibling" + tag)
        return [jnp.where(ci == 0, jnp.concatenate([own, other], axis=HALF_AXIS[t]),
                          jnp.concatenate([other, own], axis=HALF_AXIS[t]))
                for t, (own, other) in enumerate(zip(reduced, others))]

    dh1, dmod1, dgath1, dsmall1 = vjp1(dh2)
    reducing1 = reduce_begin(dgath1, "1")
    dx, dmod0, dgath0, dsmall0 = vjp0(dh1 + reducing1["token"][0, 0])
    dmods = jnp.stack([dmod0, dmod1])
    dsmalls = [dsmall0, dsmall1]

    pieces = [dmods.reshape(-1), dfinal]
    for nme in SMALL_REPL + SMALL_SHARDED:
        pieces.append(jnp.stack([dsmalls[l][nme] for l in range(DEPTH)]).reshape(-1))
    pieces.append(loss_dev.reshape(1))
    sizes = [p.shape[0] for p in pieces]
    flat = jnp.concatenate(pieces)
    padn = (-flat.shape[0]) % 128
    flat = jnp.concatenate([flat, jnp.zeros((padn,), F32)])[None]
    g_small, = all_gather8([flat], "gather_small_grads")
    total, small_done = _sum_devices(g_small)
    total = total[0]
    reducing0 = reduce_begin(dgath0, "0", small_done[0, 0])
    offs, at = [], 0
    for n_el in sizes:
        offs.append(at)
        at += n_el

    def piece(i, shape):
        return total[offs[i]:offs[i] + sizes[i]].reshape(shape)

    grads = {"ada_b": piece(0, (DEPTH, 3 * d)), "final_g": piece(1, (d,))}
    loss = piece(len(pieces) - 1, ())
    for i, nme in enumerate(SMALL_REPL + SMALL_SHARDED):
        full = piece(2 + i, small_full[nme].shape)
        if nme in SMALL_SHARDED:
            ncol = weights[nme].shape[2]
            full = lax.dynamic_slice_in_dim(full, chip * ncol, ncol, axis=2)
        grads[nme] = full

    dmod_all = g_small[:, 0, :DEPTH * 3 * d].reshape(N_DEV, DEPTH, 3 * d)
    dmod_cols = lax.dynamic_slice_in_dim(dmod_all, chip * n_ada, n_ada, axis=2)
    g_ada = []
    for l in range(DEPTH):
        dm16 = jnp.concatenate([dmod_cols[:, l], jnp.zeros((N_DEV, n_ada), F32)], axis=0)
        dm16 = dm16 + reducing0["token"][0, 0]
        g_ada.append(_mm(c_act16, dm16, ta=True, name="ada_bwd"))
    grads["ada_w"] = jnp.stack(g_ada)

    order = list(weights)
    big_names = ("w_in", "w_out", "mla_w_uq", "mla_w_ukv")
    delta, new_m, new_v = {}, {}, {}
    for nme in order:
        if nme not in big_names:
            delta[nme], new_m[nme], new_v[nme] = _adamw(weights[nme], grads[nme], m_in[nme], v_in[nme])

    def first(t):
        return t[(slice(0, 1),) * t.ndim].reshape(1)

    big_grads1 = reduce_end(reducing1, jnp.concatenate([first(dx), first(reducing0["token"])]), "1")
    done = [first(delta[nme]) for nme in order if nme not in big_names] + [first(g) for g in big_grads1]
    big_grads0 = reduce_end(reducing0, jnp.concatenate(done), "0")
    for nme, g0, g1 in zip(big_names, big_grads0, big_grads1):
        grads[nme] = jnp.stack([g0, g1])
    for nme in big_names:
        if nme == "w_in":
            w_t, m_t, v_t = (jnp.swapaxes(t, 1, 2) for t in (w_in, m_w_in, v_w_in))
            res = _adamw(w_t, grads[nme], m_t, v_t)
            delta[nme], new_m[nme], new_v[nme] = (jnp.swapaxes(t, 1, 2) for t in res)
            grads[nme] = jnp.swapaxes(grads[nme], 1, 2)
            continue
        delta[nme], new_m[nme], new_v[nme] = _adamw(weights[nme], grads[nme], m_in[nme], v_in[nme])
    return (loss, dx[None], *[grads[n_] for n_ in order], *[delta[n_] for n_ in order],
            *[new_m[n_] for n_ in order], *[new_v[n_] for n_ in order])
```

```python
import functools

import jax
import jax.numpy as jnp
from jax import lax
from jax.experimental import pallas as pl
from jax.experimental.pallas import tpu as pltpu

F32 = jnp.float32
BF16 = jnp.bfloat16
MESH = pl.DeviceIdType.MESH
HIGHEST = lax.Precision.HIGHEST

DEPTH = 2
D_MODEL = 2048
GLA_HEADS = 6
GLA_DK = 64
GLA_DV = 128
GLA_RANK = 16
GLA_TEMP = 16.0
GLA_CHUNK = 64
GLA_W = GLA_HEADS * GLA_DV
MLA_HEADS = 6
MLA_QL = 384
MLA_KVL = 256
MLA_NOPE = 128
MLA_ROPE = 64
MLA_DV = 128
MLA_W = MLA_HEADS * MLA_DV
CONV_CH = D_MODEL - GLA_W - MLA_W
ROPE_THETA = 10000.0
EPS = 1e-6
IN_DIM = 5856
N_CHIPS = 4
N_DEV = 8

ADAM_LR = 0.001
ADAM_B1 = 0.9
ADAM_B2 = 0.999
ADAM_EPS = 1e-08
ADAM_WD = 0.01
ADAM_STEP = 10

PROJ_SEGS = (
    ("gq", 0, 384, 384), ("gk", 384, 384, 384), ("gv", 768, 768, 768), ("glr", 1536, 32, 128),
    ("mq", 1568, 384, 384), ("mkv", 1952, 256, 256), ("mkr", 2208, 64, 128),
    ("cb", 2272, 512, 512), ("cc", 2784, 512, 512), ("cx", 3296, 512, 512),
    ("pad", 3808, 0, 128), ("z", 3808, 2048, 2048),
)
PROJ_AL = sum(s[3] for s in PROJ_SEGS)

ANY = pl.BlockSpec(memory_space=pl.ANY)
VMEM_LIMIT = 48 * 1024 * 1024
BLOCK_BYTES = 2 * 1024 * 1024


def _params(sem=None):
    return pltpu.CompilerParams(dimension_semantics=sem, vmem_limit_bytes=VMEM_LIMIT)


def _dot(a, b, ca, cb, precision=None):
    return lax.dot_general(a, b, (((ca,), (cb,)), ((), ())), preferred_element_type=F32, precision=precision)


def _tile(dim, prefs):
    for t in prefs:
        if dim % t == 0:
            return t
    return dim


def _pick_rows(rows, width, itemsize=4):
    for t in (2048, 1024, 512, 256, 128, 64, 32, 16, 8):
        if rows % t == 0 and t * width * itemsize <= BLOCK_BYTES:
            return t
    return rows


def _mm(a, b, *, ta=False, tb=False, bias=None, out_dtype=F32, name="mm"):
    if ta:
        K, M = a.shape
    else:
        M, K = a.shape
    if tb:
        N, Kb = b.shape
    else:
        Kb, N = b.shape
    assert K == Kb, (a.shape, b.shape, ta, tb)
    tm = _tile(M, (1024, 512, 256, 128))
    tn = _tile(N, (1024, 512, 384, 256, 128))
    tk = _tile(K, (2048, 1024, 512, 256, 128))
    nk = K // tk
    has_bias = bias is not None

    def body(*refs):
        a_ref, b_ref = refs[0], refs[1]
        bias_ref = refs[2] if has_bias else None
        o_ref = refs[3 if has_bias else 2]
        part = _dot(a_ref[...].astype(BF16), b_ref[...].astype(BF16), 0 if ta else 1, 1 if tb else 0)

        def finish(r):
            if has_bias:
                r = r + bias_ref[...]
            o_ref[...] = r.astype(out_dtype)

        if nk == 1:
            finish(part)
            return
        acc_ref = refs[-1]
        k = pl.program_id(2)

        @pl.when(k == 0)
        def _():
            acc_ref[...] = part

        @pl.when(k != 0)
        def _():
            acc_ref[...] += part

        @pl.when(k == nk - 1)
        def _():
            finish(acc_ref[...])

    a_spec = pl.BlockSpec((tk, tm), lambda i, j, k: (k, i)) if ta else pl.BlockSpec((tm, tk), lambda i, j, k: (i, k))
    b_spec = pl.BlockSpec((tn, tk), lambda i, j, k: (j, k)) if tb else pl.BlockSpec((tk, tn), lambda i, j, k: (k, j))
    in_specs = [a_spec, b_spec]
    args = [a, b]
    if has_bias:
        in_specs.append(pl.BlockSpec((1, tn), lambda i, j, k: (0, j)))
        args.append(bias)
    return pl.pallas_call(
        body, name=name, grid=(M // tm, N // tn, nk),
        in_specs=in_specs, out_specs=pl.BlockSpec((tm, tn), lambda i, j, k: (i, j)),
        out_shape=jax.ShapeDtypeStruct((M, N), out_dtype),
        scratch_shapes=[pltpu.VMEM((tm, tn), F32)] if nk > 1 else [],
        compiler_params=_params(("parallel", "parallel", "arbitrary")),
    )(*args)


@jax.custom_vjp
def mm(a, b):
    return _mm(a, b, name="mm_fwd")


def _mm_f(a, b):
    return _mm(a, b, name="mm_fwd"), (a, b)


def _mm_b(res, g):
    a, b = res
    return _mm(g, b, tb=True, out_dtype=a.dtype, name="mm_da"), _mm(a, g, ta=True, out_dtype=b.dtype, name="mm_db")


mm.defvjp(_mm_f, _mm_b)


@jax.custom_vjp
def mm16(a, b):
    return _mm(a, b, out_dtype=BF16, name="mm16_fwd")


def _mm16_f(a, b):
    return mm16(a, b), (a, b)


mm16.defvjp(_mm16_f, _mm_b)


def _rows(body, name, tiled, full, tiled_out, acc_out, tr=None):
    rows = tiled[0].shape[0]
    if tr is None:
        width = max([a.shape[1] for a in tiled] + [s.shape[1] for s in tiled_out])
        tr = _pick_rows(rows, width)
    in_specs = [pl.BlockSpec((tr, a.shape[1]), lambda i: (i, 0)) for a in tiled]
    in_specs += [pl.BlockSpec(a.shape, lambda i: (0, 0)) for a in full]
    out_specs = [pl.BlockSpec((tr, s.shape[1]), lambda i: (i, 0)) for s in tiled_out]
    out_specs += [pl.BlockSpec(s.shape, lambda i: (0, 0)) for s in acc_out]

    def wrapped(*refs):
        body(pl.program_id(0), *refs)

    outs = pl.pallas_call(
        wrapped, name=name, grid=(rows // tr,), in_specs=in_specs, out_specs=out_specs,
        out_shape=list(tiled_out) + list(acc_out),
        compiler_params=_params(("arbitrary",)),
    )(*tiled, *full)
    return outs


def _sds(shape, dtype=F32):
    return jax.ShapeDtypeStruct(tuple(shape), dtype)


def _acc(step, ref, val):
    @pl.when(step == 0)
    def _():
        ref[...] = val

    @pl.when(step != 0)
    def _():
        ref[...] += val


def _colsum(v):
    return jnp.sum(v, axis=0, keepdims=True)


def _rstd(x):
    return lax.rsqrt(jnp.mean(x * x, axis=-1, keepdims=True) + EPS)


def _norm_grid(x, g):
    rows, w = x.shape[0], g.shape[1]
    tr = _pick_rows(rows, w)
    blk = pl.BlockSpec((tr, w), lambda i, j: (i, j))
    gblk = pl.BlockSpec((1, w), lambda i, j: (0, 0))
    return (rows // tr, x.shape[1] // w), blk, gblk


@jax.custom_vjp
def rmsnorm(x, g):
    def body(x_ref, g_ref, o_ref):
        x = x_ref[...]
        o_ref[...] = x * _rstd(x) * g_ref[...]

    grid, blk, gblk = _norm_grid(x, g)
    return pl.pallas_call(body, name="rmsnorm_fwd", grid=grid, in_specs=[blk, gblk], out_specs=blk,
                          out_shape=_sds(x.shape), compiler_params=_params(("parallel", "parallel")))(x, g)


def _rmsnorm_f(x, g):
    return rmsnorm(x, g), (x, g)


def _rmsnorm_b(res, dy):
    x, g = res

    def body(x_ref, dy_ref, g_ref, dx_ref, dg_ref):
        x = x_ref[...]
        dy = dy_ref[...]
        r = _rstd(x)
        xh = x * r
        dxh = dy * g_ref[...]
        dx_ref[...] = r * (dxh - xh * jnp.mean(dxh * xh, axis=-1, keepdims=True))
        first = jnp.logical_and(pl.program_id(0) == 0, pl.program_id(1) == 0)
        _acc(jnp.where(first, 0, 1), dg_ref, _colsum(dy * xh))

    grid, blk, gblk = _norm_grid(x, g)
    dx, dg = pl.pallas_call(body, name="rmsnorm_bwd", grid=grid, in_specs=[blk, blk, gblk], out_specs=[blk, gblk],
                            out_shape=[_sds(x.shape), _sds(g.shape)],
                            compiler_params=_params(("arbitrary", "arbitrary")))(x, dy, g)
    return dx, dg


rmsnorm.defvjp(_rmsnorm_f, _rmsnorm_b)


def _modulate(x, g, scale, shift):
    def body(i, x_ref, g_ref, sc_ref, sh_ref, o_ref):
        x = x_ref[...]
        xn = x * _rstd(x) * g_ref[...]
        o_ref[...] = (xn * (1.0 + sc_ref[...]) + sh_ref[...]).astype(BF16)
    return _rows(body, "modulate_fwd", [x], [g, scale, shift], [_sds(x.shape, BF16)], [])[0]


def _modulate_bwd(x, g, scale, shift, dh):
    def body(i, x_ref, dh_ref, g_ref, sc_ref, dx_ref, dg_ref, dsc_ref, dsh_ref):
        x = x_ref[...]
        dh = dh_ref[...]
        gv = g_ref[...]
        r = _rstd(x)
        xh = x * r
        dxn = dh * (1.0 + sc_ref[...])
        dxh = dxn * gv
        dx_ref[...] = r * (dxh - xh * jnp.mean(dxh * xh, axis=-1, keepdims=True))
        _acc(i, dg_ref, _colsum(dxn * xh))
        _acc(i, dsc_ref, _colsum(dh * (xh * gv)))
        _acc(i, dsh_ref, _colsum(dh))

    v = _sds(g.shape)
    return _rows(body, "modulate_bwd", [x, dh], [g, scale], [_sds(x.shape)], [v, v, v])


@jax.custom_vjp
def mod_mm(x, g, scale, shift, wt):
    return _mm(_modulate(x, g, scale, shift), wt, tb=True, name="mm_in")


def _mod_mm_f(x, g, scale, shift, wt):
    h = _modulate(x, g, scale, shift)
    return _mm(h, wt, tb=True, name="mm_in"), (x, g, scale, shift, wt, h)


def _mod_mm_b(res, dproj):
    x, g, scale, shift, wt, h = res
    dproj = dproj.astype(BF16)
    dh = _mm(dproj, wt, name="mm_in_dh")
    dwt = _mm(dproj, h, ta=True, out_dtype=wt.dtype, name="mm_in_dw")
    dx, dg, dsc, dsh = _modulate_bwd(x, g, scale, shift, dh)
    return dx, dg, dsc, dsh, dwt


mod_mm.defvjp(_mod_mm_f, _mod_mm_b)


def _sigmoid(z):
    return 1.0 / (1.0 + jnp.exp(-z))


def _gate_mul(o, z):
    def body(i, o_ref, z_ref, y_ref):
        z = z_ref[...]
        y_ref[...] = (o_ref[...] * (z * _sigmoid(z))).astype(BF16)
    return _rows(body, "gate_mul_fwd", [o, z], [], [_sds(o.shape, BF16)], [])[0]


def _gate_mul_bwd(o, z, dy):
    def body(i, o_ref, z_ref, dy_ref, do_ref, dz_ref):
        z = z_ref[...]
        dy = dy_ref[...]
        s = _sigmoid(z)
        do_ref[...] = dy * (z * s)
        dz_ref[...] = dy * o_ref[...] * (s * (1.0 + z * (1.0 - s)))
    return _rows(body, "gate_mul_bwd", [o, z, dy], [], [_sds(o.shape), _sds(o.shape)], [])


def _residual(x, u, gate):
    def body(i, x_ref, u_ref, g_ref, o_ref):
        o_ref[...] = x_ref[...] + g_ref[...] * u_ref[...]
    return _rows(body, "residual_fwd", [x, u], [gate], [_sds(x.shape)], [])[0]


def _residual_bwd(d, u, gate):
    def body(i, d_ref, u_ref, g_ref, du_ref, dg_ref):
        d = d_ref[...]
        du_ref[...] = (g_ref[...] * d).astype(BF16)
        _acc(i, dg_ref, _colsum(d * u_ref[...]))

    return _rows(body, "residual_bwd", [d, u], [gate], [_sds(u.shape, BF16)], [_sds(gate.shape)])


@jax.custom_vjp
def out_block(o, z, w, x, gate):
    return _residual(x, _mm(_gate_mul(o, z), w, name="mm_out"), gate)


def _out_block_f(o, z, w, x, gate):
    y = _gate_mul(o, z)
    u = _mm(y, w, name="mm_out")
    return _residual(x, u, gate), (o, z, w, y, u, gate)


def _out_block_b(res, d):
    o, z, w, y, u, gate = res
    du, dgate = _residual_bwd(d, u, gate)
    dy = _mm(du, w, tb=True, name="mm_out_dy")
    dw = _mm(y, du, ta=True, out_dtype=w.dtype, name="mm_out_dw")
    do, dz = _gate_mul_bwd(o, z, dy)
    return do, dz, dw, d, dgate


out_block.defvjp(_out_block_f, _out_block_b)


@jax.custom_vjp
def gate_act(u, b):
    def body(i, u_ref, b_ref, o_ref):
        t = u_ref[...] + b_ref[...]
        o_ref[...] = (jnp.minimum(t, 0.0) - jnp.log(1.0 + jnp.exp(-jnp.abs(t)))) / GLA_TEMP
    return _rows(body, "gate_act_fwd", [u], [b], [_sds(u.shape)], [])[0]


def _gate_act_f(u, b):
    return gate_act(u, b), (u, b)


def _gate_act_b(res, d):
    u, b = res

    def body(i, u_ref, d_ref, b_ref, du_ref, db_ref):
        t = u_ref[...] + b_ref[...]
        du = d_ref[...] * _sigmoid(-t) / GLA_TEMP
        du_ref[...] = du
        _acc(i, db_ref, _colsum(du))

    du, db = _rows(body, "gate_act_bwd", [u, d], [b], [_sds(u.shape)], [_sds(b.shape)])
    return du, db


gate_act.defvjp(_gate_act_f, _gate_act_b)


@jax.custom_vjp
def fma(a, b, c, d):
    def body(i, a_ref, b_ref, c_ref, d_ref, o_ref):
        o_ref[...] = a_ref[...] * b_ref[...] + c_ref[...] * d_ref[...]
    return _rows(body, "fma_fwd", [a, b, c, d], [], [_sds(a.shape)], [])[0]


def _fma_f(a, b, c, d):
    return fma(a, b, c, d), (b, d)


def _fma_b(res, g):
    b, d = res

    def body(i, g_ref, b_ref, d_ref, da_ref, dc_ref):
        g = g_ref[...]
        da_ref[...] = g * b_ref[...]
        dc_ref[...] = g * d_ref[...]

    da, dc = _rows(body, "fma_bwd", [g, b, d], [], [_sds(g.shape), _sds(g.shape)], [])
    return da, jnp.zeros_like(b), dc, jnp.zeros_like(d)


fma.defvjp(_fma_f, _fma_b)


def _silu_rows(c):
    def body(i, c_ref, o_ref):
        v = c_ref[...]
        o_ref[...] = v * _sigmoid(v)
    return _rows(body, "silu", [c], [], [_sds(c.shape)], [])[0]


@jax.custom_vjp
def loss_op(y, t):
    return _loss_fwd(y, t)[0]


def _loss_fwd(y, t):
    inv = 1.0 / y.shape[1]

    def body(i, y_ref, t_ref, d_ref, l_ref):
        e = y_ref[...] - t_ref[...]
        d_ref[...] = e * inv
        _acc(i, l_ref, jnp.sum(_colsum(e * e), axis=1, keepdims=True) * (0.5 * inv))

    d, l = _rows(body, "loss_fwd", [y, t], [], [_sds(y.shape)], [_sds((1, 1))])
    return l, d


def _loss_f(y, t):
    l, d = _loss_fwd(y, t)
    return l, d


def _loss_b(d, g):
    return d * g, jnp.zeros_like(d)


loss_op.defvjp(_loss_f, _loss_b)


def _conv_terms(cc, cx, rows, n):
    u = cc * cx
    up = jnp.where(rows == 0, 0.0, pltpu.roll(u, 1, 0))
    un = jnp.where(rows == n - 1, 0.0, pltpu.roll(u, n - 1, 0))
    return u, up, un


CONV_COLS = 128


def _conv_specs(s, n_in):
    blk = pl.BlockSpec((s, CONV_COLS), lambda j: (0, j))
    wblk = pl.BlockSpec((8, CONV_COLS), lambda j: (0, j))
    return [blk] * n_in + [wblk], blk, wblk


@jax.custom_vjp
def conv_op(cb, cc, cx, w):
    s, ch = cb.shape

    def body(cb_ref, cc_ref, cx_ref, w_ref, o_ref):
        rows = lax.broadcasted_iota(jnp.int32, (s, CONV_COLS), 0)
        u, up, un = _conv_terms(cc_ref[...], cx_ref[...], rows, s)
        conv = up * w_ref[0:1, :] + u * w_ref[1:2, :] + un * w_ref[2:3, :]
        o_ref[...] = cb_ref[...] * conv

    in_specs, blk, _ = _conv_specs(s, 3)
    return pl.pallas_call(
        body, name="conv_fwd", grid=(ch // CONV_COLS,), in_specs=in_specs, out_specs=blk,
        out_shape=_sds(cb.shape), compiler_params=_params(("parallel",)),
    )(cb, cc, cx, w)


def _conv_f(cb, cc, cx, w):
    return conv_op(cb, cc, cx, w), (cb, cc, cx, w)


def _conv_b(res, d):
    cb, cc, cx, w = res
    s, ch = cb.shape

    def body(cb_ref, cc_ref, cx_ref, d_ref, w_ref, dcb_ref, dcc_ref, dcx_ref, dw_ref):
        rows = lax.broadcasted_iota(jnp.int32, (s, CONV_COLS), 0)
        cc_v = cc_ref[...]
        cx_v = cx_ref[...]
        u, up, un = _conv_terms(cc_v, cx_v, rows, s)
        w0, w1, w2 = w_ref[0:1, :], w_ref[1:2, :], w_ref[2:3, :]
        dv = d_ref[...]
        dcb_ref[...] = dv * (up * w0 + u * w1 + un * w2)
        dconv = dv * cb_ref[...]
        d_next = jnp.where(rows == s - 1, 0.0, pltpu.roll(dconv, s - 1, 0))
        d_prev = jnp.where(rows == 0, 0.0, pltpu.roll(dconv, 1, 0))
        du = w0 * d_next + w1 * dconv + w2 * d_prev
        dcc_ref[...] = du * cx_v
        dcx_ref[...] = du * cc_v
        dw_ref[...] = jnp.zeros_like(dw_ref)
        dw_ref[0:1, :] = _colsum(dconv * up)
        dw_ref[1:2, :] = _colsum(dconv * u)
        dw_ref[2:3, :] = _colsum(dconv * un)

    in_specs, blk, wblk = _conv_specs(s, 4)
    v = _sds(cb.shape)
    return tuple(pl.pallas_call(
        body, name="conv_bwd", grid=(ch // CONV_COLS,), in_specs=in_specs, out_specs=[blk, blk, blk, wblk],
        out_shape=[v, v, v, _sds(w.shape)], compiler_params=_params(("parallel",)),
    )(cb, cc, cx, d, w))


conv_op.defvjp(_conv_f, _conv_b)


def _gla_masks(rev):
    c = GLA_CHUNK
    row = lax.broadcasted_iota(jnp.int32, (c, c), 0)
    col = lax.broadcasted_iota(jnp.int32, (c, c), 1)
    mask = (row < col) if rev else (row >= col)
    return rev, mask


def _chunk_cumsum(g, rev):
    c = g.shape[0]
    row = lax.broadcasted_iota(jnp.int32, g.shape, 0)
    b = g
    s = 1
    while s < c:
        if rev:
            b = b + jnp.where(row < c - s, pltpu.roll(b, c - s, 0), 0.0)
        else:
            b = b + jnp.where(row >= s, pltpu.roll(b, s, 0), 0.0)
        s *= 2
    return b


GLA_UNROLL = 4


def _gla_rows(n):
    return pl.ds(pl.multiple_of(n * GLA_CHUNK, GLA_CHUNK), GLA_CHUNK)


def _gla_scan(s_ref, bt_ref, st_ref, n_chunks, descending):
    st_ref[...] = jnp.zeros_like(st_ref)

    def step(i, carry):
        n = (n_chunks - 1 - i) if descending else i
        own = s_ref[n]
        st = st_ref[...]
        s_ref[n] = st
        st_ref[...] = st * jnp.exp(bt_ref[n]) + own
        return carry

    lax.fori_loop(0, n_chunks, step, 0)


GLA_PAIR = 2


def _gla_specs(s):
    dk, dv = GLA_DK, GLA_DV
    n_pairs = GLA_HEADS // GLA_PAIR
    blk_k = pl.BlockSpec((s, GLA_PAIR * dk), lambda p: (0, p))
    blk_gb = pl.BlockSpec((s, GLA_PAIR * dk), lambda p: (0, n_pairs + p))
    blk_v = pl.BlockSpec((s, GLA_PAIR * dv), lambda p: (0, p))
    return n_pairs, blk_k, blk_gb, blk_v


def _head_lanes(hh):
    lane = lax.broadcasted_iota(jnp.int32, (1, GLA_PAIR * GLA_DK), 1)
    return jnp.logical_and(lane >= hh * GLA_DK, lane < (hh + 1) * GLA_DK)


def _gla_fwd(q, k, v, la):
    s = q.shape[0]
    dk, dv = GLA_DK, GLA_DV
    pw = GLA_PAIR * dk
    n_chunks = s // GLA_CHUNK
    scale = GLA_DK ** -0.5

    def body(q_ref, k_ref, v_ref, gf_ref, gb_ref, o_ref, sf_ref, sb_ref, bf_ref, bb_ref, btf_ref, btb_ref, st_ref):
        masks = [_gla_masks(rev) for rev in (False, True)]
        dirs = ((False, gf_ref, sf_ref, bf_ref, btf_ref), (True, gb_ref, sb_ref, bb_ref, btb_ref))

        def decays(n, carry):
            rows = _gla_rows(n)
            for rev, g_ref, _, b_ref, bt_ref in dirs:
                g = g_ref[rows, :]
                b_ref[rows, :] = _chunk_cumsum(g, rev)
                bt_ref[n] = _colsum(g)
            return carry

        lax.fori_loop(0, n_chunks, decays, 0, unroll=GLA_UNROLL)
        for hh in range(GLA_PAIR):
            m = _head_lanes(hh)
            vl = slice(hh * dv, (hh + 1) * dv)

            def prepare(n, carry, m=m, vl=vl):
                rows = _gla_rows(n)
                kk = k_ref[rows, :]
                vb = v_ref[rows, vl].astype(BF16)
                for rev, _, s_ref, b_ref, bt_ref in dirs:
                    ke = jnp.where(m, kk * jnp.exp(bt_ref[n] - b_ref[rows, :]), 0.0).astype(BF16)
                    s_ref[n] = _dot(vb, ke, 0, 0)
                return carry

            lax.fori_loop(0, n_chunks, prepare, 0, unroll=GLA_UNROLL)
            for rev, _, s_ref, _, bt_ref in dirs:
                _gla_scan(s_ref, bt_ref, st_ref, n_chunks, descending=rev)

            def emit(n, carry, m=m, vl=vl):
                rows = _gla_rows(n)
                qs = q_ref[rows, :] * scale
                kk = k_ref[rows, :]
                vb = v_ref[rows, vl].astype(BF16)
                o = None
                for (rev, _, s_ref, b_ref, _), (_, mask) in zip(dirs, masks):
                    b = b_ref[rows, :]
                    qd = jnp.where(m, qs * jnp.exp(b), 0.0).astype(BF16)
                    ki = jnp.where(m, kk * jnp.exp(-b), 0.0).astype(BF16)
                    a = jnp.where(mask, _dot(qd, ki, 1, 1), 0.0).astype(BF16)
                    od = _dot(a, vb, 1, 0) + _dot(qd, s_ref[n].astype(BF16), 1, 1)
                    o = od if o is None else o + od
                o_ref[rows, vl] = o
                return carry

            lax.fori_loop(0, n_chunks, emit, 0, unroll=GLA_UNROLL)

    n_pairs, blk_k, blk_gb, blk_v = _gla_specs(s)
    state = pltpu.VMEM((n_chunks, dv, pw), F32)
    scratch = [state, state, pltpu.VMEM((s, pw), F32), pltpu.VMEM((s, pw), F32), pltpu.VMEM((n_chunks, 1, pw), F32),
               pltpu.VMEM((n_chunks, 1, pw), F32), pltpu.VMEM((dv, pw), F32)]
    return pl.pallas_call(
        body, name="gla_fwd", grid=(n_pairs,), in_specs=[blk_k, blk_k, blk_v, blk_k, blk_gb],
        out_specs=blk_v, out_shape=_sds(v.shape), scratch_shapes=scratch,
        compiler_params=_params(("parallel",)),
    )(q, k, v, la, la)


def _gla_bwd(q, k, v, la, do):
    s = q.shape[0]
    dk, dv = GLA_DK, GLA_DV
    pw = GLA_PAIR * dk
    c = GLA_CHUNK
    n_chunks = s // c
    scale = GLA_DK ** -0.5

    def body(q_ref, k_ref, v_ref, gf_ref, gb_ref, do_ref, dq_ref, dk_ref, dv_ref, dgf_ref, dgb_ref,
             sf_ref, sb_ref, bf_ref, bb_ref, btf_ref, btb_ref, dsf_ref, dsb_ref, st_ref):
        masks = [_gla_masks(rev) for rev in (False, True)]
        rowc = lax.broadcasted_iota(jnp.int32, (c, pw), 0)
        dirs = ((False, gf_ref, sf_ref, bf_ref, btf_ref, dsf_ref, dgf_ref),
                (True, gb_ref, sb_ref, bb_ref, btb_ref, dsb_ref, dgb_ref))

        def decays(n, carry):
            rows = _gla_rows(n)
            for rev, g_ref, _, b_ref, bt_ref, _, _ in dirs:
                g = g_ref[rows, :]
                b_ref[rows, :] = _chunk_cumsum(g, rev)
                bt_ref[n] = _colsum(g)
            return carry

        lax.fori_loop(0, n_chunks, decays, 0, unroll=GLA_UNROLL)
        for hh in range(GLA_PAIR):
            m = _head_lanes(hh)
            vl = slice(hh * dv, (hh + 1) * dv)

            def prepare(n, carry, m=m, vl=vl):
                rows = _gla_rows(n)
                qs = q_ref[rows, :] * scale
                kk = k_ref[rows, :]
                vb = v_ref[rows, vl].astype(BF16)
                do_b = do_ref[rows, vl].astype(BF16)
                for rev, _, s_ref, b_ref, bt_ref, ds_ref, _ in dirs:
                    b = b_ref[rows, :]
                    ke = jnp.where(m, kk * jnp.exp(bt_ref[n] - b), 0.0).astype(BF16)
                    qd = jnp.where(m, qs * jnp.exp(b), 0.0).astype(BF16)
                    s_ref[n] = _dot(vb, ke, 0, 0)
                    ds_ref[n] = _dot(do_b, qd, 0, 0)
                return carry

            lax.fori_loop(0, n_chunks, prepare, 0, unroll=GLA_UNROLL)
            for rev, _, s_ref, _, bt_ref, ds_ref, _ in dirs:
                _gla_scan(s_ref, bt_ref, st_ref, n_chunks, descending=rev)
                _gla_scan(ds_ref, bt_ref, st_ref, n_chunks, descending=not rev)

            def emit(n, carry, m=m, vl=vl, first=(hh == 0)):
                rows = _gla_rows(n)
                qs = q_ref[rows, :] * scale
                kk = k_ref[rows, :]
                vb = v_ref[rows, vl].astype(BF16)
                do_b = do_ref[rows, vl].astype(BF16)
                dq = dkk = dvv = None
                for (rev, _, s_ref, b_ref, bt_ref, ds_ref, dg_ref), (_, mask) in zip(dirs, masks):
                    b = b_ref[rows, :]
                    bt = bt_ref[n]
                    eb = jnp.where(m, jnp.exp(b), 0.0)
                    enb = jnp.where(m, jnp.exp(-b), 0.0)
                    etb = jnp.where(m, jnp.exp(bt - b), 0.0)
                    ebt = jnp.exp(bt)
                    qd = qs * eb
                    ki = kk * enb
                    ke = kk * etb
                    qd_b, ki_b, ke_b = qd.astype(BF16), ki.astype(BF16), ke.astype(BF16)
                    st = s_ref[n]
                    dst = ds_ref[n]
                    dst_b = dst.astype(BF16)
                    a = jnp.where(mask, _dot(qd_b, ki_b, 1, 1), 0.0).astype(BF16)
                    da = jnp.where(mask, _dot(do_b, vb, 1, 1), 0.0).astype(BF16)
                    dv_d = _dot(a, do_b, 0, 0) + _dot(ke_b, dst_b, 1, 1)
                    dqd = _dot(da, ki_b, 1, 0) + _dot(do_b, st.astype(BF16), 1, 0)
                    dki = _dot(da, qd_b, 0, 0)
                    dke = _dot(vb, dst_b, 1, 0)
                    dbt = _colsum(st * dst) * ebt + _colsum(dke * ke)
                    db = dqd * qd - dki * ki - dke * ke
                    db = db + jnp.where(rowc == (0 if rev else c - 1), dbt, 0.0)
                    dg = _chunk_cumsum(db, not rev)
                    if first:
                        dg_ref[rows, :] = dg
                    else:
                        dg_ref[rows, :] += dg
                    dq_d = dqd * eb * scale
                    dk_d = dki * enb + dke * etb
                    dq = dq_d if dq is None else dq + dq_d
                    dkk = dk_d if dkk is None else dkk + dk_d
                    dvv = dv_d if dvv is None else dvv + dv_d
                if first:
                    dq_ref[rows, :] = dq
                    dk_ref[rows, :] = dkk
                else:
                    dq_ref[rows, :] += dq
                    dk_ref[rows, :] += dkk
                dv_ref[rows, vl] = dvv
                return carry

            lax.fori_loop(0, n_chunks, emit, 0, unroll=2)

    n_pairs, blk_k, blk_gb, blk_v = _gla_specs(s)
    vk, vv = _sds(q.shape), _sds(v.shape)
    state = pltpu.VMEM((n_chunks, dv, pw), F32)
    scratch = [state, state, pltpu.VMEM((s, pw), F32), pltpu.VMEM((s, pw), F32), pltpu.VMEM((n_chunks, 1, pw), F32),
               pltpu.VMEM((n_chunks, 1, pw), F32), state, state, pltpu.VMEM((dv, pw), F32)]
    return pl.pallas_call(
        body, name="gla_bwd", grid=(n_pairs,), in_specs=[blk_k, blk_k, blk_v, blk_k, blk_gb, blk_v],
        out_specs=[blk_k, blk_k, blk_v, blk_k, blk_k], out_shape=[vk, vk, vv, vk, vk],
        scratch_shapes=scratch, compiler_params=_params(("parallel",)),
    )(q, k, v, la, la, do)


@jax.custom_vjp
def gla(q, k, v, la):
    return _gla_fwd(q, k, v, la)


def _gla_f(q, k, v, la):
    return _gla_fwd(q, k, v, la), (q, k, v, la)


def _gla_b(res, do):
    dq, dk, dv, dgf, dgb = _gla_bwd(*res, do)
    return dq, dk, dv, jnp.concatenate([dgf, dgb], axis=1)


gla.defvjp(_gla_f, _gla_b)


ATTN_TQ = 256
HEAD_LANES = 128


def _attn_blocks(s, tq):
    per_q = pl.BlockSpec((tq, HEAD_LANES), lambda h, j: (j, h))
    k_nope = pl.BlockSpec((s, HEAD_LANES), lambda h, j: (0, 2 * h))
    v_blk = pl.BlockSpec((s, HEAD_LANES), lambda h, j: (0, 2 * h + 1))
    k_rope = pl.BlockSpec((s, HEAD_LANES), lambda h, j: (0, 0))
    lse = pl.BlockSpec((1, tq, 1), lambda h, j: (h, j, 0))
    return per_q, k_nope, v_blk, k_rope, lse


def _attn_fwd(qn, qr, kv, kr):
    s = qn.shape[0]
    tq = min(ATTN_TQ, s)
    scale = (MLA_NOPE + MLA_ROPE) ** -0.5

    def body(qn_ref, qr_ref, kn_ref, v_ref, kr_ref, o_ref, lse_ref):
        q = jnp.concatenate([qn_ref[...], qr_ref[...]], axis=1)
        k = jnp.concatenate([kn_ref[...], kr_ref[...]], axis=1)
        sc = _dot(q, k, 1, 1) * scale
        m = jnp.max(sc, axis=-1, keepdims=True)
        p = jnp.exp(sc - m)
        l = jnp.sum(p, axis=-1, keepdims=True)
        p = p * (1.0 / l)
        o_ref[...] = _dot(p.astype(BF16), v_ref[...], 1, 0)
        lse_ref[0] = m + jnp.log(l)

    per_q, k_nope, v_blk, k_rope, lse = _attn_blocks(s, tq)
    return pl.pallas_call(
        body, name="attn_fwd", grid=(MLA_HEADS, s // tq), in_specs=[per_q, per_q, k_nope, v_blk, k_rope],
        out_specs=[per_q, lse], out_shape=[_sds(qn.shape), _sds((MLA_HEADS, s, 1))],
        compiler_params=_params(("parallel", "parallel")),
    )(qn, qr, kv, kv, kr)


def _attn_bwd(qn, qr, kv, kr, o, lse, do):
    s = qn.shape[0]
    tq = min(ATTN_TQ, s)
    n_q = s // tq
    scale = (MLA_NOPE + MLA_ROPE) ** -0.5

    def body(qn_ref, qr_ref, kn_ref, v_ref, kr_ref, o_ref, lse_ref, do_ref, dqn_ref, dqr_ref, dkv_ref, dkr_ref,
             dk_acc, dv_acc, dkr_acc):
        h, j = pl.program_id(0), pl.program_id(1)
        q = jnp.concatenate([qn_ref[...], qr_ref[...]], axis=1)
        k = jnp.concatenate([kn_ref[...], kr_ref[...]], axis=1)
        do = do_ref[...]
        do_b = do.astype(BF16)
        p = jnp.exp(_dot(q, k, 1, 1) * scale - lse_ref[0])
        dp = _dot(do_b, v_ref[...], 1, 1)
        delta = jnp.sum(do * o_ref[...], axis=-1, keepdims=True)
        ds = (p * (dp - delta) * scale).astype(BF16)
        dq = _dot(ds, k, 1, 0)
        dqn_ref[...] = dq[:, :HEAD_LANES].astype(BF16)
        dqr_ref[...] = dq[:, HEAD_LANES:].astype(BF16)
        dk = _dot(ds, q, 0, 0)
        _acc(j, dk_acc, dk[:, :HEAD_LANES])
        _acc(j, dv_acc, _dot(p.astype(BF16), do_b, 0, 0))
        _acc(jnp.where(jnp.logical_and(h == 0, j == 0), 0, 1), dkr_acc, dk[:, HEAD_LANES:])

        @pl.when(j == n_q - 1)
        def _():
            dkv_ref[:, 0:HEAD_LANES] = dk_acc[...].astype(BF16)
            dkv_ref[:, HEAD_LANES:2 * HEAD_LANES] = dv_acc[...].astype(BF16)

        @pl.when(jnp.logical_and(h == MLA_HEADS - 1, j == n_q - 1))
        def _():
            dkr_ref[...] = dkr_acc[...].astype(BF16)

    per_q, k_nope, v_blk, k_rope, lse_blk = _attn_blocks(s, tq)
    dkv_blk = pl.BlockSpec((s, 2 * HEAD_LANES), lambda h, j: (0, h))
    acc = pltpu.VMEM((s, HEAD_LANES), F32)
    return pl.pallas_call(
        body, name="attn_bwd", grid=(MLA_HEADS, n_q),
        in_specs=[per_q, per_q, k_nope, v_blk, k_rope, per_q, lse_blk, per_q],
        out_specs=[per_q, per_q, dkv_blk, k_rope],
        out_shape=[_sds(qn.shape, BF16), _sds(qr.shape, BF16), _sds(kv.shape, BF16), _sds(kr.shape, BF16)],
        scratch_shapes=[acc, acc, acc], compiler_params=_params(("arbitrary", "arbitrary")),
    )(qn, qr, kv, kv, kr, o, lse, do)


@jax.custom_vjp
def attn(qn, qr, kv, kr):
    return _attn_fwd(qn, qr, kv, kr)[0]


def _attn_f(qn, qr, kv, kr):
    o, lse = _attn_fwd(qn, qr, kv, kr)
    return o, (qn, qr, kv, kr, o, lse)


def _attn_b(res, do):
    return tuple(_attn_bwd(*res, do))


attn.defvjp(_attn_f, _attn_b)


@jax.custom_vjp
def split_proj(proj):
    out, at = [], 0
    for _, _, _, wp in PROJ_SEGS:
        out.append(proj[:, at:at + wp])
        at += wp
    return tuple(out)


def _split_f(proj):
    return split_proj(proj), None


def _split_b(_, gs):
    return (jnp.concatenate(gs, axis=1),)


split_proj.defvjp(_split_f, _split_b)


def _tile2d(rows, width, limit=BLOCK_BYTES):
    fits = [t for t in range(16, rows + 1, 16) if rows % t == 0 and t * width * 4 <= limit]
    if fits and (fits[-1] >= 64 or fits[-1] == rows):
        return fits[-1], width
    if rows * width * 4 <= limit:
        return rows, width
    cols = [t for t in range(128, width + 1, 128) if width % t == 0 and rows * t * 4 <= limit]
    return (rows, cols[-1]) if cols else (rows, width)


def _add_pair(stacked, theirs, c_idx):
    g, r, w = theirs.shape
    tr, tc = _tile2d(r, w)

    def body(c_ref, a_ref, b_ref, o_ref):
        o_ref[0] = (a_ref[0, 0].astype(F32) + b_ref[0].astype(F32)).astype(BF16)

    blk = pl.BlockSpec((1, tr, tc), lambda k, i, j, c: (k, i, j))
    spec = pltpu.PrefetchScalarGridSpec(
        num_scalar_prefetch=1, grid=(g, r // tr, w // tc),
        in_specs=[pl.BlockSpec((1, 1, tr, tc), lambda k, i, j, c: (c[0], k, i, j)), blk], out_specs=blk)
    return pl.pallas_call(body, name="add_pair", grid_spec=spec, out_shape=_sds(theirs.shape, BF16),
                          compiler_params=_params(("parallel", "parallel", "parallel")))(c_idx, stacked, theirs)


def _add_chips(pair, landed, chip_idx):
    _, r, w = pair.shape
    tr, tc = _tile2d(r, w)

    def body(c_ref, p_ref, l0_ref, l1_ref, l2_ref, o_ref):
        o_ref[...] = ((p_ref[0].astype(F32) + l0_ref[0].astype(F32)) + l1_ref[0].astype(F32)) + l2_ref[0].astype(F32)

    specs = [pl.BlockSpec((1, tr, tc), lambda i, j, c: (c[0], i, j))]
    specs += [pl.BlockSpec((1, tr, tc), functools.partial(lambda i, j, c, k: (k, i, j), k=k)) for k in range(N_CHIPS - 1)]
    spec = pltpu.PrefetchScalarGridSpec(num_scalar_prefetch=1, grid=(r // tr, w // tc), in_specs=specs,
                                        out_specs=pl.BlockSpec((tr, tc), lambda i, j, c: (i, j)))
    return pl.pallas_call(body, name="add_chips", grid_spec=spec, out_shape=_sds((r, w)),
                          compiler_params=_params(("parallel", "parallel")))(chip_idx, pair, landed, landed, landed)


def _sum_devices(g):
    n = g.shape[2]

    def body(g_ref, o_ref, done_ref):
        t = g_ref[0]
        for j in range(1, N_DEV):
            t = t + g_ref[j]
        o_ref[...] = t
        done_ref[...] = jnp.zeros_like(done_ref)

    return pl.pallas_call(body, name="sum_devices", out_shape=[_sds((1, n)), _sds((8, 128))],
                          compiler_params=_params())(g)


def _adamw_math(w, gv, m, v):
    c1 = 1.0 - ADAM_B1 ** ADAM_STEP
    c2 = 1.0 - ADAM_B2 ** ADAM_STEP
    mn = ADAM_B1 * m + (1.0 - ADAM_B1) * gv
    vn = ADAM_B2 * v + (1.0 - ADAM_B2) * (gv * gv)
    return -ADAM_LR * ((mn / c1) / (jnp.sqrt(vn / c2) + ADAM_EPS) + ADAM_WD * w), mn, vn


def _adamw(w, g, m, v):
    shp = w.shape
    shp3 = (1, 1, shp[0]) if len(shp) == 1 else (-1,) + tuple(shp[-2:])
    w3, g3, m3, v3 = (t.reshape(shp3) for t in (w, g, m, v))

    def body(w_ref, g_ref, m_ref, v_ref, d_ref, mo_ref, vo_ref):
        d_ref[...], mo_ref[...], vo_ref[...] = _adamw_math(w_ref[...], g_ref[...], m_ref[...], v_ref[...])

    nl, r, wd = w3.shape
    tr, tc = _tile2d(r, wd, BLOCK_BYTES // 2)
    blk = pl.BlockSpec((1, tr, tc), lambda l, i, j: (l, i, j))
    s3 = _sds(w3.shape)
    d, mn, vn = pl.pallas_call(
        body, name="adamw", grid=(nl, r // tr, wd // tc), in_specs=[blk] * 4, out_specs=[blk] * 3,
        out_shape=[s3, s3, s3], compiler_params=_params(("parallel", "parallel", "parallel")),
    )(w3, g3, m3, v3)
    return d.reshape(shp), mn.reshape(shp), vn.reshape(shp)


PIECE_BYTES = 1 << 20


def _place():
    return lax.axis_index("x"), lax.axis_index("y"), lax.axis_index("c")


def _pieces(shape, itemsize):
    if len(shape) >= 3:
        return [(i,) + p for i in range(shape[0]) for p in _pieces(shape[1:], itemsize)]
    rows = shape[0]
    row_bytes = itemsize
    for dsz in shape[1:]:
        row_bytes *= dsz
    k = 1
    while rows % (2 * k) == 0 and (rows // (2 * k)) % 16 == 0 and (rows // k) * row_bytes > PIECE_BYTES:
        k *= 2
    step = rows // k
    return [(pl.ds(j * step, step),) for j in range(k)]


def _split_start(make, src, dst, pieces):
    for p in pieces:
        make(src.at[p], dst.at[p]).start()
    return make(src, dst)


def _comm_call(body, name, arrs, out_shapes, n_remote, n_local):
    return pl.pallas_call(
        body, name=name, in_specs=[ANY] * len(arrs), out_specs=[ANY] * len(out_shapes), out_shape=out_shapes,
        scratch_shapes=[pltpu.SemaphoreType.DMA((n_remote,)), pltpu.SemaphoreType.DMA((n_remote,)),
                        pltpu.SemaphoreType.DMA((n_local,))],
    )(*arrs)


def all_gather8(arrs, name):
    n = len(arrs)
    pieces = [_pieces(a.shape, a.dtype.itemsize) for a in arrs]

    def body(*refs):
        ins, outs = refs[:n], refs[n:2 * n]
        send, recv, _ = refs[2 * n:]
        x, y, c = _place()
        me, sib = (x, y, c), (x, y, 1 - c)
        chips = [(1 - x, y), (x, 1 - y), (1 - x, 1 - y)]

        def slot(p):
            return 4 * p[0] + 2 * p[1] + p[2]

        def maker(t, k, to):
            def make(s, d):
                return pltpu.make_async_remote_copy(src_ref=s, dst_ref=d, send_sem=send.at[7 * t + k],
                                                    recv_sem=recv.at[7 * t + k], device_id=to, device_id_type=MESH)
            return make

        def landing(t, k, block):
            dst = outs[t].at[slot(block)]
            return maker(t, k, me)(dst, dst)

        sent = []
        for t in range(n):
            dst = outs[t].at[slot(me)]
            sent.append(_split_start(maker(t, 0, sib), ins[t], dst, pieces[t]))
            for j, chip in enumerate(chips):
                sent.append(_split_start(maker(t, 1 + j, (*chip, c)), ins[t], dst, pieces[t]))
        for j, chip in enumerate(chips):
            for t in range(n):
                landing(t, 1 + j, (*chip, c)).wait_recv()
                blk = outs[t].at[slot((*chip, c))]
                sent.append(_split_start(maker(t, 4 + j, sib), blk, blk, pieces[t]))
        for t in range(n):
            landing(t, 0, sib).wait_recv()
            for j, chip in enumerate(chips):
                landing(t, 4 + j, (*chip, 1 - c)).wait_recv()
        for cp in sent:
            cp.wait_send()

    outs = [_sds((N_DEV,) + a.shape, a.dtype) for a in arrs]
    got = _comm_call(body, name, arrs, outs, 7 * n, 1)
    x, y, c = _place()
    return [lax.dynamic_update_index_in_dim(g, a, 4 * x + 2 * y + c, 0) for g, a in zip(got, arrs)]


def sibling_send(arrs, name):
    n = len(arrs)
    pieces = [_pieces(a.shape[1:], a.dtype.itemsize) for a in arrs]

    def body(*refs):
        ins, theirs = refs[:n], refs[n:2 * n]
        send, recv, _ = refs[2 * n:]
        x, y, c = _place()
        rem = []
        for t in range(n):
            def make(s, d, t=t):
                return pltpu.make_async_remote_copy(src_ref=s, dst_ref=d, send_sem=send.at[t], recv_sem=recv.at[t],
                                                    device_id=(x, y, 1 - c), device_id_type=MESH)
            rem.append(_split_start(make, ins[t].at[1 - c], theirs[t], pieces[t]))
        for cp in rem:
            cp.wait_recv()
        for cp in rem:
            cp.wait_send()

    outs = [_sds(a.shape[1:], a.dtype) for a in arrs]
    return _comm_call(body, name, arrs, outs, n, 1)


def exchange_chips(arrs, name):
    n = len(arrs)
    pieces = [_pieces(a.shape[1:], a.dtype.itemsize) for a in arrs]

    def body(*refs):
        ins, outs = refs[:n], refs[n:2 * n]
        send, recv, _ = refs[2 * n:]
        x, y, c = _place()
        peers = [(1 - x, y), (x, 1 - y), (1 - x, 1 - y)]
        rem = []
        for t in range(n):
            for j, (px, py) in enumerate(peers):
                def make(s, d, t=t, j=j, px=px, py=py):
                    return pltpu.make_async_remote_copy(
                        src_ref=s, dst_ref=d, send_sem=send.at[3 * t + j], recv_sem=recv.at[3 * t + j],
                        device_id=(px, py, c), device_id_type=MESH)
                rem.append(_split_start(make, ins[t].at[2 * px + py], outs[t].at[j], pieces[t]))
        for cp in rem:
            cp.wait_recv()
        for cp in rem:
            cp.wait_send()

    outs = [_sds((N_CHIPS - 1,) + a.shape[1:], a.dtype) for a in arrs]
    return _comm_call(body, name, arrs, outs, 3 * n, 1)


def sibling_swap(arrs, name):
    n = len(arrs)
    pieces = [_pieces(a.shape, a.dtype.itemsize) for a in arrs]

    def body(*refs):
        ins, outs = refs[:n], refs[n:2 * n]
        send, recv, _ = refs[2 * n:]
        x, y, c = _place()
        rem = []
        for t in range(n):
            def make(s, d, t=t):
                return pltpu.make_async_remote_copy(src_ref=s, dst_ref=d, send_sem=send.at[t], recv_sem=recv.at[t],
                                                    device_id=(x, y, 1 - c), device_id_type=MESH)
            rem.append(_split_start(make, ins[t], outs[t], pieces[t]))
        for cp in rem:
            cp.wait_recv()
        for cp in rem:
            cp.wait_send()

    outs = [_sds(a.shape, a.dtype) for a in arrs]
    return _comm_call(body, name, arrs, outs, n, 1)


def _peer_copies(srcs, lands, send, recv, mode):
    x, y, c = _place()
    my_chip = 2 * x + y
    out = []
    for t in range(len(srcs)):
        for j, (px, py) in enumerate([(1 - x, y), (x, 1 - y), (1 - x, 1 - y)]):
            if mode == "gather":
                s, dst = srcs[t], lands[t].at[c, my_chip]
            else:
                s, dst = srcs[t].at[2 * px + py], lands[t].at[j]
            out.append(pltpu.make_async_remote_copy(
                src_ref=s, dst_ref=dst, send_sem=send.at[3 * t + j], recv_sem=recv.at[3 * t + j],
                device_id=(px, py, c), device_id_type=MESH))
    return out


HBM = pl.BlockSpec(memory_space=pltpu.HBM)
SEM = pl.BlockSpec(memory_space=pltpu.SEMAPHORE)
EFFECT = pltpu.SideEffectType.DATAFLOW_SIDE_EFFECTING


def ici_start(srcs, lands, mode, name):
    n = len(srcs)

    def body(*refs):
        send, recv = refs[2 * n], refs[2 * n + 1]
        for cp in _peer_copies(refs[:n], refs[n:2 * n], send, recv, mode):
            cp.start()
        refs[-1][...] = jnp.zeros_like(refs[-1])

    thru = [pltpu.HBM(a.shape, a.dtype) for a in list(srcs) + list(lands)]
    outs = pl.pallas_call(
        body, name=name, in_specs=[HBM] * (2 * n), out_specs=[SEM, SEM] + [HBM] * (2 * n) + [pl.BlockSpec(memory_space=pltpu.VMEM)],
        out_shape=[pltpu.SemaphoreType.DMA((3 * n,)), pltpu.SemaphoreType.DMA((3 * n,))] + thru + [_sds((8, 128))],
        input_output_aliases={i: 2 + i for i in range(2 * n)},
        compiler_params=pltpu.CompilerParams(has_side_effects=EFFECT),
    )(*[pltpu.with_memory_space_constraint(a, pltpu.HBM) for a in list(srcs) + list(lands)])
    return dict(send=outs[0], recv=outs[1], srcs=outs[2:2 + n], lands=outs[2 + n:2 + 2 * n], token=outs[-1])


def ici_wait(handle, after, mode, name):
    n = len(handle["srcs"])

    def body(*refs):
        send, recv = refs[2 * n], refs[2 * n + 1]
        for cp in _peer_copies(refs[:n], refs[n:2 * n], send, recv, mode):
            cp.wait_send()
            cp.wait_recv()

    arrs = list(handle["srcs"]) + list(handle["lands"])
    outs = pl.pallas_call(
        body, name=name, in_specs=[HBM] * (2 * n) + [SEM, SEM, ANY], out_specs=[HBM] * (2 * n),
        out_shape=[pltpu.HBM(a.shape, a.dtype) for a in arrs], input_output_aliases={i: i for i in range(2 * n)},
        compiler_params=pltpu.CompilerParams(has_side_effects=EFFECT),
    )(*arrs, handle["send"], handle["recv"], after)
    return outs[:n], outs[n:]


def gather_share(blocks, lands, name):
    n = len(blocks)

    def body(*refs):
        own, buf = refs[:n], refs[2 * n:3 * n]
        done, send, recv = refs[3 * n:]
        x, y, c = _place()
        my_chip = 2 * x + y
        chips = [2 * (1 - x) + y, 2 * x + (1 - y), 2 * (1 - x) + (1 - y)]
        sent = []
        for t in range(n):
            def make(s, d, k, t=t):
                return pltpu.make_async_remote_copy(src_ref=s, dst_ref=d, send_sem=send.at[4 * t + k],
                                                    recv_sem=recv.at[4 * t + k], device_id=(x, y, 1 - c),
                                                    device_id_type=MESH)
            cp = make(own[t], buf[t].at[c, my_chip], 0)
            cp.start()
            sent.append(cp)
            for k, pc in enumerate(chips):
                cp = make(buf[t].at[c, pc], buf[t].at[c, pc], 1 + k)
                cp.start()
                sent.append(cp)
        for t in range(n):
            for k in range(4):
                got = buf[t].at[1 - c, k]
                pltpu.make_async_remote_copy(src_ref=got, dst_ref=got, send_sem=send.at[4 * t + k],
                                             recv_sem=recv.at[4 * t + k], device_id=(x, y, 1 - c),
                                             device_id_type=MESH).wait_recv()
        for cp in sent:
            cp.wait_send()
        done[...] = jnp.zeros_like(done)

    outs = pl.pallas_call(
        body, name=name, in_specs=[ANY] * (2 * n), out_specs=[ANY] * n + [pl.BlockSpec(memory_space=pltpu.VMEM)],
        out_shape=[_sds(a.shape, a.dtype) for a in lands] + [_sds((8, 128))],
        input_output_aliases={n + t: t for t in range(n)},
        scratch_shapes=[pltpu.SemaphoreType.DMA((4 * n,)), pltpu.SemaphoreType.DMA((4 * n,))],
    )(*blocks, *lands)
    return outs[:n], outs[n]


@jax.custom_vjp
def _build_w_in(w4):
    full = w4.reshape(-1, w4.shape[-1])
    parts = []
    for _, start, width, wp in PROJ_SEGS:
        if width:
            parts.append(full[start:start + width])
        if wp > width:
            parts.append(jnp.zeros((wp - width, full.shape[1]), full.dtype))
    return jnp.concatenate(parts, axis=0)


def _build_w_in_f(w4):
    return _build_w_in(w4), None


def _build_w_in_b(_, g):
    parts, at = [], 0
    for _, _, width, wp in PROJ_SEGS:
        if width:
            parts.append(g[at:at + width])
        at += wp
    return (jnp.concatenate(parts, axis=0).reshape(N_CHIPS, -1, g.shape[1]),)


_build_w_in.defvjp(_build_w_in_f, _build_w_in_b)


def _split_w_uq(w):
    w3 = w.reshape(w.shape[0], MLA_HEADS, MLA_NOPE + MLA_ROPE)
    return w3[:, :, :MLA_NOPE].reshape(w.shape[0], -1), w3[:, :, MLA_NOPE:].reshape(w.shape[0], -1)


def _swap_halves(t, width):
    t3 = t.reshape(t.shape[0], -1, 2, width // 2)
    return jnp.concatenate([t3[:, :, 1:], t3[:, :, :1]], axis=2).reshape(t.shape)


def _pad_heads(t, width):
    t3 = t.reshape(t.shape[0], -1, width)
    t3 = jnp.pad(t3, ((0, 0), (0, 0), (0, HEAD_LANES - width)))
    return t3.reshape(t.shape[0], -1).astype(BF16)


def _layer(xh, mod, big, small, rope_q, rope_k):
    d = D_MODEL
    shift, scale, gate = mod[None, 0:d], mod[None, d:2 * d], mod[None, 2 * d:3 * d]
    w_al = _build_w_in(big["w_in"])
    proj = mod_mm(xh, small["norm_g"][None], scale, shift, w_al)
    gq, gk, gv, glr, mq, mkv, mkr, cb, cc, cx, _, z = split_proj(proj)

    rk = GLA_RANK
    hk = GLA_HEADS * GLA_DK
    wg = jnp.zeros((128, 2 * hk), F32)
    wg = wg.at[0:rk, 0:hk].set(small["gla_wg_f"]).at[rk:2 * rk, hk:].set(small["gla_wg_b"])
    bg = jnp.concatenate([small["gla_bg_f"], small["gla_bg_b"]])[None]
    la = gate_act(mm(glr, wg), bg)
    o_gla = rmsnorm(gla(gq, gk, gv, la), small["gla_norm_g"][None])

    cq = rmsnorm(mq, small["mla_q_norm_g"][None])
    w_nope, w_rope = _split_w_uq(jnp.concatenate([big["w_uq"][j] for j in range(N_CHIPS)], axis=1))
    qn = mm16(cq, w_nope)
    qr = mm(cq, w_rope)
    qr = fma(qr, rope_q[0], _swap_halves(qr, MLA_ROPE), rope_q[1])
    ckv = rmsnorm(mkv, small["mla_kv_norm_g"][None])
    kv = mm16(ckv, jnp.concatenate([big["w_ukv"][j] for j in range(N_CHIPS)], axis=1))
    kr = mkr[:, :MLA_ROPE]
    kr = fma(kr, rope_k[0], _swap_halves(kr, MLA_ROPE), rope_k[1])
    o_mla = rmsnorm(attn(qn, _pad_heads(qr, MLA_ROPE), kv, _pad_heads(kr, MLA_ROPE)), small["mla_out_g"][None])

    cw = jnp.concatenate([small["conv_w"], jnp.zeros((5, CONV_CH), F32)], axis=0)
    o_conv = rmsnorm(conv_op(cb, cc, cx, cw), small["conv_out_g"][None])

    o = jnp.concatenate([o_gla, o_mla, o_conv], axis=1)
    w_out = big["w_out"].reshape(d, d)
    return out_block(o, z, w_out, xh, gate)


SMALL_REPL = ("norm_g", "gla_bg_f", "gla_bg_b", "gla_norm_g", "mla_q_norm_g", "mla_kv_norm_g", "mla_out_g",
              "conv_out_g")
SMALL_SHARDED = ("gla_wg_f", "gla_wg_b", "conv_w")
BIG = ("w_in", "w_out", "w_uq", "w_ukv")
HALF_AXIS = (1, 0, 0, 0)


def kernel(x, c, positions, ada_w, ada_b, norm_g, w_in, gla_wg_f, gla_bg_f, gla_wg_b, gla_bg_b, gla_norm_g, mla_q_norm_g, mla_kv_norm_g, mla_w_uq, mla_w_ukv, mla_out_g, conv_w, conv_out_g, w_out, final_g, loss_target, m_ada_w, m_ada_b, m_norm_g, m_w_in, m_gla_wg_f, m_gla_bg_f, m_gla_wg_b, m_gla_bg_b, m_gla_norm_g, m_mla_q_norm_g, m_mla_kv_norm_g, m_mla_w_uq, m_mla_w_ukv, m_mla_out_g, m_conv_w, m_conv_out_g, m_w_out, m_final_g, v_ada_w, v_ada_b, v_norm_g, v_w_in, v_gla_wg_f, v_gla_bg_f, v_gla_wg_b, v_gla_bg_b, v_gla_norm_g, v_mla_q_norm_g, v_mla_kv_norm_g, v_mla_w_uq, v_mla_w_ukv, v_mla_out_g, v_conv_w, v_conv_out_g, v_w_out, v_final_g):
    xi, yi, ci = _place()
    chip = 2 * xi + yi
    dev = 2 * chip + ci
    s = x.shape[1]
    d = D_MODEL
    weights = dict(ada_w=ada_w, ada_b=ada_b, norm_g=norm_g, w_in=w_in, gla_wg_f=gla_wg_f, gla_bg_f=gla_bg_f,
                   gla_wg_b=gla_wg_b, gla_bg_b=gla_bg_b, gla_norm_g=gla_norm_g, mla_q_norm_g=mla_q_norm_g,
                   mla_kv_norm_g=mla_kv_norm_g, mla_w_uq=mla_w_uq, mla_w_ukv=mla_w_ukv, mla_out_g=mla_out_g,
                   conv_w=conv_w, conv_out_g=conv_out_g, w_out=w_out, final_g=final_g)
    m_in = dict(ada_w=m_ada_w, ada_b=m_ada_b, norm_g=m_norm_g, w_in=m_w_in, gla_wg_f=m_gla_wg_f, gla_bg_f=m_gla_bg_f,
                gla_wg_b=m_gla_wg_b, gla_bg_b=m_gla_bg_b, gla_norm_g=m_gla_norm_g, mla_q_norm_g=m_mla_q_norm_g,
                mla_kv_norm_g=m_mla_kv_norm_g, mla_w_uq=m_mla_w_uq, mla_w_ukv=m_mla_w_ukv, mla_out_g=m_mla_out_g,
                conv_w=m_conv_w, conv_out_g=m_conv_out_g, w_out=m_w_out, final_g=m_final_g)
    v_in = dict(ada_w=v_ada_w, ada_b=v_ada_b, norm_g=v_norm_g, w_in=v_w_in, gla_wg_f=v_gla_wg_f, gla_bg_f=v_gla_bg_f,
                gla_wg_b=v_gla_wg_b, gla_bg_b=v_gla_bg_b, gla_norm_g=v_gla_norm_g, mla_q_norm_g=v_mla_q_norm_g,
                mla_kv_norm_g=v_mla_kv_norm_g, mla_w_uq=v_mla_w_uq, mla_w_ukv=v_mla_w_ukv, mla_out_g=v_mla_out_g,
                conv_w=v_conv_w, conv_out_g=v_conv_out_g, w_out=v_w_out, final_g=v_final_g)

    g_c, g_wgf, g_wgb, g_cw = all_gather8([c, gla_wg_f, gla_wg_b, conv_w], "gather_small")

    def unshard_cols(g):
        g4 = g[0::2]
        return g4.transpose(1, 2, 0, 3).reshape(g4.shape[1], g4.shape[2], -1)

    small_full = dict(gla_wg_f=unshard_cols(g_wgf), gla_wg_b=unshard_cols(g_wgb), conv_w=unshard_cols(g_cw))
    for nme in SMALL_REPL:
        small_full[nme] = weights[nme]
    smalls = [{nme: small_full[nme][l] for nme in SMALL_REPL + SMALL_SHARDED} for l in range(DEPTH)]

    big_src = (jnp.swapaxes(w_in, 1, 2), w_out, mla_w_uq, mla_w_ukv)

    def my_halves(l, zero=0):
        out = []
        for t, a in enumerate(big_src):
            n_half = a.shape[1 + HALF_AXIS[t]] // 2
            out.append(lax.dynamic_slice_in_dim(a[l], ci * n_half + zero, n_half, axis=HALF_AXIS[t]).astype(BF16))
        return out

    def landing(blocks):
        return [lax.empty((2, N_CHIPS) + b.shape, b.dtype) for b in blocks]

    def finish_gather(handle, after, tag):
        blocks, lands = ici_wait(handle, after, "gather", "gather_wait" + tag)
        lands, done = gather_share(blocks, lands, "gather_share" + tag)
        full = [lax.dynamic_update_slice(g, b[None, None], (ci, chip) + (0,) * b.ndim) for g, b in zip(lands, blocks)]
        return full, done

    halves0 = my_halves(0)
    started0 = ici_start(halves0, landing(halves0), "gather", "gather_start0")

    c_act = _silu_rows(g_c[:, 0, :])
    c_act16 = jnp.concatenate([c_act, jnp.zeros_like(c_act)], axis=0)
    n_ada = ada_w.shape[2]
    parts = []
    for l in range(DEPTH):
        bias = lax.dynamic_slice_in_dim(ada_b[l], chip * n_ada, n_ada)[None]
        parts.append(_mm(c_act16, ada_w[l], bias=bias, name="ada_fwd"))
    g_mod, = all_gather8([jnp.stack(parts)], "gather_mod")
    mod_mine = lax.dynamic_index_in_dim(g_mod[0::2], dev, 2, keepdims=False)
    mods = mod_mine.transpose(1, 0, 2).reshape(DEPTH, 3 * d)

    inv_freq = ROPE_THETA ** (-jnp.arange(0, MLA_ROPE, 2, dtype=F32) / MLA_ROPE)
    ang = positions[0].astype(F32)[:, None] * inv_freq
    cos, sin = jnp.cos(ang), jnp.sin(ang)
    rope_k = (jnp.concatenate([cos, cos], axis=1), jnp.concatenate([-sin, sin], axis=1))
    rope_q = (jnp.tile(rope_k[0], (1, MLA_HEADS)), jnp.tile(rope_k[1], (1, MLA_HEADS)))

    def run_layer(xh, mod, gathered, small):
        big = {nme: jnp.concatenate([g[0], g[1]], axis=HALF_AXIS[t] + 1) for t, (nme, g) in enumerate(zip(BIG, gathered))}
        return _layer(xh, mod, big, small, rope_q, rope_k)

    def head(hh, fg):
        return loss_op(rmsnorm(hh, fg[None]), loss_target[0])[0, 0]

    gathered0, done0 = finish_gather(started0, mods, "0")
    halves1 = my_halves(1, done0[0, 0].astype(jnp.int32))
    started1 = ici_start(halves1, landing(halves1), "gather", "gather_start1")
    h1, vjp0 = jax.vjp(run_layer, x[0], mods[0] + started1["token"][0, 0], gathered0, smalls[0])
    gathered1, _ = finish_gather(started1, h1, "1")
    h2, vjp1 = jax.vjp(run_layer, h1, mods[1], gathered1, smalls[1])
    loss_dev, vjp_head = jax.vjp(head, h2, final_g)
    dh2, dfinal = vjp_head(jnp.ones((), F32))

    c_idx = jnp.reshape(ci, (1,)).astype(jnp.int32)
    chip_idx = jnp.reshape(chip, (1,)).astype(jnp.int32)

    def reduce_begin(dgath, tag, zero=None):
        theirs = sibling_send(dgath, "reduce_sibling" + tag)
        pair = [_add_pair(a, b, c_idx) for a, b in zip(dgath, theirs)]
        shapes = [(N_CHIPS - 1,) + p.shape[1:] for p in pair]
        if zero is None:
            lands = [lax.empty(shp, BF16) for shp in shapes]
        else:
            lands = [jnp.broadcast_to(zero.astype(BF16), shp) for shp in shapes]
        return ici_start(pair, lands, "reduce", "reduce_start" + tag)

    def reduce_end(handle, after, tag):
        pair, landed = ici_wait(handle, after, "reduce", "reduce_wait" + tag)
        reduced = [_add_chips(p, q, chip_idx) for p, q in zip(pair, landed)]
        others = sibling_swap(reduced, "share_sibling" + tag)
        return [jnp.where(ci == 0, jnp.concatenate([own, other], axis=HALF_AXIS[t]),
                          jnp.concatenate([other, own], axis=HALF_AXIS[t]))
                for t, (own, other) in enumerate(zip(reduced, others))]

    dh1, dmod1, dgath1, dsmall1 = vjp1(dh2)
    reducing1 = reduce_begin(dgath1, "1")
    dx, dmod0, dgath0, dsmall0 = vjp0(dh1 + reducing1["token"][0, 0])
    dmods = jnp.stack([dmod0, dmod1])
    dsmalls = [dsmall0, dsmall1]

    pieces = [dmods.reshape(-1), dfinal]
    for nme in SMALL_REPL + SMALL_SHARDED:
        pieces.append(jnp.stack([dsmalls[l][nme] for l in range(DEPTH)]).reshape(-1))
    pieces.append(loss_dev.reshape(1))
    sizes = [p.shape[0] for p in pieces]
    flat = jnp.concatenate(pieces)
    padn = (-flat.shape[0]) % 128
    flat = jnp.concatenate([flat, jnp.zeros((padn,), F32)])[None]
    g_small, = all_gather8([flat], "gather_small_grads")
    total, small_done = _sum_devices(g_small)
    total = total[0]
    reducing0 = reduce_begin(dgath0, "0", small_done[0, 0])
    offs, at = [], 0
    for n_el in sizes:
        offs.append(at)
        at += n_el

    def piece(i, shape):
        return total[offs[i]:offs[i] + sizes[i]].reshape(shape)

    grads = {"ada_b": piece(0, (DEPTH, 3 * d)), "final_g": piece(1, (d,))}
    loss = piece(len(pieces) - 1, ())
    for i, nme in enumerate(SMALL_REPL + SMALL_SHARDED):
        full = piece(2 + i, small_full[nme].shape)
        if nme in SMALL_SHARDED:
            ncol = weights[nme].shape[2]
            full = lax.dynamic_slice_in_dim(full, chip * ncol, ncol, axis=2)
        grads[nme] = full

    dmod_all = g_small[:, 0, :DEPTH * 3 * d].reshape(N_DEV, DEPTH, 3 * d)
    dmod_cols = lax.dynamic_slice_in_dim(dmod_all, chip * n_ada, n_ada, axis=2)
    g_ada = []
    for l in range(DEPTH):
        dm16 = jnp.concatenate([dmod_cols[:, l], jnp.zeros((N_DEV, n_ada), F32)], axis=0)
        dm16 = dm16 + reducing0["token"][0, 0]
        g_ada.append(_mm(c_act16, dm16, ta=True, name="ada_bwd"))
    grads["ada_w"] = jnp.stack(g_ada)

    order = list(weights)
    big_names = ("w_in", "w_out", "mla_w_uq", "mla_w_ukv")
    delta, new_m, new_v = {}, {}, {}
    for nme in order:
        if nme not in big_names:
            delta[nme], new_m[nme], new_v[nme] = _adamw(weights[nme], grads[nme], m_in[nme], v_in[nme])

    def first(t):
        return t[(slice(0, 1),) * t.ndim].reshape(1)

    big_grads1 = reduce_end(reducing1, jnp.concatenate([first(dx), first(reducing0["token"])]), "1")
    done = [first(delta[nme]) for nme in order if nme not in big_names] + [first(g) for g in big_grads1]
    big_grads0 = reduce_end(reducing0, jnp.concatenate(done), "0")
    for nme, g0, g1 in zip(big_names, big_grads0, big_grads1):
        grads[nme] = jnp.stack([g0, g1])
    for nme in big_names:
        if nme == "w_in":
            w_t, m_t, v_t = (jnp.swapaxes(t, 1, 2) for t in (w_in, m_w_in, v_w_in))
            res = _adamw(w_t, grads[nme], m_t, v_t)
            delta[nme], new_m[nme], new_v[nme] = (jnp.swapaxes(t, 1, 2) for t in res)
            grads[nme] = jnp.swapaxes(grads[nme], 1, 2)
            continue
        delta[nme], new_m[nme], new_v[nme] = _adamw(weights[nme], grads[nme], m_in[nme], v_in[nme])
    return (loss, dx[None], *[grads[n_] for n_ in order], *[delta[n_] for n_ in order],
            *[new_m[n_] for n_ in order], *[new_v[n_] for n_ in order])
```

```python
import functools

import jax
import jax.numpy as jnp
from jax import lax
from jax.experimental import pallas as pl
from jax.experimental.pallas import tpu as pltpu

F32 = jnp.float32
BF16 = jnp.bfloat16
MESH = pl.DeviceIdType.MESH
HIGHEST = lax.Precision.HIGHEST

DEPTH = 2
D_MODEL = 2048
GLA_HEADS = 6
GLA_DK = 64
GLA_DV = 128
GLA_RANK = 16
GLA_TEMP = 16.0
GLA_CHUNK = 64
GLA_W = GLA_HEADS * GLA_DV
MLA_HEADS = 6
MLA_QL = 384
MLA_KVL = 256
MLA_NOPE = 128
MLA_ROPE = 64
MLA_DV = 128
MLA_W = MLA_HEADS * MLA_DV
CONV_CH = D_MODEL - GLA_W - MLA_W
ROPE_THETA = 10000.0
EPS = 1e-6
IN_DIM = 5856
N_CHIPS = 4
N_DEV = 8

ADAM_LR = 0.001
ADAM_B1 = 0.9
ADAM_B2 = 0.999
ADAM_EPS = 1e-08
ADAM_WD = 0.01
ADAM_STEP = 10

PROJ_SEGS = (
    ("gq", 0, 384, 384), ("gk", 384, 384, 384), ("gv", 768, 768, 768), ("glr", 1536, 32, 128),
    ("mq", 1568, 384, 384), ("mkv", 1952, 256, 256), ("mkr", 2208, 64, 128),
    ("cb", 2272, 512, 512), ("cc", 2784, 512, 512), ("cx", 3296, 512, 512),
    ("pad", 3808, 0, 128), ("z", 3808, 2048, 2048),
)
PROJ_AL = sum(s[3] for s in PROJ_SEGS)

ANY = pl.BlockSpec(memory_space=pl.ANY)
VMEM_LIMIT = 48 * 1024 * 1024
BLOCK_BYTES = 2 * 1024 * 1024


def _params(sem=None):
    return pltpu.CompilerParams(dimension_semantics=sem, vmem_limit_bytes=VMEM_LIMIT)


def _dot(a, b, ca, cb, precision=None):
    return lax.dot_general(a, b, (((ca,), (cb,)), ((), ())), preferred_element_type=F32, precision=precision)


def _tile(dim, prefs):
    for t in prefs:
        if dim % t == 0:
            return t
    return dim


def _pick_rows(rows, width, itemsize=4):
    for t in (2048, 1024, 512, 256, 128, 64, 32, 16, 8):
        if rows % t == 0 and t * width * itemsize <= BLOCK_BYTES:
            return t
    return rows


def _mm(a, b, *, ta=False, tb=False, bias=None, out_dtype=F32, name="mm"):
    if ta:
        K, M = a.shape
    else:
        M, K = a.shape
    if tb:
        N, Kb = b.shape
    else:
        Kb, N = b.shape
    assert K == Kb, (a.shape, b.shape, ta, tb)
    tm = _tile(M, (2048, 1024, 512, 256, 128))
    tn = _tile(N, (512, 384, 256, 128) if tm >= 2048 else (1024, 512, 384, 256, 128))
    tk = _tile(K, (2048, 1024, 512, 256, 128))
    nk = K // tk
    has_bias = bias is not None

    def body(*refs):
        a_ref, b_ref = refs[0], refs[1]
        bias_ref = refs[2] if has_bias else None
        o_ref = refs[3 if has_bias else 2]
        part = _dot(a_ref[...].astype(BF16), b_ref[...].astype(BF16), 0 if ta else 1, 1 if tb else 0)

        def finish(r):
            if has_bias:
                r = r + bias_ref[...]
            o_ref[...] = r.astype(out_dtype)

        if nk == 1:
            finish(part)
            return
        acc_ref = refs[-1]
        k = pl.program_id(2)

        @pl.when(k == 0)
        def _():
            acc_ref[...] = part

        @pl.when(k != 0)
        def _():
            acc_ref[...] += part

        @pl.when(k == nk - 1)
        def _():
            finish(acc_ref[...])

    a_spec = pl.BlockSpec((tk, tm), lambda i, j, k: (k, i)) if ta else pl.BlockSpec((tm, tk), lambda i, j, k: (i, k))
    b_spec = pl.BlockSpec((tn, tk), lambda i, j, k: (j, k)) if tb else pl.BlockSpec((tk, tn), lambda i, j, k: (k, j))
    in_specs = [a_spec, b_spec]
    args = [a, b]
    if has_bias:
        in_specs.append(pl.BlockSpec((1, tn), lambda i, j, k: (0, j)))
        args.append(bias)
    return pl.pallas_call(
        body, name=name, grid=(M // tm, N // tn, nk),
        in_specs=in_specs, out_specs=pl.BlockSpec((tm, tn), lambda i, j, k: (i, j)),
        out_shape=jax.ShapeDtypeStruct((M, N), out_dtype),
        scratch_shapes=[pltpu.VMEM((tm, tn), F32)] if nk > 1 else [],
        compiler_params=_params(("parallel", "parallel", "arbitrary")),
    )(*args)


@jax.custom_vjp
def mm(a, b):
    return _mm(a, b, name="mm_fwd")


def _mm_f(a, b):
    return _mm(a, b, name="mm_fwd"), (a, b)


def _mm_b(res, g):
    a, b = res
    return _mm(g, b, tb=True, out_dtype=a.dtype, name="mm_da"), _mm(a, g, ta=True, out_dtype=b.dtype, name="mm_db")


mm.defvjp(_mm_f, _mm_b)


@jax.custom_vjp
def mm16(a, b):
    return _mm(a, b, out_dtype=BF16, name="mm16_fwd")


def _mm16_f(a, b):
    return mm16(a, b), (a, b)


mm16.defvjp(_mm16_f, _mm_b)


def _rows(body, name, tiled, full, tiled_out, acc_out, tr=None):
    rows = tiled[0].shape[0]
    if tr is None:
        width = max([a.shape[1] for a in tiled] + [s.shape[1] for s in tiled_out])
        tr = _pick_rows(rows, width)
    in_specs = [pl.BlockSpec((tr, a.shape[1]), lambda i: (i, 0)) for a in tiled]
    in_specs += [pl.BlockSpec(a.shape, lambda i: (0, 0)) for a in full]
    out_specs = [pl.BlockSpec((tr, s.shape[1]), lambda i: (i, 0)) for s in tiled_out]
    out_specs += [pl.BlockSpec(s.shape, lambda i: (0, 0)) for s in acc_out]

    def wrapped(*refs):
        body(pl.program_id(0), *refs)

    outs = pl.pallas_call(
        wrapped, name=name, grid=(rows // tr,), in_specs=in_specs, out_specs=out_specs,
        out_shape=list(tiled_out) + list(acc_out),
        compiler_params=_params(("arbitrary",)),
    )(*tiled, *full)
    return outs


def _sds(shape, dtype=F32):
    return jax.ShapeDtypeStruct(tuple(shape), dtype)


def _acc(step, ref, val):
    @pl.when(step == 0)
    def _():
        ref[...] = val

    @pl.when(step != 0)
    def _():
        ref[...] += val


def _colsum(v):
    return jnp.sum(v, axis=0, keepdims=True)


def _rstd(x):
    return lax.rsqrt(jnp.mean(x * x, axis=-1, keepdims=True) + EPS)


def _norm_grid(x, g):
    rows, w = x.shape[0], g.shape[1]
    tr = _pick_rows(rows, w)
    blk = pl.BlockSpec((tr, w), lambda i, j: (i, j))
    gblk = pl.BlockSpec((1, w), lambda i, j: (0, 0))
    return (rows // tr, x.shape[1] // w), blk, gblk


@jax.custom_vjp
def rmsnorm(x, g):
    def body(x_ref, g_ref, o_ref):
        x = x_ref[...]
        o_ref[...] = x * _rstd(x) * g_ref[...]

    grid, blk, gblk = _norm_grid(x, g)
    return pl.pallas_call(body, name="rmsnorm_fwd", grid=grid, in_specs=[blk, gblk], out_specs=blk,
                          out_shape=_sds(x.shape), compiler_params=_params(("parallel", "parallel")))(x, g)


def _rmsnorm_f(x, g):
    return rmsnorm(x, g), (x, g)


def _rmsnorm_b(res, dy):
    x, g = res

    def body(x_ref, dy_ref, g_ref, dx_ref, dg_ref):
        x = x_ref[...]
        dy = dy_ref[...]
        r = _rstd(x)
        xh = x * r
        dxh = dy * g_ref[...]
        dx_ref[...] = r * (dxh - xh * jnp.mean(dxh * xh, axis=-1, keepdims=True))
        first = jnp.logical_and(pl.program_id(0) == 0, pl.program_id(1) == 0)
        _acc(jnp.where(first, 0, 1), dg_ref, _colsum(dy * xh))

    grid, blk, gblk = _norm_grid(x, g)
    dx, dg = pl.pallas_call(body, name="rmsnorm_bwd", grid=grid, in_specs=[blk, blk, gblk], out_specs=[blk, gblk],
                            out_shape=[_sds(x.shape), _sds(g.shape)],
                            compiler_params=_params(("arbitrary", "arbitrary")))(x, dy, g)
    return dx, dg


rmsnorm.defvjp(_rmsnorm_f, _rmsnorm_b)


def _modulate(x, g, scale, shift):
    def body(i, x_ref, g_ref, sc_ref, sh_ref, o_ref):
        x = x_ref[...]
        xn = x * _rstd(x) * g_ref[...]
        o_ref[...] = (xn * (1.0 + sc_ref[...]) + sh_ref[...]).astype(BF16)
    return _rows(body, "modulate_fwd", [x], [g, scale, shift], [_sds(x.shape, BF16)], [])[0]


def _modulate_bwd(x, g, scale, shift, dh):
    def body(i, x_ref, dh_ref, g_ref, sc_ref, dx_ref, dg_ref, dsc_ref, dsh_ref):
        x = x_ref[...]
        dh = dh_ref[...]
        gv = g_ref[...]
        r = _rstd(x)
        xh = x * r
        dxn = dh * (1.0 + sc_ref[...])
        dxh = dxn * gv
        dx_ref[...] = r * (dxh - xh * jnp.mean(dxh * xh, axis=-1, keepdims=True))
        _acc(i, dg_ref, _colsum(dxn * xh))
        _acc(i, dsc_ref, _colsum(dh * (xh * gv)))
        _acc(i, dsh_ref, _colsum(dh))

    v = _sds(g.shape)
    return _rows(body, "modulate_bwd", [x, dh], [g, scale], [_sds(x.shape)], [v, v, v])


@jax.custom_vjp
def mod_mm(x, g, scale, shift, wt):
    return _mm(_modulate(x, g, scale, shift), wt, tb=True, name="mm_in")


def _mod_mm_f(x, g, scale, shift, wt):
    h = _modulate(x, g, scale, shift)
    return _mm(h, wt, tb=True, name="mm_in"), (x, g, scale, shift, wt, h)


def _mod_mm_b(res, dproj):
    x, g, scale, shift, wt, h = res
    dproj = dproj.astype(BF16)
    dh = _mm(dproj, wt, name="mm_in_dh")
    dwt = _mm(dproj, h, ta=True, out_dtype=wt.dtype, name="mm_in_dw")
    dx, dg, dsc, dsh = _modulate_bwd(x, g, scale, shift, dh)
    return dx, dg, dsc, dsh, dwt


mod_mm.defvjp(_mod_mm_f, _mod_mm_b)


def _sigmoid(z):
    return 1.0 / (1.0 + jnp.exp(-z))


def _gate_mul(o, z):
    def body(i, o_ref, z_ref, y_ref):
        z = z_ref[...]
        y_ref[...] = (o_ref[...] * (z * _sigmoid(z))).astype(BF16)
    return _rows(body, "gate_mul_fwd", [o, z], [], [_sds(o.shape, BF16)], [])[0]


def _gate_mul_bwd(o, z, dy):
    def body(i, o_ref, z_ref, dy_ref, do_ref, dz_ref):
        z = z_ref[...]
        dy = dy_ref[...]
        s = _sigmoid(z)
        do_ref[...] = dy * (z * s)
        dz_ref[...] = dy * o_ref[...] * (s * (1.0 + z * (1.0 - s)))
    return _rows(body, "gate_mul_bwd", [o, z, dy], [], [_sds(o.shape), _sds(o.shape)], [])


def _residual(x, u, gate):
    def body(i, x_ref, u_ref, g_ref, o_ref):
        o_ref[...] = x_ref[...] + g_ref[...] * u_ref[...]
    return _rows(body, "residual_fwd", [x, u], [gate], [_sds(x.shape)], [])[0]


def _residual_bwd(d, u, gate):
    def body(i, d_ref, u_ref, g_ref, du_ref, dg_ref):
        d = d_ref[...]
        du_ref[...] = (g_ref[...] * d).astype(BF16)
        _acc(i, dg_ref, _colsum(d * u_ref[...]))

    return _rows(body, "residual_bwd", [d, u], [gate], [_sds(u.shape, BF16)], [_sds(gate.shape)])


@jax.custom_vjp
def out_block(o, z, w, x, gate):
    return _residual(x, _mm(_gate_mul(o, z), w, name="mm_out"), gate)


def _out_block_f(o, z, w, x, gate):
    y = _gate_mul(o, z)
    u = _mm(y, w, name="mm_out")
    return _residual(x, u, gate), (o, z, w, y, u, gate)


def _out_block_b(res, d):
    o, z, w, y, u, gate = res
    du, dgate = _residual_bwd(d, u, gate)
    dy = _mm(du, w, tb=True, name="mm_out_dy")
    dw = _mm(y, du, ta=True, out_dtype=w.dtype, name="mm_out_dw")
    do, dz = _gate_mul_bwd(o, z, dy)
    return do, dz, dw, d, dgate


out_block.defvjp(_out_block_f, _out_block_b)


@jax.custom_vjp
def gate_act(u, b):
    def body(i, u_ref, b_ref, o_ref):
        t = u_ref[...] + b_ref[...]
        o_ref[...] = (jnp.minimum(t, 0.0) - jnp.log(1.0 + jnp.exp(-jnp.abs(t)))) / GLA_TEMP
    return _rows(body, "gate_act_fwd", [u], [b], [_sds(u.shape)], [])[0]


def _gate_act_f(u, b):
    return gate_act(u, b), (u, b)


def _gate_act_b(res, d):
    u, b = res

    def body(i, u_ref, d_ref, b_ref, du_ref, db_ref):
        t = u_ref[...] + b_ref[...]
        du = d_ref[...] * _sigmoid(-t) / GLA_TEMP
        du_ref[...] = du
        _acc(i, db_ref, _colsum(du))

    du, db = _rows(body, "gate_act_bwd", [u, d], [b], [_sds(u.shape)], [_sds(b.shape)])
    return du, db


gate_act.defvjp(_gate_act_f, _gate_act_b)


@jax.custom_vjp
def fma(a, b, c, d):
    def body(i, a_ref, b_ref, c_ref, d_ref, o_ref):
        o_ref[...] = a_ref[...] * b_ref[...] + c_ref[...] * d_ref[...]
    return _rows(body, "fma_fwd", [a, b, c, d], [], [_sds(a.shape)], [])[0]


def _fma_f(a, b, c, d):
    return fma(a, b, c, d), (b, d)


def _fma_b(res, g):
    b, d = res

    def body(i, g_ref, b_ref, d_ref, da_ref, dc_ref):
        g = g_ref[...]
        da_ref[...] = g * b_ref[...]
        dc_ref[...] = g * d_ref[...]

    da, dc = _rows(body, "fma_bwd", [g, b, d], [], [_sds(g.shape), _sds(g.shape)], [])
    return da, jnp.zeros_like(b), dc, jnp.zeros_like(d)


fma.defvjp(_fma_f, _fma_b)


def _silu_rows(c):
    def body(i, c_ref, o_ref):
        v = c_ref[...]
        o_ref[...] = v * _sigmoid(v)
    return _rows(body, "silu", [c], [], [_sds(c.shape)], [])[0]


@jax.custom_vjp
def loss_op(y, t):
    return _loss_fwd(y, t)[0]


def _loss_fwd(y, t):
    inv = 1.0 / y.shape[1]

    def body(i, y_ref, t_ref, d_ref, l_ref):
        e = y_ref[...] - t_ref[...]
        d_ref[...] = e * inv
        _acc(i, l_ref, jnp.sum(_colsum(e * e), axis=1, keepdims=True) * (0.5 * inv))

    d, l = _rows(body, "loss_fwd", [y, t], [], [_sds(y.shape)], [_sds((1, 1))])
    return l, d


def _loss_f(y, t):
    l, d = _loss_fwd(y, t)
    return l, d


def _loss_b(d, g):
    return d * g, jnp.zeros_like(d)


loss_op.defvjp(_loss_f, _loss_b)


def _conv_terms(cc, cx, rows, n):
    u = cc * cx
    up = jnp.where(rows == 0, 0.0, pltpu.roll(u, 1, 0))
    un = jnp.where(rows == n - 1, 0.0, pltpu.roll(u, n - 1, 0))
    return u, up, un


CONV_COLS = 128


def _conv_specs(s, n_in):
    blk = pl.BlockSpec((s, CONV_COLS), lambda j: (0, j))
    wblk = pl.BlockSpec((8, CONV_COLS), lambda j: (0, j))
    return [blk] * n_in + [wblk], blk, wblk


@jax.custom_vjp
def conv_op(cb, cc, cx, w):
    s, ch = cb.shape

    def body(cb_ref, cc_ref, cx_ref, w_ref, o_ref):
        rows = lax.broadcasted_iota(jnp.int32, (s, CONV_COLS), 0)
        u, up, un = _conv_terms(cc_ref[...], cx_ref[...], rows, s)
        conv = up * w_ref[0:1, :] + u * w_ref[1:2, :] + un * w_ref[2:3, :]
        o_ref[...] = cb_ref[...] * conv

    in_specs, blk, _ = _conv_specs(s, 3)
    return pl.pallas_call(
        body, name="conv_fwd", grid=(ch // CONV_COLS,), in_specs=in_specs, out_specs=blk,
        out_shape=_sds(cb.shape), compiler_params=_params(("parallel",)),
    )(cb, cc, cx, w)


def _conv_f(cb, cc, cx, w):
    return conv_op(cb, cc, cx, w), (cb, cc, cx, w)


def _conv_b(res, d):
    cb, cc, cx, w = res
    s, ch = cb.shape

    def body(cb_ref, cc_ref, cx_ref, d_ref, w_ref, dcb_ref, dcc_ref, dcx_ref, dw_ref):
        rows = lax.broadcasted_iota(jnp.int32, (s, CONV_COLS), 0)
        cc_v = cc_ref[...]
        cx_v = cx_ref[...]
        u, up, un = _conv_terms(cc_v, cx_v, rows, s)
        w0, w1, w2 = w_ref[0:1, :], w_ref[1:2, :], w_ref[2:3, :]
        dv = d_ref[...]
        dcb_ref[...] = dv * (up * w0 + u * w1 + un * w2)
        dconv = dv * cb_ref[...]
        d_next = jnp.where(rows == s - 1, 0.0, pltpu.roll(dconv, s - 1, 0))
        d_prev = jnp.where(rows == 0, 0.0, pltpu.roll(dconv, 1, 0))
        du = w0 * d_next + w1 * dconv + w2 * d_prev
        dcc_ref[...] = du * cx_v
        dcx_ref[...] = du * cc_v
        dw_ref[...] = jnp.zeros_like(dw_ref)
        dw_ref[0:1, :] = _colsum(dconv * up)
        dw_ref[1:2, :] = _colsum(dconv * u)
        dw_ref[2:3, :] = _colsum(dconv * un)

    in_specs, blk, wblk = _conv_specs(s, 4)
    v = _sds(cb.shape)
    return tuple(pl.pallas_call(
        body, name="conv_bwd", grid=(ch // CONV_COLS,), in_specs=in_specs, out_specs=[blk, blk, blk, wblk],
        out_shape=[v, v, v, _sds(w.shape)], compiler_params=_params(("parallel",)),
    )(cb, cc, cx, d, w))


conv_op.defvjp(_conv_f, _conv_b)


def _gla_masks(rev):
    c = GLA_CHUNK
    row = lax.broadcasted_iota(jnp.int32, (c, c), 0)
    col = lax.broadcasted_iota(jnp.int32, (c, c), 1)
    mask = (row < col) if rev else (row >= col)
    return rev, mask


def _chunk_cumsum(g, rev):
    c = g.shape[0]
    row = lax.broadcasted_iota(jnp.int32, g.shape, 0)
    b = g
    s = 1
    while s < c:
        if rev:
            b = b + jnp.where(row < c - s, pltpu.roll(b, c - s, 0), 0.0)
        else:
            b = b + jnp.where(row >= s, pltpu.roll(b, s, 0), 0.0)
        s *= 2
    return b


GLA_UNROLL = 4


def _gla_rows(n):
    return pl.ds(pl.multiple_of(n * GLA_CHUNK, GLA_CHUNK), GLA_CHUNK)


def _gla_scan(s_ref, bt_ref, st_ref, n_chunks, descending):
    st_ref[...] = jnp.zeros_like(st_ref)

    def step(i, carry):
        n = (n_chunks - 1 - i) if descending else i
        own = s_ref[n]
        st = st_ref[...]
        s_ref[n] = st
        st_ref[...] = st * jnp.exp(bt_ref[n]) + own
        return carry

    lax.fori_loop(0, n_chunks, step, 0)


GLA_PAIR = 2


def _gla_specs(s):
    dk, dv = GLA_DK, GLA_DV
    n_pairs = GLA_HEADS // GLA_PAIR
    blk_k = pl.BlockSpec((s, GLA_PAIR * dk), lambda p: (0, p))
    blk_gb = pl.BlockSpec((s, GLA_PAIR * dk), lambda p: (0, n_pairs + p))
    blk_v = pl.BlockSpec((s, GLA_PAIR * dv), lambda p: (0, p))
    return n_pairs, blk_k, blk_gb, blk_v


def _head_lanes(hh):
    lane = lax.broadcasted_iota(jnp.int32, (1, GLA_PAIR * GLA_DK), 1)
    return jnp.logical_and(lane >= hh * GLA_DK, lane < (hh + 1) * GLA_DK)


def _gla_fwd(q, k, v, la):
    s = q.shape[0]
    dk, dv = GLA_DK, GLA_DV
    pw = GLA_PAIR * dk
    n_chunks = s // GLA_CHUNK
    scale = GLA_DK ** -0.5

    def body(q_ref, k_ref, v_ref, gf_ref, gb_ref, o_ref, sf_ref, sb_ref, bf_ref, bb_ref, btf_ref, btb_ref, st_ref):
        masks = [_gla_masks(rev) for rev in (False, True)]
        dirs = ((False, gf_ref, sf_ref, bf_ref, btf_ref), (True, gb_ref, sb_ref, bb_ref, btb_ref))

        def decays(n, carry):
            rows = _gla_rows(n)
            for rev, g_ref, _, b_ref, bt_ref in dirs:
                g = g_ref[rows, :]
                b_ref[rows, :] = _chunk_cumsum(g, rev)
                bt_ref[n] = _colsum(g)
            return carry

        lax.fori_loop(0, n_chunks, decays, 0, unroll=GLA_UNROLL)
        for hh in range(GLA_PAIR):
            m = _head_lanes(hh)
            vl = slice(hh * dv, (hh + 1) * dv)

            def prepare(n, carry, m=m, vl=vl):
                rows = _gla_rows(n)
                kk = k_ref[rows, :]
                vb = v_ref[rows, vl].astype(BF16)
                for rev, _, s_ref, b_ref, bt_ref in dirs:
                    ke = jnp.where(m, kk * jnp.exp(bt_ref[n] - b_ref[rows, :]), 0.0).astype(BF16)
                    s_ref[n] = _dot(vb, ke, 0, 0)
                return carry

            lax.fori_loop(0, n_chunks, prepare, 0, unroll=GLA_UNROLL)
            for rev, _, s_ref, _, bt_ref in dirs:
                _gla_scan(s_ref, bt_ref, st_ref, n_chunks, descending=rev)

            def emit(n, carry, m=m, vl=vl):
                rows = _gla_rows(n)
                qs = q_ref[rows, :] * scale
                kk = k_ref[rows, :]
                vb = v_ref[rows, vl].astype(BF16)
                o = None
                for (rev, _, s_ref, b_ref, _), (_, mask) in zip(dirs, masks):
                    b = b_ref[rows, :]
                    qd = jnp.where(m, qs * jnp.exp(b), 0.0).astype(BF16)
                    ki = jnp.where(m, kk * jnp.exp(-b), 0.0).astype(BF16)
                    a = jnp.where(mask, _dot(qd, ki, 1, 1), 0.0).astype(BF16)
                    od = _dot(a, vb, 1, 0) + _dot(qd, s_ref[n].astype(BF16), 1, 1)
                    o = od if o is None else o + od
                o_ref[rows, vl] = o
                return carry

            lax.fori_loop(0, n_chunks, emit, 0, unroll=GLA_UNROLL)

    n_pairs, blk_k, blk_gb, blk_v = _gla_specs(s)
    state = pltpu.VMEM((n_chunks, dv, pw), F32)
    scratch = [state, state, pltpu.VMEM((s, pw), F32), pltpu.VMEM((s, pw), F32), pltpu.VMEM((n_chunks, 1, pw), F32),
               pltpu.VMEM((n_chunks, 1, pw), F32), pltpu.VMEM((dv, pw), F32)]
    return pl.pallas_call(
        body, name="gla_fwd", grid=(n_pairs,), in_specs=[blk_k, blk_k, blk_v, blk_k, blk_gb],
        out_specs=blk_v, out_shape=_sds(v.shape), scratch_shapes=scratch,
        compiler_params=_params(("parallel",)),
    )(q, k, v, la, la)


def _gla_bwd(q, k, v, la, do):
    s = q.shape[0]
    dk, dv = GLA_DK, GLA_DV
    pw = GLA_PAIR * dk
    c = GLA_CHUNK
    n_chunks = s // c
    scale = GLA_DK ** -0.5

    def body(q_ref, k_ref, v_ref, gf_ref, gb_ref, do_ref, dq_ref, dk_ref, dv_ref, dgf_ref, dgb_ref,
             sf_ref, sb_ref, bf_ref, bb_ref, btf_ref, btb_ref, dsf_ref, dsb_ref, st_ref):
        masks = [_gla_masks(rev) for rev in (False, True)]
        rowc = lax.broadcasted_iota(jnp.int32, (c, pw), 0)
        dirs = ((False, gf_ref, sf_ref, bf_ref, btf_ref, dsf_ref, dgf_ref),
                (True, gb_ref, sb_ref, bb_ref, btb_ref, dsb_ref, dgb_ref))

        def decays(n, carry):
            rows = _gla_rows(n)
            for rev, g_ref, _, b_ref, bt_ref, _, _ in dirs:
                g = g_ref[rows, :]
                b_ref[rows, :] = _chunk_cumsum(g, rev)
                bt_ref[n] = _colsum(g)
            return carry

        lax.fori_loop(0, n_chunks, decays, 0, unroll=GLA_UNROLL)
        for hh in range(GLA_PAIR):
            m = _head_lanes(hh)
            vl = slice(hh * dv, (hh + 1) * dv)

            def prepare(n, carry, m=m, vl=vl):
                rows = _gla_rows(n)
                qs = q_ref[rows, :] * scale
                kk = k_ref[rows, :]
                vb = v_ref[rows, vl].astype(BF16)
                do_b = do_ref[rows, vl].astype(BF16)
                for rev, _, s_ref, b_ref, bt_ref, ds_ref, _ in dirs:
                    b = b_ref[rows, :]
                    ke = jnp.where(m, kk * jnp.exp(bt_ref[n] - b), 0.0).astype(BF16)
                    qd = jnp.where(m, qs * jnp.exp(b), 0.0).astype(BF16)
                    s_ref[n] = _dot(vb, ke, 0, 0)
                    ds_ref[n] = _dot(do_b, qd, 0, 0)
                return carry

            lax.fori_loop(0, n_chunks, prepare, 0, unroll=GLA_UNROLL)
            for rev, _, s_ref, _, bt_ref, ds_ref, _ in dirs:
                _gla_scan(s_ref, bt_ref, st_ref, n_chunks, descending=rev)
                _gla_scan(ds_ref, bt_ref, st_ref, n_chunks, descending=not rev)

            def emit(n, carry, m=m, vl=vl, first=(hh == 0)):
                rows = _gla_rows(n)
                qs = q_ref[rows, :] * scale
                kk = k_ref[rows, :]
                vb = v_ref[rows, vl].astype(BF16)
                do_b = do_ref[rows, vl].astype(BF16)
                dq = dkk = dvv = None
                for (rev, _, s_ref, b_ref, bt_ref, ds_ref, dg_ref), (_, mask) in zip(dirs, masks):
                    b = b_ref[rows, :]
                    bt = bt_ref[n]
                    eb = jnp.where(m, jnp.exp(b), 0.0)
                    enb = jnp.where(m, jnp.exp(-b), 0.0)
                    etb = jnp.where(m, jnp.exp(bt - b), 0.0)
                    ebt = jnp.exp(bt)
                    qd = qs * eb
                    ki = kk * enb
                    ke = kk * etb
                    qd_b, ki_b, ke_b = qd.astype(BF16), ki.astype(BF16), ke.astype(BF16)
                    st = s_ref[n]
                    dst = ds_ref[n]
                    dst_b = dst.astype(BF16)
                    a = jnp.where(mask, _dot(qd_b, ki_b, 1, 1), 0.0).astype(BF16)
                    da = jnp.where(mask, _dot(do_b, vb, 1, 1), 0.0).astype(BF16)
                    dv_d = _dot(a, do_b, 0, 0) + _dot(ke_b, dst_b, 1, 1)
                    dqd = _dot(da, ki_b, 1, 0) + _dot(do_b, st.astype(BF16), 1, 0)
                    dki = _dot(da, qd_b, 0, 0)
                    dke = _dot(vb, dst_b, 1, 0)
                    dbt = _colsum(st * dst) * ebt + _colsum(dke * ke)
                    db = dqd * qd - dki * ki - dke * ke
                    db = db + jnp.where(rowc == (0 if rev else c - 1), dbt, 0.0)
                    dg = _chunk_cumsum(db, not rev)
                    if first:
                        dg_ref[rows, :] = dg
                    else:
                        dg_ref[rows, :] += dg
                    dq_d = dqd * eb * scale
                    dk_d = dki * enb + dke * etb
                    dq = dq_d if dq is None else dq + dq_d
                    dkk = dk_d if dkk is None else dkk + dk_d
                    dvv = dv_d if dvv is None else dvv + dv_d
                if first:
                    dq_ref[rows, :] = dq
                    dk_ref[rows, :] = dkk
                else:
                    dq_ref[rows, :] += dq
                    dk_ref[rows, :] += dkk
                dv_ref[rows, vl] = dvv
                return carry

            lax.fori_loop(0, n_chunks, emit, 0, unroll=2)

    n_pairs, blk_k, blk_gb, blk_v = _gla_specs(s)
    vk, vv = _sds(q.shape), _sds(v.shape)
    state = pltpu.VMEM((n_chunks, dv, pw), F32)
    scratch = [state, state, pltpu.VMEM((s, pw), F32), pltpu.VMEM((s, pw), F32), pltpu.VMEM((n_chunks, 1, pw), F32),
               pltpu.VMEM((n_chunks, 1, pw), F32), state, state, pltpu.VMEM((dv, pw), F32)]
    return pl.pallas_call(
        body, name="gla_bwd", grid=(n_pairs,), in_specs=[blk_k, blk_k, blk_v, blk_k, blk_gb, blk_v],
        out_specs=[blk_k, blk_k, blk_v, blk_k, blk_k], out_shape=[vk, vk, vv, vk, vk],
        scratch_shapes=scratch, compiler_params=_params(("parallel",)),
    )(q, k, v, la, la, do)


@jax.custom_vjp
def gla(q, k, v, la):
    return _gla_fwd(q, k, v, la)


def _gla_f(q, k, v, la):
    return _gla_fwd(q, k, v, la), (q, k, v, la)


def _gla_b(res, do):
    dq, dk, dv, dgf, dgb = _gla_bwd(*res, do)
    return dq, dk, dv, jnp.concatenate([dgf, dgb], axis=1)


gla.defvjp(_gla_f, _gla_b)


ATTN_TQ = 256
HEAD_LANES = 128


def _attn_blocks(s, tq):
    per_q = pl.BlockSpec((tq, HEAD_LANES), lambda h, j: (j, h))
    k_nope = pl.BlockSpec((s, HEAD_LANES), lambda h, j: (0, 2 * h))
    v_blk = pl.BlockSpec((s, HEAD_LANES), lambda h, j: (0, 2 * h + 1))
    k_rope = pl.BlockSpec((s, HEAD_LANES), lambda h, j: (0, 0))
    lse = pl.BlockSpec((1, tq, 1), lambda h, j: (h, j, 0))
    return per_q, k_nope, v_blk, k_rope, lse


def _attn_fwd(qn, qr, kv, kr):
    s = qn.shape[0]
    tq = min(ATTN_TQ, s)
    scale = (MLA_NOPE + MLA_ROPE) ** -0.5

    def body(qn_ref, qr_ref, kn_ref, v_ref, kr_ref, o_ref, lse_ref):
        q = jnp.concatenate([qn_ref[...], qr_ref[...]], axis=1)
        k = jnp.concatenate([kn_ref[...], kr_ref[...]], axis=1)
        sc = _dot(q, k, 1, 1) * scale
        m = jnp.max(sc, axis=-1, keepdims=True)
        p = jnp.exp(sc - m)
        l = jnp.sum(p, axis=-1, keepdims=True)
        p = p * (1.0 / l)
        o_ref[...] = _dot(p.astype(BF16), v_ref[...], 1, 0)
        lse_ref[0] = m + jnp.log(l)

    per_q, k_nope, v_blk, k_rope, lse = _attn_blocks(s, tq)
    return pl.pallas_call(
        body, name="attn_fwd", grid=(MLA_HEADS, s // tq), in_specs=[per_q, per_q, k_nope, v_blk, k_rope],
        out_specs=[per_q, lse], out_shape=[_sds(qn.shape), _sds((MLA_HEADS, s, 1))],
        compiler_params=_params(("parallel", "parallel")),
    )(qn, qr, kv, kv, kr)


def _attn_bwd(qn, qr, kv, kr, o, lse, do):
    s = qn.shape[0]
    tq = min(ATTN_TQ, s)
    n_q = s // tq
    scale = (MLA_NOPE + MLA_ROPE) ** -0.5

    def body(qn_ref, qr_ref, kn_ref, v_ref, kr_ref, o_ref, lse_ref, do_ref, dqn_ref, dqr_ref, dkv_ref, dkr_ref,
             dk_acc, dv_acc, dkr_acc):
        h, j = pl.program_id(0), pl.program_id(1)
        q = jnp.concatenate([qn_ref[...], qr_ref[...]], axis=1)
        k = jnp.concatenate([kn_ref[...], kr_ref[...]], axis=1)
        do = do_ref[...]
        do_b = do.astype(BF16)
        p = jnp.exp(_dot(q, k, 1, 1) * scale - lse_ref[0])
        dp = _dot(do_b, v_ref[...], 1, 1)
        delta = jnp.sum(do * o_ref[...], axis=-1, keepdims=True)
        ds = (p * (dp - delta) * scale).astype(BF16)
        dq = _dot(ds, k, 1, 0)
        dqn_ref[...] = dq[:, :HEAD_LANES].astype(BF16)
        dqr_ref[...] = dq[:, HEAD_LANES:].astype(BF16)
        dk = _dot(ds, q, 0, 0)
        _acc(j, dk_acc, dk[:, :HEAD_LANES])
        _acc(j, dv_acc, _dot(p.astype(BF16), do_b, 0, 0))
        _acc(jnp.where(jnp.logical_and(h == 0, j == 0), 0, 1), dkr_acc, dk[:, HEAD_LANES:])

        @pl.when(j == n_q - 1)
        def _():
            dkv_ref[:, 0:HEAD_LANES] = dk_acc[...].astype(BF16)
            dkv_ref[:, HEAD_LANES:2 * HEAD_LANES] = dv_acc[...].astype(BF16)

        @pl.when(jnp.logical_and(h == MLA_HEADS - 1, j == n_q - 1))
        def _():
            dkr_ref[...] = dkr_acc[...].astype(BF16)

    per_q, k_nope, v_blk, k_rope, lse_blk = _attn_blocks(s, tq)
    dkv_blk = pl.BlockSpec((s, 2 * HEAD_LANES), lambda h, j: (0, h))
    acc = pltpu.VMEM((s, HEAD_LANES), F32)
    return pl.pallas_call(
        body, name="attn_bwd", grid=(MLA_HEADS, n_q),
        in_specs=[per_q, per_q, k_nope, v_blk, k_rope, per_q, lse_blk, per_q],
        out_specs=[per_q, per_q, dkv_blk, k_rope],
        out_shape=[_sds(qn.shape, BF16), _sds(qr.shape, BF16), _sds(kv.shape, BF16), _sds(kr.shape, BF16)],
        scratch_shapes=[acc, acc, acc], compiler_params=_params(("arbitrary", "arbitrary")),
    )(qn, qr, kv, kv, kr, o, lse, do)


@jax.custom_vjp
def attn(qn, qr, kv, kr):
    return _attn_fwd(qn, qr, kv, kr)[0]


def _attn_f(qn, qr, kv, kr):
    o, lse = _attn_fwd(qn, qr, kv, kr)
    return o, (qn, qr, kv, kr, o, lse)


def _attn_b(res, do):
    return tuple(_attn_bwd(*res, do))


attn.defvjp(_attn_f, _attn_b)


@jax.custom_vjp
def split_proj(proj):
    out, at = [], 0
    for _, _, _, wp in PROJ_SEGS:
        out.append(proj[:, at:at + wp])
        at += wp
    return tuple(out)


def _split_f(proj):
    return split_proj(proj), None


def _split_b(_, gs):
    return (jnp.concatenate(gs, axis=1),)


split_proj.defvjp(_split_f, _split_b)


def _tile2d(rows, width, limit=BLOCK_BYTES):
    fits = [t for t in range(16, rows + 1, 16) if rows % t == 0 and t * width * 4 <= limit]
    if fits and (fits[-1] >= 64 or fits[-1] == rows):
        return fits[-1], width
    if rows * width * 4 <= limit:
        return rows, width
    cols = [t for t in range(128, width + 1, 128) if width % t == 0 and rows * t * 4 <= limit]
    return (rows, cols[-1]) if cols else (rows, width)


def _add_pair(stacked, theirs, c_idx):
    g, r, w = theirs.shape
    tr, tc = _tile2d(r, w)

    def body(c_ref, a_ref, b_ref, o_ref):
        o_ref[0] = (a_ref[0, 0].astype(F32) + b_ref[0].astype(F32)).astype(BF16)

    blk = pl.BlockSpec((1, tr, tc), lambda k, i, j, c: (k, i, j))
    spec = pltpu.PrefetchScalarGridSpec(
        num_scalar_prefetch=1, grid=(g, r // tr, w // tc),
        in_specs=[pl.BlockSpec((1, 1, tr, tc), lambda k, i, j, c: (c[0], k, i, j)), blk], out_specs=blk)
    return pl.pallas_call(body, name="add_pair", grid_spec=spec, out_shape=_sds(theirs.shape, BF16),
                          compiler_params=_params(("parallel", "parallel", "parallel")))(c_idx, stacked, theirs)


def _add_chips(pair, landed, chip_idx):
    _, r, w = pair.shape
    tr, tc = _tile2d(r, w)

    def body(c_ref, p_ref, l0_ref, l1_ref, l2_ref, o_ref):
        o_ref[...] = ((p_ref[0].astype(F32) + l0_ref[0].astype(F32)) + l1_ref[0].astype(F32)) + l2_ref[0].astype(F32)

    specs = [pl.BlockSpec((1, tr, tc), lambda i, j, c: (c[0], i, j))]
    specs += [pl.BlockSpec((1, tr, tc), functools.partial(lambda i, j, c, k: (k, i, j), k=k)) for k in range(N_CHIPS - 1)]
    spec = pltpu.PrefetchScalarGridSpec(num_scalar_prefetch=1, grid=(r // tr, w // tc), in_specs=specs,
                                        out_specs=pl.BlockSpec((tr, tc), lambda i, j, c: (i, j)))
    return pl.pallas_call(body, name="add_chips", grid_spec=spec, out_shape=_sds((r, w)),
                          compiler_params=_params(("parallel", "parallel")))(chip_idx, pair, landed, landed, landed)


def _sum_devices(g):
    n = g.shape[2]

    def body(g_ref, o_ref, done_ref):
        t = g_ref[0]
        for j in range(1, N_DEV):
            t = t + g_ref[j]
        o_ref[...] = t
        done_ref[...] = jnp.zeros_like(done_ref)

    return pl.pallas_call(body, name="sum_devices", out_shape=[_sds((1, n)), _sds((8, 128))],
                          compiler_params=_params())(g)


def _adamw_math(w, gv, m, v):
    c1 = 1.0 - ADAM_B1 ** ADAM_STEP
    c2 = 1.0 - ADAM_B2 ** ADAM_STEP
    mn = ADAM_B1 * m + (1.0 - ADAM_B1) * gv
    vn = ADAM_B2 * v + (1.0 - ADAM_B2) * (gv * gv)
    return -ADAM_LR * ((mn / c1) / (jnp.sqrt(vn / c2) + ADAM_EPS) + ADAM_WD * w), mn, vn


def _adamw(w, g, m, v):
    shp = w.shape
    shp3 = (1, 1, shp[0]) if len(shp) == 1 else (-1,) + tuple(shp[-2:])
    w3, g3, m3, v3 = (t.reshape(shp3) for t in (w, g, m, v))

    def body(w_ref, g_ref, m_ref, v_ref, d_ref, mo_ref, vo_ref):
        d_ref[...], mo_ref[...], vo_ref[...] = _adamw_math(w_ref[...], g_ref[...], m_ref[...], v_ref[...])

    nl, r, wd = w3.shape
    tr, tc = _tile2d(r, wd, BLOCK_BYTES // 2)
    blk = pl.BlockSpec((1, tr, tc), lambda l, i, j: (l, i, j))
    s3 = _sds(w3.shape)
    d, mn, vn = pl.pallas_call(
        body, name="adamw", grid=(nl, r // tr, wd // tc), in_specs=[blk] * 4, out_specs=[blk] * 3,
        out_shape=[s3, s3, s3], compiler_params=_params(("parallel", "parallel", "parallel")),
    )(w3, g3, m3, v3)
    return d.reshape(shp), mn.reshape(shp), vn.reshape(shp)


PIECE_BYTES = 1 << 20


def _place():
    return lax.axis_index("x"), lax.axis_index("y"), lax.axis_index("c")


def _pieces(shape, itemsize):
    if len(shape) >= 3:
        return [(i,) + p for i in range(shape[0]) for p in _pieces(shape[1:], itemsize)]
    rows = shape[0]
    row_bytes = itemsize
    for dsz in shape[1:]:
        row_bytes *= dsz
    k = 1
    while rows % (2 * k) == 0 and (rows // (2 * k)) % 16 == 0 and (rows // k) * row_bytes > PIECE_BYTES:
        k *= 2
    step = rows // k
    return [(pl.ds(j * step, step),) for j in range(k)]


def _split_start(make, src, dst, pieces):
    for p in pieces:
        make(src.at[p], dst.at[p]).start()
    return make(src, dst)


def _comm_call(body, name, arrs, out_shapes, n_remote, n_local):
    return pl.pallas_call(
        body, name=name, in_specs=[ANY] * len(arrs), out_specs=[ANY] * len(out_shapes), out_shape=out_shapes,
        scratch_shapes=[pltpu.SemaphoreType.DMA((n_remote,)), pltpu.SemaphoreType.DMA((n_remote,)),
                        pltpu.SemaphoreType.DMA((n_local,))],
    )(*arrs)


def all_gather8(arrs, name):
    n = len(arrs)
    pieces = [_pieces(a.shape, a.dtype.itemsize) for a in arrs]

    def body(*refs):
        ins, outs = refs[:n], refs[n:2 * n]
        send, recv, _ = refs[2 * n:]
        x, y, c = _place()
        me, sib = (x, y, c), (x, y, 1 - c)
        chips = [(1 - x, y), (x, 1 - y), (1 - x, 1 - y)]

        def slot(p):
            return 4 * p[0] + 2 * p[1] + p[2]

        def maker(t, k, to):
            def make(s, d):
                return pltpu.make_async_remote_copy(src_ref=s, dst_ref=d, send_sem=send.at[7 * t + k],
                                                    recv_sem=recv.at[7 * t + k], device_id=to, device_id_type=MESH)
            return make

        def landing(t, k, block):
            dst = outs[t].at[slot(block)]
            return maker(t, k, me)(dst, dst)

        sent = []
        for t in range(n):
            dst = outs[t].at[slot(me)]
            sent.append(_split_start(maker(t, 0, sib), ins[t], dst, pieces[t]))
            for j, chip in enumerate(chips):
                sent.append(_split_start(maker(t, 1 + j, (*chip, c)), ins[t], dst, pieces[t]))
        for j, chip in enumerate(chips):
            for t in range(n):
                landing(t, 1 + j, (*chip, c)).wait_recv()
                blk = outs[t].at[slot((*chip, c))]
                sent.append(_split_start(maker(t, 4 + j, sib), blk, blk, pieces[t]))
        for t in range(n):
            landing(t, 0, sib).wait_recv()
            for j, chip in enumerate(chips):
                landing(t, 4 + j, (*chip, 1 - c)).wait_recv()
        for cp in sent:
            cp.wait_send()

    outs = [_sds((N_DEV,) + a.shape, a.dtype) for a in arrs]
    got = _comm_call(body, name, arrs, outs, 7 * n, 1)
    x, y, c = _place()
    return [lax.dynamic_update_index_in_dim(g, a, 4 * x + 2 * y + c, 0) for g, a in zip(got, arrs)]


def sibling_send(arrs, name):
    n = len(arrs)
    pieces = [_pieces(a.shape[1:], a.dtype.itemsize) for a in arrs]

    def body(*refs):
        ins, theirs = refs[:n], refs[n:2 * n]
        send, recv, _ = refs[2 * n:]
        x, y, c = _place()
        rem = []
        for t in range(n):
            def make(s, d, t=t):
                return pltpu.make_async_remote_copy(src_ref=s, dst_ref=d, send_sem=send.at[t], recv_sem=recv.at[t],
                                                    device_id=(x, y, 1 - c), device_id_type=MESH)
            rem.append(_split_start(make, ins[t].at[1 - c], theirs[t], pieces[t]))
        for cp in rem:
            cp.wait_recv()
        for cp in rem:
            cp.wait_send()

    outs = [_sds(a.shape[1:], a.dtype) for a in arrs]
    return _comm_call(body, name, arrs, outs, n, 1)


def exchange_chips(arrs, name):
    n = len(arrs)
    pieces = [_pieces(a.shape[1:], a.dtype.itemsize) for a in arrs]

    def body(*refs):
        ins, outs = refs[:n], refs[n:2 * n]
        send, recv, _ = refs[2 * n:]
        x, y, c = _place()
        peers = [(1 - x, y), (x, 1 - y), (1 - x, 1 - y)]
        rem = []
        for t in range(n):
            for j, (px, py) in enumerate(peers):
                def make(s, d, t=t, j=j, px=px, py=py):
                    return pltpu.make_async_remote_copy(
                        src_ref=s, dst_ref=d, send_sem=send.at[3 * t + j], recv_sem=recv.at[3 * t + j],
                        device_id=(px, py, c), device_id_type=MESH)
                rem.append(_split_start(make, ins[t].at[2 * px + py], outs[t].at[j], pieces[t]))
        for cp in rem:
            cp.wait_recv()
        for cp in rem:
            cp.wait_send()

    outs = [_sds((N_CHIPS - 1,) + a.shape[1:], a.dtype) for a in arrs]
    return _comm_call(body, name, arrs, outs, 3 * n, 1)


def sibling_swap(arrs, name):
    n = len(arrs)
    pieces = [_pieces(a.shape, a.dtype.itemsize) for a in arrs]

    def body(*refs):
        ins, outs = refs[:n], refs[n:2 * n]
        send, recv, _ = refs[2 * n:]
        x, y, c = _place()
        rem = []
        for t in range(n):
            def make(s, d, t=t):
                return pltpu.make_async_remote_copy(src_ref=s, dst_ref=d, send_sem=send.at[t], recv_sem=recv.at[t],
                                                    device_id=(x, y, 1 - c), device_id_type=MESH)
            rem.append(_split_start(make, ins[t], outs[t], pieces[t]))
        for cp in rem:
            cp.wait_recv()
        for cp in rem:
            cp.wait_send()

    outs = [_sds(a.shape, a.dtype) for a in arrs]
    return _comm_call(body, name, arrs, outs, n, 1)


def _peer_copies(srcs, lands, send, recv, mode):
    x, y, c = _place()
    my_chip = 2 * x + y
    out = []
    for t in range(len(srcs)):
        for j, (px, py) in enumerate([(1 - x, y), (x, 1 - y), (1 - x, 1 - y)]):
            if mode == "gather":
                s, dst = srcs[t], lands[t].at[c, my_chip]
            else:
                s, dst = srcs[t].at[2 * px + py], lands[t].at[j]
            out.append(pltpu.make_async_remote_copy(
                src_ref=s, dst_ref=dst, send_sem=send.at[3 * t + j], recv_sem=recv.at[3 * t + j],
                device_id=(px, py, c), device_id_type=MESH))
    return out


HBM = pl.BlockSpec(memory_space=pltpu.HBM)
SEM = pl.BlockSpec(memory_space=pltpu.SEMAPHORE)
EFFECT = pltpu.SideEffectType.DATAFLOW_SIDE_EFFECTING


def ici_start(srcs, lands, mode, name):
    n = len(srcs)

    def body(*refs):
        send, recv = refs[2 * n], refs[2 * n + 1]
        for cp in _peer_copies(refs[:n], refs[n:2 * n], send, recv, mode):
            cp.start()
        refs[-1][...] = jnp.zeros_like(refs[-1])

    thru = [pltpu.HBM(a.shape, a.dtype) for a in list(srcs) + list(lands)]
    outs = pl.pallas_call(
        body, name=name, in_specs=[HBM] * (2 * n), out_specs=[SEM, SEM] + [HBM] * (2 * n) + [pl.BlockSpec(memory_space=pltpu.VMEM)],
        out_shape=[pltpu.SemaphoreType.DMA((3 * n,)), pltpu.SemaphoreType.DMA((3 * n,))] + thru + [_sds((8, 128))],
        input_output_aliases={i: 2 + i for i in range(2 * n)},
        compiler_params=pltpu.CompilerParams(has_side_effects=EFFECT),
    )(*[pltpu.with_memory_space_constraint(a, pltpu.HBM) for a in list(srcs) + list(lands)])
    return dict(send=outs[0], recv=outs[1], srcs=outs[2:2 + n], lands=outs[2 + n:2 + 2 * n], token=outs[-1])


def ici_wait(handle, after, mode, name):
    n = len(handle["srcs"])

    def body(*refs):
        send, recv = refs[2 * n], refs[2 * n + 1]
        for cp in _peer_copies(refs[:n], refs[n:2 * n], send, recv, mode):
            cp.wait_send()
            cp.wait_recv()

    arrs = list(handle["srcs"]) + list(handle["lands"])
    outs = pl.pallas_call(
        body, name=name, in_specs=[HBM] * (2 * n) + [SEM, SEM, ANY], out_specs=[HBM] * (2 * n),
        out_shape=[pltpu.HBM(a.shape, a.dtype) for a in arrs], input_output_aliases={i: i for i in range(2 * n)},
        compiler_params=pltpu.CompilerParams(has_side_effects=EFFECT),
    )(*arrs, handle["send"], handle["recv"], after)
    return outs[:n], outs[n:]


def gather_share(blocks, lands, name):
    n = len(blocks)

    def body(*refs):
        own, buf = refs[:n], refs[2 * n:3 * n]
        done, send, recv = refs[3 * n:]
        x, y, c = _place()
        my_chip = 2 * x + y
        chips = [2 * (1 - x) + y, 2 * x + (1 - y), 2 * (1 - x) + (1 - y)]
        sent = []
        for t in range(n):
            def make(s, d, k, t=t):
                return pltpu.make_async_remote_copy(src_ref=s, dst_ref=d, send_sem=send.at[4 * t + k],
                                                    recv_sem=recv.at[4 * t + k], device_id=(x, y, 1 - c),
                                                    device_id_type=MESH)
            cp = make(own[t], buf[t].at[c, my_chip], 0)
            cp.start()
            sent.append(cp)
            for k, pc in enumerate(chips):
                cp = make(buf[t].at[c, pc], buf[t].at[c, pc], 1 + k)
                cp.start()
                sent.append(cp)
        for t in range(n):
            for k in range(4):
                got = buf[t].at[1 - c, k]
                pltpu.make_async_remote_copy(src_ref=got, dst_ref=got, send_sem=send.at[4 * t + k],
                                             recv_sem=recv.at[4 * t + k], device_id=(x, y, 1 - c),
                                             device_id_type=MESH).wait_recv()
        for cp in sent:
            cp.wait_send()
        done[...] = jnp.zeros_like(done)

    outs = pl.pallas_call(
        body, name=name, in_specs=[ANY] * (2 * n), out_specs=[ANY] * n + [pl.BlockSpec(memory_space=pltpu.VMEM)],
        out_shape=[_sds(a.shape, a.dtype) for a in lands] + [_sds((8, 128))],
        input_output_aliases={n + t: t for t in range(n)},
        scratch_shapes=[pltpu.SemaphoreType.DMA((4 * n,)), pltpu.SemaphoreType.DMA((4 * n,))],
    )(*blocks, *lands)
    return outs[:n], outs[n]


@jax.custom_vjp
def _build_w_in(w4):
    full = w4.reshape(-1, w4.shape[-1])
    parts = []
    for _, start, width, wp in PROJ_SEGS:
        if width:
            parts.append(full[start:start + width])
        if wp > width:
            parts.append(jnp.zeros((wp - width, full.shape[1]), full.dtype))
    return jnp.concatenate(parts, axis=0)


def _build_w_in_f(w4):
    return _build_w_in(w4), None


def _build_w_in_b(_, g):
    parts, at = [], 0
    for _, _, width, wp in PROJ_SEGS:
        if width:
            parts.append(g[at:at + width])
        at += wp
    return (jnp.concatenate(parts, axis=0).reshape(N_CHIPS, -1, g.shape[1]),)


_build_w_in.defvjp(_build_w_in_f, _build_w_in_b)


def _split_w_uq(w):
    w3 = w.reshape(w.shape[0], MLA_HEADS, MLA_NOPE + MLA_ROPE)
    return w3[:, :, :MLA_NOPE].reshape(w.shape[0], -1), w3[:, :, MLA_NOPE:].reshape(w.shape[0], -1)


def _swap_halves(t, width):
    t3 = t.reshape(t.shape[0], -1, 2, width // 2)
    return jnp.concatenate([t3[:, :, 1:], t3[:, :, :1]], axis=2).reshape(t.shape)


def _pad_heads(t, width):
    t3 = t.reshape(t.shape[0], -1, width)
    t3 = jnp.pad(t3, ((0, 0), (0, 0), (0, HEAD_LANES - width)))
    return t3.reshape(t.shape[0], -1).astype(BF16)


def _layer(xh, mod, big, small, rope_q, rope_k):
    d = D_MODEL
    shift, scale, gate = mod[None, 0:d], mod[None, d:2 * d], mod[None, 2 * d:3 * d]
    w_al = _build_w_in(big["w_in"])
    proj = mod_mm(xh, small["norm_g"][None], scale, shift, w_al)
    gq, gk, gv, glr, mq, mkv, mkr, cb, cc, cx, _, z = split_proj(proj)

    rk = GLA_RANK
    hk = GLA_HEADS * GLA_DK
    wg = jnp.zeros((128, 2 * hk), F32)
    wg = wg.at[0:rk, 0:hk].set(small["gla_wg_f"]).at[rk:2 * rk, hk:].set(small["gla_wg_b"])
    bg = jnp.concatenate([small["gla_bg_f"], small["gla_bg_b"]])[None]
    la = gate_act(mm(glr, wg), bg)
    o_gla = rmsnorm(gla(gq, gk, gv, la), small["gla_norm_g"][None])

    cq = rmsnorm(mq, small["mla_q_norm_g"][None])
    w_nope, w_rope = _split_w_uq(jnp.concatenate([big["w_uq"][j] for j in range(N_CHIPS)], axis=1))
    qn = mm16(cq, w_nope)
    qr = mm(cq, w_rope)
    qr = fma(qr, rope_q[0], _swap_halves(qr, MLA_ROPE), rope_q[1])
    ckv = rmsnorm(mkv, small["mla_kv_norm_g"][None])
    kv = mm16(ckv, jnp.concatenate([big["w_ukv"][j] for j in range(N_CHIPS)], axis=1))
    kr = mkr[:, :MLA_ROPE]
    kr = fma(kr, rope_k[0], _swap_halves(kr, MLA_ROPE), rope_k[1])
    o_mla = rmsnorm(attn(qn, _pad_heads(qr, MLA_ROPE), kv, _pad_heads(kr, MLA_ROPE)), small["mla_out_g"][None])

    cw = jnp.concatenate([small["conv_w"], jnp.zeros((5, CONV_CH), F32)], axis=0)
    o_conv = rmsnorm(conv_op(cb, cc, cx, cw), small["conv_out_g"][None])

    o = jnp.concatenate([o_gla, o_mla, o_conv], axis=1)
    w_out = big["w_out"].reshape(d, d)
    return out_block(o, z, w_out, xh, gate)


SMALL_REPL = ("norm_g", "gla_bg_f", "gla_bg_b", "gla_norm_g", "mla_q_norm_g", "mla_kv_norm_g", "mla_out_g",
              "conv_out_g")
SMALL_SHARDED = ("gla_wg_f", "gla_wg_b", "conv_w")
BIG = ("w_in", "w_out", "w_uq", "w_ukv")
HALF_AXIS = (1, 0, 0, 0)


def kernel(x, c, positions, ada_w, ada_b, norm_g, w_in, gla_wg_f, gla_bg_f, gla_wg_b, gla_bg_b, gla_norm_g, mla_q_norm_g, mla_kv_norm_g, mla_w_uq, mla_w_ukv, mla_out_g, conv_w, conv_out_g, w_out, final_g, loss_target, m_ada_w, m_ada_b, m_norm_g, m_w_in, m_gla_wg_f, m_gla_bg_f, m_gla_wg_b, m_gla_bg_b, m_gla_norm_g, m_mla_q_norm_g, m_mla_kv_norm_g, m_mla_w_uq, m_mla_w_ukv, m_mla_out_g, m_conv_w, m_conv_out_g, m_w_out, m_final_g, v_ada_w, v_ada_b, v_norm_g, v_w_in, v_gla_wg_f, v_gla_bg_f, v_gla_wg_b, v_gla_bg_b, v_gla_norm_g, v_mla_q_norm_g, v_mla_kv_norm_g, v_mla_w_uq, v_mla_w_ukv, v_mla_out_g, v_conv_w, v_conv_out_g, v_w_out, v_final_g):
    xi, yi, ci = _place()
    chip = 2 * xi + yi
    dev = 2 * chip + ci
    s = x.shape[1]
    d = D_MODEL
    weights = dict(ada_w=ada_w, ada_b=ada_b, norm_g=norm_g, w_in=w_in, gla_wg_f=gla_wg_f, gla_bg_f=gla_bg_f,
                   gla_wg_b=gla_wg_b, gla_bg_b=gla_bg_b, gla_norm_g=gla_norm_g, mla_q_norm_g=mla_q_norm_g,
                   mla_kv_norm_g=mla_kv_norm_g, mla_w_uq=mla_w_uq, mla_w_ukv=mla_w_ukv, mla_out_g=mla_out_g,
                   conv_w=conv_w, conv_out_g=conv_out_g, w_out=w_out, final_g=final_g)
    m_in = dict(ada_w=m_ada_w, ada_b=m_ada_b, norm_g=m_norm_g, w_in=m_w_in, gla_wg_f=m_gla_wg_f, gla_bg_f=m_gla_bg_f,
                gla_wg_b=m_gla_wg_b, gla_bg_b=m_gla_bg_b, gla_norm_g=m_gla_norm_g, mla_q_norm_g=m_mla_q_norm_g,
                mla_kv_norm_g=m_mla_kv_norm_g, mla_w_uq=m_mla_w_uq, mla_w_ukv=m_mla_w_ukv, mla_out_g=m_mla_out_g,
                conv_w=m_conv_w, conv_out_g=m_conv_out_g, w_out=m_w_out, final_g=m_final_g)
    v_in = dict(ada_w=v_ada_w, ada_b=v_ada_b, norm_g=v_norm_g, w_in=v_w_in, gla_wg_f=v_gla_wg_f, gla_bg_f=v_gla_bg_f,
                gla_wg_b=v_gla_wg_b, gla_bg_b=v_gla_bg_b, gla_norm_g=v_gla_norm_g, mla_q_norm_g=v_mla_q_norm_g,
                mla_kv_norm_g=v_mla_kv_norm_g, mla_w_uq=v_mla_w_uq, mla_w_ukv=v_mla_w_ukv, mla_out_g=v_mla_out_g,
                conv_w=v_conv_w, conv_out_g=v_conv_out_g, w_out=v_w_out, final_g=v_final_g)

    g_c, g_wgf, g_wgb, g_cw = all_gather8([c, gla_wg_f, gla_wg_b, conv_w], "gather_small")

    def unshard_cols(g):
        g4 = g[0::2]
        return g4.transpose(1, 2, 0, 3).reshape(g4.shape[1], g4.shape[2], -1)

    small_full = dict(gla_wg_f=unshard_cols(g_wgf), gla_wg_b=unshard_cols(g_wgb), conv_w=unshard_cols(g_cw))
    for nme in SMALL_REPL:
        small_full[nme] = weights[nme]
    smalls = [{nme: small_full[nme][l] for nme in SMALL_REPL + SMALL_SHARDED} for l in range(DEPTH)]

    big_src = (jnp.swapaxes(w_in, 1, 2), w_out, mla_w_uq, mla_w_ukv)

    def my_halves(l, zero=0):
        out = []
        for t, a in enumerate(big_src):
            n_half = a.shape[1 + HALF_AXIS[t]] // 2
            out.append(lax.dynamic_slice_in_dim(a[l], ci * n_half + zero, n_half, axis=HALF_AXIS[t]).astype(BF16))
        return out

    def landing(blocks):
        return [lax.empty((2, N_CHIPS) + b.shape, b.dtype) for b in blocks]

    def finish_gather(handle, after, tag):
        blocks, lands = ici_wait(handle, after, "gather", "gather_wait" + tag)
        lands, done = gather_share(blocks, lands, "gather_share" + tag)
        full = [lax.dynamic_update_slice(g, b[None, None], (ci, chip) + (0,) * b.ndim) for g, b in zip(lands, blocks)]
        return full, done

    halves0 = my_halves(0)
    started0 = ici_start(halves0, landing(halves0), "gather", "gather_start0")

    c_act = _silu_rows(g_c[:, 0, :])
    c_act16 = jnp.concatenate([c_act, jnp.zeros_like(c_act)], axis=0)
    n_ada = ada_w.shape[2]
    parts = []
    for l in range(DEPTH):
        bias = lax.dynamic_slice_in_dim(ada_b[l], chip * n_ada, n_ada)[None]
        parts.append(_mm(c_act16, ada_w[l], bias=bias, name="ada_fwd"))
    g_mod, = all_gather8([jnp.stack(parts)], "gather_mod")
    mod_mine = lax.dynamic_index_in_dim(g_mod[0::2], dev, 2, keepdims=False)
    mods = mod_mine.transpose(1, 0, 2).reshape(DEPTH, 3 * d)

    inv_freq = ROPE_THETA ** (-jnp.arange(0, MLA_ROPE, 2, dtype=F32) / MLA_ROPE)
    ang = positions[0].astype(F32)[:, None] * inv_freq
    cos, sin = jnp.cos(ang), jnp.sin(ang)
    rope_k = (jnp.concatenate([cos, cos], axis=1), jnp.concatenate([-sin, sin], axis=1))
    rope_q = (jnp.tile(rope_k[0], (1, MLA_HEADS)), jnp.tile(rope_k[1], (1, MLA_HEADS)))

    def run_layer(xh, mod, gathered, small):
        big = {nme: jnp.concatenate([g[0], g[1]], axis=HALF_AXIS[t] + 1) for t, (nme, g) in enumerate(zip(BIG, gathered))}
        return _layer(xh, mod, big, small, rope_q, rope_k)

    def head(hh, fg):
        return loss_op(rmsnorm(hh, fg[None]), loss_target[0])[0, 0]

    gathered0, done0 = finish_gather(started0, mods, "0")
    halves1 = my_halves(1, done0[0, 0].astype(jnp.int32))
    started1 = ici_start(halves1, landing(halves1), "gather", "gather_start1")
    h1, vjp0 = jax.vjp(run_layer, x[0], mods[0] + started1["token"][0, 0], gathered0, smalls[0])
    gathered1, _ = finish_gather(started1, h1, "1")
    h2, vjp1 = jax.vjp(run_layer, h1, mods[1], gathered1, smalls[1])
    loss_dev, vjp_head = jax.vjp(head, h2, final_g)
    dh2, dfinal = vjp_head(jnp.ones((), F32))

    c_idx = jnp.reshape(ci, (1,)).astype(jnp.int32)
    chip_idx = jnp.reshape(chip, (1,)).astype(jnp.int32)

    def reduce_begin(dgath, tag, zero=None):
        theirs = sibling_send(dgath, "reduce_sibling" + tag)
        pair = [_add_pair(a, b, c_idx) for a, b in zip(dgath, theirs)]
        shapes = [(N_CHIPS - 1,) + p.shape[1:] for p in pair]
        if zero is None:
            lands = [lax.empty(shp, BF16) for shp in shapes]
        else:
            lands = [jnp.broadcast_to(zero.astype(BF16), shp) for shp in shapes]
        return ici_start(pair, lands, "reduce", "reduce_start" + tag)

    def reduce_end(handle, after, tag):
        pair, landed = ici_wait(handle, after, "reduce", "reduce_wait" + tag)
        reduced = [_add_chips(p, q, chip_idx) for p, q in zip(pair, landed)]
        others = sibling_swap(reduced, "share_sibling" + tag)
        return [jnp.where(ci == 0, jnp.concatenate([own, other], axis=HALF_AXIS[t]),
                          jnp.concatenate([other, own], axis=HALF_AXIS[t]))
                for t, (own, other) in enumerate(zip(reduced, others))]

    dh1, dmod1, dgath1, dsmall1 = vjp1(dh2)
    reducing1 = reduce_begin(dgath1, "1")
    dx, dmod0, dgath0, dsmall0 = vjp0(dh1 + reducing1["token"][0, 0])
    dmods = jnp.stack([dmod0, dmod1])
    dsmalls = [dsmall0, dsmall1]

    pieces = [dmods.reshape(-1), dfinal]
    for nme in SMALL_REPL + SMALL_SHARDED:
        pieces.append(jnp.stack([dsmalls[l][nme] for l in range(DEPTH)]).reshape(-1))
    pieces.append(loss_dev.reshape(1))
    sizes = [p.shape[0] for p in pieces]
    flat = jnp.concatenate(pieces)
    padn = (-flat.shape[0]) % 128
    flat = jnp.concatenate([flat, jnp.zeros((padn,), F32)])[None]
    g_small, = all_gather8([flat], "gather_small_grads")
    total, small_done = _sum_devices(g_small)
    total = total[0]
    reducing0 = reduce_begin(dgath0, "0", small_done[0, 0])
    offs, at = [], 0
    for n_el in sizes:
        offs.append(at)
        at += n_el

    def piece(i, shape):
        return total[offs[i]:offs[i] + sizes[i]].reshape(shape)

    grads = {"ada_b": piece(0, (DEPTH, 3 * d)), "final_g": piece(1, (d,))}
    loss = piece(len(pieces) - 1, ())
    for i, nme in enumerate(SMALL_REPL + SMALL_SHARDED):
        full = piece(2 + i, small_full[nme].shape)
        if nme in SMALL_SHARDED:
            ncol = weights[nme].shape[2]
            full = lax.dynamic_slice_in_dim(full, chip * ncol, ncol, axis=2)
        grads[nme] = full

    dmod_all = g_small[:, 0, :DEPTH * 3 * d].reshape(N_DEV, DEPTH, 3 * d)
    dmod_cols = lax.dynamic_slice_in_dim(dmod_all, chip * n_ada, n_ada, axis=2)
    g_ada = []
    for l in range(DEPTH):
        dm16 = jnp.concatenate([dmod_cols[:, l], jnp.zeros((N_DEV, n_ada), F32)], axis=0)
        dm16 = dm16 + reducing0["token"][0, 0]
        g_ada.append(_mm(c_act16, dm16, ta=True, name="ada_bwd"))
    grads["ada_w"] = jnp.stack(g_ada)

    order = list(weights)
    big_names = ("w_in", "w_out", "mla_w_uq", "mla_w_ukv")
    delta, new_m, new_v = {}, {}, {}
    for nme in order:
        if nme not in big_names:
            delta[nme], new_m[nme], new_v[nme] = _adamw(weights[nme], grads[nme], m_in[nme], v_in[nme])

    def first(t):
        return t[(slice(0, 1),) * t.ndim].reshape(1)

    big_grads1 = reduce_end(reducing1, jnp.concatenate([first(dx), first(reducing0["token"])]), "1")
    done = [first(delta[nme]) for nme in order if nme not in big_names] + [first(g) for g in big_grads1]
    big_grads0 = reduce_end(reducing0, jnp.concatenate(done), "0")
    for nme, g0, g1 in zip(big_names, big_grads0, big_grads1):
        grads[nme] = jnp.stack([g0, g1])
    for nme in big_names:
        if nme == "w_in":
            w_t, m_t, v_t = (jnp.swapaxes(t, 1, 2) for t in (w_in, m_w_in, v_w_in))
            res = _adamw(w_t, grads[nme], m_t, v_t)
            delta[nme], new_m[nme], new_v[nme] = (jnp.swapaxes(t, 1, 2) for t in res)
            grads[nme] = jnp.swapaxes(grads[nme], 1, 2)
            continue
        delta[nme], new_m[nme], new_v[nme] = _adamw(weights[nme], grads[nme], m_in[nme], v_in[nme])
    return (loss, dx[None], *[grads[n_] for n_ in order], *[delta[n_] for n_ in order],
            *[new_m[n_] for n_ in order], *[new_v[n_] for n_ in order])
```

```python
import functools

import jax
import jax.numpy as jnp
from jax import lax
from jax.experimental import pallas as pl
from jax.experimental.pallas import tpu as pltpu

F32 = jnp.float32
BF16 = jnp.bfloat16
MESH = pl.DeviceIdType.MESH
HIGHEST = lax.Precision.HIGHEST

DEPTH = 2
D_MODEL = 2048
GLA_HEADS = 6
GLA_DK = 64
GLA_DV = 128
GLA_RANK = 16
GLA_TEMP = 16.0
GLA_CHUNK = 64
GLA_W = GLA_HEADS * GLA_DV
MLA_HEADS = 6
MLA_QL = 384
MLA_KVL = 256
MLA_NOPE = 128
MLA_ROPE = 64
MLA_DV = 128
MLA_W = MLA_HEADS * MLA_DV
CONV_CH = D_MODEL - GLA_W - MLA_W
ROPE_THETA = 10000.0
EPS = 1e-6
IN_DIM = 5856
N_CHIPS = 4
N_DEV = 8

ADAM_LR = 0.001
ADAM_B1 = 0.9
ADAM_B2 = 0.999
ADAM_EPS = 1e-08
ADAM_WD = 0.01
ADAM_STEP = 10

PROJ_SEGS = (
    ("gq", 0, 384, 384), ("gk", 384, 384, 384), ("gv", 768, 768, 768), ("glr", 1536, 32, 128),
    ("mq", 1568, 384, 384), ("mkv", 1952, 256, 256), ("mkr", 2208, 64, 128),
    ("cb", 2272, 512, 512), ("cc", 2784, 512, 512), ("cx", 3296, 512, 512),
    ("pad", 3808, 0, 128), ("z", 3808, 2048, 2048),
)
PROJ_AL = sum(s[3] for s in PROJ_SEGS)

ANY = pl.BlockSpec(memory_space=pl.ANY)
VMEM_LIMIT = 48 * 1024 * 1024
BLOCK_BYTES = 2 * 1024 * 1024


def _params(sem=None):
    return pltpu.CompilerParams(dimension_semantics=sem, vmem_limit_bytes=VMEM_LIMIT)


def _dot(a, b, ca, cb, precision=None):
    return lax.dot_general(a, b, (((ca,), (cb,)), ((), ())), preferred_element_type=F32, precision=precision)


def _tile(dim, prefs):
    for t in prefs:
        if dim % t == 0:
            return t
    return dim


def _pick_rows(rows, width, itemsize=4):
    for t in (2048, 1024, 512, 256, 128, 64, 32, 16, 8):
        if rows % t == 0 and t * width * itemsize <= BLOCK_BYTES:
            return t
    return rows


def _mm(a, b, *, ta=False, tb=False, bias=None, out_dtype=F32, name="mm"):
    if ta:
        K, M = a.shape
    else:
        M, K = a.shape
    if tb:
        N, Kb = b.shape
    else:
        Kb, N = b.shape
    assert K == Kb, (a.shape, b.shape, ta, tb)
    tm = _tile(M, (2048, 1024, 512, 256, 128))
    tn = _tile(N, (512, 384, 256, 128) if tm >= 2048 else (1024, 512, 384, 256, 128))
    tk = _tile(K, (2048, 1024, 512, 256, 128))
    nk = K // tk
    has_bias = bias is not None

    def body(*refs):
        a_ref, b_ref = refs[0], refs[1]
        bias_ref = refs[2] if has_bias else None
        o_ref = refs[3 if has_bias else 2]
        part = _dot(a_ref[...].astype(BF16), b_ref[...].astype(BF16), 0 if ta else 1, 1 if tb else 0)

        def finish(r):
            if has_bias:
                r = r + bias_ref[...]
            o_ref[...] = r.astype(out_dtype)

        if nk == 1:
            finish(part)
            return
        acc_ref = refs[-1]
        k = pl.program_id(2)

        @pl.when(k == 0)
        def _():
            acc_ref[...] = part

        @pl.when(k != 0)
        def _():
            acc_ref[...] += part

        @pl.when(k == nk - 1)
        def _():
            finish(acc_ref[...])

    a_spec = pl.BlockSpec((tk, tm), lambda i, j, k: (k, i)) if ta else pl.BlockSpec((tm, tk), lambda i, j, k: (i, k))
    b_spec = pl.BlockSpec((tn, tk), lambda i, j, k: (j, k)) if tb else pl.BlockSpec((tk, tn), lambda i, j, k: (k, j))
    in_specs = [a_spec, b_spec]
    args = [a, b]
    if has_bias:
        in_specs.append(pl.BlockSpec((1, tn), lambda i, j, k: (0, j)))
        args.append(bias)
    return pl.pallas_call(
        body, name=name, grid=(M // tm, N // tn, nk),
        in_specs=in_specs, out_specs=pl.BlockSpec((tm, tn), lambda i, j, k: (i, j)),
        out_shape=jax.ShapeDtypeStruct((M, N), out_dtype),
        scratch_shapes=[pltpu.VMEM((tm, tn), F32)] if nk > 1 else [],
        compiler_params=_params(("parallel", "parallel", "arbitrary")),
    )(*args)


@jax.custom_vjp
def mm(a, b):
    return _mm(a, b, name="mm_fwd")


def _mm_f(a, b):
    return _mm(a, b, name="mm_fwd"), (a, b)


def _mm_b(res, g):
    a, b = res
    return _mm(g, b, tb=True, out_dtype=a.dtype, name="mm_da"), _mm(a, g, ta=True, out_dtype=b.dtype, name="mm_db")


mm.defvjp(_mm_f, _mm_b)


@jax.custom_vjp
def mm16(a, b):
    return _mm(a, b, out_dtype=BF16, name="mm16_fwd")


def _mm16_f(a, b):
    return mm16(a, b), (a, b)


mm16.defvjp(_mm16_f, _mm_b)


def _rows(body, name, tiled, full, tiled_out, acc_out, tr=None):
    rows = tiled[0].shape[0]
    if tr is None:
        width = max([a.shape[1] for a in tiled] + [s.shape[1] for s in tiled_out])
        tr = _pick_rows(rows, width)
    in_specs = [pl.BlockSpec((tr, a.shape[1]), lambda i: (i, 0)) for a in tiled]
    in_specs += [pl.BlockSpec(a.shape, lambda i: (0, 0)) for a in full]
    out_specs = [pl.BlockSpec((tr, s.shape[1]), lambda i: (i, 0)) for s in tiled_out]
    out_specs += [pl.BlockSpec(s.shape, lambda i: (0, 0)) for s in acc_out]

    def wrapped(*refs):
        body(pl.program_id(0), *refs)

    outs = pl.pallas_call(
        wrapped, name=name, grid=(rows // tr,), in_specs=in_specs, out_specs=out_specs,
        out_shape=list(tiled_out) + list(acc_out),
        compiler_params=_params(("arbitrary",)),
    )(*tiled, *full)
    return outs


def _sds(shape, dtype=F32):
    return jax.ShapeDtypeStruct(tuple(shape), dtype)


def _acc(step, ref, val):
    @pl.when(step == 0)
    def _():
        ref[...] = val

    @pl.when(step != 0)
    def _():
        ref[...] += val


def _colsum(v):
    return jnp.sum(v, axis=0, keepdims=True)


def _rstd(x):
    return lax.rsqrt(jnp.mean(x * x, axis=-1, keepdims=True) + EPS)


def _norm_grid(x, g):
    rows, w = x.shape[0], g.shape[1]
    tr = _pick_rows(rows, w)
    blk = pl.BlockSpec((tr, w), lambda i, j: (i, j))
    gblk = pl.BlockSpec((1, w), lambda i, j: (0, 0))
    return (rows // tr, x.shape[1] // w), blk, gblk


@jax.custom_vjp
def rmsnorm(x, g):
    def body(x_ref, g_ref, o_ref):
        x = x_ref[...]
        o_ref[...] = x * _rstd(x) * g_ref[...]

    grid, blk, gblk = _norm_grid(x, g)
    return pl.pallas_call(body, name="rmsnorm_fwd", grid=grid, in_specs=[blk, gblk], out_specs=blk,
                          out_shape=_sds(x.shape), compiler_params=_params(("parallel", "parallel")))(x, g)


def _rmsnorm_f(x, g):
    return rmsnorm(x, g), (x, g)


def _rmsnorm_b(res, dy):
    x, g = res

    def body(x_ref, dy_ref, g_ref, dx_ref, dg_ref):
        x = x_ref[...]
        dy = dy_ref[...]
        r = _rstd(x)
        xh = x * r
        dxh = dy * g_ref[...]
        dx_ref[...] = r * (dxh - xh * jnp.mean(dxh * xh, axis=-1, keepdims=True))
        first = jnp.logical_and(pl.program_id(0) == 0, pl.program_id(1) == 0)
        _acc(jnp.where(first, 0, 1), dg_ref, _colsum(dy * xh))

    grid, blk, gblk = _norm_grid(x, g)
    dx, dg = pl.pallas_call(body, name="rmsnorm_bwd", grid=grid, in_specs=[blk, blk, gblk], out_specs=[blk, gblk],
                            out_shape=[_sds(x.shape), _sds(g.shape)],
                            compiler_params=_params(("arbitrary", "arbitrary")))(x, dy, g)
    return dx, dg


rmsnorm.defvjp(_rmsnorm_f, _rmsnorm_b)


def _modulate(x, g, scale, shift):
    def body(i, x_ref, g_ref, sc_ref, sh_ref, o_ref):
        x = x_ref[...]
        xn = x * _rstd(x) * g_ref[...]
        o_ref[...] = (xn * (1.0 + sc_ref[...]) + sh_ref[...]).astype(BF16)
    return _rows(body, "modulate_fwd", [x], [g, scale, shift], [_sds(x.shape, BF16)], [])[0]


def _modulate_bwd(x, g, scale, shift, dh):
    def body(i, x_ref, dh_ref, g_ref, sc_ref, dx_ref, dg_ref, dsc_ref, dsh_ref):
        x = x_ref[...]
        dh = dh_ref[...]
        gv = g_ref[...]
        r = _rstd(x)
        xh = x * r
        dxn = dh * (1.0 + sc_ref[...])
        dxh = dxn * gv
        dx_ref[...] = r * (dxh - xh * jnp.mean(dxh * xh, axis=-1, keepdims=True))
        _acc(i, dg_ref, _colsum(dxn * xh))
        _acc(i, dsc_ref, _colsum(dh * (xh * gv)))
        _acc(i, dsh_ref, _colsum(dh))

    v = _sds(g.shape)
    return _rows(body, "modulate_bwd", [x, dh], [g, scale], [_sds(x.shape)], [v, v, v])


@jax.custom_vjp
def mod_mm(x, g, scale, shift, wt):
    return _mm(_modulate(x, g, scale, shift), wt, tb=True, name="mm_in")


def _mod_mm_f(x, g, scale, shift, wt):
    h = _modulate(x, g, scale, shift)
    return _mm(h, wt, tb=True, name="mm_in"), (x, g, scale, shift, wt, h)


def _mod_mm_b(res, dproj):
    x, g, scale, shift, wt, h = res
    dproj = dproj.astype(BF16)
    dh = _mm(dproj, wt, name="mm_in_dh")
    dwt = _mm(dproj, h, ta=True, out_dtype=wt.dtype, name="mm_in_dw")
    dx, dg, dsc, dsh = _modulate_bwd(x, g, scale, shift, dh)
    return dx, dg, dsc, dsh, dwt


mod_mm.defvjp(_mod_mm_f, _mod_mm_b)


def _sigmoid(z):
    return 1.0 / (1.0 + jnp.exp(-z))


def _gate_mul(o, z):
    def body(i, o_ref, z_ref, y_ref):
        z = z_ref[...]
        y_ref[...] = (o_ref[...] * (z * _sigmoid(z))).astype(BF16)
    return _rows(body, "gate_mul_fwd", [o, z], [], [_sds(o.shape, BF16)], [])[0]


def _gate_mul_bwd(o, z, dy):
    def body(i, o_ref, z_ref, dy_ref, do_ref, dz_ref):
        z = z_ref[...]
        dy = dy_ref[...]
        s = _sigmoid(z)
        do_ref[...] = dy * (z * s)
        dz_ref[...] = dy * o_ref[...] * (s * (1.0 + z * (1.0 - s)))
    return _rows(body, "gate_mul_bwd", [o, z, dy], [], [_sds(o.shape), _sds(o.shape)], [])


def _residual(x, u, gate):
    def body(i, x_ref, u_ref, g_ref, o_ref):
        o_ref[...] = x_ref[...] + g_ref[...] * u_ref[...]
    return _rows(body, "residual_fwd", [x, u], [gate], [_sds(x.shape)], [])[0]


def _residual_bwd(d, u, gate):
    def body(i, d_ref, u_ref, g_ref, du_ref, dg_ref):
        d = d_ref[...]
        du_ref[...] = (g_ref[...] * d).astype(BF16)
        _acc(i, dg_ref, _colsum(d * u_ref[...]))

    return _rows(body, "residual_bwd", [d, u], [gate], [_sds(u.shape, BF16)], [_sds(gate.shape)])


@jax.custom_vjp
def out_block(o, z, w, x, gate):
    return _residual(x, _mm(_gate_mul(o, z), w, name="mm_out"), gate)


def _out_block_f(o, z, w, x, gate):
    y = _gate_mul(o, z)
    u = _mm(y, w, name="mm_out")
    return _residual(x, u, gate), (o, z, w, y, u, gate)


def _out_block_b(res, d):
    o, z, w, y, u, gate = res
    du, dgate = _residual_bwd(d, u, gate)
    dy = _mm(du, w, tb=True, name="mm_out_dy")
    dw = _mm(y, du, ta=True, out_dtype=w.dtype, name="mm_out_dw")
    do, dz = _gate_mul_bwd(o, z, dy)
    return do, dz, dw, d, dgate


out_block.defvjp(_out_block_f, _out_block_b)


@jax.custom_vjp
def gate_act(u, b):
    def body(i, u_ref, b_ref, o_ref):
        t = u_ref[...] + b_ref[...]
        o_ref[...] = (jnp.minimum(t, 0.0) - jnp.log(1.0 + jnp.exp(-jnp.abs(t)))) / GLA_TEMP
    return _rows(body, "gate_act_fwd", [u], [b], [_sds(u.shape)], [])[0]


def _gate_act_f(u, b):
    return gate_act(u, b), (u, b)


def _gate_act_b(res, d):
    u, b = res

    def body(i, u_ref, d_ref, b_ref, du_ref, db_ref):
        t = u_ref[...] + b_ref[...]
        du = d_ref[...] * _sigmoid(-t) / GLA_TEMP
        du_ref[...] = du
        _acc(i, db_ref, _colsum(du))

    du, db = _rows(body, "gate_act_bwd", [u, d], [b], [_sds(u.shape)], [_sds(b.shape)])
    return du, db


gate_act.defvjp(_gate_act_f, _gate_act_b)


@jax.custom_vjp
def fma(a, b, c, d):
    def body(i, a_ref, b_ref, c_ref, d_ref, o_ref):
        o_ref[...] = a_ref[...] * b_ref[...] + c_ref[...] * d_ref[...]
    return _rows(body, "fma_fwd", [a, b, c, d], [], [_sds(a.shape)], [])[0]


def _fma_f(a, b, c, d):
    return fma(a, b, c, d), (b, d)


def _fma_b(res, g):
    b, d = res

    def body(i, g_ref, b_ref, d_ref, da_ref, dc_ref):
        g = g_ref[...]
        da_ref[...] = g * b_ref[...]
        dc_ref[...] = g * d_ref[...]

    da, dc = _rows(body, "fma_bwd", [g, b, d], [], [_sds(g.shape), _sds(g.shape)], [])
    return da, jnp.zeros_like(b), dc, jnp.zeros_like(d)


fma.defvjp(_fma_f, _fma_b)


def _silu_rows(c):
    def body(i, c_ref, o_ref):
        v = c_ref[...]
        o_ref[...] = v * _sigmoid(v)
    return _rows(body, "silu", [c], [], [_sds(c.shape)], [])[0]


@jax.custom_vjp
def loss_op(y, t):
    return _loss_fwd(y, t)[0]


def _loss_fwd(y, t):
    inv = 1.0 / y.shape[1]

    def body(i, y_ref, t_ref, d_ref, l_ref):
        e = y_ref[...] - t_ref[...]
        d_ref[...] = e * inv
        _acc(i, l_ref, jnp.sum(_colsum(e * e), axis=1, keepdims=True) * (0.5 * inv))

    d, l = _rows(body, "loss_fwd", [y, t], [], [_sds(y.shape)], [_sds((1, 1))])
    return l, d


def _loss_f(y, t):
    l, d = _loss_fwd(y, t)
    return l, d


def _loss_b(d, g):
    return d * g, jnp.zeros_like(d)


loss_op.defvjp(_loss_f, _loss_b)


def _conv_terms(cc, cx, rows, n):
    u = cc * cx
    up = jnp.where(rows == 0, 0.0, pltpu.roll(u, 1, 0))
    un = jnp.where(rows == n - 1, 0.0, pltpu.roll(u, n - 1, 0))
    return u, up, un


CONV_COLS = 128


def _conv_specs(s, n_in):
    blk = pl.BlockSpec((s, CONV_COLS), lambda j: (0, j))
    wblk = pl.BlockSpec((8, CONV_COLS), lambda j: (0, j))
    return [blk] * n_in + [wblk], blk, wblk


@jax.custom_vjp
def conv_op(cb, cc, cx, w):
    s, ch = cb.shape

    def body(cb_ref, cc_ref, cx_ref, w_ref, o_ref):
        rows = lax.broadcasted_iota(jnp.int32, (s, CONV_COLS), 0)
        u, up, un = _conv_terms(cc_ref[...], cx_ref[...], rows, s)
        conv = up * w_ref[0:1, :] + u * w_ref[1:2, :] + un * w_ref[2:3, :]
        o_ref[...] = cb_ref[...] * conv

    in_specs, blk, _ = _conv_specs(s, 3)
    return pl.pallas_call(
        body, name="conv_fwd", grid=(ch // CONV_COLS,), in_specs=in_specs, out_specs=blk,
        out_shape=_sds(cb.shape), compiler_params=_params(("parallel",)),
    )(cb, cc, cx, w)


def _conv_f(cb, cc, cx, w):
    return conv_op(cb, cc, cx, w), (cb, cc, cx, w)


def _conv_b(res, d):
    cb, cc, cx, w = res
    s, ch = cb.shape

    def body(cb_ref, cc_ref, cx_ref, d_ref, w_ref, dcb_ref, dcc_ref, dcx_ref, dw_ref):
        rows = lax.broadcasted_iota(jnp.int32, (s, CONV_COLS), 0)
        cc_v = cc_ref[...]
        cx_v = cx_ref[...]
        u, up, un = _conv_terms(cc_v, cx_v, rows, s)
        w0, w1, w2 = w_ref[0:1, :], w_ref[1:2, :], w_ref[2:3, :]
        dv = d_ref[...]
        dcb_ref[...] = dv * (up * w0 + u * w1 + un * w2)
        dconv = dv * cb_ref[...]
        d_next = jnp.where(rows == s - 1, 0.0, pltpu.roll(dconv, s - 1, 0))
        d_prev = jnp.where(rows == 0, 0.0, pltpu.roll(dconv, 1, 0))
        du = w0 * d_next + w1 * dconv + w2 * d_prev
        dcc_ref[...] = du * cx_v
        dcx_ref[...] = du * cc_v
        dw_ref[...] = jnp.zeros_like(dw_ref)
        dw_ref[0:1, :] = _colsum(dconv * up)
        dw_ref[1:2, :] = _colsum(dconv * u)
        dw_ref[2:3, :] = _colsum(dconv * un)

    in_specs, blk, wblk = _conv_specs(s, 4)
    v = _sds(cb.shape)
    return tuple(pl.pallas_call(
        body, name="conv_bwd", grid=(ch // CONV_COLS,), in_specs=in_specs, out_specs=[blk, blk, blk, wblk],
        out_shape=[v, v, v, _sds(w.shape)], compiler_params=_params(("parallel",)),
    )(cb, cc, cx, d, w))


conv_op.defvjp(_conv_f, _conv_b)


def _gla_masks(rev):
    c = GLA_CHUNK
    row = lax.broadcasted_iota(jnp.int32, (c, c), 0)
    col = lax.broadcasted_iota(jnp.int32, (c, c), 1)
    mask = (row < col) if rev else (row >= col)
    return rev, mask


def _chunk_cumsum(g, rev):
    c = g.shape[0]
    row = lax.broadcasted_iota(jnp.int32, g.shape, 0)
    b = g
    s = 1
    while s < c:
        if rev:
            b = b + jnp.where(row < c - s, pltpu.roll(b, c - s, 0), 0.0)
        else:
            b = b + jnp.where(row >= s, pltpu.roll(b, s, 0), 0.0)
        s *= 2
    return b


GLA_UNROLL = 4


def _gla_rows(n):
    return pl.ds(pl.multiple_of(n * GLA_CHUNK, GLA_CHUNK), GLA_CHUNK)


def _gla_scan(s_ref, bt_ref, st_ref, n_chunks, descending):
    st_ref[...] = jnp.zeros_like(st_ref)

    def step(i, carry):
        n = (n_chunks - 1 - i) if descending else i
        own = s_ref[n]
        st = st_ref[...]
        s_ref[n] = st
        st_ref[...] = st * jnp.exp(bt_ref[n]) + own
        return carry

    lax.fori_loop(0, n_chunks, step, 0)


GLA_PAIR = 2


def _gla_specs(s):
    dk, dv = GLA_DK, GLA_DV
    n_pairs = GLA_HEADS // GLA_PAIR
    blk_k = pl.BlockSpec((s, GLA_PAIR * dk), lambda p: (0, p))
    blk_gb = pl.BlockSpec((s, GLA_PAIR * dk), lambda p: (0, n_pairs + p))
    blk_v = pl.BlockSpec((s, GLA_PAIR * dv), lambda p: (0, p))
    return n_pairs, blk_k, blk_gb, blk_v


def _head_lanes(hh):
    lane = lax.broadcasted_iota(jnp.int32, (1, GLA_PAIR * GLA_DK), 1)
    return jnp.logical_and(lane >= hh * GLA_DK, lane < (hh + 1) * GLA_DK)


def _gla_fwd(q, k, v, la):
    s = q.shape[0]
    dk, dv = GLA_DK, GLA_DV
    pw = GLA_PAIR * dk
    n_chunks = s // GLA_CHUNK
    scale = GLA_DK ** -0.5

    def body(q_ref, k_ref, v_ref, gf_ref, gb_ref, o_ref, sf_ref, sb_ref, bf_ref, bb_ref, btf_ref, btb_ref, st_ref):
        masks = [_gla_masks(rev) for rev in (False, True)]
        dirs = ((False, gf_ref, sf_ref, bf_ref, btf_ref), (True, gb_ref, sb_ref, bb_ref, btb_ref))

        def decays(n, carry):
            rows = _gla_rows(n)
            for rev, g_ref, _, b_ref, bt_ref in dirs:
                g = g_ref[rows, :]
                b_ref[rows, :] = _chunk_cumsum(g, rev)
                bt_ref[n] = _colsum(g)
            return carry

        lax.fori_loop(0, n_chunks, decays, 0, unroll=GLA_UNROLL)
        for hh in range(GLA_PAIR):
            m = _head_lanes(hh)
            vl = slice(hh * dv, (hh + 1) * dv)

            def prepare(n, carry, m=m, vl=vl):
                rows = _gla_rows(n)
                kk = k_ref[rows, :]
                vb = v_ref[rows, vl].astype(BF16)
                for rev, _, s_ref, b_ref, bt_ref in dirs:
                    ke = jnp.where(m, kk * jnp.exp(bt_ref[n] - b_ref[rows, :]), 0.0).astype(BF16)
                    s_ref[n] = _dot(vb, ke, 0, 0)
                return carry

            lax.fori_loop(0, n_chunks, prepare, 0, unroll=GLA_UNROLL)
            for rev, _, s_ref, _, bt_ref in dirs:
                _gla_scan(s_ref, bt_ref, st_ref, n_chunks, descending=rev)

            def emit(n, carry, m=m, vl=vl):
                rows = _gla_rows(n)
                qs = q_ref[rows, :] * scale
                kk = k_ref[rows, :]
                vb = v_ref[rows, vl].astype(BF16)
                o = None
                for (rev, _, s_ref, b_ref, _), (_, mask) in zip(dirs, masks):
                    b = b_ref[rows, :]
                    qd = jnp.where(m, qs * jnp.exp(b), 0.0).astype(BF16)
                    ki = jnp.where(m, kk * jnp.exp(-b), 0.0).astype(BF16)
                    a = jnp.where(mask, _dot(qd, ki, 1, 1), 0.0).astype(BF16)
                    od = _dot(a, vb, 1, 0) + _dot(qd, s_ref[n].astype(BF16), 1, 1)
                    o = od if o is None else o + od
                o_ref[rows, vl] = o
                return carry

            lax.fori_loop(0, n_chunks, emit, 0, unroll=GLA_UNROLL)

    n_pairs, blk_k, blk_gb, blk_v = _gla_specs(s)
    state = pltpu.VMEM((n_chunks, dv, pw), F32)
    scratch = [state, state, pltpu.VMEM((s, pw), F32), pltpu.VMEM((s, pw), F32), pltpu.VMEM((n_chunks, 1, pw), F32),
               pltpu.VMEM((n_chunks, 1, pw), F32), pltpu.VMEM((dv, pw), F32)]
    return pl.pallas_call(
        body, name="gla_fwd", grid=(n_pairs,), in_specs=[blk_k, blk_k, blk_v, blk_k, blk_gb],
        out_specs=blk_v, out_shape=_sds(v.shape), scratch_shapes=scratch,
        compiler_params=_params(("parallel",)),
    )(q, k, v, la, la)


def _gla_bwd(q, k, v, la, do):
    s = q.shape[0]
    dk, dv = GLA_DK, GLA_DV
    pw = GLA_PAIR * dk
    c = GLA_CHUNK
    n_chunks = s // c
    scale = GLA_DK ** -0.5

    def body(q_ref, k_ref, v_ref, gf_ref, gb_ref, do_ref, dq_ref, dk_ref, dv_ref, dgf_ref, dgb_ref,
             sf_ref, sb_ref, bf_ref, bb_ref, btf_ref, btb_ref, dsf_ref, dsb_ref, st_ref):
        masks = [_gla_masks(rev) for rev in (False, True)]
        rowc = lax.broadcasted_iota(jnp.int32, (c, pw), 0)
        dirs = ((False, gf_ref, sf_ref, bf_ref, btf_ref, dsf_ref, dgf_ref),
                (True, gb_ref, sb_ref, bb_ref, btb_ref, dsb_ref, dgb_ref))

        def decays(n, carry):
            rows = _gla_rows(n)
            for rev, g_ref, _, b_ref, bt_ref, _, _ in dirs:
                g = g_ref[rows, :]
                b_ref[rows, :] = _chunk_cumsum(g, rev)
                bt_ref[n] = _colsum(g)
            return carry

        lax.fori_loop(0, n_chunks, decays, 0, unroll=GLA_UNROLL)
        for hh in range(GLA_PAIR):
            m = _head_lanes(hh)
            vl = slice(hh * dv, (hh + 1) * dv)

            def prepare(n, carry, m=m, vl=vl):
                rows = _gla_rows(n)
                qs = q_ref[rows, :] * scale
                kk = k_ref[rows, :]
                vb = v_ref[rows, vl].astype(BF16)
                do_b = do_ref[rows, vl].astype(BF16)
                for rev, _, s_ref, b_ref, bt_ref, ds_ref, _ in dirs:
                    b = b_ref[rows, :]
                    ke = jnp.where(m, kk * jnp.exp(bt_ref[n] - b), 0.0).astype(BF16)
                    qd = jnp.where(m, qs * jnp.exp(b), 0.0).astype(BF16)
                    s_ref[n] = _dot(vb, ke, 0, 0)
                    ds_ref[n] = _dot(do_b, qd, 0, 0)
                return carry

            lax.fori_loop(0, n_chunks, prepare, 0, unroll=GLA_UNROLL)
            for rev, _, s_ref, _, bt_ref, ds_ref, _ in dirs:
                _gla_scan(s_ref, bt_ref, st_ref, n_chunks, descending=rev)
                _gla_scan(ds_ref, bt_ref, st_ref, n_chunks, descending=not rev)

            def emit(n, carry, m=m, vl=vl, first=(hh == 0)):
                rows = _gla_rows(n)
                qs = q_ref[rows, :] * scale
                kk = k_ref[rows, :]
                vb = v_ref[rows, vl].astype(BF16)
                do_b = do_ref[rows, vl].astype(BF16)
                dq = dkk = dvv = None
                for (rev, _, s_ref, b_ref, bt_ref, ds_ref, dg_ref), (_, mask) in zip(dirs, masks):
                    b = b_ref[rows, :]
                    bt = bt_ref[n]
                    eb = jnp.where(m, jnp.exp(b), 0.0)
                    enb = jnp.where(m, jnp.exp(-b), 0.0)
                    etb = jnp.where(m, jnp.exp(bt - b), 0.0)
                    ebt = jnp.exp(bt)
                    qd = qs * eb
                    ki = kk * enb
                    ke = kk * etb
                    qd_b, ki_b, ke_b = qd.astype(BF16), ki.astype(BF16), ke.astype(BF16)
                    st = s_ref[n]
                    dst = ds_ref[n]
                    dst_b = dst.astype(BF16)
                    a = jnp.where(mask, _dot(qd_b, ki_b, 1, 1), 0.0).astype(BF16)
                    da = jnp.where(mask, _dot(do_b, vb, 1, 1), 0.0).astype(BF16)
                    dv_d = _dot(a, do_b, 0, 0) + _dot(ke_b, dst_b, 1, 1)
                    dqd = _dot(da, ki_b, 1, 0) + _dot(do_b, st.astype(BF16), 1, 0)
                    dki = _dot(da, qd_b, 0, 0)
                    dke = _dot(vb, dst_b, 1, 0)
                    dbt = _colsum(st * dst) * ebt + _colsum(dke * ke)
                    db = dqd * qd - dki * ki - dke * ke
                    db = db + jnp.where(rowc == (0 if rev else c - 1), dbt, 0.0)
                    dg = _chunk_cumsum(db, not rev)
                    if first:
                        dg_ref[rows, :] = dg
                    else:
                        dg_ref[rows, :] += dg
                    dq_d = dqd * eb * scale
                    dk_d = dki * enb + dke * etb
                    dq = dq_d if dq is None else dq + dq_d
                    dkk = dk_d if dkk is None else dkk + dk_d
                    dvv = dv_d if dvv is None else dvv + dv_d
                if first:
                    dq_ref[rows, :] = dq
                    dk_ref[rows, :] = dkk
                else:
                    dq_ref[rows, :] += dq
                    dk_ref[rows, :] += dkk
                dv_ref[rows, vl] = dvv
                return carry

            lax.fori_loop(0, n_chunks, emit, 0, unroll=GLA_UNROLL)

    n_pairs, blk_k, blk_gb, blk_v = _gla_specs(s)
    vk, vv = _sds(q.shape), _sds(v.shape)
    state = pltpu.VMEM((n_chunks, dv, pw), F32)
    scratch = [state, state, pltpu.VMEM((s, pw), F32), pltpu.VMEM((s, pw), F32), pltpu.VMEM((n_chunks, 1, pw), F32),
               pltpu.VMEM((n_chunks, 1, pw), F32), state, state, pltpu.VMEM((dv, pw), F32)]
    return pl.pallas_call(
        body, name="gla_bwd", grid=(n_pairs,), in_specs=[blk_k, blk_k, blk_v, blk_k, blk_gb, blk_v],
        out_specs=[blk_k, blk_k, blk_v, blk_k, blk_k], out_shape=[vk, vk, vv, vk, vk],
        scratch_shapes=scratch, compiler_params=_params(("parallel",)),
    )(q, k, v, la, la, do)


@jax.custom_vjp
def gla(q, k, v, la):
    return _gla_fwd(q, k, v, la)


def _gla_f(q, k, v, la):
    return _gla_fwd(q, k, v, la), (q, k, v, la)


def _gla_b(res, do):
    dq, dk, dv, dgf, dgb = _gla_bwd(*res, do)
    return dq, dk, dv, jnp.concatenate([dgf, dgb], axis=1)


gla.defvjp(_gla_f, _gla_b)


ATTN_TQ = 256
ATTN_TQ_BWD = 512
HEAD_LANES = 128


def _attn_blocks(s, tq):
    per_q = pl.BlockSpec((tq, HEAD_LANES), lambda h, j: (j, h))
    k_nope = pl.BlockSpec((s, HEAD_LANES), lambda h, j: (0, 2 * h))
    v_blk = pl.BlockSpec((s, HEAD_LANES), lambda h, j: (0, 2 * h + 1))
    k_rope = pl.BlockSpec((s, HEAD_LANES), lambda h, j: (0, 0))
    lse = pl.BlockSpec((1, tq, 1), lambda h, j: (h, j, 0))
    return per_q, k_nope, v_blk, k_rope, lse


def _attn_fwd(qn, qr, kv, kr):
    s = qn.shape[0]
    tq = min(ATTN_TQ, s)
    scale = (MLA_NOPE + MLA_ROPE) ** -0.5

    def body(qn_ref, qr_ref, kn_ref, v_ref, kr_ref, o_ref, lse_ref):
        q = jnp.concatenate([qn_ref[...], qr_ref[...]], axis=1)
        k = jnp.concatenate([kn_ref[...], kr_ref[...]], axis=1)
        sc = _dot(q, k, 1, 1) * scale
        m = jnp.max(sc, axis=-1, keepdims=True)
        p = jnp.exp(sc - m)
        l = jnp.sum(p, axis=-1, keepdims=True)
        p = p * (1.0 / l)
        o_ref[...] = _dot(p.astype(BF16), v_ref[...], 1, 0)
        lse_ref[0] = m + jnp.log(l)

    per_q, k_nope, v_blk, k_rope, lse = _attn_blocks(s, tq)
    return pl.pallas_call(
        body, name="attn_fwd", grid=(MLA_HEADS, s // tq), in_specs=[per_q, per_q, k_nope, v_blk, k_rope],
        out_specs=[per_q, lse], out_shape=[_sds(qn.shape), _sds((MLA_HEADS, s, 1))],
        compiler_params=_params(("parallel", "parallel")),
    )(qn, qr, kv, kv, kr)


def _attn_bwd(qn, qr, kv, kr, o, lse, do):
    s = qn.shape[0]
    tq = min(ATTN_TQ_BWD, s)
    n_q = s // tq
    scale = (MLA_NOPE + MLA_ROPE) ** -0.5

    def body(qn_ref, qr_ref, kn_ref, v_ref, kr_ref, o_ref, lse_ref, do_ref, dqn_ref, dqr_ref, dkv_ref, dkr_ref,
             dk_acc, dv_acc, dkr_acc):
        h, j = pl.program_id(0), pl.program_id(1)
        q = jnp.concatenate([qn_ref[...], qr_ref[...]], axis=1)
        k = jnp.concatenate([kn_ref[...], kr_ref[...]], axis=1)
        do = do_ref[...]
        do_b = do.astype(BF16)
        p = jnp.exp(_dot(q, k, 1, 1) * scale - lse_ref[0])
        dp = _dot(do_b, v_ref[...], 1, 1)
        delta = jnp.sum(do * o_ref[...], axis=-1, keepdims=True)
        ds = (p * (dp - delta) * scale).astype(BF16)
        dq = _dot(ds, k, 1, 0)
        dqn_ref[...] = dq[:, :HEAD_LANES].astype(BF16)
        dqr_ref[...] = dq[:, HEAD_LANES:].astype(BF16)
        dk = _dot(ds, q, 0, 0)
        _acc(j, dk_acc, dk[:, :HEAD_LANES])
        _acc(j, dv_acc, _dot(p.astype(BF16), do_b, 0, 0))
        _acc(jnp.where(jnp.logical_and(h == 0, j == 0), 0, 1), dkr_acc, dk[:, HEAD_LANES:])

        @pl.when(j == n_q - 1)
        def _():
            dkv_ref[:, 0:HEAD_LANES] = dk_acc[...].astype(BF16)
            dkv_ref[:, HEAD_LANES:2 * HEAD_LANES] = dv_acc[...].astype(BF16)

        @pl.when(jnp.logical_and(h == MLA_HEADS - 1, j == n_q - 1))
        def _():
            dkr_ref[...] = dkr_acc[...].astype(BF16)

    per_q, k_nope, v_blk, k_rope, lse_blk = _attn_blocks(s, tq)
    dkv_blk = pl.BlockSpec((s, 2 * HEAD_LANES), lambda h, j: (0, h))
    acc = pltpu.VMEM((s, HEAD_LANES), F32)
    return pl.pallas_call(
        body, name="attn_bwd", grid=(MLA_HEADS, n_q),
        in_specs=[per_q, per_q, k_nope, v_blk, k_rope, per_q, lse_blk, per_q],
        out_specs=[per_q, per_q, dkv_blk, k_rope],
        out_shape=[_sds(qn.shape, BF16), _sds(qr.shape, BF16), _sds(kv.shape, BF16), _sds(kr.shape, BF16)],
        scratch_shapes=[acc, acc, acc], compiler_params=_params(("arbitrary", "arbitrary")),
    )(qn, qr, kv, kv, kr, o, lse, do)


@jax.custom_vjp
def attn(qn, qr, kv, kr):
    return _attn_fwd(qn, qr, kv, kr)[0]


def _attn_f(qn, qr, kv, kr):
    o, lse = _attn_fwd(qn, qr, kv, kr)
    return o, (qn, qr, kv, kr, o, lse)


def _attn_b(res, do):
    return tuple(_attn_bwd(*res, do))


attn.defvjp(_attn_f, _attn_b)


@jax.custom_vjp
def split_proj(proj):
    out, at = [], 0
    for _, _, _, wp in PROJ_SEGS:
        out.append(proj[:, at:at + wp])
        at += wp
    return tuple(out)


def _split_f(proj):
    return split_proj(proj), None


def _split_b(_, gs):
    return (jnp.concatenate(gs, axis=1),)


split_proj.defvjp(_split_f, _split_b)


def _tile2d(rows, width, limit=BLOCK_BYTES):
    fits = [t for t in range(16, rows + 1, 16) if rows % t == 0 and t * width * 4 <= limit]
    if fits and (fits[-1] >= 64 or fits[-1] == rows):
        return fits[-1], width
    if rows * width * 4 <= limit:
        return rows, width
    cols = [t for t in range(128, width + 1, 128) if width % t == 0 and rows * t * 4 <= limit]
    return (rows, cols[-1]) if cols else (rows, width)


def _add_pair(stacked, theirs, c_idx):
    g, r, w = theirs.shape
    tr, tc = _tile2d(r, w)

    def body(c_ref, a_ref, b_ref, o_ref):
        o_ref[0] = (a_ref[0, 0].astype(F32) + b_ref[0].astype(F32)).astype(BF16)

    blk = pl.BlockSpec((1, tr, tc), lambda k, i, j, c: (k, i, j))
    spec = pltpu.PrefetchScalarGridSpec(
        num_scalar_prefetch=1, grid=(g, r // tr, w // tc),
        in_specs=[pl.BlockSpec((1, 1, tr, tc), lambda k, i, j, c: (c[0], k, i, j)), blk], out_specs=blk)
    return pl.pallas_call(body, name="add_pair", grid_spec=spec, out_shape=_sds(theirs.shape, BF16),
                          compiler_params=_params(("parallel", "parallel", "parallel")))(c_idx, stacked, theirs)


def _add_chips(pair, landed, chip_idx):
    _, r, w = pair.shape
    tr, tc = _tile2d(r, w)

    def body(c_ref, p_ref, l0_ref, l1_ref, l2_ref, o_ref):
        o_ref[...] = ((p_ref[0].astype(F32) + l0_ref[0].astype(F32)) + l1_ref[0].astype(F32)) + l2_ref[0].astype(F32)

    specs = [pl.BlockSpec((1, tr, tc), lambda i, j, c: (c[0], i, j))]
    specs += [pl.BlockSpec((1, tr, tc), functools.partial(lambda i, j, c, k: (k, i, j), k=k)) for k in range(N_CHIPS - 1)]
    spec = pltpu.PrefetchScalarGridSpec(num_scalar_prefetch=1, grid=(r // tr, w // tc), in_specs=specs,
                                        out_specs=pl.BlockSpec((tr, tc), lambda i, j, c: (i, j)))
    return pl.pallas_call(body, name="add_chips", grid_spec=spec, out_shape=_sds((r, w)),
                          compiler_params=_params(("parallel", "parallel")))(chip_idx, pair, landed, landed, landed)


def _sum_devices(g):
    n = g.shape[2]

    def body(g_ref, o_ref, done_ref):
        t = g_ref[0]
        for j in range(1, N_DEV):
            t = t + g_ref[j]
        o_ref[...] = t
        done_ref[...] = jnp.zeros_like(done_ref)

    return pl.pallas_call(body, name="sum_devices", out_shape=[_sds((1, n)), _sds((8, 128))],
                          compiler_params=_params())(g)


def _adamw_math(w, gv, m, v):
    c1 = 1.0 - ADAM_B1 ** ADAM_STEP
    c2 = 1.0 - ADAM_B2 ** ADAM_STEP
    mn = ADAM_B1 * m + (1.0 - ADAM_B1) * gv
    vn = ADAM_B2 * v + (1.0 - ADAM_B2) * (gv * gv)
    return -ADAM_LR * ((mn / c1) / (jnp.sqrt(vn / c2) + ADAM_EPS) + ADAM_WD * w), mn, vn


def _adamw(w, g, m, v):
    shp = w.shape
    shp3 = (1, 1, shp[0]) if len(shp) == 1 else (-1,) + tuple(shp[-2:])
    w3, g3, m3, v3 = (t.reshape(shp3) for t in (w, g, m, v))

    def body(w_ref, g_ref, m_ref, v_ref, d_ref, mo_ref, vo_ref):
        d_ref[...], mo_ref[...], vo_ref[...] = _adamw_math(w_ref[...], g_ref[...], m_ref[...], v_ref[...])

    nl, r, wd = w3.shape
    tr, tc = _tile2d(r, wd, BLOCK_BYTES // 2)
    blk = pl.BlockSpec((1, tr, tc), lambda l, i, j: (l, i, j))
    s3 = _sds(w3.shape)
    d, mn, vn = pl.pallas_call(
        body, name="adamw", grid=(nl, r // tr, wd // tc), in_specs=[blk] * 4, out_specs=[blk] * 3,
        out_shape=[s3, s3, s3], compiler_params=_params(("parallel", "parallel", "parallel")),
    )(w3, g3, m3, v3)
    return d.reshape(shp), mn.reshape(shp), vn.reshape(shp)


PIECE_BYTES = 1 << 20


def _place():
    return lax.axis_index("x"), lax.axis_index("y"), lax.axis_index("c")


def _pieces(shape, itemsize):
    if len(shape) >= 3:
        return [(i,) + p for i in range(shape[0]) for p in _pieces(shape[1:], itemsize)]
    rows = shape[0]
    row_bytes = itemsize
    for dsz in shape[1:]:
        row_bytes *= dsz
    k = 1
    while rows % (2 * k) == 0 and (rows // (2 * k)) % 16 == 0 and (rows // k) * row_bytes > PIECE_BYTES:
        k *= 2
    step = rows // k
    return [(pl.ds(j * step, step),) for j in range(k)]


def _split_start(make, src, dst, pieces):
    for p in pieces:
        make(src.at[p], dst.at[p]).start()
    return make(src, dst)


def _comm_call(body, name, arrs, out_shapes, n_remote, n_local):
    return pl.pallas_call(
        body, name=name, in_specs=[ANY] * len(arrs), out_specs=[ANY] * len(out_shapes), out_shape=out_shapes,
        scratch_shapes=[pltpu.SemaphoreType.DMA((n_remote,)), pltpu.SemaphoreType.DMA((n_remote,)),
                        pltpu.SemaphoreType.DMA((n_local,))],
    )(*arrs)


def all_gather8(arrs, name):
    n = len(arrs)
    pieces = [_pieces(a.shape, a.dtype.itemsize) for a in arrs]

    def body(*refs):
        ins, outs = refs[:n], refs[n:2 * n]
        send, recv, _ = refs[2 * n:]
        x, y, c = _place()
        me, sib = (x, y, c), (x, y, 1 - c)
        chips = [(1 - x, y), (x, 1 - y), (1 - x, 1 - y)]

        def slot(p):
            return 4 * p[0] + 2 * p[1] + p[2]

        def maker(t, k, to):
            def make(s, d):
                return pltpu.make_async_remote_copy(src_ref=s, dst_ref=d, send_sem=send.at[7 * t + k],
                                                    recv_sem=recv.at[7 * t + k], device_id=to, device_id_type=MESH)
            return make

        def landing(t, k, block):
            dst = outs[t].at[slot(block)]
            return maker(t, k, me)(dst, dst)

        sent = []
        for t in range(n):
            dst = outs[t].at[slot(me)]
            sent.append(_split_start(maker(t, 0, sib), ins[t], dst, pieces[t]))
            for j, chip in enumerate(chips):
                sent.append(_split_start(maker(t, 1 + j, (*chip, c)), ins[t], dst, pieces[t]))
        for j, chip in enumerate(chips):
            for t in range(n):
                landing(t, 1 + j, (*chip, c)).wait_recv()
                blk = outs[t].at[slot((*chip, c))]
                sent.append(_split_start(maker(t, 4 + j, sib), blk, blk, pieces[t]))
        for t in range(n):
            landing(t, 0, sib).wait_recv()
            for j, chip in enumerate(chips):
                landing(t, 4 + j, (*chip, 1 - c)).wait_recv()
        for cp in sent:
            cp.wait_send()

    outs = [_sds((N_DEV,) + a.shape, a.dtype) for a in arrs]
    got = _comm_call(body, name, arrs, outs, 7 * n, 1)
    x, y, c = _place()
    return [lax.dynamic_update_index_in_dim(g, a, 4 * x + 2 * y + c, 0) for g, a in zip(got, arrs)]


def sibling_send(arrs, name):
    n = len(arrs)
    pieces = [_pieces(a.shape[1:], a.dtype.itemsize) for a in arrs]

    def body(*refs):
        ins, theirs = refs[:n], refs[n:2 * n]
        send, recv, _ = refs[2 * n:]
        x, y, c = _place()
        rem = []
        for t in range(n):
            def make(s, d, t=t):
                return pltpu.make_async_remote_copy(src_ref=s, dst_ref=d, send_sem=send.at[t], recv_sem=recv.at[t],
                                                    device_id=(x, y, 1 - c), device_id_type=MESH)
            rem.append(_split_start(make, ins[t].at[1 - c], theirs[t], pieces[t]))
        for cp in rem:
            cp.wait_recv()
        for cp in rem:
            cp.wait_send()

    outs = [_sds(a.shape[1:], a.dtype) for a in arrs]
    return _comm_call(body, name, arrs, outs, n, 1)


def exchange_chips(arrs, name):
    n = len(arrs)
    pieces = [_pieces(a.shape[1:], a.dtype.itemsize) for a in arrs]

    def body(*refs):
        ins, outs = refs[:n], refs[n:2 * n]
        send, recv, _ = refs[2 * n:]
        x, y, c = _place()
        peers = [(1 - x, y), (x, 1 - y), (1 - x, 1 - y)]
        rem = []
        for t in range(n):
            for j, (px, py) in enumerate(peers):
                def make(s, d, t=t, j=j, px=px, py=py):
                    return pltpu.make_async_remote_copy(
                        src_ref=s, dst_ref=d, send_sem=send.at[3 * t + j], recv_sem=recv.at[3 * t + j],
                        device_id=(px, py, c), device_id_type=MESH)
                rem.append(_split_start(make, ins[t].at[2 * px + py], outs[t].at[j], pieces[t]))
        for cp in rem:
            cp.wait_recv()
        for cp in rem:
            cp.wait_send()

    outs = [_sds((N_CHIPS - 1,) + a.shape[1:], a.dtype) for a in arrs]
    return _comm_call(body, name, arrs, outs, 3 * n, 1)


def sibling_swap(arrs, name):
    n = len(arrs)
    pieces = [_pieces(a.shape, a.dtype.itemsize) for a in arrs]

    def body(*refs):
        ins, outs = refs[:n], refs[n:2 * n]
        send, recv, _ = refs[2 * n:]
        x, y, c = _place()
        rem = []
        for t in range(n):
            def make(s, d, t=t):
                return pltpu.make_async_remote_copy(src_ref=s, dst_ref=d, send_sem=send.at[t], recv_sem=recv.at[t],
                                                    device_id=(x, y, 1 - c), device_id_type=MESH)
            rem.append(_split_start(make, ins[t], outs[t], pieces[t]))
        for cp in rem:
            cp.wait_recv()
        for cp in rem:
            cp.wait_send()

    outs = [_sds(a.shape, a.dtype) for a in arrs]
    return _comm_call(body, name, arrs, outs, n, 1)


def _peer_copies(srcs, lands, send, recv, mode):
    x, y, c = _place()
    my_chip = 2 * x + y
    out = []
    for t in range(len(srcs)):
        for j, (px, py) in enumerate([(1 - x, y), (x, 1 - y), (1 - x, 1 - y)]):
            if mode == "gather":
                s, dst = srcs[t], lands[t].at[c, my_chip]
            else:
                s, dst = srcs[t].at[2 * px + py], lands[t].at[j]
            out.append(pltpu.make_async_remote_copy(
                src_ref=s, dst_ref=dst, send_sem=send.at[3 * t + j], recv_sem=recv.at[3 * t + j],
                device_id=(px, py, c), device_id_type=MESH))
    return out


HBM = pl.BlockSpec(memory_space=pltpu.HBM)
SEM = pl.BlockSpec(memory_space=pltpu.SEMAPHORE)
EFFECT = pltpu.SideEffectType.DATAFLOW_SIDE_EFFECTING


def ici_start(srcs, lands, mode, name):
    n = len(srcs)

    def body(*refs):
        send, recv = refs[2 * n], refs[2 * n + 1]
        for cp in _peer_copies(refs[:n], refs[n:2 * n], send, recv, mode):
            cp.start()
        refs[-1][...] = jnp.zeros_like(refs[-1])

    thru = [pltpu.HBM(a.shape, a.dtype) for a in list(srcs) + list(lands)]
    outs = pl.pallas_call(
        body, name=name, in_specs=[HBM] * (2 * n), out_specs=[SEM, SEM] + [HBM] * (2 * n) + [pl.BlockSpec(memory_space=pltpu.VMEM)],
        out_shape=[pltpu.SemaphoreType.DMA((3 * n,)), pltpu.SemaphoreType.DMA((3 * n,))] + thru + [_sds((8, 128))],
        input_output_aliases={i: 2 + i for i in range(2 * n)},
        compiler_params=pltpu.CompilerParams(has_side_effects=EFFECT),
    )(*[pltpu.with_memory_space_constraint(a, pltpu.HBM) for a in list(srcs) + list(lands)])
    return dict(send=outs[0], recv=outs[1], srcs=outs[2:2 + n], lands=outs[2 + n:2 + 2 * n], token=outs[-1])


def ici_wait(handle, after, mode, name):
    n = len(handle["srcs"])

    def body(*refs):
        send, recv = refs[2 * n], refs[2 * n + 1]
        for cp in _peer_copies(refs[:n], refs[n:2 * n], send, recv, mode):
            cp.wait_send()
            cp.wait_recv()

    arrs = list(handle["srcs"]) + list(handle["lands"])
    outs = pl.pallas_call(
        body, name=name, in_specs=[HBM] * (2 * n) + [SEM, SEM, ANY], out_specs=[HBM] * (2 * n),
        out_shape=[pltpu.HBM(a.shape, a.dtype) for a in arrs], input_output_aliases={i: i for i in range(2 * n)},
        compiler_params=pltpu.CompilerParams(has_side_effects=EFFECT),
    )(*arrs, handle["send"], handle["recv"], after)
    return outs[:n], outs[n:]


def gather_share(blocks, lands, name):
    n = len(blocks)

    def body(*refs):
        own, buf = refs[:n], refs[2 * n:3 * n]
        done, send, recv = refs[3 * n:]
        x, y, c = _place()
        my_chip = 2 * x + y
        chips = [2 * (1 - x) + y, 2 * x + (1 - y), 2 * (1 - x) + (1 - y)]
        sent = []
        for t in range(n):
            def make(s, d, k, t=t):
                return pltpu.make_async_remote_copy(src_ref=s, dst_ref=d, send_sem=send.at[4 * t + k],
                                                    recv_sem=recv.at[4 * t + k], device_id=(x, y, 1 - c),
                                                    device_id_type=MESH)
            cp = make(own[t], buf[t].at[c, my_chip], 0)
            cp.start()
            sent.append(cp)
            for k, pc in enumerate(chips):
                cp = make(buf[t].at[c, pc], buf[t].at[c, pc], 1 + k)
                cp.start()
                sent.append(cp)
        for t in range(n):
            for k in range(4):
                got = buf[t].at[1 - c, k]
                pltpu.make_async_remote_copy(src_ref=got, dst_ref=got, send_sem=send.at[4 * t + k],
                                             recv_sem=recv.at[4 * t + k], device_id=(x, y, 1 - c),
                                             device_id_type=MESH).wait_recv()
        for cp in sent:
            cp.wait_send()
        done[...] = jnp.zeros_like(done)

    outs = pl.pallas_call(
        body, name=name, in_specs=[ANY] * (2 * n), out_specs=[ANY] * n + [pl.BlockSpec(memory_space=pltpu.VMEM)],
        out_shape=[_sds(a.shape, a.dtype) for a in lands] + [_sds((8, 128))],
        input_output_aliases={n + t: t for t in range(n)},
        scratch_shapes=[pltpu.SemaphoreType.DMA((4 * n,)), pltpu.SemaphoreType.DMA((4 * n,))],
    )(*blocks, *lands)
    return outs[:n], outs[n]


@jax.custom_vjp
def _build_w_in(w4):
    full = w4.reshape(-1, w4.shape[-1])
    parts = []
    for _, start, width, wp in PROJ_SEGS:
        if width:
            parts.append(full[start:start + width])
        if wp > width:
            parts.append(jnp.zeros((wp - width, full.shape[1]), full.dtype))
    return jnp.concatenate(parts, axis=0)


def _build_w_in_f(w4):
    return _build_w_in(w4), None


def _build_w_in_b(_, g):
    parts, at = [], 0
    for _, _, width, wp in PROJ_SEGS:
        if width:
            parts.append(g[at:at + width])
        at += wp
    return (jnp.concatenate(parts, axis=0).reshape(N_CHIPS, -1, g.shape[1]),)


_build_w_in.defvjp(_build_w_in_f, _build_w_in_b)


def _split_w_uq(w):
    w3 = w.reshape(w.shape[0], MLA_HEADS, MLA_NOPE + MLA_ROPE)
    return w3[:, :, :MLA_NOPE].reshape(w.shape[0], -1), w3[:, :, MLA_NOPE:].reshape(w.shape[0], -1)


def _swap_halves(t, width):
    t3 = t.reshape(t.shape[0], -1, 2, width // 2)
    return jnp.concatenate([t3[:, :, 1:], t3[:, :, :1]], axis=2).reshape(t.shape)


def _pad_heads(t, width):
    t3 = t.reshape(t.shape[0], -1, width)
    t3 = jnp.pad(t3, ((0, 0), (0, 0), (0, HEAD_LANES - width)))
    return t3.reshape(t.shape[0], -1).astype(BF16)


def _layer(xh, mod, big, small, rope_q, rope_k):
    d = D_MODEL
    shift, scale, gate = mod[None, 0:d], mod[None, d:2 * d], mod[None, 2 * d:3 * d]
    w_al = _build_w_in(big["w_in"])
    proj = mod_mm(xh, small["norm_g"][None], scale, shift, w_al)
    gq, gk, gv, glr, mq, mkv, mkr, cb, cc, cx, _, z = split_proj(proj)

    rk = GLA_RANK
    hk = GLA_HEADS * GLA_DK
    wg = jnp.zeros((128, 2 * hk), F32)
    wg = wg.at[0:rk, 0:hk].set(small["gla_wg_f"]).at[rk:2 * rk, hk:].set(small["gla_wg_b"])
    bg = jnp.concatenate([small["gla_bg_f"], small["gla_bg_b"]])[None]
    la = gate_act(mm(glr, wg), bg)
    o_gla = rmsnorm(gla(gq, gk, gv, la), small["gla_norm_g"][None])

    cq = rmsnorm(mq, small["mla_q_norm_g"][None])
    w_nope, w_rope = _split_w_uq(jnp.concatenate([big["w_uq"][j] for j in range(N_CHIPS)], axis=1))
    qn = mm16(cq, w_nope)
    qr = mm(cq, w_rope)
    qr = fma(qr, rope_q[0], _swap_halves(qr, MLA_ROPE), rope_q[1])
    ckv = rmsnorm(mkv, small["mla_kv_norm_g"][None])
    kv = mm16(ckv, jnp.concatenate([big["w_ukv"][j] for j in range(N_CHIPS)], axis=1))
    kr = mkr[:, :MLA_ROPE]
    kr = fma(kr, rope_k[0], _swap_halves(kr, MLA_ROPE), rope_k[1])
    o_mla = rmsnorm(attn(qn, _pad_heads(qr, MLA_ROPE), kv, _pad_heads(kr, MLA_ROPE)), small["mla_out_g"][None])

    cw = jnp.concatenate([small["conv_w"], jnp.zeros((5, CONV_CH), F32)], axis=0)
    o_conv = rmsnorm(conv_op(cb, cc, cx, cw), small["conv_out_g"][None])

    o = jnp.concatenate([o_gla, o_mla, o_conv], axis=1)
    w_out = big["w_out"].reshape(d, d)
    return out_block(o, z, w_out, xh, gate)


SMALL_REPL = ("norm_g", "gla_bg_f", "gla_bg_b", "gla_norm_g", "mla_q_norm_g", "mla_kv_norm_g", "mla_out_g",
              "conv_out_g")
SMALL_SHARDED = ("gla_wg_f", "gla_wg_b", "conv_w")
BIG = ("w_in", "w_out", "w_uq", "w_ukv")
HALF_AXIS = (1, 0, 0, 0)


def kernel(x, c, positions, ada_w, ada_b, norm_g, w_in, gla_wg_f, gla_bg_f, gla_wg_b, gla_bg_b, gla_norm_g, mla_q_norm_g, mla_kv_norm_g, mla_w_uq, mla_w_ukv, mla_out_g, conv_w, conv_out_g, w_out, final_g, loss_target, m_ada_w, m_ada_b, m_norm_g, m_w_in, m_gla_wg_f, m_gla_bg_f, m_gla_wg_b, m_gla_bg_b, m_gla_norm_g, m_mla_q_norm_g, m_mla_kv_norm_g, m_mla_w_uq, m_mla_w_ukv, m_mla_out_g, m_conv_w, m_conv_out_g, m_w_out, m_final_g, v_ada_w, v_ada_b, v_norm_g, v_w_in, v_gla_wg_f, v_gla_bg_f, v_gla_wg_b, v_gla_bg_b, v_gla_norm_g, v_mla_q_norm_g, v_mla_kv_norm_g, v_mla_w_uq, v_mla_w_ukv, v_mla_out_g, v_conv_w, v_conv_out_g, v_w_out, v_final_g):
    xi, yi, ci = _place()
    chip = 2 * xi + yi
    dev = 2 * chip + ci
    s = x.shape[1]
    d = D_MODEL
    weights = dict(ada_w=ada_w, ada_b=ada_b, norm_g=norm_g, w_in=w_in, gla_wg_f=gla_wg_f, gla_bg_f=gla_bg_f,
                   gla_wg_b=gla_wg_b, gla_bg_b=gla_bg_b, gla_norm_g=gla_norm_g, mla_q_norm_g=mla_q_norm_g,
                   mla_kv_norm_g=mla_kv_norm_g, mla_w_uq=mla_w_uq, mla_w_ukv=mla_w_ukv, mla_out_g=mla_out_g,
                   conv_w=conv_w, conv_out_g=conv_out_g, w_out=w_out, final_g=final_g)
    m_in = dict(ada_w=m_ada_w, ada_b=m_ada_b, norm_g=m_norm_g, w_in=m_w_in, gla_wg_f=m_gla_wg_f, gla_bg_f=m_gla_bg_f,
                gla_wg_b=m_gla_wg_b, gla_bg_b=m_gla_bg_b, gla_norm_g=m_gla_norm_g, mla_q_norm_g=m_mla_q_norm_g,
                mla_kv_norm_g=m_mla_kv_norm_g, mla_w_uq=m_mla_w_uq, mla_w_ukv=m_mla_w_ukv, mla_out_g=m_mla_out_g,
                conv_w=m_conv_w, conv_out_g=m_conv_out_g, w_out=m_w_out, final_g=m_final_g)
    v_in = dict(ada_w=v_ada_w, ada_b=v_ada_b, norm_g=v_norm_g, w_in=v_w_in, gla_wg_f=v_gla_wg_f, gla_bg_f=v_gla_bg_f,
                gla_wg_b=v_gla_wg_b, gla_bg_b=v_gla_bg_b, gla_norm_g=v_gla_norm_g, mla_q_norm_g=v_mla_q_norm_g,
                mla_kv_norm_g=v_mla_kv_norm_g, mla_w_uq=v_mla_w_uq, mla_w_ukv=v_mla_w_ukv, mla_out_g=v_mla_out_g,
                conv_w=v_conv_w, conv_out_g=v_conv_out_g, w_out=v_w_out, final_g=v_final_g)

    g_c, g_wgf, g_wgb, g_cw = all_gather8([c, gla_wg_f, gla_wg_b, conv_w], "gather_small")

    def unshard_cols(g):
        g4 = g[0::2]
        return g4.transpose(1, 2, 0, 3).reshape(g4.shape[1], g4.shape[2], -1)

    small_full = dict(gla_wg_f=unshard_cols(g_wgf), gla_wg_b=unshard_cols(g_wgb), conv_w=unshard_cols(g_cw))
    for nme in SMALL_REPL:
        small_full[nme] = weights[nme]
    smalls = [{nme: small_full[nme][l] for nme in SMALL_REPL + SMALL_SHARDED} for l in range(DEPTH)]

    big_src = (jnp.swapaxes(w_in, 1, 2), w_out, mla_w_uq, mla_w_ukv)

    def my_halves(l, zero=0):
        out = []
        for t, a in enumerate(big_src):
            n_half = a.shape[1 + HALF_AXIS[t]] // 2
            out.append(lax.dynamic_slice_in_dim(a[l], ci * n_half + zero, n_half, axis=HALF_AXIS[t]).astype(BF16))
        return out

    def landing(blocks):
        return [lax.empty((2, N_CHIPS) + b.shape, b.dtype) for b in blocks]

    def finish_gather(handle, after, tag):
        blocks, lands = ici_wait(handle, after, "gather", "gather_wait" + tag)
        lands, done = gather_share(blocks, lands, "gather_share" + tag)
        full = [lax.dynamic_update_slice(g, b[None, None], (ci, chip) + (0,) * b.ndim) for g, b in zip(lands, blocks)]
        return full, done

    halves0 = my_halves(0)
    started0 = ici_start(halves0, landing(halves0), "gather", "gather_start0")

    c_act = _silu_rows(g_c[:, 0, :])
    c_act16 = jnp.concatenate([c_act, jnp.zeros_like(c_act)], axis=0)
    n_ada = ada_w.shape[2]
    parts = []
    for l in range(DEPTH):
        bias = lax.dynamic_slice_in_dim(ada_b[l], chip * n_ada, n_ada)[None]
        parts.append(_mm(c_act16, ada_w[l], bias=bias, name="ada_fwd"))
    g_mod, = all_gather8([jnp.stack(parts)], "gather_mod")
    mod_mine = lax.dynamic_index_in_dim(g_mod[0::2], dev, 2, keepdims=False)
    mods = mod_mine.transpose(1, 0, 2).reshape(DEPTH, 3 * d)

    inv_freq = ROPE_THETA ** (-jnp.arange(0, MLA_ROPE, 2, dtype=F32) / MLA_ROPE)
    ang = positions[0].astype(F32)[:, None] * inv_freq
    cos, sin = jnp.cos(ang), jnp.sin(ang)
    rope_k = (jnp.concatenate([cos, cos], axis=1), jnp.concatenate([-sin, sin], axis=1))
    rope_q = (jnp.tile(rope_k[0], (1, MLA_HEADS)), jnp.tile(rope_k[1], (1, MLA_HEADS)))

    def run_layer(xh, mod, gathered, small):
        big = {nme: jnp.concatenate([g[0], g[1]], axis=HALF_AXIS[t] + 1) for t, (nme, g) in enumerate(zip(BIG, gathered))}
        return _layer(xh, mod, big, small, rope_q, rope_k)

    def head(hh, fg):
        return loss_op(rmsnorm(hh, fg[None]), loss_target[0])[0, 0]

    gathered0, done0 = finish_gather(started0, mods, "0")
    halves1 = my_halves(1, done0[0, 0].astype(jnp.int32))
    started1 = ici_start(halves1, landing(halves1), "gather", "gather_start1")
    h1, vjp0 = jax.vjp(run_layer, x[0], mods[0] + started1["token"][0, 0], gathered0, smalls[0])
    gathered1, _ = finish_gather(started1, h1, "1")
    h2, vjp1 = jax.vjp(run_layer, h1, mods[1], gathered1, smalls[1])
    loss_dev, vjp_head = jax.vjp(head, h2, final_g)
    dh2, dfinal = vjp_head(jnp.ones((), F32))

    c_idx = jnp.reshape(ci, (1,)).astype(jnp.int32)
    chip_idx = jnp.reshape(chip, (1,)).astype(jnp.int32)

    def reduce_begin(dgath, tag, zero=None):
        theirs = sibling_send(dgath, "reduce_sibling" + tag)
        pair = [_add_pair(a, b, c_idx) for a, b in zip(dgath, theirs)]
        shapes = [(N_CHIPS - 1,) + p.shape[1:] for p in pair]
        if zero is None:
            lands = [lax.empty(shp, BF16) for shp in shapes]
        else:
            lands = [jnp.broadcast_to(zero.astype(BF16), shp) for shp in shapes]
        return ici_start(pair, lands, "reduce", "reduce_start" + tag)

    def reduce_end(handle, after, tag):
        pair, landed = ici_wait(handle, after, "reduce", "reduce_wait" + tag)
        reduced = [_add_chips(p, q, chip_idx) for p, q in zip(pair, landed)]
        others = sibling_swap(reduced, "share_sibling" + tag)
        return [jnp.where(ci == 0, jnp.concatenate([own, other], axis=HALF_AXIS[t]),
                          jnp.concatenate([other, own], axis=HALF_AXIS[t]))
                for t, (own, other) in enumerate(zip(reduced, others))]

    dh1, dmod1, dgath1, dsmall1 = vjp1(dh2)
    reducing1 = reduce_begin(dgath1, "1")
    dx, dmod0, dgath0, dsmall0 = vjp0(dh1 + reducing1["token"][0, 0])
    dmods = jnp.stack([dmod0, dmod1])
    dsmalls = [dsmall0, dsmall1]

    pieces = [dmods.reshape(-1), dfinal]
    for nme in SMALL_REPL + SMALL_SHARDED:
        pieces.append(jnp.stack([dsmalls[l][nme] for l in range(DEPTH)]).reshape(-1))
    pieces.append(loss_dev.reshape(1))
    sizes = [p.shape[0] for p in pieces]
    flat = jnp.concatenate(pieces)
    padn = (-flat.shape[0]) % 128
    flat = jnp.concatenate([flat, jnp.zeros((padn,), F32)])[None]
    g_small, = all_gather8([flat], "gather_small_grads")
    total, small_done = _sum_devices(g_small)
    total = total[0]
    reducing0 = reduce_begin(dgath0, "0", small_done[0, 0])
    offs, at = [], 0
    for n_el in sizes:
        offs.append(at)
        at += n_el

    def piece(i, shape):
        return total[offs[i]:offs[i] + sizes[i]].reshape(shape)

    grads = {"ada_b": piece(0, (DEPTH, 3 * d)), "final_g": piece(1, (d,))}
    loss = piece(len(pieces) - 1, ())
    for i, nme in enumerate(SMALL_REPL + SMALL_SHARDED):
        full = piece(2 + i, small_full[nme].shape)
        if nme in SMALL_SHARDED:
            ncol = weights[nme].shape[2]
            full = lax.dynamic_slice_in_dim(full, chip * ncol, ncol, axis=2)
        grads[nme] = full

    dmod_all = g_small[:, 0, :DEPTH * 3 * d].reshape(N_DEV, DEPTH, 3 * d)
    dmod_cols = lax.dynamic_slice_in_dim(dmod_all, chip * n_ada, n_ada, axis=2)
    g_ada = []
    for l in range(DEPTH):
        dm16 = jnp.concatenate([dmod_cols[:, l], jnp.zeros((N_DEV, n_ada), F32)], axis=0)
        dm16 = dm16 + reducing0["token"][0, 0]
        g_ada.append(_mm(c_act16, dm16, ta=True, name="ada_bwd"))
    grads["ada_w"] = jnp.stack(g_ada)

    order = list(weights)
    big_names = ("w_in", "w_out", "mla_w_uq", "mla_w_ukv")
    delta, new_m, new_v = {}, {}, {}
    for nme in order:
        if nme not in big_names:
            delta[nme], new_m[nme], new_v[nme] = _adamw(weights[nme], grads[nme], m_in[nme], v_in[nme])

    def first(t):
        return t[(slice(0, 1),) * t.ndim].reshape(1)

    big_grads1 = reduce_end(reducing1, jnp.concatenate([first(dx), first(reducing0["token"])]), "1")
    done = [first(delta[nme]) for nme in order if nme not in big_names] + [first(g) for g in big_grads1]
    big_grads0 = reduce_end(reducing0, jnp.concatenate(done), "0")
    for nme, g0, g1 in zip(big_names, big_grads0, big_grads1):
        grads[nme] = jnp.stack([g0, g1])
    for nme in big_names:
        if nme == "w_in":
            w_t, m_t, v_t = (jnp.swapaxes(t, 1, 2) for t in (w_in, m_w_in, v_w_in))
            res = _adamw(w_t, grads[nme], m_t, v_t)
            delta[nme], new_m[nme], new_v[nme] = (jnp.swapaxes(t, 1, 2) for t in res)
            grads[nme] = jnp.swapaxes(grads[nme], 1, 2)
            continue
        delta[nme], new_m[nme], new_v[nme] = _adamw(weights[nme], grads[nme], m_in[nme], v_in[nme])
    return (loss, dx[None], *[grads[n_] for n_ in order], *[delta[n_] for n_ in order],
            *[new_m[n_] for n_ in order], *[new_v[n_] for n_ in order])
```

```python
import functools

import jax
import jax.numpy as jnp
from jax import lax
from jax.experimental import pallas as pl
from jax.experimental.pallas import tpu as pltpu

F32 = jnp.float32
BF16 = jnp.bfloat16
MESH = pl.DeviceIdType.MESH
HIGHEST = lax.Precision.HIGHEST

DEPTH = 2
D_MODEL = 2048
GLA_HEADS = 6
GLA_DK = 64
GLA_DV = 128
GLA_RANK = 16
GLA_TEMP = 16.0
GLA_CHUNK = 64
GLA_W = GLA_HEADS * GLA_DV
MLA_HEADS = 6
MLA_QL = 384
MLA_KVL = 256
MLA_NOPE = 128
MLA_ROPE = 64
MLA_DV = 128
MLA_W = MLA_HEADS * MLA_DV
CONV_CH = D_MODEL - GLA_W - MLA_W
ROPE_THETA = 10000.0
EPS = 1e-6
IN_DIM = 5856
N_CHIPS = 4
N_DEV = 8

ADAM_LR = 0.001
ADAM_B1 = 0.9
ADAM_B2 = 0.999
ADAM_EPS = 1e-08
ADAM_WD = 0.01
ADAM_STEP = 10

PROJ_SEGS = (
    ("gq", 0, 384, 384), ("gk", 384, 384, 384), ("gv", 768, 768, 768), ("glr", 1536, 32, 128),
    ("mq", 1568, 384, 384), ("mkv", 1952, 256, 256), ("mkr", 2208, 64, 128),
    ("cb", 2272, 512, 512), ("cc", 2784, 512, 512), ("cx", 3296, 512, 512),
    ("pad", 3808, 0, 128), ("z", 3808, 2048, 2048),
)
PROJ_AL = sum(s[3] for s in PROJ_SEGS)

ANY = pl.BlockSpec(memory_space=pl.ANY)
VMEM_LIMIT = 48 * 1024 * 1024
BLOCK_BYTES = 2 * 1024 * 1024


def _params(sem=None):
    return pltpu.CompilerParams(dimension_semantics=sem, vmem_limit_bytes=VMEM_LIMIT)


def _dot(a, b, ca, cb, precision=None):
    return lax.dot_general(a, b, (((ca,), (cb,)), ((), ())), preferred_element_type=F32, precision=precision)


def _tile(dim, prefs):
    for t in prefs:
        if dim % t == 0:
            return t
    return dim


def _pick_rows(rows, width, itemsize=4):
    for t in (2048, 1024, 512, 256, 128, 64, 32, 16, 8):
        if rows % t == 0 and t * width * itemsize <= BLOCK_BYTES:
            return t
    return rows


def _mm(a, b, *, ta=False, tb=False, bias=None, out_dtype=F32, name="mm"):
    if ta:
        K, M = a.shape
    else:
        M, K = a.shape
    if tb:
        N, Kb = b.shape
    else:
        Kb, N = b.shape
    assert K == Kb, (a.shape, b.shape, ta, tb)
    tm = _tile(M, (2048, 1024, 512, 256, 128))
    tn = _tile(N, (512, 384, 256, 128) if tm >= 2048 else (1024, 512, 384, 256, 128))
    tk = _tile(K, (2048, 1024, 512, 256, 128))
    nk = K // tk
    has_bias = bias is not None

    def body(*refs):
        a_ref, b_ref = refs[0], refs[1]
        bias_ref = refs[2] if has_bias else None
        o_ref = refs[3 if has_bias else 2]
        part = _dot(a_ref[...].astype(BF16), b_ref[...].astype(BF16), 0 if ta else 1, 1 if tb else 0)

        def finish(r):
            if has_bias:
                r = r + bias_ref[...]
            o_ref[...] = r.astype(out_dtype)

        if nk == 1:
            finish(part)
            return
        acc_ref = refs[-1]
        k = pl.program_id(2)

        @pl.when(k == 0)
        def _():
            acc_ref[...] = part

        @pl.when(k != 0)
        def _():
            acc_ref[...] += part

        @pl.when(k == nk - 1)
        def _():
            finish(acc_ref[...])

    a_spec = pl.BlockSpec((tk, tm), lambda i, j, k: (k, i)) if ta else pl.BlockSpec((tm, tk), lambda i, j, k: (i, k))
    b_spec = pl.BlockSpec((tn, tk), lambda i, j, k: (j, k)) if tb else pl.BlockSpec((tk, tn), lambda i, j, k: (k, j))
    in_specs = [a_spec, b_spec]
    args = [a, b]
    if has_bias:
        in_specs.append(pl.BlockSpec((1, tn), lambda i, j, k: (0, j)))
        args.append(bias)
    return pl.pallas_call(
        body, name=name, grid=(M // tm, N // tn, nk),
        in_specs=in_specs, out_specs=pl.BlockSpec((tm, tn), lambda i, j, k: (i, j)),
        out_shape=jax.ShapeDtypeStruct((M, N), out_dtype),
        scratch_shapes=[pltpu.VMEM((tm, tn), F32)] if nk > 1 else [],
        compiler_params=_params(("parallel", "parallel", "arbitrary")),
    )(*args)


@jax.custom_vjp
def mm(a, b):
    return _mm(a, b, name="mm_fwd")


def _mm_f(a, b):
    return _mm(a, b, name="mm_fwd"), (a, b)


def _mm_b(res, g):
    a, b = res
    return _mm(g, b, tb=True, out_dtype=a.dtype, name="mm_da"), _mm(a, g, ta=True, out_dtype=b.dtype, name="mm_db")


mm.defvjp(_mm_f, _mm_b)


@jax.custom_vjp
def mm16(a, b):
    return _mm(a, b, out_dtype=BF16, name="mm16_fwd")


def _mm16_f(a, b):
    return mm16(a, b), (a, b)


mm16.defvjp(_mm16_f, _mm_b)


def _rows(body, name, tiled, full, tiled_out, acc_out, tr=None):
    rows = tiled[0].shape[0]
    if tr is None:
        width = max([a.shape[1] for a in tiled] + [s.shape[1] for s in tiled_out])
        tr = _pick_rows(rows, width)
    in_specs = [pl.BlockSpec((tr, a.shape[1]), lambda i: (i, 0)) for a in tiled]
    in_specs += [pl.BlockSpec(a.shape, lambda i: (0, 0)) for a in full]
    out_specs = [pl.BlockSpec((tr, s.shape[1]), lambda i: (i, 0)) for s in tiled_out]
    out_specs += [pl.BlockSpec(s.shape, lambda i: (0, 0)) for s in acc_out]

    def wrapped(*refs):
        body(pl.program_id(0), *refs)

    outs = pl.pallas_call(
        wrapped, name=name, grid=(rows // tr,), in_specs=in_specs, out_specs=out_specs,
        out_shape=list(tiled_out) + list(acc_out),
        compiler_params=_params(("arbitrary",)),
    )(*tiled, *full)
    return outs


def _sds(shape, dtype=F32):
    return jax.ShapeDtypeStruct(tuple(shape), dtype)


def _acc(step, ref, val):
    @pl.when(step == 0)
    def _():
        ref[...] = val

    @pl.when(step != 0)
    def _():
        ref[...] += val


def _colsum(v):
    return jnp.sum(v, axis=0, keepdims=True)


def _rstd(x):
    return lax.rsqrt(jnp.mean(x * x, axis=-1, keepdims=True) + EPS)


def _norm_grid(x, g):
    rows, w = x.shape[0], g.shape[1]
    tr = _pick_rows(rows, w)
    blk = pl.BlockSpec((tr, w), lambda i, j: (i, j))
    gblk = pl.BlockSpec((1, w), lambda i, j: (0, 0))
    return (rows // tr, x.shape[1] // w), blk, gblk


@jax.custom_vjp
def rmsnorm(x, g):
    def body(x_ref, g_ref, o_ref):
        x = x_ref[...]
        o_ref[...] = x * _rstd(x) * g_ref[...]

    grid, blk, gblk = _norm_grid(x, g)
    return pl.pallas_call(body, name="rmsnorm_fwd", grid=grid, in_specs=[blk, gblk], out_specs=blk,
                          out_shape=_sds(x.shape), compiler_params=_params(("parallel", "parallel")))(x, g)


def _rmsnorm_f(x, g):
    return rmsnorm(x, g), (x, g)


def _rmsnorm_b(res, dy):
    x, g = res

    def body(x_ref, dy_ref, g_ref, dx_ref, dg_ref):
        x = x_ref[...]
        dy = dy_ref[...]
        r = _rstd(x)
        xh = x * r
        dxh = dy * g_ref[...]
        dx_ref[...] = r * (dxh - xh * jnp.mean(dxh * xh, axis=-1, keepdims=True))
        first = jnp.logical_and(pl.program_id(0) == 0, pl.program_id(1) == 0)
        _acc(jnp.where(first, 0, 1), dg_ref, _colsum(dy * xh))

    grid, blk, gblk = _norm_grid(x, g)
    dx, dg = pl.pallas_call(body, name="rmsnorm_bwd", grid=grid, in_specs=[blk, blk, gblk], out_specs=[blk, gblk],
                            out_shape=[_sds(x.shape), _sds(g.shape)],
                            compiler_params=_params(("arbitrary", "arbitrary")))(x, dy, g)
    return dx, dg


rmsnorm.defvjp(_rmsnorm_f, _rmsnorm_b)


def _modulate(x, g, scale, shift):
    def body(i, x_ref, g_ref, sc_ref, sh_ref, o_ref):
        x = x_ref[...]
        xn = x * _rstd(x) * g_ref[...]
        o_ref[...] = (xn * (1.0 + sc_ref[...]) + sh_ref[...]).astype(BF16)
    return _rows(body, "modulate_fwd", [x], [g, scale, shift], [_sds(x.shape, BF16)], [])[0]


def _modulate_bwd(x, g, scale, shift, dh):
    def body(i, x_ref, dh_ref, g_ref, sc_ref, dx_ref, dg_ref, dsc_ref, dsh_ref):
        x = x_ref[...]
        dh = dh_ref[...]
        gv = g_ref[...]
        r = _rstd(x)
        xh = x * r
        dxn = dh * (1.0 + sc_ref[...])
        dxh = dxn * gv
        dx_ref[...] = r * (dxh - xh * jnp.mean(dxh * xh, axis=-1, keepdims=True))
        _acc(i, dg_ref, _colsum(dxn * xh))
        _acc(i, dsc_ref, _colsum(dh * (xh * gv)))
        _acc(i, dsh_ref, _colsum(dh))

    v = _sds(g.shape)
    return _rows(body, "modulate_bwd", [x, dh], [g, scale], [_sds(x.shape)], [v, v, v])


@jax.custom_vjp
def mod_mm(x, g, scale, shift, wt):
    return _mm(_modulate(x, g, scale, shift), wt, tb=True, name="mm_in")


def _mod_mm_f(x, g, scale, shift, wt):
    h = _modulate(x, g, scale, shift)
    return _mm(h, wt, tb=True, name="mm_in"), (x, g, scale, shift, wt, h)


def _mod_mm_b(res, dproj):
    x, g, scale, shift, wt, h = res
    dproj = dproj.astype(BF16)
    dh = _mm(dproj, wt, name="mm_in_dh")
    dwt = _mm(dproj, h, ta=True, out_dtype=wt.dtype, name="mm_in_dw")
    dx, dg, dsc, dsh = _modulate_bwd(x, g, scale, shift, dh)
    return dx, dg, dsc, dsh, dwt


mod_mm.defvjp(_mod_mm_f, _mod_mm_b)


def _sigmoid(z):
    return 1.0 / (1.0 + jnp.exp(-z))


def _gate_mul(parts, z):
    n = len(parts)

    def body(i, *refs):
        z = refs[n][...]
        o = jnp.concatenate([r[...] for r in refs[:n]], axis=1)
        refs[-1][...] = (o * (z * _sigmoid(z))).astype(BF16)

    return _rows(body, "gate_mul_fwd", list(parts) + [z], [], [_sds(z.shape, BF16)], [])[0]


def _gate_mul_bwd(parts, z, dy):
    n = len(parts)

    def body(i, *refs):
        z, dy = refs[n][...], refs[n + 1][...]
        s = _sigmoid(z)
        o = jnp.concatenate([r[...] for r in refs[:n]], axis=1)
        do = dy * (z * s)
        at = 0
        for do_ref in refs[n + 2:2 * n + 2]:
            do_ref[...] = do[:, at:at + do_ref.shape[1]]
            at += do_ref.shape[1]
        refs[-1][...] = dy * o * (s * (1.0 + z * (1.0 - s)))

    outs = _rows(body, "gate_mul_bwd", list(parts) + [z, dy], [], [_sds(p.shape) for p in parts] + [_sds(z.shape)], [])
    return tuple(outs[:-1]), outs[-1]


def _residual(x, u, gate):
    def body(i, x_ref, u_ref, g_ref, o_ref):
        o_ref[...] = x_ref[...] + g_ref[...] * u_ref[...]
    return _rows(body, "residual_fwd", [x, u], [gate], [_sds(x.shape)], [])[0]


def _residual_bwd(d, u, gate):
    def body(i, d_ref, u_ref, g_ref, du_ref, dg_ref):
        d = d_ref[...]
        du_ref[...] = (g_ref[...] * d).astype(BF16)
        _acc(i, dg_ref, _colsum(d * u_ref[...]))

    return _rows(body, "residual_bwd", [d, u], [gate], [_sds(u.shape, BF16)], [_sds(gate.shape)])


@jax.custom_vjp
def out_block(parts, z, w, x, gate):
    return _residual(x, _mm(_gate_mul(parts, z), w, name="mm_out"), gate)


def _out_block_f(parts, z, w, x, gate):
    y = _gate_mul(parts, z)
    u = _mm(y, w, name="mm_out")
    return _residual(x, u, gate), (parts, z, w, y, u, gate)


def _out_block_b(res, d):
    o, z, w, y, u, gate = res
    du, dgate = _residual_bwd(d, u, gate)
    dy = _mm(du, w, tb=True, name="mm_out_dy")
    dw = _mm(y, du, ta=True, out_dtype=w.dtype, name="mm_out_dw")
    do, dz = _gate_mul_bwd(o, z, dy)
    return do, dz, dw, d, dgate


out_block.defvjp(_out_block_f, _out_block_b)


@jax.custom_vjp
def gate_act(u, b):
    def body(i, u_ref, b_ref, o_ref):
        t = u_ref[...] + b_ref[...]
        o_ref[...] = (jnp.minimum(t, 0.0) - jnp.log(1.0 + jnp.exp(-jnp.abs(t)))) / GLA_TEMP
    return _rows(body, "gate_act_fwd", [u], [b], [_sds(u.shape)], [])[0]


def _gate_act_f(u, b):
    return gate_act(u, b), (u, b)


def _gate_act_b(res, d):
    u, b = res

    def body(i, u_ref, d_ref, b_ref, du_ref, db_ref):
        t = u_ref[...] + b_ref[...]
        du = d_ref[...] * _sigmoid(-t) / GLA_TEMP
        du_ref[...] = du
        _acc(i, db_ref, _colsum(du))

    du, db = _rows(body, "gate_act_bwd", [u, d], [b], [_sds(u.shape)], [_sds(b.shape)])
    return du, db


gate_act.defvjp(_gate_act_f, _gate_act_b)


@jax.custom_vjp
def fma(a, b, c, d):
    def body(i, a_ref, b_ref, c_ref, d_ref, o_ref):
        o_ref[...] = a_ref[...] * b_ref[...] + c_ref[...] * d_ref[...]
    return _rows(body, "fma_fwd", [a, b, c, d], [], [_sds(a.shape)], [])[0]


def _fma_f(a, b, c, d):
    return fma(a, b, c, d), (b, d)


def _fma_b(res, g):
    b, d = res

    def body(i, g_ref, b_ref, d_ref, da_ref, dc_ref):
        g = g_ref[...]
        da_ref[...] = g * b_ref[...]
        dc_ref[...] = g * d_ref[...]

    da, dc = _rows(body, "fma_bwd", [g, b, d], [], [_sds(g.shape), _sds(g.shape)], [])
    return da, jnp.zeros_like(b), dc, jnp.zeros_like(d)


fma.defvjp(_fma_f, _fma_b)


def _silu_rows(c):
    def body(i, c_ref, o_ref):
        v = c_ref[...]
        o_ref[...] = v * _sigmoid(v)
    return _rows(body, "silu", [c], [], [_sds(c.shape)], [])[0]


@jax.custom_vjp
def loss_op(y, t):
    return _loss_fwd(y, t)[0]


def _loss_fwd(y, t):
    inv = 1.0 / y.shape[1]

    def body(i, y_ref, t_ref, d_ref, l_ref):
        e = y_ref[...] - t_ref[...]
        d_ref[...] = e * inv
        _acc(i, l_ref, jnp.sum(_colsum(e * e), axis=1, keepdims=True) * (0.5 * inv))

    d, l = _rows(body, "loss_fwd", [y, t], [], [_sds(y.shape)], [_sds((1, 1))])
    return l, d


def _loss_f(y, t):
    l, d = _loss_fwd(y, t)
    return l, d


def _loss_b(d, g):
    return d * g, jnp.zeros_like(d)


loss_op.defvjp(_loss_f, _loss_b)


def _conv_terms(cc, cx, rows, n):
    u = cc * cx
    up = jnp.where(rows == 0, 0.0, pltpu.roll(u, 1, 0))
    un = jnp.where(rows == n - 1, 0.0, pltpu.roll(u, n - 1, 0))
    return u, up, un


CONV_COLS = 128


def _conv_specs(s, n_in):
    blk = pl.BlockSpec((s, CONV_COLS), lambda j: (0, j))
    wblk = pl.BlockSpec((8, CONV_COLS), lambda j: (0, j))
    return [blk] * n_in + [wblk], blk, wblk


@jax.custom_vjp
def conv_op(cb, cc, cx, w):
    s, ch = cb.shape

    def body(cb_ref, cc_ref, cx_ref, w_ref, o_ref):
        rows = lax.broadcasted_iota(jnp.int32, (s, CONV_COLS), 0)
        u, up, un = _conv_terms(cc_ref[...], cx_ref[...], rows, s)
        conv = up * w_ref[0:1, :] + u * w_ref[1:2, :] + un * w_ref[2:3, :]
        o_ref[...] = cb_ref[...] * conv

    in_specs, blk, _ = _conv_specs(s, 3)
    return pl.pallas_call(
        body, name="conv_fwd", grid=(ch // CONV_COLS,), in_specs=in_specs, out_specs=blk,
        out_shape=_sds(cb.shape), compiler_params=_params(("parallel",)),
    )(cb, cc, cx, w)


def _conv_f(cb, cc, cx, w):
    return conv_op(cb, cc, cx, w), (cb, cc, cx, w)


def _conv_b(res, d):
    cb, cc, cx, w = res
    s, ch = cb.shape

    def body(cb_ref, cc_ref, cx_ref, d_ref, w_ref, dcb_ref, dcc_ref, dcx_ref, dw_ref):
        rows = lax.broadcasted_iota(jnp.int32, (s, CONV_COLS), 0)
        cc_v = cc_ref[...]
        cx_v = cx_ref[...]
        u, up, un = _conv_terms(cc_v, cx_v, rows, s)
        w0, w1, w2 = w_ref[0:1, :], w_ref[1:2, :], w_ref[2:3, :]
        dv = d_ref[...]
        dcb_ref[...] = dv * (up * w0 + u * w1 + un * w2)
        dconv = dv * cb_ref[...]
        d_next = jnp.where(rows == s - 1, 0.0, pltpu.roll(dconv, s - 1, 0))
        d_prev = jnp.where(rows == 0, 0.0, pltpu.roll(dconv, 1, 0))
        du = w0 * d_next + w1 * dconv + w2 * d_prev
        dcc_ref[...] = du * cx_v
        dcx_ref[...] = du * cc_v
        dw_ref[...] = jnp.zeros_like(dw_ref)
        dw_ref[0:1, :] = _colsum(dconv * up)
        dw_ref[1:2, :] = _colsum(dconv * u)
        dw_ref[2:3, :] = _colsum(dconv * un)

    in_specs, blk, wblk = _conv_specs(s, 4)
    v = _sds(cb.shape)
    return tuple(pl.pallas_call(
        body, name="conv_bwd", grid=(ch // CONV_COLS,), in_specs=in_specs, out_specs=[blk, blk, blk, wblk],
        out_shape=[v, v, v, _sds(w.shape)], compiler_params=_params(("parallel",)),
    )(cb, cc, cx, d, w))


conv_op.defvjp(_conv_f, _conv_b)


def _gla_masks(rev):
    c = GLA_CHUNK
    row = lax.broadcasted_iota(jnp.int32, (c, c), 0)
    col = lax.broadcasted_iota(jnp.int32, (c, c), 1)
    mask = (row < col) if rev else (row >= col)
    return rev, mask


def _chunk_cumsum(g, rev):
    c = g.shape[0]
    row = lax.broadcasted_iota(jnp.int32, g.shape, 0)
    b = g
    s = 1
    while s < c:
        if rev:
            b = b + jnp.where(row < c - s, pltpu.roll(b, c - s, 0), 0.0)
        else:
            b = b + jnp.where(row >= s, pltpu.roll(b, s, 0), 0.0)
        s *= 2
    return b


GLA_UNROLL = 4


def _gla_rows(n):
    return pl.ds(pl.multiple_of(n * GLA_CHUNK, GLA_CHUNK), GLA_CHUNK)


def _gla_scan(s_ref, bt_ref, st_ref, n_chunks, descending):
    st_ref[...] = jnp.zeros_like(st_ref)

    def step(i, carry):
        n = (n_chunks - 1 - i) if descending else i
        own = s_ref[n]
        st = st_ref[...]
        s_ref[n] = st
        st_ref[...] = st * jnp.exp(bt_ref[n]) + own
        return carry

    lax.fori_loop(0, n_chunks, step, 0)


GLA_PAIR = 2


def _gla_specs(s):
    dk, dv = GLA_DK, GLA_DV
    n_pairs = GLA_HEADS // GLA_PAIR
    blk_k = pl.BlockSpec((s, GLA_PAIR * dk), lambda p: (0, p))
    blk_gb = pl.BlockSpec((s, GLA_PAIR * dk), lambda p: (0, n_pairs + p))
    blk_v = pl.BlockSpec((s, GLA_PAIR * dv), lambda p: (0, p))
    return n_pairs, blk_k, blk_gb, blk_v


def _head_lanes(hh):
    lane = lax.broadcasted_iota(jnp.int32, (1, GLA_PAIR * GLA_DK), 1)
    return jnp.logical_and(lane >= hh * GLA_DK, lane < (hh + 1) * GLA_DK)


def _gla_fwd(q, k, v, la):
    s = q.shape[0]
    dk, dv = GLA_DK, GLA_DV
    pw = GLA_PAIR * dk
    n_chunks = s // GLA_CHUNK
    scale = GLA_DK ** -0.5

    def body(q_ref, k_ref, v_ref, gf_ref, gb_ref, o_ref, sf_ref, sb_ref, bf_ref, bb_ref, btf_ref, btb_ref, st_ref):
        masks = [_gla_masks(rev) for rev in (False, True)]
        dirs = ((False, gf_ref, sf_ref, bf_ref, btf_ref), (True, gb_ref, sb_ref, bb_ref, btb_ref))

        def decays(n, carry):
            rows = _gla_rows(n)
            for rev, g_ref, _, b_ref, bt_ref in dirs:
                g = g_ref[rows, :]
                b_ref[rows, :] = _chunk_cumsum(g, rev)
                bt_ref[n] = _colsum(g)
            return carry

        lax.fori_loop(0, n_chunks, decays, 0, unroll=GLA_UNROLL)
        for hh in range(GLA_PAIR):
            m = _head_lanes(hh)
            vl = slice(hh * dv, (hh + 1) * dv)

            def prepare(n, carry, m=m, vl=vl):
                rows = _gla_rows(n)
                kk = k_ref[rows, :]
                vb = v_ref[rows, vl].astype(BF16)
                for rev, _, s_ref, b_ref, bt_ref in dirs:
                    ke = jnp.where(m, kk * jnp.exp(bt_ref[n] - b_ref[rows, :]), 0.0).astype(BF16)
                    s_ref[n] = _dot(vb, ke, 0, 0)
                return carry

            lax.fori_loop(0, n_chunks, prepare, 0, unroll=GLA_UNROLL)
            for rev, _, s_ref, _, bt_ref in dirs:
                _gla_scan(s_ref, bt_ref, st_ref, n_chunks, descending=rev)

            def emit(n, carry, m=m, vl=vl):
                rows = _gla_rows(n)
                qs = q_ref[rows, :] * scale
                kk = k_ref[rows, :]
                vb = v_ref[rows, vl].astype(BF16)
                o = None
                for (rev, _, s_ref, b_ref, _), (_, mask) in zip(dirs, masks):
                    b = b_ref[rows, :]
                    qd = jnp.where(m, qs * jnp.exp(b), 0.0).astype(BF16)
                    ki = jnp.where(m, kk * jnp.exp(-b), 0.0).astype(BF16)
                    a = jnp.where(mask, _dot(qd, ki, 1, 1), 0.0).astype(BF16)
                    od = _dot(a, vb, 1, 0) + _dot(qd, s_ref[n].astype(BF16), 1, 1)
                    o = od if o is None else o + od
                o_ref[rows, vl] = o
                return carry

            lax.fori_loop(0, n_chunks, emit, 0, unroll=GLA_UNROLL)

    n_pairs, blk_k, blk_gb, blk_v = _gla_specs(s)
    state = pltpu.VMEM((n_chunks, dv, pw), F32)
    scratch = [state, state, pltpu.VMEM((s, pw), F32), pltpu.VMEM((s, pw), F32), pltpu.VMEM((n_chunks, 1, pw), F32),
               pltpu.VMEM((n_chunks, 1, pw), F32), pltpu.VMEM((dv, pw), F32)]
    return pl.pallas_call(
        body, name="gla_fwd", grid=(n_pairs,), in_specs=[blk_k, blk_k, blk_v, blk_k, blk_gb],
        out_specs=blk_v, out_shape=_sds(v.shape), scratch_shapes=scratch,
        compiler_params=_params(("parallel",)),
    )(q, k, v, la, la)


def _gla_bwd(q, k, v, la, do):
    s = q.shape[0]
    dk, dv = GLA_DK, GLA_DV
    pw = GLA_PAIR * dk
    c = GLA_CHUNK
    n_chunks = s // c
    scale = GLA_DK ** -0.5

    def body(q_ref, k_ref, v_ref, gf_ref, gb_ref, do_ref, dq_ref, dk_ref, dv_ref, dgf_ref, dgb_ref,
             sf_ref, sb_ref, bf_ref, bb_ref, btf_ref, btb_ref, dsf_ref, dsb_ref, st_ref):
        masks = [_gla_masks(rev) for rev in (False, True)]
        rowc = lax.broadcasted_iota(jnp.int32, (c, pw), 0)
        dirs = ((False, gf_ref, sf_ref, bf_ref, btf_ref, dsf_ref, dgf_ref),
                (True, gb_ref, sb_ref, bb_ref, btb_ref, dsb_ref, dgb_ref))

        def decays(n, carry):
            rows = _gla_rows(n)
            for rev, g_ref, _, b_ref, bt_ref, _, _ in dirs:
                g = g_ref[rows, :]
                b_ref[rows, :] = _chunk_cumsum(g, rev)
                bt_ref[n] = _colsum(g)
            return carry

        lax.fori_loop(0, n_chunks, decays, 0, unroll=GLA_UNROLL)
        for hh in range(GLA_PAIR):
            m = _head_lanes(hh)
            vl = slice(hh * dv, (hh + 1) * dv)

            def prepare(n, carry, m=m, vl=vl):
                rows = _gla_rows(n)
                qs = q_ref[rows, :] * scale
                kk = k_ref[rows, :]
                vb = v_ref[rows, vl].astype(BF16)
                do_b = do_ref[rows, vl].astype(BF16)
                for rev, _, s_ref, b_ref, bt_ref, ds_ref, _ in dirs:
                    b = b_ref[rows, :]
                    ke = jnp.where(m, kk * jnp.exp(bt_ref[n] - b), 0.0).astype(BF16)
                    qd = jnp.where(m, qs * jnp.exp(b), 0.0).astype(BF16)
                    s_ref[n] = _dot(vb, ke, 0, 0)
                    ds_ref[n] = _dot(do_b, qd, 0, 0)
                return carry

            lax.fori_loop(0, n_chunks, prepare, 0, unroll=GLA_UNROLL)
            for rev, _, s_ref, _, bt_ref, ds_ref, _ in dirs:
                _gla_scan(s_ref, bt_ref, st_ref, n_chunks, descending=rev)
                _gla_scan(ds_ref, bt_ref, st_ref, n_chunks, descending=not rev)

            def emit(n, carry, m=m, vl=vl, first=(hh == 0)):
                rows = _gla_rows(n)
                qs = q_ref[rows, :] * scale
                kk = k_ref[rows, :]
                vb = v_ref[rows, vl].astype(BF16)
                do_b = do_ref[rows, vl].astype(BF16)
                dq = dkk = dvv = None
                for (rev, _, s_ref, b_ref, bt_ref, ds_ref, dg_ref), (_, mask) in zip(dirs, masks):
                    b = b_ref[rows, :]
                    bt = bt_ref[n]
                    eb = jnp.where(m, jnp.exp(b), 0.0)
                    enb = jnp.where(m, jnp.exp(-b), 0.0)
                    etb = jnp.where(m, jnp.exp(bt - b), 0.0)
                    ebt = jnp.exp(bt)
                    qd = qs * eb
                    ki = kk * enb
                    ke = kk * etb
                    qd_b, ki_b, ke_b = qd.astype(BF16), ki.astype(BF16), ke.astype(BF16)
                    st = s_ref[n]
                    dst = ds_ref[n]
                    dst_b = dst.astype(BF16)
                    a = jnp.where(mask, _dot(qd_b, ki_b, 1, 1), 0.0).astype(BF16)
                    da = jnp.where(mask, _dot(do_b, vb, 1, 1), 0.0).astype(BF16)
                    dv_d = _dot(a, do_b, 0, 0) + _dot(ke_b, dst_b, 1, 1)
                    dqd = _dot(da, ki_b, 1, 0) + _dot(do_b, st.astype(BF16), 1, 0)
                    dki = _dot(da, qd_b, 0, 0)
                    dke = _dot(vb, dst_b, 1, 0)
                    dbt = _colsum(st * dst) * ebt + _colsum(dke * ke)
                    db = dqd * qd - dki * ki - dke * ke
                    db = db + jnp.where(rowc == (0 if rev else c - 1), dbt, 0.0)
                    dg = _chunk_cumsum(db, not rev)
                    if first:
                        dg_ref[rows, :] = dg
                    else:
                        dg_ref[rows, :] += dg
                    dq_d = dqd * eb * scale
                    dk_d = dki * enb + dke * etb
                    dq = dq_d if dq is None else dq + dq_d
                    dkk = dk_d if dkk is None else dkk + dk_d
                    dvv = dv_d if dvv is None else dvv + dv_d
                if first:
                    dq_ref[rows, :] = dq
                    dk_ref[rows, :] = dkk
                else:
                    dq_ref[rows, :] += dq
                    dk_ref[rows, :] += dkk
                dv_ref[rows, vl] = dvv
                return carry

            lax.fori_loop(0, n_chunks, emit, 0, unroll=GLA_UNROLL)

    n_pairs, blk_k, blk_gb, blk_v = _gla_specs(s)
    vk, vv = _sds(q.shape), _sds(v.shape)
    state = pltpu.VMEM((n_chunks, dv, pw), F32)
    scratch = [state, state, pltpu.VMEM((s, pw), F32), pltpu.VMEM((s, pw), F32), pltpu.VMEM((n_chunks, 1, pw), F32),
               pltpu.VMEM((n_chunks, 1, pw), F32), state, state, pltpu.VMEM((dv, pw), F32)]
    return pl.pallas_call(
        body, name="gla_bwd", grid=(n_pairs,), in_specs=[blk_k, blk_k, blk_v, blk_k, blk_gb, blk_v],
        out_specs=[blk_k, blk_k, blk_v, blk_k, blk_k], out_shape=[vk, vk, vv, vk, vk],
        scratch_shapes=scratch, compiler_params=_params(("parallel",)),
    )(q, k, v, la, la, do)


@jax.custom_vjp
def gla(q, k, v, la):
    return _gla_fwd(q, k, v, la)


def _gla_f(q, k, v, la):
    return _gla_fwd(q, k, v, la), (q, k, v, la)


def _gla_b(res, do):
    dq, dk, dv, dgf, dgb = _gla_bwd(*res, do)
    return dq, dk, dv, jnp.concatenate([dgf, dgb], axis=1)


gla.defvjp(_gla_f, _gla_b)


ATTN_TQ = 256
ATTN_TQ_BWD = 512
HEAD_LANES = 128


def _attn_blocks(s, tq):
    per_q = pl.BlockSpec((tq, HEAD_LANES), lambda h, j: (j, h))
    k_nope = pl.BlockSpec((s, HEAD_LANES), lambda h, j: (0, 2 * h))
    v_blk = pl.BlockSpec((s, HEAD_LANES), lambda h, j: (0, 2 * h + 1))
    k_rope = pl.BlockSpec((s, HEAD_LANES), lambda h, j: (0, 0))
    lse = pl.BlockSpec((1, tq, 1), lambda h, j: (h, j, 0))
    return per_q, k_nope, v_blk, k_rope, lse


def _attn_fwd(qn, qr, kv, kr):
    s = qn.shape[0]
    tq = min(ATTN_TQ, s)
    scale = (MLA_NOPE + MLA_ROPE) ** -0.5

    def body(qn_ref, qr_ref, kn_ref, v_ref, kr_ref, o_ref, lse_ref):
        q = jnp.concatenate([qn_ref[...], qr_ref[...]], axis=1)
        k = jnp.concatenate([kn_ref[...], kr_ref[...]], axis=1)
        sc = _dot(q, k, 1, 1) * scale
        m = jnp.max(sc, axis=-1, keepdims=True)
        p = jnp.exp(sc - m)
        l = jnp.sum(p, axis=-1, keepdims=True)
        p = p * (1.0 / l)
        o_ref[...] = _dot(p.astype(BF16), v_ref[...], 1, 0)
        lse_ref[0] = m + jnp.log(l)

    per_q, k_nope, v_blk, k_rope, lse = _attn_blocks(s, tq)
    return pl.pallas_call(
        body, name="attn_fwd", grid=(MLA_HEADS, s // tq), in_specs=[per_q, per_q, k_nope, v_blk, k_rope],
        out_specs=[per_q, lse], out_shape=[_sds(qn.shape), _sds((MLA_HEADS, s, 1))],
        compiler_params=_params(("parallel", "parallel")),
    )(qn, qr, kv, kv, kr)


def _attn_bwd(qn, qr, kv, kr, o, lse, do):
    s = qn.shape[0]
    tq = min(ATTN_TQ_BWD, s)
    n_q = s // tq
    scale = (MLA_NOPE + MLA_ROPE) ** -0.5

    def body(qn_ref, qr_ref, kn_ref, v_ref, kr_ref, o_ref, lse_ref, do_ref, dqn_ref, dqr_ref, dkv_ref, dkr_ref,
             dk_acc, dv_acc, dkr_acc):
        h, j = pl.program_id(0), pl.program_id(1)
        q = jnp.concatenate([qn_ref[...], qr_ref[...]], axis=1)
        k = jnp.concatenate([kn_ref[...], kr_ref[...]], axis=1)
        do = do_ref[...]
        do_b = do.astype(BF16)
        p = jnp.exp(_dot(q, k, 1, 1) * scale - lse_ref[0])
        dp = _dot(do_b, v_ref[...], 1, 1)
        delta = jnp.sum(do * o_ref[...], axis=-1, keepdims=True)
        ds = (p * (dp - delta) * scale).astype(BF16)
        dq = _dot(ds, k, 1, 0)
        dqn_ref[...] = dq[:, :HEAD_LANES].astype(BF16)
        dqr_ref[...] = dq[:, HEAD_LANES:].astype(BF16)
        dk = _dot(ds, q, 0, 0)
        _acc(j, dk_acc, dk[:, :HEAD_LANES])
        _acc(j, dv_acc, _dot(p.astype(BF16), do_b, 0, 0))
        _acc(jnp.where(jnp.logical_and(h == 0, j == 0), 0, 1), dkr_acc, dk[:, HEAD_LANES:])

        @pl.when(j == n_q - 1)
        def _():
            dkv_ref[:, 0:HEAD_LANES] = dk_acc[...].astype(BF16)
            dkv_ref[:, HEAD_LANES:2 * HEAD_LANES] = dv_acc[...].astype(BF16)

        @pl.when(jnp.logical_and(h == MLA_HEADS - 1, j == n_q - 1))
        def _():
            dkr_ref[...] = dkr_acc[...].astype(BF16)

    per_q, k_nope, v_blk, k_rope, lse_blk = _attn_blocks(s, tq)
    dkv_blk = pl.BlockSpec((s, 2 * HEAD_LANES), lambda h, j: (0, h))
    acc = pltpu.VMEM((s, HEAD_LANES), F32)
    return pl.pallas_call(
        body, name="attn_bwd", grid=(MLA_HEADS, n_q),
        in_specs=[per_q, per_q, k_nope, v_blk, k_rope, per_q, lse_blk, per_q],
        out_specs=[per_q, per_q, dkv_blk, k_rope],
        out_shape=[_sds(qn.shape, BF16), _sds(qr.shape, BF16), _sds(kv.shape, BF16), _sds(kr.shape, BF16)],
        scratch_shapes=[acc, acc, acc], compiler_params=_params(("arbitrary", "arbitrary")),
    )(qn, qr, kv, kv, kr, o, lse, do)


@jax.custom_vjp
def attn(qn, qr, kv, kr):
    return _attn_fwd(qn, qr, kv, kr)[0]


def _attn_f(qn, qr, kv, kr):
    o, lse = _attn_fwd(qn, qr, kv, kr)
    return o, (qn, qr, kv, kr, o, lse)


def _attn_b(res, do):
    return tuple(_attn_bwd(*res, do))


attn.defvjp(_attn_f, _attn_b)


@jax.custom_vjp
def split_proj(proj):
    out, at = [], 0
    for _, _, _, wp in PROJ_SEGS:
        out.append(proj[:, at:at + wp])
        at += wp
    return tuple(out)


def _split_f(proj):
    return split_proj(proj), None


def _split_b(_, gs):
    return (jnp.concatenate(gs, axis=1),)


split_proj.defvjp(_split_f, _split_b)


def _tile2d(rows, width, limit=BLOCK_BYTES):
    fits = [t for t in range(16, rows + 1, 16) if rows % t == 0 and t * width * 4 <= limit]
    if fits and (fits[-1] >= 64 or fits[-1] == rows):
        return fits[-1], width
    if rows * width * 4 <= limit:
        return rows, width
    cols = [t for t in range(128, width + 1, 128) if width % t == 0 and rows * t * 4 <= limit]
    return (rows, cols[-1]) if cols else (rows, width)


def _add_pair(stacked, theirs, c_idx):
    g, r, w = theirs.shape
    tr, tc = _tile2d(r, w)

    def body(c_ref, a_ref, b_ref, o_ref):
        o_ref[0] = (a_ref[0, 0].astype(F32) + b_ref[0].astype(F32)).astype(BF16)

    blk = pl.BlockSpec((1, tr, tc), lambda k, i, j, c: (k, i, j))
    spec = pltpu.PrefetchScalarGridSpec(
        num_scalar_prefetch=1, grid=(g, r // tr, w // tc),
        in_specs=[pl.BlockSpec((1, 1, tr, tc), lambda k, i, j, c: (c[0], k, i, j)), blk], out_specs=blk)
    return pl.pallas_call(body, name="add_pair", grid_spec=spec, out_shape=_sds(theirs.shape, BF16),
                          compiler_params=_params(("parallel", "parallel", "parallel")))(c_idx, stacked, theirs)


def _add_chips(pair, landed, chip_idx):
    _, r, w = pair.shape
    tr, tc = _tile2d(r, w)

    def body(c_ref, p_ref, l0_ref, l1_ref, l2_ref, o_ref):
        o_ref[...] = ((p_ref[0].astype(F32) + l0_ref[0].astype(F32)) + l1_ref[0].astype(F32)) + l2_ref[0].astype(F32)

    specs = [pl.BlockSpec((1, tr, tc), lambda i, j, c: (c[0], i, j))]
    specs += [pl.BlockSpec((1, tr, tc), functools.partial(lambda i, j, c, k: (k, i, j), k=k)) for k in range(N_CHIPS - 1)]
    spec = pltpu.PrefetchScalarGridSpec(num_scalar_prefetch=1, grid=(r // tr, w // tc), in_specs=specs,
                                        out_specs=pl.BlockSpec((tr, tc), lambda i, j, c: (i, j)))
    return pl.pallas_call(body, name="add_chips", grid_spec=spec, out_shape=_sds((r, w)),
                          compiler_params=_params(("parallel", "parallel")))(chip_idx, pair, landed, landed, landed)


def _sum_devices(g):
    n = g.shape[2]

    def body(g_ref, o_ref, done_ref):
        t = g_ref[0]
        for j in range(1, N_DEV):
            t = t + g_ref[j]
        o_ref[...] = t
        done_ref[...] = jnp.zeros_like(done_ref)

    return pl.pallas_call(body, name="sum_devices", out_shape=[_sds((1, n)), _sds((8, 128))],
                          compiler_params=_params())(g)


def _adamw_math(w, gv, m, v):
    c1 = 1.0 - ADAM_B1 ** ADAM_STEP
    c2 = 1.0 - ADAM_B2 ** ADAM_STEP
    mn = ADAM_B1 * m + (1.0 - ADAM_B1) * gv
    vn = ADAM_B2 * v + (1.0 - ADAM_B2) * (gv * gv)
    return -ADAM_LR * ((mn / c1) / (jnp.sqrt(vn / c2) + ADAM_EPS) + ADAM_WD * w), mn, vn


def _adamw(w, g, m, v):
    shp = w.shape
    shp3 = (1, 1, shp[0]) if len(shp) == 1 else (-1,) + tuple(shp[-2:])
    w3, g3, m3, v3 = (t.reshape(shp3) for t in (w, g, m, v))

    def body(w_ref, g_ref, m_ref, v_ref, d_ref, mo_ref, vo_ref):
        d_ref[...], mo_ref[...], vo_ref[...] = _adamw_math(w_ref[...], g_ref[...], m_ref[...], v_ref[...])

    nl, r, wd = w3.shape
    tr, tc = _tile2d(r, wd, BLOCK_BYTES // 2)
    blk = pl.BlockSpec((1, tr, tc), lambda l, i, j: (l, i, j))
    s3 = _sds(w3.shape)
    d, mn, vn = pl.pallas_call(
        body, name="adamw", grid=(nl, r // tr, wd // tc), in_specs=[blk] * 4, out_specs=[blk] * 3,
        out_shape=[s3, s3, s3], compiler_params=_params(("parallel", "parallel", "parallel")),
    )(w3, g3, m3, v3)
    return d.reshape(shp), mn.reshape(shp), vn.reshape(shp)


PIECE_BYTES = 1 << 20


def _place():
    return lax.axis_index("x"), lax.axis_index("y"), lax.axis_index("c")


def _pieces(shape, itemsize):
    if len(shape) >= 3:
        return [(i,) + p for i in range(shape[0]) for p in _pieces(shape[1:], itemsize)]
    rows = shape[0]
    row_bytes = itemsize
    for dsz in shape[1:]:
        row_bytes *= dsz
    k = 1
    while rows % (2 * k) == 0 and (rows // (2 * k)) % 16 == 0 and (rows // k) * row_bytes > PIECE_BYTES:
        k *= 2
    step = rows // k
    return [(pl.ds(j * step, step),) for j in range(k)]


def _split_start(make, src, dst, pieces):
    for p in pieces:
        make(src.at[p], dst.at[p]).start()
    return make(src, dst)


def _comm_call(body, name, arrs, out_shapes, n_remote, n_local):
    return pl.pallas_call(
        body, name=name, in_specs=[ANY] * len(arrs), out_specs=[ANY] * len(out_shapes), out_shape=out_shapes,
        scratch_shapes=[pltpu.SemaphoreType.DMA((n_remote,)), pltpu.SemaphoreType.DMA((n_remote,)),
                        pltpu.SemaphoreType.DMA((n_local,))],
    )(*arrs)


def all_gather8(arrs, name):
    n = len(arrs)
    pieces = [_pieces(a.shape, a.dtype.itemsize) for a in arrs]

    def body(*refs):
        ins, outs = refs[:n], refs[n:2 * n]
        send, recv, _ = refs[2 * n:]
        x, y, c = _place()
        me, sib = (x, y, c), (x, y, 1 - c)
        chips = [(1 - x, y), (x, 1 - y), (1 - x, 1 - y)]

        def slot(p):
            return 4 * p[0] + 2 * p[1] + p[2]

        def maker(t, k, to):
            def make(s, d):
                return pltpu.make_async_remote_copy(src_ref=s, dst_ref=d, send_sem=send.at[7 * t + k],
                                                    recv_sem=recv.at[7 * t + k], device_id=to, device_id_type=MESH)
            return make

        def landing(t, k, block):
            dst = outs[t].at[slot(block)]
            return maker(t, k, me)(dst, dst)

        sent = []
        for t in range(n):
            dst = outs[t].at[slot(me)]
            sent.append(_split_start(maker(t, 0, sib), ins[t], dst, pieces[t]))
            for j, chip in enumerate(chips):
                sent.append(_split_start(maker(t, 1 + j, (*chip, c)), ins[t], dst, pieces[t]))
        for j, chip in enumerate(chips):
            for t in range(n):
                landing(t, 1 + j, (*chip, c)).wait_recv()
                blk = outs[t].at[slot((*chip, c))]
                sent.append(_split_start(maker(t, 4 + j, sib), blk, blk, pieces[t]))
        for t in range(n):
            landing(t, 0, sib).wait_recv()
            for j, chip in enumerate(chips):
                landing(t, 4 + j, (*chip, 1 - c)).wait_recv()
        for cp in sent:
            cp.wait_send()

    outs = [_sds((N_DEV,) + a.shape, a.dtype) for a in arrs]
    got = _comm_call(body, name, arrs, outs, 7 * n, 1)
    x, y, c = _place()
    return [lax.dynamic_update_index_in_dim(g, a, 4 * x + 2 * y + c, 0) for g, a in zip(got, arrs)]


def sibling_send(arrs, name):
    n = len(arrs)
    pieces = [_pieces(a.shape[1:], a.dtype.itemsize) for a in arrs]

    def body(*refs):
        ins, theirs = refs[:n], refs[n:2 * n]
        send, recv, _ = refs[2 * n:]
        x, y, c = _place()
        rem = []
        for t in range(n):
            def make(s, d, t=t):
                return pltpu.make_async_remote_copy(src_ref=s, dst_ref=d, send_sem=send.at[t], recv_sem=recv.at[t],
                                                    device_id=(x, y, 1 - c), device_id_type=MESH)
            rem.append(_split_start(make, ins[t].at[1 - c], theirs[t], pieces[t]))
        for cp in rem:
            cp.wait_recv()
        for cp in rem:
            cp.wait_send()

    outs = [_sds(a.shape[1:], a.dtype) for a in arrs]
    return _comm_call(body, name, arrs, outs, n, 1)


def exchange_chips(arrs, name):
    n = len(arrs)
    pieces = [_pieces(a.shape[1:], a.dtype.itemsize) for a in arrs]

    def body(*refs):
        ins, outs = refs[:n], refs[n:2 * n]
        send, recv, _ = refs[2 * n:]
        x, y, c = _place()
        peers = [(1 - x, y), (x, 1 - y), (1 - x, 1 - y)]
        rem = []
        for t in range(n):
            for j, (px, py) in enumerate(peers):
                def make(s, d, t=t, j=j, px=px, py=py):
                    return pltpu.make_async_remote_copy(
                        src_ref=s, dst_ref=d, send_sem=send.at[3 * t + j], recv_sem=recv.at[3 * t + j],
                        device_id=(px, py, c), device_id_type=MESH)
                rem.append(_split_start(make, ins[t].at[2 * px + py], outs[t].at[j], pieces[t]))
        for cp in rem:
            cp.wait_recv()
        for cp in rem:
            cp.wait_send()

    outs = [_sds((N_CHIPS - 1,) + a.shape[1:], a.dtype) for a in arrs]
    return _comm_call(body, name, arrs, outs, 3 * n, 1)


def sibling_swap(arrs, name):
    n = len(arrs)
    pieces = [_pieces(a.shape, a.dtype.itemsize) for a in arrs]

    def body(*refs):
        ins, outs = refs[:n], refs[n:2 * n]
        send, recv, _ = refs[2 * n:]
        x, y, c = _place()
        rem = []
        for t in range(n):
            def make(s, d, t=t):
                return pltpu.make_async_remote_copy(src_ref=s, dst_ref=d, send_sem=send.at[t], recv_sem=recv.at[t],
                                                    device_id=(x, y, 1 - c), device_id_type=MESH)
            rem.append(_split_start(make, ins[t], outs[t], pieces[t]))
        for cp in rem:
            cp.wait_recv()
        for cp in rem:
            cp.wait_send()

    outs = [_sds(a.shape, a.dtype) for a in arrs]
    return _comm_call(body, name, arrs, outs, n, 1)


def _peer_copies(srcs, lands, send, recv, mode):
    x, y, c = _place()
    my_chip = 2 * x + y
    out = []
    for t in range(len(srcs)):
        for j, (px, py) in enumerate([(1 - x, y), (x, 1 - y), (1 - x, 1 - y)]):
            if mode == "gather":
                s, dst = srcs[t], lands[t].at[c, my_chip]
            else:
                s, dst = srcs[t].at[2 * px + py], lands[t].at[j]
            out.append(pltpu.make_async_remote_copy(
                src_ref=s, dst_ref=dst, send_sem=send.at[3 * t + j], recv_sem=recv.at[3 * t + j],
                device_id=(px, py, c), device_id_type=MESH))
    return out


HBM = pl.BlockSpec(memory_space=pltpu.HBM)
SEM = pl.BlockSpec(memory_space=pltpu.SEMAPHORE)
EFFECT = pltpu.SideEffectType.DATAFLOW_SIDE_EFFECTING


def ici_start(srcs, lands, mode, name):
    n = len(srcs)

    def body(*refs):
        send, recv = refs[2 * n], refs[2 * n + 1]
        for cp in _peer_copies(refs[:n], refs[n:2 * n], send, recv, mode):
            cp.start()
        refs[-1][...] = jnp.zeros_like(refs[-1])

    thru = [pltpu.HBM(a.shape, a.dtype) for a in list(srcs) + list(lands)]
    outs = pl.pallas_call(
        body, name=name, in_specs=[HBM] * (2 * n), out_specs=[SEM, SEM] + [HBM] * (2 * n) + [pl.BlockSpec(memory_space=pltpu.VMEM)],
        out_shape=[pltpu.SemaphoreType.DMA((3 * n,)), pltpu.SemaphoreType.DMA((3 * n,))] + thru + [_sds((8, 128))],
        input_output_aliases={i: 2 + i for i in range(2 * n)},
        compiler_params=pltpu.CompilerParams(has_side_effects=EFFECT),
    )(*[pltpu.with_memory_space_constraint(a, pltpu.HBM) for a in list(srcs) + list(lands)])
    return dict(send=outs[0], recv=outs[1], srcs=outs[2:2 + n], lands=outs[2 + n:2 + 2 * n], token=outs[-1])


def ici_wait(handle, after, mode, name):
    n = len(handle["srcs"])

    def body(*refs):
        send, recv = refs[2 * n], refs[2 * n + 1]
        for cp in _peer_copies(refs[:n], refs[n:2 * n], send, recv, mode):
            cp.wait_send()
            cp.wait_recv()

    arrs = list(handle["srcs"]) + list(handle["lands"])
    outs = pl.pallas_call(
        body, name=name, in_specs=[HBM] * (2 * n) + [SEM, SEM, ANY], out_specs=[HBM] * (2 * n),
        out_shape=[pltpu.HBM(a.shape, a.dtype) for a in arrs], input_output_aliases={i: i for i in range(2 * n)},
        compiler_params=pltpu.CompilerParams(has_side_effects=EFFECT),
    )(*arrs, handle["send"], handle["recv"], after)
    return outs[:n], outs[n:]


def gather_share(blocks, lands, name):
    n = len(blocks)

    def body(*refs):
        own, buf = refs[:n], refs[2 * n:3 * n]
        done, send, recv = refs[3 * n:]
        x, y, c = _place()
        my_chip = 2 * x + y
        chips = [2 * (1 - x) + y, 2 * x + (1 - y), 2 * (1 - x) + (1 - y)]
        sent = []
        for t in range(n):
            def make(s, d, k, t=t):
                return pltpu.make_async_remote_copy(src_ref=s, dst_ref=d, send_sem=send.at[4 * t + k],
                                                    recv_sem=recv.at[4 * t + k], device_id=(x, y, 1 - c),
                                                    device_id_type=MESH)
            cp = make(own[t], buf[t].at[c, my_chip], 0)
            cp.start()
            sent.append(cp)
            for k, pc in enumerate(chips):
                cp = make(buf[t].at[c, pc], buf[t].at[c, pc], 1 + k)
                cp.start()
                sent.append(cp)
        for t in range(n):
            for k in range(4):
                got = buf[t].at[1 - c, k]
                pltpu.make_async_remote_copy(src_ref=got, dst_ref=got, send_sem=send.at[4 * t + k],
                                             recv_sem=recv.at[4 * t + k], device_id=(x, y, 1 - c),
                                             device_id_type=MESH).wait_recv()
        for cp in sent:
            cp.wait_send()
        done[...] = jnp.zeros_like(done)

    outs = pl.pallas_call(
        body, name=name, in_specs=[ANY] * (2 * n), out_specs=[ANY] * n + [pl.BlockSpec(memory_space=pltpu.VMEM)],
        out_shape=[_sds(a.shape, a.dtype) for a in lands] + [_sds((8, 128))],
        input_output_aliases={n + t: t for t in range(n)},
        scratch_shapes=[pltpu.SemaphoreType.DMA((4 * n,)), pltpu.SemaphoreType.DMA((4 * n,))],
    )(*blocks, *lands)
    return outs[:n], outs[n]


@jax.custom_vjp
def _build_w_in(w4):
    full = w4.reshape(-1, w4.shape[-1])
    parts = []
    for _, start, width, wp in PROJ_SEGS:
        if width:
            parts.append(full[start:start + width])
        if wp > width:
            parts.append(jnp.zeros((wp - width, full.shape[1]), full.dtype))
    return jnp.concatenate(parts, axis=0)


def _build_w_in_f(w4):
    return _build_w_in(w4), None


def _build_w_in_b(_, g):
    parts, at = [], 0
    for _, _, width, wp in PROJ_SEGS:
        if width:
            parts.append(g[at:at + width])
        at += wp
    return (jnp.concatenate(parts, axis=0).reshape(N_CHIPS, -1, g.shape[1]),)


_build_w_in.defvjp(_build_w_in_f, _build_w_in_b)


def _split_w_uq(w):
    w3 = w.reshape(w.shape[0], MLA_HEADS, MLA_NOPE + MLA_ROPE)
    return w3[:, :, :MLA_NOPE].reshape(w.shape[0], -1), w3[:, :, MLA_NOPE:].reshape(w.shape[0], -1)


def _swap_halves(t, width):
    t3 = t.reshape(t.shape[0], -1, 2, width // 2)
    return jnp.concatenate([t3[:, :, 1:], t3[:, :, :1]], axis=2).reshape(t.shape)


def _pad_heads(t, width):
    t3 = t.reshape(t.shape[0], -1, width)
    t3 = jnp.pad(t3, ((0, 0), (0, 0), (0, HEAD_LANES - width)))
    return t3.reshape(t.shape[0], -1).astype(BF16)


def _layer(xh, mod, big, small, rope_q, rope_k):
    d = D_MODEL
    shift, scale, gate = mod[None, 0:d], mod[None, d:2 * d], mod[None, 2 * d:3 * d]
    w_al = _build_w_in(big["w_in"])
    proj = mod_mm(xh, small["norm_g"][None], scale, shift, w_al)
    gq, gk, gv, glr, mq, mkv, mkr, cb, cc, cx, _, z = split_proj(proj)

    rk = GLA_RANK
    hk = GLA_HEADS * GLA_DK
    wg = jnp.zeros((128, 2 * hk), F32)
    wg = wg.at[0:rk, 0:hk].set(small["gla_wg_f"]).at[rk:2 * rk, hk:].set(small["gla_wg_b"])
    bg = jnp.concatenate([small["gla_bg_f"], small["gla_bg_b"]])[None]
    la = gate_act(mm(glr, wg), bg)
    o_gla = rmsnorm(gla(gq, gk, gv, la), small["gla_norm_g"][None])

    cq = rmsnorm(mq, small["mla_q_norm_g"][None])
    w_nope, w_rope = _split_w_uq(jnp.concatenate([big["w_uq"][j] for j in range(N_CHIPS)], axis=1))
    qn = mm16(cq, w_nope)
    qr = mm(cq, w_rope)
    qr = fma(qr, rope_q[0], _swap_halves(qr, MLA_ROPE), rope_q[1])
    ckv = rmsnorm(mkv, small["mla_kv_norm_g"][None])
    kv = mm16(ckv, jnp.concatenate([big["w_ukv"][j] for j in range(N_CHIPS)], axis=1))
    kr = mkr[:, :MLA_ROPE]
    kr = fma(kr, rope_k[0], _swap_halves(kr, MLA_ROPE), rope_k[1])
    o_mla = rmsnorm(attn(qn, _pad_heads(qr, MLA_ROPE), kv, _pad_heads(kr, MLA_ROPE)), small["mla_out_g"][None])

    cw = jnp.concatenate([small["conv_w"], jnp.zeros((5, CONV_CH), F32)], axis=0)
    o_conv = rmsnorm(conv_op(cb, cc, cx, cw), small["conv_out_g"][None])

    w_out = big["w_out"].reshape(d, d)
    return out_block((o_gla, o_mla, o_conv), z, w_out, xh, gate)


SMALL_REPL = ("norm_g", "gla_bg_f", "gla_bg_b", "gla_norm_g", "mla_q_norm_g", "mla_kv_norm_g", "mla_out_g",
              "conv_out_g")
SMALL_SHARDED = ("gla_wg_f", "gla_wg_b", "conv_w")
BIG = ("w_in", "w_out", "w_uq", "w_ukv")
HALF_AXIS = (1, 0, 0, 0)


def kernel(x, c, positions, ada_w, ada_b, norm_g, w_in, gla_wg_f, gla_bg_f, gla_wg_b, gla_bg_b, gla_norm_g, mla_q_norm_g, mla_kv_norm_g, mla_w_uq, mla_w_ukv, mla_out_g, conv_w, conv_out_g, w_out, final_g, loss_target, m_ada_w, m_ada_b, m_norm_g, m_w_in, m_gla_wg_f, m_gla_bg_f, m_gla_wg_b, m_gla_bg_b, m_gla_norm_g, m_mla_q_norm_g, m_mla_kv_norm_g, m_mla_w_uq, m_mla_w_ukv, m_mla_out_g, m_conv_w, m_conv_out_g, m_w_out, m_final_g, v_ada_w, v_ada_b, v_norm_g, v_w_in, v_gla_wg_f, v_gla_bg_f, v_gla_wg_b, v_gla_bg_b, v_gla_norm_g, v_mla_q_norm_g, v_mla_kv_norm_g, v_mla_w_uq, v_mla_w_ukv, v_mla_out_g, v_conv_w, v_conv_out_g, v_w_out, v_final_g):
    xi, yi, ci = _place()
    chip = 2 * xi + yi
    dev = 2 * chip + ci
    s = x.shape[1]
    d = D_MODEL
    weights = dict(ada_w=ada_w, ada_b=ada_b, norm_g=norm_g, w_in=w_in, gla_wg_f=gla_wg_f, gla_bg_f=gla_bg_f,
                   gla_wg_b=gla_wg_b, gla_bg_b=gla_bg_b, gla_norm_g=gla_norm_g, mla_q_norm_g=mla_q_norm_g,
                   mla_kv_norm_g=mla_kv_norm_g, mla_w_uq=mla_w_uq, mla_w_ukv=mla_w_ukv, mla_out_g=mla_out_g,
                   conv_w=conv_w, conv_out_g=conv_out_g, w_out=w_out, final_g=final_g)
    m_in = dict(ada_w=m_ada_w, ada_b=m_ada_b, norm_g=m_norm_g, w_in=m_w_in, gla_wg_f=m_gla_wg_f, gla_bg_f=m_gla_bg_f,
                gla_wg_b=m_gla_wg_b, gla_bg_b=m_gla_bg_b, gla_norm_g=m_gla_norm_g, mla_q_norm_g=m_mla_q_norm_g,
                mla_kv_norm_g=m_mla_kv_norm_g, mla_w_uq=m_mla_w_uq, mla_w_ukv=m_mla_w_ukv, mla_out_g=m_mla_out_g,
                conv_w=m_conv_w, conv_out_g=m_conv_out_g, w_out=m_w_out, final_g=m_final_g)
    v_in = dict(ada_w=v_ada_w, ada_b=v_ada_b, norm_g=v_norm_g, w_in=v_w_in, gla_wg_f=v_gla_wg_f, gla_bg_f=v_gla_bg_f,
                gla_wg_b=v_gla_wg_b, gla_bg_b=v_gla_bg_b, gla_norm_g=v_gla_norm_g, mla_q_norm_g=v_mla_q_norm_g,
                mla_kv_norm_g=v_mla_kv_norm_g, mla_w_uq=v_mla_w_uq, mla_w_ukv=v_mla_w_ukv, mla_out_g=v_mla_out_g,
                conv_w=v_conv_w, conv_out_g=v_conv_out_g, w_out=v_w_out, final_g=v_final_g)

    g_c, g_wgf, g_wgb, g_cw = all_gather8([c, gla_wg_f, gla_wg_b, conv_w], "gather_small")

    def unshard_cols(g):
        g4 = g[0::2]
        return g4.transpose(1, 2, 0, 3).reshape(g4.shape[1], g4.shape[2], -1)

    small_full = dict(gla_wg_f=unshard_cols(g_wgf), gla_wg_b=unshard_cols(g_wgb), conv_w=unshard_cols(g_cw))
    for nme in SMALL_REPL:
        small_full[nme] = weights[nme]
    smalls = [{nme: small_full[nme][l] for nme in SMALL_REPL + SMALL_SHARDED} for l in range(DEPTH)]

    big_src = (jnp.swapaxes(w_in, 1, 2), w_out, mla_w_uq, mla_w_ukv)

    def my_halves(l, zero=0):
        out = []
        for t, a in enumerate(big_src):
            n_half = a.shape[1 + HALF_AXIS[t]] // 2
            out.append(lax.dynamic_slice_in_dim(a[l], ci * n_half + zero, n_half, axis=HALF_AXIS[t]).astype(BF16))
        return out

    def landing(blocks):
        return [lax.empty((2, N_CHIPS) + b.shape, b.dtype) for b in blocks]

    def finish_gather(handle, after, tag):
        blocks, lands = ici_wait(handle, after, "gather", "gather_wait" + tag)
        lands, done = gather_share(blocks, lands, "gather_share" + tag)
        full = [lax.dynamic_update_slice(g, b[None, None], (ci, chip) + (0,) * b.ndim) for g, b in zip(lands, blocks)]
        return full, done

    halves0 = my_halves(0)
    started0 = ici_start(halves0, landing(halves0), "gather", "gather_start0")

    c_act = _silu_rows(g_c[:, 0, :])
    c_act16 = jnp.concatenate([c_act, jnp.zeros_like(c_act)], axis=0)
    n_ada = ada_w.shape[2]
    parts = []
    for l in range(DEPTH):
        bias = lax.dynamic_slice_in_dim(ada_b[l], chip * n_ada, n_ada)[None]
        parts.append(_mm(c_act16, ada_w[l], bias=bias, name="ada_fwd"))
    g_mod, = all_gather8([jnp.stack(parts)], "gather_mod")
    mod_mine = lax.dynamic_index_in_dim(g_mod[0::2], dev, 2, keepdims=False)
    mods = mod_mine.transpose(1, 0, 2).reshape(DEPTH, 3 * d)

    inv_freq = ROPE_THETA ** (-jnp.arange(0, MLA_ROPE, 2, dtype=F32) / MLA_ROPE)
    ang = positions[0].astype(F32)[:, None] * inv_freq
    cos, sin = jnp.cos(ang), jnp.sin(ang)
    rope_k = (jnp.concatenate([cos, cos], axis=1), jnp.concatenate([-sin, sin], axis=1))
    rope_q = (jnp.tile(rope_k[0], (1, MLA_HEADS)), jnp.tile(rope_k[1], (1, MLA_HEADS)))

    def run_layer(xh, mod, gathered, small):
        big = {nme: jnp.concatenate([g[0], g[1]], axis=HALF_AXIS[t] + 1) for t, (nme, g) in enumerate(zip(BIG, gathered))}
        return _layer(xh, mod, big, small, rope_q, rope_k)

    def head(hh, fg):
        return loss_op(rmsnorm(hh, fg[None]), loss_target[0])[0, 0]

    gathered0, done0 = finish_gather(started0, mods, "0")
    halves1 = my_halves(1, done0[0, 0].astype(jnp.int32))
    started1 = ici_start(halves1, landing(halves1), "gather", "gather_start1")
    h1, vjp0 = jax.vjp(run_layer, x[0], mods[0] + started1["token"][0, 0], gathered0, smalls[0])
    gathered1, _ = finish_gather(started1, h1, "1")
    h2, vjp1 = jax.vjp(run_layer, h1, mods[1], gathered1, smalls[1])
    loss_dev, vjp_head = jax.vjp(head, h2, final_g)
    dh2, dfinal = vjp_head(jnp.ones((), F32))

    c_idx = jnp.reshape(ci, (1,)).astype(jnp.int32)
    chip_idx = jnp.reshape(chip, (1,)).astype(jnp.int32)

    def reduce_begin(dgath, tag, zero=None):
        theirs = sibling_send(dgath, "reduce_sibling" + tag)
        pair = [_add_pair(a, b, c_idx) for a, b in zip(dgath, theirs)]
        shapes = [(N_CHIPS - 1,) + p.shape[1:] for p in pair]
        if zero is None:
            lands = [lax.empty(shp, BF16) for shp in shapes]
        else:
            lands = [jnp.broadcast_to(zero.astype(BF16), shp) for shp in shapes]
        return ici_start(pair, lands, "reduce", "reduce_start" + tag)

    def reduce_end(handle, after, tag):
        pair, landed = ici_wait(handle, after, "reduce", "reduce_wait" + tag)
        reduced = [_add_chips(p, q, chip_idx) for p, q in zip(pair, landed)]
        others = sibling_swap(reduced, "share_sibling" + tag)
        return [jnp.where(ci == 0, jnp.concatenate([own, other], axis=HALF_AXIS[t]),
                          jnp.concatenate([other, own], axis=HALF_AXIS[t]))
                for t, (own, other) in enumerate(zip(reduced, others))]

    dh1, dmod1, dgath1, dsmall1 = vjp1(dh2)
    reducing1 = reduce_begin(dgath1, "1")
    dx, dmod0, dgath0, dsmall0 = vjp0(dh1 + reducing1["token"][0, 0])
    dmods = jnp.stack([dmod0, dmod1])
    dsmalls = [dsmall0, dsmall1]

    pieces = [dmods.reshape(-1), dfinal]
    for nme in SMALL_REPL + SMALL_SHARDED:
        pieces.append(jnp.stack([dsmalls[l][nme] for l in range(DEPTH)]).reshape(-1))
    pieces.append(loss_dev.reshape(1))
    sizes = [p.shape[0] for p in pieces]
    flat = jnp.concatenate(pieces)
    padn = (-flat.shape[0]) % 128
    flat = jnp.concatenate([flat, jnp.zeros((padn,), F32)])[None]
    g_small, = all_gather8([flat], "gather_small_grads")
    total, small_done = _sum_devices(g_small)
    total = total[0]
    reducing0 = reduce_begin(dgath0, "0", small_done[0, 0])
    offs, at = [], 0
    for n_el in sizes:
        offs.append(at)
        at += n_el

    def piece(i, shape):
        return total[offs[i]:offs[i] + sizes[i]].reshape(shape)

    grads = {"ada_b": piece(0, (DEPTH, 3 * d)), "final_g": piece(1, (d,))}
    loss = piece(len(pieces) - 1, ())
    for i, nme in enumerate(SMALL_REPL + SMALL_SHARDED):
        full = piece(2 + i, small_full[nme].shape)
        if nme in SMALL_SHARDED:
            ncol = weights[nme].shape[2]
            full = lax.dynamic_slice_in_dim(full, chip * ncol, ncol, axis=2)
        grads[nme] = full

    dmod_all = g_small[:, 0, :DEPTH * 3 * d].reshape(N_DEV, DEPTH, 3 * d)
    dmod_cols = lax.dynamic_slice_in_dim(dmod_all, chip * n_ada, n_ada, axis=2)
    g_ada = []
    for l in range(DEPTH):
        dm16 = jnp.concatenate([dmod_cols[:, l], jnp.zeros((N_DEV, n_ada), F32)], axis=0)
        dm16 = dm16 + reducing0["token"][0, 0]
        g_ada.append(_mm(c_act16, dm16, ta=True, name="ada_bwd"))
    grads["ada_w"] = jnp.stack(g_ada)

    order = list(weights)
    big_names = ("w_in", "w_out", "mla_w_uq", "mla_w_ukv")
    delta, new_m, new_v = {}, {}, {}
    for nme in order:
        if nme not in big_names:
            delta[nme], new_m[nme], new_v[nme] = _adamw(weights[nme], grads[nme], m_in[nme], v_in[nme])

    def first(t):
        return t[(slice(0, 1),) * t.ndim].reshape(1)

    big_grads1 = reduce_end(reducing1, jnp.concatenate([first(dx), first(reducing0["token"])]), "1")
    done = [first(delta[nme]) for nme in order if nme not in big_names] + [first(g) for g in big_grads1]
    big_grads0 = reduce_end(reducing0, jnp.concatenate(done), "0")
    for nme, g0, g1 in zip(big_names, big_grads0, big_grads1):
        grads[nme] = jnp.stack([g0, g1])
    for nme in big_names:
        if nme == "w_in":
            w_t, m_t, v_t = (jnp.swapaxes(t, 1, 2) for t in (w_in, m_w_in, v_w_in))
            res = _adamw(w_t, grads[nme], m_t, v_t)
            delta[nme], new_m[nme], new_v[nme] = (jnp.swapaxes(t, 1, 2) for t in res)
            grads[nme] = jnp.swapaxes(grads[nme], 1, 2)
            continue
        delta[nme], new_m[nme], new_v[nme] = _adamw(weights[nme], grads[nme], m_in[nme], v_in[nme])
    return (loss, dx[None], *[grads[n_] for n_ in order], *[delta[n_] for n_ in order],
            *[new_m[n_] for n_ in order], *[new_v[n_] for n_ in order])
```

```python
import functools

import jax
import jax.numpy as jnp
from jax import lax
from jax.experimental import pallas as pl
from jax.experimental.pallas import tpu as pltpu

F32 = jnp.float32
BF16 = jnp.bfloat16
MESH = pl.DeviceIdType.MESH
HIGHEST = lax.Precision.HIGHEST

DEPTH = 2
D_MODEL = 2048
GLA_HEADS = 6
GLA_DK = 64
GLA_DV = 128
GLA_RANK = 16
GLA_TEMP = 16.0
GLA_CHUNK = 64
GLA_W = GLA_HEADS * GLA_DV
MLA_HEADS = 6
MLA_QL = 384
MLA_KVL = 256
MLA_NOPE = 128
MLA_ROPE = 64
MLA_DV = 128
MLA_W = MLA_HEADS * MLA_DV
CONV_CH = D_MODEL - GLA_W - MLA_W
ROPE_THETA = 10000.0
EPS = 1e-6
IN_DIM = 5856
N_CHIPS = 4
N_DEV = 8

ADAM_LR = 0.001
ADAM_B1 = 0.9
ADAM_B2 = 0.999
ADAM_EPS = 1e-08
ADAM_WD = 0.01
ADAM_STEP = 10

PROJ_SEGS = (
    ("gq", 0, 384, 384), ("gk", 384, 384, 384), ("gv", 768, 768, 768), ("glr", 1536, 32, 128),
    ("mq", 1568, 384, 384), ("mkv", 1952, 256, 256), ("mkr", 2208, 64, 128),
    ("cb", 2272, 512, 512), ("cc", 2784, 512, 512), ("cx", 3296, 512, 512),
    ("pad", 3808, 0, 128), ("z", 3808, 2048, 2048),
)
PROJ_AL = sum(s[3] for s in PROJ_SEGS)

ANY = pl.BlockSpec(memory_space=pl.ANY)
VMEM_LIMIT = 48 * 1024 * 1024
BLOCK_BYTES = 2 * 1024 * 1024


def _params(sem=None):
    return pltpu.CompilerParams(dimension_semantics=sem, vmem_limit_bytes=VMEM_LIMIT)


def _dot(a, b, ca, cb, precision=None):
    return lax.dot_general(a, b, (((ca,), (cb,)), ((), ())), preferred_element_type=F32, precision=precision)


def _tile(dim, prefs):
    for t in prefs:
        if dim % t == 0:
            return t
    return dim


def _pick_rows(rows, width, itemsize=4):
    for t in (2048, 1024, 512, 256, 128, 64, 32, 16, 8):
        if rows % t == 0 and t * width * itemsize <= BLOCK_BYTES:
            return t
    return rows


def _mm(a, b, *, ta=False, tb=False, bias=None, out_dtype=F32, name="mm"):
    if ta:
        K, M = a.shape
    else:
        M, K = a.shape
    if tb:
        N, Kb = b.shape
    else:
        Kb, N = b.shape
    assert K == Kb, (a.shape, b.shape, ta, tb)
    tm = _tile(M, (2048, 1024, 512, 256, 128))
    tn = _tile(N, (512, 384, 256, 128) if tm >= 2048 else (1024, 512, 384, 256, 128))
    tk = _tile(K, (2048, 1024, 512, 256, 128))
    nk = K // tk
    has_bias = bias is not None

    def body(*refs):
        a_ref, b_ref = refs[0], refs[1]
        bias_ref = refs[2] if has_bias else None
        o_ref = refs[3 if has_bias else 2]
        part = _dot(a_ref[...].astype(BF16), b_ref[...].astype(BF16), 0 if ta else 1, 1 if tb else 0)

        def finish(r):
            if has_bias:
                r = r + bias_ref[...]
            o_ref[...] = r.astype(out_dtype)

        if nk == 1:
            finish(part)
            return
        acc_ref = refs[-1]
        k = pl.program_id(2)

        @pl.when(k == 0)
        def _():
            acc_ref[...] = part

        @pl.when(k != 0)
        def _():
            acc_ref[...] += part

        @pl.when(k == nk - 1)
        def _():
            finish(acc_ref[...])

    a_spec = pl.BlockSpec((tk, tm), lambda i, j, k: (k, i)) if ta else pl.BlockSpec((tm, tk), lambda i, j, k: (i, k))
    b_spec = pl.BlockSpec((tn, tk), lambda i, j, k: (j, k)) if tb else pl.BlockSpec((tk, tn), lambda i, j, k: (k, j))
    in_specs = [a_spec, b_spec]
    args = [a, b]
    if has_bias:
        in_specs.append(pl.BlockSpec((1, tn), lambda i, j, k: (0, j)))
        args.append(bias)
    return pl.pallas_call(
        body, name=name, grid=(M // tm, N // tn, nk),
        in_specs=in_specs, out_specs=pl.BlockSpec((tm, tn), lambda i, j, k: (i, j)),
        out_shape=jax.ShapeDtypeStruct((M, N), out_dtype),
        scratch_shapes=[pltpu.VMEM((tm, tn), F32)] if nk > 1 else [],
        compiler_params=_params(("parallel", "parallel", "arbitrary")),
    )(*args)


@jax.custom_vjp
def mm(a, b):
    return _mm(a, b, name="mm_fwd")


def _mm_f(a, b):
    return _mm(a, b, name="mm_fwd"), (a, b)


def _mm_b(res, g):
    a, b = res
    return _mm(g, b, tb=True, out_dtype=a.dtype, name="mm_da"), _mm(a, g, ta=True, out_dtype=b.dtype, name="mm_db")


mm.defvjp(_mm_f, _mm_b)


@jax.custom_vjp
def mm16(a, b):
    return _mm(a, b, out_dtype=BF16, name="mm16_fwd")


def _mm16_f(a, b):
    return mm16(a, b), (a, b)


mm16.defvjp(_mm16_f, _mm_b)


def _rows(body, name, tiled, full, tiled_out, acc_out, tr=None):
    rows = tiled[0].shape[0]
    if tr is None:
        width = max([a.shape[1] for a in tiled] + [s.shape[1] for s in tiled_out])
        tr = _pick_rows(rows, width)
    in_specs = [pl.BlockSpec((tr, a.shape[1]), lambda i: (i, 0)) for a in tiled]
    in_specs += [pl.BlockSpec(a.shape, lambda i: (0, 0)) for a in full]
    out_specs = [pl.BlockSpec((tr, s.shape[1]), lambda i: (i, 0)) for s in tiled_out]
    out_specs += [pl.BlockSpec(s.shape, lambda i: (0, 0)) for s in acc_out]

    def wrapped(*refs):
        body(pl.program_id(0), *refs)

    outs = pl.pallas_call(
        wrapped, name=name, grid=(rows // tr,), in_specs=in_specs, out_specs=out_specs,
        out_shape=list(tiled_out) + list(acc_out),
        compiler_params=_params(("arbitrary",)),
    )(*tiled, *full)
    return outs


def _sds(shape, dtype=F32):
    return jax.ShapeDtypeStruct(tuple(shape), dtype)


def _acc(step, ref, val):
    @pl.when(step == 0)
    def _():
        ref[...] = val

    @pl.when(step != 0)
    def _():
        ref[...] += val


def _colsum(v):
    return jnp.sum(v, axis=0, keepdims=True)


def _rstd(x):
    return lax.rsqrt(jnp.mean(x * x, axis=-1, keepdims=True) + EPS)


def _norm_grid(x, g):
    rows, w = x.shape[0], g.shape[1]
    tr = _pick_rows(rows, w)
    blk = pl.BlockSpec((tr, w), lambda i, j: (i, j))
    gblk = pl.BlockSpec((1, w), lambda i, j: (0, 0))
    return (rows // tr, x.shape[1] // w), blk, gblk


@jax.custom_vjp
def rmsnorm(x, g):
    def body(x_ref, g_ref, o_ref):
        x = x_ref[...]
        o_ref[...] = x * _rstd(x) * g_ref[...]

    grid, blk, gblk = _norm_grid(x, g)
    return pl.pallas_call(body, name="rmsnorm_fwd", grid=grid, in_specs=[blk, gblk], out_specs=blk,
                          out_shape=_sds(x.shape), compiler_params=_params(("parallel", "parallel")))(x, g)


def _rmsnorm_f(x, g):
    return rmsnorm(x, g), (x, g)


def _rmsnorm_b(res, dy):
    x, g = res

    def body(x_ref, dy_ref, g_ref, dx_ref, dg_ref):
        x = x_ref[...]
        dy = dy_ref[...]
        r = _rstd(x)
        xh = x * r
        dxh = dy * g_ref[...]
        dx_ref[...] = r * (dxh - xh * jnp.mean(dxh * xh, axis=-1, keepdims=True))
        first = jnp.logical_and(pl.program_id(0) == 0, pl.program_id(1) == 0)
        _acc(jnp.where(first, 0, 1), dg_ref, _colsum(dy * xh))

    grid, blk, gblk = _norm_grid(x, g)
    dx, dg = pl.pallas_call(body, name="rmsnorm_bwd", grid=grid, in_specs=[blk, blk, gblk], out_specs=[blk, gblk],
                            out_shape=[_sds(x.shape), _sds(g.shape)],
                            compiler_params=_params(("arbitrary", "arbitrary")))(x, dy, g)
    return dx, dg


rmsnorm.defvjp(_rmsnorm_f, _rmsnorm_b)


def _modulate(x, g, scale, shift):
    def body(i, x_ref, g_ref, sc_ref, sh_ref, o_ref):
        x = x_ref[...]
        xn = x * _rstd(x) * g_ref[...]
        o_ref[...] = (xn * (1.0 + sc_ref[...]) + sh_ref[...]).astype(BF16)
    return _rows(body, "modulate_fwd", [x], [g, scale, shift], [_sds(x.shape, BF16)], [])[0]


def _modulate_bwd(x, g, scale, shift, dh):
    def body(i, x_ref, dh_ref, g_ref, sc_ref, dx_ref, dg_ref, dsc_ref, dsh_ref):
        x = x_ref[...]
        dh = dh_ref[...]
        gv = g_ref[...]
        r = _rstd(x)
        xh = x * r
        dxn = dh * (1.0 + sc_ref[...])
        dxh = dxn * gv
        dx_ref[...] = r * (dxh - xh * jnp.mean(dxh * xh, axis=-1, keepdims=True))
        _acc(i, dg_ref, _colsum(dxn * xh))
        _acc(i, dsc_ref, _colsum(dh * (xh * gv)))
        _acc(i, dsh_ref, _colsum(dh))

    v = _sds(g.shape)
    return _rows(body, "modulate_bwd", [x, dh], [g, scale], [_sds(x.shape)], [v, v, v])


@jax.custom_vjp
def mod_mm(x, g, scale, shift, wt):
    return _mm(_modulate(x, g, scale, shift), wt, tb=True, name="mm_in")


def _mod_mm_f(x, g, scale, shift, wt):
    h = _modulate(x, g, scale, shift)
    return _mm(h, wt, tb=True, name="mm_in"), (x, g, scale, shift, wt, h)


def _mod_mm_b(res, dproj):
    x, g, scale, shift, wt, h = res
    dproj = dproj.astype(BF16)
    dh = _mm(dproj, wt, name="mm_in_dh")
    dwt = _mm(dproj, h, ta=True, out_dtype=wt.dtype, name="mm_in_dw")
    dx, dg, dsc, dsh = _modulate_bwd(x, g, scale, shift, dh)
    return dx, dg, dsc, dsh, dwt


mod_mm.defvjp(_mod_mm_f, _mod_mm_b)


def _sigmoid(z):
    return 1.0 / (1.0 + jnp.exp(-z))


def _gate_mul(o, z):
    def body(i, o_ref, z_ref, y_ref):
        z = z_ref[...]
        y_ref[...] = (o_ref[...] * (z * _sigmoid(z))).astype(BF16)
    return _rows(body, "gate_mul_fwd", [o, z], [], [_sds(o.shape, BF16)], [])[0]


def _gate_mul_bwd(o, z, dy):
    def body(i, o_ref, z_ref, dy_ref, do_ref, dz_ref):
        z = z_ref[...]
        dy = dy_ref[...]
        s = _sigmoid(z)
        do_ref[...] = dy * (z * s)
        dz_ref[...] = dy * o_ref[...] * (s * (1.0 + z * (1.0 - s)))
    return _rows(body, "gate_mul_bwd", [o, z, dy], [], [_sds(o.shape), _sds(o.shape)], [])


def _residual(x, u, gate):
    def body(i, x_ref, u_ref, g_ref, o_ref):
        o_ref[...] = x_ref[...] + g_ref[...] * u_ref[...]
    return _rows(body, "residual_fwd", [x, u], [gate], [_sds(x.shape)], [])[0]


def _residual_bwd(d, u, gate):
    def body(i, d_ref, u_ref, g_ref, du_ref, dg_ref):
        d = d_ref[...]
        du_ref[...] = (g_ref[...] * d).astype(BF16)
        _acc(i, dg_ref, _colsum(d * u_ref[...]))

    return _rows(body, "residual_bwd", [d, u], [gate], [_sds(u.shape, BF16)], [_sds(gate.shape)])


@jax.custom_vjp
def out_block(o, z, w, x, gate):
    return _residual(x, _mm(_gate_mul(o, z), w, name="mm_out"), gate)


def _out_block_f(o, z, w, x, gate):
    y = _gate_mul(o, z)
    u = _mm(y, w, name="mm_out")
    return _residual(x, u, gate), (o, z, w, y, u, gate)


def _out_block_b(res, d):
    o, z, w, y, u, gate = res
    du, dgate = _residual_bwd(d, u, gate)
    dy = _mm(du, w, tb=True, name="mm_out_dy")
    dw = _mm(y, du, ta=True, out_dtype=w.dtype, name="mm_out_dw")
    do, dz = _gate_mul_bwd(o, z, dy)
    return do, dz, dw, d, dgate


out_block.defvjp(_out_block_f, _out_block_b)


@jax.custom_vjp
def gate_act(u, b):
    def body(i, u_ref, b_ref, o_ref):
        t = u_ref[...] + b_ref[...]
        o_ref[...] = (jnp.minimum(t, 0.0) - jnp.log(1.0 + jnp.exp(-jnp.abs(t)))) / GLA_TEMP
    return _rows(body, "gate_act_fwd", [u], [b], [_sds(u.shape)], [])[0]


def _gate_act_f(u, b):
    return gate_act(u, b), (u, b)


def _gate_act_b(res, d):
    u, b = res

    def body(i, u_ref, d_ref, b_ref, du_ref, db_ref):
        t = u_ref[...] + b_ref[...]
        du = d_ref[...] * _sigmoid(-t) / GLA_TEMP
        du_ref[...] = du
        _acc(i, db_ref, _colsum(du))

    du, db = _rows(body, "gate_act_bwd", [u, d], [b], [_sds(u.shape)], [_sds(b.shape)])
    return du, db


gate_act.defvjp(_gate_act_f, _gate_act_b)


@jax.custom_vjp
def fma(a, b, c, d):
    def body(i, a_ref, b_ref, c_ref, d_ref, o_ref):
        o_ref[...] = a_ref[...] * b_ref[...] + c_ref[...] * d_ref[...]
    return _rows(body, "fma_fwd", [a, b, c, d], [], [_sds(a.shape)], [])[0]


def _fma_f(a, b, c, d):
    return fma(a, b, c, d), (b, d)


def _fma_b(res, g):
    b, d = res

    def body(i, g_ref, b_ref, d_ref, da_ref, dc_ref):
        g = g_ref[...]
        da_ref[...] = g * b_ref[...]
        dc_ref[...] = g * d_ref[...]

    da, dc = _rows(body, "fma_bwd", [g, b, d], [], [_sds(g.shape), _sds(g.shape)], [])
    return da, jnp.zeros_like(b), dc, jnp.zeros_like(d)


fma.defvjp(_fma_f, _fma_b)


def _silu_rows(c):
    def body(i, c_ref, o_ref):
        v = c_ref[...]
        o_ref[...] = v * _sigmoid(v)
    return _rows(body, "silu", [c], [], [_sds(c.shape)], [])[0]


@jax.custom_vjp
def loss_op(y, t):
    return _loss_fwd(y, t)[0]


def _loss_fwd(y, t):
    inv = 1.0 / y.shape[1]

    def body(i, y_ref, t_ref, d_ref, l_ref):
        e = y_ref[...] - t_ref[...]
        d_ref[...] = e * inv
        _acc(i, l_ref, jnp.sum(_colsum(e * e), axis=1, keepdims=True) * (0.5 * inv))

    d, l = _rows(body, "loss_fwd", [y, t], [], [_sds(y.shape)], [_sds((1, 1))])
    return l, d


def _loss_f(y, t):
    l, d = _loss_fwd(y, t)
    return l, d


def _loss_b(d, g):
    return d * g, jnp.zeros_like(d)


loss_op.defvjp(_loss_f, _loss_b)


def _conv_terms(cc, cx, rows, n):
    u = cc * cx
    up = jnp.where(rows == 0, 0.0, pltpu.roll(u, 1, 0))
    un = jnp.where(rows == n - 1, 0.0, pltpu.roll(u, n - 1, 0))
    return u, up, un


CONV_COLS = 128


def _conv_specs(s, n_in):
    blk = pl.BlockSpec((s, CONV_COLS), lambda j: (0, j))
    wblk = pl.BlockSpec((8, CONV_COLS), lambda j: (0, j))
    return [blk] * n_in + [wblk], blk, wblk


@jax.custom_vjp
def conv_op(cb, cc, cx, w):
    s, ch = cb.shape

    def body(cb_ref, cc_ref, cx_ref, w_ref, o_ref):
        rows = lax.broadcasted_iota(jnp.int32, (s, CONV_COLS), 0)
        u, up, un = _conv_terms(cc_ref[...], cx_ref[...], rows, s)
        conv = up * w_ref[0:1, :] + u * w_ref[1:2, :] + un * w_ref[2:3, :]
        o_ref[...] = cb_ref[...] * conv

    in_specs, blk, _ = _conv_specs(s, 3)
    return pl.pallas_call(
        body, name="conv_fwd", grid=(ch // CONV_COLS,), in_specs=in_specs, out_specs=blk,
        out_shape=_sds(cb.shape), compiler_params=_params(("parallel",)),
    )(cb, cc, cx, w)


def _conv_f(cb, cc, cx, w):
    return conv_op(cb, cc, cx, w), (cb, cc, cx, w)


def _conv_b(res, d):
    cb, cc, cx, w = res
    s, ch = cb.shape

    def body(cb_ref, cc_ref, cx_ref, d_ref, w_ref, dcb_ref, dcc_ref, dcx_ref, dw_ref):
        rows = lax.broadcasted_iota(jnp.int32, (s, CONV_COLS), 0)
        cc_v = cc_ref[...]
        cx_v = cx_ref[...]
        u, up, un = _conv_terms(cc_v, cx_v, rows, s)
        w0, w1, w2 = w_ref[0:1, :], w_ref[1:2, :], w_ref[2:3, :]
        dv = d_ref[...]
        dcb_ref[...] = dv * (up * w0 + u * w1 + un * w2)
        dconv = dv * cb_ref[...]
        d_next = jnp.where(rows == s - 1, 0.0, pltpu.roll(dconv, s - 1, 0))
        d_prev = jnp.where(rows == 0, 0.0, pltpu.roll(dconv, 1, 0))
        du = w0 * d_next + w1 * dconv + w2 * d_prev
        dcc_ref[...] = du * cx_v
        dcx_ref[...] = du * cc_v
        dw_ref[...] = jnp.zeros_like(dw_ref)
        dw_ref[0:1, :] = _colsum(dconv * up)
        dw_ref[1:2, :] = _colsum(dconv * u)
        dw_ref[2:3, :] = _colsum(dconv * un)

    in_specs, blk, wblk = _conv_specs(s, 4)
    v = _sds(cb.shape)
    return tuple(pl.pallas_call(
        body, name="conv_bwd", grid=(ch // CONV_COLS,), in_specs=in_specs, out_specs=[blk, blk, blk, wblk],
        out_shape=[v, v, v, _sds(w.shape)], compiler_params=_params(("parallel",)),
    )(cb, cc, cx, d, w))


conv_op.defvjp(_conv_f, _conv_b)


def _gla_masks(rev):
    c = GLA_CHUNK
    row = lax.broadcasted_iota(jnp.int32, (c, c), 0)
    col = lax.broadcasted_iota(jnp.int32, (c, c), 1)
    mask = (row < col) if rev else (row >= col)
    return rev, mask


def _chunk_cumsum(g, rev):
    c = g.shape[0]
    row = lax.broadcasted_iota(jnp.int32, g.shape, 0)
    b = g
    s = 1
    while s < c:
        if rev:
            b = b + jnp.where(row < c - s, pltpu.roll(b, c - s, 0), 0.0)
        else:
            b = b + jnp.where(row >= s, pltpu.roll(b, s, 0), 0.0)
        s *= 2
    return b


GLA_UNROLL = 16


def _gla_rows(n):
    return pl.ds(pl.multiple_of(n * GLA_CHUNK, GLA_CHUNK), GLA_CHUNK)


def _gla_scan(s_ref, bt_ref, st_ref, n_chunks, descending):
    st_ref[...] = jnp.zeros_like(st_ref)

    def step(i, carry):
        n = (n_chunks - 1 - i) if descending else i
        own = s_ref[n]
        st = st_ref[...]
        s_ref[n] = st
        st_ref[...] = st * jnp.exp(bt_ref[n]) + own
        return carry

    lax.fori_loop(0, n_chunks, step, 0)


GLA_PAIR = 2


def _gla_specs(s):
    dk, dv = GLA_DK, GLA_DV
    n_pairs = GLA_HEADS // GLA_PAIR
    blk_k = pl.BlockSpec((s, GLA_PAIR * dk), lambda p: (0, p))
    blk_gb = pl.BlockSpec((s, GLA_PAIR * dk), lambda p: (0, n_pairs + p))
    blk_v = pl.BlockSpec((s, GLA_PAIR * dv), lambda p: (0, p))
    return n_pairs, blk_k, blk_gb, blk_v


def _head_lanes(hh):
    lane = lax.broadcasted_iota(jnp.int32, (1, GLA_PAIR * GLA_DK), 1)
    return jnp.logical_and(lane >= hh * GLA_DK, lane < (hh + 1) * GLA_DK)


def _gla_fwd(q, k, v, la):
    s = q.shape[0]
    dk, dv = GLA_DK, GLA_DV
    pw = GLA_PAIR * dk
    n_chunks = s // GLA_CHUNK
    scale = GLA_DK ** -0.5

    def body(q_ref, k_ref, v_ref, gf_ref, gb_ref, o_ref, sf_ref, sb_ref, bf_ref, bb_ref, btf_ref, btb_ref, st_ref):
        masks = [_gla_masks(rev) for rev in (False, True)]
        dirs = ((False, gf_ref, sf_ref, bf_ref, btf_ref), (True, gb_ref, sb_ref, bb_ref, btb_ref))

        def decays(n, carry):
            rows = _gla_rows(n)
            for rev, g_ref, _, b_ref, bt_ref in dirs:
                g = g_ref[rows, :]
                b_ref[rows, :] = _chunk_cumsum(g, rev)
                bt_ref[n] = _colsum(g)
            return carry

        lax.fori_loop(0, n_chunks, decays, 0, unroll=GLA_UNROLL)
        for hh in range(GLA_PAIR):
            m = _head_lanes(hh)
            vl = slice(hh * dv, (hh + 1) * dv)

            def prepare(n, carry, m=m, vl=vl):
                rows = _gla_rows(n)
                kk = k_ref[rows, :]
                vb = v_ref[rows, vl].astype(BF16)
                for rev, _, s_ref, b_ref, bt_ref in dirs:
                    ke = jnp.where(m, kk * jnp.exp(bt_ref[n] - b_ref[rows, :]), 0.0).astype(BF16)
                    s_ref[n] = _dot(vb, ke, 0, 0)
                return carry

            lax.fori_loop(0, n_chunks, prepare, 0, unroll=GLA_UNROLL)
            for rev, _, s_ref, _, bt_ref in dirs:
                _gla_scan(s_ref, bt_ref, st_ref, n_chunks, descending=rev)

            def emit(n, carry, m=m, vl=vl):
                rows = _gla_rows(n)
                qs = q_ref[rows, :] * scale
                kk = k_ref[rows, :]
                vb = v_ref[rows, vl].astype(BF16)
                o = None
                for (rev, _, s_ref, b_ref, _), (_, mask) in zip(dirs, masks):
                    b = b_ref[rows, :]
                    qd = jnp.where(m, qs * jnp.exp(b), 0.0).astype(BF16)
                    ki = jnp.where(m, kk * jnp.exp(-b), 0.0).astype(BF16)
                    a = jnp.where(mask, _dot(qd, ki, 1, 1), 0.0).astype(BF16)
                    od = _dot(a, vb, 1, 0) + _dot(qd, s_ref[n].astype(BF16), 1, 1)
                    o = od if o is None else o + od
                o_ref[rows, vl] = o
                return carry

            lax.fori_loop(0, n_chunks, emit, 0, unroll=GLA_UNROLL)

    n_pairs, blk_k, blk_gb, blk_v = _gla_specs(s)
    state = pltpu.VMEM((n_chunks, dv, pw), F32)
    scratch = [state, state, pltpu.VMEM((s, pw), F32), pltpu.VMEM((s, pw), F32), pltpu.VMEM((n_chunks, 1, pw), F32),
               pltpu.VMEM((n_chunks, 1, pw), F32), pltpu.VMEM((dv, pw), F32)]
    return pl.pallas_call(
        body, name="gla_fwd", grid=(n_pairs,), in_specs=[blk_k, blk_k, blk_v, blk_k, blk_gb],
        out_specs=blk_v, out_shape=_sds(v.shape), scratch_shapes=scratch,
        compiler_params=_params(("parallel",)),
    )(q, k, v, la, la)


def _gla_bwd(q, k, v, la, do):
    s = q.shape[0]
    dk, dv = GLA_DK, GLA_DV
    pw = GLA_PAIR * dk
    c = GLA_CHUNK
    n_chunks = s // c
    scale = GLA_DK ** -0.5

    def body(q_ref, k_ref, v_ref, gf_ref, gb_ref, do_ref, dq_ref, dk_ref, dv_ref, dgf_ref, dgb_ref,
             sf_ref, sb_ref, bf_ref, bb_ref, btf_ref, btb_ref, dsf_ref, dsb_ref, st_ref):
        masks = [_gla_masks(rev) for rev in (False, True)]
        rowc = lax.broadcasted_iota(jnp.int32, (c, pw), 0)
        dirs = ((False, gf_ref, sf_ref, bf_ref, btf_ref, dsf_ref, dgf_ref),
                (True, gb_ref, sb_ref, bb_ref, btb_ref, dsb_ref, dgb_ref))

        def decays(n, carry):
            rows = _gla_rows(n)
            for rev, g_ref, _, b_ref, bt_ref, _, _ in dirs:
                g = g_ref[rows, :]
                b_ref[rows, :] = _chunk_cumsum(g, rev)
                bt_ref[n] = _colsum(g)
            return carry

        lax.fori_loop(0, n_chunks, decays, 0, unroll=GLA_UNROLL)
        for hh in range(GLA_PAIR):
            m = _head_lanes(hh)
            vl = slice(hh * dv, (hh + 1) * dv)

            def prepare(n, carry, m=m, vl=vl):
                rows = _gla_rows(n)
                qs = q_ref[rows, :] * scale
                kk = k_ref[rows, :]
                vb = v_ref[rows, vl].astype(BF16)
                do_b = do_ref[rows, vl].astype(BF16)
                for rev, _, s_ref, b_ref, bt_ref, ds_ref, _ in dirs:
                    b = b_ref[rows, :]
                    ke = jnp.where(m, kk * jnp.exp(bt_ref[n] - b), 0.0).astype(BF16)
                    qd = jnp.where(m, qs * jnp.exp(b), 0.0).astype(BF16)
                    s_ref[n] = _dot(vb, ke, 0, 0)
                    ds_ref[n] = _dot(do_b, qd, 0, 0)
                return carry

            lax.fori_loop(0, n_chunks, prepare, 0, unroll=GLA_UNROLL)
            for rev, _, s_ref, _, bt_ref, ds_ref, _ in dirs:
                _gla_scan(s_ref, bt_ref, st_ref, n_chunks, descending=rev)
                _gla_scan(ds_ref, bt_ref, st_ref, n_chunks, descending=not rev)

            def emit(n, carry, m=m, vl=vl, first=(hh == 0)):
                rows = _gla_rows(n)
                qs = q_ref[rows, :] * scale
                kk = k_ref[rows, :]
                vb = v_ref[rows, vl].astype(BF16)
                do_b = do_ref[rows, vl].astype(BF16)
                dq = dkk = dvv = None
                for (rev, _, s_ref, b_ref, bt_ref, ds_ref, dg_ref), (_, mask) in zip(dirs, masks):
                    b = b_ref[rows, :]
                    bt = bt_ref[n]
                    eb = jnp.where(m, jnp.exp(b), 0.0)
                    enb = jnp.where(m, jnp.exp(-b), 0.0)
                    etb = jnp.where(m, jnp.exp(bt - b), 0.0)
                    ebt = jnp.exp(bt)
                    qd = qs * eb
                    ki = kk * enb
                    ke = kk * etb
                    qd_b, ki_b, ke_b = qd.astype(BF16), ki.astype(BF16), ke.astype(BF16)
                    st = s_ref[n]
                    dst = ds_ref[n]
                    dst_b = dst.astype(BF16)
                    a = jnp.where(mask, _dot(qd_b, ki_b, 1, 1), 0.0).astype(BF16)
                    da = jnp.where(mask, _dot(do_b, vb, 1, 1), 0.0).astype(BF16)
                    dv_d = _dot(a, do_b, 0, 0) + _dot(ke_b, dst_b, 1, 1)
                    dqd = _dot(da, ki_b, 1, 0) + _dot(do_b, st.astype(BF16), 1, 0)
                    dki = _dot(da, qd_b, 0, 0)
                    dke = _dot(vb, dst_b, 1, 0)
                    dbt = _colsum(st * dst) * ebt + _colsum(dke * ke)
                    db = dqd * qd - dki * ki - dke * ke
                    db = db + jnp.where(rowc == (0 if rev else c - 1), dbt, 0.0)
                    dg = _chunk_cumsum(db, not rev)
                    if first:
                        dg_ref[rows, :] = dg
                    else:
                        dg_ref[rows, :] += dg
                    dq_d = dqd * eb * scale
                    dk_d = dki * enb + dke * etb
                    dq = dq_d if dq is None else dq + dq_d
                    dkk = dk_d if dkk is None else dkk + dk_d
                    dvv = dv_d if dvv is None else dvv + dv_d
                if first:
                    dq_ref[rows, :] = dq
                    dk_ref[rows, :] = dkk
                else:
                    dq_ref[rows, :] += dq
                    dk_ref[rows, :] += dkk
                dv_ref[rows, vl] = dvv
                return carry

            lax.fori_loop(0, n_chunks, emit, 0, unroll=GLA_UNROLL)

    n_pairs, blk_k, blk_gb, blk_v = _gla_specs(s)
    vk, vv = _sds(q.shape), _sds(v.shape)
    state = pltpu.VMEM((n_chunks, dv, pw), F32)
    scratch = [state, state, pltpu.VMEM((s, pw), F32), pltpu.VMEM((s, pw), F32), pltpu.VMEM((n_chunks, 1, pw), F32),
               pltpu.VMEM((n_chunks, 1, pw), F32), state, state, pltpu.VMEM((dv, pw), F32)]
    return pl.pallas_call(
        body, name="gla_bwd", grid=(n_pairs,), in_specs=[blk_k, blk_k, blk_v, blk_k, blk_gb, blk_v],
        out_specs=[blk_k, blk_k, blk_v, blk_k, blk_k], out_shape=[vk, vk, vv, vk, vk],
        scratch_shapes=scratch, compiler_params=_params(("parallel",)),
    )(q, k, v, la, la, do)


@jax.custom_vjp
def gla(q, k, v, la):
    return _gla_fwd(q, k, v, la)


def _gla_f(q, k, v, la):
    return _gla_fwd(q, k, v, la), (q, k, v, la)


def _gla_b(res, do):
    dq, dk, dv, dgf, dgb = _gla_bwd(*res, do)
    return dq, dk, dv, jnp.concatenate([dgf, dgb], axis=1)


gla.defvjp(_gla_f, _gla_b)


ATTN_TQ = 256
ATTN_TQ_BWD = 1024
HEAD_LANES = 128


def _attn_blocks(s, tq):
    per_q = pl.BlockSpec((tq, HEAD_LANES), lambda h, j: (j, h))
    k_nope = pl.BlockSpec((s, HEAD_LANES), lambda h, j: (0, 2 * h))
    v_blk = pl.BlockSpec((s, HEAD_LANES), lambda h, j: (0, 2 * h + 1))
    k_rope = pl.BlockSpec((s, HEAD_LANES), lambda h, j: (0, 0))
    lse = pl.BlockSpec((1, tq, 1), lambda h, j: (h, j, 0))
    return per_q, k_nope, v_blk, k_rope, lse


def _attn_fwd(qn, qr, kv, kr):
    s = qn.shape[0]
    tq = min(ATTN_TQ, s)
    scale = (MLA_NOPE + MLA_ROPE) ** -0.5

    def body(qn_ref, qr_ref, kn_ref, v_ref, kr_ref, o_ref, lse_ref):
        q = jnp.concatenate([qn_ref[...], qr_ref[...]], axis=1)
        k = jnp.concatenate([kn_ref[...], kr_ref[...]], axis=1)
        sc = _dot(q, k, 1, 1) * scale
        m = jnp.max(sc, axis=-1, keepdims=True)
        p = jnp.exp(sc - m)
        l = jnp.sum(p, axis=-1, keepdims=True)
        p = p * (1.0 / l)
        o_ref[...] = _dot(p.astype(BF16), v_ref[...], 1, 0)
        lse_ref[0] = m + jnp.log(l)

    per_q, k_nope, v_blk, k_rope, lse = _attn_blocks(s, tq)
    return pl.pallas_call(
        body, name="attn_fwd", grid=(MLA_HEADS, s // tq), in_specs=[per_q, per_q, k_nope, v_blk, k_rope],
        out_specs=[per_q, lse], out_shape=[_sds(qn.shape), _sds((MLA_HEADS, s, 1))],
        compiler_params=_params(("parallel", "parallel")),
    )(qn, qr, kv, kv, kr)


def _attn_bwd(qn, qr, kv, kr, o, lse, do):
    s = qn.shape[0]
    tq = min(ATTN_TQ_BWD, s)
    n_q = s // tq
    scale = (MLA_NOPE + MLA_ROPE) ** -0.5

    def body(qn_ref, qr_ref, kn_ref, v_ref, kr_ref, o_ref, lse_ref, do_ref, dqn_ref, dqr_ref, dkv_ref, dkr_ref,
             dk_acc, dv_acc, dkr_acc):
        h, j = pl.program_id(0), pl.program_id(1)
        q = jnp.concatenate([qn_ref[...], qr_ref[...]], axis=1)
        k = jnp.concatenate([kn_ref[...], kr_ref[...]], axis=1)
        do = do_ref[...]
        do_b = do.astype(BF16)
        p = jnp.exp(_dot(q, k, 1, 1) * scale - lse_ref[0])
        dp = _dot(do_b, v_ref[...], 1, 1)
        delta = jnp.sum(do * o_ref[...], axis=-1, keepdims=True)
        ds = (p * (dp - delta) * scale).astype(BF16)
        dq = _dot(ds, k, 1, 0)
        dqn_ref[...] = dq[:, :HEAD_LANES].astype(BF16)
        dqr_ref[...] = dq[:, HEAD_LANES:].astype(BF16)
        dk = _dot(ds, q, 0, 0)
        _acc(j, dk_acc, dk[:, :HEAD_LANES])
        _acc(j, dv_acc, _dot(p.astype(BF16), do_b, 0, 0))
        _acc(jnp.where(jnp.logical_and(h == 0, j == 0), 0, 1), dkr_acc, dk[:, HEAD_LANES:])

        @pl.when(j == n_q - 1)
        def _():
            dkv_ref[:, 0:HEAD_LANES] = dk_acc[...].astype(BF16)
            dkv_ref[:, HEAD_LANES:2 * HEAD_LANES] = dv_acc[...].astype(BF16)

        @pl.when(jnp.logical_and(h == MLA_HEADS - 1, j == n_q - 1))
        def _():
            dkr_ref[...] = dkr_acc[...].astype(BF16)

    per_q, k_nope, v_blk, k_rope, lse_blk = _attn_blocks(s, tq)
    dkv_blk = pl.BlockSpec((s, 2 * HEAD_LANES), lambda h, j: (0, h))
    acc = pltpu.VMEM((s, HEAD_LANES), F32)
    return pl.pallas_call(
        body, name="attn_bwd", grid=(MLA_HEADS, n_q),
        in_specs=[per_q, per_q, k_nope, v_blk, k_rope, per_q, lse_blk, per_q],
        out_specs=[per_q, per_q, dkv_blk, k_rope],
        out_shape=[_sds(qn.shape, BF16), _sds(qr.shape, BF16), _sds(kv.shape, BF16), _sds(kr.shape, BF16)],
        scratch_shapes=[acc, acc, acc], compiler_params=_params(("arbitrary", "arbitrary")),
    )(qn, qr, kv, kv, kr, o, lse, do)


@jax.custom_vjp
def attn(qn, qr, kv, kr):
    return _attn_fwd(qn, qr, kv, kr)[0]


def _attn_f(qn, qr, kv, kr):
    o, lse = _attn_fwd(qn, qr, kv, kr)
    return o, (qn, qr, kv, kr, o, lse)


def _attn_b(res, do):
    return tuple(_attn_bwd(*res, do))


attn.defvjp(_attn_f, _attn_b)


@jax.custom_vjp
def split_proj(proj):
    out, at = [], 0
    for _, _, _, wp in PROJ_SEGS:
        out.append(proj[:, at:at + wp])
        at += wp
    return tuple(out)


def _split_f(proj):
    return split_proj(proj), None


def _split_b(_, gs):
    return (jnp.concatenate(gs, axis=1),)


split_proj.defvjp(_split_f, _split_b)


def _tile2d(rows, width, limit=BLOCK_BYTES):
    fits = [t for t in range(16, rows + 1, 16) if rows % t == 0 and t * width * 4 <= limit]
    if fits and (fits[-1] >= 64 or fits[-1] == rows):
        return fits[-1], width
    if rows * width * 4 <= limit:
        return rows, width
    cols = [t for t in range(128, width + 1, 128) if width % t == 0 and rows * t * 4 <= limit]
    return (rows, cols[-1]) if cols else (rows, width)


def _add_pair(stacked, theirs, c_idx):
    g, r, w = theirs.shape
    tr, tc = _tile2d(r, w)

    def body(c_ref, a_ref, b_ref, o_ref):
        o_ref[0] = (a_ref[0, 0].astype(F32) + b_ref[0].astype(F32)).astype(BF16)

    blk = pl.BlockSpec((1, tr, tc), lambda k, i, j, c: (k, i, j))
    spec = pltpu.PrefetchScalarGridSpec(
        num_scalar_prefetch=1, grid=(g, r // tr, w // tc),
        in_specs=[pl.BlockSpec((1, 1, tr, tc), lambda k, i, j, c: (c[0], k, i, j)), blk], out_specs=blk)
    return pl.pallas_call(body, name="add_pair", grid_spec=spec, out_shape=_sds(theirs.shape, BF16),
                          compiler_params=_params(("parallel", "parallel", "parallel")))(c_idx, stacked, theirs)


def _add_chips(pair, landed, chip_idx):
    _, r, w = pair.shape
    tr, tc = _tile2d(r, w)

    def body(c_ref, p_ref, l0_ref, l1_ref, l2_ref, o_ref):
        o_ref[...] = ((p_ref[0].astype(F32) + l0_ref[0].astype(F32)) + l1_ref[0].astype(F32)) + l2_ref[0].astype(F32)

    specs = [pl.BlockSpec((1, tr, tc), lambda i, j, c: (c[0], i, j))]
    specs += [pl.BlockSpec((1, tr, tc), functools.partial(lambda i, j, c, k: (k, i, j), k=k)) for k in range(N_CHIPS - 1)]
    spec = pltpu.PrefetchScalarGridSpec(num_scalar_prefetch=1, grid=(r // tr, w // tc), in_specs=specs,
                                        out_specs=pl.BlockSpec((tr, tc), lambda i, j, c: (i, j)))
    return pl.pallas_call(body, name="add_chips", grid_spec=spec, out_shape=_sds((r, w)),
                          compiler_params=_params(("parallel", "parallel")))(chip_idx, pair, landed, landed, landed)


def _sum_devices(g):
    n = g.shape[2]

    def body(g_ref, o_ref, done_ref):
        t = g_ref[0]
        for j in range(1, N_DEV):
            t = t + g_ref[j]
        o_ref[...] = t
        done_ref[...] = jnp.zeros_like(done_ref)

    return pl.pallas_call(body, name="sum_devices", out_shape=[_sds((1, n)), _sds((8, 128))],
                          compiler_params=_params())(g)


def _adamw_math(w, gv, m, v):
    c1 = 1.0 - ADAM_B1 ** ADAM_STEP
    c2 = 1.0 - ADAM_B2 ** ADAM_STEP
    mn = ADAM_B1 * m + (1.0 - ADAM_B1) * gv
    vn = ADAM_B2 * v + (1.0 - ADAM_B2) * (gv * gv)
    return -ADAM_LR * ((mn / c1) / (jnp.sqrt(vn / c2) + ADAM_EPS) + ADAM_WD * w), mn, vn


def _adamw(w, g, m, v):
    shp = w.shape
    shp3 = (1, 1, shp[0]) if len(shp) == 1 else (-1,) + tuple(shp[-2:])
    w3, g3, m3, v3 = (t.reshape(shp3) for t in (w, g, m, v))

    def body(w_ref, g_ref, m_ref, v_ref, d_ref, mo_ref, vo_ref):
        d_ref[...], mo_ref[...], vo_ref[...] = _adamw_math(w_ref[...], g_ref[...], m_ref[...], v_ref[...])

    nl, r, wd = w3.shape
    tr, tc = _tile2d(r, wd, BLOCK_BYTES // 2)
    blk = pl.BlockSpec((1, tr, tc), lambda l, i, j: (l, i, j))
    s3 = _sds(w3.shape)
    d, mn, vn = pl.pallas_call(
        body, name="adamw", grid=(nl, r // tr, wd // tc), in_specs=[blk] * 4, out_specs=[blk] * 3,
        out_shape=[s3, s3, s3], compiler_params=_params(("parallel", "parallel", "parallel")),
    )(w3, g3, m3, v3)
    return d.reshape(shp), mn.reshape(shp), vn.reshape(shp)


PIECE_BYTES = 1 << 20


def _place():
    return lax.axis_index("x"), lax.axis_index("y"), lax.axis_index("c")


def _pieces(shape, itemsize):
    if len(shape) >= 3:
        return [(i,) + p for i in range(shape[0]) for p in _pieces(shape[1:], itemsize)]
    rows = shape[0]
    row_bytes = itemsize
    for dsz in shape[1:]:
        row_bytes *= dsz
    k = 1
    while rows % (2 * k) == 0 and (rows // (2 * k)) % 16 == 0 and (rows // k) * row_bytes > PIECE_BYTES:
        k *= 2
    step = rows // k
    return [(pl.ds(j * step, step),) for j in range(k)]


def _split_start(make, src, dst, pieces):
    for p in pieces:
        make(src.at[p], dst.at[p]).start()
    return make(src, dst)


def _comm_call(body, name, arrs, out_shapes, n_remote, n_local):
    return pl.pallas_call(
        body, name=name, in_specs=[ANY] * len(arrs), out_specs=[ANY] * len(out_shapes), out_shape=out_shapes,
        scratch_shapes=[pltpu.SemaphoreType.DMA((n_remote,)), pltpu.SemaphoreType.DMA((n_remote,)),
                        pltpu.SemaphoreType.DMA((n_local,))],
    )(*arrs)


def all_gather8(arrs, name):
    n = len(arrs)
    pieces = [_pieces(a.shape, a.dtype.itemsize) for a in arrs]

    def body(*refs):
        ins, outs = refs[:n], refs[n:2 * n]
        send, recv, _ = refs[2 * n:]
        x, y, c = _place()
        me, sib = (x, y, c), (x, y, 1 - c)
        chips = [(1 - x, y), (x, 1 - y), (1 - x, 1 - y)]

        def slot(p):
            return 4 * p[0] + 2 * p[1] + p[2]

        def maker(t, k, to):
            def make(s, d):
                return pltpu.make_async_remote_copy(src_ref=s, dst_ref=d, send_sem=send.at[7 * t + k],
                                                    recv_sem=recv.at[7 * t + k], device_id=to, device_id_type=MESH)
            return make

        def landing(t, k, block):
            dst = outs[t].at[slot(block)]
            return maker(t, k, me)(dst, dst)

        sent = []
        for t in range(n):
            dst = outs[t].at[slot(me)]
            sent.append(_split_start(maker(t, 0, sib), ins[t], dst, pieces[t]))
            for j, chip in enumerate(chips):
                sent.append(_split_start(maker(t, 1 + j, (*chip, c)), ins[t], dst, pieces[t]))
        for j, chip in enumerate(chips):
            for t in range(n):
                landing(t, 1 + j, (*chip, c)).wait_recv()
                blk = outs[t].at[slot((*chip, c))]
                sent.append(_split_start(maker(t, 4 + j, sib), blk, blk, pieces[t]))
        for t in range(n):
            landing(t, 0, sib).wait_recv()
            for j, chip in enumerate(chips):
                landing(t, 4 + j, (*chip, 1 - c)).wait_recv()
        for cp in sent:
            cp.wait_send()

    outs = [_sds((N_DEV,) + a.shape, a.dtype) for a in arrs]
    got = _comm_call(body, name, arrs, outs, 7 * n, 1)
    x, y, c = _place()
    return [lax.dynamic_update_index_in_dim(g, a, 4 * x + 2 * y + c, 0) for g, a in zip(got, arrs)]


def sibling_send(arrs, name):
    n = len(arrs)
    pieces = [_pieces(a.shape[1:], a.dtype.itemsize) for a in arrs]

    def body(*refs):
        ins, theirs = refs[:n], refs[n:2 * n]
        send, recv, _ = refs[2 * n:]
        x, y, c = _place()
        rem = []
        for t in range(n):
            def make(s, d, t=t):
                return pltpu.make_async_remote_copy(src_ref=s, dst_ref=d, send_sem=send.at[t], recv_sem=recv.at[t],
                                                    device_id=(x, y, 1 - c), device_id_type=MESH)
            rem.append(_split_start(make, ins[t].at[1 - c], theirs[t], pieces[t]))
        for cp in rem:
            cp.wait_recv()
        for cp in rem:
            cp.wait_send()

    outs = [_sds(a.shape[1:], a.dtype) for a in arrs]
    return _comm_call(body, name, arrs, outs, n, 1)


def exchange_chips(arrs, name):
    n = len(arrs)
    pieces = [_pieces(a.shape[1:], a.dtype.itemsize) for a in arrs]

    def body(*refs):
        ins, outs = refs[:n], refs[n:2 * n]
        send, recv, _ = refs[2 * n:]
        x, y, c = _place()
        peers = [(1 - x, y), (x, 1 - y), (1 - x, 1 - y)]
        rem = []
        for t in range(n):
            for j, (px, py) in enumerate(peers):
                def make(s, d, t=t, j=j, px=px, py=py):
                    return pltpu.make_async_remote_copy(
                        src_ref=s, dst_ref=d, send_sem=send.at[3 * t + j], recv_sem=recv.at[3 * t + j],
                        device_id=(px, py, c), device_id_type=MESH)
                rem.append(_split_start(make, ins[t].at[2 * px + py], outs[t].at[j], pieces[t]))
        for cp in rem:
            cp.wait_recv()
        for cp in rem:
            cp.wait_send()

    outs = [_sds((N_CHIPS - 1,) + a.shape[1:], a.dtype) for a in arrs]
    return _comm_call(body, name, arrs, outs, 3 * n, 1)


def sibling_swap(arrs, name):
    n = len(arrs)
    pieces = [_pieces(a.shape, a.dtype.itemsize) for a in arrs]

    def body(*refs):
        ins, outs = refs[:n], refs[n:2 * n]
        send, recv, _ = refs[2 * n:]
        x, y, c = _place()
        rem = []
        for t in range(n):
            def make(s, d, t=t):
                return pltpu.make_async_remote_copy(src_ref=s, dst_ref=d, send_sem=send.at[t], recv_sem=recv.at[t],
                                                    device_id=(x, y, 1 - c), device_id_type=MESH)
            rem.append(_split_start(make, ins[t], outs[t], pieces[t]))
        for cp in rem:
            cp.wait_recv()
        for cp in rem:
            cp.wait_send()

    outs = [_sds(a.shape, a.dtype) for a in arrs]
    return _comm_call(body, name, arrs, outs, n, 1)


def _peer_copies(srcs, lands, send, recv, mode):
    x, y, c = _place()
    my_chip = 2 * x + y
    out = []
    for t in range(len(srcs)):
        for j, (px, py) in enumerate([(1 - x, y), (x, 1 - y), (1 - x, 1 - y)]):
            if mode == "gather":
                s, dst = srcs[t], lands[t].at[c, my_chip]
            else:
                s, dst = srcs[t].at[2 * px + py], lands[t].at[j]
            out.append(pltpu.make_async_remote_copy(
                src_ref=s, dst_ref=dst, send_sem=send.at[3 * t + j], recv_sem=recv.at[3 * t + j],
                device_id=(px, py, c), device_id_type=MESH))
    return out


HBM = pl.BlockSpec(memory_space=pltpu.HBM)
SEM = pl.BlockSpec(memory_space=pltpu.SEMAPHORE)
EFFECT = pltpu.SideEffectType.DATAFLOW_SIDE_EFFECTING


def ici_start(srcs, lands, mode, name):
    n = len(srcs)

    def body(*refs):
        send, recv = refs[2 * n], refs[2 * n + 1]
        for cp in _peer_copies(refs[:n], refs[n:2 * n], send, recv, mode):
            cp.start()
        refs[-1][...] = jnp.zeros_like(refs[-1])

    thru = [pltpu.HBM(a.shape, a.dtype) for a in list(srcs) + list(lands)]
    outs = pl.pallas_call(
        body, name=name, in_specs=[HBM] * (2 * n), out_specs=[SEM, SEM] + [HBM] * (2 * n) + [pl.BlockSpec(memory_space=pltpu.VMEM)],
        out_shape=[pltpu.SemaphoreType.DMA((3 * n,)), pltpu.SemaphoreType.DMA((3 * n,))] + thru + [_sds((8, 128))],
        input_output_aliases={i: 2 + i for i in range(2 * n)},
        compiler_params=pltpu.CompilerParams(has_side_effects=EFFECT),
    )(*[pltpu.with_memory_space_constraint(a, pltpu.HBM) for a in list(srcs) + list(lands)])
    return dict(send=outs[0], recv=outs[1], srcs=outs[2:2 + n], lands=outs[2 + n:2 + 2 * n], token=outs[-1])


def ici_wait(handle, after, mode, name):
    n = len(handle["srcs"])

    def body(*refs):
        send, recv = refs[2 * n], refs[2 * n + 1]
        for cp in _peer_copies(refs[:n], refs[n:2 * n], send, recv, mode):
            cp.wait_send()
            cp.wait_recv()

    arrs = list(handle["srcs"]) + list(handle["lands"])
    outs = pl.pallas_call(
        body, name=name, in_specs=[HBM] * (2 * n) + [SEM, SEM, ANY], out_specs=[HBM] * (2 * n),
        out_shape=[pltpu.HBM(a.shape, a.dtype) for a in arrs], input_output_aliases={i: i for i in range(2 * n)},
        compiler_params=pltpu.CompilerParams(has_side_effects=EFFECT),
    )(*arrs, handle["send"], handle["recv"], after)
    return outs[:n], outs[n:]


def gather_share(blocks, lands, name):
    n = len(blocks)

    def body(*refs):
        own, buf = refs[:n], refs[2 * n:3 * n]
        done, send, recv = refs[3 * n:]
        x, y, c = _place()
        my_chip = 2 * x + y
        chips = [2 * (1 - x) + y, 2 * x + (1 - y), 2 * (1 - x) + (1 - y)]
        sent = []
        for t in range(n):
            def make(s, d, k, t=t):
                return pltpu.make_async_remote_copy(src_ref=s, dst_ref=d, send_sem=send.at[4 * t + k],
                                                    recv_sem=recv.at[4 * t + k], device_id=(x, y, 1 - c),
                                                    device_id_type=MESH)
            cp = make(own[t], buf[t].at[c, my_chip], 0)
            cp.start()
            sent.append(cp)
            for k, pc in enumerate(chips):
                cp = make(buf[t].at[c, pc], buf[t].at[c, pc], 1 + k)
                cp.start()
                sent.append(cp)
        for t in range(n):
            for k in range(4):
                got = buf[t].at[1 - c, k]
                pltpu.make_async_remote_copy(src_ref=got, dst_ref=got, send_sem=send.at[4 * t + k],
                                             recv_sem=recv.at[4 * t + k], device_id=(x, y, 1 - c),
                                             device_id_type=MESH).wait_recv()
        for cp in sent:
            cp.wait_send()
        done[...] = jnp.zeros_like(done)

    outs = pl.pallas_call(
        body, name=name, in_specs=[ANY] * (2 * n), out_specs=[ANY] * n + [pl.BlockSpec(memory_space=pltpu.VMEM)],
        out_shape=[_sds(a.shape, a.dtype) for a in lands] + [_sds((8, 128))],
        input_output_aliases={n + t: t for t in range(n)},
        scratch_shapes=[pltpu.SemaphoreType.DMA((4 * n,)), pltpu.SemaphoreType.DMA((4 * n,))],
    )(*blocks, *lands)
    return outs[:n], outs[n]


@jax.custom_vjp
def _build_w_in(w4):
    full = w4.reshape(-1, w4.shape[-1])
    parts = []
    for _, start, width, wp in PROJ_SEGS:
        if width:
            parts.append(full[start:start + width])
        if wp > width:
            parts.append(jnp.zeros((wp - width, full.shape[1]), full.dtype))
    return jnp.concatenate(parts, axis=0)


def _build_w_in_f(w4):
    return _build_w_in(w4), None


def _build_w_in_b(_, g):
    parts, at = [], 0
    for _, _, width, wp in PROJ_SEGS:
        if width:
            parts.append(g[at:at + width])
        at += wp
    return (jnp.concatenate(parts, axis=0).reshape(N_CHIPS, -1, g.shape[1]),)


_build_w_in.defvjp(_build_w_in_f, _build_w_in_b)


def _split_w_uq(w):
    w3 = w.reshape(w.shape[0], MLA_HEADS, MLA_NOPE + MLA_ROPE)
    return w3[:, :, :MLA_NOPE].reshape(w.shape[0], -1), w3[:, :, MLA_NOPE:].reshape(w.shape[0], -1)


def _swap_halves(t, width):
    t3 = t.reshape(t.shape[0], -1, 2, width // 2)
    return jnp.concatenate([t3[:, :, 1:], t3[:, :, :1]], axis=2).reshape(t.shape)


def _pad_heads(t, width):
    t3 = t.reshape(t.shape[0], -1, width)
    t3 = jnp.pad(t3, ((0, 0), (0, 0), (0, HEAD_LANES - width)))
    return t3.reshape(t.shape[0], -1).astype(BF16)


def _layer(xh, mod, big, small, rope_q, rope_k):
    d = D_MODEL
    shift, scale, gate = mod[None, 0:d], mod[None, d:2 * d], mod[None, 2 * d:3 * d]
    w_al = _build_w_in(big["w_in"])
    proj = mod_mm(xh, small["norm_g"][None], scale, shift, w_al)
    gq, gk, gv, glr, mq, mkv, mkr, cb, cc, cx, _, z = split_proj(proj)

    rk = GLA_RANK
    hk = GLA_HEADS * GLA_DK
    wg = jnp.zeros((128, 2 * hk), F32)
    wg = wg.at[0:rk, 0:hk].set(small["gla_wg_f"]).at[rk:2 * rk, hk:].set(small["gla_wg_b"])
    bg = jnp.concatenate([small["gla_bg_f"], small["gla_bg_b"]])[None]
    la = gate_act(mm(glr, wg), bg)
    o_gla = rmsnorm(gla(gq, gk, gv, la), small["gla_norm_g"][None])

    cq = rmsnorm(mq, small["mla_q_norm_g"][None])
    w_nope, w_rope = _split_w_uq(jnp.concatenate([big["w_uq"][j] for j in range(N_CHIPS)], axis=1))
    qn = mm16(cq, w_nope)
    qr = mm(cq, w_rope)
    qr = fma(qr, rope_q[0], _swap_halves(qr, MLA_ROPE), rope_q[1])
    ckv = rmsnorm(mkv, small["mla_kv_norm_g"][None])
    kv = mm16(ckv, jnp.concatenate([big["w_ukv"][j] for j in range(N_CHIPS)], axis=1))
    kr = mkr[:, :MLA_ROPE]
    kr = fma(kr, rope_k[0], _swap_halves(kr, MLA_ROPE), rope_k[1])
    o_mla = rmsnorm(attn(qn, _pad_heads(qr, MLA_ROPE), kv, _pad_heads(kr, MLA_ROPE)), small["mla_out_g"][None])

    cw = jnp.concatenate([small["conv_w"], jnp.zeros((5, CONV_CH), F32)], axis=0)
    o_conv = rmsnorm(conv_op(cb, cc, cx, cw), small["conv_out_g"][None])

    o = jnp.concatenate([o_gla, o_mla, o_conv], axis=1)
    w_out = big["w_out"].reshape(d, d)
    return out_block(o, z, w_out, xh, gate)


SMALL_REPL = ("norm_g", "gla_bg_f", "gla_bg_b", "gla_norm_g", "mla_q_norm_g", "mla_kv_norm_g", "mla_out_g",
              "conv_out_g")
SMALL_SHARDED = ("gla_wg_f", "gla_wg_b", "conv_w")
BIG = ("w_in", "w_out", "w_uq", "w_ukv")
HALF_AXIS = (1, 0, 0, 0)


def kernel(x, c, positions, ada_w, ada_b, norm_g, w_in, gla_wg_f, gla_bg_f, gla_wg_b, gla_bg_b, gla_norm_g, mla_q_norm_g, mla_kv_norm_g, mla_w_uq, mla_w_ukv, mla_out_g, conv_w, conv_out_g, w_out, final_g, loss_target, m_ada_w, m_ada_b, m_norm_g, m_w_in, m_gla_wg_f, m_gla_bg_f, m_gla_wg_b, m_gla_bg_b, m_gla_norm_g, m_mla_q_norm_g, m_mla_kv_norm_g, m_mla_w_uq, m_mla_w_ukv, m_mla_out_g, m_conv_w, m_conv_out_g, m_w_out, m_final_g, v_ada_w, v_ada_b, v_norm_g, v_w_in, v_gla_wg_f, v_gla_bg_f, v_gla_wg_b, v_gla_bg_b, v_gla_norm_g, v_mla_q_norm_g, v_mla_kv_norm_g, v_mla_w_uq, v_mla_w_ukv, v_mla_out_g, v_conv_w, v_conv_out_g, v_w_out, v_final_g):
    xi, yi, ci = _place()
    chip = 2 * xi + yi
    dev = 2 * chip + ci
    s = x.shape[1]
    d = D_MODEL
    weights = dict(ada_w=ada_w, ada_b=ada_b, norm_g=norm_g, w_in=w_in, gla_wg_f=gla_wg_f, gla_bg_f=gla_bg_f,
                   gla_wg_b=gla_wg_b, gla_bg_b=gla_bg_b, gla_norm_g=gla_norm_g, mla_q_norm_g=mla_q_norm_g,
                   mla_kv_norm_g=mla_kv_norm_g, mla_w_uq=mla_w_uq, mla_w_ukv=mla_w_ukv, mla_out_g=mla_out_g,
                   conv_w=conv_w, conv_out_g=conv_out_g, w_out=w_out, final_g=final_g)
    m_in = dict(ada_w=m_ada_w, ada_b=m_ada_b, norm_g=m_norm_g, w_in=m_w_in, gla_wg_f=m_gla_wg_f, gla_bg_f=m_gla_bg_f,
                gla_wg_b=m_gla_wg_b, gla_bg_b=m_gla_bg_b, gla_norm_g=m_gla_norm_g, mla_q_norm_g=m_mla_q_norm_g,
                mla_kv_norm_g=m_mla_kv_norm_g, mla_w_uq=m_mla_w_uq, mla_w_ukv=m_mla_w_ukv, mla_out_g=m_mla_out_g,
                conv_w=m_conv_w, conv_out_g=m_conv_out_g, w_out=m_w_out, final_g=m_final_g)
    v_in = dict(ada_w=v_ada_w, ada_b=v_ada_b, norm_g=v_norm_g, w_in=v_w_in, gla_wg_f=v_gla_wg_f, gla_bg_f=v_gla_bg_f,
                gla_wg_b=v_gla_wg_b, gla_bg_b=v_gla_bg_b, gla_norm_g=v_gla_norm_g, mla_q_norm_g=v_mla_q_norm_g,
                mla_kv_norm_g=v_mla_kv_norm_g, mla_w_uq=v_mla_w_uq, mla_w_ukv=v_mla_w_ukv, mla_out_g=v_mla_out_g,
                conv_w=v_conv_w, conv_out_g=v_conv_out_g, w_out=v_w_out, final_g=v_final_g)

    g_c, g_wgf, g_wgb, g_cw = all_gather8([c, gla_wg_f, gla_wg_b, conv_w], "gather_small")

    def unshard_cols(g):
        g4 = g[0::2]
        return g4.transpose(1, 2, 0, 3).reshape(g4.shape[1], g4.shape[2], -1)

    small_full = dict(gla_wg_f=unshard_cols(g_wgf), gla_wg_b=unshard_cols(g_wgb), conv_w=unshard_cols(g_cw))
    for nme in SMALL_REPL:
        small_full[nme] = weights[nme]
    smalls = [{nme: small_full[nme][l] for nme in SMALL_REPL + SMALL_SHARDED} for l in range(DEPTH)]

    big_src = (jnp.swapaxes(w_in, 1, 2), w_out, mla_w_uq, mla_w_ukv)

    def my_halves(l, zero=0):
        out = []
        for t, a in enumerate(big_src):
            n_half = a.shape[1 + HALF_AXIS[t]] // 2
            out.append(lax.dynamic_slice_in_dim(a[l], ci * n_half + zero, n_half, axis=HALF_AXIS[t]).astype(BF16))
        return out

    def landing(blocks):
        return [lax.empty((2, N_CHIPS) + b.shape, b.dtype) for b in blocks]

    def finish_gather(handle, after, tag):
        blocks, lands = ici_wait(handle, after, "gather", "gather_wait" + tag)
        lands, done = gather_share(blocks, lands, "gather_share" + tag)
        full = [lax.dynamic_update_slice(g, b[None, None], (ci, chip) + (0,) * b.ndim) for g, b in zip(lands, blocks)]
        return full, done

    halves0 = my_halves(0)
    started0 = ici_start(halves0, landing(halves0), "gather", "gather_start0")

    c_act = _silu_rows(g_c[:, 0, :])
    c_act16 = jnp.concatenate([c_act, jnp.zeros_like(c_act)], axis=0)
    n_ada = ada_w.shape[2]
    parts = []
    for l in range(DEPTH):
        bias = lax.dynamic_slice_in_dim(ada_b[l], chip * n_ada, n_ada)[None]
        parts.append(_mm(c_act16, ada_w[l], bias=bias, name="ada_fwd"))
    g_mod, = all_gather8([jnp.stack(parts)], "gather_mod")
    mod_mine = lax.dynamic_index_in_dim(g_mod[0::2], dev, 2, keepdims=False)
    mods = mod_mine.transpose(1, 0, 2).reshape(DEPTH, 3 * d)

    inv_freq = ROPE_THETA ** (-jnp.arange(0, MLA_ROPE, 2, dtype=F32) / MLA_ROPE)
    ang = positions[0].astype(F32)[:, None] * inv_freq
    cos, sin = jnp.cos(ang), jnp.sin(ang)
    rope_k = (jnp.concatenate([cos, cos], axis=1), jnp.concatenate([-sin, sin], axis=1))
    rope_q = (jnp.tile(rope_k[0], (1, MLA_HEADS)), jnp.tile(rope_k[1], (1, MLA_HEADS)))

    def run_layer(xh, mod, gathered, small):
        big = {nme: jnp.concatenate([g[0], g[1]], axis=HALF_AXIS[t] + 1) for t, (nme, g) in enumerate(zip(BIG, gathered))}
        return _layer(xh, mod, big, small, rope_q, rope_k)

    def head(hh, fg):
        return loss_op(rmsnorm(hh, fg[None]), loss_target[0])[0, 0]

    gathered0, done0 = finish_gather(started0, mods, "0")
    halves1 = my_halves(1, done0[0, 0].astype(jnp.int32))
    started1 = ici_start(halves1, landing(halves1), "gather", "gather_start1")
    h1, vjp0 = jax.vjp(run_layer, x[0], mods[0] + started1["token"][0, 0], gathered0, smalls[0])
    gathered1, _ = finish_gather(started1, h1, "1")
    h2, vjp1 = jax.vjp(run_layer, h1, mods[1], gathered1, smalls[1])
    loss_dev, vjp_head = jax.vjp(head, h2, final_g)
    dh2, dfinal = vjp_head(jnp.ones((), F32))

    c_idx = jnp.reshape(ci, (1,)).astype(jnp.int32)
    chip_idx = jnp.reshape(chip, (1,)).astype(jnp.int32)

    def reduce_begin(dgath, tag, zero=None):
        theirs = sibling_send(dgath, "reduce_sibling" + tag)
        pair = [_add_pair(a, b, c_idx) for a, b in zip(dgath, theirs)]
        shapes = [(N_CHIPS - 1,) + p.shape[1:] for p in pair]
        if zero is None:
            lands = [lax.empty(shp, BF16) for shp in shapes]
        else:
            lands = [jnp.broadcast_to(zero.astype(BF16), shp) for shp in shapes]
        return ici_start(pair, lands, "reduce", "reduce_start" + tag)

    def reduce_end(handle, after, tag):
        pair, landed = ici_wait(handle, after, "reduce", "reduce_wait" + tag)
        reduced = [_add_chips(p, q, chip_idx) for p, q in zip(pair, landed)]
        others = sibling_swap(reduced, "share_sibling" + tag)
        return [jnp.where(ci == 0, jnp.concatenate([own, other], axis=HALF_AXIS[t]),
                          jnp.concatenate([other, own], axis=HALF_AXIS[t]))
                for t, (own, other) in enumerate(zip(reduced, others))]

    dh1, dmod1, dgath1, dsmall1 = vjp1(dh2)
    reducing1 = reduce_begin(dgath1, "1")
    dx, dmod0, dgath0, dsmall0 = vjp0(dh1 + reducing1["token"][0, 0])
    dmods = jnp.stack([dmod0, dmod1])
    dsmalls = [dsmall0, dsmall1]

    pieces = [dmods.reshape(-1), dfinal]
    for nme in SMALL_REPL + SMALL_SHARDED:
        pieces.append(jnp.stack([dsmalls[l][nme] for l in range(DEPTH)]).reshape(-1))
    pieces.append(loss_dev.reshape(1))
    sizes = [p.shape[0] for p in pieces]
    flat = jnp.concatenate(pieces)
    padn = (-flat.shape[0]) % 128
    flat = jnp.concatenate([flat, jnp.zeros((padn,), F32)])[None]
    g_small, = all_gather8([flat], "gather_small_grads")
    total, small_done = _sum_devices(g_small)
    total = total[0]
    reducing0 = reduce_begin(dgath0, "0", small_done[0, 0])
    offs, at = [], 0
    for n_el in sizes:
        offs.append(at)
        at += n_el

    def piece(i, shape):
        return total[offs[i]:offs[i] + sizes[i]].reshape(shape)

    grads = {"ada_b": piece(0, (DEPTH, 3 * d)), "final_g": piece(1, (d,))}
    loss = piece(len(pieces) - 1, ())
    for i, nme in enumerate(SMALL_REPL + SMALL_SHARDED):
        full = piece(2 + i, small_full[nme].shape)
        if nme in SMALL_SHARDED:
            ncol = weights[nme].shape[2]
            full = lax.dynamic_slice_in_dim(full, chip * ncol, ncol, axis=2)
        grads[nme] = full

    dmod_all = g_small[:, 0, :DEPTH * 3 * d].reshape(N_DEV, DEPTH, 3 * d)
    dmod_cols = lax.dynamic_slice_in_dim(dmod_all, chip * n_ada, n_ada, axis=2)
    g_ada = []
    for l in range(DEPTH):
        dm16 = jnp.concatenate([dmod_cols[:, l], jnp.zeros((N_DEV, n_ada), F32)], axis=0)
        dm16 = dm16 + reducing0["token"][0, 0]
        g_ada.append(_mm(c_act16, dm16, ta=True, name="ada_bwd"))
    grads["ada_w"] = jnp.stack(g_ada)

    order = list(weights)
    big_names = ("w_in", "w_out", "mla_w_uq", "mla_w_ukv")
    delta, new_m, new_v = {}, {}, {}
    for nme in order:
        if nme not in big_names:
            delta[nme], new_m[nme], new_v[nme] = _adamw(weights[nme], grads[nme], m_in[nme], v_in[nme])

    def first(t):
        return t[(slice(0, 1),) * t.ndim].reshape(1)

    big_grads1 = reduce_end(reducing1, jnp.concatenate([first(dx), first(reducing0["token"])]), "1")
    done = [first(delta[nme]) for nme in order if nme not in big_names] + [first(g) for g in big_grads1]
    big_grads0 = reduce_end(reducing0, jnp.concatenate(done), "0")
    for nme, g0, g1 in zip(big_names, big_grads0, big_grads1):
        grads[nme] = jnp.stack([g0, g1])
    for nme in big_names:
        if nme == "w_in":
            w_t, m_t, v_t = (jnp.swapaxes(t, 1, 2) for t in (w_in, m_w_in, v_w_in))
            res = _adamw(w_t, grads[nme], m_t, v_t)
            delta[nme], new_m[nme], new_v[nme] = (jnp.swapaxes(t, 1, 2) for t in res)
            grads[nme] = jnp.swapaxes(grads[nme], 1, 2)
            continue
        delta[nme], new_m[nme], new_v[nme] = _adamw(weights[nme], grads[nme], m_in[nme], v_in[nme])
    return (loss, dx[None], *[grads[n_] for n_ in order], *[delta[n_] for n_ in order],
            *[new_m[n_] for n_ in order], *[new_v[n_] for n_ in order])
```

```python
import functools

import jax
import jax.numpy as jnp
from jax import lax
from jax.experimental import pallas as pl
from jax.experimental.pallas import tpu as pltpu

F32 = jnp.float32
BF16 = jnp.bfloat16
MESH = pl.DeviceIdType.MESH
HIGHEST = lax.Precision.HIGHEST

DEPTH = 2
D_MODEL = 2048
GLA_HEADS = 6
GLA_DK = 64
GLA_DV = 128
GLA_RANK = 16
GLA_TEMP = 16.0
GLA_CHUNK = 64
GLA_W = GLA_HEADS * GLA_DV
MLA_HEADS = 6
MLA_QL = 384
MLA_KVL = 256
MLA_NOPE = 128
MLA_ROPE = 64
MLA_DV = 128
MLA_W = MLA_HEADS * MLA_DV
CONV_CH = D_MODEL - GLA_W - MLA_W
ROPE_THETA = 10000.0
EPS = 1e-6
IN_DIM = 5856
N_CHIPS = 4
N_DEV = 8

ADAM_LR = 0.001
ADAM_B1 = 0.9
ADAM_B2 = 0.999
ADAM_EPS = 1e-08
ADAM_WD = 0.01
ADAM_STEP = 10

PROJ_SEGS = (
    ("gq", 0, 384, 384), ("gk", 384, 384, 384), ("gv", 768, 768, 768), ("glr", 1536, 32, 128),
    ("mq", 1568, 384, 384), ("mkv", 1952, 256, 256), ("mkr", 2208, 64, 128),
    ("cb", 2272, 512, 512), ("cc", 2784, 512, 512), ("cx", 3296, 512, 512),
    ("pad", 3808, 0, 128), ("z", 3808, 2048, 2048),
)
PROJ_AL = sum(s[3] for s in PROJ_SEGS)

ANY = pl.BlockSpec(memory_space=pl.ANY)
VMEM_LIMIT = 48 * 1024 * 1024
BLOCK_BYTES = 2 * 1024 * 1024


def _params(sem=None):
    return pltpu.CompilerParams(dimension_semantics=sem, vmem_limit_bytes=VMEM_LIMIT)


def _dot(a, b, ca, cb, precision=None):
    return lax.dot_general(a, b, (((ca,), (cb,)), ((), ())), preferred_element_type=F32, precision=precision)


def _tile(dim, prefs):
    for t in prefs:
        if dim % t == 0:
            return t
    return dim


def _pick_rows(rows, width, itemsize=4):
    for t in (2048, 1024, 512, 256, 128, 64, 32, 16, 8):
        if rows % t == 0 and t * width * itemsize <= BLOCK_BYTES:
            return t
    return rows


def _mm(a, b, *, ta=False, tb=False, bias=None, out_dtype=F32, name="mm"):
    if ta:
        K, M = a.shape
    else:
        M, K = a.shape
    if tb:
        N, Kb = b.shape
    else:
        Kb, N = b.shape
    assert K == Kb, (a.shape, b.shape, ta, tb)
    tm = _tile(M, (2048, 1024, 512, 256, 128))
    tn = _tile(N, (512, 384, 256, 128) if tm >= 2048 else (1024, 512, 384, 256, 128))
    tk = _tile(K, (2048, 1024, 512, 256, 128))
    nk = K // tk
    has_bias = bias is not None

    def body(*refs):
        a_ref, b_ref = refs[0], refs[1]
        bias_ref = refs[2] if has_bias else None
        o_ref = refs[3 if has_bias else 2]
        part = _dot(a_ref[...].astype(BF16), b_ref[...].astype(BF16), 0 if ta else 1, 1 if tb else 0)

        def finish(r):
            if has_bias:
                r = r + bias_ref[...]
            o_ref[...] = r.astype(out_dtype)

        if nk == 1:
            finish(part)
            return
        acc_ref = refs[-1]
        k = pl.program_id(2)

        @pl.when(k == 0)
        def _():
            acc_ref[...] = part

        @pl.when(k != 0)
        def _():
            acc_ref[...] += part

        @pl.when(k == nk - 1)
        def _():
            finish(acc_ref[...])

    a_spec = pl.BlockSpec((tk, tm), lambda i, j, k: (k, i)) if ta else pl.BlockSpec((tm, tk), lambda i, j, k: (i, k))
    b_spec = pl.BlockSpec((tn, tk), lambda i, j, k: (j, k)) if tb else pl.BlockSpec((tk, tn), lambda i, j, k: (k, j))
    in_specs = [a_spec, b_spec]
    args = [a, b]
    if has_bias:
        in_specs.append(pl.BlockSpec((1, tn), lambda i, j, k: (0, j)))
        args.append(bias)
    return pl.pallas_call(
        body, name=name, grid=(M // tm, N // tn, nk),
        in_specs=in_specs, out_specs=pl.BlockSpec((tm, tn), lambda i, j, k: (i, j)),
        out_shape=jax.ShapeDtypeStruct((M, N), out_dtype),
        scratch_shapes=[pltpu.VMEM((tm, tn), F32)] if nk > 1 else [],
        compiler_params=_params(("parallel", "parallel", "arbitrary")),
    )(*args)


@jax.custom_vjp
def mm(a, b):
    return _mm(a, b, name="mm_fwd")


def _mm_f(a, b):
    return _mm(a, b, name="mm_fwd"), (a, b)


def _mm_b(res, g):
    a, b = res
    return _mm(g, b, tb=True, out_dtype=a.dtype, name="mm_da"), _mm(a, g, ta=True, out_dtype=b.dtype, name="mm_db")


mm.defvjp(_mm_f, _mm_b)


@jax.custom_vjp
def mm16(a, b):
    return _mm(a, b, out_dtype=BF16, name="mm16_fwd")


def _mm16_f(a, b):
    return mm16(a, b), (a, b)


mm16.defvjp(_mm16_f, _mm_b)


def _rows(body, name, tiled, full, tiled_out, acc_out, tr=None):
    rows = tiled[0].shape[0]
    if tr is None:
        width = max([a.shape[1] for a in tiled] + [s.shape[1] for s in tiled_out])
        tr = _pick_rows(rows, width)
    in_specs = [pl.BlockSpec((tr, a.shape[1]), lambda i: (i, 0)) for a in tiled]
    in_specs += [pl.BlockSpec(a.shape, lambda i: (0, 0)) for a in full]
    out_specs = [pl.BlockSpec((tr, s.shape[1]), lambda i: (i, 0)) for s in tiled_out]
    out_specs += [pl.BlockSpec(s.shape, lambda i: (0, 0)) for s in acc_out]

    def wrapped(*refs):
        body(pl.program_id(0), *refs)

    outs = pl.pallas_call(
        wrapped, name=name, grid=(rows // tr,), in_specs=in_specs, out_specs=out_specs,
        out_shape=list(tiled_out) + list(acc_out),
        compiler_params=_params(("arbitrary",)),
    )(*tiled, *full)
    return outs


def _sds(shape, dtype=F32):
    return jax.ShapeDtypeStruct(tuple(shape), dtype)


def _acc(step, ref, val):
    @pl.when(step == 0)
    def _():
        ref[...] = val

    @pl.when(step != 0)
    def _():
        ref[...] += val


def _colsum(v):
    return jnp.sum(v, axis=0, keepdims=True)


def _rstd(x):
    return lax.rsqrt(jnp.mean(x * x, axis=-1, keepdims=True) + EPS)


def _norm_grid(x, g):
    rows, w = x.shape[0], g.shape[1]
    tr = _pick_rows(rows, w)
    blk = pl.BlockSpec((tr, w), lambda i, j: (i, j))
    gblk = pl.BlockSpec((1, w), lambda i, j: (0, 0))
    return (rows // tr, x.shape[1] // w), blk, gblk


@jax.custom_vjp
def rmsnorm(x, g):
    def body(x_ref, g_ref, o_ref):
        x = x_ref[...]
        o_ref[...] = x * _rstd(x) * g_ref[...]

    grid, blk, gblk = _norm_grid(x, g)
    return pl.pallas_call(body, name="rmsnorm_fwd", grid=grid, in_specs=[blk, gblk], out_specs=blk,
                          out_shape=_sds(x.shape), compiler_params=_params(("parallel", "parallel")))(x, g)


def _rmsnorm_f(x, g):
    return rmsnorm(x, g), (x, g)


def _rmsnorm_b(res, dy):
    x, g = res

    def body(x_ref, dy_ref, g_ref, dx_ref, dg_ref):
        x = x_ref[...]
        dy = dy_ref[...]
        r = _rstd(x)
        xh = x * r
        dxh = dy * g_ref[...]
        dx_ref[...] = r * (dxh - xh * jnp.mean(dxh * xh, axis=-1, keepdims=True))
        first = jnp.logical_and(pl.program_id(0) == 0, pl.program_id(1) == 0)
        _acc(jnp.where(first, 0, 1), dg_ref, _colsum(dy * xh))

    grid, blk, gblk = _norm_grid(x, g)
    dx, dg = pl.pallas_call(body, name="rmsnorm_bwd", grid=grid, in_specs=[blk, blk, gblk], out_specs=[blk, gblk],
                            out_shape=[_sds(x.shape), _sds(g.shape)],
                            compiler_params=_params(("arbitrary", "arbitrary")))(x, dy, g)
    return dx, dg


rmsnorm.defvjp(_rmsnorm_f, _rmsnorm_b)


def _modulate(x, g, scale, shift):
    def body(i, x_ref, g_ref, sc_ref, sh_ref, o_ref):
        x = x_ref[...]
        xn = x * _rstd(x) * g_ref[...]
        o_ref[...] = (xn * (1.0 + sc_ref[...]) + sh_ref[...]).astype(BF16)
    return _rows(body, "modulate_fwd", [x], [g, scale, shift], [_sds(x.shape, BF16)], [])[0]


def _modulate_bwd(x, g, scale, shift, dh):
    def body(i, x_ref, dh_ref, g_ref, sc_ref, dx_ref, dg_ref, dsc_ref, dsh_ref):
        x = x_ref[...]
        dh = dh_ref[...]
        gv = g_ref[...]
        r = _rstd(x)
        xh = x * r
        dxn = dh * (1.0 + sc_ref[...])
        dxh = dxn * gv
        dx_ref[...] = r * (dxh - xh * jnp.mean(dxh * xh, axis=-1, keepdims=True))
        _acc(i, dg_ref, _colsum(dxn * xh))
        _acc(i, dsc_ref, _colsum(dh * (xh * gv)))
        _acc(i, dsh_ref, _colsum(dh))

    v = _sds(g.shape)
    return _rows(body, "modulate_bwd", [x, dh], [g, scale], [_sds(x.shape)], [v, v, v])


@jax.custom_vjp
def mod_mm(x, g, scale, shift, wt):
    return _mm(_modulate(x, g, scale, shift), wt, tb=True, name="mm_in")


def _mod_mm_f(x, g, scale, shift, wt):
    h = _modulate(x, g, scale, shift)
    return _mm(h, wt, tb=True, name="mm_in"), (x, g, scale, shift, wt, h)


def _mod_mm_b(res, dproj):
    x, g, scale, shift, wt, h = res
    dproj = dproj.astype(BF16)
    dh = _mm(dproj, wt, name="mm_in_dh")
    dwt = _mm(dproj, h, ta=True, out_dtype=wt.dtype, name="mm_in_dw")
    dx, dg, dsc, dsh = _modulate_bwd(x, g, scale, shift, dh)
    return dx, dg, dsc, dsh, dwt


mod_mm.defvjp(_mod_mm_f, _mod_mm_b)


def _sigmoid(z):
    return 1.0 / (1.0 + jnp.exp(-z))


def _gate_mul(o, z):
    def body(i, o_ref, z_ref, y_ref):
        z = z_ref[...]
        y_ref[...] = (o_ref[...] * (z * _sigmoid(z))).astype(BF16)
    return _rows(body, "gate_mul_fwd", [o, z], [], [_sds(o.shape, BF16)], [])[0]


def _gate_mul_bwd(o, z, dy):
    def body(i, o_ref, z_ref, dy_ref, do_ref, dz_ref):
        z = z_ref[...]
        dy = dy_ref[...]
        s = _sigmoid(z)
        do_ref[...] = dy * (z * s)
        dz_ref[...] = dy * o_ref[...] * (s * (1.0 + z * (1.0 - s)))
    return _rows(body, "gate_mul_bwd", [o, z, dy], [], [_sds(o.shape), _sds(o.shape)], [])


def _residual(x, u, gate):
    def body(i, x_ref, u_ref, g_ref, o_ref):
        o_ref[...] = x_ref[...] + g_ref[...] * u_ref[...]
    return _rows(body, "residual_fwd", [x, u], [gate], [_sds(x.shape)], [])[0]


def _residual_bwd(d, u, gate):
    def body(i, d_ref, u_ref, g_ref, du_ref, dg_ref):
        d = d_ref[...]
        du_ref[...] = (g_ref[...] * d).astype(BF16)
        _acc(i, dg_ref, _colsum(d * u_ref[...]))

    return _rows(body, "residual_bwd", [d, u], [gate], [_sds(u.shape, BF16)], [_sds(gate.shape)])


@jax.custom_vjp
def out_block(o, z, w, x, gate):
    return _residual(x, _mm(_gate_mul(o, z), w, name="mm_out"), gate)


def _out_block_f(o, z, w, x, gate):
    y = _gate_mul(o, z)
    u = _mm(y, w, name="mm_out")
    return _residual(x, u, gate), (o, z, w, y, u, gate)


def _out_block_b(res, d):
    o, z, w, y, u, gate = res
    du, dgate = _residual_bwd(d, u, gate)
    dy = _mm(du, w, tb=True, name="mm_out_dy")
    dw = _mm(y, du, ta=True, out_dtype=w.dtype, name="mm_out_dw")
    do, dz = _gate_mul_bwd(o, z, dy)
    return do, dz, dw, d, dgate


out_block.defvjp(_out_block_f, _out_block_b)


@jax.custom_vjp
def gate_act(u, b):
    def body(i, u_ref, b_ref, o_ref):
        t = u_ref[...] + b_ref[...]
        o_ref[...] = (jnp.minimum(t, 0.0) - jnp.log(1.0 + jnp.exp(-jnp.abs(t)))) / GLA_TEMP
    return _rows(body, "gate_act_fwd", [u], [b], [_sds(u.shape)], [])[0]


def _gate_act_f(u, b):
    return gate_act(u, b), (u, b)


def _gate_act_b(res, d):
    u, b = res

    def body(i, u_ref, d_ref, b_ref, du_ref, db_ref):
        t = u_ref[...] + b_ref[...]
        du = d_ref[...] * _sigmoid(-t) / GLA_TEMP
        du_ref[...] = du
        _acc(i, db_ref, _colsum(du))

    du, db = _rows(body, "gate_act_bwd", [u, d], [b], [_sds(u.shape)], [_sds(b.shape)])
    return du, db


gate_act.defvjp(_gate_act_f, _gate_act_b)


@jax.custom_vjp
def fma(a, b, c, d):
    def body(i, a_ref, b_ref, c_ref, d_ref, o_ref):
        o_ref[...] = a_ref[...] * b_ref[...] + c_ref[...] * d_ref[...]
    return _rows(body, "fma_fwd", [a, b, c, d], [], [_sds(a.shape)], [])[0]


def _fma_f(a, b, c, d):
    return fma(a, b, c, d), (b, d)


def _fma_b(res, g):
    b, d = res

    def body(i, g_ref, b_ref, d_ref, da_ref, dc_ref):
        g = g_ref[...]
        da_ref[...] = g * b_ref[...]
        dc_ref[...] = g * d_ref[...]

    da, dc = _rows(body, "fma_bwd", [g, b, d], [], [_sds(g.shape), _sds(g.shape)], [])
    return da, jnp.zeros_like(b), dc, jnp.zeros_like(d)


fma.defvjp(_fma_f, _fma_b)


def _silu_rows(c):
    def body(i, c_ref, o_ref):
        v = c_ref[...]
        o_ref[...] = v * _sigmoid(v)
    return _rows(body, "silu", [c], [], [_sds(c.shape)], [])[0]


@jax.custom_vjp
def loss_op(y, t):
    return _loss_fwd(y, t)[0]


def _loss_fwd(y, t):
    inv = 1.0 / y.shape[1]

    def body(i, y_ref, t_ref, d_ref, l_ref):
        e = y_ref[...] - t_ref[...]
        d_ref[...] = e * inv
        _acc(i, l_ref, jnp.sum(_colsum(e * e), axis=1, keepdims=True) * (0.5 * inv))

    d, l = _rows(body, "loss_fwd", [y, t], [], [_sds(y.shape)], [_sds((1, 1))])
    return l, d


def _loss_f(y, t):
    l, d = _loss_fwd(y, t)
    return l, d


def _loss_b(d, g):
    return d * g, jnp.zeros_like(d)


loss_op.defvjp(_loss_f, _loss_b)


def _conv_terms(cc, cx, rows, n):
    u = cc * cx
    up = jnp.where(rows == 0, 0.0, pltpu.roll(u, 1, 0))
    un = jnp.where(rows == n - 1, 0.0, pltpu.roll(u, n - 1, 0))
    return u, up, un


CONV_COLS = 128


def _conv_specs(s, n_in):
    blk = pl.BlockSpec((s, CONV_COLS), lambda j: (0, j))
    wblk = pl.BlockSpec((8, CONV_COLS), lambda j: (0, j))
    return [blk] * n_in + [wblk], blk, wblk


@jax.custom_vjp
def conv_op(cb, cc, cx, w):
    s, ch = cb.shape

    def body(cb_ref, cc_ref, cx_ref, w_ref, o_ref):
        rows = lax.broadcasted_iota(jnp.int32, (s, CONV_COLS), 0)
        u, up, un = _conv_terms(cc_ref[...], cx_ref[...], rows, s)
        conv = up * w_ref[0:1, :] + u * w_ref[1:2, :] + un * w_ref[2:3, :]
        o_ref[...] = cb_ref[...] * conv

    in_specs, blk, _ = _conv_specs(s, 3)
    return pl.pallas_call(
        body, name="conv_fwd", grid=(ch // CONV_COLS,), in_specs=in_specs, out_specs=blk,
        out_shape=_sds(cb.shape), compiler_params=_params(("parallel",)),
    )(cb, cc, cx, w)


def _conv_f(cb, cc, cx, w):
    return conv_op(cb, cc, cx, w), (cb, cc, cx, w)


def _conv_b(res, d):
    cb, cc, cx, w = res
    s, ch = cb.shape

    def body(cb_ref, cc_ref, cx_ref, d_ref, w_ref, dcb_ref, dcc_ref, dcx_ref, dw_ref):
        rows = lax.broadcasted_iota(jnp.int32, (s, CONV_COLS), 0)
        cc_v = cc_ref[...]
        cx_v = cx_ref[...]
        u, up, un = _conv_terms(cc_v, cx_v, rows, s)
        w0, w1, w2 = w_ref[0:1, :], w_ref[1:2, :], w_ref[2:3, :]
        dv = d_ref[...]
        dcb_ref[...] = dv * (up * w0 + u * w1 + un * w2)
        dconv = dv * cb_ref[...]
        d_next = jnp.where(rows == s - 1, 0.0, pltpu.roll(dconv, s - 1, 0))
        d_prev = jnp.where(rows == 0, 0.0, pltpu.roll(dconv, 1, 0))
        du = w0 * d_next + w1 * dconv + w2 * d_prev
        dcc_ref[...] = du * cx_v
        dcx_ref[...] = du * cc_v
        dw_ref[...] = jnp.zeros_like(dw_ref)
        dw_ref[0:1, :] = _colsum(dconv * up)
        dw_ref[1:2, :] = _colsum(dconv * u)
        dw_ref[2:3, :] = _colsum(dconv * un)

    in_specs, blk, wblk = _conv_specs(s, 4)
    v = _sds(cb.shape)
    return tuple(pl.pallas_call(
        body, name="conv_bwd", grid=(ch // CONV_COLS,), in_specs=in_specs, out_specs=[blk, blk, blk, wblk],
        out_shape=[v, v, v, _sds(w.shape)], compiler_params=_params(("parallel",)),
    )(cb, cc, cx, d, w))


conv_op.defvjp(_conv_f, _conv_b)


def _gla_masks(rev):
    c = GLA_CHUNK
    row = lax.broadcasted_iota(jnp.int32, (c, c), 0)
    col = lax.broadcasted_iota(jnp.int32, (c, c), 1)
    mask = (row < col) if rev else (row >= col)
    return rev, mask


def _chunk_cumsum(g, rev):
    c = g.shape[0]
    row = lax.broadcasted_iota(jnp.int32, g.shape, 0)
    b = g
    s = 1
    while s < c:
        if rev:
            b = b + jnp.where(row < c - s, pltpu.roll(b, c - s, 0), 0.0)
        else:
            b = b + jnp.where(row >= s, pltpu.roll(b, s, 0), 0.0)
        s *= 2
    return b


GLA_UNROLL = 16


def _gla_rows(n):
    return pl.ds(pl.multiple_of(n * GLA_CHUNK, GLA_CHUNK), GLA_CHUNK)


def _gla_scan(s_ref, bt_ref, st_ref, n_chunks, descending):
    st_ref[...] = jnp.zeros_like(st_ref)

    def step(i, carry):
        n = (n_chunks - 1 - i) if descending else i
        own = s_ref[n]
        st = st_ref[...]
        s_ref[n] = st
        st_ref[...] = st * jnp.exp(bt_ref[n]) + own
        return carry

    lax.fori_loop(0, n_chunks, step, 0)


GLA_PAIR = 2


def _gla_specs(s):
    dk, dv = GLA_DK, GLA_DV
    n_pairs = GLA_HEADS // GLA_PAIR
    blk_k = pl.BlockSpec((s, GLA_PAIR * dk), lambda p: (0, p))
    blk_gb = pl.BlockSpec((s, GLA_PAIR * dk), lambda p: (0, n_pairs + p))
    blk_v = pl.BlockSpec((s, GLA_PAIR * dv), lambda p: (0, p))
    return n_pairs, blk_k, blk_gb, blk_v


def _head_lanes(hh):
    lane = lax.broadcasted_iota(jnp.int32, (1, GLA_PAIR * GLA_DK), 1)
    return jnp.logical_and(lane >= hh * GLA_DK, lane < (hh + 1) * GLA_DK)


def _gla_fwd(q, k, v, la):
    s = q.shape[0]
    dk, dv = GLA_DK, GLA_DV
    pw = GLA_PAIR * dk
    n_chunks = s // GLA_CHUNK
    scale = GLA_DK ** -0.5

    def body(q_ref, k_ref, v_ref, gf_ref, gb_ref, o_ref, sf_ref, sb_ref, bf_ref, bb_ref, btf_ref, btb_ref, st_ref):
        masks = [_gla_masks(rev) for rev in (False, True)]
        dirs = ((False, gf_ref, sf_ref, bf_ref, btf_ref), (True, gb_ref, sb_ref, bb_ref, btb_ref))

        def decays(n, carry):
            rows = _gla_rows(n)
            for rev, g_ref, _, b_ref, bt_ref in dirs:
                g = g_ref[rows, :]
                b_ref[rows, :] = _chunk_cumsum(g, rev)
                bt_ref[n] = _colsum(g)
            return carry

        lax.fori_loop(0, n_chunks, decays, 0, unroll=GLA_UNROLL)
        for hh in range(GLA_PAIR):
            m = _head_lanes(hh)
            vl = slice(hh * dv, (hh + 1) * dv)

            def prepare(n, carry, m=m, vl=vl):
                rows = _gla_rows(n)
                kk = k_ref[rows, :]
                vb = v_ref[rows, vl].astype(BF16)
                for rev, _, s_ref, b_ref, bt_ref in dirs:
                    ke = jnp.where(m, kk * jnp.exp(bt_ref[n] - b_ref[rows, :]), 0.0).astype(BF16)
                    s_ref[n] = _dot(vb, ke, 0, 0)
                return carry

            lax.fori_loop(0, n_chunks, prepare, 0, unroll=GLA_UNROLL)
            for rev, _, s_ref, _, bt_ref in dirs:
                _gla_scan(s_ref, bt_ref, st_ref, n_chunks, descending=rev)

            def emit(n, carry, m=m, vl=vl):
                rows = _gla_rows(n)
                qs = q_ref[rows, :] * scale
                kk = k_ref[rows, :]
                vb = v_ref[rows, vl].astype(BF16)
                o = None
                for (rev, _, s_ref, b_ref, _), (_, mask) in zip(dirs, masks):
                    b = b_ref[rows, :]
                    qd = jnp.where(m, qs * jnp.exp(b), 0.0).astype(BF16)
                    ki = jnp.where(m, kk * jnp.exp(-b), 0.0).astype(BF16)
                    a = jnp.where(mask, _dot(qd, ki, 1, 1), 0.0).astype(BF16)
                    od = _dot(a, vb, 1, 0) + _dot(qd, s_ref[n].astype(BF16), 1, 1)
                    o = od if o is None else o + od
                o_ref[rows, vl] = o
                return carry

            lax.fori_loop(0, n_chunks, emit, 0, unroll=GLA_UNROLL)

    n_pairs, blk_k, blk_gb, blk_v = _gla_specs(s)
    state = pltpu.VMEM((n_chunks, dv, pw), F32)
    scratch = [state, state, pltpu.VMEM((s, pw), F32), pltpu.VMEM((s, pw), F32), pltpu.VMEM((n_chunks, 1, pw), F32),
               pltpu.VMEM((n_chunks, 1, pw), F32), pltpu.VMEM((dv, pw), F32)]
    return pl.pallas_call(
        body, name="gla_fwd", grid=(n_pairs,), in_specs=[blk_k, blk_k, blk_v, blk_k, blk_gb],
        out_specs=blk_v, out_shape=_sds(v.shape), scratch_shapes=scratch,
        compiler_params=_params(("parallel",)),
    )(q, k, v, la, la)


def _gla_bwd(q, k, v, la, do):
    s = q.shape[0]
    dk, dv = GLA_DK, GLA_DV
    pw = GLA_PAIR * dk
    c = GLA_CHUNK
    n_chunks = s // c
    scale = GLA_DK ** -0.5

    def body(q_ref, k_ref, v_ref, gf_ref, gb_ref, do_ref, dq_ref, dk_ref, dv_ref, dgf_ref, dgb_ref,
             sf_ref, sb_ref, bf_ref, bb_ref, btf_ref, btb_ref, dsf_ref, dsb_ref, st_ref):
        masks = [_gla_masks(rev) for rev in (False, True)]
        rowc = lax.broadcasted_iota(jnp.int32, (c, pw), 0)
        dirs = ((False, gf_ref, sf_ref, bf_ref, btf_ref, dsf_ref, dgf_ref),
                (True, gb_ref, sb_ref, bb_ref, btb_ref, dsb_ref, dgb_ref))

        def decays(n, carry):
            rows = _gla_rows(n)
            for rev, g_ref, _, b_ref, bt_ref, _, _ in dirs:
                g = g_ref[rows, :]
                b_ref[rows, :] = _chunk_cumsum(g, rev)
                bt_ref[n] = _colsum(g)
            return carry

        lax.fori_loop(0, n_chunks, decays, 0, unroll=GLA_UNROLL)
        for hh in range(GLA_PAIR):
            m = _head_lanes(hh)
            vl = slice(hh * dv, (hh + 1) * dv)

            def prepare(n, carry, m=m, vl=vl):
                rows = _gla_rows(n)
                qs = q_ref[rows, :] * scale
                kk = k_ref[rows, :]
                vb = v_ref[rows, vl].astype(BF16)
                do_b = do_ref[rows, vl].astype(BF16)
                for rev, _, s_ref, b_ref, bt_ref, ds_ref, _ in dirs:
                    b = b_ref[rows, :]
                    ke = jnp.where(m, kk * jnp.exp(bt_ref[n] - b), 0.0).astype(BF16)
                    qd = jnp.where(m, qs * jnp.exp(b), 0.0).astype(BF16)
                    s_ref[n] = _dot(vb, ke, 0, 0)
                    ds_ref[n] = _dot(do_b, qd, 0, 0)
                return carry

            lax.fori_loop(0, n_chunks, prepare, 0, unroll=GLA_UNROLL)
            for rev, _, s_ref, _, bt_ref, ds_ref, _ in dirs:
                _gla_scan(s_ref, bt_ref, st_ref, n_chunks, descending=rev)
                _gla_scan(ds_ref, bt_ref, st_ref, n_chunks, descending=not rev)

            def emit(n, carry, m=m, vl=vl, first=(hh == 0)):
                rows = _gla_rows(n)
                qs = q_ref[rows, :] * scale
                kk = k_ref[rows, :]
                vb = v_ref[rows, vl].astype(BF16)
                do_b = do_ref[rows, vl].astype(BF16)
                dq = dkk = dvv = None
                for (rev, _, s_ref, b_ref, bt_ref, ds_ref, dg_ref), (_, mask) in zip(dirs, masks):
                    b = b_ref[rows, :]
                    bt = bt_ref[n]
                    eb = jnp.where(m, jnp.exp(b), 0.0)
                    enb = jnp.where(m, jnp.exp(-b), 0.0)
                    etb = jnp.where(m, jnp.exp(bt - b), 0.0)
                    ebt = jnp.exp(bt)
                    qd = qs * eb
                    ki = kk * enb
                    ke = kk * etb
                    qd_b, ki_b, ke_b = qd.astype(BF16), ki.astype(BF16), ke.astype(BF16)
                    st = s_ref[n]
                    dst = ds_ref[n]
                    dst_b = dst.astype(BF16)
                    a = jnp.where(mask, _dot(qd_b, ki_b, 1, 1), 0.0).astype(BF16)
                    da = jnp.where(mask, _dot(do_b, vb, 1, 1), 0.0).astype(BF16)
                    dv_d = _dot(a, do_b, 0, 0) + _dot(ke_b, dst_b, 1, 1)
                    dqd = _dot(da, ki_b, 1, 0) + _dot(do_b, st.astype(BF16), 1, 0)
                    dki = _dot(da, qd_b, 0, 0)
                    dke = _dot(vb, dst_b, 1, 0)
                    dbt = _colsum(st * dst) * ebt + _colsum(dke * ke)
                    db = dqd * qd - dki * ki - dke * ke
                    db = db + jnp.where(rowc == (0 if rev else c - 1), dbt, 0.0)
                    dg = _chunk_cumsum(db, not rev)
                    if first:
                        dg_ref[rows, :] = dg
                    else:
                        dg_ref[rows, :] += dg
                    dq_d = dqd * eb * scale
                    dk_d = dki * enb + dke * etb
                    dq = dq_d if dq is None else dq + dq_d
                    dkk = dk_d if dkk is None else dkk + dk_d
                    dvv = dv_d if dvv is None else dvv + dv_d
                if first:
                    dq_ref[rows, :] = dq
                    dk_ref[rows, :] = dkk
                else:
                    dq_ref[rows, :] += dq
                    dk_ref[rows, :] += dkk
                dv_ref[rows, vl] = dvv
                return carry

            lax.fori_loop(0, n_chunks, emit, 0, unroll=GLA_UNROLL)

    n_pairs, blk_k, blk_gb, blk_v = _gla_specs(s)
    vk, vv = _sds(q.shape), _sds(v.shape)
    state = pltpu.VMEM((n_chunks, dv, pw), F32)
    scratch = [state, state, pltpu.VMEM((s, pw), F32), pltpu.VMEM((s, pw), F32), pltpu.VMEM((n_chunks, 1, pw), F32),
               pltpu.VMEM((n_chunks, 1, pw), F32), state, state, pltpu.VMEM((dv, pw), F32)]
    return pl.pallas_call(
        body, name="gla_bwd", grid=(n_pairs,), in_specs=[blk_k, blk_k, blk_v, blk_k, blk_gb, blk_v],
        out_specs=[blk_k, blk_k, blk_v, blk_k, blk_k], out_shape=[vk, vk, vv, vk, vk],
        scratch_shapes=scratch, compiler_params=_params(("parallel",)),
    )(q, k, v, la, la, do)


@jax.custom_vjp
def gla(q, k, v, la):
    return _gla_fwd(q, k, v, la)


def _gla_f(q, k, v, la):
    return _gla_fwd(q, k, v, la), (q, k, v, la)


def _gla_b(res, do):
    dq, dk, dv, dgf, dgb = _gla_bwd(*res, do)
    return dq, dk, dv, jnp.concatenate([dgf, dgb], axis=1)


gla.defvjp(_gla_f, _gla_b)


ATTN_TQ = 256
ATTN_TQ_BWD = 1024
HEAD_LANES = 128


def _attn_blocks(s, tq):
    per_q = pl.BlockSpec((tq, HEAD_LANES), lambda h, j: (j, h))
    k_nope = pl.BlockSpec((s, HEAD_LANES), lambda h, j: (0, 2 * h))
    v_blk = pl.BlockSpec((s, HEAD_LANES), lambda h, j: (0, 2 * h + 1))
    k_rope = pl.BlockSpec((s, HEAD_LANES), lambda h, j: (0, 0))
    lse = pl.BlockSpec((1, tq, 1), lambda h, j: (h, j, 0))
    return per_q, k_nope, v_blk, k_rope, lse


def _attn_fwd(qn, qr, kv, kr):
    s = qn.shape[0]
    tq = min(ATTN_TQ, s)
    scale = (MLA_NOPE + MLA_ROPE) ** -0.5

    def body(qn_ref, qr_ref, kn_ref, v_ref, kr_ref, o_ref, lse_ref):
        q = jnp.concatenate([qn_ref[...], qr_ref[...]], axis=1)
        k = jnp.concatenate([kn_ref[...], kr_ref[...]], axis=1)
        sc = _dot(q, k, 1, 1) * scale
        m = jnp.max(sc, axis=-1, keepdims=True)
        p = jnp.exp(sc - m)
        l = jnp.sum(p, axis=-1, keepdims=True)
        p = p * (1.0 / l)
        o_ref[...] = _dot(p.astype(BF16), v_ref[...], 1, 0)
        lse_ref[0] = m + jnp.log(l)

    per_q, k_nope, v_blk, k_rope, lse = _attn_blocks(s, tq)
    return pl.pallas_call(
        body, name="attn_fwd", grid=(MLA_HEADS, s // tq), in_specs=[per_q, per_q, k_nope, v_blk, k_rope],
        out_specs=[per_q, lse], out_shape=[_sds(qn.shape), _sds((MLA_HEADS, s, 1))],
        compiler_params=_params(("parallel", "parallel")),
    )(qn, qr, kv, kv, kr)


def _attn_bwd(qn, qr, kv, kr, o, lse, do):
    s = qn.shape[0]
    tq = min(ATTN_TQ_BWD, s)
    n_q = s // tq
    scale = (MLA_NOPE + MLA_ROPE) ** -0.5

    def body(qn_ref, qr_ref, kn_ref, v_ref, kr_ref, o_ref, lse_ref, do_ref, dqn_ref, dqr_ref, dkv_ref, dkr_ref,
             dk_acc, dv_acc, dkr_acc):
        h, j = pl.program_id(0), pl.program_id(1)
        q = jnp.concatenate([qn_ref[...], qr_ref[...]], axis=1)
        k = jnp.concatenate([kn_ref[...], kr_ref[...]], axis=1)
        do = do_ref[...]
        do_b = do.astype(BF16)
        p = jnp.exp(_dot(q, k, 1, 1) * scale - lse_ref[0])
        dp = _dot(do_b, v_ref[...], 1, 1)
        delta = jnp.sum(do * o_ref[...], axis=-1, keepdims=True)
        ds = (p * (dp - delta) * scale).astype(BF16)
        dq = _dot(ds, k, 1, 0)
        dqn_ref[...] = dq[:, :HEAD_LANES].astype(BF16)
        dqr_ref[...] = dq[:, HEAD_LANES:].astype(BF16)
        dk = _dot(ds, q, 0, 0)
        _acc(j, dk_acc, dk[:, :HEAD_LANES])
        _acc(j, dv_acc, _dot(p.astype(BF16), do_b, 0, 0))
        _acc(jnp.where(jnp.logical_and(h == 0, j == 0), 0, 1), dkr_acc, dk[:, HEAD_LANES:])

        @pl.when(j == n_q - 1)
        def _():
            dkv_ref[:, 0:HEAD_LANES] = dk_acc[...].astype(BF16)
            dkv_ref[:, HEAD_LANES:2 * HEAD_LANES] = dv_acc[...].astype(BF16)

        @pl.when(jnp.logical_and(h == MLA_HEADS - 1, j == n_q - 1))
        def _():
            dkr_ref[...] = dkr_acc[...].astype(BF16)

    per_q, k_nope, v_blk, k_rope, lse_blk = _attn_blocks(s, tq)
    dkv_blk = pl.BlockSpec((s, 2 * HEAD_LANES), lambda h, j: (0, h))
    acc = pltpu.VMEM((s, HEAD_LANES), F32)
    return pl.pallas_call(
        body, name="attn_bwd", grid=(MLA_HEADS, n_q),
        in_specs=[per_q, per_q, k_nope, v_blk, k_rope, per_q, lse_blk, per_q],
        out_specs=[per_q, per_q, dkv_blk, k_rope],
        out_shape=[_sds(qn.shape, BF16), _sds(qr.shape, BF16), _sds(kv.shape, BF16), _sds(kr.shape, BF16)],
        scratch_shapes=[acc, acc, acc], compiler_params=_params(("arbitrary", "arbitrary")),
    )(qn, qr, kv, kv, kr, o, lse, do)


@jax.custom_vjp
def attn(qn, qr, kv, kr):
    return _attn_fwd(qn, qr, kv, kr)[0]


def _attn_f(qn, qr, kv, kr):
    o, lse = _attn_fwd(qn, qr, kv, kr)
    return o, (qn, qr, kv, kr, o, lse)


def _attn_b(res, do):
    return tuple(_attn_bwd(*res, do))


attn.defvjp(_attn_f, _attn_b)


@jax.custom_vjp
def split_proj(proj):
    out, at = [], 0
    for _, _, _, wp in PROJ_SEGS:
        out.append(proj[:, at:at + wp])
        at += wp
    return tuple(out)


def _split_f(proj):
    return split_proj(proj), None


def _split_b(_, gs):
    return (jnp.concatenate(gs, axis=1),)


split_proj.defvjp(_split_f, _split_b)


def _tile2d(rows, width, limit=BLOCK_BYTES):
    fits = [t for t in range(16, rows + 1, 16) if rows % t == 0 and t * width * 4 <= limit]
    if fits and (fits[-1] >= 64 or fits[-1] == rows):
        return fits[-1], width
    if rows * width * 4 <= limit:
        return rows, width
    cols = [t for t in range(128, width + 1, 128) if width % t == 0 and rows * t * 4 <= limit]
    return (rows, cols[-1]) if cols else (rows, width)


def _add_pair(stacked, theirs, c_idx):
    g, r, w = theirs.shape
    tr, tc = _tile2d(r, w)

    def body(c_ref, a_ref, b_ref, o_ref):
        o_ref[0] = (a_ref[0, 0].astype(F32) + b_ref[0].astype(F32)).astype(BF16)

    blk = pl.BlockSpec((1, tr, tc), lambda k, i, j, c: (k, i, j))
    spec = pltpu.PrefetchScalarGridSpec(
        num_scalar_prefetch=1, grid=(g, r // tr, w // tc),
        in_specs=[pl.BlockSpec((1, 1, tr, tc), lambda k, i, j, c: (c[0], k, i, j)), blk], out_specs=blk)
    return pl.pallas_call(body, name="add_pair", grid_spec=spec, out_shape=_sds(theirs.shape, BF16),
                          compiler_params=_params(("parallel", "parallel", "parallel")))(c_idx, stacked, theirs)


def _add_chips(pair, landed, chip_idx):
    _, r, w = pair.shape
    tr, tc = _tile2d(r, w)

    def body(c_ref, p_ref, l0_ref, l1_ref, l2_ref, o_ref):
        o_ref[...] = ((p_ref[0].astype(F32) + l0_ref[0].astype(F32)) + l1_ref[0].astype(F32)) + l2_ref[0].astype(F32)

    specs = [pl.BlockSpec((1, tr, tc), lambda i, j, c: (c[0], i, j))]
    specs += [pl.BlockSpec((1, tr, tc), functools.partial(lambda i, j, c, k: (k, i, j), k=k)) for k in range(N_CHIPS - 1)]
    spec = pltpu.PrefetchScalarGridSpec(num_scalar_prefetch=1, grid=(r // tr, w // tc), in_specs=specs,
                                        out_specs=pl.BlockSpec((tr, tc), lambda i, j, c: (i, j)))
    return pl.pallas_call(body, name="add_chips", grid_spec=spec, out_shape=_sds((r, w)),
                          compiler_params=_params(("parallel", "parallel")))(chip_idx, pair, landed, landed, landed)


def _sum_devices(g):
    n = g.shape[2]

    def body(g_ref, o_ref, done_ref):
        t = g_ref[0]
        for j in range(1, N_DEV):
            t = t + g_ref[j]
        o_ref[...] = t
        done_ref[...] = jnp.zeros_like(done_ref)

    return pl.pallas_call(body, name="sum_devices", out_shape=[_sds((1, n)), _sds((8, 128))],
                          compiler_params=_params())(g)


def _adamw_math(w, gv, m, v):
    c1 = 1.0 - ADAM_B1 ** ADAM_STEP
    c2 = 1.0 - ADAM_B2 ** ADAM_STEP
    mn = ADAM_B1 * m + (1.0 - ADAM_B1) * gv
    vn = ADAM_B2 * v + (1.0 - ADAM_B2) * (gv * gv)
    return -ADAM_LR * ((mn / c1) / (jnp.sqrt(vn / c2) + ADAM_EPS) + ADAM_WD * w), mn, vn


def _adamw(w, g, m, v):
    shp = w.shape
    shp3 = (1, 1, shp[0]) if len(shp) == 1 else (-1,) + tuple(shp[-2:])
    w3, g3, m3, v3 = (t.reshape(shp3) for t in (w, g, m, v))

    def body(w_ref, g_ref, m_ref, v_ref, d_ref, mo_ref, vo_ref):
        d_ref[...], mo_ref[...], vo_ref[...] = _adamw_math(w_ref[...], g_ref[...], m_ref[...], v_ref[...])

    nl, r, wd = w3.shape
    tr, tc = _tile2d(r, wd, BLOCK_BYTES // 2)
    blk = pl.BlockSpec((1, tr, tc), lambda l, i, j: (l, i, j))
    s3 = _sds(w3.shape)
    d, mn, vn = pl.pallas_call(
        body, name="adamw", grid=(nl, r // tr, wd // tc), in_specs=[blk] * 4, out_specs=[blk] * 3,
        out_shape=[s3, s3, s3], compiler_params=_params(("parallel", "parallel", "parallel")),
    )(w3, g3, m3, v3)
    return d.reshape(shp), mn.reshape(shp), vn.reshape(shp)


PIECE_BYTES = 1 << 20


def _place():
    return lax.axis_index("x"), lax.axis_index("y"), lax.axis_index("c")


def _pieces(shape, itemsize):
    if len(shape) >= 3:
        return [(i,) + p for i in range(shape[0]) for p in _pieces(shape[1:], itemsize)]
    rows = shape[0]
    row_bytes = itemsize
    for dsz in shape[1:]:
        row_bytes *= dsz
    k = 1
    while rows % (2 * k) == 0 and (rows // (2 * k)) % 16 == 0 and (rows // k) * row_bytes > PIECE_BYTES:
        k *= 2
    step = rows // k
    return [(pl.ds(j * step, step),) for j in range(k)]


def _split_start(make, src, dst, pieces):
    for p in pieces:
        make(src.at[p], dst.at[p]).start()
    return make(src, dst)


def _comm_call(body, name, arrs, out_shapes, n_remote, n_local):
    return pl.pallas_call(
        body, name=name, in_specs=[ANY] * len(arrs), out_specs=[ANY] * len(out_shapes), out_shape=out_shapes,
        scratch_shapes=[pltpu.SemaphoreType.DMA((n_remote,)), pltpu.SemaphoreType.DMA((n_remote,)),
                        pltpu.SemaphoreType.DMA((n_local,))],
    )(*arrs)


def all_gather8(arrs, name):
    n = len(arrs)
    pieces = [_pieces(a.shape, a.dtype.itemsize) for a in arrs]

    def body(*refs):
        ins, outs = refs[:n], refs[n:2 * n]
        send, recv, _ = refs[2 * n:]
        x, y, c = _place()
        me, sib = (x, y, c), (x, y, 1 - c)
        chips = [(1 - x, y), (x, 1 - y), (1 - x, 1 - y)]

        def slot(p):
            return 4 * p[0] + 2 * p[1] + p[2]

        def maker(t, k, to):
            def make(s, d):
                return pltpu.make_async_remote_copy(src_ref=s, dst_ref=d, send_sem=send.at[7 * t + k],
                                                    recv_sem=recv.at[7 * t + k], device_id=to, device_id_type=MESH)
            return make

        def landing(t, k, block):
            dst = outs[t].at[slot(block)]
            return maker(t, k, me)(dst, dst)

        sent = []
        for t in range(n):
            dst = outs[t].at[slot(me)]
            sent.append(_split_start(maker(t, 0, sib), ins[t], dst, pieces[t]))
            for j, chip in enumerate(chips):
                sent.append(_split_start(maker(t, 1 + j, (*chip, c)), ins[t], dst, pieces[t]))
        for j, chip in enumerate(chips):
            for t in range(n):
                landing(t, 1 + j, (*chip, c)).wait_recv()
                blk = outs[t].at[slot((*chip, c))]
                sent.append(_split_start(maker(t, 4 + j, sib), blk, blk, pieces[t]))
        for t in range(n):
            landing(t, 0, sib).wait_recv()
            for j, chip in enumerate(chips):
                landing(t, 4 + j, (*chip, 1 - c)).wait_recv()
        for cp in sent:
            cp.wait_send()

    outs = [_sds((N_DEV,) + a.shape, a.dtype) for a in arrs]
    got = _comm_call(body, name, arrs, outs, 7 * n, 1)
    x, y, c = _place()
    return [lax.dynamic_update_index_in_dim(g, a, 4 * x + 2 * y + c, 0) for g, a in zip(got, arrs)]


def sibling_send(arrs, name):
    n = len(arrs)
    pieces = [_pieces(a.shape[1:], a.dtype.itemsize) for a in arrs]

    def body(*refs):
        ins, theirs = refs[:n], refs[n:2 * n]
        send, recv, _ = refs[2 * n:]
        x, y, c = _place()
        rem = []
        for t in range(n):
            def make(s, d, t=t):
                return pltpu.make_async_remote_copy(src_ref=s, dst_ref=d, send_sem=send.at[t], recv_sem=recv.at[t],
                                                    device_id=(x, y, 1 - c), device_id_type=MESH)
            rem.append(_split_start(make, ins[t].at[1 - c], theirs[t], pieces[t]))
        for cp in rem:
            cp.wait_recv()
        for cp in rem:
            cp.wait_send()

    outs = [_sds(a.shape[1:], a.dtype) for a in arrs]
    return _comm_call(body, name, arrs, outs, n, 1)


def exchange_chips(arrs, name):
    n = len(arrs)
    pieces = [_pieces(a.shape[1:], a.dtype.itemsize) for a in arrs]

    def body(*refs):
        ins, outs = refs[:n], refs[n:2 * n]
        send, recv, _ = refs[2 * n:]
        x, y, c = _place()
        peers = [(1 - x, y), (x, 1 - y), (1 - x, 1 - y)]
        rem = []
        for t in range(n):
            for j, (px, py) in enumerate(peers):
                def make(s, d, t=t, j=j, px=px, py=py):
                    return pltpu.make_async_remote_copy(
                        src_ref=s, dst_ref=d, send_sem=send.at[3 * t + j], recv_sem=recv.at[3 * t + j],
                        device_id=(px, py, c), device_id_type=MESH)
                rem.append(_split_start(make, ins[t].at[2 * px + py], outs[t].at[j], pieces[t]))
        for cp in rem:
            cp.wait_recv()
        for cp in rem:
            cp.wait_send()

    outs = [_sds((N_CHIPS - 1,) + a.shape[1:], a.dtype) for a in arrs]
    return _comm_call(body, name, arrs, outs, 3 * n, 1)


def sibling_swap(arrs, name):
    n = len(arrs)
    pieces = [_pieces(a.shape, a.dtype.itemsize) for a in arrs]

    def body(*refs):
        ins, outs = refs[:n], refs[n:2 * n]
        send, recv, _ = refs[2 * n:]
        x, y, c = _place()
        rem = []
        for t in range(n):
            def make(s, d, t=t):
                return pltpu.make_async_remote_copy(src_ref=s, dst_ref=d, send_sem=send.at[t], recv_sem=recv.at[t],
                                                    device_id=(x, y, 1 - c), device_id_type=MESH)
            rem.append(_split_start(make, ins[t], outs[t], pieces[t]))
        for cp in rem:
            cp.wait_recv()
        for cp in rem:
            cp.wait_send()

    outs = [_sds(a.shape, a.dtype) for a in arrs]
    return _comm_call(body, name, arrs, outs, n, 1)


def _peer_copies(srcs, lands, send, recv, mode):
    x, y, c = _place()
    my_chip = 2 * x + y
    out = []
    for t in range(len(srcs)):
        for j, (px, py) in enumerate([(1 - x, y), (x, 1 - y), (1 - x, 1 - y)]):
            if mode == "gather":
                s, dst = srcs[t], lands[t].at[c, my_chip]
            else:
                s, dst = srcs[t].at[2 * px + py], lands[t].at[j]
            out.append(pltpu.make_async_remote_copy(
                src_ref=s, dst_ref=dst, send_sem=send.at[3 * t + j], recv_sem=recv.at[3 * t + j],
                device_id=(px, py, c), device_id_type=MESH))
    return out


HBM = pl.BlockSpec(memory_space=pltpu.HBM)
SEM = pl.BlockSpec(memory_space=pltpu.SEMAPHORE)
EFFECT = pltpu.SideEffectType.DATAFLOW_SIDE_EFFECTING


def ici_start(srcs, lands, mode, name):
    n = len(srcs)

    def body(*refs):
        send, recv = refs[2 * n], refs[2 * n + 1]
        for cp in _peer_copies(refs[:n], refs[n:2 * n], send, recv, mode):
            cp.start()
        refs[-1][...] = jnp.zeros_like(refs[-1])

    thru = [pltpu.HBM(a.shape, a.dtype) for a in list(srcs) + list(lands)]
    outs = pl.pallas_call(
        body, name=name, in_specs=[HBM] * (2 * n), out_specs=[SEM, SEM] + [HBM] * (2 * n) + [pl.BlockSpec(memory_space=pltpu.VMEM)],
        out_shape=[pltpu.SemaphoreType.DMA((3 * n,)), pltpu.SemaphoreType.DMA((3 * n,))] + thru + [_sds((8, 128))],
        input_output_aliases={i: 2 + i for i in range(2 * n)},
        compiler_params=pltpu.CompilerParams(has_side_effects=EFFECT),
    )(*[pltpu.with_memory_space_constraint(a, pltpu.HBM) for a in list(srcs) + list(lands)])
    return dict(send=outs[0], recv=outs[1], srcs=outs[2:2 + n], lands=outs[2 + n:2 + 2 * n], token=outs[-1])


def ici_wait(handle, after, mode, name):
    n = len(handle["srcs"])

    def body(*refs):
        send, recv = refs[2 * n], refs[2 * n + 1]
        for cp in _peer_copies(refs[:n], refs[n:2 * n], send, recv, mode):
            cp.wait_send()
            cp.wait_recv()

    arrs = list(handle["srcs"]) + list(handle["lands"])
    outs = pl.pallas_call(
        body, name=name, in_specs=[HBM] * (2 * n) + [SEM, SEM, ANY], out_specs=[HBM] * (2 * n),
        out_shape=[pltpu.HBM(a.shape, a.dtype) for a in arrs], input_output_aliases={i: i for i in range(2 * n)},
        compiler_params=pltpu.CompilerParams(has_side_effects=EFFECT),
    )(*arrs, handle["send"], handle["recv"], after)
    return outs[:n], outs[n:]


def gather_share(blocks, lands, name):
    n = len(blocks)

    def body(*refs):
        own, buf = refs[:n], refs[2 * n:3 * n]
        done, send, recv = refs[3 * n:]
        x, y, c = _place()
        my_chip = 2 * x + y
        chips = [2 * (1 - x) + y, 2 * x + (1 - y), 2 * (1 - x) + (1 - y)]
        sent = []
        for t in range(n):
            def make(s, d, k, t=t):
                return pltpu.make_async_remote_copy(src_ref=s, dst_ref=d, send_sem=send.at[4 * t + k],
                                                    recv_sem=recv.at[4 * t + k], device_id=(x, y, 1 - c),
                                                    device_id_type=MESH)
            cp = make(own[t], buf[t].at[c, my_chip], 0)
            cp.start()
            sent.append(cp)
            for k, pc in enumerate(chips):
                cp = make(buf[t].at[c, pc], buf[t].at[c, pc], 1 + k)
                cp.start()
                sent.append(cp)
        for t in range(n):
            for k in range(4):
                got = buf[t].at[1 - c, k]
                pltpu.make_async_remote_copy(src_ref=got, dst_ref=got, send_sem=send.at[4 * t + k],
                                             recv_sem=recv.at[4 * t + k], device_id=(x, y, 1 - c),
                                             device_id_type=MESH).wait_recv()
        for cp in sent:
            cp.wait_send()
        done[...] = jnp.zeros_like(done)

    outs = pl.pallas_call(
        body, name=name, in_specs=[ANY] * (2 * n), out_specs=[ANY] * n + [pl.BlockSpec(memory_space=pltpu.VMEM)],
        out_shape=[_sds(a.shape, a.dtype) for a in lands] + [_sds((8, 128))],
        input_output_aliases={n + t: t for t in range(n)},
        scratch_shapes=[pltpu.SemaphoreType.DMA((4 * n,)), pltpu.SemaphoreType.DMA((4 * n,))],
    )(*blocks, *lands)
    return outs[:n], outs[n]


@jax.custom_vjp
def _build_w_in(w4):
    full = w4.reshape(-1, w4.shape[-1])
    parts = []
    for _, start, width, wp in PROJ_SEGS:
        if width:
            parts.append(full[start:start + width])
        if wp > width:
            parts.append(jnp.zeros((wp - width, full.shape[1]), full.dtype))
    return jnp.concatenate(parts, axis=0)


def _build_w_in_f(w4):
    return _build_w_in(w4), None


def _build_w_in_b(_, g):
    parts, at = [], 0
    for _, _, width, wp in PROJ_SEGS:
        if width:
            parts.append(g[at:at + width])
        at += wp
    return (jnp.concatenate(parts, axis=0).reshape(N_CHIPS, -1, g.shape[1]),)


_build_w_in.defvjp(_build_w_in_f, _build_w_in_b)


def _split_w_uq(w):
    w3 = w.reshape(w.shape[0], MLA_HEADS, MLA_NOPE + MLA_ROPE)
    return w3[:, :, :MLA_NOPE].reshape(w.shape[0], -1), w3[:, :, MLA_NOPE:].reshape(w.shape[0], -1)


def _swap_halves(t, width):
    t3 = t.reshape(t.shape[0], -1, 2, width // 2)
    return jnp.concatenate([t3[:, :, 1:], t3[:, :, :1]], axis=2).reshape(t.shape)


def _pad_heads(t, width):
    t3 = t.reshape(t.shape[0], -1, width)
    t3 = jnp.pad(t3, ((0, 0), (0, 0), (0, HEAD_LANES - width)))
    return t3.reshape(t.shape[0], -1).astype(BF16)


def _project(xh, mod, w_in4, norm_g):
    d = D_MODEL
    return mod_mm(xh, norm_g[None], mod[None, d:2 * d], mod[None, 0:d], _build_w_in(w_in4))


def _layer(xh, mod, big, small, rope_q, rope_k, proj=None):
    d = D_MODEL
    gate = mod[None, 2 * d:3 * d]
    if proj is None:
        proj = _project(xh, mod, big["w_in"], small["norm_g"])
    gq, gk, gv, glr, mq, mkv, mkr, cb, cc, cx, _, z = split_proj(proj)

    rk = GLA_RANK
    hk = GLA_HEADS * GLA_DK
    wg = jnp.zeros((128, 2 * hk), F32)
    wg = wg.at[0:rk, 0:hk].set(small["gla_wg_f"]).at[rk:2 * rk, hk:].set(small["gla_wg_b"])
    bg = jnp.concatenate([small["gla_bg_f"], small["gla_bg_b"]])[None]
    la = gate_act(mm(glr, wg), bg)
    o_gla = rmsnorm(gla(gq, gk, gv, la), small["gla_norm_g"][None])

    cq = rmsnorm(mq, small["mla_q_norm_g"][None])
    w_nope, w_rope = _split_w_uq(jnp.concatenate([big["w_uq"][j] for j in range(N_CHIPS)], axis=1))
    qn = mm16(cq, w_nope)
    qr = mm(cq, w_rope)
    qr = fma(qr, rope_q[0], _swap_halves(qr, MLA_ROPE), rope_q[1])
    ckv = rmsnorm(mkv, small["mla_kv_norm_g"][None])
    kv = mm16(ckv, jnp.concatenate([big["w_ukv"][j] for j in range(N_CHIPS)], axis=1))
    kr = mkr[:, :MLA_ROPE]
    kr = fma(kr, rope_k[0], _swap_halves(kr, MLA_ROPE), rope_k[1])
    o_mla = rmsnorm(attn(qn, _pad_heads(qr, MLA_ROPE), kv, _pad_heads(kr, MLA_ROPE)), small["mla_out_g"][None])

    cw = jnp.concatenate([small["conv_w"], jnp.zeros((5, CONV_CH), F32)], axis=0)
    o_conv = rmsnorm(conv_op(cb, cc, cx, cw), small["conv_out_g"][None])

    o = jnp.concatenate([o_gla, o_mla, o_conv], axis=1)
    w_out = big["w_out"].reshape(d, d)
    return out_block(o, z, w_out, xh, gate)


SMALL_REPL = ("norm_g", "gla_bg_f", "gla_bg_b", "gla_norm_g", "mla_q_norm_g", "mla_kv_norm_g", "mla_out_g",
              "conv_out_g")
SMALL_SHARDED = ("gla_wg_f", "gla_wg_b", "conv_w")
BIG = ("w_in", "w_out", "w_uq", "w_ukv")
HALF_AXIS = (1, 0, 0, 0)


def kernel(x, c, positions, ada_w, ada_b, norm_g, w_in, gla_wg_f, gla_bg_f, gla_wg_b, gla_bg_b, gla_norm_g, mla_q_norm_g, mla_kv_norm_g, mla_w_uq, mla_w_ukv, mla_out_g, conv_w, conv_out_g, w_out, final_g, loss_target, m_ada_w, m_ada_b, m_norm_g, m_w_in, m_gla_wg_f, m_gla_bg_f, m_gla_wg_b, m_gla_bg_b, m_gla_norm_g, m_mla_q_norm_g, m_mla_kv_norm_g, m_mla_w_uq, m_mla_w_ukv, m_mla_out_g, m_conv_w, m_conv_out_g, m_w_out, m_final_g, v_ada_w, v_ada_b, v_norm_g, v_w_in, v_gla_wg_f, v_gla_bg_f, v_gla_wg_b, v_gla_bg_b, v_gla_norm_g, v_mla_q_norm_g, v_mla_kv_norm_g, v_mla_w_uq, v_mla_w_ukv, v_mla_out_g, v_conv_w, v_conv_out_g, v_w_out, v_final_g):
    xi, yi, ci = _place()
    chip = 2 * xi + yi
    dev = 2 * chip + ci
    s = x.shape[1]
    d = D_MODEL
    weights = dict(ada_w=ada_w, ada_b=ada_b, norm_g=norm_g, w_in=w_in, gla_wg_f=gla_wg_f, gla_bg_f=gla_bg_f,
                   gla_wg_b=gla_wg_b, gla_bg_b=gla_bg_b, gla_norm_g=gla_norm_g, mla_q_norm_g=mla_q_norm_g,
                   mla_kv_norm_g=mla_kv_norm_g, mla_w_uq=mla_w_uq, mla_w_ukv=mla_w_ukv, mla_out_g=mla_out_g,
                   conv_w=conv_w, conv_out_g=conv_out_g, w_out=w_out, final_g=final_g)
    m_in = dict(ada_w=m_ada_w, ada_b=m_ada_b, norm_g=m_norm_g, w_in=m_w_in, gla_wg_f=m_gla_wg_f, gla_bg_f=m_gla_bg_f,
                gla_wg_b=m_gla_wg_b, gla_bg_b=m_gla_bg_b, gla_norm_g=m_gla_norm_g, mla_q_norm_g=m_mla_q_norm_g,
                mla_kv_norm_g=m_mla_kv_norm_g, mla_w_uq=m_mla_w_uq, mla_w_ukv=m_mla_w_ukv, mla_out_g=m_mla_out_g,
                conv_w=m_conv_w, conv_out_g=m_conv_out_g, w_out=m_w_out, final_g=m_final_g)
    v_in = dict(ada_w=v_ada_w, ada_b=v_ada_b, norm_g=v_norm_g, w_in=v_w_in, gla_wg_f=v_gla_wg_f, gla_bg_f=v_gla_bg_f,
                gla_wg_b=v_gla_wg_b, gla_bg_b=v_gla_bg_b, gla_norm_g=v_gla_norm_g, mla_q_norm_g=v_mla_q_norm_g,
                mla_kv_norm_g=v_mla_kv_norm_g, mla_w_uq=v_mla_w_uq, mla_w_ukv=v_mla_w_ukv, mla_out_g=v_mla_out_g,
                conv_w=v_conv_w, conv_out_g=v_conv_out_g, w_out=v_w_out, final_g=v_final_g)

    g_c, g_wgf, g_wgb, g_cw = all_gather8([c, gla_wg_f, gla_wg_b, conv_w], "gather_small")

    def unshard_cols(g):
        g4 = g[0::2]
        return g4.transpose(1, 2, 0, 3).reshape(g4.shape[1], g4.shape[2], -1)

    small_full = dict(gla_wg_f=unshard_cols(g_wgf), gla_wg_b=unshard_cols(g_wgb), conv_w=unshard_cols(g_cw))
    for nme in SMALL_REPL:
        small_full[nme] = weights[nme]
    smalls = [{nme: small_full[nme][l] for nme in SMALL_REPL + SMALL_SHARDED} for l in range(DEPTH)]

    big_src = (jnp.swapaxes(w_in, 1, 2), w_out, mla_w_uq, mla_w_ukv)

    def my_halves(l, zero=0):
        out = []
        for t, a in enumerate(big_src):
            n_half = a.shape[1 + HALF_AXIS[t]] // 2
            out.append(lax.dynamic_slice_in_dim(a[l], ci * n_half + zero, n_half, axis=HALF_AXIS[t]).astype(BF16))
        return out

    def landing(blocks):
        return [lax.empty((2, N_CHIPS) + b.shape, b.dtype) for b in blocks]

    def finish_gather(handle, after, tag):
        blocks, lands = ici_wait(handle, after, "gather", "gather_wait" + tag)
        lands, done = gather_share(blocks, lands, "gather_share" + tag)
        full = [lax.dynamic_update_slice(g, b[None, None], (ci, chip) + (0,) * b.ndim) for g, b in zip(lands, blocks)]
        return full, done

    halves0 = my_halves(0)
    started0a = ici_start(halves0[:1], landing(halves0[:1]), "gather", "gather_start0a")
    started0b = ici_start(halves0[1:], landing(halves0[1:]), "gather", "gather_start0b")

    c_act = _silu_rows(g_c[:, 0, :])
    c_act16 = jnp.concatenate([c_act, jnp.zeros_like(c_act)], axis=0)
    n_ada = ada_w.shape[2]
    parts = []
    for l in range(DEPTH):
        bias = lax.dynamic_slice_in_dim(ada_b[l], chip * n_ada, n_ada)[None]
        parts.append(_mm(c_act16, ada_w[l], bias=bias, name="ada_fwd"))
    g_mod, = all_gather8([jnp.stack(parts)], "gather_mod")
    mod_mine = lax.dynamic_index_in_dim(g_mod[0::2], dev, 2, keepdims=False)
    mods = mod_mine.transpose(1, 0, 2).reshape(DEPTH, 3 * d)

    inv_freq = ROPE_THETA ** (-jnp.arange(0, MLA_ROPE, 2, dtype=F32) / MLA_ROPE)
    ang = positions[0].astype(F32)[:, None] * inv_freq
    cos, sin = jnp.cos(ang), jnp.sin(ang)
    rope_k = (jnp.concatenate([cos, cos], axis=1), jnp.concatenate([-sin, sin], axis=1))
    rope_q = (jnp.tile(rope_k[0], (1, MLA_HEADS)), jnp.tile(rope_k[1], (1, MLA_HEADS)))

    def joined(gathered, first=0):
        return {BIG[first + t]: jnp.concatenate([g[0], g[1]], axis=HALF_AXIS[first + t] + 1)
                for t, g in enumerate(gathered)}

    def run_layer(xh, mod, gathered, small):
        return _layer(xh, mod, joined(gathered), small, rope_q, rope_k)

    def project0(xh, mod, gathered, norm_g_l):
        return _project(xh, mod, joined(gathered)["w_in"], norm_g_l)

    def mix0(proj, xh, mod, gathered, small):
        return _layer(xh, mod, joined(gathered, 1), small, rope_q, rope_k, proj=proj)

    def head(hh, fg):
        return loss_op(rmsnorm(hh, fg[None]), loss_target[0])[0, 0]

    gathered0a, _ = finish_gather(started0a, mods, "0a")
    proj0, vjp0a = jax.vjp(project0, x[0], mods[0], gathered0a, smalls[0]["norm_g"])
    gathered0b, done0 = finish_gather(started0b, proj0, "0b")
    halves1 = my_halves(1, done0[0, 0].astype(jnp.int32))
    started1 = ici_start(halves1, landing(halves1), "gather", "gather_start1")
    h1, vjp0b = jax.vjp(mix0, proj0, x[0], mods[0] + started1["token"][0, 0], gathered0b, smalls[0])
    gathered1, _ = finish_gather(started1, h1, "1")
    h2, vjp1 = jax.vjp(run_layer, h1, mods[1], gathered1, smalls[1])
    loss_dev, vjp_head = jax.vjp(head, h2, final_g)
    dh2, dfinal = vjp_head(jnp.ones((), F32))

    c_idx = jnp.reshape(ci, (1,)).astype(jnp.int32)
    chip_idx = jnp.reshape(chip, (1,)).astype(jnp.int32)

    def reduce_begin(dgath, tag, zero=None):
        theirs = sibling_send(dgath, "reduce_sibling" + tag)
        pair = [_add_pair(a, b, c_idx) for a, b in zip(dgath, theirs)]
        shapes = [(N_CHIPS - 1,) + p.shape[1:] for p in pair]
        if zero is None:
            lands = [lax.empty(shp, BF16) for shp in shapes]
        else:
            lands = [jnp.broadcast_to(zero.astype(BF16), shp) for shp in shapes]
        return ici_start(pair, lands, "reduce", "reduce_start" + tag)

    def reduce_end(handle, after, tag):
        pair, landed = ici_wait(handle, after, "reduce", "reduce_wait" + tag)
        reduced = [_add_chips(p, q, chip_idx) for p, q in zip(pair, landed)]
        others = sibling_swap(reduced, "share_sibling" + tag)
        return [jnp.where(ci == 0, jnp.concatenate([own, other], axis=HALF_AXIS[t]),
                          jnp.concatenate([other, own], axis=HALF_AXIS[t]))
                for t, (own, other) in enumerate(zip(reduced, others))]

    dh1, dmod1, dgath1, dsmall1 = vjp1(dh2)
    reducing1 = reduce_begin(dgath1, "1")
    dproj0, dx_mix, dmod_mix, dgath0b, dsmall0 = vjp0b(dh1 + reducing1["token"][0, 0])
    dx_in, dmod_in, dgath0a, dnorm0 = vjp0a(dproj0)
    dx, dmod0, dgath0 = dx_in + dx_mix, dmod_in + dmod_mix, list(dgath0a) + list(dgath0b)
    dsmall0 = dict(dsmall0, norm_g=dsmall0["norm_g"] + dnorm0)
    dmods = jnp.stack([dmod0, dmod1])
    dsmalls = [dsmall0, dsmall1]

    pieces = [dmods.reshape(-1), dfinal]
    for nme in SMALL_REPL + SMALL_SHARDED:
        pieces.append(jnp.stack([dsmalls[l][nme] for l in range(DEPTH)]).reshape(-1))
    pieces.append(loss_dev.reshape(1))
    sizes = [p.shape[0] for p in pieces]
    flat = jnp.concatenate(pieces)
    padn = (-flat.shape[0]) % 128
    flat = jnp.concatenate([flat, jnp.zeros((padn,), F32)])[None]
    g_small, = all_gather8([flat], "gather_small_grads")
    total, small_done = _sum_devices(g_small)
    total = total[0]
    reducing0 = reduce_begin(dgath0, "0", small_done[0, 0])
    offs, at = [], 0
    for n_el in sizes:
        offs.append(at)
        at += n_el

    def piece(i, shape):
        return total[offs[i]:offs[i] + sizes[i]].reshape(shape)

    grads = {"ada_b": piece(0, (DEPTH, 3 * d)), "final_g": piece(1, (d,))}
    loss = piece(len(pieces) - 1, ())
    for i, nme in enumerate(SMALL_REPL + SMALL_SHARDED):
        full = piece(2 + i, small_full[nme].shape)
        if nme in SMALL_SHARDED:
            ncol = weights[nme].shape[2]
            full = lax.dynamic_slice_in_dim(full, chip * ncol, ncol, axis=2)
        grads[nme] = full

    dmod_all = g_small[:, 0, :DEPTH * 3 * d].reshape(N_DEV, DEPTH, 3 * d)
    dmod_cols = lax.dynamic_slice_in_dim(dmod_all, chip * n_ada, n_ada, axis=2)
    g_ada = []
    for l in range(DEPTH):
        dm16 = jnp.concatenate([dmod_cols[:, l], jnp.zeros((N_DEV, n_ada), F32)], axis=0)
        dm16 = dm16 + reducing0["token"][0, 0]
        g_ada.append(_mm(c_act16, dm16, ta=True, name="ada_bwd"))
    grads["ada_w"] = jnp.stack(g_ada)

    order = list(weights)
    big_names = ("w_in", "w_out", "mla_w_uq", "mla_w_ukv")
    delta, new_m, new_v = {}, {}, {}
    for nme in order:
        if nme not in big_names:
            delta[nme], new_m[nme], new_v[nme] = _adamw(weights[nme], grads[nme], m_in[nme], v_in[nme])

    def first(t):
        return t[(slice(0, 1),) * t.ndim].reshape(1)

    big_grads1 = reduce_end(reducing1, jnp.concatenate([first(dx), first(reducing0["token"])]), "1")
    done = [first(delta[nme]) for nme in order if nme not in big_names] + [first(g) for g in big_grads1]
    big_grads0 = reduce_end(reducing0, jnp.concatenate(done), "0")
    for nme, g0, g1 in zip(big_names, big_grads0, big_grads1):
        grads[nme] = jnp.stack([g0, g1])
    for nme in big_names:
        if nme == "w_in":
            w_t, m_t, v_t = (jnp.swapaxes(t, 1, 2) for t in (w_in, m_w_in, v_w_in))
            res = _adamw(w_t, grads[nme], m_t, v_t)
            delta[nme], new_m[nme], new_v[nme] = (jnp.swapaxes(t, 1, 2) for t in res)
            grads[nme] = jnp.swapaxes(grads[nme], 1, 2)
            continue
        delta[nme], new_m[nme], new_v[nme] = _adamw(weights[nme], grads[nme], m_in[nme], v_in[nme])
    return (loss, dx[None], *[grads[n_] for n_ in order], *[delta[n_] for n_ in order],
            *[new_m[n_] for n_ in order], *[new_v[n_] for n_ in order])
```

```python
import functools

import jax
import jax.numpy as jnp
from jax import lax
from jax.experimental import pallas as pl
from jax.experimental.pallas import tpu as pltpu

F32 = jnp.float32
BF16 = jnp.bfloat16
MESH = pl.DeviceIdType.MESH

DEPTH = 2
D_MODEL = 2048
GLA_HEADS = 6
GLA_DK = 64
GLA_DV = 128
GLA_RANK = 16
GLA_TEMP = 16.0
GLA_CHUNK = 64
GLA_W = GLA_HEADS * GLA_DV
MLA_HEADS = 6
MLA_QL = 384
MLA_KVL = 256
MLA_NOPE = 128
MLA_ROPE = 64
MLA_DV = 128
MLA_W = MLA_HEADS * MLA_DV
CONV_CH = D_MODEL - GLA_W - MLA_W
ROPE_THETA = 10000.0
EPS = 1e-6
IN_DIM = 5856
N_CHIPS = 4
N_DEV = 8

ADAM_LR = 0.001
ADAM_B1 = 0.9
ADAM_B2 = 0.999
ADAM_EPS = 1e-08
ADAM_WD = 0.01
ADAM_STEP = 10

PROJ_SEGS = (
    ("gq", 0, 384, 384), ("gk", 384, 384, 384), ("gv", 768, 768, 768), ("glr", 1536, 32, 128),
    ("mq", 1568, 384, 384), ("mkv", 1952, 256, 256), ("mkr", 2208, 64, 128),
    ("cb", 2272, 512, 512), ("cc", 2784, 512, 512), ("cx", 3296, 512, 512),
    ("pad", 3808, 0, 128), ("z", 3808, 2048, 2048),
)
PROJ_AL = sum(s[3] for s in PROJ_SEGS)

ANY = pl.BlockSpec(memory_space=pl.ANY)
VMEM_LIMIT = 48 * 1024 * 1024
BLOCK_BYTES = 2 * 1024 * 1024


def _params(sem=None):
    return pltpu.CompilerParams(dimension_semantics=sem, vmem_limit_bytes=VMEM_LIMIT)


def _dot(a, b, ca, cb, precision=None):
    return lax.dot_general(a, b, (((ca,), (cb,)), ((), ())), preferred_element_type=F32, precision=precision)


def _tile(dim, prefs):
    for t in prefs:
        if dim % t == 0:
            return t
    return dim


def _pick_rows(rows, width, itemsize=4):
    for t in (2048, 1024, 512, 256, 128, 64, 32, 16, 8):
        if rows % t == 0 and t * width * itemsize <= BLOCK_BYTES:
            return t
    return rows


def _mm(a, b, *, ta=False, tb=False, bias=None, out_dtype=F32, name="mm"):
    if ta:
        K, M = a.shape
    else:
        M, K = a.shape
    if tb:
        N, Kb = b.shape
    else:
        Kb, N = b.shape
    assert K == Kb, (a.shape, b.shape, ta, tb)
    tm = _tile(M, (2048, 1024, 512, 256, 128))
    tn = _tile(N, (512, 384, 256, 128) if tm >= 2048 else (1024, 512, 384, 256, 128))
    tk = _tile(K, (2048, 1024, 512, 256, 128))
    nk = K // tk
    has_bias = bias is not None

    def body(*refs):
        a_ref, b_ref = refs[0], refs[1]
        bias_ref = refs[2] if has_bias else None
        o_ref = refs[3 if has_bias else 2]
        part = _dot(a_ref[...].astype(BF16), b_ref[...].astype(BF16), 0 if ta else 1, 1 if tb else 0)

        def finish(r):
            if has_bias:
                r = r + bias_ref[...]
            o_ref[...] = r.astype(out_dtype)

        if nk == 1:
            finish(part)
            return
        acc_ref = refs[-1]
        k = pl.program_id(2)

        @pl.when(k == 0)
        def _():
            acc_ref[...] = part

        @pl.when(k != 0)
        def _():
            acc_ref[...] += part

        @pl.when(k == nk - 1)
        def _():
            finish(acc_ref[...])

    a_spec = pl.BlockSpec((tk, tm), lambda i, j, k: (k, i)) if ta else pl.BlockSpec((tm, tk), lambda i, j, k: (i, k))
    b_spec = pl.BlockSpec((tn, tk), lambda i, j, k: (j, k)) if tb else pl.BlockSpec((tk, tn), lambda i, j, k: (k, j))
    in_specs = [a_spec, b_spec]
    args = [a, b]
    if has_bias:
        in_specs.append(pl.BlockSpec((1, tn), lambda i, j, k: (0, j)))
        args.append(bias)
    return pl.pallas_call(
        body, name=name, grid=(M // tm, N // tn, nk),
        in_specs=in_specs, out_specs=pl.BlockSpec((tm, tn), lambda i, j, k: (i, j)),
        out_shape=jax.ShapeDtypeStruct((M, N), out_dtype),
        scratch_shapes=[pltpu.VMEM((tm, tn), F32)] if nk > 1 else [],
        compiler_params=_params(("parallel", "parallel", "arbitrary")),
    )(*args)


@jax.custom_vjp
def mm(a, b):
    return _mm(a, b, name="mm_fwd")


def _mm_f(a, b):
    return _mm(a, b, name="mm_fwd"), (a, b)


def _mm_b(res, g):
    a, b = res
    return _mm(g, b, tb=True, out_dtype=a.dtype, name="mm_da"), _mm(a, g, ta=True, out_dtype=b.dtype, name="mm_db")


mm.defvjp(_mm_f, _mm_b)


@jax.custom_vjp
def mm16(a, b):
    return _mm(a, b, out_dtype=BF16, name="mm16_fwd")


def _mm16_f(a, b):
    return mm16(a, b), (a, b)


mm16.defvjp(_mm16_f, _mm_b)


def _rows(body, name, tiled, full, tiled_out, acc_out, tr=None):
    rows = tiled[0].shape[0]
    if tr is None:
        width = max([a.shape[1] for a in tiled] + [s.shape[1] for s in tiled_out])
        tr = _pick_rows(rows, width)
    in_specs = [pl.BlockSpec((tr, a.shape[1]), lambda i: (i, 0)) for a in tiled]
    in_specs += [pl.BlockSpec(a.shape, lambda i: (0, 0)) for a in full]
    out_specs = [pl.BlockSpec((tr, s.shape[1]), lambda i: (i, 0)) for s in tiled_out]
    out_specs += [pl.BlockSpec(s.shape, lambda i: (0, 0)) for s in acc_out]

    def wrapped(*refs):
        body(pl.program_id(0), *refs)

    outs = pl.pallas_call(
        wrapped, name=name, grid=(rows // tr,), in_specs=in_specs, out_specs=out_specs,
        out_shape=list(tiled_out) + list(acc_out),
        compiler_params=_params(("arbitrary",)),
    )(*tiled, *full)
    return outs


def _sds(shape, dtype=F32):
    return jax.ShapeDtypeStruct(tuple(shape), dtype)


def _acc(step, ref, val):
    @pl.when(step == 0)
    def _():
        ref[...] = val

    @pl.when(step != 0)
    def _():
        ref[...] += val


def _colsum(v):
    return jnp.sum(v, axis=0, keepdims=True)


def _rstd(x):
    return lax.rsqrt(jnp.mean(x * x, axis=-1, keepdims=True) + EPS)


def _norm_grid(x, g):
    rows, w = x.shape[0], g.shape[1]
    tr = _pick_rows(rows, w)
    blk = pl.BlockSpec((tr, w), lambda i, j: (i, j))
    gblk = pl.BlockSpec((1, w), lambda i, j: (0, 0))
    return (rows // tr, x.shape[1] // w), blk, gblk


@jax.custom_vjp
def rmsnorm(x, g):
    def body(x_ref, g_ref, o_ref):
        x = x_ref[...]
        o_ref[...] = x * _rstd(x) * g_ref[...]

    grid, blk, gblk = _norm_grid(x, g)
    return pl.pallas_call(body, name="rmsnorm_fwd", grid=grid, in_specs=[blk, gblk], out_specs=blk,
                          out_shape=_sds(x.shape), compiler_params=_params(("parallel", "parallel")))(x, g)


def _rmsnorm_f(x, g):
    return rmsnorm(x, g), (x, g)


def _rmsnorm_b(res, dy):
    x, g = res

    def body(x_ref, dy_ref, g_ref, dx_ref, dg_ref):
        x = x_ref[...]
        dy = dy_ref[...]
        r = _rstd(x)
        xh = x * r
        dxh = dy * g_ref[...]
        dx_ref[...] = r * (dxh - xh * jnp.mean(dxh * xh, axis=-1, keepdims=True))
        first = jnp.logical_and(pl.program_id(0) == 0, pl.program_id(1) == 0)
        _acc(jnp.where(first, 0, 1), dg_ref, _colsum(dy * xh))

    grid, blk, gblk = _norm_grid(x, g)
    dx, dg = pl.pallas_call(body, name="rmsnorm_bwd", grid=grid, in_specs=[blk, blk, gblk], out_specs=[blk, gblk],
                            out_shape=[_sds(x.shape), _sds(g.shape)],
                            compiler_params=_params(("arbitrary", "arbitrary")))(x, dy, g)
    return dx, dg


rmsnorm.defvjp(_rmsnorm_f, _rmsnorm_b)


def _modulate(x, g, scale, shift):
    def body(i, x_ref, g_ref, sc_ref, sh_ref, o_ref):
        x = x_ref[...]
        xn = x * _rstd(x) * g_ref[...]
        o_ref[...] = (xn * (1.0 + sc_ref[...]) + sh_ref[...]).astype(BF16)
    return _rows(body, "modulate_fwd", [x], [g, scale, shift], [_sds(x.shape, BF16)], [])[0]


def _modulate_bwd(x, g, scale, shift, dh):
    def body(i, x_ref, dh_ref, g_ref, sc_ref, dx_ref, dg_ref, dsc_ref, dsh_ref):
        x = x_ref[...]
        dh = dh_ref[...]
        gv = g_ref[...]
        r = _rstd(x)
        xh = x * r
        dxn = dh * (1.0 + sc_ref[...])
        dxh = dxn * gv
        dx_ref[...] = r * (dxh - xh * jnp.mean(dxh * xh, axis=-1, keepdims=True))
        _acc(i, dg_ref, _colsum(dxn * xh))
        _acc(i, dsc_ref, _colsum(dh * (xh * gv)))
        _acc(i, dsh_ref, _colsum(dh))

    v = _sds(g.shape)
    return _rows(body, "modulate_bwd", [x, dh], [g, scale], [_sds(x.shape)], [v, v, v])


@jax.custom_vjp
def mod_mm(x, g, scale, shift, wt):
    return _mm(_modulate(x, g, scale, shift), wt, tb=True, name="mm_in")


def _mod_mm_f(x, g, scale, shift, wt):
    h = _modulate(x, g, scale, shift)
    return _mm(h, wt, tb=True, name="mm_in"), (x, g, scale, shift, wt, h)


def _mod_mm_b(res, dproj):
    x, g, scale, shift, wt, h = res
    dproj = dproj.astype(BF16)
    dh = _mm(dproj, wt, name="mm_in_dh")
    dwt = _mm(dproj, h, ta=True, out_dtype=wt.dtype, name="mm_in_dw")
    dx, dg, dsc, dsh = _modulate_bwd(x, g, scale, shift, dh)
    return dx, dg, dsc, dsh, dwt


mod_mm.defvjp(_mod_mm_f, _mod_mm_b)


def _sigmoid(z):
    return 1.0 / (1.0 + jnp.exp(-z))


def _gate_mul(o, z):
    def body(i, o_ref, z_ref, y_ref):
        z = z_ref[...]
        y_ref[...] = (o_ref[...] * (z * _sigmoid(z))).astype(BF16)
    return _rows(body, "gate_mul_fwd", [o, z], [], [_sds(o.shape, BF16)], [])[0]


def _gate_mul_bwd(o, z, dy):
    def body(i, o_ref, z_ref, dy_ref, do_ref, dz_ref):
        z = z_ref[...]
        dy = dy_ref[...]
        s = _sigmoid(z)
        do_ref[...] = dy * (z * s)
        dz_ref[...] = dy * o_ref[...] * (s * (1.0 + z * (1.0 - s)))
    return _rows(body, "gate_mul_bwd", [o, z, dy], [], [_sds(o.shape), _sds(o.shape)], [])


def _residual(x, u, gate):
    def body(i, x_ref, u_ref, g_ref, o_ref):
        o_ref[...] = x_ref[...] + g_ref[...] * u_ref[...]
    return _rows(body, "residual_fwd", [x, u], [gate], [_sds(x.shape)], [])[0]


def _residual_bwd(d, u, gate):
    def body(i, d_ref, u_ref, g_ref, du_ref, dg_ref):
        d = d_ref[...]
        du_ref[...] = (g_ref[...] * d).astype(BF16)
        _acc(i, dg_ref, _colsum(d * u_ref[...]))

    return _rows(body, "residual_bwd", [d, u], [gate], [_sds(u.shape, BF16)], [_sds(gate.shape)])


@jax.custom_vjp
def out_block(o, z, w, x, gate):
    return _residual(x, _mm(_gate_mul(o, z), w, name="mm_out"), gate)


def _out_block_f(o, z, w, x, gate):
    y = _gate_mul(o, z)
    u = _mm(y, w, name="mm_out")
    return _residual(x, u, gate), (o, z, w, y, u, gate)


def _out_block_b(res, d):
    o, z, w, y, u, gate = res
    du, dgate = _residual_bwd(d, u, gate)
    dy = _mm(du, w, tb=True, name="mm_out_dy")
    dw = _mm(y, du, ta=True, out_dtype=w.dtype, name="mm_out_dw")
    do, dz = _gate_mul_bwd(o, z, dy)
    return do, dz, dw, d, dgate


out_block.defvjp(_out_block_f, _out_block_b)


@jax.custom_vjp
def gate_act(u, b):
    def body(i, u_ref, b_ref, o_ref):
        t = u_ref[...] + b_ref[...]
        o_ref[...] = (jnp.minimum(t, 0.0) - jnp.log(1.0 + jnp.exp(-jnp.abs(t)))) / GLA_TEMP
    return _rows(body, "gate_act_fwd", [u], [b], [_sds(u.shape)], [])[0]


def _gate_act_f(u, b):
    return gate_act(u, b), (u, b)


def _gate_act_b(res, d):
    u, b = res

    def body(i, u_ref, d_ref, b_ref, du_ref, db_ref):
        t = u_ref[...] + b_ref[...]
        du = d_ref[...] * _sigmoid(-t) / GLA_TEMP
        du_ref[...] = du
        _acc(i, db_ref, _colsum(du))

    du, db = _rows(body, "gate_act_bwd", [u, d], [b], [_sds(u.shape)], [_sds(b.shape)])
    return du, db


gate_act.defvjp(_gate_act_f, _gate_act_b)


@jax.custom_vjp
def fma(a, b, c, d):
    def body(i, a_ref, b_ref, c_ref, d_ref, o_ref):
        o_ref[...] = a_ref[...] * b_ref[...] + c_ref[...] * d_ref[...]
    return _rows(body, "fma_fwd", [a, b, c, d], [], [_sds(a.shape)], [])[0]


def _fma_f(a, b, c, d):
    return fma(a, b, c, d), (b, d)


def _fma_b(res, g):
    b, d = res

    def body(i, g_ref, b_ref, d_ref, da_ref, dc_ref):
        g = g_ref[...]
        da_ref[...] = g * b_ref[...]
        dc_ref[...] = g * d_ref[...]

    da, dc = _rows(body, "fma_bwd", [g, b, d], [], [_sds(g.shape), _sds(g.shape)], [])
    return da, jnp.zeros_like(b), dc, jnp.zeros_like(d)


fma.defvjp(_fma_f, _fma_b)


def _silu_rows(c):
    def body(i, c_ref, o_ref):
        v = c_ref[...]
        o_ref[...] = v * _sigmoid(v)
    return _rows(body, "silu", [c], [], [_sds(c.shape)], [])[0]


@jax.custom_vjp
def loss_op(y, t):
    return _loss_fwd(y, t)[0]


def _loss_fwd(y, t):
    inv = 1.0 / y.shape[1]

    def body(i, y_ref, t_ref, d_ref, l_ref):
        e = y_ref[...] - t_ref[...]
        d_ref[...] = e * inv
        _acc(i, l_ref, jnp.sum(_colsum(e * e), axis=1, keepdims=True) * (0.5 * inv))

    d, l = _rows(body, "loss_fwd", [y, t], [], [_sds(y.shape)], [_sds((1, 1))])
    return l, d


def _loss_f(y, t):
    l, d = _loss_fwd(y, t)
    return l, d


def _loss_b(d, g):
    return d * g, jnp.zeros_like(d)


loss_op.defvjp(_loss_f, _loss_b)


def _conv_terms(cc, cx, rows, n):
    u = cc * cx
    up = jnp.where(rows == 0, 0.0, pltpu.roll(u, 1, 0))
    un = jnp.where(rows == n - 1, 0.0, pltpu.roll(u, n - 1, 0))
    return u, up, un


CONV_COLS = 128


def _conv_specs(s, n_in):
    blk = pl.BlockSpec((s, CONV_COLS), lambda j: (0, j))
    wblk = pl.BlockSpec((8, CONV_COLS), lambda j: (0, j))
    return [blk] * n_in + [wblk], blk, wblk


@jax.custom_vjp
def conv_op(cb, cc, cx, w):
    s, ch = cb.shape

    def body(cb_ref, cc_ref, cx_ref, w_ref, o_ref):
        rows = lax.broadcasted_iota(jnp.int32, (s, CONV_COLS), 0)
        u, up, un = _conv_terms(cc_ref[...], cx_ref[...], rows, s)
        conv = up * w_ref[0:1, :] + u * w_ref[1:2, :] + un * w_ref[2:3, :]
        o_ref[...] = cb_ref[...] * conv

    in_specs, blk, _ = _conv_specs(s, 3)
    return pl.pallas_call(
        body, name="conv_fwd", grid=(ch // CONV_COLS,), in_specs=in_specs, out_specs=blk,
        out_shape=_sds(cb.shape), compiler_params=_params(("parallel",)),
    )(cb, cc, cx, w)


def _conv_f(cb, cc, cx, w):
    return conv_op(cb, cc, cx, w), (cb, cc, cx, w)


def _conv_b(res, d):
    cb, cc, cx, w = res
    s, ch = cb.shape

    def body(cb_ref, cc_ref, cx_ref, d_ref, w_ref, dcb_ref, dcc_ref, dcx_ref, dw_ref):
        rows = lax.broadcasted_iota(jnp.int32, (s, CONV_COLS), 0)
        cc_v = cc_ref[...]
        cx_v = cx_ref[...]
        u, up, un = _conv_terms(cc_v, cx_v, rows, s)
        w0, w1, w2 = w_ref[0:1, :], w_ref[1:2, :], w_ref[2:3, :]
        dv = d_ref[...]
        dcb_ref[...] = dv * (up * w0 + u * w1 + un * w2)
        dconv = dv * cb_ref[...]
        d_next = jnp.where(rows == s - 1, 0.0, pltpu.roll(dconv, s - 1, 0))
        d_prev = jnp.where(rows == 0, 0.0, pltpu.roll(dconv, 1, 0))
        du = w0 * d_next + w1 * dconv + w2 * d_prev
        dcc_ref[...] = du * cx_v
        dcx_ref[...] = du * cc_v
        dw_ref[...] = jnp.zeros_like(dw_ref)
        dw_ref[0:1, :] = _colsum(dconv * up)
        dw_ref[1:2, :] = _colsum(dconv * u)
        dw_ref[2:3, :] = _colsum(dconv * un)

    in_specs, blk, wblk = _conv_specs(s, 4)
    v = _sds(cb.shape)
    return tuple(pl.pallas_call(
        body, name="conv_bwd", grid=(ch // CONV_COLS,), in_specs=in_specs, out_specs=[blk, blk, blk, wblk],
        out_shape=[v, v, v, _sds(w.shape)], compiler_params=_params(("parallel",)),
    )(cb, cc, cx, d, w))


conv_op.defvjp(_conv_f, _conv_b)


def _gla_masks(rev):
    c = GLA_CHUNK
    row = lax.broadcasted_iota(jnp.int32, (c, c), 0)
    col = lax.broadcasted_iota(jnp.int32, (c, c), 1)
    mask = (row < col) if rev else (row >= col)
    return rev, mask


def _chunk_cumsum(g, rev):
    c = g.shape[0]
    row = lax.broadcasted_iota(jnp.int32, g.shape, 0)
    b = g
    s = 1
    while s < c:
        if rev:
            b = b + jnp.where(row < c - s, pltpu.roll(b, c - s, 0), 0.0)
        else:
            b = b + jnp.where(row >= s, pltpu.roll(b, s, 0), 0.0)
        s *= 2
    return b


GLA_UNROLL = 16


def _gla_rows(n):
    return pl.ds(pl.multiple_of(n * GLA_CHUNK, GLA_CHUNK), GLA_CHUNK)


def _gla_scan(s_ref, bt_ref, st_ref, n_chunks, descending):
    st_ref[...] = jnp.zeros_like(st_ref)

    def step(i, carry):
        n = (n_chunks - 1 - i) if descending else i
        own = s_ref[n]
        st = st_ref[...]
        s_ref[n] = st
        st_ref[...] = st * jnp.exp(bt_ref[n]) + own
        return carry

    lax.fori_loop(0, n_chunks, step, 0)


GLA_PAIR = 2


def _gla_specs(s):
    dk, dv = GLA_DK, GLA_DV
    n_pairs = GLA_HEADS // GLA_PAIR
    blk_k = pl.BlockSpec((s, GLA_PAIR * dk), lambda p: (0, p))
    blk_gb = pl.BlockSpec((s, GLA_PAIR * dk), lambda p: (0, n_pairs + p))
    blk_v = pl.BlockSpec((s, GLA_PAIR * dv), lambda p: (0, p))
    return n_pairs, blk_k, blk_gb, blk_v


def _head_lanes(hh):
    lane = lax.broadcasted_iota(jnp.int32, (1, GLA_PAIR * GLA_DK), 1)
    return jnp.logical_and(lane >= hh * GLA_DK, lane < (hh + 1) * GLA_DK)


def _gla_fwd(q, k, v, la):
    s = q.shape[0]
    dk, dv = GLA_DK, GLA_DV
    pw = GLA_PAIR * dk
    n_chunks = s // GLA_CHUNK
    scale = GLA_DK ** -0.5

    def body(q_ref, k_ref, v_ref, gf_ref, gb_ref, o_ref, sf_ref, sb_ref, bf_ref, bb_ref, btf_ref, btb_ref, st_ref):
        masks = [_gla_masks(rev) for rev in (False, True)]
        dirs = ((False, gf_ref, sf_ref, bf_ref, btf_ref), (True, gb_ref, sb_ref, bb_ref, btb_ref))

        def decays(n, carry):
            rows = _gla_rows(n)
            for rev, g_ref, _, b_ref, bt_ref in dirs:
                g = g_ref[rows, :]
                b_ref[rows, :] = _chunk_cumsum(g, rev)
                bt_ref[n] = _colsum(g)
            return carry

        lax.fori_loop(0, n_chunks, decays, 0, unroll=GLA_UNROLL)
        for hh in range(GLA_PAIR):
            m = _head_lanes(hh)
            vl = slice(hh * dv, (hh + 1) * dv)

            def prepare(n, carry, m=m, vl=vl):
                rows = _gla_rows(n)
                kk = k_ref[rows, :]
                vb = v_ref[rows, vl].astype(BF16)
                for rev, _, s_ref, b_ref, bt_ref in dirs:
                    ke = jnp.where(m, kk * jnp.exp(bt_ref[n] - b_ref[rows, :]), 0.0).astype(BF16)
                    s_ref[n] = _dot(vb, ke, 0, 0)
                return carry

            lax.fori_loop(0, n_chunks, prepare, 0, unroll=GLA_UNROLL)
            for rev, _, s_ref, _, bt_ref in dirs:
                _gla_scan(s_ref, bt_ref, st_ref, n_chunks, descending=rev)

            def emit(n, carry, m=m, vl=vl):
                rows = _gla_rows(n)
                qs = q_ref[rows, :] * scale
                kk = k_ref[rows, :]
                vb = v_ref[rows, vl].astype(BF16)
                o = None
                for (rev, _, s_ref, b_ref, _), (_, mask) in zip(dirs, masks):
                    b = b_ref[rows, :]
                    qd = jnp.where(m, qs * jnp.exp(b), 0.0).astype(BF16)
                    ki = jnp.where(m, kk * jnp.exp(-b), 0.0).astype(BF16)
                    a = jnp.where(mask, _dot(qd, ki, 1, 1), 0.0).astype(BF16)
                    od = _dot(a, vb, 1, 0) + _dot(qd, s_ref[n].astype(BF16), 1, 1)
                    o = od if o is None else o + od
                o_ref[rows, vl] = o
                return carry

            lax.fori_loop(0, n_chunks, emit, 0, unroll=GLA_UNROLL)

    n_pairs, blk_k, blk_gb, blk_v = _gla_specs(s)
    state = pltpu.VMEM((n_chunks, dv, pw), F32)
    scratch = [state, state, pltpu.VMEM((s, pw), F32), pltpu.VMEM((s, pw), F32), pltpu.VMEM((n_chunks, 1, pw), F32),
               pltpu.VMEM((n_chunks, 1, pw), F32), pltpu.VMEM((dv, pw), F32)]
    return pl.pallas_call(
        body, name="gla_fwd", grid=(n_pairs,), in_specs=[blk_k, blk_k, blk_v, blk_k, blk_gb],
        out_specs=blk_v, out_shape=_sds(v.shape), scratch_shapes=scratch,
        compiler_params=_params(("parallel",)),
    )(q, k, v, la, la)


def _gla_bwd(q, k, v, la, do):
    s = q.shape[0]
    dk, dv = GLA_DK, GLA_DV
    pw = GLA_PAIR * dk
    c = GLA_CHUNK
    n_chunks = s // c
    scale = GLA_DK ** -0.5

    def body(q_ref, k_ref, v_ref, gf_ref, gb_ref, do_ref, dq_ref, dk_ref, dv_ref, dgf_ref, dgb_ref,
             sf_ref, sb_ref, bf_ref, bb_ref, btf_ref, btb_ref, dsf_ref, dsb_ref, st_ref):
        masks = [_gla_masks(rev) for rev in (False, True)]
        rowc = lax.broadcasted_iota(jnp.int32, (c, pw), 0)
        dirs = ((False, gf_ref, sf_ref, bf_ref, btf_ref, dsf_ref, dgf_ref),
                (True, gb_ref, sb_ref, bb_ref, btb_ref, dsb_ref, dgb_ref))

        def decays(n, carry):
            rows = _gla_rows(n)
            for rev, g_ref, _, b_ref, bt_ref, _, _ in dirs:
                g = g_ref[rows, :]
                b_ref[rows, :] = _chunk_cumsum(g, rev)
                bt_ref[n] = _colsum(g)
            return carry

        lax.fori_loop(0, n_chunks, decays, 0, unroll=GLA_UNROLL)
        for hh in range(GLA_PAIR):
            m = _head_lanes(hh)
            vl = slice(hh * dv, (hh + 1) * dv)

            def prepare(n, carry, m=m, vl=vl):
                rows = _gla_rows(n)
                qs = q_ref[rows, :] * scale
                kk = k_ref[rows, :]
                vb = v_ref[rows, vl].astype(BF16)
                do_b = do_ref[rows, vl].astype(BF16)
                for rev, _, s_ref, b_ref, bt_ref, ds_ref, _ in dirs:
                    b = b_ref[rows, :]
                    ke = jnp.where(m, kk * jnp.exp(bt_ref[n] - b), 0.0).astype(BF16)
                    qd = jnp.where(m, qs * jnp.exp(b), 0.0).astype(BF16)
                    s_ref[n] = _dot(vb, ke, 0, 0)
                    ds_ref[n] = _dot(do_b, qd, 0, 0)
                return carry

            lax.fori_loop(0, n_chunks, prepare, 0, unroll=GLA_UNROLL)
            for rev, _, s_ref, _, bt_ref, ds_ref, _ in dirs:
                _gla_scan(s_ref, bt_ref, st_ref, n_chunks, descending=rev)
                _gla_scan(ds_ref, bt_ref, st_ref, n_chunks, descending=not rev)

            def emit(n, carry, m=m, vl=vl, first=(hh == 0)):
                rows = _gla_rows(n)
                qs = q_ref[rows, :] * scale
                kk = k_ref[rows, :]
                vb = v_ref[rows, vl].astype(BF16)
                do_b = do_ref[rows, vl].astype(BF16)
                dq = dkk = dvv = None
                for (rev, _, s_ref, b_ref, bt_ref, ds_ref, dg_ref), (_, mask) in zip(dirs, masks):
                    b = b_ref[rows, :]
                    bt = bt_ref[n]
                    eb = jnp.where(m, jnp.exp(b), 0.0)
                    enb = jnp.where(m, jnp.exp(-b), 0.0)
                    etb = jnp.where(m, jnp.exp(bt - b), 0.0)
                    ebt = jnp.exp(bt)
                    qd = qs * eb
                    ki = kk * enb
                    ke = kk * etb
                    qd_b, ki_b, ke_b = qd.astype(BF16), ki.astype(BF16), ke.astype(BF16)
                    st = s_ref[n]
                    dst = ds_ref[n]
                    dst_b = dst.astype(BF16)
                    a = jnp.where(mask, _dot(qd_b, ki_b, 1, 1), 0.0).astype(BF16)
                    da = jnp.where(mask, _dot(do_b, vb, 1, 1), 0.0).astype(BF16)
                    dv_d = _dot(a, do_b, 0, 0) + _dot(ke_b, dst_b, 1, 1)
                    dqd = _dot(da, ki_b, 1, 0) + _dot(do_b, st.astype(BF16), 1, 0)
                    dki = _dot(da, qd_b, 0, 0)
                    dke = _dot(vb, dst_b, 1, 0)
                    dbt = _colsum(st * dst) * ebt + _colsum(dke * ke)
                    db = dqd * qd - dki * ki - dke * ke
                    db = db + jnp.where(rowc == (0 if rev else c - 1), dbt, 0.0)
                    dg = _chunk_cumsum(db, not rev)
                    if first:
                        dg_ref[rows, :] = dg
                    else:
                        dg_ref[rows, :] += dg
                    dq_d = dqd * eb * scale
                    dk_d = dki * enb + dke * etb
                    dq = dq_d if dq is None else dq + dq_d
                    dkk = dk_d if dkk is None else dkk + dk_d
                    dvv = dv_d if dvv is None else dvv + dv_d
                if first:
                    dq_ref[rows, :] = dq
                    dk_ref[rows, :] = dkk
                else:
                    dq_ref[rows, :] += dq
                    dk_ref[rows, :] += dkk
                dv_ref[rows, vl] = dvv
                return carry

            lax.fori_loop(0, n_chunks, emit, 0, unroll=GLA_UNROLL)

    n_pairs, blk_k, blk_gb, blk_v = _gla_specs(s)
    vk, vv = _sds(q.shape), _sds(v.shape)
    state = pltpu.VMEM((n_chunks, dv, pw), F32)
    scratch = [state, state, pltpu.VMEM((s, pw), F32), pltpu.VMEM((s, pw), F32), pltpu.VMEM((n_chunks, 1, pw), F32),
               pltpu.VMEM((n_chunks, 1, pw), F32), state, state, pltpu.VMEM((dv, pw), F32)]
    return pl.pallas_call(
        body, name="gla_bwd", grid=(n_pairs,), in_specs=[blk_k, blk_k, blk_v, blk_k, blk_gb, blk_v],
        out_specs=[blk_k, blk_k, blk_v, blk_k, blk_k], out_shape=[vk, vk, vv, vk, vk],
        scratch_shapes=scratch, compiler_params=_params(("parallel",)),
    )(q, k, v, la, la, do)


@jax.custom_vjp
def gla(q, k, v, la):
    return _gla_fwd(q, k, v, la)


def _gla_f(q, k, v, la):
    return _gla_fwd(q, k, v, la), (q, k, v, la)


def _gla_b(res, do):
    dq, dk, dv, dgf, dgb = _gla_bwd(*res, do)
    return dq, dk, dv, jnp.concatenate([dgf, dgb], axis=1)


gla.defvjp(_gla_f, _gla_b)


ATTN_TQ = 1024
ATTN_SUB = 128
ATTN_TQ_BWD = 1024
HEAD_LANES = 128


def _attn_blocks(s, tq):
    per_q = pl.BlockSpec((tq, HEAD_LANES), lambda h, j: (j, h))
    k_nope = pl.BlockSpec((s, HEAD_LANES), lambda h, j: (0, 2 * h))
    v_blk = pl.BlockSpec((s, HEAD_LANES), lambda h, j: (0, 2 * h + 1))
    k_rope = pl.BlockSpec((s, HEAD_LANES), lambda h, j: (0, 0))
    lse = pl.BlockSpec((1, tq, 1), lambda h, j: (h, j, 0))
    return per_q, k_nope, v_blk, k_rope, lse


def _attn_fwd(qn, qr, kv, kr):
    s = qn.shape[0]
    tq = min(ATTN_TQ, s)
    scale = (MLA_NOPE + MLA_ROPE) ** -0.5

    def body(qn_ref, qr_ref, kn_ref, v_ref, kr_ref, o_ref, lse_ref):
        k = jnp.concatenate([kn_ref[...], kr_ref[...]], axis=1)

        def rows_of(i, carry):
            rows = pl.ds(pl.multiple_of(i * ATTN_SUB, ATTN_SUB), ATTN_SUB)
            q = jnp.concatenate([qn_ref[rows, :], qr_ref[rows, :]], axis=1)
            sc = _dot(q, k, 1, 1) * scale
            m = jnp.max(sc, axis=-1, keepdims=True)
            p = jnp.exp(sc - m)
            l = jnp.sum(p, axis=-1, keepdims=True)
            p = p * (1.0 / l)
            o_ref[rows, :] = _dot(p.astype(BF16), v_ref[...], 1, 0)
            lse_ref[0, rows, :] = m + jnp.log(l)
            return carry

        lax.fori_loop(0, tq // ATTN_SUB, rows_of, 0)

    per_q, k_nope, v_blk, k_rope, lse = _attn_blocks(s, tq)
    return pl.pallas_call(
        body, name="attn_fwd", grid=(MLA_HEADS, s // tq), in_specs=[per_q, per_q, k_nope, v_blk, k_rope],
        out_specs=[per_q, lse], out_shape=[_sds(qn.shape), _sds((MLA_HEADS, s, 1))],
        compiler_params=_params(("parallel", "parallel")),
    )(qn, qr, kv, kv, kr)


def _attn_bwd(qn, qr, kv, kr, o, lse, do):
    s = qn.shape[0]
    tq = min(ATTN_TQ_BWD, s)
    n_q = s // tq
    scale = (MLA_NOPE + MLA_ROPE) ** -0.5

    def body(qn_ref, qr_ref, kn_ref, v_ref, kr_ref, o_ref, lse_ref, do_ref, dqn_ref, dqr_ref, dkv_ref, dkr_ref,
             dk_acc, dv_acc, dkr_acc):
        h, j = pl.program_id(0), pl.program_id(1)
        q = jnp.concatenate([qn_ref[...], qr_ref[...]], axis=1)
        k = jnp.concatenate([kn_ref[...], kr_ref[...]], axis=1)
        do = do_ref[...]
        do_b = do.astype(BF16)
        p = jnp.exp(_dot(q, k, 1, 1) * scale - lse_ref[0])
        dp = _dot(do_b, v_ref[...], 1, 1)
        delta = jnp.sum(do * o_ref[...], axis=-1, keepdims=True)
        ds = (p * (dp - delta) * scale).astype(BF16)
        dq = _dot(ds, k, 1, 0)
        dqn_ref[...] = dq[:, :HEAD_LANES].astype(BF16)
        dqr_ref[...] = dq[:, HEAD_LANES:].astype(BF16)
        dk = _dot(ds, q, 0, 0)
        _acc(j, dk_acc, dk[:, :HEAD_LANES])
        _acc(j, dv_acc, _dot(p.astype(BF16), do_b, 0, 0))
        _acc(jnp.where(jnp.logical_and(h == 0, j == 0), 0, 1), dkr_acc, dk[:, HEAD_LANES:])

        @pl.when(j == n_q - 1)
        def _():
            dkv_ref[:, 0:HEAD_LANES] = dk_acc[...].astype(BF16)
            dkv_ref[:, HEAD_LANES:2 * HEAD_LANES] = dv_acc[...].astype(BF16)

        @pl.when(jnp.logical_and(h == MLA_HEADS - 1, j == n_q - 1))
        def _():
            dkr_ref[...] = dkr_acc[...].astype(BF16)

    per_q, k_nope, v_blk, k_rope, lse_blk = _attn_blocks(s, tq)
    dkv_blk = pl.BlockSpec((s, 2 * HEAD_LANES), lambda h, j: (0, h))
    acc = pltpu.VMEM((s, HEAD_LANES), F32)
    return pl.pallas_call(
        body, name="attn_bwd", grid=(MLA_HEADS, n_q),
        in_specs=[per_q, per_q, k_nope, v_blk, k_rope, per_q, lse_blk, per_q],
        out_specs=[per_q, per_q, dkv_blk, k_rope],
        out_shape=[_sds(qn.shape, BF16), _sds(qr.shape, BF16), _sds(kv.shape, BF16), _sds(kr.shape, BF16)],
        scratch_shapes=[acc, acc, acc], compiler_params=_params(("arbitrary", "arbitrary")),
    )(qn, qr, kv, kv, kr, o, lse, do)


@jax.custom_vjp
def attn(qn, qr, kv, kr):
    return _attn_fwd(qn, qr, kv, kr)[0]


def _attn_f(qn, qr, kv, kr):
    o, lse = _attn_fwd(qn, qr, kv, kr)
    return o, (qn, qr, kv, kr, o, lse)


def _attn_b(res, do):
    return tuple(_attn_bwd(*res, do))


attn.defvjp(_attn_f, _attn_b)


@jax.custom_vjp
def split_proj(proj):
    out, at = [], 0
    for _, _, _, wp in PROJ_SEGS:
        out.append(proj[:, at:at + wp])
        at += wp
    return tuple(out)


def _split_f(proj):
    return split_proj(proj), None


def _split_b(_, gs):
    return (jnp.concatenate(gs, axis=1),)


split_proj.defvjp(_split_f, _split_b)


def _tile2d(rows, width, limit=BLOCK_BYTES):
    fits = [t for t in range(16, rows + 1, 16) if rows % t == 0 and t * width * 4 <= limit]
    if fits and (fits[-1] >= 64 or fits[-1] == rows):
        return fits[-1], width
    if rows * width * 4 <= limit:
        return rows, width
    cols = [t for t in range(128, width + 1, 128) if width % t == 0 and rows * t * 4 <= limit]
    return (rows, cols[-1]) if cols else (rows, width)


def _add_pair(stacked, theirs, c_idx):
    g, r, w = theirs.shape
    tr, tc = _tile2d(r, w)

    def body(c_ref, a_ref, b_ref, o_ref):
        o_ref[0] = (a_ref[0, 0].astype(F32) + b_ref[0].astype(F32)).astype(BF16)

    blk = pl.BlockSpec((1, tr, tc), lambda k, i, j, c: (k, i, j))
    spec = pltpu.PrefetchScalarGridSpec(
        num_scalar_prefetch=1, grid=(g, r // tr, w // tc),
        in_specs=[pl.BlockSpec((1, 1, tr, tc), lambda k, i, j, c: (c[0], k, i, j)), blk], out_specs=blk)
    return pl.pallas_call(body, name="add_pair", grid_spec=spec, out_shape=_sds(theirs.shape, BF16),
                          compiler_params=_params(("parallel", "parallel", "parallel")))(c_idx, stacked, theirs)


def _add_chips(pair, landed, chip_idx):
    _, r, w = pair.shape
    tr, tc = _tile2d(r, w)

    def body(c_ref, p_ref, l0_ref, l1_ref, l2_ref, o_ref):
        o_ref[...] = ((p_ref[0].astype(F32) + l0_ref[0].astype(F32)) + l1_ref[0].astype(F32)) + l2_ref[0].astype(F32)

    specs = [pl.BlockSpec((1, tr, tc), lambda i, j, c: (c[0], i, j))]
    specs += [pl.BlockSpec((1, tr, tc), functools.partial(lambda i, j, c, k: (k, i, j), k=k)) for k in range(N_CHIPS - 1)]
    spec = pltpu.PrefetchScalarGridSpec(num_scalar_prefetch=1, grid=(r // tr, w // tc), in_specs=specs,
                                        out_specs=pl.BlockSpec((tr, tc), lambda i, j, c: (i, j)))
    return pl.pallas_call(body, name="add_chips", grid_spec=spec, out_shape=_sds((r, w)),
                          compiler_params=_params(("parallel", "parallel")))(chip_idx, pair, landed, landed, landed)


def _sum_devices(g):
    n = g.shape[2]

    def body(g_ref, o_ref, done_ref):
        t = g_ref[0]
        for j in range(1, N_DEV):
            t = t + g_ref[j]
        o_ref[...] = t
        done_ref[...] = jnp.zeros_like(done_ref)

    return pl.pallas_call(body, name="sum_devices", out_shape=[_sds((1, n)), _sds((8, 128))],
                          compiler_params=_params())(g)


def _adamw_math(w, gv, m, v):
    c1 = 1.0 - ADAM_B1 ** ADAM_STEP
    c2 = 1.0 - ADAM_B2 ** ADAM_STEP
    mn = ADAM_B1 * m + (1.0 - ADAM_B1) * gv
    vn = ADAM_B2 * v + (1.0 - ADAM_B2) * (gv * gv)
    return -ADAM_LR * ((mn / c1) / (jnp.sqrt(vn / c2) + ADAM_EPS) + ADAM_WD * w), mn, vn


def _adamw(w, g, m, v):
    shp = w.shape
    shp3 = (1, 1, shp[0]) if len(shp) == 1 else (-1,) + tuple(shp[-2:])
    w3, g3, m3, v3 = (t.reshape(shp3) for t in (w, g, m, v))

    def body(w_ref, g_ref, m_ref, v_ref, d_ref, mo_ref, vo_ref):
        d_ref[...], mo_ref[...], vo_ref[...] = _adamw_math(w_ref[...], g_ref[...], m_ref[...], v_ref[...])

    nl, r, wd = w3.shape
    tr, tc = _tile2d(r, wd, BLOCK_BYTES // 2)
    blk = pl.BlockSpec((1, tr, tc), lambda l, i, j: (l, i, j))
    s3 = _sds(w3.shape)
    d, mn, vn = pl.pallas_call(
        body, name="adamw", grid=(nl, r // tr, wd // tc), in_specs=[blk] * 4, out_specs=[blk] * 3,
        out_shape=[s3, s3, s3], compiler_params=_params(("parallel", "parallel", "parallel")),
    )(w3, g3, m3, v3)
    return d.reshape(shp), mn.reshape(shp), vn.reshape(shp)


PIECE_BYTES = 1 << 20


def _place():
    return lax.axis_index("x"), lax.axis_index("y"), lax.axis_index("c")


def _pieces(shape, itemsize):
    if len(shape) >= 3:
        return [(i,) + p for i in range(shape[0]) for p in _pieces(shape[1:], itemsize)]
    rows = shape[0]
    row_bytes = itemsize
    for dsz in shape[1:]:
        row_bytes *= dsz
    k = 1
    while rows % (2 * k) == 0 and (rows // (2 * k)) % 16 == 0 and (rows // k) * row_bytes > PIECE_BYTES:
        k *= 2
    step = rows // k
    return [(pl.ds(j * step, step),) for j in range(k)]


def _split_start(make, src, dst, pieces):
    for p in pieces:
        make(src.at[p], dst.at[p]).start()
    return make(src, dst)


def _comm_call(body, name, arrs, out_shapes, n_remote, n_local):
    return pl.pallas_call(
        body, name=name, in_specs=[ANY] * len(arrs), out_specs=[ANY] * len(out_shapes), out_shape=out_shapes,
        scratch_shapes=[pltpu.SemaphoreType.DMA((n_remote,)), pltpu.SemaphoreType.DMA((n_remote,)),
                        pltpu.SemaphoreType.DMA((n_local,))],
    )(*arrs)


def all_gather8(arrs, name):
    n = len(arrs)
    pieces = [_pieces(a.shape, a.dtype.itemsize) for a in arrs]

    def body(*refs):
        ins, outs = refs[:n], refs[n:2 * n]
        send, recv, _ = refs[2 * n:]
        x, y, c = _place()
        me, sib = (x, y, c), (x, y, 1 - c)
        chips = [(1 - x, y), (x, 1 - y), (1 - x, 1 - y)]

        def slot(p):
            return 4 * p[0] + 2 * p[1] + p[2]

        def maker(t, k, to):
            def make(s, d):
                return pltpu.make_async_remote_copy(src_ref=s, dst_ref=d, send_sem=send.at[7 * t + k],
                                                    recv_sem=recv.at[7 * t + k], device_id=to, device_id_type=MESH)
            return make

        def landing(t, k, block):
            dst = outs[t].at[slot(block)]
            return maker(t, k, me)(dst, dst)

        sent = []
        for t in range(n):
            dst = outs[t].at[slot(me)]
            sent.append(_split_start(maker(t, 0, sib), ins[t], dst, pieces[t]))
            for j, chip in enumerate(chips):
                sent.append(_split_start(maker(t, 1 + j, (*chip, c)), ins[t], dst, pieces[t]))
        for j, chip in enumerate(chips):
            for t in range(n):
                landing(t, 1 + j, (*chip, c)).wait_recv()
                blk = outs[t].at[slot((*chip, c))]
                sent.append(_split_start(maker(t, 4 + j, sib), blk, blk, pieces[t]))
        for t in range(n):
            landing(t, 0, sib).wait_recv()
            for j, chip in enumerate(chips):
                landing(t, 4 + j, (*chip, 1 - c)).wait_recv()
        for cp in sent:
            cp.wait_send()

    outs = [_sds((N_DEV,) + a.shape, a.dtype) for a in arrs]
    got = _comm_call(body, name, arrs, outs, 7 * n, 1)
    x, y, c = _place()
    return [lax.dynamic_update_index_in_dim(g, a, 4 * x + 2 * y + c, 0) for g, a in zip(got, arrs)]


def sibling_send(arrs, name):
    n = len(arrs)
    pieces = [_pieces(a.shape[1:], a.dtype.itemsize) for a in arrs]

    def body(*refs):
        ins, theirs = refs[:n], refs[n:2 * n]
        send, recv, _ = refs[2 * n:]
        x, y, c = _place()
        rem = []
        for t in range(n):
            def make(s, d, t=t):
                return pltpu.make_async_remote_copy(src_ref=s, dst_ref=d, send_sem=send.at[t], recv_sem=recv.at[t],
                                                    device_id=(x, y, 1 - c), device_id_type=MESH)
            rem.append(_split_start(make, ins[t].at[1 - c], theirs[t], pieces[t]))
        for cp in rem:
            cp.wait_recv()
        for cp in rem:
            cp.wait_send()

    outs = [_sds(a.shape[1:], a.dtype) for a in arrs]
    return _comm_call(body, name, arrs, outs, n, 1)


def sibling_swap(arrs, name):
    n = len(arrs)
    pieces = [_pieces(a.shape, a.dtype.itemsize) for a in arrs]

    def body(*refs):
        ins, outs = refs[:n], refs[n:2 * n]
        send, recv, _ = refs[2 * n:]
        x, y, c = _place()
        rem = []
        for t in range(n):
            def make(s, d, t=t):
                return pltpu.make_async_remote_copy(src_ref=s, dst_ref=d, send_sem=send.at[t], recv_sem=recv.at[t],
                                                    device_id=(x, y, 1 - c), device_id_type=MESH)
            rem.append(_split_start(make, ins[t], outs[t], pieces[t]))
        for cp in rem:
            cp.wait_recv()
        for cp in rem:
            cp.wait_send()

    outs = [_sds(a.shape, a.dtype) for a in arrs]
    return _comm_call(body, name, arrs, outs, n, 1)


def _peer_copies(srcs, lands, send, recv, mode):
    x, y, c = _place()
    my_chip = 2 * x + y
    out = []
    for t in range(len(srcs)):
        for j, (px, py) in enumerate([(1 - x, y), (x, 1 - y), (1 - x, 1 - y)]):
            if mode == "gather":
                s, dst = srcs[t], lands[t].at[c, my_chip]
            else:
                s, dst = srcs[t].at[2 * px + py], lands[t].at[j]
            out.append(pltpu.make_async_remote_copy(
                src_ref=s, dst_ref=dst, send_sem=send.at[3 * t + j], recv_sem=recv.at[3 * t + j],
                device_id=(px, py, c), device_id_type=MESH))
    return out


HBM = pl.BlockSpec(memory_space=pltpu.HBM)
SEM = pl.BlockSpec(memory_space=pltpu.SEMAPHORE)
EFFECT = pltpu.SideEffectType.DATAFLOW_SIDE_EFFECTING


def ici_start(srcs, lands, mode, name):
    n = len(srcs)

    def body(*refs):
        send, recv = refs[2 * n], refs[2 * n + 1]
        for cp in _peer_copies(refs[:n], refs[n:2 * n], send, recv, mode):
            cp.start()
        refs[-1][...] = jnp.zeros_like(refs[-1])

    thru = [pltpu.HBM(a.shape, a.dtype) for a in list(srcs) + list(lands)]
    outs = pl.pallas_call(
        body, name=name, in_specs=[HBM] * (2 * n), out_specs=[SEM, SEM] + [HBM] * (2 * n) + [pl.BlockSpec(memory_space=pltpu.VMEM)],
        out_shape=[pltpu.SemaphoreType.DMA((3 * n,)), pltpu.SemaphoreType.DMA((3 * n,))] + thru + [_sds((8, 128))],
        input_output_aliases={i: 2 + i for i in range(2 * n)},
        compiler_params=pltpu.CompilerParams(has_side_effects=EFFECT),
    )(*[pltpu.with_memory_space_constraint(a, pltpu.HBM) for a in list(srcs) + list(lands)])
    return dict(send=outs[0], recv=outs[1], srcs=outs[2:2 + n], lands=outs[2 + n:2 + 2 * n], token=outs[-1])


def ici_wait(handle, after, mode, name):
    n = len(handle["srcs"])

    def body(*refs):
        send, recv = refs[2 * n], refs[2 * n + 1]
        for cp in _peer_copies(refs[:n], refs[n:2 * n], send, recv, mode):
            cp.wait_send()
            cp.wait_recv()

    arrs = list(handle["srcs"]) + list(handle["lands"])
    outs = pl.pallas_call(
        body, name=name, in_specs=[HBM] * (2 * n) + [SEM, SEM, ANY], out_specs=[HBM] * (2 * n),
        out_shape=[pltpu.HBM(a.shape, a.dtype) for a in arrs], input_output_aliases={i: i for i in range(2 * n)},
        compiler_params=pltpu.CompilerParams(has_side_effects=EFFECT),
    )(*arrs, handle["send"], handle["recv"], after)
    return outs[:n], outs[n:]


def gather_share(blocks, lands, name):
    n = len(blocks)

    def body(*refs):
        own, buf = refs[:n], refs[2 * n:3 * n]
        done, send, recv = refs[3 * n:]
        x, y, c = _place()
        my_chip = 2 * x + y
        chips = [2 * (1 - x) + y, 2 * x + (1 - y), 2 * (1 - x) + (1 - y)]
        sent = []
        for t in range(n):
            def make(s, d, k, t=t):
                return pltpu.make_async_remote_copy(src_ref=s, dst_ref=d, send_sem=send.at[4 * t + k],
                                                    recv_sem=recv.at[4 * t + k], device_id=(x, y, 1 - c),
                                                    device_id_type=MESH)
            cp = make(own[t], buf[t].at[c, my_chip], 0)
            cp.start()
            sent.append(cp)
            for k, pc in enumerate(chips):
                cp = make(buf[t].at[c, pc], buf[t].at[c, pc], 1 + k)
                cp.start()
                sent.append(cp)
        for t in range(n):
            for k in range(4):
                got = buf[t].at[1 - c, k]
                pltpu.make_async_remote_copy(src_ref=got, dst_ref=got, send_sem=send.at[4 * t + k],
                                             recv_sem=recv.at[4 * t + k], device_id=(x, y, 1 - c),
                                             device_id_type=MESH).wait_recv()
        for cp in sent:
            cp.wait_send()
        done[...] = jnp.zeros_like(done)

    outs = pl.pallas_call(
        body, name=name, in_specs=[ANY] * (2 * n), out_specs=[ANY] * n + [pl.BlockSpec(memory_space=pltpu.VMEM)],
        out_shape=[_sds(a.shape, a.dtype) for a in lands] + [_sds((8, 128))],
        input_output_aliases={n + t: t for t in range(n)},
        scratch_shapes=[pltpu.SemaphoreType.DMA((4 * n,)), pltpu.SemaphoreType.DMA((4 * n,))],
    )(*blocks, *lands)
    return outs[:n], outs[n]


@jax.custom_vjp
def _build_w_in(w4):
    full = w4.reshape(-1, w4.shape[-1])
    parts = []
    for _, start, width, wp in PROJ_SEGS:
        if width:
            parts.append(full[start:start + width])
        if wp > width:
            parts.append(jnp.zeros((wp - width, full.shape[1]), full.dtype))
    return jnp.concatenate(parts, axis=0)


def _build_w_in_f(w4):
    return _build_w_in(w4), None


def _build_w_in_b(_, g):
    parts, at = [], 0
    for _, _, width, wp in PROJ_SEGS:
        if width:
            parts.append(g[at:at + width])
        at += wp
    return (jnp.concatenate(parts, axis=0).reshape(N_CHIPS, -1, g.shape[1]),)


_build_w_in.defvjp(_build_w_in_f, _build_w_in_b)


def _split_w_uq(w):
    w3 = w.reshape(w.shape[0], MLA_HEADS, MLA_NOPE + MLA_ROPE)
    return w3[:, :, :MLA_NOPE].reshape(w.shape[0], -1), w3[:, :, MLA_NOPE:].reshape(w.shape[0], -1)


def _swap_halves(t, width):
    t3 = t.reshape(t.shape[0], -1, 2, width // 2)
    return jnp.concatenate([t3[:, :, 1:], t3[:, :, :1]], axis=2).reshape(t.shape)


def _pad_heads(t, width):
    t3 = t.reshape(t.shape[0], -1, width)
    t3 = jnp.pad(t3, ((0, 0), (0, 0), (0, HEAD_LANES - width)))
    return t3.reshape(t.shape[0], -1).astype(BF16)


def _layer(xh, mod, big, small, rope_q, rope_k):
    d = D_MODEL
    shift, scale, gate = mod[None, 0:d], mod[None, d:2 * d], mod[None, 2 * d:3 * d]
    w_al = _build_w_in(big["w_in"])
    proj = mod_mm(xh, small["norm_g"][None], scale, shift, w_al)
    gq, gk, gv, glr, mq, mkv, mkr, cb, cc, cx, _, z = split_proj(proj)

    rk = GLA_RANK
    hk = GLA_HEADS * GLA_DK
    wg = jnp.zeros((128, 2 * hk), F32)
    wg = wg.at[0:rk, 0:hk].set(small["gla_wg_f"]).at[rk:2 * rk, hk:].set(small["gla_wg_b"])
    bg = jnp.concatenate([small["gla_bg_f"], small["gla_bg_b"]])[None]
    la = gate_act(mm(glr, wg), bg)
    o_gla = rmsnorm(gla(gq, gk, gv, la), small["gla_norm_g"][None])

    cq = rmsnorm(mq, small["mla_q_norm_g"][None])
    w_nope, w_rope = _split_w_uq(jnp.concatenate([big["w_uq"][j] for j in range(N_CHIPS)], axis=1))
    qn = mm16(cq, w_nope)
    qr = mm(cq, w_rope)
    qr = fma(qr, rope_q[0], _swap_halves(qr, MLA_ROPE), rope_q[1])
    ckv = rmsnorm(mkv, small["mla_kv_norm_g"][None])
    kv = mm16(ckv, jnp.concatenate([big["w_ukv"][j] for j in range(N_CHIPS)], axis=1))
    kr = mkr[:, :MLA_ROPE]
    kr = fma(kr, rope_k[0], _swap_halves(kr, MLA_ROPE), rope_k[1])
    o_mla = rmsnorm(attn(qn, _pad_heads(qr, MLA_ROPE), kv, _pad_heads(kr, MLA_ROPE)), small["mla_out_g"][None])

    cw = jnp.concatenate([small["conv_w"], jnp.zeros((5, CONV_CH), F32)], axis=0)
    o_conv = rmsnorm(conv_op(cb, cc, cx, cw), small["conv_out_g"][None])

    o = jnp.concatenate([o_gla, o_mla, o_conv], axis=1)
    w_out = big["w_out"].reshape(d, d)
    return out_block(o, z, w_out, xh, gate)


SMALL_REPL = ("norm_g", "gla_bg_f", "gla_bg_b", "gla_norm_g", "mla_q_norm_g", "mla_kv_norm_g", "mla_out_g",
              "conv_out_g")
SMALL_SHARDED = ("gla_wg_f", "gla_wg_b", "conv_w")
BIG = ("w_in", "w_out", "w_uq", "w_ukv")
HALF_AXIS = (1, 0, 0, 0)


def kernel(x, c, positions, ada_w, ada_b, norm_g, w_in, gla_wg_f, gla_bg_f, gla_wg_b, gla_bg_b, gla_norm_g, mla_q_norm_g, mla_kv_norm_g, mla_w_uq, mla_w_ukv, mla_out_g, conv_w, conv_out_g, w_out, final_g, loss_target, m_ada_w, m_ada_b, m_norm_g, m_w_in, m_gla_wg_f, m_gla_bg_f, m_gla_wg_b, m_gla_bg_b, m_gla_norm_g, m_mla_q_norm_g, m_mla_kv_norm_g, m_mla_w_uq, m_mla_w_ukv, m_mla_out_g, m_conv_w, m_conv_out_g, m_w_out, m_final_g, v_ada_w, v_ada_b, v_norm_g, v_w_in, v_gla_wg_f, v_gla_bg_f, v_gla_wg_b, v_gla_bg_b, v_gla_norm_g, v_mla_q_norm_g, v_mla_kv_norm_g, v_mla_w_uq, v_mla_w_ukv, v_mla_out_g, v_conv_w, v_conv_out_g, v_w_out, v_final_g):
    xi, yi, ci = _place()
    chip = 2 * xi + yi
    dev = 2 * chip + ci
    s = x.shape[1]
    d = D_MODEL
    weights = dict(ada_w=ada_w, ada_b=ada_b, norm_g=norm_g, w_in=w_in, gla_wg_f=gla_wg_f, gla_bg_f=gla_bg_f,
                   gla_wg_b=gla_wg_b, gla_bg_b=gla_bg_b, gla_norm_g=gla_norm_g, mla_q_norm_g=mla_q_norm_g,
                   mla_kv_norm_g=mla_kv_norm_g, mla_w_uq=mla_w_uq, mla_w_ukv=mla_w_ukv, mla_out_g=mla_out_g,
                   conv_w=conv_w, conv_out_g=conv_out_g, w_out=w_out, final_g=final_g)
    m_in = dict(ada_w=m_ada_w, ada_b=m_ada_b, norm_g=m_norm_g, w_in=m_w_in, gla_wg_f=m_gla_wg_f, gla_bg_f=m_gla_bg_f,
                gla_wg_b=m_gla_wg_b, gla_bg_b=m_gla_bg_b, gla_norm_g=m_gla_norm_g, mla_q_norm_g=m_mla_q_norm_g,
                mla_kv_norm_g=m_mla_kv_norm_g, mla_w_uq=m_mla_w_uq, mla_w_ukv=m_mla_w_ukv, mla_out_g=m_mla_out_g,
                conv_w=m_conv_w, conv_out_g=m_conv_out_g, w_out=m_w_out, final_g=m_final_g)
    v_in = dict(ada_w=v_ada_w, ada_b=v_ada_b, norm_g=v_norm_g, w_in=v_w_in, gla_wg_f=v_gla_wg_f, gla_bg_f=v_gla_bg_f,
                gla_wg_b=v_gla_wg_b, gla_bg_b=v_gla_bg_b, gla_norm_g=v_gla_norm_g, mla_q_norm_g=v_mla_q_norm_g,
                mla_kv_norm_g=v_mla_kv_norm_g, mla_w_uq=v_mla_w_uq, mla_w_ukv=v_mla_w_ukv, mla_out_g=v_mla_out_g,
                conv_w=v_conv_w, conv_out_g=v_conv_out_g, w_out=v_w_out, final_g=v_final_g)

    g_c, g_wgf, g_wgb, g_cw = all_gather8([c, gla_wg_f, gla_wg_b, conv_w], "gather_small")

    def unshard_cols(g):
        g4 = g[0::2]
        return g4.transpose(1, 2, 0, 3).reshape(g4.shape[1], g4.shape[2], -1)

    small_full = dict(gla_wg_f=unshard_cols(g_wgf), gla_wg_b=unshard_cols(g_wgb), conv_w=unshard_cols(g_cw))
    for nme in SMALL_REPL:
        small_full[nme] = weights[nme]
    smalls = [{nme: small_full[nme][l] for nme in SMALL_REPL + SMALL_SHARDED} for l in range(DEPTH)]

    big_src = (jnp.swapaxes(w_in, 1, 2), w_out, mla_w_uq, mla_w_ukv)

    def my_halves(l, zero=0):
        out = []
        for t, a in enumerate(big_src):
            n_half = a.shape[1 + HALF_AXIS[t]] // 2
            out.append(lax.dynamic_slice_in_dim(a[l], ci * n_half + zero, n_half, axis=HALF_AXIS[t]).astype(BF16))
        return out

    def landing(blocks):
        return [lax.empty((2, N_CHIPS) + b.shape, b.dtype) for b in blocks]

    def finish_gather(handle, after, tag):
        blocks, lands = ici_wait(handle, after, "gather", "gather_wait" + tag)
        lands, done = gather_share(blocks, lands, "gather_share" + tag)
        full = [lax.dynamic_update_slice(g, b[None, None], (ci, chip) + (0,) * b.ndim) for g, b in zip(lands, blocks)]
        return full, done

    halves0 = my_halves(0)
    started0 = ici_start(halves0, landing(halves0), "gather", "gather_start0")

    c_act = _silu_rows(g_c[:, 0, :])
    c_act16 = jnp.concatenate([c_act, jnp.zeros_like(c_act)], axis=0)
    n_ada = ada_w.shape[2]
    parts = []
    for l in range(DEPTH):
        bias = lax.dynamic_slice_in_dim(ada_b[l], chip * n_ada, n_ada)[None]
        parts.append(_mm(c_act16, ada_w[l], bias=bias, name="ada_fwd"))
    g_mod, = all_gather8([jnp.stack(parts)], "gather_mod")
    mod_mine = lax.dynamic_index_in_dim(g_mod[0::2], dev, 2, keepdims=False)
    mods = mod_mine.transpose(1, 0, 2).reshape(DEPTH, 3 * d)

    inv_freq = ROPE_THETA ** (-jnp.arange(0, MLA_ROPE, 2, dtype=F32) / MLA_ROPE)
    ang = positions[0].astype(F32)[:, None] * inv_freq
    cos, sin = jnp.cos(ang), jnp.sin(ang)
    rope_k = (jnp.concatenate([cos, cos], axis=1), jnp.concatenate([-sin, sin], axis=1))
    rope_q = (jnp.tile(rope_k[0], (1, MLA_HEADS)), jnp.tile(rope_k[1], (1, MLA_HEADS)))

    def run_layer(xh, mod, gathered, small):
        big = {nme: jnp.concatenate([g[0], g[1]], axis=HALF_AXIS[t] + 1) for t, (nme, g) in enumerate(zip(BIG, gathered))}
        return _layer(xh, mod, big, small, rope_q, rope_k)

    def head(hh, fg):
        return loss_op(rmsnorm(hh, fg[None]), loss_target[0])[0, 0]

    gathered0, done0 = finish_gather(started0, mods, "0")
    halves1 = my_halves(1, done0[0, 0].astype(jnp.int32))
    started1 = ici_start(halves1, landing(halves1), "gather", "gather_start1")
    h1, vjp0 = jax.vjp(run_layer, x[0], mods[0] + started1["token"][0, 0], gathered0, smalls[0])
    gathered1, _ = finish_gather(started1, h1, "1")
    h2, vjp1 = jax.vjp(run_layer, h1, mods[1], gathered1, smalls[1])
    loss_dev, vjp_head = jax.vjp(head, h2, final_g)
    dh2, dfinal = vjp_head(jnp.ones((), F32))

    c_idx = jnp.reshape(ci, (1,)).astype(jnp.int32)
    chip_idx = jnp.reshape(chip, (1,)).astype(jnp.int32)

    def reduce_begin(dgath, tag, zero=None):
        theirs = sibling_send(dgath, "reduce_sibling" + tag)
        pair = [_add_pair(a, b, c_idx) for a, b in zip(dgath, theirs)]
        shapes = [(N_CHIPS - 1,) + p.shape[1:] for p in pair]
        if zero is None:
            lands = [lax.empty(shp, BF16) for shp in shapes]
        else:
            lands = [jnp.broadcast_to(zero.astype(BF16), shp) for shp in shapes]
        return ici_start(pair, lands, "reduce", "reduce_start" + tag)

    def reduce_end(handle, after, tag):
        pair, landed = ici_wait(handle, after, "reduce", "reduce_wait" + tag)
        reduced = [_add_chips(p, q, chip_idx) for p, q in zip(pair, landed)]
        others = sibling_swap(reduced, "share_sibling" + tag)
        return [jnp.where(ci == 0, jnp.concatenate([own, other], axis=HALF_AXIS[t]),
                          jnp.concatenate([other, own], axis=HALF_AXIS[t]))
                for t, (own, other) in enumerate(zip(reduced, others))]

    dh1, dmod1, dgath1, dsmall1 = vjp1(dh2)
    reducing1 = reduce_begin(dgath1, "1")
    dx, dmod0, dgath0, dsmall0 = vjp0(dh1 + reducing1["token"][0, 0])
    dmods = jnp.stack([dmod0, dmod1])
    dsmalls = [dsmall0, dsmall1]

    pieces = [dmods.reshape(-1), dfinal]
    for nme in SMALL_REPL + SMALL_SHARDED:
        pieces.append(jnp.stack([dsmalls[l][nme] for l in range(DEPTH)]).reshape(-1))
    pieces.append(loss_dev.reshape(1))
    sizes = [p.shape[0] for p in pieces]
    flat = jnp.concatenate(pieces)
    padn = (-flat.shape[0]) % 128
    flat = jnp.concatenate([flat, jnp.zeros((padn,), F32)])[None]
    g_small, = all_gather8([flat], "gather_small_grads")
    total, small_done = _sum_devices(g_small)
    total = total[0]
    reducing0 = reduce_begin(dgath0, "0", small_done[0, 0])
    offs, at = [], 0
    for n_el in sizes:
        offs.append(at)
        at += n_el

    def piece(i, shape):
        return total[offs[i]:offs[i] + sizes[i]].reshape(shape)

    grads = {"ada_b": piece(0, (DEPTH, 3 * d)), "final_g": piece(1, (d,))}
    loss = piece(len(pieces) - 1, ())
    for i, nme in enumerate(SMALL_REPL + SMALL_SHARDED):
        full = piece(2 + i, small_full[nme].shape)
        if nme in SMALL_SHARDED:
            ncol = weights[nme].shape[2]
            full = lax.dynamic_slice_in_dim(full, chip * ncol, ncol, axis=2)
        grads[nme] = full

    dmod_all = g_small[:, 0, :DEPTH * 3 * d].reshape(N_DEV, DEPTH, 3 * d)
    dmod_cols = lax.dynamic_slice_in_dim(dmod_all, chip * n_ada, n_ada, axis=2)
    g_ada = []
    for l in range(DEPTH):
        dm16 = jnp.concatenate([dmod_cols[:, l], jnp.zeros((N_DEV, n_ada), F32)], axis=0)
        dm16 = dm16 + reducing0["token"][0, 0]
        g_ada.append(_mm(c_act16, dm16, ta=True, name="ada_bwd"))
    grads["ada_w"] = jnp.stack(g_ada)

    order = list(weights)
    big_names = ("w_in", "w_out", "mla_w_uq", "mla_w_ukv")
    delta, new_m, new_v = {}, {}, {}
    for nme in order:
        if nme not in big_names:
            delta[nme], new_m[nme], new_v[nme] = _adamw(weights[nme], grads[nme], m_in[nme], v_in[nme])

    def first(t):
        return t[(slice(0, 1),) * t.ndim].reshape(1)

    big_grads1 = reduce_end(reducing1, jnp.concatenate([first(dx), first(reducing0["token"])]), "1")
    done = [first(delta[nme]) for nme in order if nme not in big_names] + [first(g) for g in big_grads1]
    big_grads0 = reduce_end(reducing0, jnp.concatenate(done), "0")
    for nme, g0, g1 in zip(big_names, big_grads0, big_grads1):
        grads[nme] = jnp.stack([g0, g1])
    for nme in big_names:
        if nme == "w_in":
            w_t, m_t, v_t = (jnp.swapaxes(t, 1, 2) for t in (w_in, m_w_in, v_w_in))
            res = _adamw(w_t, grads[nme], m_t, v_t)
            delta[nme], new_m[nme], new_v[nme] = (jnp.swapaxes(t, 1, 2) for t in res)
            grads[nme] = jnp.swapaxes(grads[nme], 1, 2)
            continue
        delta[nme], new_m[nme], new_v[nme] = _adamw(weights[nme], grads[nme], m_in[nme], v_in[nme])
    return (loss, dx[None], *[grads[n_] for n_ in order], *[delta[n_] for n_ in order],
            *[new_m[n_] for n_ in order], *[new_v[n_] for n_ in order])
```
